```python
import math
import jax, jax.numpy as jnp
from jax import lax
import numpy as np

D_MODEL = 2048
BATCH = 8
SEQ = 2048
DEPTH = 1

CTX_LEN = 256
GRID_W = 64
EPS = 1e-6
S5_WIDTH = D_MODEL // 2
S5_GROUP = 16
S5_GROUPS = S5_WIDTH // S5_GROUP
S5_STATE = 64
MLA_HEADS = 8
QK_NOPE = 128
QK_ROPE = 64
V_DIM = 128
Q_RANK = 512
KV_RANK = 256
ROPE_BASE = 10000.0
Q_BLOCK = 128
ATTN_SCALE = (QK_NOPE + QK_ROPE) ** -0.5
N_BRANCH = 2
D_FF = -(-8 * D_MODEL // (3 * 256)) * 256
IN_COLS = S5_WIDTH + Q_RANK + KV_RANK + QK_ROPE + N_BRANCH * D_MODEL

kernel_name = 'hybrid_s5_mla_dit_block'


def rmsnorm(x, g):
    xf = x.astype(jnp.float32)
    y = xf * lax.rsqrt(jnp.mean(xf * xf, axis=-1, keepdims=True) + EPS)
    return (y * g.astype(jnp.float32)).astype(x.dtype)


def ada(cvec, w_mod, b_mod):
    m = jax.nn.silu(cvec) @ w_mod + b_mod
    return m.reshape(m.shape[:-1] + (6, D_MODEL))


def rope2d_tables(n_tokens):
    rows = n_tokens // GRID_W
    row = jnp.repeat(jnp.arange(rows, dtype=jnp.float32), GRID_W)
    col = jnp.tile(jnp.arange(GRID_W, dtype=jnp.float32), rows)
    n_freq = QK_ROPE // 4
    inv = ROPE_BASE ** (-jnp.arange(n_freq, dtype=jnp.float32) / n_freq)
    ang = jnp.stack([row[:, None] * inv, col[:, None] * inv], axis=1)
    return jnp.cos(ang), jnp.sin(ang)


def apply_rope2d(x, cos, sin):
    xs = x.reshape(x.shape[:-1] + (2, 2, QK_ROPE // 4))
    x1, x2 = xs[..., 0, :], xs[..., 1, :]
    c = cos[None, :, None].astype(x.dtype)
    s = sin[None, :, None].astype(x.dtype)
    out = jnp.stack([x1 * c - x2 * s, x2 * c + x1 * s], axis=-2)
    return out.reshape(x.shape)


def split_in(h):
    o = S5_WIDTH
    u = h[..., :o]
    cq = h[..., o:o + Q_RANK]
    o += Q_RANK
    ckv = h[..., o:o + KV_RANK]
    o += KV_RANK
    kr = h[..., o:o + QK_ROPE]
    o += QK_ROPE
    return u, cq, ckv, kr, h[..., o:]


def s5_discretize(a_re, a_im, log_dt, b_re, b_im):
    f32 = jnp.float32
    dt = jnp.exp(log_dt.astype(f32))[:, None]
    lr, li = a_re.astype(f32), a_im.astype(f32)
    mag = jnp.exp(lr * dt)
    ab_re, ab_im = mag * jnp.cos(li * dt), mag * jnp.sin(li * dt)
    den = lr * lr + li * li
    nr, ni = ab_re - 1.0, ab_im
    co_re = (nr * lr + ni * li) / den
    co_im = (ni * lr - nr * li) / den
    br, bi = b_re.astype(f32), b_im.astype(f32)
    bb_re = co_re[..., None] * br - co_im[..., None] * bi
    bb_im = co_re[..., None] * bi + co_im[..., None] * br
    return ab_re, ab_im, bb_re, bb_im


def _ssm_combine(e1, e2):
    a1r, a1i, b1r, b1i = e1
    a2r, a2i, b2r, b2i = e2
    return (a2r * a1r - a2i * a1i, a2r * a1i + a2i * a1r,
            a2r * b1r - a2i * b1i + b2r, a2r * b1i + a2i * b1r + b2i)


def s5_scan(u, disc, h0, reverse):
    ab_re, ab_im, bb_re, bb_im = disc
    bu_re = jnp.einsum('blgp,gnp->blgn', u, bb_re)
    bu_im = jnp.einsum('blgp,gnp->blgn', u, bb_im)
    if h0 is not None:
        idx = -1 if reverse else 0
        h_re, h_im = h0
        bu_re = bu_re.at[:, idx].add(ab_re * h_re - ab_im * h_im)
        bu_im = bu_im.at[:, idx].add(ab_re * h_im + ab_im * h_re)
    a_re = jnp.broadcast_to(ab_re, bu_re.shape)
    a_im = jnp.broadcast_to(ab_im, bu_re.shape)
    _, _, h_re, h_im = lax.associative_scan(_ssm_combine, (a_re, a_im, bu_re, bu_im),
                                            reverse=reverse, axis=1)
    return h_re, h_im


def s5_readout(h, c_re, c_im):
    h_re, h_im = h
    return (jnp.einsum('blgn,gpn->blgp', h_re, c_re)
            - jnp.einsum('blgn,gpn->blgp', h_im, c_im))


def s5_mixer(u_ctx, u_lat, p, need_ctx_out):
    f32 = jnp.float32
    B, L = u_lat.shape[:2]
    Lc = u_ctx.shape[1]
    uc = u_ctx.astype(f32).reshape(B, Lc, S5_GROUPS, S5_GROUP)
    ul = u_lat.astype(f32).reshape(B, L, S5_GROUPS, S5_GROUP)
    d_skip = p['s5_d'].astype(f32)
    y_lat = d_skip * ul
    y_ctx = d_skip * uc if need_ctx_out else None
    for d, rev in enumerate((False, True)):
        disc = s5_discretize(p['s5_a_re'][d], p['s5_a_im'][d], p['s5_log_dt'][d],
                             p['s5_b_re'][d], p['s5_b_im'][d])
        c_re, c_im = p['s5_c_re'][d].astype(f32), p['s5_c_im'][d].astype(f32)
        hc = s5_scan(uc, disc, None, rev)
        last = 0 if rev else -1
        hl = s5_scan(ul, disc, (hc[0][:, last], hc[1][:, last]), rev)
        y_lat = y_lat + s5_readout(hl, c_re, c_im)
        if need_ctx_out:
            y_ctx = y_ctx + s5_readout(hc, c_re, c_im)
    y_lat = y_lat.reshape(B, L, S5_WIDTH).astype(u_lat.dtype)
    if need_ctx_out:
        y_ctx = y_ctx.reshape(B, Lc, S5_WIDTH).astype(u_ctx.dtype)
    return y_lat, y_ctx


def mla_qkv(cq, ckv, kr, p, rope):
    B, L = cq.shape[:2]
    q = (rmsnorm(cq, p['q_norm']) @ p['w_uq']).reshape(B, L, MLA_HEADS, QK_NOPE + QK_ROPE)
    kv = (rmsnorm(ckv, p['kv_norm']) @ p['w_ukv']).reshape(B, L, MLA_HEADS, QK_NOPE + V_DIM)
    q_nope, q_rope = q[..., :QK_NOPE], q[..., QK_NOPE:]
    k_nope, v = kv[..., :QK_NOPE], kv[..., QK_NOPE:]
    k_rope = kr[:, :, None, :]
    if rope is not None:
        cos, sin = rope
        q_rope = apply_rope2d(q_rope, cos, sin)
        k_rope = apply_rope2d(k_rope, cos, sin)
    q = jnp.concatenate([q_nope, q_rope], axis=-1)
    k = jnp.concatenate([k_nope, jnp.broadcast_to(k_rope, (B, L, MLA_HEADS, QK_ROPE))], axis=-1)
    return q, k, v


def attend(q, k, v):
    s = jnp.einsum('bqhd,bkhd->bhqk', q, k, preferred_element_type=jnp.float32) * ATTN_SCALE
    pr = jax.nn.softmax(s, axis=-1).astype(v.dtype)
    return jnp.einsum('bhqk,bkhd->bqhd', pr, v)


def blocked_attend(q, k, v):
    B, L, H, dk = q.shape
    nb = L // Q_BLOCK
    qb = q.reshape(B, nb, Q_BLOCK, H, dk).transpose(1, 0, 2, 3, 4)
    ob = lax.map(lambda qi: attend(qi, k, v), qb)
    return ob.transpose(1, 0, 2, 3, 4).reshape(B, L, H, v.shape[-1])


def merge_branches(y5, o_mla, gate_cols, p):
    z = jax.nn.gelu(y5)
    a, b = jnp.split(z @ p['w_glu'], 2, axis=-1)
    br_s5 = a * jax.nn.sigmoid(b)
    br_mla = o_mla.reshape(o_mla.shape[:2] + (MLA_HEADS * V_DIM,)) @ p['w_mla_o']
    g_s5, g_mla = jnp.split(jax.nn.sigmoid(gate_cols), 2, axis=-1)
    return (g_s5 * br_s5 + g_mla * br_mla) @ p['w_out']


def swiglu(h, p):
    a, b = jnp.split(h @ p['w_ffn_in'], 2, axis=-1)
    return (jax.nn.silu(a) * b) @ p['w_ffn_out']


def layer(x, xc, m_lat, m_ctx, cos, sin, p, need_ctx_out):
    sh1, sc1, g1, sh2, sc2, g2 = (m_lat[..., i, :] for i in range(6))
    csh1, csc1, cg1, csh2, csc2, cg2 = (m_ctx[..., i, :] for i in range(6))
    hl = (rmsnorm(x, p['norm1']) * (1.0 + sc1) + sh1) @ p['w_in']
    hc = (rmsnorm(xc, p['norm1']) * (1.0 + csc1) + csh1) @ p['w_in']
    ul, cql, ckvl, krl, gl = split_in(hl)
    uc, cqc, ckvc, krc, gc = split_in(hc)
    y5_lat, y5_ctx = s5_mixer(uc, ul, p, need_ctx_out)
    qc, kc, vc = mla_qkv(cqc, ckvc, krc, p, None)
    ql, kl, vl = mla_qkv(cql, ckvl, krl, p, (cos, sin))
    k_all = jnp.concatenate([kl, kc], axis=1)
    v_all = jnp.concatenate([vl, vc], axis=1)
    ol = blocked_attend(ql, k_all, v_all)
    x = x + g1 * merge_branches(y5_lat, ol, gl, p)
    x = x + g2 * swiglu(rmsnorm(x, p['norm2']) * (1.0 + sc2) + sh2, p)
    if need_ctx_out:
        oc = attend(qc, kc, vc)
        xc = xc + cg1 * merge_branches(y5_ctx, oc, gc, p)
        xc = xc + cg2 * swiglu(rmsnorm(xc, p['norm2']) * (1.0 + csc2) + csh2, p)
    return x, xc


def _fwd_setup_inputs(seed: int = 0) -> dict:
    key = jax.random.key(seed)
    ks = jax.random.split(key, 32)
    f32 = jnp.float32

    def nrm(k, shape, scale):
        return jax.random.normal(k, shape, f32) * scale

    G, N, P = S5_GROUPS, S5_STATE, S5_GROUP
    n_idx = jnp.arange(N, dtype=f32)
    return {
        'x': nrm(ks[0], (BATCH, SEQ, D_MODEL), 1.0),
        'c': nrm(ks[1], (BATCH, D_MODEL), 1.0),
        'ctx': nrm(ks[2], (BATCH, CTX_LEN, D_MODEL), 1.0),
        'c_ctx': nrm(ks[3], (D_MODEL,), 1.0),
        'w_mod': nrm(ks[4], (DEPTH, D_MODEL, 6 * D_MODEL), 0.3 * D_MODEL ** -0.5),
        'b_mod': nrm(ks[5], (DEPTH, 6 * D_MODEL), 0.02),
        'norm1': 1.0 + nrm(ks[6], (DEPTH, D_MODEL), 0.01),
        'norm2': 1.0 + nrm(ks[7], (DEPTH, D_MODEL), 0.01),
        'w_in': nrm(ks[8], (DEPTH, D_MODEL, IN_COLS), D_MODEL ** -0.5),
        's5_a_re': -0.5 + nrm(ks[9], (DEPTH, 2, G, N), 0.01),
        's5_a_im': math.pi * n_idx + nrm(ks[10], (DEPTH, 2, G, N), 0.01),
        's5_log_dt': jax.random.uniform(ks[11], (DEPTH, 2, G), f32, math.log(1e-3), math.log(1e-1)),
        's5_b_re': nrm(ks[12], (DEPTH, 2, G, N, P), (2 * P) ** -0.5),
        's5_b_im': nrm(ks[13], (DEPTH, 2, G, N, P), (2 * P) ** -0.5),
        's5_c_re': nrm(ks[14], (DEPTH, 2, G, P, N), N ** -0.5),
        's5_c_im': nrm(ks[15], (DEPTH, 2, G, P, N), N ** -0.5),
        's5_d': nrm(ks[16], (DEPTH, G, P), 0.5),
        'w_glu': nrm(ks[17], (DEPTH, S5_WIDTH, 2 * D_MODEL), S5_WIDTH ** -0.5),
        'q_norm': 1.0 + nrm(ks[18], (DEPTH, Q_RANK), 0.01),
        'kv_norm': 1.0 + nrm(ks[19], (DEPTH, KV_RANK), 0.01),
        'w_uq': nrm(ks[20], (DEPTH, Q_RANK, MLA_HEADS * (QK_NOPE + QK_ROPE)), Q_RANK ** -0.5),
        'w_ukv': nrm(ks[21], (DEPTH, KV_RANK, MLA_HEADS * (QK_NOPE + V_DIM)), KV_RANK ** -0.5),
        'w_mla_o': nrm(ks[22], (DEPTH, MLA_HEADS * V_DIM, D_MODEL), (MLA_HEADS * V_DIM) ** -0.5),
        'w_out': nrm(ks[23], (DEPTH, D_MODEL, D_MODEL), D_MODEL ** -0.5),
        'w_ffn_in': nrm(ks[24], (DEPTH, D_MODEL, 2 * D_FF), D_MODEL ** -0.5),
        'w_ffn_out': nrm(ks[25], (DEPTH, D_FF, D_MODEL), D_FF ** -0.5),
        'norm_f': 1.0 + nrm(ks[26], (D_MODEL,), 0.01),
    }


def _fwd_reference(x, c, ctx, c_ctx, w_mod, b_mod, norm1, norm2, w_in, s5_a_re, s5_a_im, s5_log_dt,
              s5_b_re, s5_b_im, s5_c_re, s5_c_im, s5_d, w_glu, q_norm, kv_norm, w_uq, w_ukv,
              w_mla_o, w_out, w_ffn_in, w_ffn_out, norm_f):
    cos, sin = rope2d_tables(x.shape[1])
    xc = ctx
    for l in range(DEPTH):
        p = {
            'norm1': norm1[l], 'norm2': norm2[l], 'w_in': w_in[l],
            's5_a_re': s5_a_re[l], 's5_a_im': s5_a_im[l], 's5_log_dt': s5_log_dt[l],
            's5_b_re': s5_b_re[l], 's5_b_im': s5_b_im[l], 's5_c_re': s5_c_re[l], 's5_c_im': s5_c_im[l],
            's5_d': s5_d[l], 'w_glu': w_glu[l], 'q_norm': q_norm[l], 'kv_norm': kv_norm[l],
            'w_uq': w_uq[l], 'w_ukv': w_ukv[l], 'w_mla_o': w_mla_o[l], 'w_out': w_out[l],
            'w_ffn_in': w_ffn_in[l], 'w_ffn_out': w_ffn_out[l],
        }
        m_lat = ada(c, w_mod[l], b_mod[l])[:, None]
        m_ctx = ada(c_ctx, w_mod[l], b_mod[l])
        x, xc = layer(x, xc, m_lat, m_ctx, cos, sin, p, l < DEPTH - 1)
    return rmsnorm(x, norm_f)


import jax as _jax
import jax.numpy as _jnp

TWIN_FORMAT = 'train_step'
FWD_PARAMS = ['x', 'c', 'ctx', 'c_ctx', 'w_mod', 'b_mod', 'norm1', 'norm2', 'w_in', 's5_a_re', 's5_a_im', 's5_log_dt', 's5_b_re', 's5_b_im', 's5_c_re', 's5_c_im', 's5_d', 'w_glu', 'q_norm', 'kv_norm', 'w_uq', 'w_ukv', 'w_mla_o', 'w_out', 'w_ffn_in', 'w_ffn_out', 'norm_f']
TWIN_WEIGHTS = ['c_ctx', 'w_mod', 'b_mod', 'norm1', 'norm2', 'w_in', 's5_a_re', 's5_a_im', 's5_log_dt', 's5_b_re', 's5_b_im', 's5_c_re', 's5_c_im', 's5_d', 'w_glu', 'q_norm', 'kv_norm', 'w_uq', 'w_ukv', 'w_mla_o', 'w_out', 'w_ffn_in', 'w_ffn_out', 'norm_f']
TWIN_DIFF_INPUT = 'x'
TWIN_INPUTS = ['x', 'c', 'ctx', 'c_ctx', 'w_mod', 'b_mod', 'norm1', 'norm2', 'w_in', 's5_a_re', 's5_a_im', 's5_log_dt', 's5_b_re', 's5_b_im', 's5_c_re', 's5_c_im', 's5_d', 'w_glu', 'q_norm', 'kv_norm', 'w_uq', 'w_ukv', 'w_mla_o', 'w_out', 'w_ffn_in', 'w_ffn_out', 'norm_f', 'loss_target', 'm_c_ctx', 'm_w_mod', 'm_b_mod', 'm_norm1', 'm_norm2', 'm_w_in', 'm_s5_a_re', 'm_s5_a_im', 'm_s5_log_dt', 'm_s5_b_re', 'm_s5_b_im', 'm_s5_c_re', 'm_s5_c_im', 'm_s5_d', 'm_w_glu', 'm_q_norm', 'm_kv_norm', 'm_w_uq', 'm_w_ukv', 'm_w_mla_o', 'm_w_out', 'm_w_ffn_in', 'm_w_ffn_out', 'm_norm_f', 'v_c_ctx', 'v_w_mod', 'v_b_mod', 'v_norm1', 'v_norm2', 'v_w_in', 'v_s5_a_re', 'v_s5_a_im', 'v_s5_log_dt', 'v_s5_b_re', 'v_s5_b_im', 'v_s5_c_re', 'v_s5_c_im', 'v_s5_d', 'v_w_glu', 'v_q_norm', 'v_kv_norm', 'v_w_uq', 'v_w_ukv', 'v_w_mla_o', 'v_w_out', 'v_w_ffn_in', 'v_w_ffn_out', 'v_norm_f']
TWIN_OUTPUTS = ['loss', 'grad_x', 'grad_c_ctx', 'grad_w_mod', 'grad_b_mod', 'grad_norm1', 'grad_norm2', 'grad_w_in', 'grad_s5_a_re', 'grad_s5_a_im', 'grad_s5_log_dt', 'grad_s5_b_re', 'grad_s5_b_im', 'grad_s5_c_re', 'grad_s5_c_im', 'grad_s5_d', 'grad_w_glu', 'grad_q_norm', 'grad_kv_norm', 'grad_w_uq', 'grad_w_ukv', 'grad_w_mla_o', 'grad_w_out', 'grad_w_ffn_in', 'grad_w_ffn_out', 'grad_norm_f', 'delta_c_ctx', 'delta_w_mod', 'delta_b_mod', 'delta_norm1', 'delta_norm2', 'delta_w_in', 'delta_s5_a_re', 'delta_s5_a_im', 'delta_s5_log_dt', 'delta_s5_b_re', 'delta_s5_b_im', 'delta_s5_c_re', 'delta_s5_c_im', 'delta_s5_d', 'delta_w_glu', 'delta_q_norm', 'delta_kv_norm', 'delta_w_uq', 'delta_w_ukv', 'delta_w_mla_o', 'delta_w_out', 'delta_w_ffn_in', 'delta_w_ffn_out', 'delta_norm_f', 'new_m_c_ctx', 'new_m_w_mod', 'new_m_b_mod', 'new_m_norm1', 'new_m_norm2', 'new_m_w_in', 'new_m_s5_a_re', 'new_m_s5_a_im', 'new_m_s5_log_dt', 'new_m_s5_b_re', 'new_m_s5_b_im', 'new_m_s5_c_re', 'new_m_s5_c_im', 'new_m_s5_d', 'new_m_w_glu', 'new_m_q_norm', 'new_m_kv_norm', 'new_m_w_uq', 'new_m_w_ukv', 'new_m_w_mla_o', 'new_m_w_out', 'new_m_w_ffn_in', 'new_m_w_ffn_out', 'new_m_norm_f', 'new_v_c_ctx', 'new_v_w_mod', 'new_v_b_mod', 'new_v_norm1', 'new_v_norm2', 'new_v_w_in', 'new_v_s5_a_re', 'new_v_s5_a_im', 'new_v_s5_log_dt', 'new_v_s5_b_re', 'new_v_s5_b_im', 'new_v_s5_c_re', 'new_v_s5_c_im', 'new_v_s5_d', 'new_v_w_glu', 'new_v_q_norm', 'new_v_kv_norm', 'new_v_w_uq', 'new_v_w_ukv', 'new_v_w_mla_o', 'new_v_w_out', 'new_v_w_ffn_in', 'new_v_w_ffn_out', 'new_v_norm_f']
TWIN_LEAF_KINDS = {'loss': 'loss', 'grad_x': 'grad_x', 'grad_c_ctx': 'grad_w', 'grad_w_mod': 'grad_w', 'grad_b_mod': 'grad_w', 'grad_norm1': 'grad_w', 'grad_norm2': 'grad_w', 'grad_w_in': 'grad_w', 'grad_s5_a_re': 'grad_w', 'grad_s5_a_im': 'grad_w', 'grad_s5_log_dt': 'grad_w', 'grad_s5_b_re': 'grad_w', 'grad_s5_b_im': 'grad_w', 'grad_s5_c_re': 'grad_w', 'grad_s5_c_im': 'grad_w', 'grad_s5_d': 'grad_w', 'grad_w_glu': 'grad_w', 'grad_q_norm': 'grad_w', 'grad_kv_norm': 'grad_w', 'grad_w_uq': 'grad_w', 'grad_w_ukv': 'grad_w', 'grad_w_mla_o': 'grad_w', 'grad_w_out': 'grad_w', 'grad_w_ffn_in': 'grad_w', 'grad_w_ffn_out': 'grad_w', 'grad_norm_f': 'grad_w', 'delta_c_ctx': 'delta_w', 'delta_w_mod': 'delta_w', 'delta_b_mod': 'delta_w', 'delta_norm1': 'delta_w', 'delta_norm2': 'delta_w', 'delta_w_in': 'delta_w', 'delta_s5_a_re': 'delta_w', 'delta_s5_a_im': 'delta_w', 'delta_s5_log_dt': 'delta_w', 'delta_s5_b_re': 'delta_w', 'delta_s5_b_im': 'delta_w', 'delta_s5_c_re': 'delta_w', 'delta_s5_c_im': 'delta_w', 'delta_s5_d': 'delta_w', 'delta_w_glu': 'delta_w', 'delta_q_norm': 'delta_w', 'delta_kv_norm': 'delta_w', 'delta_w_uq': 'delta_w', 'delta_w_ukv': 'delta_w', 'delta_w_mla_o': 'delta_w', 'delta_w_out': 'delta_w', 'delta_w_ffn_in': 'delta_w', 'delta_w_ffn_out': 'delta_w', 'delta_norm_f': 'delta_w', 'new_m_c_ctx': 'new_m', 'new_m_w_mod': 'new_m', 'new_m_b_mod': 'new_m', 'new_m_norm1': 'new_m', 'new_m_norm2': 'new_m', 'new_m_w_in': 'new_m', 'new_m_s5_a_re': 'new_m', 'new_m_s5_a_im': 'new_m', 'new_m_s5_log_dt': 'new_m', 'new_m_s5_b_re': 'new_m', 'new_m_s5_b_im': 'new_m', 'new_m_s5_c_re': 'new_m', 'new_m_s5_c_im': 'new_m', 'new_m_s5_d': 'new_m', 'new_m_w_glu': 'new_m', 'new_m_q_norm': 'new_m', 'new_m_kv_norm': 'new_m', 'new_m_w_uq': 'new_m', 'new_m_w_ukv': 'new_m', 'new_m_w_mla_o': 'new_m', 'new_m_w_out': 'new_m', 'new_m_w_ffn_in': 'new_m', 'new_m_w_ffn_out': 'new_m', 'new_m_norm_f': 'new_m', 'new_v_c_ctx': 'new_v', 'new_v_w_mod': 'new_v', 'new_v_b_mod': 'new_v', 'new_v_norm1': 'new_v', 'new_v_norm2': 'new_v', 'new_v_w_in': 'new_v', 'new_v_s5_a_re': 'new_v', 'new_v_s5_a_im': 'new_v', 'new_v_s5_log_dt': 'new_v', 'new_v_s5_b_re': 'new_v', 'new_v_s5_b_im': 'new_v', 'new_v_s5_c_re': 'new_v', 'new_v_s5_c_im': 'new_v', 'new_v_s5_d': 'new_v', 'new_v_w_glu': 'new_v', 'new_v_q_norm': 'new_v', 'new_v_kv_norm': 'new_v', 'new_v_w_uq': 'new_v', 'new_v_w_ukv': 'new_v', 'new_v_w_mla_o': 'new_v', 'new_v_w_out': 'new_v', 'new_v_w_ffn_in': 'new_v', 'new_v_w_ffn_out': 'new_v', 'new_v_norm_f': 'new_v'}


def _forward(args):
    return _fwd_reference(*[args[k] for k in FWD_PARAMS])


def _output_shape():
    out = _jax.eval_shape(lambda: _forward(_fwd_setup_inputs(0)))
    return out.shape, out.dtype

N_MICROBATCH = 1
ADAM_LR = 0.001
ADAM_B1 = 0.9
ADAM_B2 = 0.999
ADAM_EPS = 1e-08
ADAM_WD = 0.01
ADAM_STEP = 10
PER_EXAMPLE_BATCH_AXIS = {'x': 0, 'c': 0, 'ctx': 0, 'loss_target': 0}
SHARED_INPUTS = []
_WEIGHT_DTYPES = {'c_ctx': _jnp.float32, 'w_mod': _jnp.float32, 'b_mod': _jnp.float32, 'norm1': _jnp.float32, 'norm2': _jnp.float32, 'w_in': _jnp.float32, 's5_a_re': _jnp.float32, 's5_a_im': _jnp.float32, 's5_log_dt': _jnp.float32, 's5_b_re': _jnp.float32, 's5_b_im': _jnp.float32, 's5_c_re': _jnp.float32, 's5_c_im': _jnp.float32, 's5_d': _jnp.float32, 'w_glu': _jnp.float32, 'q_norm': _jnp.float32, 'kv_norm': _jnp.float32, 'w_uq': _jnp.float32, 'w_ukv': _jnp.float32, 'w_mla_o': _jnp.float32, 'w_out': _jnp.float32, 'w_ffn_in': _jnp.float32, 'w_ffn_out': _jnp.float32, 'norm_f': _jnp.float32}
MOMENT_SCALE = {'c_ctx': 5.735779e-04, 'w_mod': 1.062243e-02, 'b_mod': 1.749736e-02, 'norm1': 1.476617e-03, 'norm2': 1.040410e-02, 'w_in': 9.855108e-04, 's5_a_re': 1.912659e-04, 's5_a_im': 2.244191e-04, 's5_log_dt': 1.140971e-01, 's5_b_re': 1.347589e-04, 's5_b_im': 1.360536e-04, 's5_c_re': 1.995935e-04, 's5_c_im': 1.998131e-04, 's5_d': 2.931246e-03, 'w_glu': 7.369404e-04, 'q_norm': 1.115861e-03, 'kv_norm': 3.395469e-03, 'w_uq': 6.446537e-04, 'w_ukv': 1.097133e-03, 'w_mla_o': 1.074803e-03, 'w_out': 1.493024e-03, 'w_ffn_in': 4.428994e-03, 'w_ffn_out': 7.230367e-03, 'norm_f': 7.996321e+00}


def _to_microbatches(a, axis):
    t = _jnp.moveaxis(a, axis, 0)
    t = t.reshape((N_MICROBATCH, t.shape[0] // N_MICROBATCH) + t.shape[1:])
    return _jnp.moveaxis(t, 1, axis + 1)


def setup_inputs(seed: int = 0) -> dict:
    inp = _fwd_setup_inputs(seed)
    key = _jax.random.fold_in(_jax.random.key(seed), 7919)
    shape, _ = _output_shape()
    out = dict(inp)
    out["loss_target"] = _jax.random.normal(_jax.random.fold_in(key, 0), shape, _jnp.float32)
    for i, name in enumerate(TWIN_WEIGHTS):
        w = inp[name].astype(_jnp.float32)
        if MOMENT_SCALE is None:
            s = _jnp.sqrt(_jnp.mean(_jnp.square(w)) + 1e-30)
        else:
            s = MOMENT_SCALE[name]
        km, kv = _jax.random.split(_jax.random.fold_in(key, i + 1))
        out[name] = w
        out["m_" + name] = s * _jax.random.normal(km, w.shape, _jnp.float32)
        out["v_" + name] = (s * s) * _jax.random.uniform(kv, w.shape, _jnp.float32, 0.5, 1.5)
    if N_MICROBATCH > 1:
        for name, axis in PER_EXAMPLE_BATCH_AXIS.items():
            out[name] = _to_microbatches(out[name], axis)
    return {'x': out['x'], 'c': out['c'], 'ctx': out['ctx'], 'c_ctx': out['c_ctx'], 'w_mod': out['w_mod'], 'b_mod': out['b_mod'], 'norm1': out['norm1'], 'norm2': out['norm2'], 'w_in': out['w_in'], 's5_a_re': out['s5_a_re'], 's5_a_im': out['s5_a_im'], 's5_log_dt': out['s5_log_dt'], 's5_b_re': out['s5_b_re'], 's5_b_im': out['s5_b_im'], 's5_c_re': out['s5_c_re'], 's5_c_im': out['s5_c_im'], 's5_d': out['s5_d'], 'w_glu': out['w_glu'], 'q_norm': out['q_norm'], 'kv_norm': out['kv_norm'], 'w_uq': out['w_uq'], 'w_ukv': out['w_ukv'], 'w_mla_o': out['w_mla_o'], 'w_out': out['w_out'], 'w_ffn_in': out['w_ffn_in'], 'w_ffn_out': out['w_ffn_out'], 'norm_f': out['norm_f'], 'loss_target': out['loss_target'], 'm_c_ctx': out['m_c_ctx'], 'm_w_mod': out['m_w_mod'], 'm_b_mod': out['m_b_mod'], 'm_norm1': out['m_norm1'], 'm_norm2': out['m_norm2'], 'm_w_in': out['m_w_in'], 'm_s5_a_re': out['m_s5_a_re'], 'm_s5_a_im': out['m_s5_a_im'], 'm_s5_log_dt': out['m_s5_log_dt'], 'm_s5_b_re': out['m_s5_b_re'], 'm_s5_b_im': out['m_s5_b_im'], 'm_s5_c_re': out['m_s5_c_re'], 'm_s5_c_im': out['m_s5_c_im'], 'm_s5_d': out['m_s5_d'], 'm_w_glu': out['m_w_glu'], 'm_q_norm': out['m_q_norm'], 'm_kv_norm': out['m_kv_norm'], 'm_w_uq': out['m_w_uq'], 'm_w_ukv': out['m_w_ukv'], 'm_w_mla_o': out['m_w_mla_o'], 'm_w_out': out['m_w_out'], 'm_w_ffn_in': out['m_w_ffn_in'], 'm_w_ffn_out': out['m_w_ffn_out'], 'm_norm_f': out['m_norm_f'], 'v_c_ctx': out['v_c_ctx'], 'v_w_mod': out['v_w_mod'], 'v_b_mod': out['v_b_mod'], 'v_norm1': out['v_norm1'], 'v_norm2': out['v_norm2'], 'v_w_in': out['v_w_in'], 'v_s5_a_re': out['v_s5_a_re'], 'v_s5_a_im': out['v_s5_a_im'], 'v_s5_log_dt': out['v_s5_log_dt'], 'v_s5_b_re': out['v_s5_b_re'], 'v_s5_b_im': out['v_s5_b_im'], 'v_s5_c_re': out['v_s5_c_re'], 'v_s5_c_im': out['v_s5_c_im'], 'v_s5_d': out['v_s5_d'], 'v_w_glu': out['v_w_glu'], 'v_q_norm': out['v_q_norm'], 'v_kv_norm': out['v_kv_norm'], 'v_w_uq': out['v_w_uq'], 'v_w_ukv': out['v_w_ukv'], 'v_w_mla_o': out['v_w_mla_o'], 'v_w_out': out['v_w_out'], 'v_w_ffn_in': out['v_w_ffn_in'], 'v_w_ffn_out': out['v_w_ffn_out'], 'v_norm_f': out['v_norm_f']}


def _loss(weights, diff, rest, loss_target):
    with _jax.named_scope("forward"):
        args = {**rest, TWIN_DIFF_INPUT: diff, **{k: w.astype(_WEIGHT_DTYPES[k]) for k, w in weights.items()}}
        y = _forward(args)
    with _jax.named_scope("loss_head"):
        err = _jnp.square(y.astype(_jnp.float32) - loss_target)
        return 0.5 * _jnp.sum(_jnp.mean(err, axis=-1)) if err.ndim else 0.5 * err


def _adamw(w, g, m, v):
    m = ADAM_B1 * m + (1.0 - ADAM_B1) * g
    v = ADAM_B2 * v + (1.0 - ADAM_B2) * _jnp.square(g)
    m_hat = m / (1.0 - ADAM_B1 ** ADAM_STEP)
    v_hat = v / (1.0 - ADAM_B2 ** ADAM_STEP)
    delta = -ADAM_LR * (m_hat / (_jnp.sqrt(v_hat) + ADAM_EPS) + ADAM_WD * w)
    return delta, m, v


def reference(x, c, ctx, c_ctx, w_mod, b_mod, norm1, norm2, w_in, s5_a_re, s5_a_im, s5_log_dt, s5_b_re, s5_b_im, s5_c_re, s5_c_im, s5_d, w_glu, q_norm, kv_norm, w_uq, w_ukv, w_mla_o, w_out, w_ffn_in, w_ffn_out, norm_f, loss_target, m_c_ctx, m_w_mod, m_b_mod, m_norm1, m_norm2, m_w_in, m_s5_a_re, m_s5_a_im, m_s5_log_dt, m_s5_b_re, m_s5_b_im, m_s5_c_re, m_s5_c_im, m_s5_d, m_w_glu, m_q_norm, m_kv_norm, m_w_uq, m_w_ukv, m_w_mla_o, m_w_out, m_w_ffn_in, m_w_ffn_out, m_norm_f, v_c_ctx, v_w_mod, v_b_mod, v_norm1, v_norm2, v_w_in, v_s5_a_re, v_s5_a_im, v_s5_log_dt, v_s5_b_re, v_s5_b_im, v_s5_c_re, v_s5_c_im, v_s5_d, v_w_glu, v_q_norm, v_kv_norm, v_w_uq, v_w_ukv, v_w_mla_o, v_w_out, v_w_ffn_in, v_w_ffn_out, v_norm_f):
    given = dict(x=x, c=c, ctx=ctx, c_ctx=c_ctx, w_mod=w_mod, b_mod=b_mod, norm1=norm1, norm2=norm2, w_in=w_in, s5_a_re=s5_a_re, s5_a_im=s5_a_im, s5_log_dt=s5_log_dt, s5_b_re=s5_b_re, s5_b_im=s5_b_im, s5_c_re=s5_c_re, s5_c_im=s5_c_im, s5_d=s5_d, w_glu=w_glu, q_norm=q_norm, kv_norm=kv_norm, w_uq=w_uq, w_ukv=w_ukv, w_mla_o=w_mla_o, w_out=w_out, w_ffn_in=w_ffn_in, w_ffn_out=w_ffn_out, norm_f=norm_f, loss_target=loss_target, m_c_ctx=m_c_ctx, m_w_mod=m_w_mod, m_b_mod=m_b_mod, m_norm1=m_norm1, m_norm2=m_norm2, m_w_in=m_w_in, m_s5_a_re=m_s5_a_re, m_s5_a_im=m_s5_a_im, m_s5_log_dt=m_s5_log_dt, m_s5_b_re=m_s5_b_re, m_s5_b_im=m_s5_b_im, m_s5_c_re=m_s5_c_re, m_s5_c_im=m_s5_c_im, m_s5_d=m_s5_d, m_w_glu=m_w_glu, m_q_norm=m_q_norm, m_kv_norm=m_kv_norm, m_w_uq=m_w_uq, m_w_ukv=m_w_ukv, m_w_mla_o=m_w_mla_o, m_w_out=m_w_out, m_w_ffn_in=m_w_ffn_in, m_w_ffn_out=m_w_ffn_out, m_norm_f=m_norm_f, v_c_ctx=v_c_ctx, v_w_mod=v_w_mod, v_b_mod=v_b_mod, v_norm1=v_norm1, v_norm2=v_norm2, v_w_in=v_w_in, v_s5_a_re=v_s5_a_re, v_s5_a_im=v_s5_a_im, v_s5_log_dt=v_s5_log_dt, v_s5_b_re=v_s5_b_re, v_s5_b_im=v_s5_b_im, v_s5_c_re=v_s5_c_re, v_s5_c_im=v_s5_c_im, v_s5_d=v_s5_d, v_w_glu=v_w_glu, v_q_norm=v_q_norm, v_kv_norm=v_kv_norm, v_w_uq=v_w_uq, v_w_ukv=v_w_ukv, v_w_mla_o=v_w_mla_o, v_w_out=v_w_out, v_w_ffn_in=v_w_ffn_in, v_w_ffn_out=v_w_ffn_out, v_norm_f=v_norm_f)
    weights = {n: given[n] for n in TWIN_WEIGHTS}
    shared = {n: given[n] for n in SHARED_INPUTS}
    per_example = {n: given[n] for n in ['x', 'c', 'ctx']}
    grad_fn = _jax.value_and_grad(_loss, argnums=(0, 1))

    def one_microbatch(ex, loss_target):
        ex = dict(ex)
        diff = ex.pop(TWIN_DIFF_INPUT)
        return grad_fn(weights, diff, {**shared, **ex}, loss_target)

    if N_MICROBATCH == 1:
        loss, (grad_w, grad_x) = one_microbatch(per_example, given["loss_target"])
    else:
        def body(carry, xs):
            loss_sum, grad_sum = carry
            l_k, (gw_k, gx_k) = one_microbatch(xs[0], xs[1])
            with _jax.named_scope("update"):
                return (loss_sum + l_k, _jax.tree.map(_jnp.add, grad_sum, gw_k)), gx_k

        init = (_jnp.zeros((), _jnp.float32), _jax.tree.map(_jnp.zeros_like, weights))
        (loss, grad_w), grad_x = _jax.lax.scan(body, init, (per_example, given["loss_target"]))
    with _jax.named_scope("update"):
        delta_w, new_m, new_v = {}, {}, {}
        for n in TWIN_WEIGHTS:
            delta_w[n], new_m[n], new_v[n] = _adamw(weights[n], grad_w[n], given["m_" + n], given["v_" + n])
    return (loss, grad_x, *[grad_w[n] for n in TWIN_WEIGHTS], *[delta_w[n] for n in TWIN_WEIGHTS],
            *[new_m[n] for n in TWIN_WEIGHTS], *[new_v[n] for n in TWIN_WEIGHTS])
```

```python
import functools
import math

import jax
import jax.numpy as jnp
from jax import lax
from jax.experimental import pallas as pl
from jax.experimental.pallas import tpu as pltpu

F32 = jnp.float32
BF16 = jnp.bfloat16

EPS = 1e-6
GRID_W = 64
S5_GROUP = 16
S5_STATE = 64
MLA_HEADS = 8
QK_NOPE = 128
QK_ROPE = 64
V_DIM = 128
ROPE_BASE = 10000.0
ATTN_SCALE = (QK_NOPE + QK_ROPE) ** -0.5
ADAM_LR = 0.001
ADAM_B1 = 0.9
ADAM_B2 = 0.999
ADAM_EPS = 1e-08
ADAM_WD = 0.01
ADAM_STEP = 10

SUBLANES = 8
LANES = 128
V7X_VMEM_BYTES = 64 * 1024 * 1024
VMEM_LIMIT = (V7X_VMEM_BYTES * 7) // 8
N_SEG = SUBLANES
S5_BLOCK_GROUPS = 16
MESH = pl.DeviceIdType.MESH


def _pick(n, target, mult):
    best = None
    d = mult
    while d <= min(n, target):
        if n % d == 0:
            best = d
        d += mult
    return n if best is None else best


def _cparams(sem=None):
    return pltpu.CompilerParams(dimension_semantics=sem, vmem_limit_bytes=VMEM_LIMIT)


def _mm(a, b, *, ta=False, tb=False, out_dtype=F32, name):
    if ta:
        K, M = a.shape
    else:
        M, K = a.shape
    if tb:
        N, K2 = b.shape
    else:
        K2, N = b.shape
    assert K == K2, (a.shape, b.shape, ta, tb)
    tm = _pick(M, 1024, LANES if ta else 16)
    tn = _pick(N, 1024, LANES)
    tk = _pick(K, 512, LANES if (not ta or tb) else 16)
    nk = K // tk
    dims = (((0 if ta else 1,), (1 if tb else 0,)), ((), ()))

    def body(a_ref, b_ref, o_ref, acc_ref):
        k = pl.program_id(2)

        @pl.when(k == 0)
        def _():
            acc_ref[...] = jnp.zeros_like(acc_ref)

        acc_ref[...] += lax.dot_general(a_ref[...].astype(BF16), b_ref[...].astype(BF16), dims,
                                        preferred_element_type=F32)

        @pl.when(k == nk - 1)
        def _():
            o_ref[...] = acc_ref[...].astype(o_ref.dtype)

    a_spec = pl.BlockSpec((tk, tm), lambda i, j, k: (k, i)) if ta else pl.BlockSpec((tm, tk), lambda i, j, k: (i, k))
    b_spec = pl.BlockSpec((tn, tk), lambda i, j, k: (j, k)) if tb else pl.BlockSpec((tk, tn), lambda i, j, k: (k, j))
    return pl.pallas_call(
        body, name=name, grid=(M // tm, N // tn, nk),
        in_specs=[a_spec, b_spec],
        out_specs=pl.BlockSpec((tm, tn), lambda i, j, k: (i, j)),
        out_shape=jax.ShapeDtypeStruct((M, N), out_dtype),
        scratch_shapes=[pltpu.VMEM((tm, tn), F32)],
        compiler_params=_cparams(("parallel", "parallel", "arbitrary")),
    )(a, b)


def _row_tile(tiled, extra_bytes=0):
    rows = tiled[0].shape[0]
    per_row = sum(a.shape[1] * 4 for a in tiled) + extra_bytes
    target = max(SUBLANES, (6 * 1024 * 1024) // max(per_row, 1))
    return _pick(rows, min(target, 512), 16)


def _rw(f, tiled, bcast, out_dtypes, *, name):
    nt, nb = len(tiled), len(bcast)
    rows = tiled[0].shape[0]
    outs_aval = jax.eval_shape(f, *[jax.ShapeDtypeStruct((16, a.shape[1]), F32) for a in tiled],
                               *[jax.ShapeDtypeStruct(b.shape, F32) for b in bcast])
    widths = [o.shape[1] for o in outs_aval]
    tm = _row_tile(tiled, sum(w * 4 for w in widths))

    def body(*refs):
        tin = [r[...].astype(F32) for r in refs[:nt]]
        bin_ = [r[...].astype(F32) for r in refs[nt:nt + nb]]
        outs = f(*tin, *bin_)
        for o_ref, o in zip(refs[nt + nb:], outs):
            o_ref[...] = o.astype(o_ref.dtype)

    in_specs = [pl.BlockSpec((tm, a.shape[1]), lambda i: (i, 0)) for a in tiled]
    in_specs += [pl.BlockSpec(b.shape, lambda i: (0, 0)) for b in bcast]
    res = pl.pallas_call(
        body, name=name, grid=(rows // tm,), in_specs=in_specs,
        out_specs=[pl.BlockSpec((tm, w), lambda i: (i, 0)) for w in widths],
        out_shape=[jax.ShapeDtypeStruct((rows, w), dt) for w, dt in zip(widths, out_dtypes)],
        compiler_params=_cparams(("parallel",)),
    )(*tiled, *bcast)
    return list(res)


def _rw_vjp(f, tiled, bcast, cts, need_t, need_b, t_dtypes, *, name):
    nt, nb = len(tiled), len(bcast)
    rows = tiled[0].shape[0]
    flat_cts = [c for group in cts for c in group]
    t_idx = [i for i in range(nt) if need_t[i]]
    b_idx = [i for i in range(nb) if need_b[i]]
    tm = _row_tile(list(tiled) + flat_cts, sum(tiled[i].shape[1] * 4 for i in t_idx))
    nc = len(flat_cts)

    def body(*refs):
        i = pl.program_id(0)
        tin = [r[...].astype(F32) for r in refs[:nt]]
        bin_ = [r[...].astype(F32) for r in refs[nt:nt + nb]]
        ct_refs = refs[nt + nb:nt + nb + nc]
        out_refs = refs[nt + nb + nc:]
        outs, vjp_fn = jax.vjp(f, *tin, *bin_)
        ct_vals, pos = [], 0
        for o, group in zip(outs, cts):
            acc = jnp.zeros_like(o)
            for _ in group:
                acc = acc + ct_refs[pos][...].astype(F32)
                pos += 1
            ct_vals.append(acc)
        grads = vjp_fn(tuple(ct_vals))
        for o_ref, k in zip(out_refs[:len(t_idx)], t_idx):
            o_ref[...] = grads[k].astype(o_ref.dtype)
        for o_ref, k in zip(out_refs[len(t_idx):], b_idx):
            @pl.when(i == 0)
            def _(o_ref=o_ref):
                o_ref[...] = jnp.zeros_like(o_ref)

            o_ref[...] += grads[nt + k]

    in_specs = [pl.BlockSpec((tm, a.shape[1]), lambda i: (i, 0)) for a in tiled]
    in_specs += [pl.BlockSpec(b.shape, lambda i: (0, 0)) for b in bcast]
    in_specs += [pl.BlockSpec((tm, c.shape[1]), lambda i: (i, 0)) for c in flat_cts]
    out_specs = [pl.BlockSpec((tm, tiled[k].shape[1]), lambda i: (i, 0)) for k in t_idx]
    out_specs += [pl.BlockSpec(bcast[k].shape, lambda i: (0, 0)) for k in b_idx]
    out_shape = [jax.ShapeDtypeStruct(tiled[k].shape, dt) for k, dt in zip(t_idx, t_dtypes)]
    out_shape += [jax.ShapeDtypeStruct(bcast[k].shape, F32) for k in b_idx]
    res = pl.pallas_call(
        body, name=name, grid=(rows // tm,), in_specs=in_specs, out_specs=out_specs, out_shape=out_shape,
        compiler_params=_cparams(("arbitrary",)),
    )(*tiled, *bcast, *flat_cts)
    res = list(res)
    return res[:len(t_idx)], res[len(t_idx):]


def _rms(x, g):
    return x * lax.rsqrt(jnp.mean(x * x, axis=-1, keepdims=True) + EPS) * g


def _f_norm_mod(x, g, sc, sh):
    return (_rms(x, g) * (1.0 + sc) + sh,)


def _f_norm_mod_keep(x, g, sc, sh):
    return (_rms(x, g) * (1.0 + sc) + sh, x)


@jax.custom_vjp
def _swap16(x):
    w = x.shape[-1]
    lane = lax.broadcasted_iota(jnp.int32, x.shape, x.ndim - 1)
    return jnp.where((lane & 16) == 0, pltpu.roll(x, w - 16, x.ndim - 1), pltpu.roll(x, 16, x.ndim - 1))


_swap16.defvjp(lambda x: (_swap16(x), None), lambda _, g: (_swap16(g),))


def _rope(x, cos, sin):
    return x * cos + _swap16(x) * sin


def _make_f_post_in(sw, q_rank, kv_rank, with_q):
    o1, o2, o3 = sw, sw + q_rank, sw + q_rank + kv_rank

    if with_q:
        def f(ha, cos, sin, qg, kvg):
            u = ha[:, :o1]
            cqn = _rms(ha[:, o1:o2], qg)
            ckvn = _rms(ha[:, o2:o3], kvg)
            kr = _rope(ha[:, o3:o3 + LANES], cos, sin)
            return u, cqn, ckvn, kr
    else:
        def f(ha, kvg):
            return ha[:, :o1], _rms(ha[:, o2:o3], kvg), ha[:, o3:o3 + LANES]
    return f


def _f_qpost(q2, cos, sin):
    w = q2.shape[1] // 2
    reps = w // LANES
    qr = _rope(q2[:, w:], jnp.tile(cos, (1, reps)), jnp.tile(sin, (1, reps)))
    return (jnp.concatenate([q2[:, :w], qr], axis=1),)


def _f_s5post(u, r, d):
    return (jax.nn.gelu(d * u + r, approximate=True),)


def _f_merge(ab, bm, gt):
    d = bm.shape[1]
    br_s5 = ab[:, :d] * jax.nn.sigmoid(ab[:, d:])
    g = jax.nn.sigmoid(gt)
    return (g[:, :d] * br_s5 + g[:, d:] * bm,)


def _f_resid_norm(x, out, g1, n2, sc2, sh2):
    x1 = x + g1 * out
    return x1, _rms(x1, n2) * (1.0 + sc2) + sh2


def _f_swiglu(ab):
    d = ab.shape[1] // 2
    return (jax.nn.silu(ab[:, :d]) * ab[:, d:],)


def _f_final(x1, f, tgt, g2, nf):
    y = _rms(x1 + g2 * f, nf)
    return (0.5 * jnp.mean(jnp.square(y - tgt), axis=-1, keepdims=True),)


def _bd_fanout(x, ws, *, name):
    nw = len(ws)
    nb, kb, nn = ws[0].shape
    T = x.shape[0]
    tm = _pick(T, 512, 16)

    def body(*refs):
        xb = refs[0][...].astype(BF16)
        for w_ref, o_ref in zip(refs[1:1 + nw], refs[1 + nw:]):
            o_ref[...] = jnp.dot(xb, w_ref[0].astype(BF16), preferred_element_type=F32)

    return list(pl.pallas_call(
        body, name=name, grid=(nb, T // tm),
        in_specs=[pl.BlockSpec((tm, kb), lambda j, i: (i, j))] + [pl.BlockSpec((1, kb, nn), lambda j, i: (j, 0, 0))] * nw,
        out_specs=[pl.BlockSpec((tm, nn), lambda j, i: (i, j))] * nw,
        out_shape=[jax.ShapeDtypeStruct((T, nb * nn), F32)] * nw,
        compiler_params=_cparams(("parallel", "parallel")),
    )(x, *ws))


def _bd_fanin(xs, ws, *, name):
    nw = len(ws)
    nb, kb, nn = ws[0].shape
    T = xs[0].shape[0]
    tm = _pick(T, 512, 16)

    def body(*refs):
        acc = None
        for x_ref, w_ref in zip(refs[:nw], refs[nw:2 * nw]):
            t = jnp.dot(x_ref[...].astype(BF16), w_ref[0].astype(BF16), preferred_element_type=F32)
            acc = t if acc is None else acc + t
        refs[2 * nw][...] = acc

    return pl.pallas_call(
        body, name=name, grid=(nb, T // tm),
        in_specs=[pl.BlockSpec((tm, kb), lambda j, i: (i, j))] * nw + [pl.BlockSpec((1, kb, nn), lambda j, i: (j, 0, 0))] * nw,
        out_specs=pl.BlockSpec((tm, nn), lambda j, i: (i, j)),
        out_shape=jax.ShapeDtypeStruct((T, nb * nn), F32),
        compiler_params=_cparams(("parallel", "parallel")),
    )(*xs, *ws)


def _bd_dw(xs, dys, nb, *, name):
    npair = len(xs)
    T = xs[0].shape[0]
    kb = xs[0].shape[1] // nb
    nn = dys[0].shape[1] // nb
    tm = _pick(T, 512, 16)
    dims = (((0,), (0,)), ((), ()))

    def body(*refs):
        i = pl.program_id(1)
        for x_ref, d_ref, o_ref in zip(refs[:npair], refs[npair:2 * npair], refs[2 * npair:]):
            @pl.when(i == 0)
            def _(o_ref=o_ref):
                o_ref[...] = jnp.zeros_like(o_ref)

            o_ref[0] += lax.dot_general(x_ref[...].astype(BF16), d_ref[...].astype(BF16), dims,
                                        preferred_element_type=F32)

    return list(pl.pallas_call(
        body, name=name, grid=(nb, T // tm),
        in_specs=[pl.BlockSpec((tm, kb), lambda j, i: (i, j))] * npair + [pl.BlockSpec((tm, nn), lambda j, i: (i, j))] * npair,
        out_specs=[pl.BlockSpec((1, kb, nn), lambda j, i: (j, 0, 0))] * npair,
        out_shape=[jax.ShapeDtypeStruct((nb, kb, nn), F32)] * npair,
        compiler_params=_cparams(("parallel", "arbitrary")),
    )(*xs, *dys))


def _cmul(ar, ai, br, bi):
    return ar * br - ai * bi, ar * bi + ai * br


def _cpow(lr, li, n):
    rr, ri = None, None
    br, bi = lr, li
    while n:
        if n & 1:
            rr, ri = (br, bi) if rr is None else _cmul(rr, ri, br, bi)
        n >>= 1
        if n:
            br, bi = _cmul(br, bi, br, bi)
    return rr, ri


def _s5_scan(b_re, b_im, lam_re, lam_im, h0_re, h0_im, e0_re, e0_im, *, reverse, name):
    rows, C = b_re.shape
    n = rows // N_SEG
    cb = _pick(C, 512, LANES)
    seg_order = list(range(N_SEG))[::-1] if reverse else list(range(N_SEG))
    s_first, s_last = seg_order[0], seg_order[-1]

    def body(br_ref, bi_ref, lr_ref, li_ref, h0r_ref, h0i_ref, e0r_ref, e0i_ref, hr_ref, hi_ref, htr_ref, hti_ref):
        shape = (N_SEG, cb)
        lr = jnp.broadcast_to(lr_ref[...], shape)
        li = jnp.broadcast_to(li_ref[...], shape)
        row = lax.broadcasted_iota(jnp.int32, shape, 0)

        def step_of(k):
            return (n - 1 - k) if reverse else k

        def rows_of(k):
            return pl.ds(pl.multiple_of(step_of(k) * N_SEG, N_SEG), N_SEG)

        first = row == s_first
        hr = br_ref[rows_of(0), :] + jnp.where(first, e0r_ref[...], 0.0)
        hi = bi_ref[rows_of(0), :] + jnp.where(first, e0i_ref[...], 0.0)
        hr_ref[rows_of(0), :] = hr
        hi_ref[rows_of(0), :] = hi

        def pass1(k, carry):
            hr, hi = carry
            pr, pi = _cmul(lr, li, hr, hi)
            hr = pr + br_ref[rows_of(k), :]
            hi = pi + bi_ref[rows_of(k), :]
            hr_ref[rows_of(k), :] = hr
            hi_ref[rows_of(k), :] = hi
            return hr, hi

        er, ei = lax.fori_loop(1, n, pass1, (hr, hi))

        lnr, lni = _cpow(lr_ref[...], li_ref[...], n)
        cr, ci = h0r_ref[...], h0i_ref[...]
        cin_r = jnp.zeros(shape, F32)
        cin_i = jnp.zeros(shape, F32)
        for s in seg_order:
            cin_r = jnp.where(row == s, cr, cin_r)
            cin_i = jnp.where(row == s, ci, cin_i)
            if s != s_last:
                pr, pi = _cmul(lnr, lni, cr, ci)
                cr = pr + jnp.sum(jnp.where(row == s, er, 0.0), axis=0, keepdims=True)
                ci = pi + jnp.sum(jnp.where(row == s, ei, 0.0), axis=0, keepdims=True)

        def pass2(k, carry):
            pr, pi = carry
            ar, ai = _cmul(pr, pi, cin_r, cin_i)
            hr = hr_ref[rows_of(k), :] + ar
            hi = hi_ref[rows_of(k), :] + ai
            hr_ref[rows_of(k), :] = hr
            hi_ref[rows_of(k), :] = hi
            npr, npi = _cmul(pr, pi, lr, li)
            return npr, npi

        lax.fori_loop(0, n, pass2, (lr, li))
        last_r = hr_ref[rows_of(n - 1), :]
        last_i = hi_ref[rows_of(n - 1), :]
        htr_ref[...] = jnp.sum(jnp.where(row == s_last, last_r, 0.0), axis=0, keepdims=True)
        hti_ref[...] = jnp.sum(jnp.where(row == s_last, last_i, 0.0), axis=0, keepdims=True)

    big = pl.BlockSpec((rows, cb), lambda j: (0, j))
    vec = pl.BlockSpec((1, cb), lambda j: (0, j))
    return pl.pallas_call(
        body, name=name, grid=(C // cb,),
        in_specs=[big, big] + [vec] * 6,
        out_specs=[big, big, vec, vec],
        out_shape=[jax.ShapeDtypeStruct((rows, C), F32)] * 2 + [jax.ShapeDtypeStruct((1, C), F32)] * 2,
        compiler_params=_cparams(("parallel",)),
    )(b_re, b_im, lam_re, lam_im, h0_re, h0_im, e0_re, e0_im)


def _s5_dlam(mu_re, mu_im, h_re, h_im, h0_re, h0_im, *, reverse, name):
    rows, C = h_re.shape
    n = rows // N_SEG
    cb = _pick(C, 512, LANES)
    s_first = N_SEG - 1 if reverse else 0

    def body(mr_ref, mi_ref, hr_ref, hi_ref, h0r_ref, h0i_ref, dr_ref, di_ref):
        shape = (N_SEG, cb)
        row = lax.broadcasted_iota(jnp.int32, shape, 0)

        def rows_of(k):
            step = (n - 1 - k) if reverse else k
            return pl.ds(pl.multiple_of(step * N_SEG, N_SEG), N_SEG)

        def term(k, pr, pi):
            mr, mi = mr_ref[rows_of(k), :], mi_ref[rows_of(k), :]
            return mr * pr + mi * pi, mi * pr - mr * pi

        shift = N_SEG - 1 if reverse else 1
        pr = jnp.where(row == s_first, h0r_ref[...], pltpu.roll(hr_ref[rows_of(n - 1), :], shift, 0))
        pi = jnp.where(row == s_first, h0i_ref[...], pltpu.roll(hi_ref[rows_of(n - 1), :], shift, 0))
        acc = term(0, pr, pi)

        def loop(k, acc):
            tr, ti = term(k, hr_ref[rows_of(k - 1), :], hi_ref[rows_of(k - 1), :])
            return acc[0] + tr, acc[1] + ti

        ar, ai = lax.fori_loop(1, n, loop, acc)
        dr_ref[...] = jnp.sum(ar, axis=0, keepdims=True)
        di_ref[...] = jnp.sum(ai, axis=0, keepdims=True)

    big = pl.BlockSpec((rows, cb), lambda j: (0, j))
    vec = pl.BlockSpec((1, cb), lambda j: (0, j))
    return pl.pallas_call(
        body, name=name, grid=(C // cb,),
        in_specs=[big] * 4 + [vec] * 2, out_specs=[vec, vec],
        out_shape=[jax.ShapeDtypeStruct((1, C), F32)] * 2,
        compiler_params=_cparams(("parallel",)),
    )(mu_re, mu_im, h_re, h_im, h0_re, h0_im)


def _attn_scores(qn, qr, kn, kr):
    nt = (((1,), (1,)), ((), ()))
    s = lax.dot_general(qn, kn, nt, preferred_element_type=F32) + lax.dot_general(qr, kr, nt, preferred_element_type=F32)
    s = s * ATTN_SCALE
    p = jnp.exp(s - jnp.max(s, axis=-1, keepdims=True))
    return p / jnp.sum(p, axis=-1, keepdims=True)


def _attn_specs(L, T, tq):
    H = MLA_HEADS
    return [
        pl.BlockSpec((tq, LANES), lambda h, i: (i, h)),
        pl.BlockSpec((tq, LANES), lambda h, i: (i, H + h)),
        pl.BlockSpec((T, LANES), lambda h, i: (0, 2 * h)),
        pl.BlockSpec((T, LANES), lambda h, i: (0, 2 * h + 1)),
        pl.BlockSpec((T, LANES), lambda h, i: (0, 0)),
    ]


def _attn_fwd(qq, kv, kr, *, name):
    L, T = qq.shape[0], kv.shape[0]
    tq = _pick(L, 256, 16)

    def body(qn_ref, qr_ref, kn_ref, v_ref, kr_ref, o_ref):
        p = _attn_scores(qn_ref[...], qr_ref[...], kn_ref[...], kr_ref[...])
        o_ref[...] = jnp.dot(p.astype(BF16), v_ref[...], preferred_element_type=F32).astype(o_ref.dtype)

    return pl.pallas_call(
        body, name=name, grid=(MLA_HEADS, L // tq), in_specs=_attn_specs(L, T, tq),
        out_specs=pl.BlockSpec((tq, LANES), lambda h, i: (i, h)),
        out_shape=jax.ShapeDtypeStruct((L, MLA_HEADS * V_DIM), BF16),
        compiler_params=_cparams(("parallel", "parallel")),
    )(qq, qq, kv, kv, kr)


def _attn_bwd(qq, kv, kr, do, *, name):
    L, T = qq.shape[0], kv.shape[0]
    H = MLA_HEADS
    tq = _pick(L, 256, 16)
    tn = (((0,), (0,)), ((), ()))
    nt = (((1,), (1,)), ((), ()))

    def body(qn_ref, qr_ref, kn_ref, v_ref, kr_ref, do_ref, dqn_ref, dqr_ref, dkn_ref, dv_ref, dkr_ref, dkn_acc, dv_acc):
        h, i = pl.program_id(0), pl.program_id(1)
        qn, qr, kn, v, krv, dov = qn_ref[...], qr_ref[...], kn_ref[...], v_ref[...], kr_ref[...], do_ref[...]
        p = _attn_scores(qn, qr, kn, krv)
        pb = p.astype(BF16)
        dp = lax.dot_general(dov, v, nt, preferred_element_type=F32)
        ds = (p * (dp - jnp.sum(dp * p, axis=-1, keepdims=True)) * ATTN_SCALE).astype(BF16)
        dqn_ref[...] = jnp.dot(ds, kn, preferred_element_type=F32)
        dqr_ref[...] = jnp.dot(ds, krv, preferred_element_type=F32)

        @pl.when(i == 0)
        def _():
            dkn_acc[...] = jnp.zeros_like(dkn_acc)
            dv_acc[...] = jnp.zeros_like(dv_acc)

        @pl.when((i == 0) & (h == 0))
        def _():
            dkr_ref[...] = jnp.zeros_like(dkr_ref)

        dv_acc[...] += lax.dot_general(pb, dov, tn, preferred_element_type=F32)
        dkn_acc[...] += lax.dot_general(ds, qn, tn, preferred_element_type=F32)
        dkr_ref[...] += lax.dot_general(ds, qr, tn, preferred_element_type=F32)

        @pl.when(i == pl.num_programs(1) - 1)
        def _():
            dkn_ref[...] = dkn_acc[...].astype(dkn_ref.dtype)
            dv_ref[...] = dv_acc[...].astype(dv_ref.dtype)

    in_specs = _attn_specs(L, T, tq) + [pl.BlockSpec((tq, LANES), lambda h, i: (i, h))]
    dqn, dqr, dkn, dv, dkr = pl.pallas_call(
        body, name=name, grid=(H, L // tq), in_specs=in_specs,
        out_specs=[pl.BlockSpec((tq, LANES), lambda h, i: (i, h)), pl.BlockSpec((tq, LANES), lambda h, i: (i, h)),
                   pl.BlockSpec((T, LANES), lambda h, i: (0, h)), pl.BlockSpec((T, LANES), lambda h, i: (0, h)),
                   pl.BlockSpec((T, LANES), lambda h, i: (0, 0))],
        out_shape=[jax.ShapeDtypeStruct((L, H * LANES), F32), jax.ShapeDtypeStruct((L, H * LANES), F32),
                   jax.ShapeDtypeStruct((T, H * LANES), BF16), jax.ShapeDtypeStruct((T, H * LANES), BF16),
                   jax.ShapeDtypeStruct((T, LANES), F32)],
        scratch_shapes=[pltpu.VMEM((T, LANES), F32), pltpu.VMEM((T, LANES), F32)],
        compiler_params=_cparams(("arbitrary", "arbitrary")),
    )(qq, qq, kv, kv, kr, do)
    return dqn, dqr, dkn, dv, dkr


def _adamw(w, g, m, v, *, name):
    c1 = 1.0 - ADAM_B1 ** ADAM_STEP
    c2 = 1.0 - ADAM_B2 ** ADAM_STEP

    def f(w, g, m, v):
        m = ADAM_B1 * m + (1.0 - ADAM_B1) * g
        v = ADAM_B2 * v + (1.0 - ADAM_B2) * jnp.square(g)
        delta = -ADAM_LR * ((m / c1) / (jnp.sqrt(v / c2) + ADAM_EPS) + ADAM_WD * w)
        return g, delta, m, v

    return _rw(f, [w, g, m, v], [], [F32] * 4, name=name)


def _sum_rows(parts, out_dtype, *, name):
    def f(*xs):
        acc = xs[0]
        for x in xs[1:]:
            acc = acc + x
        return (acc,)

    return _rw(f, parts, [], [out_dtype], name=name)[0]


def _place():
    return lax.axis_index("x"), lax.axis_index("y"), lax.axis_index("c")


def _other_chips(x, y):
    chips = [(1 - x, y), (x, 1 - y), (1 - x, 1 - y)]
    return chips, [2 * cx + cy for cx, cy in chips]


HBM = pl.BlockSpec(memory_space=pl.ANY)


def _allgather8(v, *, name):
    rows, cols = v.shape

    def body(v_ref, out_ref, send_sems, recv_sems):
        x, y, c = _place()
        me = 4 * x + 2 * y + c
        out_ref[me] = v_ref[...]
        copies = []
        for k in range(1, 8):
            bx, by, bc = (k >> 2) & 1, (k >> 1) & 1, k & 1
            px, py, pc = x ^ bx, y ^ by, c ^ bc
            cp = pltpu.make_async_remote_copy(
                src_ref=v_ref, dst_ref=out_ref.at[me], send_sem=send_sems.at[k - 1], recv_sem=recv_sems.at[k - 1],
                device_id=(px, py, pc), device_id_type=MESH)
            cp.start()
            copies.append((cp, 4 * px + 2 * py + pc))
        for k, (cp, peer) in enumerate(copies):
            pltpu.make_async_remote_copy(
                src_ref=v_ref, dst_ref=out_ref.at[peer], send_sem=send_sems.at[k], recv_sem=recv_sems.at[k],
                device_id=(x, y, c), device_id_type=MESH).wait_recv()
        for cp, _ in copies:
            cp.wait_send()

    return pl.pallas_call(
        body, name=name, out_shape=jax.ShapeDtypeStruct((8, rows, cols), v.dtype),
        in_specs=[pl.BlockSpec(memory_space=pltpu.VMEM)], out_specs=pl.BlockSpec(memory_space=pltpu.VMEM),
        scratch_shapes=[pltpu.SemaphoreType.DMA((7,)), pltpu.SemaphoreType.DMA((7,))],
        compiler_params=pltpu.CompilerParams(vmem_limit_bytes=VMEM_LIMIT),
    )(v)


def _allgather_shards(shards, *, name):
    n = len(shards)

    def body(*refs):
        ins, outs = refs[:n], refs[n:2 * n]
        send_sems, recv_sems, local_sems = refs[2 * n:]
        x, y, c = _place()
        me_chip = 2 * x + y
        sibling = (x, y, 1 - c)
        chips, chip_ids = _other_chips(x, y)

        def half(k, hf):
            hr = shards[k].shape[0] // 2
            return pl.ds(pl.multiple_of(hf * hr, 16), hr)

        def remote(k, j, blk, hf, to, src=None):
            dst = outs[k].at[blk, half(k, hf), :]
            return pltpu.make_async_remote_copy(
                src_ref=dst if src is None else src, dst_ref=dst, send_sem=send_sems.at[6 * k + j],
                recv_sem=recv_sems.at[6 * k + j], device_id=to, device_id_type=MESH)

        local = [pltpu.make_async_copy(ins[k], outs[k].at[me_chip], local_sems.at[k]) for k in range(n)]
        for cp in local:
            cp.start()
        sends = []
        for k in range(n):
            for j, chip in enumerate(chips):
                cp = remote(k, j, me_chip, c, (*chip, c), src=ins[k].at[half(k, c), :])
                cp.start()
                sends.append(cp)
        for k in range(n):
            for j, chip in enumerate(chips):
                remote(k, j, chip_ids[j], c, (x, y, c)).wait_recv()
                cp = remote(k, 3 + j, chip_ids[j], c, sibling)
                cp.start()
                sends.append(cp)
        for k in range(n):
            for j in range(3):
                remote(k, 3 + j, chip_ids[j], 1 - c, (x, y, c)).wait_recv()
        for cp in sends:
            cp.wait_send()
        for cp in local:
            cp.wait()

    return list(pl.pallas_call(
        body, name=name, out_shape=[jax.ShapeDtypeStruct((4,) + s.shape, s.dtype) for s in shards],
        in_specs=[HBM] * n, out_specs=[HBM] * n,
        scratch_shapes=[pltpu.SemaphoreType.DMA((6 * n,)), pltpu.SemaphoreType.DMA((6 * n,)), pltpu.SemaphoreType.DMA((n,))],
    )(*shards))


def _pair_exchange(gs, *, name):
    n = len(gs)

    def body(*refs):
        ins, outs = refs[:n], refs[n:2 * n]
        send_sems, recv_sems = refs[2 * n:]
        x, y, c = _place()
        copies = []
        for k in range(n):
            hr = gs[k].shape[1] // 2
            src = ins[k].at[:, pl.ds(pl.multiple_of((1 - c) * hr, 16), hr), :]
            cp = pltpu.make_async_remote_copy(src_ref=src, dst_ref=outs[k], send_sem=send_sems.at[k], recv_sem=recv_sems.at[k],
                                              device_id=(x, y, 1 - c), device_id_type=MESH)
            cp.start()
            copies.append(cp)
        for cp in copies:
            cp.wait()

    return list(pl.pallas_call(
        body, name=name,
        out_shape=[jax.ShapeDtypeStruct((4, g.shape[1] // 2, g.shape[2]), g.dtype) for g in gs],
        in_specs=[HBM] * n, out_specs=[HBM] * n,
        scratch_shapes=[pltpu.SemaphoreType.DMA((n,)), pltpu.SemaphoreType.DMA((n,))],
    )(*gs))


def _chip_exchange(ps, *, name):
    n = len(ps)

    def body(*refs):
        ins, outs = refs[:n], refs[n:2 * n]
        send_sems, recv_sems = refs[2 * n:]
        x, y, c = _place()
        chips, chip_ids = _other_chips(x, y)
        copies = []
        for k in range(n):
            for j, chip in enumerate(chips):
                cp = pltpu.make_async_remote_copy(
                    src_ref=ins[k].at[chip_ids[j]], dst_ref=outs[k].at[j], send_sem=send_sems.at[3 * k + j],
                    recv_sem=recv_sems.at[3 * k + j], device_id=(*chip, c), device_id_type=MESH)
                cp.start()
                copies.append(cp)
        for cp in copies:
            cp.wait()

    return list(pl.pallas_call(
        body, name=name,
        out_shape=[jax.ShapeDtypeStruct((3,) + p.shape[1:], p.dtype) for p in ps],
        in_specs=[HBM] * n, out_specs=[HBM] * n,
        scratch_shapes=[pltpu.SemaphoreType.DMA((3 * n,)), pltpu.SemaphoreType.DMA((3 * n,))],
    )(*ps))


def _pair_gather(fs, *, name):
    n = len(fs)

    def body(*refs):
        ins, outs = refs[:n], refs[n:2 * n]
        send_sems, recv_sems, local_sems = refs[2 * n:]
        x, y, c = _place()
        copies = []
        for k in range(n):
            loc = pltpu.make_async_copy(ins[k], outs[k].at[c], local_sems.at[k])
            loc.start()
            cp = pltpu.make_async_remote_copy(src_ref=ins[k], dst_ref=outs[k].at[c], send_sem=send_sems.at[k],
                                              recv_sem=recv_sems.at[k], device_id=(x, y, 1 - c), device_id_type=MESH)
            cp.start()
            copies.append((loc, cp))
        for k, (loc, cp) in enumerate(copies):
            loc.wait()
            cp.wait_send()
            pltpu.make_async_remote_copy(src_ref=ins[k], dst_ref=outs[k].at[1 - c], send_sem=send_sems.at[k],
                                         recv_sem=recv_sems.at[k], device_id=(x, y, c), device_id_type=MESH).wait_recv()

    return list(pl.pallas_call(
        body, name=name, out_shape=[jax.ShapeDtypeStruct((2,) + f.shape, f.dtype) for f in fs],
        in_specs=[HBM] * n, out_specs=[HBM] * n,
        scratch_shapes=[pltpu.SemaphoreType.DMA((n,)), pltpu.SemaphoreType.DMA((n,)), pltpu.SemaphoreType.DMA((n,))],
    )(*fs))


def _reduce_scatter(gs):
    x, y, c = _place()
    me_chip = 2 * x + y
    got = _pair_exchange(gs, name="rs_pair_exchange")
    pair = []
    for k, (g, r) in enumerate(zip(gs, got)):
        hr = g.shape[1] // 2
        mine = lax.dynamic_slice_in_dim(g, c * hr, hr, axis=1)
        s = _sum_rows([mine.reshape(4 * hr, -1), r.reshape(4 * hr, -1)], g.dtype, name=f"rs_pair_sum_{k}")
        pair.append(s.reshape(4, hr, -1))
    landed = _chip_exchange(pair, name="rs_chip_exchange")
    halves = []
    for k, (p, l) in enumerate(zip(pair, landed)):
        own = lax.dynamic_index_in_dim(p, me_chip, axis=0, keepdims=False)
        halves.append(_sum_rows([own, l[0], l[1], l[2]], F32, name=f"rs_chip_sum_{k}"))
    full = _pair_gather(halves, name="rs_pair_gather")
    return [f.reshape(2 * f.shape[1], f.shape[2]) for f in full]


def _to_segments(a):
    rows = a.shape[0]
    return a.reshape(N_SEG, rows // N_SEG, -1).transpose(1, 0, 2).reshape(rows, -1)


def _from_segments(a):
    rows = a.shape[0]
    return a.reshape(rows // N_SEG, N_SEG, -1).transpose(1, 0, 2).reshape(rows, -1)


def _rope_tables(L):
    t = jnp.arange(L, dtype=jnp.int32)
    row = (t // GRID_W).astype(F32)
    col = (t % GRID_W).astype(F32)
    n_freq = QK_ROPE // 4
    inv = ROPE_BASE ** (-jnp.arange(n_freq, dtype=F32) / n_freq)
    a0, a1 = row[:, None] * inv, col[:, None] * inv
    z = jnp.zeros((L, LANES - QK_ROPE), F32)
    cos = jnp.concatenate([jnp.cos(a0), jnp.cos(a0), jnp.cos(a1), jnp.cos(a1), z], axis=1)
    sin = jnp.concatenate([-jnp.sin(a0), jnp.sin(a0), -jnp.sin(a1), jnp.sin(a1), z], axis=1)
    return _to_segments(cos), _to_segments(sin)


def _col_blocks(w, nblk):
    r, c = w.shape
    return w.reshape(r, nblk, c // nblk).transpose(1, 0, 2)


def _from_col_blocks(w4):
    nblk, r, c = w4.shape
    return w4.transpose(1, 0, 2).reshape(r, nblk * c)


def _s5_discretize(a_re, a_im, log_dt, b_re, b_im):
    dt = jnp.exp(log_dt)[:, None]
    mag = jnp.exp(a_re * dt)
    ab_re, ab_im = mag * jnp.cos(a_im * dt), mag * jnp.sin(a_im * dt)
    den = a_re * a_re + a_im * a_im
    nr, ni = ab_re - 1.0, ab_im
    co_re = (nr * a_re + ni * a_im) / den
    co_im = (ni * a_re - nr * a_im) / den
    bb_re = co_re[..., None] * b_re - co_im[..., None] * b_im
    bb_im = co_re[..., None] * b_im + co_im[..., None] * b_re
    return ab_re, ab_im, bb_re, bb_im


def _diag_blocks_in(bb, gpb):
    G, N, P = bb.shape
    eye = jnp.eye(gpb, dtype=bb.dtype)
    t = jnp.einsum("jgnp,gh->jgphn", bb.reshape(G // gpb, gpb, N, P), eye)
    return t.reshape(G // gpb, gpb * P, gpb * N)


def _diag_blocks_out(cc, gpb):
    G, P, N = cc.shape
    eye = jnp.eye(gpb, dtype=cc.dtype)
    t = jnp.einsum("jgpn,gh->jgnhp", cc.reshape(G // gpb, gpb, P, N), eye)
    return t.reshape(G // gpb, gpb * N, gpb * P)


def _tr(ws):
    return [jnp.swapaxes(w, 1, 2) for w in ws]


WEIGHTS = ['c_ctx', 'w_mod', 'b_mod', 'norm1', 'norm2', 'w_in', 's5_a_re', 's5_a_im', 's5_log_dt', 's5_b_re', 's5_b_im',
           's5_c_re', 's5_c_im', 's5_d', 'w_glu', 'q_norm', 'kv_norm', 'w_uq', 'w_ukv', 'w_mla_o', 'w_out', 'w_ffn_in',
           'w_ffn_out', 'norm_f']
COL_SHARDED = ['w_in', 'w_glu', 'w_uq', 'w_ukv', 'w_mla_o', 'w_ffn_in']
ROW_SHARDED = ['w_out', 'w_ffn_out']
SMALL = ['norm1', 'norm2', 's5_a_re', 's5_a_im', 's5_log_dt', 's5_b_re', 's5_b_im', 's5_c_re', 's5_c_im', 's5_d',
         'q_norm', 'kv_norm', 'norm_f']


def _pad_rows(a, rows):
    return jnp.concatenate([a, jnp.zeros((rows - a.shape[0],) + a.shape[1:], a.dtype)], axis=0)


def _pack(vals, width, rows):
    flat = jnp.concatenate([v.reshape(-1).astype(F32) for v in vals])
    flat = jnp.concatenate([flat, jnp.zeros((rows * width - flat.shape[0],), F32)])
    return flat.reshape(rows, width)


def _unpack(buf, like):
    flat = buf.reshape(-1)
    out, pos = [], 0
    for v in like:
        out.append(flat[pos:pos + v.size].reshape(v.shape))
        pos += v.size
    return out


def _step(x, c, ctx, loss_target, w, m, v):
    px, py, pc = _place()
    me = 4 * px + 2 * py + pc
    me_chip = 2 * px + py
    L, D = x.shape[1], x.shape[2]
    Lc = ctx.shape[1]
    T = L + Lc
    SW = D // 2
    G = SW // S5_GROUP
    C = G * S5_STATE
    H = MLA_HEADS
    q_rank = w['q_norm'].shape[1]
    kv_rank = w['kv_norm'].shape[1]
    d_ff = w['w_ffn_out'].shape[1] * 4
    wa_used = SW + q_rank + kv_rank + QK_ROPE
    WA = -(-(SW + q_rank + kv_rank + LANES) // 512) * 512

    c_rows = _pad_rows(c.astype(F32), SUBLANES)
    c_all = _allgather8(c_rows, name="ag_cond")[:, 0, :]
    cond = jnp.concatenate([c_all, w['c_ctx'].reshape(1, D)], axis=0)
    cond = _pad_rows(cond, 16)
    (act,) = _rw(lambda t: (jax.nn.silu(t),), [cond], [], [F32], name="cond_silu")
    w_mod, cs_mod = w['w_mod'][0], w['w_mod'].shape[2]
    mod_part = _mm(act, w_mod, out_dtype=F32, name="mod_fwd")
    mod_all = _allgather8(mod_part, name="ag_mod")
    mod_full = jnp.concatenate([mod_all[0], mod_all[2], mod_all[4], mod_all[6]], axis=1) + w['b_mod']
    m_lat = lax.dynamic_slice_in_dim(mod_full, me, 1, axis=0).reshape(6, D)
    m_ctx = mod_full[8].reshape(6, D)
    sh1, sc1, g1, sh2, sc2, g2 = (m_lat[i:i + 1] for i in range(6))
    csh1, csc1 = m_ctx[0:1], m_ctx[1:2]

    names = COL_SHARDED + ROW_SHARDED
    shards_bf = []
    for nme in names:
        (s,) = _rw(lambda t: (t,), [w[nme][0]], [], [BF16], name=f"cast_{nme}")
        shards_bf.append(s)
    gathered = dict(zip(names, _allgather_shards(shards_bf, name="ag_weights")))
    w_in = _from_col_blocks(gathered['w_in'])
    w_glu = _from_col_blocks(gathered['w_glu'])
    w_uq = _from_col_blocks(gathered['w_uq'])
    w_ukv = _from_col_blocks(gathered['w_ukv'])
    w_mla_o = _from_col_blocks(gathered['w_mla_o'])
    w_ffn_in = _from_col_blocks(gathered['w_ffn_in'])
    w_out = gathered['w_out'].reshape(D, D)
    w_ffn_out = gathered['w_ffn_out'].reshape(d_ff, D)
    w_a = jnp.concatenate([w_in[:, :wa_used], jnp.zeros((D, WA - wa_used), BF16)], axis=1)
    w_g = w_in[:, wa_used:]
    uq3 = w_uq.reshape(q_rank, H, QK_NOPE + QK_ROPE)
    w_q2 = jnp.concatenate([
        uq3[:, :, :QK_NOPE].reshape(q_rank, H * QK_NOPE),
        jnp.concatenate([uq3[:, :, QK_NOPE:], jnp.zeros((q_rank, H, LANES - QK_ROPE), BF16)], axis=2).reshape(q_rank, H * LANES),
    ], axis=1)

    xs = _to_segments(x[0])
    cs = _to_segments(ctx[0])
    tgt = _to_segments(loss_target[0])
    cos, sin = _rope_tables(L)
    n1, n2, nf = w['norm1'], w['norm2'], w['norm_f'].reshape(1, D)
    qg, kvg = w['q_norm'], w['kv_norm']

    (xn_lat,) = _rw(_f_norm_mod, [xs], [n1, sc1, sh1], [BF16], name="norm1_lat")
    (xn_ctx,) = _rw(_f_norm_mod, [cs], [n1, csc1, csh1], [BF16], name="norm1_ctx")
    xn = jnp.concatenate([xn_lat, xn_ctx], axis=0)
    ha = _mm(xn, w_a, out_dtype=F32, name="in_proj")
    ha_lat, ha_ctx = ha[:L], ha[L:]
    gt = _mm(xn_lat, w_g, out_dtype=F32, name="in_gates")
    f_post_lat = _make_f_post_in(SW, q_rank, kv_rank, True)
    f_post_ctx = _make_f_post_in(SW, q_rank, kv_rank, False)
    u_lat, cqn, ckvn_lat, kr_lat = _rw(f_post_lat, [ha_lat, cos, sin], [qg, kvg], [F32, BF16, BF16, BF16], name="post_in_lat")
    u_ctx, ckvn_ctx, kr_ctx = _rw(f_post_ctx, [ha_ctx], [kvg], [F32, BF16, BF16], name="post_in_ctx")

    gpb = min(S5_BLOCK_GROUPS, G)
    gpo = min(8, G)
    d_skip = w['s5_d'][0].reshape(1, SW)
    disc, vjp_disc, w_b, w_c = [], [], [], []
    for d in range(2):
        prm = (w['s5_a_re'][0, d], w['s5_a_im'][0, d], w['s5_log_dt'][0, d], w['s5_b_re'][0, d], w['s5_b_im'][0, d])

        def prep(a_re, a_im, log_dt, b_re, b_im):
            ab_re, ab_im, bb_re, bb_im = _s5_discretize(a_re, a_im, log_dt, b_re, b_im)
            return ab_re.reshape(1, C), ab_im.reshape(1, C), _diag_blocks_in(bb_re, gpb), _diag_blocks_in(bb_im, gpb)

        out, vj = jax.vjp(prep, *prm)
        disc.append(out)
        vjp_disc.append(vj)
        w_b += [out[2], out[3]]
        w_c += [_diag_blocks_out(w['s5_c_re'][0, d], gpo), -_diag_blocks_out(w['s5_c_im'][0, d], gpo)]
    nb_in = G // gpb
    nb_out = G // gpo
    bu_lat = _bd_fanout(u_lat, w_b, name="s5_bu_lat")
    bu_ctx = _bd_fanout(u_ctx, w_b, name="s5_bu_ctx")
    zero = jnp.zeros((1, C), F32)
    h_lat, h_ctx, hT_ctx = [], [], []
    for d, rev in enumerate((False, True)):
        lr, li = disc[d][0], disc[d][1]
        hcr, hci, tr, ti = _s5_scan(bu_ctx[2 * d], bu_ctx[2 * d + 1], lr, li, zero, zero, zero, zero, reverse=rev,
                                    name=f"s5_scan_ctx_{d}")
        hlr, hli, _, _ = _s5_scan(bu_lat[2 * d], bu_lat[2 * d + 1], lr, li, tr, ti, zero, zero, reverse=rev,
                                  name=f"s5_scan_lat_{d}")
        h_ctx += [hcr, hci]
        h_lat += [hlr, hli]
        hT_ctx += [tr, ti]
    r5 = _bd_fanin(h_lat, w_c, name="s5_readout")
    (z,) = _rw(_f_s5post, [u_lat, r5], [d_skip], [BF16], name="s5_post")

    q2 = _mm(cqn, w_q2, out_dtype=F32, name="q_up")
    (qq,) = _rw(_f_qpost, [q2, cos, sin], [], [BF16], name="q_rope")
    kvn = jnp.concatenate([ckvn_lat, ckvn_ctx], axis=0)
    kr_all = jnp.concatenate([kr_lat, kr_ctx], axis=0)
    kv = _mm(kvn, w_ukv, out_dtype=BF16, name="kv_up")
    o = _attn_fwd(qq, kv, kr_all, name="attn_fwd")

    ab = _mm(z, w_glu, out_dtype=F32, name="glu_proj")
    bm = _mm(o, w_mla_o, out_dtype=F32, name="mla_out")
    (mix,) = _rw(_f_merge, [ab, bm, gt], [], [BF16], name="merge")
    out1 = _mm(mix, w_out, out_dtype=F32, name="out_proj")
    x1, xn2 = _rw(_f_resid_norm, [xs, out1], [g1, n2, sc2, sh2], [F32, BF16], name="resid_norm2")
    ab2 = _mm(xn2, w_ffn_in, out_dtype=F32, name="ffn_in")
    (hmid,) = _rw(_f_swiglu, [ab2], [], [BF16], name="ffn_act")
    f2 = _mm(hmid, w_ffn_out, out_dtype=F32, name="ffn_out")
    (row_loss,) = _rw(_f_final, [x1, f2, tgt], [g2, nf], [F32], name="final_loss")
    loss = lax.psum(jnp.sum(row_loss), ("x", "y", "c"))

    ones = jnp.ones((L, 1), F32)
    (dx1_a, df2), (dg2, dnf) = _rw_vjp(_f_final, [x1, f2, tgt], [g2, nf], [[ones]], [True, True, False], [True, True],
                                       [F32, BF16], name="final_loss_bwd")
    dhmid = _mm(df2, w_ffn_out, tb=True, out_dtype=F32, name="ffn_out_dx")
    gw_ffn_out = _mm(hmid, df2, ta=True, out_dtype=BF16, name="ffn_out_dw")
    (dab2,), _ = _rw_vjp(_f_swiglu, [ab2], [], [[dhmid]], [True], [], [BF16], name="ffn_act_bwd")
    dxn2 = _mm(dab2, w_ffn_in, tb=True, out_dtype=F32, name="ffn_in_dx")
    gw_ffn_in = _mm(xn2, dab2, ta=True, out_dtype=BF16, name="ffn_in_dw")
    (dx_a, dout1), (dg1, dn2, dsc2, dsh2) = _rw_vjp(
        _f_resid_norm, [xs, out1], [g1, n2, sc2, sh2], [[dx1_a], [dxn2]], [True, True], [True] * 4, [F32, BF16],
        name="resid_norm2_bwd")
    dmix = _mm(dout1, w_out, tb=True, out_dtype=F32, name="out_proj_dx")
    gw_out = _mm(mix, dout1, ta=True, out_dtype=BF16, name="out_proj_dw")
    (dab, dbm, dgt), _ = _rw_vjp(_f_merge, [ab, bm, gt], [], [[dmix]], [True] * 3, [], [BF16] * 3, name="merge_bwd")
    dz = _mm(dab, w_glu, tb=True, out_dtype=F32, name="glu_proj_dx")
    gw_glu = _mm(z, dab, ta=True, out_dtype=BF16, name="glu_proj_dw")
    do = _mm(dbm, w_mla_o, tb=True, out_dtype=BF16, name="mla_out_dx")
    gw_mla_o = _mm(o, dbm, ta=True, out_dtype=BF16, name="mla_out_dw")
    dxn_g = _mm(dgt, w_g, tb=True, out_dtype=F32, name="in_gates_dx")
    gw_g = _mm(xn_lat, dgt, ta=True, out_dtype=BF16, name="in_gates_dw")

    (du_a, dr5), (dd_skip,) = _rw_vjp(_f_s5post, [u_lat, r5], [d_skip], [[dz]], [True, True], [True], [F32, F32],
                                      name="s5_post_bwd")
    dh_lat = _bd_fanout(dr5, _tr(w_c), name="s5_readout_dx")
    dw_c = _bd_dw(h_lat, [dr5] * 4, nb_out, name="s5_readout_dw")
    zeros_ctx = jnp.zeros((Lc, C), F32)
    mu_lat, mu_ctx, dlam = [], [], []
    for d, rev in enumerate((False, True)):
        lr, li = disc[d][0], disc[d][1]
        mlr, mli, fr, fi = _s5_scan(dh_lat[2 * d], dh_lat[2 * d + 1], lr, -li, zero, zero, zero, zero, reverse=not rev,
                                    name=f"s5_adj_lat_{d}")
        dh0r, dh0i = _cmul(lr, -li, fr, fi)
        mcr, mci, _, _ = _s5_scan(zeros_ctx, zeros_ctx, lr, -li, zero, zero, dh0r, dh0i, reverse=not rev,
                                  name=f"s5_adj_ctx_{d}")
        dl_lat = _s5_dlam(mlr, mli, h_lat[2 * d], h_lat[2 * d + 1], hT_ctx[2 * d], hT_ctx[2 * d + 1], reverse=rev,
                          name=f"s5_dlam_lat_{d}")
        dl_ctx = _s5_dlam(mcr, mci, h_ctx[2 * d], h_ctx[2 * d + 1], zero, zero, reverse=rev, name=f"s5_dlam_ctx_{d}")
        mu_lat += [mlr, mli]
        mu_ctx += [mcr, mci]
        dlam.append((dl_lat[0] + dl_ctx[0], dl_lat[1] + dl_ctx[1]))
    du_b = _bd_fanin(mu_lat, _tr(w_b), name="s5_bu_lat_dx")
    du_ctx = _bd_fanin(mu_ctx, _tr(w_b), name="s5_bu_ctx_dx")
    dw_b_lat = _bd_dw([u_lat] * 4, mu_lat, nb_in, name="s5_bu_lat_dw")
    dw_b_ctx = _bd_dw([u_ctx] * 4, mu_ctx, nb_in, name="s5_bu_ctx_dw")
    g_s5 = {}
    for d in range(2):
        ct = (dlam[d][0], dlam[d][1], dw_b_lat[2 * d] + dw_b_ctx[2 * d], dw_b_lat[2 * d + 1] + dw_b_ctx[2 * d + 1])
        ga_re, ga_im, gdt, gb_re, gb_im = vjp_disc[d](ct)
        _, vj_c = jax.vjp(lambda cr, ci: (_diag_blocks_out(cr, gpo), -_diag_blocks_out(ci, gpo)),
                          w['s5_c_re'][0, d], w['s5_c_im'][0, d])
        gc_re, gc_im = vj_c((dw_c[2 * d], dw_c[2 * d + 1]))
        for nme, val in (('s5_a_re', ga_re), ('s5_a_im', ga_im), ('s5_log_dt', gdt), ('s5_b_re', gb_re),
                         ('s5_b_im', gb_im), ('s5_c_re', gc_re), ('s5_c_im', gc_im)):
            g_s5.setdefault(nme, []).append(val)
    g_small = {nme: jnp.stack(vals)[None] for nme, vals in g_s5.items()}
    g_small['s5_d'] = dd_skip.reshape(w['s5_d'].shape)

    dqn, dqr, dkn, dv, dkr = _attn_bwd(qq, kv, kr_all, do, name="attn_bwd")
    dqq = jnp.concatenate([dqn, dqr], axis=1)
    (dq2,), _ = _rw_vjp(_f_qpost, [q2, cos, sin], [], [[dqq]], [True, False, False], [], [BF16], name="q_rope_bwd")
    dcqn = _mm(dq2, w_q2, tb=True, out_dtype=F32, name="q_up_dx")
    gw_q2 = _mm(cqn, dq2, ta=True, out_dtype=BF16, name="q_up_dw")
    dkv = jnp.stack([dkn.reshape(T, H, LANES), dv.reshape(T, H, LANES)], axis=2).reshape(T, 2 * H * LANES)
    dckvn = _mm(dkv, w_ukv, tb=True, out_dtype=F32, name="kv_up_dx")
    gw_ukv = _mm(kvn, dkv, ta=True, out_dtype=BF16, name="kv_up_dw")

    (dha_lat,), (dqg, dkvg_lat) = _rw_vjp(
        f_post_lat, [ha_lat, cos, sin], [qg, kvg], [[du_a, du_b], [dcqn], [dckvn[:L]], [dkr[:L]]], [True, False, False],
        [True, True], [BF16], name="post_in_lat_bwd")
    (dha_ctx,), (dkvg_ctx,) = _rw_vjp(f_post_ctx, [ha_ctx], [kvg], [[du_ctx], [dckvn[L:]], [dkr[L:]]], [True], [True],
                                      [BF16], name="post_in_ctx_bwd")
    dha = jnp.concatenate([dha_lat, dha_ctx], axis=0)
    dxn = _mm(dha, w_a, tb=True, out_dtype=F32, name="in_proj_dx")
    gw_a = _mm(xn, dha, ta=True, out_dtype=BF16, name="in_proj_dw")
    (dx_seg,), (dn1_lat, dsc1, dsh1) = _rw_vjp(
        _f_norm_mod_keep, [xs], [n1, sc1, sh1], [[dxn[:L], dxn_g], [dx_a]], [True], [True] * 3, [F32], name="norm1_lat_bwd")
    _, (dn1_ctx, dcsc1, dcsh1) = _rw_vjp(_f_norm_mod, [cs], [n1, csc1, csh1], [[dxn[L:]]], [False], [True] * 3, [],
                                         name="norm1_ctx_bwd")
    grad_x = _from_segments(dx_seg)[None]
    g_small.update(norm1=dn1_lat + dn1_ctx, norm2=dn2, q_norm=dqg, kv_norm=dkvg_lat + dkvg_ctx, norm_f=dnf.reshape(D))

    zD = jnp.zeros((1, D), F32)
    dm = jnp.concatenate([
        jnp.concatenate([dsh1, dsc1, dg1, dsh2, dsc2, dg2], axis=1),
        jnp.concatenate([dcsh1, dcsc1, zD, zD, zD, zD], axis=1),
    ], axis=0)
    dm_all = _allgather8(_pad_rows(dm, SUBLANES), name="ag_dmod")
    dm_ctx = dm_all[0, 1]
    for k in range(1, 8):
        dm_ctx = dm_ctx + dm_all[k, 1]
    dmod = _pad_rows(jnp.concatenate([dm_all[:, 0, :], dm_ctx[None]], axis=0), 16)
    g_b_mod = jnp.sum(dmod, axis=0, keepdims=True)
    dmod_mine = lax.dynamic_slice_in_dim(dmod, me_chip * cs_mod, cs_mod, axis=1)
    g_w_mod = _mm(act, dmod_mine, ta=True, out_dtype=F32, name="mod_dw")
    dact_part = _mm(dmod_mine, w_mod, tb=True, out_dtype=F32, name="mod_dx")
    dact_all = _allgather8(dact_part, name="ag_dact")
    dact = dact_all[0] + dact_all[2] + dact_all[4] + dact_all[6]
    (dcond_rows,), _ = _rw_vjp(lambda t: (jax.nn.silu(t),), [cond], [], [[dact]], [True], [], [F32], name="cond_silu_bwd")
    g_c_ctx = dcond_rows[8]

    uq_nope = gw_q2[:, :H * QK_NOPE].reshape(q_rank, H, QK_NOPE)
    uq_rope = gw_q2[:, H * QK_NOPE:].reshape(q_rank, H, LANES)[:, :, :QK_ROPE]
    gw_uq = jnp.concatenate([uq_nope, uq_rope], axis=2).reshape(q_rank, H * (QK_NOPE + QK_ROPE))
    gw_in = jnp.concatenate([gw_a[:, :wa_used], gw_g], axis=1)
    full_grads = {'w_in': gw_in, 'w_glu': gw_glu, 'w_uq': gw_uq, 'w_ukv': gw_ukv, 'w_mla_o': gw_mla_o,
                  'w_ffn_in': gw_ffn_in, 'w_out': gw_out, 'w_ffn_out': gw_ffn_out}
    blocks = [_col_blocks(full_grads[nme], 4) for nme in COL_SHARDED]
    blocks += [full_grads[nme].reshape(4, -1, D) for nme in ROW_SHARDED]
    small_vals = [g_small[nme] for nme in SMALL]
    n_small = sum(val.size for val in small_vals)
    small_rows = -(-n_small // (LANES * 4 * 32)) * 32
    blocks.append(_pack(small_vals, LANES, 4 * small_rows).reshape(4, small_rows, LANES))
    reduced = _reduce_scatter(blocks)
    small_mine = reduced[-1]
    small_all = _allgather_shards([small_mine], name="ag_small_grads")[0].reshape(4 * small_rows, LANES)
    g_small_red = dict(zip(SMALL, _unpack(small_all, [w[nme] for nme in SMALL])))

    grads, delta, new_m, new_v = {}, {}, {}, {}
    for nme, red in zip(names, reduced[:-1]):
        shp = w[nme].shape
        res = _adamw(w[nme][0], red, m[nme][0], v[nme][0], name=f"adamw_{nme}")
        grads[nme], delta[nme], new_m[nme], new_v[nme] = (r.reshape(shp) for r in res)
    res = _adamw(w['w_mod'][0], g_w_mod, m['w_mod'][0], v['w_mod'][0], name="adamw_w_mod")
    grads['w_mod'], delta['w_mod'], new_m['w_mod'], new_v['w_mod'] = (r.reshape(w['w_mod'].shape) for r in res)
    rest = SMALL + ['c_ctx', 'b_mod']
    g_rest = dict(g_small_red, c_ctx=g_c_ctx, b_mod=g_b_mod)
    rows_rest = -(-sum(w[nme].size for nme in rest) // (LANES * 16)) * 16
    packed = [_pack([src[nme] for nme in rest], LANES, rows_rest) for src in (w, g_rest, m, v)]
    res = _adamw(*packed, name="adamw_small")
    for dst, buf in zip((grads, delta, new_m, new_v), res):
        dst.update(zip(rest, _unpack(buf, [w[nme] for nme in rest])))
    return (loss, grad_x, *[grads[nme] for nme in WEIGHTS], *[delta[nme] for nme in WEIGHTS],
            *[new_m[nme] for nme in WEIGHTS], *[new_v[nme] for nme in WEIGHTS])


def kernel(x, c, ctx, c_ctx, w_mod, b_mod, norm1, norm2, w_in, s5_a_re, s5_a_im, s5_log_dt, s5_b_re, s5_b_im, s5_c_re, s5_c_im, s5_d, w_glu, q_norm, kv_norm, w_uq, w_ukv, w_mla_o, w_out, w_ffn_in, w_ffn_out, norm_f, loss_target, m_c_ctx, m_w_mod, m_b_mod, m_norm1, m_norm2, m_w_in, m_s5_a_re, m_s5_a_im, m_s5_log_dt, m_s5_b_re, m_s5_b_im, m_s5_c_re, m_s5_c_im, m_s5_d, m_w_glu, m_q_norm, m_kv_norm, m_w_uq, m_w_ukv, m_w_mla_o, m_w_out, m_w_ffn_in, m_w_ffn_out, m_norm_f, v_c_ctx, v_w_mod, v_b_mod, v_norm1, v_norm2, v_w_in, v_s5_a_re, v_s5_a_im, v_s5_log_dt, v_s5_b_re, v_s5_b_im, v_s5_c_re, v_s5_c_im, v_s5_d, v_w_glu, v_q_norm, v_kv_norm, v_w_uq, v_w_ukv, v_w_mla_o, v_w_out, v_w_ffn_in, v_w_ffn_out, v_norm_f):
    w = dict(c_ctx=c_ctx, w_mod=w_mod, b_mod=b_mod, norm1=norm1, norm2=norm2, w_in=w_in, s5_a_re=s5_a_re, s5_a_im=s5_a_im,
             s5_log_dt=s5_log_dt, s5_b_re=s5_b_re, s5_b_im=s5_b_im, s5_c_re=s5_c_re, s5_c_im=s5_c_im, s5_d=s5_d, w_glu=w_glu,
             q_norm=q_norm, kv_norm=kv_norm, w_uq=w_uq, w_ukv=w_ukv, w_mla_o=w_mla_o, w_out=w_out, w_ffn_in=w_ffn_in,
             w_ffn_out=w_ffn_out, norm_f=norm_f)
    m = dict(c_ctx=m_c_ctx, w_mod=m_w_mod, b_mod=m_b_mod, norm1=m_norm1, norm2=m_norm2, w_in=m_w_in, s5_a_re=m_s5_a_re,
             s5_a_im=m_s5_a_im, s5_log_dt=m_s5_log_dt, s5_b_re=m_s5_b_re, s5_b_im=m_s5_b_im, s5_c_re=m_s5_c_re,
             s5_c_im=m_s5_c_im, s5_d=m_s5_d, w_glu=m_w_glu, q_norm=m_q_norm, kv_norm=m_kv_norm, w_uq=m_w_uq, w_ukv=m_w_ukv,
             w_mla_o=m_w_mla_o, w_out=m_w_out, w_ffn_in=m_w_ffn_in, w_ffn_out=m_w_ffn_out, norm_f=m_norm_f)
    v = dict(c_ctx=v_c_ctx, w_mod=v_w_mod, b_mod=v_b_mod, norm1=v_norm1, norm2=v_norm2, w_in=v_w_in, s5_a_re=v_s5_a_re,
             s5_a_im=v_s5_a_im, s5_log_dt=v_s5_log_dt, s5_b_re=v_s5_b_re, s5_b_im=v_s5_b_im, s5_c_re=v_s5_c_re,
             s5_c_im=v_s5_c_im, s5_d=v_s5_d, w_glu=v_w_glu, q_norm=v_q_norm, kv_norm=v_kv_norm, w_uq=v_w_uq, w_ukv=v_w_ukv,
             w_mla_o=v_w_mla_o, w_out=v_w_out, w_ffn_in=v_w_ffn_in, w_ffn_out=v_w_ffn_out, norm_f=v_norm_f)
    return _step(x, c, ctx, loss_target, w, m, v)
```

```python
import functools
import math

import jax
import jax.numpy as jnp
from jax import lax
from jax.experimental import pallas as pl
from jax.experimental.pallas import tpu as pltpu

F32 = jnp.float32
BF16 = jnp.bfloat16

EPS = 1e-6
GRID_W = 64
S5_GROUP = 16
S5_STATE = 64
MLA_HEADS = 8
QK_NOPE = 128
QK_ROPE = 64
V_DIM = 128
ROPE_BASE = 10000.0
ATTN_SCALE = (QK_NOPE + QK_ROPE) ** -0.5
ADAM_LR = 0.001
ADAM_B1 = 0.9
ADAM_B2 = 0.999
ADAM_EPS = 1e-08
ADAM_WD = 0.01
ADAM_STEP = 10

SUBLANES = 8
LANES = 128
V7X_VMEM_BYTES = 64 * 1024 * 1024
VMEM_LIMIT = (V7X_VMEM_BYTES * 7) // 8
N_SEG = SUBLANES
S5_BLOCK_GROUPS = 16
MESH = pl.DeviceIdType.MESH


def _pick(n, target, mult):
    best = None
    d = mult
    while d <= min(n, target):
        if n % d == 0:
            best = d
        d += mult
    return n if best is None else best


def _cparams(sem=None):
    return pltpu.CompilerParams(dimension_semantics=sem, vmem_limit_bytes=VMEM_LIMIT)


MM_VMEM_BUDGET = (V7X_VMEM_BYTES * 5) // 8


def _mm(a, b, *, ta=False, tb=False, out_dtype=F32, name, b_shards=1, out_shards=1):
    if ta:
        K, M = a.shape
    else:
        M, K = a.shape
    if tb:
        N, K2 = b.shape[-2], b.shape[-1] * b_shards
    else:
        K2, N = b.shape[-2], b.shape[-1] * b_shards
    assert K == K2, (a.shape, b.shape, ta, tb)
    n_unit = N // max(out_shards, 1 if tb else b_shards)
    k_unit = K // (b_shards if tb else 1)
    tn = _pick(n_unit, 1024, LANES)
    tm = _pick(M, 1024 if tn >= 512 else 2048, LANES if ta else 16)
    sa, sb, so = a.dtype.itemsize, b.dtype.itemsize, jnp.dtype(out_dtype).itemsize
    k_mult = LANES if (not ta or tb) else 16
    tk = k_mult if k_unit % k_mult == 0 else k_unit
    for cand in range(k_mult, k_unit + 1, k_mult):
        if k_unit % cand == 0 and 2 * cand * (tm * sa + tn * sb) + tm * tn * (4 + 2 * so) <= MM_VMEM_BUDGET:
            tk = cand
    nk = K // tk
    dims = (((0 if ta else 1,), (1 if tb else 0,)), ((), ()))

    def body(a_ref, b_ref, o_ref, *scratch):
        part = lax.dot_general(a_ref[...].astype(BF16), b_ref[...].astype(BF16), dims, preferred_element_type=F32)
        if nk == 1:
            o_ref[...] = part.astype(o_ref.dtype)
            return
        acc_ref, = scratch
        k = pl.program_id(2)

        @pl.when(k == 0)
        def _():
            acc_ref[...] = part

        @pl.when(k > 0)
        def _():
            acc_ref[...] += part

        @pl.when(k == nk - 1)
        def _():
            o_ref[...] = acc_ref[...].astype(o_ref.dtype)

    a_spec = pl.BlockSpec((tk, tm), lambda i, j, k: (k, i)) if ta else pl.BlockSpec((tm, tk), lambda i, j, k: (i, k))
    if b_shards == 1:
        b_spec = pl.BlockSpec((tn, tk), lambda i, j, k: (j, k)) if tb else pl.BlockSpec((tk, tn), lambda i, j, k: (k, j))
    elif tb:
        kpb = k_unit // tk
        b_spec = pl.BlockSpec((None, tn, tk), lambda i, j, k: (k // kpb, j, k % kpb))
    else:
        npb = n_unit // tn
        b_spec = pl.BlockSpec((None, tk, tn), lambda i, j, k: (j // npb, k, j % npb))
    if out_shards == 1:
        out_spec = pl.BlockSpec((tm, tn), lambda i, j, k: (i, j))
        out_shape = jax.ShapeDtypeStruct((M, N), out_dtype)
    else:
        opb = n_unit // tn
        out_spec = pl.BlockSpec((None, tm, tn), lambda i, j, k: (j // opb, i, j % opb))
        out_shape = jax.ShapeDtypeStruct((out_shards, M, N // out_shards), out_dtype)
    return pl.pallas_call(
        body, name=name, grid=(M // tm, N // tn, nk),
        in_specs=[a_spec, b_spec], out_specs=out_spec, out_shape=out_shape,
        scratch_shapes=[pltpu.VMEM((tm, tn), F32)] if nk > 1 else [],
        compiler_params=_cparams(("parallel", "parallel", "arbitrary")),
    )(a, b)


def _row_tile(tiled, extra_bytes=0):
    rows = tiled[0].shape[0]
    per_row = sum(a.shape[1] * 4 for a in tiled) + extra_bytes
    target = max(SUBLANES, (6 * 1024 * 1024) // max(per_row, 1))
    return _pick(rows, min(target, 512), 16)


def _rw(f, tiled, bcast, out_dtypes, *, name):
    nt, nb = len(tiled), len(bcast)
    rows = tiled[0].shape[0]
    outs_aval = jax.eval_shape(f, *[jax.ShapeDtypeStruct((16, a.shape[1]), F32) for a in tiled],
                               *[jax.ShapeDtypeStruct(b.shape, F32) for b in bcast])
    widths = [o.shape[1] for o in outs_aval]
    tm = _row_tile(tiled, sum(w * 4 for w in widths))

    def body(*refs):
        tin = [r[...].astype(F32) for r in refs[:nt]]
        bin_ = [r[...].astype(F32) for r in refs[nt:nt + nb]]
        outs = f(*tin, *bin_)
        for o_ref, o in zip(refs[nt + nb:], outs):
            o_ref[...] = o.astype(o_ref.dtype)

    in_specs = [pl.BlockSpec((tm, a.shape[1]), lambda i: (i, 0)) for a in tiled]
    in_specs += [pl.BlockSpec(b.shape, lambda i: (0, 0)) for b in bcast]
    res = pl.pallas_call(
        body, name=name, grid=(rows // tm,), in_specs=in_specs,
        out_specs=[pl.BlockSpec((tm, w), lambda i: (i, 0)) for w in widths],
        out_shape=[jax.ShapeDtypeStruct((rows, w), dt) for w, dt in zip(widths, out_dtypes)],
        compiler_params=_cparams(("parallel",)),
    )(*tiled, *bcast)
    return list(res)


def _rw_vjp(f, tiled, bcast, cts, need_t, need_b, t_dtypes, *, name):
    nt, nb = len(tiled), len(bcast)
    rows = tiled[0].shape[0]
    flat_cts = [c for group in cts for c in group]
    t_idx = [i for i in range(nt) if need_t[i]]
    b_idx = [i for i in range(nb) if need_b[i]]
    tm = _row_tile(list(tiled) + flat_cts, sum(tiled[i].shape[1] * 4 for i in t_idx))
    nc = len(flat_cts)

    def body(*refs):
        i = pl.program_id(0)
        tin = [r[...].astype(F32) for r in refs[:nt]]
        bin_ = [r[...].astype(F32) for r in refs[nt:nt + nb]]
        ct_refs = refs[nt + nb:nt + nb + nc]
        out_refs = refs[nt + nb + nc:]
        outs, vjp_fn = jax.vjp(f, *tin, *bin_)
        ct_vals, pos = [], 0
        for o, group in zip(outs, cts):
            acc = jnp.zeros_like(o)
            for _ in group:
                acc = acc + ct_refs[pos][...].astype(F32)
                pos += 1
            ct_vals.append(acc)
        grads = vjp_fn(tuple(ct_vals))
        for o_ref, k in zip(out_refs[:len(t_idx)], t_idx):
            o_ref[...] = grads[k].astype(o_ref.dtype)
        for o_ref, k in zip(out_refs[len(t_idx):], b_idx):
            @pl.when(i == 0)
            def _(o_ref=o_ref):
                o_ref[...] = jnp.zeros_like(o_ref)

            o_ref[...] += grads[nt + k]

    in_specs = [pl.BlockSpec((tm, a.shape[1]), lambda i: (i, 0)) for a in tiled]
    in_specs += [pl.BlockSpec(b.shape, lambda i: (0, 0)) for b in bcast]
    in_specs += [pl.BlockSpec((tm, c.shape[1]), lambda i: (i, 0)) for c in flat_cts]
    out_specs = [pl.BlockSpec((tm, tiled[k].shape[1]), lambda i: (i, 0)) for k in t_idx]
    out_specs += [pl.BlockSpec(bcast[k].shape, lambda i: (0, 0)) for k in b_idx]
    out_shape = [jax.ShapeDtypeStruct(tiled[k].shape, dt) for k, dt in zip(t_idx, t_dtypes)]
    out_shape += [jax.ShapeDtypeStruct(bcast[k].shape, F32) for k in b_idx]
    res = pl.pallas_call(
        body, name=name, grid=(rows // tm,), in_specs=in_specs, out_specs=out_specs, out_shape=out_shape,
        compiler_params=_cparams(("arbitrary",)),
    )(*tiled, *bcast, *flat_cts)
    res = list(res)
    return res[:len(t_idx)], res[len(t_idx):]


def _rms(x, g):
    return x * lax.rsqrt(jnp.mean(x * x, axis=-1, keepdims=True) + EPS) * g


def _f_norm_mod(x, g, sc, sh):
    return (_rms(x, g) * (1.0 + sc) + sh,)


def _f_norm_mod_keep(x, g, sc, sh):
    return (_rms(x, g) * (1.0 + sc) + sh, x)


@jax.custom_vjp
def _swap16(x):
    w = x.shape[-1]
    lane = lax.broadcasted_iota(jnp.int32, x.shape, x.ndim - 1)
    return jnp.where((lane & 16) == 0, pltpu.roll(x, w - 16, x.ndim - 1), pltpu.roll(x, 16, x.ndim - 1))


_swap16.defvjp(lambda x: (_swap16(x), None), lambda _, g: (_swap16(g),))


def _rope(x, cos, sin):
    return x * cos + _swap16(x) * sin


def _make_f_post_in(sw, q_rank, kv_rank, with_q):
    o1, o2, o3 = sw, sw + q_rank, sw + q_rank + kv_rank

    if with_q:
        def f(ha, cos, sin, qg, kvg):
            u = ha[:, :o1]
            cqn = _rms(ha[:, o1:o2], qg)
            ckvn = _rms(ha[:, o2:o3], kvg)
            kr = _rope(ha[:, o3:o3 + LANES], cos, sin)
            return u, cqn, ckvn, kr
    else:
        def f(ha, kvg):
            return ha[:, :o1], _rms(ha[:, o2:o3], kvg), ha[:, o3:o3 + LANES]
    return f


def _f_qpost(q2, cos, sin):
    w = q2.shape[1] // 2
    reps = w // LANES
    qr = _rope(q2[:, w:], jnp.tile(cos, (1, reps)), jnp.tile(sin, (1, reps)))
    return (jnp.concatenate([q2[:, :w], qr], axis=1),)


def _f_s5post(u, r, d):
    return (jax.nn.gelu(d * u + r, approximate=True),)


def _f_merge(ab, bm, gt):
    d = bm.shape[1]
    br_s5 = ab[:, :d] * jax.nn.sigmoid(ab[:, d:])
    g = jax.nn.sigmoid(gt)
    return (g[:, :d] * br_s5 + g[:, d:] * bm,)


def _f_resid_norm(x, out, g1, n2, sc2, sh2):
    x1 = x + g1 * out
    return x1, _rms(x1, n2) * (1.0 + sc2) + sh2


def _f_swiglu(ab):
    d = ab.shape[1] // 2
    return (jax.nn.silu(ab[:, :d]) * ab[:, d:],)


def _f_final(x1, f, tgt, g2, nf):
    y = _rms(x1 + g2 * f, nf)
    return (0.5 * jnp.mean(jnp.square(y - tgt), axis=-1, keepdims=True),)


def _bd_fanout(x, ws, *, name):
    nw = len(ws)
    nb, kb, nn = ws[0].shape
    T = x.shape[0]
    tm = _pick(T, 512, 16)

    def body(*refs):
        xb = refs[0][...].astype(BF16)
        for w_ref, o_ref in zip(refs[1:1 + nw], refs[1 + nw:]):
            o_ref[...] = jnp.dot(xb, w_ref[0].astype(BF16), preferred_element_type=F32)

    return list(pl.pallas_call(
        body, name=name, grid=(nb, T // tm),
        in_specs=[pl.BlockSpec((tm, kb), lambda j, i: (i, j))] + [pl.BlockSpec((1, kb, nn), lambda j, i: (j, 0, 0))] * nw,
        out_specs=[pl.BlockSpec((tm, nn), lambda j, i: (i, j))] * nw,
        out_shape=[jax.ShapeDtypeStruct((T, nb * nn), F32)] * nw,
        compiler_params=_cparams(("parallel", "parallel")),
    )(x, *ws))


def _bd_fanin(xs, ws, *, name):
    nw = len(ws)
    nb, kb, nn = ws[0].shape
    T = xs[0].shape[0]
    tm = _pick(T, 512, 16)

    def body(*refs):
        acc = None
        for x_ref, w_ref in zip(refs[:nw], refs[nw:2 * nw]):
            t = jnp.dot(x_ref[...].astype(BF16), w_ref[0].astype(BF16), preferred_element_type=F32)
            acc = t if acc is None else acc + t
        refs[2 * nw][...] = acc

    return pl.pallas_call(
        body, name=name, grid=(nb, T // tm),
        in_specs=[pl.BlockSpec((tm, kb), lambda j, i: (i, j))] * nw + [pl.BlockSpec((1, kb, nn), lambda j, i: (j, 0, 0))] * nw,
        out_specs=pl.BlockSpec((tm, nn), lambda j, i: (i, j)),
        out_shape=jax.ShapeDtypeStruct((T, nb * nn), F32),
        compiler_params=_cparams(("parallel", "parallel")),
    )(*xs, *ws)


def _bd_dw(xs, dys, nb, *, name):
    npair = len(xs)
    T = xs[0].shape[0]
    kb = xs[0].shape[1] // nb
    nn = dys[0].shape[1] // nb
    tm = _pick(T, 512, 16)
    dims = (((0,), (0,)), ((), ()))

    def body(*refs):
        i = pl.program_id(1)
        for x_ref, d_ref, o_ref in zip(refs[:npair], refs[npair:2 * npair], refs[2 * npair:]):
            @pl.when(i == 0)
            def _(o_ref=o_ref):
                o_ref[...] = jnp.zeros_like(o_ref)

            o_ref[0] += lax.dot_general(x_ref[...].astype(BF16), d_ref[...].astype(BF16), dims,
                                        preferred_element_type=F32)

    return list(pl.pallas_call(
        body, name=name, grid=(nb, T // tm),
        in_specs=[pl.BlockSpec((tm, kb), lambda j, i: (i, j))] * npair + [pl.BlockSpec((tm, nn), lambda j, i: (i, j))] * npair,
        out_specs=[pl.BlockSpec((1, kb, nn), lambda j, i: (j, 0, 0))] * npair,
        out_shape=[jax.ShapeDtypeStruct((nb, kb, nn), F32)] * npair,
        compiler_params=_cparams(("parallel", "arbitrary")),
    )(*xs, *dys))


def _cmul(ar, ai, br, bi):
    return ar * br - ai * bi, ar * bi + ai * br


def _cpow(lr, li, n):
    rr, ri = None, None
    br, bi = lr, li
    while n:
        if n & 1:
            rr, ri = (br, bi) if rr is None else _cmul(rr, ri, br, bi)
        n >>= 1
        if n:
            br, bi = _cmul(br, bi, br, bi)
    return rr, ri


def _s5_scan(b_re, b_im, lam_re, lam_im, h0_re, h0_im, e0_re, e0_im, *, reverse, name):
    rows, C = b_re.shape
    n = rows // N_SEG
    cb = _pick(C, 512, LANES)
    seg_order = list(range(N_SEG))[::-1] if reverse else list(range(N_SEG))
    s_first, s_last = seg_order[0], seg_order[-1]

    def body(br_ref, bi_ref, lr_ref, li_ref, h0r_ref, h0i_ref, e0r_ref, e0i_ref, hr_ref, hi_ref, htr_ref, hti_ref):
        shape = (N_SEG, cb)
        lr = jnp.broadcast_to(lr_ref[...], shape)
        li = jnp.broadcast_to(li_ref[...], shape)
        row = lax.broadcasted_iota(jnp.int32, shape, 0)

        def step_of(k):
            return (n - 1 - k) if reverse else k

        def rows_of(k):
            return pl.ds(pl.multiple_of(step_of(k) * N_SEG, N_SEG), N_SEG)

        first = row == s_first
        hr = br_ref[rows_of(0), :] + jnp.where(first, e0r_ref[...], 0.0)
        hi = bi_ref[rows_of(0), :] + jnp.where(first, e0i_ref[...], 0.0)
        hr_ref[rows_of(0), :] = hr
        hi_ref[rows_of(0), :] = hi

        def pass1(k, carry):
            hr, hi = carry
            pr, pi = _cmul(lr, li, hr, hi)
            hr = pr + br_ref[rows_of(k), :]
            hi = pi + bi_ref[rows_of(k), :]
            hr_ref[rows_of(k), :] = hr
            hi_ref[rows_of(k), :] = hi
            return hr, hi

        er, ei = lax.fori_loop(1, n, pass1, (hr, hi))

        lnr, lni = _cpow(lr_ref[...], li_ref[...], n)
        cr, ci = h0r_ref[...], h0i_ref[...]
        cin_r = jnp.zeros(shape, F32)
        cin_i = jnp.zeros(shape, F32)
        for s in seg_order:
            cin_r = jnp.where(row == s, cr, cin_r)
            cin_i = jnp.where(row == s, ci, cin_i)
            if s != s_last:
                pr, pi = _cmul(lnr, lni, cr, ci)
                cr = pr + jnp.sum(jnp.where(row == s, er, 0.0), axis=0, keepdims=True)
                ci = pi + jnp.sum(jnp.where(row == s, ei, 0.0), axis=0, keepdims=True)

        def pass2(k, carry):
            pr, pi = carry
            ar, ai = _cmul(pr, pi, cin_r, cin_i)
            hr = hr_ref[rows_of(k), :] + ar
            hi = hi_ref[rows_of(k), :] + ai
            hr_ref[rows_of(k), :] = hr
            hi_ref[rows_of(k), :] = hi
            npr, npi = _cmul(pr, pi, lr, li)
            return npr, npi

        lax.fori_loop(0, n, pass2, (lr, li))
        last_r = hr_ref[rows_of(n - 1), :]
        last_i = hi_ref[rows_of(n - 1), :]
        htr_ref[...] = jnp.sum(jnp.where(row == s_last, last_r, 0.0), axis=0, keepdims=True)
        hti_ref[...] = jnp.sum(jnp.where(row == s_last, last_i, 0.0), axis=0, keepdims=True)

    big = pl.BlockSpec((rows, cb), lambda j: (0, j))
    vec = pl.BlockSpec((1, cb), lambda j: (0, j))
    return pl.pallas_call(
        body, name=name, grid=(C // cb,),
        in_specs=[big, big] + [vec] * 6,
        out_specs=[big, big, vec, vec],
        out_shape=[jax.ShapeDtypeStruct((rows, C), F32)] * 2 + [jax.ShapeDtypeStruct((1, C), F32)] * 2,
        compiler_params=_cparams(("parallel",)),
    )(b_re, b_im, lam_re, lam_im, h0_re, h0_im, e0_re, e0_im)


def _s5_dlam(mu_re, mu_im, h_re, h_im, h0_re, h0_im, *, reverse, name):
    rows, C = h_re.shape
    n = rows // N_SEG
    cb = _pick(C, 512, LANES)
    s_first = N_SEG - 1 if reverse else 0

    def body(mr_ref, mi_ref, hr_ref, hi_ref, h0r_ref, h0i_ref, dr_ref, di_ref):
        shape = (N_SEG, cb)
        row = lax.broadcasted_iota(jnp.int32, shape, 0)

        def rows_of(k):
            step = (n - 1 - k) if reverse else k
            return pl.ds(pl.multiple_of(step * N_SEG, N_SEG), N_SEG)

        def term(k, pr, pi):
            mr, mi = mr_ref[rows_of(k), :], mi_ref[rows_of(k), :]
            return mr * pr + mi * pi, mi * pr - mr * pi

        shift = N_SEG - 1 if reverse else 1
        pr = jnp.where(row == s_first, h0r_ref[...], pltpu.roll(hr_ref[rows_of(n - 1), :], shift, 0))
        pi = jnp.where(row == s_first, h0i_ref[...], pltpu.roll(hi_ref[rows_of(n - 1), :], shift, 0))
        acc = term(0, pr, pi)

        def loop(k, acc):
            tr, ti = term(k, hr_ref[rows_of(k - 1), :], hi_ref[rows_of(k - 1), :])
            return acc[0] + tr, acc[1] + ti

        ar, ai = lax.fori_loop(1, n, loop, acc)
        dr_ref[...] = jnp.sum(ar, axis=0, keepdims=True)
        di_ref[...] = jnp.sum(ai, axis=0, keepdims=True)

    big = pl.BlockSpec((rows, cb), lambda j: (0, j))
    vec = pl.BlockSpec((1, cb), lambda j: (0, j))
    return pl.pallas_call(
        body, name=name, grid=(C // cb,),
        in_specs=[big] * 4 + [vec] * 2, out_specs=[vec, vec],
        out_shape=[jax.ShapeDtypeStruct((1, C), F32)] * 2,
        compiler_params=_cparams(("parallel",)),
    )(mu_re, mu_im, h_re, h_im, h0_re, h0_im)


def _attn_scores(qn, qr, kn, kr):
    nt = (((1,), (1,)), ((), ()))
    s = lax.dot_general(qn, kn, nt, preferred_element_type=F32) + lax.dot_general(qr, kr, nt, preferred_element_type=F32)
    s = s * ATTN_SCALE
    p = jnp.exp(s - jnp.max(s, axis=-1, keepdims=True))
    return p / jnp.sum(p, axis=-1, keepdims=True)


def _attn_specs(L, T, tq):
    H = MLA_HEADS
    return [
        pl.BlockSpec((tq, LANES), lambda h, i: (i, h)),
        pl.BlockSpec((tq, LANES), lambda h, i: (i, H + h)),
        pl.BlockSpec((T, LANES), lambda h, i: (0, 2 * h)),
        pl.BlockSpec((T, LANES), lambda h, i: (0, 2 * h + 1)),
        pl.BlockSpec((T, LANES), lambda h, i: (0, 0)),
    ]


def _attn_fwd(qq, kv, kr, *, name):
    L, T = qq.shape[0], kv.shape[0]
    tq = _pick(L, 256, 16)

    def body(qn_ref, qr_ref, kn_ref, v_ref, kr_ref, o_ref):
        p = _attn_scores(qn_ref[...], qr_ref[...], kn_ref[...], kr_ref[...])
        o_ref[...] = jnp.dot(p.astype(BF16), v_ref[...], preferred_element_type=F32).astype(o_ref.dtype)

    return pl.pallas_call(
        body, name=name, grid=(MLA_HEADS, L // tq), in_specs=_attn_specs(L, T, tq),
        out_specs=pl.BlockSpec((tq, LANES), lambda h, i: (i, h)),
        out_shape=jax.ShapeDtypeStruct((L, MLA_HEADS * V_DIM), BF16),
        compiler_params=_cparams(("parallel", "parallel")),
    )(qq, qq, kv, kv, kr)


def _attn_bwd(qq, kv, kr, do, *, name):
    L, T = qq.shape[0], kv.shape[0]
    H = MLA_HEADS
    tq = _pick(L, 256, 16)
    tn = (((0,), (0,)), ((), ()))
    nt = (((1,), (1,)), ((), ()))

    def body(qn_ref, qr_ref, kn_ref, v_ref, kr_ref, do_ref, dqn_ref, dqr_ref, dkn_ref, dv_ref, dkr_ref, dkn_acc, dv_acc):
        h, i = pl.program_id(0), pl.program_id(1)
        qn, qr, kn, v, krv, dov = qn_ref[...], qr_ref[...], kn_ref[...], v_ref[...], kr_ref[...], do_ref[...]
        p = _attn_scores(qn, qr, kn, krv)
        pb = p.astype(BF16)
        dp = lax.dot_general(dov, v, nt, preferred_element_type=F32)
        ds = (p * (dp - jnp.sum(dp * p, axis=-1, keepdims=True)) * ATTN_SCALE).astype(BF16)
        dqn_ref[...] = jnp.dot(ds, kn, preferred_element_type=F32)
        dqr_ref[...] = jnp.dot(ds, krv, preferred_element_type=F32)

        @pl.when(i == 0)
        def _():
            dkn_acc[...] = jnp.zeros_like(dkn_acc)
            dv_acc[...] = jnp.zeros_like(dv_acc)

        @pl.when((i == 0) & (h == 0))
        def _():
            dkr_ref[...] = jnp.zeros_like(dkr_ref)

        dv_acc[...] += lax.dot_general(pb, dov, tn, preferred_element_type=F32)
        dkn_acc[...] += lax.dot_general(ds, qn, tn, preferred_element_type=F32)
        dkr_ref[...] += lax.dot_general(ds, qr, tn, preferred_element_type=F32)

        @pl.when(i == pl.num_programs(1) - 1)
        def _():
            dkn_ref[...] = dkn_acc[...].astype(dkn_ref.dtype)
            dv_ref[...] = dv_acc[...].astype(dv_ref.dtype)

    in_specs = _attn_specs(L, T, tq) + [pl.BlockSpec((tq, LANES), lambda h, i: (i, h))]
    dqn, dqr, dkn, dv, dkr = pl.pallas_call(
        body, name=name, grid=(H, L // tq), in_specs=in_specs,
        out_specs=[pl.BlockSpec((tq, LANES), lambda h, i: (i, h)), pl.BlockSpec((tq, LANES), lambda h, i: (i, h)),
                   pl.BlockSpec((T, LANES), lambda h, i: (0, h)), pl.BlockSpec((T, LANES), lambda h, i: (0, h)),
                   pl.BlockSpec((T, LANES), lambda h, i: (0, 0))],
        out_shape=[jax.ShapeDtypeStruct((L, H * LANES), F32), jax.ShapeDtypeStruct((L, H * LANES), F32),
                   jax.ShapeDtypeStruct((T, H * LANES), BF16), jax.ShapeDtypeStruct((T, H * LANES), BF16),
                   jax.ShapeDtypeStruct((T, LANES), F32)],
        scratch_shapes=[pltpu.VMEM((T, LANES), F32), pltpu.VMEM((T, LANES), F32)],
        compiler_params=_cparams(("arbitrary", "arbitrary")),
    )(qq, qq, kv, kv, kr, do)
    return dqn, dqr, dkn, dv, dkr


def _adamw(w, g, m, v, *, name):
    c1 = 1.0 - ADAM_B1 ** ADAM_STEP
    c2 = 1.0 - ADAM_B2 ** ADAM_STEP

    def f(w, g, m, v):
        m = ADAM_B1 * m + (1.0 - ADAM_B1) * g
        v = ADAM_B2 * v + (1.0 - ADAM_B2) * jnp.square(g)
        delta = -ADAM_LR * ((m / c1) / (jnp.sqrt(v / c2) + ADAM_EPS) + ADAM_WD * w)
        return g, delta, m, v

    return _rw(f, [w, g, m, v], [], [F32] * 4, name=name)


def _slab_rows(rows, cols, n_arrays):
    return _pick(rows, max(16, (8 * 1024 * 1024) // (cols * 4 * n_arrays)), 16)


def _scalars(*vals):
    return jnp.stack([jnp.asarray(v, jnp.int32) for v in vals])


def _into_slot(src, slot, nslots, dtype, *, name):
    R, C = src.shape
    tr = _slab_rows(R, C, 2)

    def body(s_ref, x_ref, o_ref):
        o_ref[...] = x_ref[...].astype(o_ref.dtype)

    return pl.pallas_call(
        body, name=name,
        grid_spec=pltpu.PrefetchScalarGridSpec(
            num_scalar_prefetch=1, grid=(R // tr,),
            in_specs=[pl.BlockSpec((tr, C), lambda i, s: (i, 0))],
            out_specs=pl.BlockSpec((None, tr, C), lambda i, s: (s[0], i, 0))),
        out_shape=jax.ShapeDtypeStruct((nslots, R, C), dtype),
        compiler_params=_cparams(("arbitrary",)),
    )(_scalars(slot), src)


def _pair_sum(g, got, c, *, name):
    _, R, C = g.shape
    hr = R // 2
    tr = _slab_rows(hr, C, 3)
    nblk = hr // tr

    def body(s_ref, g_ref, r_ref, o_ref):
        o_ref[...] = (g_ref[...].astype(F32) + r_ref[...].astype(F32)).astype(o_ref.dtype)

    return pl.pallas_call(
        body, name=name,
        grid_spec=pltpu.PrefetchScalarGridSpec(
            num_scalar_prefetch=1, grid=(4, nblk),
            in_specs=[pl.BlockSpec((None, tr, C), lambda j, i, s: (j, s[0] * nblk + i, 0)),
                      pl.BlockSpec((None, tr, C), lambda j, i, s: (j, i, 0))],
            out_specs=pl.BlockSpec((None, tr, C), lambda j, i, s: (j, i, 0))),
        out_shape=jax.ShapeDtypeStruct((4, hr, C), g.dtype),
        compiler_params=_cparams(("arbitrary", "arbitrary")),
    )(_scalars(c), g, got)


def _chip_sum(p, landed, me_chip, c, *, name):
    _, hr, C = p.shape
    tr = _slab_rows(hr, C, 5)

    def body(s_ref, p_ref, l0_ref, l1_ref, l2_ref, o_ref):
        o_ref[...] = ((p_ref[...].astype(F32) + l0_ref[...].astype(F32)) + l1_ref[...].astype(F32)) + l2_ref[...].astype(F32)

    return pl.pallas_call(
        body, name=name,
        grid_spec=pltpu.PrefetchScalarGridSpec(
            num_scalar_prefetch=1, grid=(hr // tr,),
            in_specs=[pl.BlockSpec((None, tr, C), lambda i, s: (s[0], i, 0))]
            + [pl.BlockSpec((None, tr, C), functools.partial(lambda i, s, k: (k, i, 0), k=k)) for k in range(3)],
            out_specs=pl.BlockSpec((None, tr, C), lambda i, s: (s[1], i, 0))),
        out_shape=jax.ShapeDtypeStruct((2, hr, C), F32),
        compiler_params=_cparams(("arbitrary",)),
    )(_scalars(me_chip, c), p, landed, landed, landed)


def _place():
    return lax.axis_index("x"), lax.axis_index("y"), lax.axis_index("c")


def _other_chips(x, y):
    chips = [(1 - x, y), (x, 1 - y), (1 - x, 1 - y)]
    return chips, [2 * cx + cy for cx, cy in chips]


HBM = pl.BlockSpec(memory_space=pl.ANY)


def _allgather8(v, *, name):
    rows, cols = v.shape

    def body(v_ref, out_ref, send_sems, recv_sems):
        x, y, c = _place()
        me = 4 * x + 2 * y + c
        out_ref[me] = v_ref[...]
        copies = []
        for k in range(1, 8):
            bx, by, bc = (k >> 2) & 1, (k >> 1) & 1, k & 1
            px, py, pc = x ^ bx, y ^ by, c ^ bc
            cp = pltpu.make_async_remote_copy(
                src_ref=v_ref, dst_ref=out_ref.at[me], send_sem=send_sems.at[k - 1], recv_sem=recv_sems.at[k - 1],
                device_id=(px, py, pc), device_id_type=MESH)
            cp.start()
            copies.append((cp, 4 * px + 2 * py + pc))
        for k, (cp, peer) in enumerate(copies):
            pltpu.make_async_remote_copy(
                src_ref=v_ref, dst_ref=out_ref.at[peer], send_sem=send_sems.at[k], recv_sem=recv_sems.at[k],
                device_id=(x, y, c), device_id_type=MESH).wait_recv()
        for cp, _ in copies:
            cp.wait_send()

    return pl.pallas_call(
        body, name=name, out_shape=jax.ShapeDtypeStruct((8, rows, cols), v.dtype),
        in_specs=[pl.BlockSpec(memory_space=pltpu.VMEM)], out_specs=pl.BlockSpec(memory_space=pltpu.VMEM),
        scratch_shapes=[pltpu.SemaphoreType.DMA((7,)), pltpu.SemaphoreType.DMA((7,))],
        compiler_params=pltpu.CompilerParams(vmem_limit_bytes=VMEM_LIMIT),
    )(v)


def _allgather_shards(bufs, *, name):
    n = len(bufs)

    def body(*refs):
        outs = refs[n:2 * n]
        send_sems, recv_sems = refs[2 * n:]
        x, y, c = _place()
        me_chip = 2 * x + y
        sibling = (x, y, 1 - c)
        chips, chip_ids = _other_chips(x, y)

        def remote(k, j, blk, hf, to):
            hr = bufs[k].shape[1] // 2
            piece = outs[k].at[blk, pl.ds(pl.multiple_of(hf * hr, 16), hr), :]
            return pltpu.make_async_remote_copy(
                src_ref=piece, dst_ref=piece, send_sem=send_sems.at[6 * k + j], recv_sem=recv_sems.at[6 * k + j],
                device_id=to, device_id_type=MESH)

        sends = []
        for k in range(n):
            for j, chip in enumerate(chips):
                cp = remote(k, j, me_chip, c, (*chip, c))
                cp.start()
                sends.append(cp)
        for k in range(n):
            for j, chip in enumerate(chips):
                remote(k, j, chip_ids[j], c, (x, y, c)).wait_recv()
                cp = remote(k, 3 + j, chip_ids[j], c, sibling)
                cp.start()
                sends.append(cp)
        for k in range(n):
            for j in range(3):
                remote(k, 3 + j, chip_ids[j], 1 - c, (x, y, c)).wait_recv()
        for cp in sends:
            cp.wait_send()

    return list(pl.pallas_call(
        body, name=name, out_shape=[jax.ShapeDtypeStruct(b.shape, b.dtype) for b in bufs],
        in_specs=[HBM] * n, out_specs=[HBM] * n, input_output_aliases={k: k for k in range(n)},
        scratch_shapes=[pltpu.SemaphoreType.DMA((6 * n,)), pltpu.SemaphoreType.DMA((6 * n,))],
    )(*bufs))


def _pair_exchange(gs, *, name):
    n = len(gs)

    def body(*refs):
        ins, outs = refs[:n], refs[n:2 * n]
        send_sems, recv_sems = refs[2 * n:]
        x, y, c = _place()
        copies = []
        for k in range(n):
            hr = gs[k].shape[1] // 2
            src = ins[k].at[:, pl.ds(pl.multiple_of((1 - c) * hr, 16), hr), :]
            cp = pltpu.make_async_remote_copy(src_ref=src, dst_ref=outs[k], send_sem=send_sems.at[k], recv_sem=recv_sems.at[k],
                                              device_id=(x, y, 1 - c), device_id_type=MESH)
            cp.start()
            copies.append(cp)
        for cp in copies:
            cp.wait()

    return list(pl.pallas_call(
        body, name=name,
        out_shape=[jax.ShapeDtypeStruct((4, g.shape[1] // 2, g.shape[2]), g.dtype) for g in gs],
        in_specs=[HBM] * n, out_specs=[HBM] * n,
        scratch_shapes=[pltpu.SemaphoreType.DMA((n,)), pltpu.SemaphoreType.DMA((n,))],
    )(*gs))


def _chip_exchange(ps, *, name):
    n = len(ps)

    def body(*refs):
        ins, outs = refs[:n], refs[n:2 * n]
        send_sems, recv_sems = refs[2 * n:]
        x, y, c = _place()
        chips, chip_ids = _other_chips(x, y)
        copies = []
        for k in range(n):
            for j, chip in enumerate(chips):
                cp = pltpu.make_async_remote_copy(
                    src_ref=ins[k].at[chip_ids[j]], dst_ref=outs[k].at[j], send_sem=send_sems.at[3 * k + j],
                    recv_sem=recv_sems.at[3 * k + j], device_id=(*chip, c), device_id_type=MESH)
                cp.start()
                copies.append(cp)
        for cp in copies:
            cp.wait()

    return list(pl.pallas_call(
        body, name=name,
        out_shape=[jax.ShapeDtypeStruct((3,) + p.shape[1:], p.dtype) for p in ps],
        in_specs=[HBM] * n, out_specs=[HBM] * n,
        scratch_shapes=[pltpu.SemaphoreType.DMA((3 * n,)), pltpu.SemaphoreType.DMA((3 * n,))],
    )(*ps))


def _pair_gather(bufs, *, name):
    n = len(bufs)

    def body(*refs):
        outs = refs[n:2 * n]
        send_sems, recv_sems = refs[2 * n:]
        x, y, c = _place()

        def remote(k, hf, to):
            return pltpu.make_async_remote_copy(src_ref=outs[k].at[hf], dst_ref=outs[k].at[hf], send_sem=send_sems.at[k],
                                                recv_sem=recv_sems.at[k], device_id=to, device_id_type=MESH)

        copies = [remote(k, c, (x, y, 1 - c)) for k in range(n)]
        for cp in copies:
            cp.start()
        for k, cp in enumerate(copies):
            cp.wait_send()
            remote(k, 1 - c, (x, y, c)).wait_recv()

    return list(pl.pallas_call(
        body, name=name, out_shape=[jax.ShapeDtypeStruct(b.shape, b.dtype) for b in bufs],
        in_specs=[HBM] * n, out_specs=[HBM] * n, input_output_aliases={k: k for k in range(n)},
        scratch_shapes=[pltpu.SemaphoreType.DMA((n,)), pltpu.SemaphoreType.DMA((n,))],
    )(*bufs))


def _reduce_scatter(gs):
    x, y, c = _place()
    me_chip = 2 * x + y
    got = _pair_exchange(gs, name="rs_pair_exchange")
    pair = [_pair_sum(g, r, c, name=f"rs_pair_sum_{k}") for k, (g, r) in enumerate(zip(gs, got))]
    landed = _chip_exchange(pair, name="rs_chip_exchange")
    halves = [_chip_sum(p, l, me_chip, c, name=f"rs_chip_sum_{k}") for k, (p, l) in enumerate(zip(pair, landed))]
    full = _pair_gather(halves, name="rs_pair_gather")
    return [f.reshape(2 * f.shape[1], f.shape[2]) for f in full]


def _to_segments(a):
    rows = a.shape[0]
    return a.reshape(N_SEG, rows // N_SEG, -1).transpose(1, 0, 2).reshape(rows, -1)


def _from_segments(a):
    rows = a.shape[0]
    return a.reshape(rows // N_SEG, N_SEG, -1).transpose(1, 0, 2).reshape(rows, -1)


def _rope_tables(L):
    t = jnp.arange(L, dtype=jnp.int32)
    row = (t // GRID_W).astype(F32)
    col = (t % GRID_W).astype(F32)
    n_freq = QK_ROPE // 4
    inv = ROPE_BASE ** (-jnp.arange(n_freq, dtype=F32) / n_freq)
    a0, a1 = row[:, None] * inv, col[:, None] * inv
    z = jnp.zeros((L, LANES - QK_ROPE), F32)
    cos = jnp.concatenate([jnp.cos(a0), jnp.cos(a0), jnp.cos(a1), jnp.cos(a1), z], axis=1)
    sin = jnp.concatenate([-jnp.sin(a0), jnp.sin(a0), -jnp.sin(a1), jnp.sin(a1), z], axis=1)
    return _to_segments(cos), _to_segments(sin)


def _col_blocks(w, nblk):
    r, c = w.shape
    return w.reshape(r, nblk, c // nblk).transpose(1, 0, 2)


def _from_col_blocks(w4):
    nblk, r, c = w4.shape
    return w4.transpose(1, 0, 2).reshape(r, nblk * c)


def _s5_discretize(a_re, a_im, log_dt, b_re, b_im):
    dt = jnp.exp(log_dt)[:, None]
    mag = jnp.exp(a_re * dt)
    ab_re, ab_im = mag * jnp.cos(a_im * dt), mag * jnp.sin(a_im * dt)
    den = a_re * a_re + a_im * a_im
    nr, ni = ab_re - 1.0, ab_im
    co_re = (nr * a_re + ni * a_im) / den
    co_im = (ni * a_re - nr * a_im) / den
    bb_re = co_re[..., None] * b_re - co_im[..., None] * b_im
    bb_im = co_re[..., None] * b_im + co_im[..., None] * b_re
    return ab_re, ab_im, bb_re, bb_im


def _diag_blocks_in(bb, gpb):
    G, N, P = bb.shape
    eye = jnp.eye(gpb, dtype=bb.dtype)
    t = jnp.einsum("jgnp,gh->jgphn", bb.reshape(G // gpb, gpb, N, P), eye)
    return t.reshape(G // gpb, gpb * P, gpb * N)


def _diag_blocks_out(cc, gpb):
    G, P, N = cc.shape
    eye = jnp.eye(gpb, dtype=cc.dtype)
    t = jnp.einsum("jgpn,gh->jgnhp", cc.reshape(G // gpb, gpb, P, N), eye)
    return t.reshape(G // gpb, gpb * N, gpb * P)


def _tr(ws):
    return [jnp.swapaxes(w, 1, 2) for w in ws]


WEIGHTS = ['c_ctx', 'w_mod', 'b_mod', 'norm1', 'norm2', 'w_in', 's5_a_re', 's5_a_im', 's5_log_dt', 's5_b_re', 's5_b_im',
           's5_c_re', 's5_c_im', 's5_d', 'w_glu', 'q_norm', 'kv_norm', 'w_uq', 'w_ukv', 'w_mla_o', 'w_out', 'w_ffn_in',
           'w_ffn_out', 'norm_f']
COL_SHARDED = ['w_in', 'w_glu', 'w_uq', 'w_ukv', 'w_mla_o', 'w_ffn_in']
ROW_SHARDED = ['w_out', 'w_ffn_out']
SMALL = ['norm1', 'norm2', 's5_a_re', 's5_a_im', 's5_log_dt', 's5_b_re', 's5_b_im', 's5_c_re', 's5_c_im', 's5_d',
         'q_norm', 'kv_norm', 'norm_f']


def _pad_rows(a, rows):
    return jnp.concatenate([a, jnp.zeros((rows - a.shape[0],) + a.shape[1:], a.dtype)], axis=0)


def _pack(vals, width, rows):
    flat = jnp.concatenate([v.reshape(-1).astype(F32) for v in vals])
    flat = jnp.concatenate([flat, jnp.zeros((rows * width - flat.shape[0],), F32)])
    return flat.reshape(rows, width)


def _unpack(buf, like):
    flat = buf.reshape(-1)
    out, pos = [], 0
    for v in like:
        out.append(flat[pos:pos + v.size].reshape(v.shape))
        pos += v.size
    return out


def _step(x, c, ctx, loss_target, w, m, v):
    px, py, pc = _place()
    me = 4 * px + 2 * py + pc
    me_chip = 2 * px + py
    L, D = x.shape[1], x.shape[2]
    Lc = ctx.shape[1]
    T = L + Lc
    SW = D // 2
    G = SW // S5_GROUP
    C = G * S5_STATE
    H = MLA_HEADS
    q_rank = w['q_norm'].shape[1]
    kv_rank = w['kv_norm'].shape[1]
    d_ff = w['w_ffn_out'].shape[1] * 4
    wa_used = SW + q_rank + kv_rank + QK_ROPE
    WA = -(-(SW + q_rank + kv_rank + LANES) // 512) * 512

    c_rows = _pad_rows(c.astype(F32), SUBLANES)
    c_all = _allgather8(c_rows, name="ag_cond")[:, 0, :]
    cond = jnp.concatenate([c_all, w['c_ctx'].reshape(1, D)], axis=0)
    cond = _pad_rows(cond, 16)
    (act,) = _rw(lambda t: (jax.nn.silu(t),), [cond], [], [F32], name="cond_silu")
    w_mod, cs_mod = w['w_mod'][0], w['w_mod'].shape[2]
    mod_part = _mm(act, w_mod, out_dtype=F32, name="mod_fwd")
    mod_all = _allgather8(mod_part, name="ag_mod")
    mod_full = jnp.concatenate([mod_all[0], mod_all[2], mod_all[4], mod_all[6]], axis=1) + w['b_mod']
    m_lat = lax.dynamic_slice_in_dim(mod_full, me, 1, axis=0).reshape(6, D)
    m_ctx = mod_full[8].reshape(6, D)
    sh1, sc1, g1, sh2, sc2, g2 = (m_lat[i:i + 1] for i in range(6))
    csh1, csc1 = m_ctx[0:1], m_ctx[1:2]

    names = COL_SHARDED + ROW_SHARDED
    bufs = [_into_slot(w[nme][0], me_chip, 4, BF16, name=f"cast_{nme}") for nme in names]
    gathered = dict(zip(names, _allgather_shards(bufs, name="ag_weights")))
    w_in = _from_col_blocks(gathered['w_in'])
    w_uq = _from_col_blocks(gathered['w_uq'])
    w_glu, w_ukv, w_mla_o, w_ffn_in = (gathered[nme] for nme in ('w_glu', 'w_ukv', 'w_mla_o', 'w_ffn_in'))
    w_out = gathered['w_out'].reshape(D, D)
    w_ffn_out = gathered['w_ffn_out'].reshape(d_ff, D)
    w_a = jnp.concatenate([w_in[:, :wa_used], jnp.zeros((D, WA - wa_used), BF16)], axis=1)
    w_g = w_in[:, wa_used:]
    uq3 = w_uq.reshape(q_rank, H, QK_NOPE + QK_ROPE)
    w_q2 = jnp.concatenate([
        uq3[:, :, :QK_NOPE].reshape(q_rank, H * QK_NOPE),
        jnp.concatenate([uq3[:, :, QK_NOPE:], jnp.zeros((q_rank, H, LANES - QK_ROPE), BF16)], axis=2).reshape(q_rank, H * LANES),
    ], axis=1)

    xs = _to_segments(x[0])
    cs = _to_segments(ctx[0])
    tgt = _to_segments(loss_target[0])
    cos, sin = _rope_tables(L)
    n1, n2, nf = w['norm1'], w['norm2'], w['norm_f'].reshape(1, D)
    qg, kvg = w['q_norm'], w['kv_norm']

    (xn_lat,) = _rw(_f_norm_mod, [xs], [n1, sc1, sh1], [BF16], name="norm1_lat")
    (xn_ctx,) = _rw(_f_norm_mod, [cs], [n1, csc1, csh1], [BF16], name="norm1_ctx")
    xn = jnp.concatenate([xn_lat, xn_ctx], axis=0)
    ha = _mm(xn, w_a, out_dtype=F32, name="in_proj")
    ha_lat, ha_ctx = ha[:L], ha[L:]
    gt = _mm(xn_lat, w_g, out_dtype=F32, name="in_gates")
    f_post_lat = _make_f_post_in(SW, q_rank, kv_rank, True)
    f_post_ctx = _make_f_post_in(SW, q_rank, kv_rank, False)
    u_lat, cqn, ckvn_lat, kr_lat = _rw(f_post_lat, [ha_lat, cos, sin], [qg, kvg], [F32, BF16, BF16, BF16], name="post_in_lat")
    u_ctx, ckvn_ctx, kr_ctx = _rw(f_post_ctx, [ha_ctx], [kvg], [F32, BF16, BF16], name="post_in_ctx")

    gpb = min(S5_BLOCK_GROUPS, G)
    gpo = min(8, G)
    d_skip = w['s5_d'][0].reshape(1, SW)
    disc, vjp_disc, w_b, w_c = [], [], [], []
    for d in range(2):
        prm = (w['s5_a_re'][0, d], w['s5_a_im'][0, d], w['s5_log_dt'][0, d], w['s5_b_re'][0, d], w['s5_b_im'][0, d])

        def prep(a_re, a_im, log_dt, b_re, b_im):
            ab_re, ab_im, bb_re, bb_im = _s5_discretize(a_re, a_im, log_dt, b_re, b_im)
            return ab_re.reshape(1, C), ab_im.reshape(1, C), _diag_blocks_in(bb_re, gpb), _diag_blocks_in(bb_im, gpb)

        out, vj = jax.vjp(prep, *prm)
        disc.append(out)
        vjp_disc.append(vj)
        w_b += [out[2], out[3]]
        w_c += [_diag_blocks_out(w['s5_c_re'][0, d], gpo), -_diag_blocks_out(w['s5_c_im'][0, d], gpo)]
    nb_in = G // gpb
    nb_out = G // gpo
    bu_lat = _bd_fanout(u_lat, w_b, name="s5_bu_lat")
    bu_ctx = _bd_fanout(u_ctx, w_b, name="s5_bu_ctx")
    zero = jnp.zeros((1, C), F32)
    h_lat, h_ctx, hT_ctx = [], [], []
    for d, rev in enumerate((False, True)):
        lr, li = disc[d][0], disc[d][1]
        hcr, hci, tr, ti = _s5_scan(bu_ctx[2 * d], bu_ctx[2 * d + 1], lr, li, zero, zero, zero, zero, reverse=rev,
                                    name=f"s5_scan_ctx_{d}")
        hlr, hli, _, _ = _s5_scan(bu_lat[2 * d], bu_lat[2 * d + 1], lr, li, tr, ti, zero, zero, reverse=rev,
                                  name=f"s5_scan_lat_{d}")
        h_ctx += [hcr, hci]
        h_lat += [hlr, hli]
        hT_ctx += [tr, ti]
    r5 = _bd_fanin(h_lat, w_c, name="s5_readout")
    (z,) = _rw(_f_s5post, [u_lat, r5], [d_skip], [BF16], name="s5_post")

    q2 = _mm(cqn, w_q2, out_dtype=F32, name="q_up")
    (qq,) = _rw(_f_qpost, [q2, cos, sin], [], [BF16], name="q_rope")
    kvn = jnp.concatenate([ckvn_lat, ckvn_ctx], axis=0)
    kr_all = jnp.concatenate([kr_lat, kr_ctx], axis=0)
    kv = _mm(kvn, w_ukv, b_shards=4, out_dtype=BF16, name="kv_up")
    o = _attn_fwd(qq, kv, kr_all, name="attn_fwd")

    ab = _mm(z, w_glu, b_shards=4, out_dtype=F32, name="glu_proj")
    bm = _mm(o, w_mla_o, b_shards=4, out_dtype=F32, name="mla_out")
    (mix,) = _rw(_f_merge, [ab, bm, gt], [], [BF16], name="merge")
    out1 = _mm(mix, w_out, out_dtype=F32, name="out_proj")
    x1, xn2 = _rw(_f_resid_norm, [xs, out1], [g1, n2, sc2, sh2], [F32, BF16], name="resid_norm2")
    ab2 = _mm(xn2, w_ffn_in, b_shards=4, out_dtype=F32, name="ffn_in")
    (hmid,) = _rw(_f_swiglu, [ab2], [], [BF16], name="ffn_act")
    f2 = _mm(hmid, w_ffn_out, out_dtype=F32, name="ffn_out")
    (row_loss,) = _rw(_f_final, [x1, f2, tgt], [g2, nf], [F32], name="final_loss")
    loss = lax.psum(jnp.sum(row_loss), ("x", "y", "c"))

    ones = jnp.ones((L, 1), F32)
    (dx1_a, df2), (dg2, dnf) = _rw_vjp(_f_final, [x1, f2, tgt], [g2, nf], [[ones]], [True, True, False], [True, True],
                                       [F32, BF16], name="final_loss_bwd")
    dhmid = _mm(df2, w_ffn_out, tb=True, out_dtype=F32, name="ffn_out_dx")
    gw_ffn_out = _mm(hmid, df2, ta=True, out_dtype=BF16, name="ffn_out_dw")
    (dab2,), _ = _rw_vjp(_f_swiglu, [ab2], [], [[dhmid]], [True], [], [BF16], name="ffn_act_bwd")
    dxn2 = _mm(dab2, w_ffn_in, tb=True, b_shards=4, out_dtype=F32, name="ffn_in_dx")
    gw_ffn_in = _mm(xn2, dab2, ta=True, out_shards=4, out_dtype=BF16, name="ffn_in_dw")
    (dx_a, dout1), (dg1, dn2, dsc2, dsh2) = _rw_vjp(
        _f_resid_norm, [xs, out1], [g1, n2, sc2, sh2], [[dx1_a], [dxn2]], [True, True], [True] * 4, [F32, BF16],
        name="resid_norm2_bwd")
    dmix = _mm(dout1, w_out, tb=True, out_dtype=F32, name="out_proj_dx")
    gw_out = _mm(mix, dout1, ta=True, out_dtype=BF16, name="out_proj_dw")
    (dab, dbm, dgt), _ = _rw_vjp(_f_merge, [ab, bm, gt], [], [[dmix]], [True] * 3, [], [BF16] * 3, name="merge_bwd")
    dz = _mm(dab, w_glu, tb=True, b_shards=4, out_dtype=F32, name="glu_proj_dx")
    gw_glu = _mm(z, dab, ta=True, out_shards=4, out_dtype=BF16, name="glu_proj_dw")
    do = _mm(dbm, w_mla_o, tb=True, b_shards=4, out_dtype=BF16, name="mla_out_dx")
    gw_mla_o = _mm(o, dbm, ta=True, out_shards=4, out_dtype=BF16, name="mla_out_dw")
    dxn_g = _mm(dgt, w_g, tb=True, out_dtype=F32, name="in_gates_dx")
    gw_g = _mm(xn_lat, dgt, ta=True, out_dtype=BF16, name="in_gates_dw")

    (du_a, dr5), (dd_skip,) = _rw_vjp(_f_s5post, [u_lat, r5], [d_skip], [[dz]], [True, True], [True], [F32, F32],
                                      name="s5_post_bwd")
    dh_lat = _bd_fanout(dr5, _tr(w_c), name="s5_readout_dx")
    dw_c = _bd_dw(h_lat, [dr5] * 4, nb_out, name="s5_readout_dw")
    zeros_ctx = jnp.zeros((Lc, C), F32)
    mu_lat, mu_ctx, dlam = [], [], []
    for d, rev in enumerate((False, True)):
        lr, li = disc[d][0], disc[d][1]
        mlr, mli, fr, fi = _s5_scan(dh_lat[2 * d], dh_lat[2 * d + 1], lr, -li, zero, zero, zero, zero, reverse=not rev,
                                    name=f"s5_adj_lat_{d}")
        dh0r, dh0i = _cmul(lr, -li, fr, fi)
        mcr, mci, _, _ = _s5_scan(zeros_ctx, zeros_ctx, lr, -li, zero, zero, dh0r, dh0i, reverse=not rev,
                                  name=f"s5_adj_ctx_{d}")
        dl_lat = _s5_dlam(mlr, mli, h_lat[2 * d], h_lat[2 * d + 1], hT_ctx[2 * d], hT_ctx[2 * d + 1], reverse=rev,
                          name=f"s5_dlam_lat_{d}")
        dl_ctx = _s5_dlam(mcr, mci, h_ctx[2 * d], h_ctx[2 * d + 1], zero, zero, reverse=rev, name=f"s5_dlam_ctx_{d}")
        mu_lat += [mlr, mli]
        mu_ctx += [mcr, mci]
        dlam.append((dl_lat[0] + dl_ctx[0], dl_lat[1] + dl_ctx[1]))
    du_b = _bd_fanin(mu_lat, _tr(w_b), name="s5_bu_lat_dx")
    du_ctx = _bd_fanin(mu_ctx, _tr(w_b), name="s5_bu_ctx_dx")
    dw_b_lat = _bd_dw([u_lat] * 4, mu_lat, nb_in, name="s5_bu_lat_dw")
    dw_b_ctx = _bd_dw([u_ctx] * 4, mu_ctx, nb_in, name="s5_bu_ctx_dw")
    g_s5 = {}
    for d in range(2):
        ct = (dlam[d][0], dlam[d][1], dw_b_lat[2 * d] + dw_b_ctx[2 * d], dw_b_lat[2 * d + 1] + dw_b_ctx[2 * d + 1])
        ga_re, ga_im, gdt, gb_re, gb_im = vjp_disc[d](ct)
        _, vj_c = jax.vjp(lambda cr, ci: (_diag_blocks_out(cr, gpo), -_diag_blocks_out(ci, gpo)),
                          w['s5_c_re'][0, d], w['s5_c_im'][0, d])
        gc_re, gc_im = vj_c((dw_c[2 * d], dw_c[2 * d + 1]))
        for nme, val in (('s5_a_re', ga_re), ('s5_a_im', ga_im), ('s5_log_dt', gdt), ('s5_b_re', gb_re),
                         ('s5_b_im', gb_im), ('s5_c_re', gc_re), ('s5_c_im', gc_im)):
            g_s5.setdefault(nme, []).append(val)
    g_small = {nme: jnp.stack(vals)[None] for nme, vals in g_s5.items()}
    g_small['s5_d'] = dd_skip.reshape(w['s5_d'].shape)

    dqn, dqr, dkn, dv, dkr = _attn_bwd(qq, kv, kr_all, do, name="attn_bwd")
    dqq = jnp.concatenate([dqn, dqr], axis=1)
    (dq2,), _ = _rw_vjp(_f_qpost, [q2, cos, sin], [], [[dqq]], [True, False, False], [], [BF16], name="q_rope_bwd")
    dcqn = _mm(dq2, w_q2, tb=True, out_dtype=F32, name="q_up_dx")
    gw_q2 = _mm(cqn, dq2, ta=True, out_dtype=BF16, name="q_up_dw")
    dkv = jnp.stack([dkn.reshape(T, H, LANES), dv.reshape(T, H, LANES)], axis=2).reshape(T, 2 * H * LANES)
    dckvn = _mm(dkv, w_ukv, tb=True, b_shards=4, out_dtype=F32, name="kv_up_dx")
    gw_ukv = _mm(kvn, dkv, ta=True, out_shards=4, out_dtype=BF16, name="kv_up_dw")

    (dha_lat,), (dqg, dkvg_lat) = _rw_vjp(
        f_post_lat, [ha_lat, cos, sin], [qg, kvg], [[du_a, du_b], [dcqn], [dckvn[:L]], [dkr[:L]]], [True, False, False],
        [True, True], [BF16], name="post_in_lat_bwd")
    (dha_ctx,), (dkvg_ctx,) = _rw_vjp(f_post_ctx, [ha_ctx], [kvg], [[du_ctx], [dckvn[L:]], [dkr[L:]]], [True], [True],
                                      [BF16], name="post_in_ctx_bwd")
    dha = jnp.concatenate([dha_lat, dha_ctx], axis=0)
    dxn = _mm(dha, w_a, tb=True, out_dtype=F32, name="in_proj_dx")
    gw_a = _mm(xn, dha, ta=True, out_dtype=BF16, name="in_proj_dw")
    (dx_seg,), (dn1_lat, dsc1, dsh1) = _rw_vjp(
        _f_norm_mod_keep, [xs], [n1, sc1, sh1], [[dxn[:L], dxn_g], [dx_a]], [True], [True] * 3, [F32], name="norm1_lat_bwd")
    _, (dn1_ctx, dcsc1, dcsh1) = _rw_vjp(_f_norm_mod, [cs], [n1, csc1, csh1], [[dxn[L:]]], [False], [True] * 3, [],
                                         name="norm1_ctx_bwd")
    grad_x = _from_segments(dx_seg)[None]
    g_small.update(norm1=dn1_lat + dn1_ctx, norm2=dn2, q_norm=dqg, kv_norm=dkvg_lat + dkvg_ctx, norm_f=dnf.reshape(D))

    zD = jnp.zeros((1, D), F32)
    dm = jnp.concatenate([
        jnp.concatenate([dsh1, dsc1, dg1, dsh2, dsc2, dg2], axis=1),
        jnp.concatenate([dcsh1, dcsc1, zD, zD, zD, zD], axis=1),
    ], axis=0)
    dm_all = _allgather8(_pad_rows(dm, SUBLANES), name="ag_dmod")
    dm_ctx = dm_all[0, 1]
    for k in range(1, 8):
        dm_ctx = dm_ctx + dm_all[k, 1]
    dmod = _pad_rows(jnp.concatenate([dm_all[:, 0, :], dm_ctx[None]], axis=0), 16)
    g_b_mod = jnp.sum(dmod, axis=0, keepdims=True)
    dmod_mine = lax.dynamic_slice_in_dim(dmod, me_chip * cs_mod, cs_mod, axis=1)
    g_w_mod = _mm(act, dmod_mine, ta=True, out_dtype=F32, name="mod_dw")
    dact_part = _mm(dmod_mine, w_mod, tb=True, out_dtype=F32, name="mod_dx")
    dact_all = _allgather8(dact_part, name="ag_dact")
    dact = dact_all[0] + dact_all[2] + dact_all[4] + dact_all[6]
    (dcond_rows,), _ = _rw_vjp(lambda t: (jax.nn.silu(t),), [cond], [], [[dact]], [True], [], [F32], name="cond_silu_bwd")
    g_c_ctx = dcond_rows[8]

    uq_nope = gw_q2[:, :H * QK_NOPE].reshape(q_rank, H, QK_NOPE)
    uq_rope = gw_q2[:, H * QK_NOPE:].reshape(q_rank, H, LANES)[:, :, :QK_ROPE]
    gw_uq = jnp.concatenate([uq_nope, uq_rope], axis=2).reshape(q_rank, H * (QK_NOPE + QK_ROPE))
    gw_in = jnp.concatenate([gw_a[:, :wa_used], gw_g], axis=1)
    grad_blocks = {'w_in': _col_blocks(gw_in, 4), 'w_glu': gw_glu, 'w_uq': _col_blocks(gw_uq, 4), 'w_ukv': gw_ukv,
                   'w_mla_o': gw_mla_o, 'w_ffn_in': gw_ffn_in, 'w_out': gw_out.reshape(4, -1, D),
                   'w_ffn_out': gw_ffn_out.reshape(4, -1, D)}
    blocks = [grad_blocks[nme] for nme in names]
    small_vals = [g_small[nme] for nme in SMALL]
    n_small = sum(val.size for val in small_vals)
    small_rows = -(-n_small // (LANES * 4 * 32)) * 32
    blocks.append(_pack(small_vals, LANES, 4 * small_rows).reshape(4, small_rows, LANES))
    reduced = _reduce_scatter(blocks)
    small_mine = reduced[-1]
    small_buf = _into_slot(small_mine, me_chip, 4, F32, name="small_grads_slot")
    small_all = _allgather_shards([small_buf], name="ag_small_grads")[0].reshape(4 * small_rows, LANES)
    g_small_red = dict(zip(SMALL, _unpack(small_all, [w[nme] for nme in SMALL])))

    grads, delta, new_m, new_v = {}, {}, {}, {}
    for nme, red in zip(names, reduced[:-1]):
        shp = w[nme].shape
        res = _adamw(w[nme][0], red, m[nme][0], v[nme][0], name=f"adamw_{nme}")
        grads[nme], delta[nme], new_m[nme], new_v[nme] = (r.reshape(shp) for r in res)
    res = _adamw(w['w_mod'][0], g_w_mod, m['w_mod'][0], v['w_mod'][0], name="adamw_w_mod")
    grads['w_mod'], delta['w_mod'], new_m['w_mod'], new_v['w_mod'] = (r.reshape(w['w_mod'].shape) for r in res)
    rest = SMALL + ['c_ctx', 'b_mod']
    g_rest = dict(g_small_red, c_ctx=g_c_ctx, b_mod=g_b_mod)
    rows_rest = -(-sum(w[nme].size for nme in rest) // (LANES * 16)) * 16
    packed = [_pack([src[nme] for nme in rest], LANES, rows_rest) for src in (w, g_rest, m, v)]
    res = _adamw(*packed, name="adamw_small")
    for dst, buf in zip((grads, delta, new_m, new_v), res):
        dst.update(zip(rest, _unpack(buf, [w[nme] for nme in rest])))
    return (loss, grad_x, *[grads[nme] for nme in WEIGHTS], *[delta[nme] for nme in WEIGHTS],
            *[new_m[nme] for nme in WEIGHTS], *[new_v[nme] for nme in WEIGHTS])


def kernel(x, c, ctx, c_ctx, w_mod, b_mod, norm1, norm2, w_in, s5_a_re, s5_a_im, s5_log_dt, s5_b_re, s5_b_im, s5_c_re, s5_c_im, s5_d, w_glu, q_norm, kv_norm, w_uq, w_ukv, w_mla_o, w_out, w_ffn_in, w_ffn_out, norm_f, loss_target, m_c_ctx, m_w_mod, m_b_mod, m_norm1, m_norm2, m_w_in, m_s5_a_re, m_s5_a_im, m_s5_log_dt, m_s5_b_re, m_s5_b_im, m_s5_c_re, m_s5_c_im, m_s5_d, m_w_glu, m_q_norm, m_kv_norm, m_w_uq, m_w_ukv, m_w_mla_o, m_w_out, m_w_ffn_in, m_w_ffn_out, m_norm_f, v_c_ctx, v_w_mod, v_b_mod, v_norm1, v_norm2, v_w_in, v_s5_a_re, v_s5_a_im, v_s5_log_dt, v_s5_b_re, v_s5_b_im, v_s5_c_re, v_s5_c_im, v_s5_d, v_w_glu, v_q_norm, v_kv_norm, v_w_uq, v_w_ukv, v_w_mla_o, v_w_out, v_w_ffn_in, v_w_ffn_out, v_norm_f):
    w = dict(c_ctx=c_ctx, w_mod=w_mod, b_mod=b_mod, norm1=norm1, norm2=norm2, w_in=w_in, s5_a_re=s5_a_re, s5_a_im=s5_a_im,
             s5_log_dt=s5_log_dt, s5_b_re=s5_b_re, s5_b_im=s5_b_im, s5_c_re=s5_c_re, s5_c_im=s5_c_im, s5_d=s5_d, w_glu=w_glu,
             q_norm=q_norm, kv_norm=kv_norm, w_uq=w_uq, w_ukv=w_ukv, w_mla_o=w_mla_o, w_out=w_out, w_ffn_in=w_ffn_in,
             w_ffn_out=w_ffn_out, norm_f=norm_f)
    m = dict(c_ctx=m_c_ctx, w_mod=m_w_mod, b_mod=m_b_mod, norm1=m_norm1, norm2=m_norm2, w_in=m_w_in, s5_a_re=m_s5_a_re,
             s5_a_im=m_s5_a_im, s5_log_dt=m_s5_log_dt, s5_b_re=m_s5_b_re, s5_b_im=m_s5_b_im, s5_c_re=m_s5_c_re,
             s5_c_im=m_s5_c_im, s5_d=m_s5_d, w_glu=m_w_glu, q_norm=m_q_norm, kv_norm=m_kv_norm, w_uq=m_w_uq, w_ukv=m_w_ukv,
             w_mla_o=m_w_mla_o, w_out=m_w_out, w_ffn_in=m_w_ffn_in, w_ffn_out=m_w_ffn_out, norm_f=m_norm_f)
    v = dict(c_ctx=v_c_ctx, w_mod=v_w_mod, b_mod=v_b_mod, norm1=v_norm1, norm2=v_norm2, w_in=v_w_in, s5_a_re=v_s5_a_re,
             s5_a_im=v_s5_a_im, s5_log_dt=v_s5_log_dt, s5_b_re=v_s5_b_re, s5_b_im=v_s5_b_im, s5_c_re=v_s5_c_re,
             s5_c_im=v_s5_c_im, s5_d=v_s5_d, w_glu=v_w_glu, q_norm=v_q_norm, kv_norm=v_kv_norm, w_uq=v_w_uq, w_ukv=v_w_ukv,
             w_mla_o=v_w_mla_o, w_out=v_w_out, w_ffn_in=v_w_ffn_in, w_ffn_out=v_w_ffn_out, norm_f=v_norm_f)
    return _step(x, c, ctx, loss_target, w, m, v)
```

```python
import functools
import math

import jax
import jax.numpy as jnp
from jax import lax
from jax.experimental import pallas as pl
from jax.experimental.pallas import tpu as pltpu

F32 = jnp.float32
BF16 = jnp.bfloat16

EPS = 1e-6
GRID_W = 64
S5_GROUP = 16
S5_STATE = 64
MLA_HEADS = 8
QK_NOPE = 128
QK_ROPE = 64
V_DIM = 128
ROPE_BASE = 10000.0
ATTN_SCALE = (QK_NOPE + QK_ROPE) ** -0.5
ADAM_LR = 0.001
ADAM_B1 = 0.9
ADAM_B2 = 0.999
ADAM_EPS = 1e-08
ADAM_WD = 0.01
ADAM_STEP = 10

SUBLANES = 8
LANES = 128
V7X_VMEM_BYTES = 64 * 1024 * 1024
VMEM_LIMIT = (V7X_VMEM_BYTES * 7) // 8
N_SEG = SUBLANES
S5_BLOCK_GROUPS = 16
MESH = pl.DeviceIdType.MESH


def _pick(n, target, mult):
    best = None
    d = mult
    while d <= min(n, target):
        if n % d == 0:
            best = d
        d += mult
    return n if best is None else best


def _cparams(sem=None):
    return pltpu.CompilerParams(dimension_semantics=sem, vmem_limit_bytes=VMEM_LIMIT)


MM_VMEM_BUDGET = (V7X_VMEM_BYTES * 5) // 8


def _mm(a, b, *, ta=False, tb=False, out_dtype=F32, name, b_shards=1, out_shards=1):
    if ta:
        K, M = a.shape
    else:
        M, K = a.shape
    if tb:
        N, K2 = b.shape[-2], b.shape[-1] * b_shards
    else:
        K2, N = b.shape[-2], b.shape[-1] * b_shards
    assert K == K2, (a.shape, b.shape, ta, tb)
    n_unit = N // max(out_shards, 1 if tb else b_shards)
    k_unit = K // (b_shards if tb else 1)
    tn = _pick(n_unit, 1024, LANES)
    tm = _pick(M, 1024 if tn >= 512 else 2048, LANES if ta else 16)
    sa, sb, so = a.dtype.itemsize, b.dtype.itemsize, jnp.dtype(out_dtype).itemsize
    k_mult = LANES if (not ta or tb) else 16
    tk = k_mult if k_unit % k_mult == 0 else k_unit
    for cand in range(k_mult, k_unit + 1, k_mult):
        if k_unit % cand == 0 and 2 * cand * (tm * sa + tn * sb) + tm * tn * (4 + 2 * so) <= MM_VMEM_BUDGET:
            tk = cand
    nk = K // tk
    dims = (((0 if ta else 1,), (1 if tb else 0,)), ((), ()))

    def body(a_ref, b_ref, o_ref, *scratch):
        part = lax.dot_general(a_ref[...].astype(BF16), b_ref[...].astype(BF16), dims, preferred_element_type=F32)
        if nk == 1:
            o_ref[...] = part.astype(o_ref.dtype)
            return
        acc_ref, = scratch
        k = pl.program_id(2)

        @pl.when(k == 0)
        def _():
            acc_ref[...] = part

        @pl.when(k > 0)
        def _():
            acc_ref[...] += part

        @pl.when(k == nk - 1)
        def _():
            o_ref[...] = acc_ref[...].astype(o_ref.dtype)

    a_spec = pl.BlockSpec((tk, tm), lambda i, j, k: (k, i)) if ta else pl.BlockSpec((tm, tk), lambda i, j, k: (i, k))
    if b_shards == 1:
        b_spec = pl.BlockSpec((tn, tk), lambda i, j, k: (j, k)) if tb else pl.BlockSpec((tk, tn), lambda i, j, k: (k, j))
    elif tb:
        kpb = k_unit // tk
        b_spec = pl.BlockSpec((None, tn, tk), lambda i, j, k: (k // kpb, j, k % kpb))
    else:
        npb = n_unit // tn
        b_spec = pl.BlockSpec((None, tk, tn), lambda i, j, k: (j // npb, k, j % npb))
    if out_shards == 1:
        out_spec = pl.BlockSpec((tm, tn), lambda i, j, k: (i, j))
        out_shape = jax.ShapeDtypeStruct((M, N), out_dtype)
    else:
        opb = n_unit // tn
        out_spec = pl.BlockSpec((None, tm, tn), lambda i, j, k: (j // opb, i, j % opb))
        out_shape = jax.ShapeDtypeStruct((out_shards, M, N // out_shards), out_dtype)
    return pl.pallas_call(
        body, name=name, grid=(M // tm, N // tn, nk),
        in_specs=[a_spec, b_spec], out_specs=out_spec, out_shape=out_shape,
        scratch_shapes=[pltpu.VMEM((tm, tn), F32)] if nk > 1 else [],
        compiler_params=_cparams(("parallel", "parallel", "arbitrary")),
    )(a, b)


def _row_tile(tiled, extra_bytes=0):
    rows = tiled[0].shape[0]
    per_row = sum(a.shape[1] * 4 for a in tiled) + extra_bytes
    target = max(SUBLANES, (6 * 1024 * 1024) // max(per_row, 1))
    return _pick(rows, min(target, 512), 16)


def _rw(f, tiled, bcast, out_dtypes, *, name):
    nt, nb = len(tiled), len(bcast)
    rows = tiled[0].shape[0]
    outs_aval = jax.eval_shape(f, *[jax.ShapeDtypeStruct((16, a.shape[1]), F32) for a in tiled],
                               *[jax.ShapeDtypeStruct(b.shape, F32) for b in bcast])
    widths = [o.shape[1] for o in outs_aval]
    tm = _row_tile(tiled, sum(w * 4 for w in widths))

    def body(*refs):
        tin = [r[...].astype(F32) for r in refs[:nt]]
        bin_ = [r[...].astype(F32) for r in refs[nt:nt + nb]]
        outs = f(*tin, *bin_)
        for o_ref, o in zip(refs[nt + nb:], outs):
            o_ref[...] = o.astype(o_ref.dtype)

    in_specs = [pl.BlockSpec((tm, a.shape[1]), lambda i: (i, 0)) for a in tiled]
    in_specs += [pl.BlockSpec(b.shape, lambda i: (0, 0)) for b in bcast]
    res = pl.pallas_call(
        body, name=name, grid=(rows // tm,), in_specs=in_specs,
        out_specs=[pl.BlockSpec((tm, w), lambda i: (i, 0)) for w in widths],
        out_shape=[jax.ShapeDtypeStruct((rows, w), dt) for w, dt in zip(widths, out_dtypes)],
        compiler_params=_cparams(("parallel",)),
    )(*tiled, *bcast)
    return list(res)


def _rw_vjp(f, tiled, bcast, cts, need_t, need_b, t_dtypes, *, name):
    nt, nb = len(tiled), len(bcast)
    rows = tiled[0].shape[0]
    flat_cts = [c for group in cts for c in group]
    t_idx = [i for i in range(nt) if need_t[i]]
    b_idx = [i for i in range(nb) if need_b[i]]
    tm = _row_tile(list(tiled) + flat_cts, sum(tiled[i].shape[1] * 4 for i in t_idx))
    nc = len(flat_cts)

    def body(*refs):
        i = pl.program_id(0)
        tin = [r[...].astype(F32) for r in refs[:nt]]
        bin_ = [r[...].astype(F32) for r in refs[nt:nt + nb]]
        ct_refs = refs[nt + nb:nt + nb + nc]
        out_refs = refs[nt + nb + nc:]
        outs, vjp_fn = jax.vjp(f, *tin, *bin_)
        ct_vals, pos = [], 0
        for o, group in zip(outs, cts):
            acc = jnp.zeros_like(o)
            for _ in group:
                acc = acc + ct_refs[pos][...].astype(F32)
                pos += 1
            ct_vals.append(acc)
        grads = vjp_fn(tuple(ct_vals))
        for o_ref, k in zip(out_refs[:len(t_idx)], t_idx):
            o_ref[...] = grads[k].astype(o_ref.dtype)
        for o_ref, k in zip(out_refs[len(t_idx):], b_idx):
            @pl.when(i == 0)
            def _(o_ref=o_ref):
                o_ref[...] = jnp.zeros_like(o_ref)

            o_ref[...] += grads[nt + k]

    in_specs = [pl.BlockSpec((tm, a.shape[1]), lambda i: (i, 0)) for a in tiled]
    in_specs += [pl.BlockSpec(b.shape, lambda i: (0, 0)) for b in bcast]
    in_specs += [pl.BlockSpec((tm, c.shape[1]), lambda i: (i, 0)) for c in flat_cts]
    out_specs = [pl.BlockSpec((tm, tiled[k].shape[1]), lambda i: (i, 0)) for k in t_idx]
    out_specs += [pl.BlockSpec(bcast[k].shape, lambda i: (0, 0)) for k in b_idx]
    out_shape = [jax.ShapeDtypeStruct(tiled[k].shape, dt) for k, dt in zip(t_idx, t_dtypes)]
    out_shape += [jax.ShapeDtypeStruct(bcast[k].shape, F32) for k in b_idx]
    res = pl.pallas_call(
        body, name=name, grid=(rows // tm,), in_specs=in_specs, out_specs=out_specs, out_shape=out_shape,
        compiler_params=_cparams(("arbitrary",)),
    )(*tiled, *bcast, *flat_cts)
    res = list(res)
    return res[:len(t_idx)], res[len(t_idx):]


def _rms(x, g):
    return x * lax.rsqrt(jnp.mean(x * x, axis=-1, keepdims=True) + EPS) * g


def _f_norm_mod(x, g, sc, sh):
    return (_rms(x, g) * (1.0 + sc) + sh,)


def _f_norm_mod_keep(x, g, sc, sh):
    return (_rms(x, g) * (1.0 + sc) + sh, x)


@jax.custom_vjp
def _swap16(x):
    w = x.shape[-1]
    lane = lax.broadcasted_iota(jnp.int32, x.shape, x.ndim - 1)
    return jnp.where((lane & 16) == 0, pltpu.roll(x, w - 16, x.ndim - 1), pltpu.roll(x, 16, x.ndim - 1))


_swap16.defvjp(lambda x: (_swap16(x), None), lambda _, g: (_swap16(g),))


def _rope(x, cos, sin):
    return x * cos + _swap16(x) * sin


def _make_f_post_in(sw, q_rank, kv_rank, with_q):
    o1, o2, o3 = sw, sw + q_rank, sw + q_rank + kv_rank

    if with_q:
        def f(ha, cos, sin, qg, kvg):
            u = ha[:, :o1]
            cqn = _rms(ha[:, o1:o2], qg)
            ckvn = _rms(ha[:, o2:o3], kvg)
            kr = _rope(ha[:, o3:o3 + LANES], cos, sin)
            return u, cqn, ckvn, kr
    else:
        def f(ha, kvg):
            return ha[:, :o1], _rms(ha[:, o2:o3], kvg), ha[:, o3:o3 + LANES]
    return f


def _f_qpost(q2, cos, sin):
    w = q2.shape[1] // 2
    reps = w // LANES
    qr = _rope(q2[:, w:], jnp.tile(cos, (1, reps)), jnp.tile(sin, (1, reps)))
    return (jnp.concatenate([q2[:, :w], qr], axis=1),)


def _f_s5post(u, r, d):
    return (jax.nn.gelu(d * u + r, approximate=True),)


def _f_merge(ab, bm, gt):
    d = bm.shape[1]
    br_s5 = ab[:, :d] * jax.nn.sigmoid(ab[:, d:])
    g = jax.nn.sigmoid(gt)
    return (g[:, :d] * br_s5 + g[:, d:] * bm,)


def _f_resid_norm(x, out, g1, n2, sc2, sh2):
    x1 = x + g1 * out
    return x1, _rms(x1, n2) * (1.0 + sc2) + sh2


def _f_swiglu(ab):
    d = ab.shape[1] // 2
    return (jax.nn.silu(ab[:, :d]) * ab[:, d:],)


def _f_final(x1, f, tgt, g2, nf):
    y = _rms(x1 + g2 * f, nf)
    return (0.5 * jnp.mean(jnp.square(y - tgt), axis=-1, keepdims=True),)


def _bd_fanout(x, ws, *, name):
    nw = len(ws)
    nb, kb, nn = ws[0].shape
    T = x.shape[0]
    tm = _pick(T, 512, 16)

    def body(*refs):
        xb = refs[0][...].astype(BF16)
        for w_ref, o_ref in zip(refs[1:1 + nw], refs[1 + nw:]):
            o_ref[...] = jnp.dot(xb, w_ref[0].astype(BF16), preferred_element_type=F32)

    return list(pl.pallas_call(
        body, name=name, grid=(nb, T // tm),
        in_specs=[pl.BlockSpec((tm, kb), lambda j, i: (i, j))] + [pl.BlockSpec((1, kb, nn), lambda j, i: (j, 0, 0))] * nw,
        out_specs=[pl.BlockSpec((tm, nn), lambda j, i: (i, j))] * nw,
        out_shape=[jax.ShapeDtypeStruct((T, nb * nn), F32)] * nw,
        compiler_params=_cparams(("parallel", "parallel")),
    )(x, *ws))


def _bd_fanin(xs, ws, *, name):
    nw = len(ws)
    nb, kb, nn = ws[0].shape
    T = xs[0].shape[0]
    tm = _pick(T, 512, 16)

    def body(*refs):
        acc = None
        for x_ref, w_ref in zip(refs[:nw], refs[nw:2 * nw]):
            t = jnp.dot(x_ref[...].astype(BF16), w_ref[0].astype(BF16), preferred_element_type=F32)
            acc = t if acc is None else acc + t
        refs[2 * nw][...] = acc

    return pl.pallas_call(
        body, name=name, grid=(nb, T // tm),
        in_specs=[pl.BlockSpec((tm, kb), lambda j, i: (i, j))] * nw + [pl.BlockSpec((1, kb, nn), lambda j, i: (j, 0, 0))] * nw,
        out_specs=pl.BlockSpec((tm, nn), lambda j, i: (i, j)),
        out_shape=jax.ShapeDtypeStruct((T, nb * nn), F32),
        compiler_params=_cparams(("parallel", "parallel")),
    )(*xs, *ws)


def _bd_dw(xs, dys, nb, *, name):
    npair = len(xs)
    T = xs[0].shape[0]
    kb = xs[0].shape[1] // nb
    nn = dys[0].shape[1] // nb
    tm = _pick(T, 512, 16)
    dims = (((0,), (0,)), ((), ()))

    def body(*refs):
        i = pl.program_id(1)
        for x_ref, d_ref, o_ref in zip(refs[:npair], refs[npair:2 * npair], refs[2 * npair:]):
            @pl.when(i == 0)
            def _(o_ref=o_ref):
                o_ref[...] = jnp.zeros_like(o_ref)

            o_ref[0] += lax.dot_general(x_ref[...].astype(BF16), d_ref[...].astype(BF16), dims,
                                        preferred_element_type=F32)

    return list(pl.pallas_call(
        body, name=name, grid=(nb, T // tm),
        in_specs=[pl.BlockSpec((tm, kb), lambda j, i: (i, j))] * npair + [pl.BlockSpec((tm, nn), lambda j, i: (i, j))] * npair,
        out_specs=[pl.BlockSpec((1, kb, nn), lambda j, i: (j, 0, 0))] * npair,
        out_shape=[jax.ShapeDtypeStruct((nb, kb, nn), F32)] * npair,
        compiler_params=_cparams(("parallel", "arbitrary")),
    )(*xs, *dys))


def _cmul(ar, ai, br, bi):
    return ar * br - ai * bi, ar * bi + ai * br


def _cpow(lr, li, n):
    rr, ri = None, None
    br, bi = lr, li
    while n:
        if n & 1:
            rr, ri = (br, bi) if rr is None else _cmul(rr, ri, br, bi)
        n >>= 1
        if n:
            br, bi = _cmul(br, bi, br, bi)
    return rr, ri


def _s5_scan(b_re, b_im, lam_re, lam_im, h0_re, h0_im, e0_re, e0_im, *, reverse, name):
    rows, C = b_re.shape
    n = rows // N_SEG
    cb = _pick(C, 512, LANES)
    seg_order = list(range(N_SEG))[::-1] if reverse else list(range(N_SEG))
    s_first, s_last = seg_order[0], seg_order[-1]

    def body(br_ref, bi_ref, lr_ref, li_ref, h0r_ref, h0i_ref, e0r_ref, e0i_ref, hr_ref, hi_ref, htr_ref, hti_ref):
        shape = (N_SEG, cb)
        lr = jnp.broadcast_to(lr_ref[...], shape)
        li = jnp.broadcast_to(li_ref[...], shape)
        row = lax.broadcasted_iota(jnp.int32, shape, 0)

        def step_of(k):
            return (n - 1 - k) if reverse else k

        def rows_of(k):
            return pl.ds(pl.multiple_of(step_of(k) * N_SEG, N_SEG), N_SEG)

        first = row == s_first
        hr = br_ref[rows_of(0), :] + jnp.where(first, e0r_ref[...], 0.0)
        hi = bi_ref[rows_of(0), :] + jnp.where(first, e0i_ref[...], 0.0)
        hr_ref[rows_of(0), :] = hr
        hi_ref[rows_of(0), :] = hi

        def pass1(k, carry):
            hr, hi = carry
            pr, pi = _cmul(lr, li, hr, hi)
            hr = pr + br_ref[rows_of(k), :]
            hi = pi + bi_ref[rows_of(k), :]
            hr_ref[rows_of(k), :] = hr
            hi_ref[rows_of(k), :] = hi
            return hr, hi

        er, ei = lax.fori_loop(1, n, pass1, (hr, hi))

        lnr, lni = _cpow(lr_ref[...], li_ref[...], n)
        cr, ci = h0r_ref[...], h0i_ref[...]
        cin_r = jnp.zeros(shape, F32)
        cin_i = jnp.zeros(shape, F32)
        for s in seg_order:
            cin_r = jnp.where(row == s, cr, cin_r)
            cin_i = jnp.where(row == s, ci, cin_i)
            if s != s_last:
                pr, pi = _cmul(lnr, lni, cr, ci)
                cr = pr + jnp.sum(jnp.where(row == s, er, 0.0), axis=0, keepdims=True)
                ci = pi + jnp.sum(jnp.where(row == s, ei, 0.0), axis=0, keepdims=True)

        def pass2(k, carry):
            pr, pi = carry
            ar, ai = _cmul(pr, pi, cin_r, cin_i)
            hr = hr_ref[rows_of(k), :] + ar
            hi = hi_ref[rows_of(k), :] + ai
            hr_ref[rows_of(k), :] = hr
            hi_ref[rows_of(k), :] = hi
            npr, npi = _cmul(pr, pi, lr, li)
            return npr, npi

        lax.fori_loop(0, n, pass2, (lr, li))
        last_r = hr_ref[rows_of(n - 1), :]
        last_i = hi_ref[rows_of(n - 1), :]
        htr_ref[...] = jnp.sum(jnp.where(row == s_last, last_r, 0.0), axis=0, keepdims=True)
        hti_ref[...] = jnp.sum(jnp.where(row == s_last, last_i, 0.0), axis=0, keepdims=True)

    big = pl.BlockSpec((rows, cb), lambda j: (0, j))
    vec = pl.BlockSpec((1, cb), lambda j: (0, j))
    return pl.pallas_call(
        body, name=name, grid=(C // cb,),
        in_specs=[big, big] + [vec] * 6,
        out_specs=[big, big, vec, vec],
        out_shape=[jax.ShapeDtypeStruct((rows, C), F32)] * 2 + [jax.ShapeDtypeStruct((1, C), F32)] * 2,
        compiler_params=_cparams(("parallel",)),
    )(b_re, b_im, lam_re, lam_im, h0_re, h0_im, e0_re, e0_im)


def _s5_dlam(mu_re, mu_im, h_re, h_im, h0_re, h0_im, *, reverse, name):
    rows, C = h_re.shape
    n = rows // N_SEG
    cb = _pick(C, 512, LANES)
    s_first = N_SEG - 1 if reverse else 0

    def body(mr_ref, mi_ref, hr_ref, hi_ref, h0r_ref, h0i_ref, dr_ref, di_ref):
        shape = (N_SEG, cb)
        row = lax.broadcasted_iota(jnp.int32, shape, 0)

        def rows_of(k):
            step = (n - 1 - k) if reverse else k
            return pl.ds(pl.multiple_of(step * N_SEG, N_SEG), N_SEG)

        def term(k, pr, pi):
            mr, mi = mr_ref[rows_of(k), :], mi_ref[rows_of(k), :]
            return mr * pr + mi * pi, mi * pr - mr * pi

        shift = N_SEG - 1 if reverse else 1
        pr = jnp.where(row == s_first, h0r_ref[...], pltpu.roll(hr_ref[rows_of(n - 1), :], shift, 0))
        pi = jnp.where(row == s_first, h0i_ref[...], pltpu.roll(hi_ref[rows_of(n - 1), :], shift, 0))
        acc = term(0, pr, pi)

        def loop(k, acc):
            tr, ti = term(k, hr_ref[rows_of(k - 1), :], hi_ref[rows_of(k - 1), :])
            return acc[0] + tr, acc[1] + ti

        ar, ai = lax.fori_loop(1, n, loop, acc)
        dr_ref[...] = jnp.sum(ar, axis=0, keepdims=True)
        di_ref[...] = jnp.sum(ai, axis=0, keepdims=True)

    big = pl.BlockSpec((rows, cb), lambda j: (0, j))
    vec = pl.BlockSpec((1, cb), lambda j: (0, j))
    return pl.pallas_call(
        body, name=name, grid=(C // cb,),
        in_specs=[big] * 4 + [vec] * 2, out_specs=[vec, vec],
        out_shape=[jax.ShapeDtypeStruct((1, C), F32)] * 2,
        compiler_params=_cparams(("parallel",)),
    )(mu_re, mu_im, h_re, h_im, h0_re, h0_im)


def _attn_scores(qn, qr, kn, kr):
    nt = (((1,), (1,)), ((), ()))
    s = lax.dot_general(qn, kn, nt, preferred_element_type=F32) + lax.dot_general(qr, kr, nt, preferred_element_type=F32)
    s = s * ATTN_SCALE
    p = jnp.exp(s - jnp.max(s, axis=-1, keepdims=True))
    return p / jnp.sum(p, axis=-1, keepdims=True)


def _attn_specs(L, T, tq):
    H = MLA_HEADS
    return [
        pl.BlockSpec((tq, LANES), lambda h, i: (i, h)),
        pl.BlockSpec((tq, LANES), lambda h, i: (i, H + h)),
        pl.BlockSpec((T, LANES), lambda h, i: (0, 2 * h)),
        pl.BlockSpec((T, LANES), lambda h, i: (0, 2 * h + 1)),
        pl.BlockSpec((T, LANES), lambda h, i: (0, 0)),
    ]


def _attn_fwd(qq, kv, kr, *, name):
    L, T = qq.shape[0], kv.shape[0]
    tq = _pick(L, 256, 16)

    def body(qn_ref, qr_ref, kn_ref, v_ref, kr_ref, o_ref):
        p = _attn_scores(qn_ref[...], qr_ref[...], kn_ref[...], kr_ref[...])
        o_ref[...] = jnp.dot(p.astype(BF16), v_ref[...], preferred_element_type=F32).astype(o_ref.dtype)

    return pl.pallas_call(
        body, name=name, grid=(MLA_HEADS, L // tq), in_specs=_attn_specs(L, T, tq),
        out_specs=pl.BlockSpec((tq, LANES), lambda h, i: (i, h)),
        out_shape=jax.ShapeDtypeStruct((L, MLA_HEADS * V_DIM), BF16),
        compiler_params=_cparams(("parallel", "parallel")),
    )(qq, qq, kv, kv, kr)


def _attn_bwd(qq, kv, kr, do, *, name):
    L, T = qq.shape[0], kv.shape[0]
    H = MLA_HEADS
    tq = _pick(L, 256, 16)
    tn = (((0,), (0,)), ((), ()))
    nt = (((1,), (1,)), ((), ()))

    def body(qn_ref, qr_ref, kn_ref, v_ref, kr_ref, do_ref, dqn_ref, dqr_ref, dkn_ref, dv_ref, dkr_ref, dkn_acc, dv_acc):
        h, i = pl.program_id(0), pl.program_id(1)
        qn, qr, kn, v, krv, dov = qn_ref[...], qr_ref[...], kn_ref[...], v_ref[...], kr_ref[...], do_ref[...]
        p = _attn_scores(qn, qr, kn, krv)
        pb = p.astype(BF16)
        dp = lax.dot_general(dov, v, nt, preferred_element_type=F32)
        ds = (p * (dp - jnp.sum(dp * p, axis=-1, keepdims=True)) * ATTN_SCALE).astype(BF16)
        dqn_ref[...] = jnp.dot(ds, kn, preferred_element_type=F32)
        dqr_ref[...] = jnp.dot(ds, krv, preferred_element_type=F32)

        @pl.when(i == 0)
        def _():
            dkn_acc[...] = jnp.zeros_like(dkn_acc)
            dv_acc[...] = jnp.zeros_like(dv_acc)

        @pl.when((i == 0) & (h == 0))
        def _():
            dkr_ref[...] = jnp.zeros_like(dkr_ref)

        dv_acc[...] += lax.dot_general(pb, dov, tn, preferred_element_type=F32)
        dkn_acc[...] += lax.dot_general(ds, qn, tn, preferred_element_type=F32)
        dkr_ref[...] += lax.dot_general(ds, qr, tn, preferred_element_type=F32)

        @pl.when(i == pl.num_programs(1) - 1)
        def _():
            dkn_ref[...] = dkn_acc[...].astype(dkn_ref.dtype)
            dv_ref[...] = dv_acc[...].astype(dv_ref.dtype)

    in_specs = _attn_specs(L, T, tq) + [pl.BlockSpec((tq, LANES), lambda h, i: (i, h))]
    dqn, dqr, dkn, dv, dkr = pl.pallas_call(
        body, name=name, grid=(H, L // tq), in_specs=in_specs,
        out_specs=[pl.BlockSpec((tq, LANES), lambda h, i: (i, h)), pl.BlockSpec((tq, LANES), lambda h, i: (i, h)),
                   pl.BlockSpec((T, LANES), lambda h, i: (0, h)), pl.BlockSpec((T, LANES), lambda h, i: (0, h)),
                   pl.BlockSpec((T, LANES), lambda h, i: (0, 0))],
        out_shape=[jax.ShapeDtypeStruct((L, H * LANES), F32), jax.ShapeDtypeStruct((L, H * LANES), F32),
                   jax.ShapeDtypeStruct((T, H * LANES), BF16), jax.ShapeDtypeStruct((T, H * LANES), BF16),
                   jax.ShapeDtypeStruct((T, LANES), F32)],
        scratch_shapes=[pltpu.VMEM((T, LANES), F32), pltpu.VMEM((T, LANES), F32)],
        compiler_params=_cparams(("arbitrary", "arbitrary")),
    )(qq, qq, kv, kv, kr, do)
    return dqn, dqr, dkn, dv, dkr


def _adamw(w, g, m, v, *, name):
    c1 = 1.0 - ADAM_B1 ** ADAM_STEP
    c2 = 1.0 - ADAM_B2 ** ADAM_STEP

    def f(w, g, m, v):
        m = ADAM_B1 * m + (1.0 - ADAM_B1) * g
        v = ADAM_B2 * v + (1.0 - ADAM_B2) * jnp.square(g)
        delta = -ADAM_LR * ((m / c1) / (jnp.sqrt(v / c2) + ADAM_EPS) + ADAM_WD * w)
        return g, delta, m, v

    return _rw(f, [w, g, m, v], [], [F32] * 4, name=name)


def _slab_rows(rows, cols, n_arrays):
    return _pick(rows, max(16, (8 * 1024 * 1024) // (cols * 4 * n_arrays)), 16)


def _scalars(*vals):
    return jnp.stack([jnp.asarray(v, jnp.int32) for v in vals])


def _into_slot(src, slot, nslots, dtype, *, name):
    R, C = src.shape
    tr = _slab_rows(R, C, 2)

    def body(s_ref, x_ref, o_ref):
        o_ref[...] = x_ref[...].astype(o_ref.dtype)

    return pl.pallas_call(
        body, name=name,
        grid_spec=pltpu.PrefetchScalarGridSpec(
            num_scalar_prefetch=1, grid=(R // tr,),
            in_specs=[pl.BlockSpec((tr, C), lambda i, s: (i, 0))],
            out_specs=pl.BlockSpec((None, tr, C), lambda i, s: (s[0], i, 0))),
        out_shape=jax.ShapeDtypeStruct((nslots, R, C), dtype),
        compiler_params=_cparams(("arbitrary",)),
    )(_scalars(slot), src)


def _pair_sum(g, got, c, *, name):
    _, R, C = g.shape
    hr = R // 2
    tr = _slab_rows(hr, C, 3)
    nblk = hr // tr

    def body(s_ref, g_ref, r_ref, o_ref):
        o_ref[...] = (g_ref[...].astype(F32) + r_ref[...].astype(F32)).astype(o_ref.dtype)

    return pl.pallas_call(
        body, name=name,
        grid_spec=pltpu.PrefetchScalarGridSpec(
            num_scalar_prefetch=1, grid=(4, nblk),
            in_specs=[pl.BlockSpec((None, tr, C), lambda j, i, s: (j, s[0] * nblk + i, 0)),
                      pl.BlockSpec((None, tr, C), lambda j, i, s: (j, i, 0))],
            out_specs=pl.BlockSpec((None, tr, C), lambda j, i, s: (j, i, 0))),
        out_shape=jax.ShapeDtypeStruct((4, hr, C), g.dtype),
        compiler_params=_cparams(("arbitrary", "arbitrary")),
    )(_scalars(c), g, got)


def _chip_sum(p, landed, me_chip, c, *, name):
    _, hr, C = p.shape
    tr = _slab_rows(hr, C, 5)

    def body(s_ref, p_ref, l0_ref, l1_ref, l2_ref, o_ref):
        o_ref[...] = ((p_ref[...].astype(F32) + l0_ref[...].astype(F32)) + l1_ref[...].astype(F32)) + l2_ref[...].astype(F32)

    return pl.pallas_call(
        body, name=name,
        grid_spec=pltpu.PrefetchScalarGridSpec(
            num_scalar_prefetch=1, grid=(hr // tr,),
            in_specs=[pl.BlockSpec((None, tr, C), lambda i, s: (s[0], i, 0))]
            + [pl.BlockSpec((None, tr, C), functools.partial(lambda i, s, k: (k, i, 0), k=k)) for k in range(3)],
            out_specs=pl.BlockSpec((None, tr, C), lambda i, s: (s[1], i, 0))),
        out_shape=jax.ShapeDtypeStruct((2, hr, C), F32),
        compiler_params=_cparams(("arbitrary",)),
    )(_scalars(me_chip, c), p, landed, landed, landed)


def _place():
    return lax.axis_index("x"), lax.axis_index("y"), lax.axis_index("c")


def _other_chips(x, y):
    chips = [(1 - x, y), (x, 1 - y), (1 - x, 1 - y)]
    return chips, [2 * cx + cy for cx, cy in chips]


HBM = pl.BlockSpec(memory_space=pl.ANY)


def _allgather8(v, *, name):
    rows, cols = v.shape

    def body(v_ref, out_ref, send_sems, recv_sems):
        x, y, c = _place()
        me = 4 * x + 2 * y + c
        out_ref[me] = v_ref[...]
        copies = []
        for k in range(1, 8):
            bx, by, bc = (k >> 2) & 1, (k >> 1) & 1, k & 1
            px, py, pc = x ^ bx, y ^ by, c ^ bc
            cp = pltpu.make_async_remote_copy(
                src_ref=v_ref, dst_ref=out_ref.at[me], send_sem=send_sems.at[k - 1], recv_sem=recv_sems.at[k - 1],
                device_id=(px, py, pc), device_id_type=MESH)
            cp.start()
            copies.append((cp, 4 * px + 2 * py + pc))
        for k, (cp, peer) in enumerate(copies):
            pltpu.make_async_remote_copy(
                src_ref=v_ref, dst_ref=out_ref.at[peer], send_sem=send_sems.at[k], recv_sem=recv_sems.at[k],
                device_id=(x, y, c), device_id_type=MESH).wait_recv()
        for cp, _ in copies:
            cp.wait_send()

    return pl.pallas_call(
        body, name=name, out_shape=jax.ShapeDtypeStruct((8, rows, cols), v.dtype),
        in_specs=[pl.BlockSpec(memory_space=pltpu.VMEM)], out_specs=pl.BlockSpec(memory_space=pltpu.VMEM),
        scratch_shapes=[pltpu.SemaphoreType.DMA((7,)), pltpu.SemaphoreType.DMA((7,))],
        compiler_params=pltpu.CompilerParams(vmem_limit_bytes=VMEM_LIMIT),
    )(v)


def _allgather_shards(bufs, *, name):
    n = len(bufs)

    def body(*refs):
        outs = refs[n:2 * n]
        send_sems, recv_sems = refs[2 * n:]
        x, y, c = _place()
        me_chip = 2 * x + y
        sibling = (x, y, 1 - c)
        chips, chip_ids = _other_chips(x, y)

        def remote(k, j, blk, hf, to):
            hr = bufs[k].shape[1] // 2
            piece = outs[k].at[blk, pl.ds(pl.multiple_of(hf * hr, 16), hr), :]
            return pltpu.make_async_remote_copy(
                src_ref=piece, dst_ref=piece, send_sem=send_sems.at[6 * k + j], recv_sem=recv_sems.at[6 * k + j],
                device_id=to, device_id_type=MESH)

        sends = []
        for k in range(n):
            for j, chip in enumerate(chips):
                cp = remote(k, j, me_chip, c, (*chip, c))
                cp.start()
                sends.append(cp)
        for k in range(n):
            for j, chip in enumerate(chips):
                remote(k, j, chip_ids[j], c, (x, y, c)).wait_recv()
                cp = remote(k, 3 + j, chip_ids[j], c, sibling)
                cp.start()
                sends.append(cp)
        for k in range(n):
            for j in range(3):
                remote(k, 3 + j, chip_ids[j], 1 - c, (x, y, c)).wait_recv()
        for cp in sends:
            cp.wait_send()

    return list(pl.pallas_call(
        body, name=name, out_shape=[jax.ShapeDtypeStruct(b.shape, b.dtype) for b in bufs],
        in_specs=[HBM] * n, out_specs=[HBM] * n, input_output_aliases={k: k for k in range(n)},
        scratch_shapes=[pltpu.SemaphoreType.DMA((6 * n,)), pltpu.SemaphoreType.DMA((6 * n,))],
    )(*bufs))


def _pair_exchange(gs, *, name):
    n = len(gs)

    def body(*refs):
        ins, outs = refs[:n], refs[n:2 * n]
        send_sems, recv_sems = refs[2 * n:]
        x, y, c = _place()
        copies = []
        for k in range(n):
            hr = gs[k].shape[1] // 2
            src = ins[k].at[:, pl.ds(pl.multiple_of((1 - c) * hr, 16), hr), :]
            cp = pltpu.make_async_remote_copy(src_ref=src, dst_ref=outs[k], send_sem=send_sems.at[k], recv_sem=recv_sems.at[k],
                                              device_id=(x, y, 1 - c), device_id_type=MESH)
            cp.start()
            copies.append(cp)
        for cp in copies:
            cp.wait()

    return list(pl.pallas_call(
        body, name=name,
        out_shape=[jax.ShapeDtypeStruct((4, g.shape[1] // 2, g.shape[2]), g.dtype) for g in gs],
        in_specs=[HBM] * n, out_specs=[HBM] * n,
        scratch_shapes=[pltpu.SemaphoreType.DMA((n,)), pltpu.SemaphoreType.DMA((n,))],
    )(*gs))


def _pair_gather(bufs, *, name):
    n = len(bufs)

    def body(*refs):
        outs = refs[n:2 * n]
        send_sems, recv_sems = refs[2 * n:]
        x, y, c = _place()

        def remote(k, hf, to):
            return pltpu.make_async_remote_copy(src_ref=outs[k].at[hf], dst_ref=outs[k].at[hf], send_sem=send_sems.at[k],
                                                recv_sem=recv_sems.at[k], device_id=to, device_id_type=MESH)

        copies = [remote(k, c, (x, y, 1 - c)) for k in range(n)]
        for cp in copies:
            cp.start()
        for k, cp in enumerate(copies):
            cp.wait_send()
            remote(k, 1 - c, (x, y, c)).wait_recv()

    return list(pl.pallas_call(
        body, name=name, out_shape=[jax.ShapeDtypeStruct(b.shape, b.dtype) for b in bufs],
        in_specs=[HBM] * n, out_specs=[HBM] * n, input_output_aliases={k: k for k in range(n)},
        scratch_shapes=[pltpu.SemaphoreType.DMA((n,)), pltpu.SemaphoreType.DMA((n,))],
    )(*bufs))


HBM_SPEC = pl.BlockSpec(memory_space=pltpu.HBM)
SEM_SPEC = pl.BlockSpec(memory_space=pltpu.SEMAPHORE)
EFFECT = pltpu.SideEffectType.DATAFLOW_SIDE_EFFECTING
TOKEN = jax.ShapeDtypeStruct((SUBLANES, LANES), F32)


def _in_hbm(a):
    return pltpu.with_memory_space_constraint(a, pltpu.HBM)


def _ici_copies(srcs, dsts, send_sems, recv_sems, send):
    x, y, c = _place()
    me_chip = 2 * x + y
    chips, chip_ids = _other_chips(x, y)
    out = []
    for k, (src, dst) in enumerate(zip(srcs, dsts)):
        for j, chip in enumerate(chips):
            s_ref, d_ref = (src(k, me_chip, chip_ids[j], j), dst(k, me_chip, chip_ids[j], j))
            out.append(pltpu.make_async_remote_copy(
                src_ref=s_ref if send else d_ref, dst_ref=d_ref, send_sem=send_sems.at[3 * k + j],
                recv_sem=recv_sems.at[3 * k + j], device_id=(*chip, c) if send else (x, y, c), device_id_type=MESH))
    return out


def _half_rows(buf, hf):
    hr = buf.shape[1] // 2
    return pl.ds(pl.multiple_of(hf * hr, 16), hr)


def _ag_pieces(refs):
    c = lax.axis_index("c")
    src = [functools.partial(lambda k, me, other, j, r: r.at[me, _half_rows(r, c), :], r=r) for r in refs]
    dst_send = src
    dst_recv = [functools.partial(lambda k, me, other, j, r: r.at[other, _half_rows(r, c), :], r=r) for r in refs]
    return src, dst_send, dst_recv


def _ag_start(bufs, groups, *, name):
    n, ng = len(bufs), len(groups)

    def body(*refs):
        sems = refs[n:n + 2 * ng]
        thru = refs[n + 2 * ng:2 * n + 2 * ng]
        token = refs[-1]
        for g, ks in enumerate(groups):
            src, dst_send, _ = _ag_pieces([thru[k] for k in ks])
            for cp in _ici_copies(src, dst_send, sems[2 * g], sems[2 * g + 1], True):
                cp.start()
        token[...] = jnp.zeros_like(token)

    out_shape = tuple(pltpu.SemaphoreType.DMA((3 * len(ks),)) for ks in groups for _ in range(2))
    out_shape += tuple(pltpu.HBM(b.shape, b.dtype) for b in bufs) + (TOKEN,)
    res = pl.pallas_call(
        body, name=name, out_shape=out_shape, in_specs=(HBM_SPEC,) * n,
        out_specs=(SEM_SPEC,) * (2 * ng) + (HBM_SPEC,) * n + (pl.BlockSpec(memory_space=pltpu.VMEM),),
        input_output_aliases={k: 2 * ng + k for k in range(n)},
        compiler_params=pltpu.CompilerParams(has_side_effects=EFFECT),
    )(*[_in_hbm(b) for b in bufs])
    sems = [(res[2 * g], res[2 * g + 1]) for g in range(ng)]
    return sems, list(res[2 * ng:2 * ng + n]), res[-1]


def _ag_wait(bufs, send_sems, recv_sems, after, *, name):
    n = len(bufs)

    def body(*refs):
        ins = refs[:n]
        send, recv = refs[n], refs[n + 1]
        src, dst_send, dst_recv = _ag_pieces(ins)
        for cp in _ici_copies(src, dst_send, send, recv, True):
            cp.wait_send()
        for cp in _ici_copies(src, dst_recv, send, recv, False):
            cp.wait_recv()

    return list(pl.pallas_call(
        body, name=name, out_shape=tuple(pltpu.HBM(b.shape, b.dtype) for b in bufs),
        in_specs=(HBM_SPEC,) * n + (SEM_SPEC, SEM_SPEC, pl.BlockSpec(memory_space=pl.ANY)),
        out_specs=(HBM_SPEC,) * n, input_output_aliases={k: k for k in range(n)},
        compiler_params=pltpu.CompilerParams(has_side_effects=EFFECT),
    )(*bufs, send_sems, recv_sems, after))


def _ag_forward(bufs, *, name):
    n = len(bufs)

    def body(*refs):
        outs = refs[n:2 * n]
        send_sems, recv_sems = refs[2 * n:]
        x, y, c = _place()
        _, chip_ids = _other_chips(x, y)

        def remote(k, j, hf, to):
            piece = outs[k].at[chip_ids[j], _half_rows(outs[k], hf), :]
            return pltpu.make_async_remote_copy(src_ref=piece, dst_ref=piece, send_sem=send_sems.at[3 * k + j],
                                                recv_sem=recv_sems.at[3 * k + j], device_id=to, device_id_type=MESH)

        sends = [remote(k, j, c, (x, y, 1 - c)) for k in range(n) for j in range(3)]
        for cp in sends:
            cp.start()
        for k in range(n):
            for j in range(3):
                remote(k, j, 1 - c, (x, y, c)).wait_recv()
        for cp in sends:
            cp.wait_send()

    return list(pl.pallas_call(
        body, name=name, out_shape=[jax.ShapeDtypeStruct(b.shape, b.dtype) for b in bufs],
        in_specs=[HBM] * n, out_specs=[HBM] * n, input_output_aliases={k: k for k in range(n)},
        scratch_shapes=[pltpu.SemaphoreType.DMA((3 * n,)), pltpu.SemaphoreType.DMA((3 * n,))],
    )(*bufs))


def _rs_pieces(p_refs, l_refs):
    src = [functools.partial(lambda k, me, other, j, r: r.at[other], r=r) for r in p_refs]
    dst = [functools.partial(lambda k, me, other, j, r: r.at[j], r=r) for r in l_refs]
    return src, dst


def _rs_start(ps, *, name):
    n = len(ps)
    lands = [lax.empty((3,) + p.shape[1:], p.dtype) for p in ps]

    def body(*refs):
        send, recv = refs[2 * n], refs[2 * n + 1]
        p_thru = refs[2 * n + 2:3 * n + 2]
        l_thru = refs[3 * n + 2:4 * n + 2]
        token = refs[-1]
        src, dst = _rs_pieces(p_thru, l_thru)
        for cp in _ici_copies(src, dst, send, recv, True):
            cp.start()
        token[...] = jnp.zeros_like(token)

    out_shape = (pltpu.SemaphoreType.DMA((3 * n,)), pltpu.SemaphoreType.DMA((3 * n,)))
    out_shape += tuple(pltpu.HBM(a.shape, a.dtype) for a in list(ps) + lands) + (TOKEN,)
    res = pl.pallas_call(
        body, name=name, out_shape=out_shape, in_specs=(HBM_SPEC,) * (2 * n),
        out_specs=(SEM_SPEC, SEM_SPEC) + (HBM_SPEC,) * (2 * n) + (pl.BlockSpec(memory_space=pltpu.VMEM),),
        input_output_aliases={k: 2 + k for k in range(2 * n)},
        compiler_params=pltpu.CompilerParams(has_side_effects=EFFECT),
    )(*[_in_hbm(a) for a in list(ps) + lands])
    return (res[0], res[1]), list(res[2:2 + n]), list(res[2 + n:2 + 2 * n]), res[-1]


def _rs_wait(ps, lands, send_sems, recv_sems, after, *, name):
    n = len(ps)

    def body(*refs):
        p_in, l_in = refs[:n], refs[n:2 * n]
        send, recv = refs[2 * n], refs[2 * n + 1]
        src, dst = _rs_pieces(p_in, l_in)
        for cp in _ici_copies(src, dst, send, recv, True):
            cp.wait_send()
        for cp in _ici_copies(src, dst, send, recv, False):
            cp.wait_recv()

    res = pl.pallas_call(
        body, name=name, out_shape=tuple(pltpu.HBM(a.shape, a.dtype) for a in list(ps) + list(lands)),
        in_specs=(HBM_SPEC,) * (2 * n) + (SEM_SPEC, SEM_SPEC, pl.BlockSpec(memory_space=pl.ANY)),
        out_specs=(HBM_SPEC,) * (2 * n), input_output_aliases={k: k for k in range(2 * n)},
        compiler_params=pltpu.CompilerParams(has_side_effects=EFFECT),
    )(*ps, *lands, send_sems, recv_sems, after)
    return list(res[:n]), list(res[n:])


def _rs_begin(gs, tag):
    c = lax.axis_index("c")
    got = _pair_exchange(gs, name=f"rs_pair_exchange_{tag}")
    pair = [_pair_sum(g, r, c, name=f"rs_pair_sum_{tag}{k}") for k, (g, r) in enumerate(zip(gs, got))]
    sems, pair, lands, token = _rs_start(pair, name=f"rs_start_{tag}")
    return (sems, pair, lands), token


def _rs_end(handle, after, tag):
    x, y, c = _place()
    (send, recv), pair, lands = handle
    pair, lands = _rs_wait(pair, lands, send, recv, after, name=f"rs_wait_{tag}")
    halves = [_chip_sum(p, l, 2 * x + y, c, name=f"rs_chip_sum_{tag}{k}") for k, (p, l) in enumerate(zip(pair, lands))]
    full = _pair_gather(halves, name=f"rs_pair_gather_{tag}")
    return [f.reshape(2 * f.shape[1], f.shape[2]) for f in full]


def _to_segments(a):
    rows = a.shape[0]
    return a.reshape(N_SEG, rows // N_SEG, -1).transpose(1, 0, 2).reshape(rows, -1)


def _from_segments(a):
    rows = a.shape[0]
    return a.reshape(rows // N_SEG, N_SEG, -1).transpose(1, 0, 2).reshape(rows, -1)


def _rope_tables(L):
    t = jnp.arange(L, dtype=jnp.int32)
    row = (t // GRID_W).astype(F32)
    col = (t % GRID_W).astype(F32)
    n_freq = QK_ROPE // 4
    inv = ROPE_BASE ** (-jnp.arange(n_freq, dtype=F32) / n_freq)
    a0, a1 = row[:, None] * inv, col[:, None] * inv
    z = jnp.zeros((L, LANES - QK_ROPE), F32)
    cos = jnp.concatenate([jnp.cos(a0), jnp.cos(a0), jnp.cos(a1), jnp.cos(a1), z], axis=1)
    sin = jnp.concatenate([-jnp.sin(a0), jnp.sin(a0), -jnp.sin(a1), jnp.sin(a1), z], axis=1)
    return _to_segments(cos), _to_segments(sin)


def _col_blocks(w, nblk):
    r, c = w.shape
    return w.reshape(r, nblk, c // nblk).transpose(1, 0, 2)


def _from_col_blocks(w4):
    nblk, r, c = w4.shape
    return w4.transpose(1, 0, 2).reshape(r, nblk * c)


def _s5_discretize(a_re, a_im, log_dt, b_re, b_im):
    dt = jnp.exp(log_dt)[:, None]
    mag = jnp.exp(a_re * dt)
    ab_re, ab_im = mag * jnp.cos(a_im * dt), mag * jnp.sin(a_im * dt)
    den = a_re * a_re + a_im * a_im
    nr, ni = ab_re - 1.0, ab_im
    co_re = (nr * a_re + ni * a_im) / den
    co_im = (ni * a_re - nr * a_im) / den
    bb_re = co_re[..., None] * b_re - co_im[..., None] * b_im
    bb_im = co_re[..., None] * b_im + co_im[..., None] * b_re
    return ab_re, ab_im, bb_re, bb_im


def _diag_blocks_in(bb, gpb):
    G, N, P = bb.shape
    eye = jnp.eye(gpb, dtype=bb.dtype)
    t = jnp.einsum("jgnp,gh->jgphn", bb.reshape(G // gpb, gpb, N, P), eye)
    return t.reshape(G // gpb, gpb * P, gpb * N)


def _diag_blocks_out(cc, gpb):
    G, P, N = cc.shape
    eye = jnp.eye(gpb, dtype=cc.dtype)
    t = jnp.einsum("jgpn,gh->jgnhp", cc.reshape(G // gpb, gpb, P, N), eye)
    return t.reshape(G // gpb, gpb * N, gpb * P)


def _tr(ws):
    return [jnp.swapaxes(w, 1, 2) for w in ws]


WEIGHTS = ['c_ctx', 'w_mod', 'b_mod', 'norm1', 'norm2', 'w_in', 's5_a_re', 's5_a_im', 's5_log_dt', 's5_b_re', 's5_b_im',
           's5_c_re', 's5_c_im', 's5_d', 'w_glu', 'q_norm', 'kv_norm', 'w_uq', 'w_ukv', 'w_mla_o', 'w_out', 'w_ffn_in',
           'w_ffn_out', 'norm_f']
AG_GROUPS = [['w_in'], ['w_glu', 'w_uq', 'w_ukv', 'w_mla_o', 'w_out'], ['w_ffn_in', 'w_ffn_out']]
SMALL = ['norm1', 'norm2', 's5_a_re', 's5_a_im', 's5_log_dt', 's5_b_re', 's5_b_im', 's5_c_re', 's5_c_im', 's5_d',
         'q_norm', 'kv_norm', 'norm_f']


def _pad_rows(a, rows):
    return jnp.concatenate([a, jnp.zeros((rows - a.shape[0],) + a.shape[1:], a.dtype)], axis=0)


def _pack(vals, width, rows):
    flat = jnp.concatenate([v.reshape(-1).astype(F32) for v in vals])
    flat = jnp.concatenate([flat, jnp.zeros((rows * width - flat.shape[0],), F32)])
    return flat.reshape(rows, width)


def _unpack(buf, like):
    flat = buf.reshape(-1)
    out, pos = [], 0
    for v in like:
        out.append(flat[pos:pos + v.size].reshape(v.shape))
        pos += v.size
    return out


def _step(x, c, ctx, loss_target, w, m, v):
    px, py, pc = _place()
    me = 4 * px + 2 * py + pc
    me_chip = 2 * px + py
    L, D = x.shape[1], x.shape[2]
    Lc = ctx.shape[1]
    T = L + Lc
    SW = D // 2
    G = SW // S5_GROUP
    C = G * S5_STATE
    H = MLA_HEADS
    q_rank = w['q_norm'].shape[1]
    kv_rank = w['kv_norm'].shape[1]
    d_ff = w['w_ffn_out'].shape[1] * 4
    wa_used = SW + q_rank + kv_rank + QK_ROPE
    WA = -(-(SW + q_rank + kv_rank + LANES) // 512) * 512

    names = [nme for grp in AG_GROUPS for nme in grp]
    bufs = [_into_slot(w[nme][0], me_chip, 4, BF16, name=f"cast_{nme}") for nme in names]
    group_idx, pos = [], 0
    for grp in AG_GROUPS:
        group_idx.append(list(range(pos, pos + len(grp))))
        pos += len(grp)
    ag_sems, bufs, ag_token = _ag_start(bufs, group_idx, name="ag_start")
    gathered = {}

    def arrive(g, after):
        got = _ag_wait([bufs[k] for k in group_idx[g]], *ag_sems[g], after, name=f"ag_wait_{g}")
        gathered.update(zip(AG_GROUPS[g], _ag_forward(got, name=f"ag_forward_{g}")))

    c_rows = _pad_rows(c.astype(F32), SUBLANES) + ag_token[0, 0]
    c_all = _allgather8(c_rows, name="ag_cond")[:, 0, :]
    cond = jnp.concatenate([c_all, w['c_ctx'].reshape(1, D)], axis=0)
    cond = _pad_rows(cond, 16)
    (act,) = _rw(lambda t: (jax.nn.silu(t),), [cond], [], [F32], name="cond_silu")
    w_mod, cs_mod = w['w_mod'][0], w['w_mod'].shape[2]
    mod_part = _mm(act, w_mod, out_dtype=F32, name="mod_fwd")
    mod_all = _allgather8(mod_part, name="ag_mod")
    mod_full = jnp.concatenate([mod_all[0], mod_all[2], mod_all[4], mod_all[6]], axis=1) + w['b_mod']
    m_lat = lax.dynamic_slice_in_dim(mod_full, me, 1, axis=0).reshape(6, D)
    m_ctx = mod_full[8].reshape(6, D)
    sh1, sc1, g1, sh2, sc2, g2 = (m_lat[i:i + 1] for i in range(6))
    csh1, csc1 = m_ctx[0:1], m_ctx[1:2]

    arrive(0, mod_full)
    w_in = _from_col_blocks(gathered['w_in'])
    w_a = jnp.concatenate([w_in[:, :wa_used], jnp.zeros((D, WA - wa_used), BF16)], axis=1)
    w_g = w_in[:, wa_used:]
    xs = _to_segments(x[0])
    cs = _to_segments(ctx[0])
    tgt = _to_segments(loss_target[0])
    cos, sin = _rope_tables(L)
    n1, n2, nf = w['norm1'], w['norm2'], w['norm_f'].reshape(1, D)
    qg, kvg = w['q_norm'], w['kv_norm']

    (xn_lat,) = _rw(_f_norm_mod, [xs], [n1, sc1, sh1], [BF16], name="norm1_lat")
    (xn_ctx,) = _rw(_f_norm_mod, [cs], [n1, csc1, csh1], [BF16], name="norm1_ctx")
    xn = jnp.concatenate([xn_lat, xn_ctx], axis=0)
    ha = _mm(xn, w_a, out_dtype=F32, name="in_proj")
    ha_lat, ha_ctx = ha[:L], ha[L:]
    gt = _mm(xn_lat, w_g, out_dtype=F32, name="in_gates")
    f_post_lat = _make_f_post_in(SW, q_rank, kv_rank, True)
    f_post_ctx = _make_f_post_in(SW, q_rank, kv_rank, False)
    u_lat, cqn, ckvn_lat, kr_lat = _rw(f_post_lat, [ha_lat, cos, sin], [qg, kvg], [F32, BF16, BF16, BF16], name="post_in_lat")
    u_ctx, ckvn_ctx, kr_ctx = _rw(f_post_ctx, [ha_ctx], [kvg], [F32, BF16, BF16], name="post_in_ctx")

    gpb = min(S5_BLOCK_GROUPS, G)
    gpo = min(8, G)
    d_skip = w['s5_d'][0].reshape(1, SW)
    disc, vjp_disc, w_b, w_c = [], [], [], []
    for d in range(2):
        prm = (w['s5_a_re'][0, d], w['s5_a_im'][0, d], w['s5_log_dt'][0, d], w['s5_b_re'][0, d], w['s5_b_im'][0, d])

        def prep(a_re, a_im, log_dt, b_re, b_im):
            ab_re, ab_im, bb_re, bb_im = _s5_discretize(a_re, a_im, log_dt, b_re, b_im)
            return ab_re.reshape(1, C), ab_im.reshape(1, C), _diag_blocks_in(bb_re, gpb), _diag_blocks_in(bb_im, gpb)

        out, vj = jax.vjp(prep, *prm)
        disc.append(out)
        vjp_disc.append(vj)
        w_b += [out[2], out[3]]
        w_c += [_diag_blocks_out(w['s5_c_re'][0, d], gpo), -_diag_blocks_out(w['s5_c_im'][0, d], gpo)]
    nb_in = G // gpb
    nb_out = G // gpo
    bu_lat = _bd_fanout(u_lat, w_b, name="s5_bu_lat")
    bu_ctx = _bd_fanout(u_ctx, w_b, name="s5_bu_ctx")
    zero = jnp.zeros((1, C), F32)
    h_lat, h_ctx, hT_ctx = [], [], []
    for d, rev in enumerate((False, True)):
        lr, li = disc[d][0], disc[d][1]
        hcr, hci, tr, ti = _s5_scan(bu_ctx[2 * d], bu_ctx[2 * d + 1], lr, li, zero, zero, zero, zero, reverse=rev,
                                    name=f"s5_scan_ctx_{d}")
        hlr, hli, _, _ = _s5_scan(bu_lat[2 * d], bu_lat[2 * d + 1], lr, li, tr, ti, zero, zero, reverse=rev,
                                  name=f"s5_scan_lat_{d}")
        h_ctx += [hcr, hci]
        h_lat += [hlr, hli]
        hT_ctx += [tr, ti]
    r5 = _bd_fanin(h_lat, w_c, name="s5_readout")
    (z,) = _rw(_f_s5post, [u_lat, r5], [d_skip], [BF16], name="s5_post")

    arrive(1, z)
    w_glu, w_ukv, w_mla_o = (gathered[nme] for nme in ('w_glu', 'w_ukv', 'w_mla_o'))
    w_out = gathered['w_out'].reshape(D, D)
    uq3 = _from_col_blocks(gathered['w_uq']).reshape(q_rank, H, QK_NOPE + QK_ROPE)
    w_q2 = jnp.concatenate([
        uq3[:, :, :QK_NOPE].reshape(q_rank, H * QK_NOPE),
        jnp.concatenate([uq3[:, :, QK_NOPE:], jnp.zeros((q_rank, H, LANES - QK_ROPE), BF16)], axis=2).reshape(q_rank, H * LANES),
    ], axis=1)
    q2 = _mm(cqn, w_q2, out_dtype=F32, name="q_up")
    (qq,) = _rw(_f_qpost, [q2, cos, sin], [], [BF16], name="q_rope")
    kvn = jnp.concatenate([ckvn_lat, ckvn_ctx], axis=0)
    kr_all = jnp.concatenate([kr_lat, kr_ctx], axis=0)
    kv = _mm(kvn, w_ukv, b_shards=4, out_dtype=BF16, name="kv_up")
    o = _attn_fwd(qq, kv, kr_all, name="attn_fwd")

    ab = _mm(z, w_glu, b_shards=4, out_dtype=F32, name="glu_proj")
    bm = _mm(o, w_mla_o, b_shards=4, out_dtype=F32, name="mla_out")
    (mix,) = _rw(_f_merge, [ab, bm, gt], [], [BF16], name="merge")
    out1 = _mm(mix, w_out, out_dtype=F32, name="out_proj")
    x1, xn2 = _rw(_f_resid_norm, [xs, out1], [g1, n2, sc2, sh2], [F32, BF16], name="resid_norm2")
    arrive(2, xn2)
    w_ffn_in = gathered['w_ffn_in']
    w_ffn_out = gathered['w_ffn_out'].reshape(d_ff, D)
    ab2 = _mm(xn2, w_ffn_in, b_shards=4, out_dtype=F32, name="ffn_in")
    (hmid,) = _rw(_f_swiglu, [ab2], [], [BF16], name="ffn_act")
    f2 = _mm(hmid, w_ffn_out, out_dtype=F32, name="ffn_out")
    (row_loss,) = _rw(_f_final, [x1, f2, tgt], [g2, nf], [F32], name="final_loss")
    loss = lax.psum(jnp.sum(row_loss), ("x", "y", "c"))

    ones = jnp.ones((L, 1), F32)
    (dx1_a, df2), (dg2, dnf) = _rw_vjp(_f_final, [x1, f2, tgt], [g2, nf], [[ones]], [True, True, False], [True, True],
                                       [F32, BF16], name="final_loss_bwd")
    dhmid = _mm(df2, w_ffn_out, tb=True, out_dtype=F32, name="ffn_out_dx")
    gw_ffn_out = _mm(hmid, df2, ta=True, out_dtype=BF16, name="ffn_out_dw")
    (dab2,), _ = _rw_vjp(_f_swiglu, [ab2], [], [[dhmid]], [True], [], [BF16], name="ffn_act_bwd")
    dxn2 = _mm(dab2, w_ffn_in, tb=True, b_shards=4, out_dtype=F32, name="ffn_in_dx")
    gw_ffn_in = _mm(xn2, dab2, ta=True, out_shards=4, out_dtype=BF16, name="ffn_in_dw")
    rs_ffn, tok = _rs_begin([gw_ffn_out.reshape(4, -1, D), gw_ffn_in], "ffn")
    (dx_a, dout1), (dg1, dn2, dsc2, dsh2) = _rw_vjp(
        _f_resid_norm, [xs, out1], [g1, n2 + tok[0, 0], sc2, sh2], [[dx1_a], [dxn2]], [True, True], [True] * 4, [F32, BF16],
        name="resid_norm2_bwd")
    dmix = _mm(dout1, w_out, tb=True, out_dtype=F32, name="out_proj_dx")
    gw_out = _mm(mix, dout1, ta=True, out_dtype=BF16, name="out_proj_dw")
    (dab, dbm, dgt), _ = _rw_vjp(_f_merge, [ab, bm, gt], [], [[dmix]], [True] * 3, [], [BF16] * 3, name="merge_bwd")
    dz = _mm(dab, w_glu, tb=True, b_shards=4, out_dtype=F32, name="glu_proj_dx")
    gw_glu = _mm(z, dab, ta=True, out_shards=4, out_dtype=BF16, name="glu_proj_dw")
    do = _mm(dbm, w_mla_o, tb=True, b_shards=4, out_dtype=BF16, name="mla_out_dx")
    gw_mla_o = _mm(o, dbm, ta=True, out_shards=4, out_dtype=BF16, name="mla_out_dw")
    dxn_g = _mm(dgt, w_g, tb=True, out_dtype=F32, name="in_gates_dx")
    gw_g = _mm(xn_lat, dgt, ta=True, out_dtype=BF16, name="in_gates_dw")

    (du_a, dr5), (dd_skip,) = _rw_vjp(_f_s5post, [u_lat, r5], [d_skip], [[dz]], [True, True], [True], [F32, F32],
                                      name="s5_post_bwd")
    dh_lat = _bd_fanout(dr5, _tr(w_c), name="s5_readout_dx")
    dw_c = _bd_dw(h_lat, [dr5] * 4, nb_out, name="s5_readout_dw")
    zeros_ctx = jnp.zeros((Lc, C), F32)
    mu_lat, mu_ctx, dlam = [], [], []
    for d, rev in enumerate((False, True)):
        lr, li = disc[d][0], disc[d][1]
        mlr, mli, fr, fi = _s5_scan(dh_lat[2 * d], dh_lat[2 * d + 1], lr, -li, zero, zero, zero, zero, reverse=not rev,
                                    name=f"s5_adj_lat_{d}")
        dh0r, dh0i = _cmul(lr, -li, fr, fi)
        mcr, mci, _, _ = _s5_scan(zeros_ctx, zeros_ctx, lr, -li, zero, zero, dh0r, dh0i, reverse=not rev,
                                  name=f"s5_adj_ctx_{d}")
        dl_lat = _s5_dlam(mlr, mli, h_lat[2 * d], h_lat[2 * d + 1], hT_ctx[2 * d], hT_ctx[2 * d + 1], reverse=rev,
                          name=f"s5_dlam_lat_{d}")
        dl_ctx = _s5_dlam(mcr, mci, h_ctx[2 * d], h_ctx[2 * d + 1], zero, zero, reverse=rev, name=f"s5_dlam_ctx_{d}")
        mu_lat += [mlr, mli]
        mu_ctx += [mcr, mci]
        dlam.append((dl_lat[0] + dl_ctx[0], dl_lat[1] + dl_ctx[1]))
    du_b = _bd_fanin(mu_lat, _tr(w_b), name="s5_bu_lat_dx")
    du_ctx = _bd_fanin(mu_ctx, _tr(w_b), name="s5_bu_ctx_dx")
    dw_b_lat = _bd_dw([u_lat] * 4, mu_lat, nb_in, name="s5_bu_lat_dw")
    dw_b_ctx = _bd_dw([u_ctx] * 4, mu_ctx, nb_in, name="s5_bu_ctx_dw")
    g_s5 = {}
    for d in range(2):
        ct = (dlam[d][0], dlam[d][1], dw_b_lat[2 * d] + dw_b_ctx[2 * d], dw_b_lat[2 * d + 1] + dw_b_ctx[2 * d + 1])
        ga_re, ga_im, gdt, gb_re, gb_im = vjp_disc[d](ct)
        _, vj_c = jax.vjp(lambda cr, ci: (_diag_blocks_out(cr, gpo), -_diag_blocks_out(ci, gpo)),
                          w['s5_c_re'][0, d], w['s5_c_im'][0, d])
        gc_re, gc_im = vj_c((dw_c[2 * d], dw_c[2 * d + 1]))
        for nme, val in (('s5_a_re', ga_re), ('s5_a_im', ga_im), ('s5_log_dt', gdt), ('s5_b_re', gb_re),
                         ('s5_b_im', gb_im), ('s5_c_re', gc_re), ('s5_c_im', gc_im)):
            g_s5.setdefault(nme, []).append(val)
    g_small = {nme: jnp.stack(vals)[None] for nme, vals in g_s5.items()}
    g_small['s5_d'] = dd_skip.reshape(w['s5_d'].shape)

    dqn, dqr, dkn, dv, dkr = _attn_bwd(qq, kv, kr_all, do, name="attn_bwd")
    dqq = jnp.concatenate([dqn, dqr], axis=1)
    (dq2,), _ = _rw_vjp(_f_qpost, [q2, cos, sin], [], [[dqq]], [True, False, False], [], [BF16], name="q_rope_bwd")
    dcqn = _mm(dq2, w_q2, tb=True, out_dtype=F32, name="q_up_dx")
    gw_q2 = _mm(cqn, dq2, ta=True, out_dtype=BF16, name="q_up_dw")
    dkv = jnp.stack([dkn.reshape(T, H, LANES), dv.reshape(T, H, LANES)], axis=2).reshape(T, 2 * H * LANES)
    dckvn = _mm(dkv, w_ukv, tb=True, b_shards=4, out_dtype=F32, name="kv_up_dx")
    gw_ukv = _mm(kvn, dkv, ta=True, out_shards=4, out_dtype=BF16, name="kv_up_dw")
    uq_nope = gw_q2[:, :H * QK_NOPE].reshape(q_rank, H, QK_NOPE)
    uq_rope = gw_q2[:, H * QK_NOPE:].reshape(q_rank, H, LANES)[:, :, :QK_ROPE]
    gw_uq = jnp.concatenate([uq_nope, uq_rope], axis=2).reshape(q_rank, H * (QK_NOPE + QK_ROPE))
    rs_mix, tok = _rs_begin([gw_out.reshape(4, -1, D), gw_glu, gw_mla_o, _col_blocks(gw_uq, 4), gw_ukv], "mix")

    (dha_lat,), (dqg, dkvg_lat) = _rw_vjp(
        f_post_lat, [ha_lat, cos, sin], [qg, kvg + tok[0, 0]], [[du_a, du_b], [dcqn], [dckvn[:L]], [dkr[:L]]],
        [True, False, False], [True, True], [BF16], name="post_in_lat_bwd")
    (dha_ctx,), (dkvg_ctx,) = _rw_vjp(f_post_ctx, [ha_ctx], [kvg], [[du_ctx], [dckvn[L:]], [dkr[L:]]], [True], [True],
                                      [BF16], name="post_in_ctx_bwd")
    dha = jnp.concatenate([dha_lat, dha_ctx], axis=0)
    dxn = _mm(dha, w_a, tb=True, out_dtype=F32, name="in_proj_dx")
    gw_a = _mm(xn, dha, ta=True, out_dtype=BF16, name="in_proj_dw")
    (dx_seg,), (dn1_lat, dsc1, dsh1) = _rw_vjp(
        _f_norm_mod_keep, [xs], [n1, sc1, sh1], [[dxn[:L], dxn_g], [dx_a]], [True], [True] * 3, [F32], name="norm1_lat_bwd")
    _, (dn1_ctx, dcsc1, dcsh1) = _rw_vjp(_f_norm_mod, [cs], [n1, csc1, csh1], [[dxn[L:]]], [False], [True] * 3, [],
                                         name="norm1_ctx_bwd")
    grad_x = _from_segments(dx_seg)[None]
    g_small.update(norm1=dn1_lat + dn1_ctx, norm2=dn2, q_norm=dqg, kv_norm=dkvg_lat + dkvg_ctx, norm_f=dnf.reshape(D))
    gw_in = jnp.concatenate([gw_a[:, :wa_used], gw_g], axis=1)
    small_vals = [g_small[nme] for nme in SMALL]
    n_small = sum(val.size for val in small_vals)
    small_rows = -(-n_small // (LANES * 4 * 32)) * 32
    rs_in, tok = _rs_begin([_col_blocks(gw_in, 4), _pack(small_vals, LANES, 4 * small_rows).reshape(4, small_rows, LANES)], "in")

    zD = jnp.zeros((1, D), F32)
    dm = jnp.concatenate([
        jnp.concatenate([dsh1, dsc1, dg1, dsh2, dsc2, dg2], axis=1),
        jnp.concatenate([dcsh1, dcsc1, zD, zD, zD, zD], axis=1),
    ], axis=0) + tok[0, 0]
    dm_all = _allgather8(_pad_rows(dm, SUBLANES), name="ag_dmod")
    dm_ctx = dm_all[0, 1]
    for k in range(1, 8):
        dm_ctx = dm_ctx + dm_all[k, 1]
    dmod = _pad_rows(jnp.concatenate([dm_all[:, 0, :], dm_ctx[None]], axis=0), 16)
    g_b_mod = jnp.sum(dmod, axis=0, keepdims=True)
    dmod_mine = lax.dynamic_slice_in_dim(dmod, me_chip * cs_mod, cs_mod, axis=1)
    g_w_mod = _mm(act, dmod_mine, ta=True, out_dtype=F32, name="mod_dw")
    dact_part = _mm(dmod_mine, w_mod, tb=True, out_dtype=F32, name="mod_dx")
    dact_all = _allgather8(dact_part, name="ag_dact")
    dact = dact_all[0] + dact_all[2] + dact_all[4] + dact_all[6]
    (dcond_rows,), _ = _rw_vjp(lambda t: (jax.nn.silu(t),), [cond], [], [[dact]], [True], [], [F32], name="cond_silu_bwd")
    g_c_ctx = dcond_rows[8]

    grads, delta, new_m, new_v = {}, {}, {}, {}

    def update(nme, red):
        res = _adamw(w[nme][0], red, m[nme][0], v[nme][0], name=f"adamw_{nme}")
        grads[nme], delta[nme], new_m[nme], new_v[nme] = (r.reshape(w[nme].shape) for r in res)
        return res[1]

    after = update('w_mod', g_w_mod)
    for handle, tag, members in ((rs_ffn, "ffn", ['w_ffn_out', 'w_ffn_in']),
                                 (rs_mix, "mix", ['w_out', 'w_glu', 'w_mla_o', 'w_uq', 'w_ukv']),
                                 (rs_in, "in", ['w_in'])):
        reduced = _rs_end(handle, after, tag)
        for nme, red in zip(members, reduced):
            after = update(nme, red)
    small_mine = reduced[-1]
    small_buf = _into_slot(small_mine, me_chip, 4, F32, name="small_grads_slot")
    small_all = _allgather_shards([small_buf], name="ag_small_grads")[0].reshape(4 * small_rows, LANES)
    g_small_red = dict(zip(SMALL, _unpack(small_all, [w[nme] for nme in SMALL])))
    rest = SMALL + ['c_ctx', 'b_mod']
    g_rest = dict(g_small_red, c_ctx=g_c_ctx, b_mod=g_b_mod)
    rows_rest = -(-sum(w[nme].size for nme in rest) // (LANES * 16)) * 16
    packed = [_pack([src[nme] for nme in rest], LANES, rows_rest) for src in (w, g_rest, m, v)]
    res = _adamw(*packed, name="adamw_small")
    for dst, buf in zip((grads, delta, new_m, new_v), res):
        dst.update(zip(rest, _unpack(buf, [w[nme] for nme in rest])))
    return (loss, grad_x, *[grads[nme] for nme in WEIGHTS], *[delta[nme] for nme in WEIGHTS],
            *[new_m[nme] for nme in WEIGHTS], *[new_v[nme] for nme in WEIGHTS])


def kernel(x, c, ctx, c_ctx, w_mod, b_mod, norm1, norm2, w_in, s5_a_re, s5_a_im, s5_log_dt, s5_b_re, s5_b_im, s5_c_re, s5_c_im, s5_d, w_glu, q_norm, kv_norm, w_uq, w_ukv, w_mla_o, w_out, w_ffn_in, w_ffn_out, norm_f, loss_target, m_c_ctx, m_w_mod, m_b_mod, m_norm1, m_norm2, m_w_in, m_s5_a_re, m_s5_a_im, m_s5_log_dt, m_s5_b_re, m_s5_b_im, m_s5_c_re, m_s5_c_im, m_s5_d, m_w_glu, m_q_norm, m_kv_norm, m_w_uq, m_w_ukv, m_w_mla_o, m_w_out, m_w_ffn_in, m_w_ffn_out, m_norm_f, v_c_ctx, v_w_mod, v_b_mod, v_norm1, v_norm2, v_w_in, v_s5_a_re, v_s5_a_im, v_s5_log_dt, v_s5_b_re, v_s5_b_im, v_s5_c_re, v_s5_c_im, v_s5_d, v_w_glu, v_q_norm, v_kv_norm, v_w_uq, v_w_ukv, v_w_mla_o, v_w_out, v_w_ffn_in, v_w_ffn_out, v_norm_f):
    w = dict(c_ctx=c_ctx, w_mod=w_mod, b_mod=b_mod, norm1=norm1, norm2=norm2, w_in=w_in, s5_a_re=s5_a_re, s5_a_im=s5_a_im,
             s5_log_dt=s5_log_dt, s5_b_re=s5_b_re, s5_b_im=s5_b_im, s5_c_re=s5_c_re, s5_c_im=s5_c_im, s5_d=s5_d, w_glu=w_glu,
             q_norm=q_norm, kv_norm=kv_norm, w_uq=w_uq, w_ukv=w_ukv, w_mla_o=w_mla_o, w_out=w_out, w_ffn_in=w_ffn_in,
             w_ffn_out=w_ffn_out, norm_f=norm_f)
    m = dict(c_ctx=m_c_ctx, w_mod=m_w_mod, b_mod=m_b_mod, norm1=m_norm1, norm2=m_norm2, w_in=m_w_in, s5_a_re=m_s5_a_re,
             s5_a_im=m_s5_a_im, s5_log_dt=m_s5_log_dt, s5_b_re=m_s5_b_re, s5_b_im=m_s5_b_im, s5_c_re=m_s5_c_re,
             s5_c_im=m_s5_c_im, s5_d=m_s5_d, w_glu=m_w_glu, q_norm=m_q_norm, kv_norm=m_kv_norm, w_uq=m_w_uq, w_ukv=m_w_ukv,
             w_mla_o=m_w_mla_o, w_out=m_w_out, w_ffn_in=m_w_ffn_in, w_ffn_out=m_w_ffn_out, norm_f=m_norm_f)
    v = dict(c_ctx=v_c_ctx, w_mod=v_w_mod, b_mod=v_b_mod, norm1=v_norm1, norm2=v_norm2, w_in=v_w_in, s5_a_re=v_s5_a_re,
             s5_a_im=v_s5_a_im, s5_log_dt=v_s5_log_dt, s5_b_re=v_s5_b_re, s5_b_im=v_s5_b_im, s5_c_re=v_s5_c_re,
             s5_c_im=v_s5_c_im, s5_d=v_s5_d, w_glu=v_w_glu, q_norm=v_q_norm, kv_norm=v_kv_norm, w_uq=v_w_uq, w_ukv=v_w_ukv,
             w_mla_o=v_w_mla_o, w_out=v_w_out, w_ffn_in=v_w_ffn_in, w_ffn_out=v_w_ffn_out, norm_f=v_norm_f)
    return _step(x, c, ctx, loss_target, w, m, v)
```

```python
import functools
import math

import jax
import jax.numpy as jnp
from jax import lax
from jax.experimental import pallas as pl
from jax.experimental.pallas import tpu as pltpu

F32 = jnp.float32
BF16 = jnp.bfloat16

EPS = 1e-6
GRID_W = 64
S5_GROUP = 16
S5_STATE = 64
MLA_HEADS = 8
QK_NOPE = 128
QK_ROPE = 64
V_DIM = 128
ROPE_BASE = 10000.0
ATTN_SCALE = (QK_NOPE + QK_ROPE) ** -0.5
ADAM_LR = 0.001
ADAM_B1 = 0.9
ADAM_B2 = 0.999
ADAM_EPS = 1e-08
ADAM_WD = 0.01
ADAM_STEP = 10

SUBLANES = 8
LANES = 128
V7X_VMEM_BYTES = 64 * 1024 * 1024
VMEM_LIMIT = (V7X_VMEM_BYTES * 7) // 8
N_SEG = SUBLANES
S5_BLOCK_GROUPS = 16
MESH = pl.DeviceIdType.MESH


def _pick(n, target, mult):
    best = None
    d = mult
    while d <= min(n, target):
        if n % d == 0:
            best = d
        d += mult
    return n if best is None else best


def _cparams(sem=None):
    return pltpu.CompilerParams(dimension_semantics=sem, vmem_limit_bytes=VMEM_LIMIT)


MM_VMEM_BUDGET = (V7X_VMEM_BYTES * 5) // 8


def _mm(a, b, *, ta=False, tb=False, out_dtype=F32, name, b_shards=1, out_shards=1):
    if ta:
        K, M = a.shape
    else:
        M, K = a.shape
    if tb:
        N, K2 = b.shape[-2], b.shape[-1] * b_shards
    else:
        K2, N = b.shape[-2], b.shape[-1] * b_shards
    assert K == K2, (a.shape, b.shape, ta, tb)
    n_unit = N // max(out_shards, 1 if tb else b_shards)
    k_unit = K // (b_shards if tb else 1)
    tn = _pick(n_unit, 1024, LANES)
    tm = _pick(M, 1024 if tn >= 512 else 2048, LANES if ta else 16)
    sa, sb, so = a.dtype.itemsize, b.dtype.itemsize, jnp.dtype(out_dtype).itemsize
    k_mult = LANES if (not ta or tb) else 16
    tk = k_mult if k_unit % k_mult == 0 else k_unit
    for cand in range(k_mult, k_unit + 1, k_mult):
        if k_unit % cand == 0 and 2 * cand * (tm * sa + tn * sb) + tm * tn * (4 + 2 * so) <= MM_VMEM_BUDGET:
            tk = cand
    nk = K // tk
    dims = (((0 if ta else 1,), (1 if tb else 0,)), ((), ()))

    def body(a_ref, b_ref, o_ref, *scratch):
        part = lax.dot_general(a_ref[...].astype(BF16), b_ref[...].astype(BF16), dims, preferred_element_type=F32)
        if nk == 1:
            o_ref[...] = part.astype(o_ref.dtype)
            return
        acc_ref, = scratch
        k = pl.program_id(2)

        @pl.when(k == 0)
        def _():
            acc_ref[...] = part

        @pl.when(k > 0)
        def _():
            acc_ref[...] += part

        @pl.when(k == nk - 1)
        def _():
            o_ref[...] = acc_ref[...].astype(o_ref.dtype)

    a_spec = pl.BlockSpec((tk, tm), lambda i, j, k: (k, i)) if ta else pl.BlockSpec((tm, tk), lambda i, j, k: (i, k))
    if b_shards == 1:
        b_spec = pl.BlockSpec((tn, tk), lambda i, j, k: (j, k)) if tb else pl.BlockSpec((tk, tn), lambda i, j, k: (k, j))
    elif tb:
        kpb = k_unit // tk
        b_spec = pl.BlockSpec((None, tn, tk), lambda i, j, k: (k // kpb, j, k % kpb))
    else:
        npb = n_unit // tn
        b_spec = pl.BlockSpec((None, tk, tn), lambda i, j, k: (j // npb, k, j % npb))
    if out_shards == 1:
        out_spec = pl.BlockSpec((tm, tn), lambda i, j, k: (i, j))
        out_shape = jax.ShapeDtypeStruct((M, N), out_dtype)
    else:
        opb = n_unit // tn
        out_spec = pl.BlockSpec((None, tm, tn), lambda i, j, k: (j // opb, i, j % opb))
        out_shape = jax.ShapeDtypeStruct((out_shards, M, N // out_shards), out_dtype)
    return pl.pallas_call(
        body, name=name, grid=(M // tm, N // tn, nk),
        in_specs=[a_spec, b_spec], out_specs=out_spec, out_shape=out_shape,
        scratch_shapes=[pltpu.VMEM((tm, tn), F32)] if nk > 1 else [],
        compiler_params=_cparams(("parallel", "parallel", "arbitrary")),
    )(a, b)


def _row_tile(tiled, extra_bytes=0):
    rows = tiled[0].shape[0]
    per_row = sum(a.shape[1] * 4 for a in tiled) + extra_bytes
    target = max(SUBLANES, (6 * 1024 * 1024) // max(per_row, 1))
    return _pick(rows, min(target, 512), 16)


def _rw(f, tiled, bcast, out_dtypes, *, name):
    nt, nb = len(tiled), len(bcast)
    rows = tiled[0].shape[0]
    outs_aval = jax.eval_shape(f, *[jax.ShapeDtypeStruct((16, a.shape[1]), F32) for a in tiled],
                               *[jax.ShapeDtypeStruct(b.shape, F32) for b in bcast])
    widths = [o.shape[1] for o in outs_aval]
    tm = _row_tile(tiled, sum(w * 4 for w in widths))

    def body(*refs):
        tin = [r[...].astype(F32) for r in refs[:nt]]
        bin_ = [r[...].astype(F32) for r in refs[nt:nt + nb]]
        outs = f(*tin, *bin_)
        for o_ref, o in zip(refs[nt + nb:], outs):
            o_ref[...] = o.astype(o_ref.dtype)

    in_specs = [pl.BlockSpec((tm, a.shape[1]), lambda i: (i, 0)) for a in tiled]
    in_specs += [pl.BlockSpec(b.shape, lambda i: (0, 0)) for b in bcast]
    res = pl.pallas_call(
        body, name=name, grid=(rows // tm,), in_specs=in_specs,
        out_specs=[pl.BlockSpec((tm, w), lambda i: (i, 0)) for w in widths],
        out_shape=[jax.ShapeDtypeStruct((rows, w), dt) for w, dt in zip(widths, out_dtypes)],
        compiler_params=_cparams(("parallel",)),
    )(*tiled, *bcast)
    return list(res)


def _rw_vjp(f, tiled, bcast, cts, need_t, need_b, t_dtypes, *, name):
    nt, nb = len(tiled), len(bcast)
    rows = tiled[0].shape[0]
    flat_cts = [c for group in cts for c in group]
    t_idx = [i for i in range(nt) if need_t[i]]
    b_idx = [i for i in range(nb) if need_b[i]]
    tm = _row_tile(list(tiled) + flat_cts, sum(tiled[i].shape[1] * 4 for i in t_idx))
    nc = len(flat_cts)

    def body(*refs):
        i = pl.program_id(0)
        tin = [r[...].astype(F32) for r in refs[:nt]]
        bin_ = [r[...].astype(F32) for r in refs[nt:nt + nb]]
        ct_refs = refs[nt + nb:nt + nb + nc]
        out_refs = refs[nt + nb + nc:]
        outs, vjp_fn = jax.vjp(f, *tin, *bin_)
        ct_vals, pos = [], 0
        for o, group in zip(outs, cts):
            acc = jnp.zeros_like(o)
            for _ in group:
                acc = acc + ct_refs[pos][...].astype(F32)
                pos += 1
            ct_vals.append(acc)
        grads = vjp_fn(tuple(ct_vals))
        for o_ref, k in zip(out_refs[:len(t_idx)], t_idx):
            o_ref[...] = grads[k].astype(o_ref.dtype)
        for o_ref, k in zip(out_refs[len(t_idx):], b_idx):
            @pl.when(i == 0)
            def _(o_ref=o_ref):
                o_ref[...] = jnp.zeros_like(o_ref)

            o_ref[...] += grads[nt + k]

    in_specs = [pl.BlockSpec((tm, a.shape[1]), lambda i: (i, 0)) for a in tiled]
    in_specs += [pl.BlockSpec(b.shape, lambda i: (0, 0)) for b in bcast]
    in_specs += [pl.BlockSpec((tm, c.shape[1]), lambda i: (i, 0)) for c in flat_cts]
    out_specs = [pl.BlockSpec((tm, tiled[k].shape[1]), lambda i: (i, 0)) for k in t_idx]
    out_specs += [pl.BlockSpec(bcast[k].shape, lambda i: (0, 0)) for k in b_idx]
    out_shape = [jax.ShapeDtypeStruct(tiled[k].shape, dt) for k, dt in zip(t_idx, t_dtypes)]
    out_shape += [jax.ShapeDtypeStruct(bcast[k].shape, F32) for k in b_idx]
    res = pl.pallas_call(
        body, name=name, grid=(rows // tm,), in_specs=in_specs, out_specs=out_specs, out_shape=out_shape,
        compiler_params=_cparams(("arbitrary",)),
    )(*tiled, *bcast, *flat_cts)
    res = list(res)
    return res[:len(t_idx)], res[len(t_idx):]


def _rms(x, g):
    return x * lax.rsqrt(jnp.mean(x * x, axis=-1, keepdims=True) + EPS) * g


def _f_norm_mod(x, g, sc, sh):
    return (_rms(x, g) * (1.0 + sc) + sh,)


def _f_norm_mod_keep(x, g, sc, sh):
    return (_rms(x, g) * (1.0 + sc) + sh, x)


@jax.custom_vjp
def _swap16(x):
    w = x.shape[-1]
    lane = lax.broadcasted_iota(jnp.int32, x.shape, x.ndim - 1)
    return jnp.where((lane & 16) == 0, pltpu.roll(x, w - 16, x.ndim - 1), pltpu.roll(x, 16, x.ndim - 1))


_swap16.defvjp(lambda x: (_swap16(x), None), lambda _, g: (_swap16(g),))


def _rope(x, cos, sin):
    return x * cos + _swap16(x) * sin


def _make_f_post_in(sw, q_rank, kv_rank, with_q):
    o1, o2, o3 = sw, sw + q_rank, sw + q_rank + kv_rank

    if with_q:
        def f(ha, cos, sin, qg, kvg):
            u = ha[:, :o1]
            cqn = _rms(ha[:, o1:o2], qg)
            ckvn = _rms(ha[:, o2:o3], kvg)
            kr = _rope(ha[:, o3:o3 + LANES], cos, sin)
            return u, cqn, ckvn, kr
    else:
        def f(ha, kvg):
            return ha[:, :o1], _rms(ha[:, o2:o3], kvg), ha[:, o3:o3 + LANES]
    return f


def _f_qpost(q2, cos, sin):
    w = q2.shape[1] // 2
    reps = w // LANES
    qr = _rope(q2[:, w:], jnp.tile(cos, (1, reps)), jnp.tile(sin, (1, reps)))
    return (jnp.concatenate([q2[:, :w], qr], axis=1),)


def _f_s5post(u, r, d):
    return (jax.nn.gelu(d * u + r, approximate=True),)


def _f_merge(ab, bm, gt):
    d = bm.shape[1]
    br_s5 = ab[:, :d] * jax.nn.sigmoid(ab[:, d:])
    g = jax.nn.sigmoid(gt)
    return (g[:, :d] * br_s5 + g[:, d:] * bm,)


def _f_resid_norm(x, out, g1, n2, sc2, sh2):
    x1 = x + g1 * out
    return x1, _rms(x1, n2) * (1.0 + sc2) + sh2


def _f_swiglu(ab):
    d = ab.shape[1] // 2
    return (jax.nn.silu(ab[:, :d]) * ab[:, d:],)


def _f_final(x1, f, tgt, g2, nf):
    y = _rms(x1 + g2 * f, nf)
    return (0.5 * jnp.mean(jnp.square(y - tgt), axis=-1, keepdims=True),)


def _bd_fanout(x, ws, *, name):
    nw = len(ws)
    nb, kb, nn = ws[0].shape
    T = x.shape[0]
    tm = _pick(T, 512, 16)

    def body(*refs):
        xb = refs[0][...].astype(BF16)
        for w_ref, o_ref in zip(refs[1:1 + nw], refs[1 + nw:]):
            o_ref[...] = jnp.dot(xb, w_ref[0].astype(BF16), preferred_element_type=F32)

    return list(pl.pallas_call(
        body, name=name, grid=(nb, T // tm),
        in_specs=[pl.BlockSpec((tm, kb), lambda j, i: (i, j))] + [pl.BlockSpec((1, kb, nn), lambda j, i: (j, 0, 0))] * nw,
        out_specs=[pl.BlockSpec((tm, nn), lambda j, i: (i, j))] * nw,
        out_shape=[jax.ShapeDtypeStruct((T, nb * nn), F32)] * nw,
        compiler_params=_cparams(("parallel", "parallel")),
    )(x, *ws))


def _bd_fanin(xs, ws, *, name):
    nw = len(ws)
    nb, kb, nn = ws[0].shape
    T = xs[0].shape[0]
    tm = _pick(T, 512, 16)

    def body(*refs):
        acc = None
        for x_ref, w_ref in zip(refs[:nw], refs[nw:2 * nw]):
            t = jnp.dot(x_ref[...].astype(BF16), w_ref[0].astype(BF16), preferred_element_type=F32)
            acc = t if acc is None else acc + t
        refs[2 * nw][...] = acc

    return pl.pallas_call(
        body, name=name, grid=(nb, T // tm),
        in_specs=[pl.BlockSpec((tm, kb), lambda j, i: (i, j))] * nw + [pl.BlockSpec((1, kb, nn), lambda j, i: (j, 0, 0))] * nw,
        out_specs=pl.BlockSpec((tm, nn), lambda j, i: (i, j)),
        out_shape=jax.ShapeDtypeStruct((T, nb * nn), F32),
        compiler_params=_cparams(("parallel", "parallel")),
    )(*xs, *ws)


def _bd_dw(xs, dys, nb, *, name):
    npair = len(xs)
    T = xs[0].shape[0]
    kb = xs[0].shape[1] // nb
    nn = dys[0].shape[1] // nb
    tm = _pick(T, 512, 16)
    dims = (((0,), (0,)), ((), ()))

    def body(*refs):
        i = pl.program_id(1)
        for x_ref, d_ref, o_ref in zip(refs[:npair], refs[npair:2 * npair], refs[2 * npair:]):
            @pl.when(i == 0)
            def _(o_ref=o_ref):
                o_ref[...] = jnp.zeros_like(o_ref)

            o_ref[0] += lax.dot_general(x_ref[...].astype(BF16), d_ref[...].astype(BF16), dims,
                                        preferred_element_type=F32)

    return list(pl.pallas_call(
        body, name=name, grid=(nb, T // tm),
        in_specs=[pl.BlockSpec((tm, kb), lambda j, i: (i, j))] * npair + [pl.BlockSpec((tm, nn), lambda j, i: (i, j))] * npair,
        out_specs=[pl.BlockSpec((1, kb, nn), lambda j, i: (j, 0, 0))] * npair,
        out_shape=[jax.ShapeDtypeStruct((nb, kb, nn), F32)] * npair,
        compiler_params=_cparams(("parallel", "arbitrary")),
    )(*xs, *dys))


def _cmul(ar, ai, br, bi):
    return ar * br - ai * bi, ar * bi + ai * br


def _cpow(lr, li, n):
    rr, ri = None, None
    br, bi = lr, li
    while n:
        if n & 1:
            rr, ri = (br, bi) if rr is None else _cmul(rr, ri, br, bi)
        n >>= 1
        if n:
            br, bi = _cmul(br, bi, br, bi)
    return rr, ri


def _s5_scan(b_re, b_im, lam_re, lam_im, h0_re, h0_im, e0_re, e0_im, *, reverse, name):
    rows, C = b_re.shape
    n = rows // N_SEG
    cb = _pick(C, 512, LANES)
    seg_order = list(range(N_SEG))[::-1] if reverse else list(range(N_SEG))
    s_first, s_last = seg_order[0], seg_order[-1]

    def body(br_ref, bi_ref, lr_ref, li_ref, h0r_ref, h0i_ref, e0r_ref, e0i_ref, hr_ref, hi_ref, htr_ref, hti_ref):
        shape = (N_SEG, cb)
        lr = jnp.broadcast_to(lr_ref[...], shape)
        li = jnp.broadcast_to(li_ref[...], shape)
        row = lax.broadcasted_iota(jnp.int32, shape, 0)

        def step_of(k):
            return (n - 1 - k) if reverse else k

        def rows_of(k):
            return pl.ds(pl.multiple_of(step_of(k) * N_SEG, N_SEG), N_SEG)

        first = row == s_first
        hr = br_ref[rows_of(0), :] + jnp.where(first, e0r_ref[...], 0.0)
        hi = bi_ref[rows_of(0), :] + jnp.where(first, e0i_ref[...], 0.0)
        hr_ref[rows_of(0), :] = hr
        hi_ref[rows_of(0), :] = hi

        def pass1(k, carry):
            hr, hi = carry
            pr, pi = _cmul(lr, li, hr, hi)
            hr = pr + br_ref[rows_of(k), :]
            hi = pi + bi_ref[rows_of(k), :]
            hr_ref[rows_of(k), :] = hr
            hi_ref[rows_of(k), :] = hi
            return hr, hi

        er, ei = lax.fori_loop(1, n, pass1, (hr, hi))

        lnr, lni = _cpow(lr_ref[...], li_ref[...], n)
        cr, ci = h0r_ref[...], h0i_ref[...]
        cin_r = jnp.zeros(shape, F32)
        cin_i = jnp.zeros(shape, F32)
        for s in seg_order:
            cin_r = jnp.where(row == s, cr, cin_r)
            cin_i = jnp.where(row == s, ci, cin_i)
            if s != s_last:
                pr, pi = _cmul(lnr, lni, cr, ci)
                cr = pr + jnp.sum(jnp.where(row == s, er, 0.0), axis=0, keepdims=True)
                ci = pi + jnp.sum(jnp.where(row == s, ei, 0.0), axis=0, keepdims=True)

        def pass2(k, carry):
            pr, pi = carry
            ar, ai = _cmul(pr, pi, cin_r, cin_i)
            hr = hr_ref[rows_of(k), :] + ar
            hi = hi_ref[rows_of(k), :] + ai
            hr_ref[rows_of(k), :] = hr
            hi_ref[rows_of(k), :] = hi
            npr, npi = _cmul(pr, pi, lr, li)
            return npr, npi

        lax.fori_loop(0, n, pass2, (lr, li))
        last_r = hr_ref[rows_of(n - 1), :]
        last_i = hi_ref[rows_of(n - 1), :]
        htr_ref[...] = jnp.sum(jnp.where(row == s_last, last_r, 0.0), axis=0, keepdims=True)
        hti_ref[...] = jnp.sum(jnp.where(row == s_last, last_i, 0.0), axis=0, keepdims=True)

    big = pl.BlockSpec((rows, cb), lambda j: (0, j))
    vec = pl.BlockSpec((1, cb), lambda j: (0, j))
    return pl.pallas_call(
        body, name=name, grid=(C // cb,),
        in_specs=[big, big] + [vec] * 6,
        out_specs=[big, big, vec, vec],
        out_shape=[jax.ShapeDtypeStruct((rows, C), F32)] * 2 + [jax.ShapeDtypeStruct((1, C), F32)] * 2,
        compiler_params=_cparams(("parallel",)),
    )(b_re, b_im, lam_re, lam_im, h0_re, h0_im, e0_re, e0_im)


def _s5_dlam(mu_re, mu_im, h_re, h_im, h0_re, h0_im, *, reverse, name):
    rows, C = h_re.shape
    n = rows // N_SEG
    cb = _pick(C, 512, LANES)
    s_first = N_SEG - 1 if reverse else 0

    def body(mr_ref, mi_ref, hr_ref, hi_ref, h0r_ref, h0i_ref, dr_ref, di_ref):
        shape = (N_SEG, cb)
        row = lax.broadcasted_iota(jnp.int32, shape, 0)

        def rows_of(k):
            step = (n - 1 - k) if reverse else k
            return pl.ds(pl.multiple_of(step * N_SEG, N_SEG), N_SEG)

        def term(k, pr, pi):
            mr, mi = mr_ref[rows_of(k), :], mi_ref[rows_of(k), :]
            return mr * pr + mi * pi, mi * pr - mr * pi

        shift = N_SEG - 1 if reverse else 1
        pr = jnp.where(row == s_first, h0r_ref[...], pltpu.roll(hr_ref[rows_of(n - 1), :], shift, 0))
        pi = jnp.where(row == s_first, h0i_ref[...], pltpu.roll(hi_ref[rows_of(n - 1), :], shift, 0))
        acc = term(0, pr, pi)

        def loop(k, acc):
            tr, ti = term(k, hr_ref[rows_of(k - 1), :], hi_ref[rows_of(k - 1), :])
            return acc[0] + tr, acc[1] + ti

        ar, ai = lax.fori_loop(1, n, loop, acc)
        dr_ref[...] = jnp.sum(ar, axis=0, keepdims=True)
        di_ref[...] = jnp.sum(ai, axis=0, keepdims=True)

    big = pl.BlockSpec((rows, cb), lambda j: (0, j))
    vec = pl.BlockSpec((1, cb), lambda j: (0, j))
    return pl.pallas_call(
        body, name=name, grid=(C // cb,),
        in_specs=[big] * 4 + [vec] * 2, out_specs=[vec, vec],
        out_shape=[jax.ShapeDtypeStruct((1, C), F32)] * 2,
        compiler_params=_cparams(("parallel",)),
    )(mu_re, mu_im, h_re, h_im, h0_re, h0_im)


NT_DIMS = (((1,), (1,)), ((), ()))
TN_DIMS = (((0,), (0,)), ((), ()))
ATTN_ROW_SPLIT = 2


def _attn_exp(q, k):
    s = lax.dot_general(q, k, NT_DIMS, preferred_element_type=F32)
    e = jnp.exp2((s - jnp.max(s, axis=-1, keepdims=True)) * (ATTN_SCALE * math.log2(math.e)))
    return e, jnp.sum(e, axis=-1, keepdims=True)


def _attn_specs(L, T, tq):
    H = MLA_HEADS
    return [
        pl.BlockSpec((tq, LANES), lambda h, i: (i, h)),
        pl.BlockSpec((tq, LANES), lambda h, i: (i, H + h)),
        pl.BlockSpec((T, LANES), lambda h, i: (0, 2 * h)),
        pl.BlockSpec((T, LANES), lambda h, i: (0, 2 * h + 1)),
        pl.BlockSpec((T, LANES), lambda h, i: (0, 0)),
    ]


def _attn_fwd(qq, kv, kr, *, name):
    L, T = qq.shape[0], kv.shape[0]
    tq = _pick(L, 256, 16)

    def body(qn_ref, qr_ref, kn_ref, v_ref, kr_ref, o_ref):
        k = jnp.concatenate([kn_ref[...], kr_ref[...]], axis=1)
        v = v_ref[...]
        rq = tq // ATTN_ROW_SPLIT
        for r in range(ATTN_ROW_SPLIT):
            rows = pl.ds(r * rq, rq)
            e, l = _attn_exp(jnp.concatenate([qn_ref[rows, :], qr_ref[rows, :]], axis=1), k)
            o = jnp.dot(e.astype(BF16), v, preferred_element_type=F32) * (1.0 / l)
            o_ref[rows, :] = o.astype(o_ref.dtype)

    return pl.pallas_call(
        body, name=name, grid=(MLA_HEADS, L // tq), in_specs=_attn_specs(L, T, tq),
        out_specs=pl.BlockSpec((tq, LANES), lambda h, i: (i, h)),
        out_shape=jax.ShapeDtypeStruct((L, MLA_HEADS * V_DIM), BF16),
        compiler_params=_cparams(("parallel", "parallel")),
    )(qq, qq, kv, kv, kr)


def _attn_bwd(qq, kv, kr, do, *, name):
    L, T = qq.shape[0], kv.shape[0]
    H = MLA_HEADS
    tq = _pick(L, 256, 16)
    nq = L // tq

    def body(qn_ref, qr_ref, kn_ref, v_ref, kr_ref, do_ref, dqn_ref, dqr_ref, dkn_ref, dv_ref, dkr_ref, dk_acc, dv_acc):
        h, i = pl.program_id(0), pl.program_id(1)

        @pl.when(i == 0)
        def _():
            dk_acc[...] = jnp.zeros_like(dk_acc)
            dv_acc[...] = jnp.zeros_like(dv_acc)

        @pl.when((i == 0) & (h == 0))
        def _():
            dkr_ref[...] = jnp.zeros_like(dkr_ref)

        k = jnp.concatenate([kn_ref[...], kr_ref[...]], axis=1)
        v = v_ref[...]
        rq = tq // ATTN_ROW_SPLIT
        for r in range(ATTN_ROW_SPLIT):
            rows = pl.ds(r * rq, rq)
            q = jnp.concatenate([qn_ref[rows, :], qr_ref[rows, :]], axis=1)
            dov = do_ref[rows, :]
            e, l = _attn_exp(q, k)
            inv = 1.0 / l
            ps = e * (inv * ATTN_SCALE)
            t = lax.dot_general(dov, v, NT_DIMS, preferred_element_type=F32) * ps
            ds = (t - ps * (jnp.sum(t, axis=-1, keepdims=True) * (1.0 / ATTN_SCALE))).astype(BF16)
            dq = jnp.dot(ds, k, preferred_element_type=F32)
            dqn_ref[rows, :] = dq[:, :LANES]
            dqr_ref[rows, :] = dq[:, LANES:]
            dv_acc[...] += lax.dot_general(e.astype(BF16), (dov.astype(F32) * inv).astype(BF16), TN_DIMS,
                                           preferred_element_type=F32)
            dk_acc[...] += lax.dot_general(ds, q, TN_DIMS, preferred_element_type=F32)

        @pl.when(i == nq - 1)
        def _():
            dkn_ref[...] = dk_acc[:, :LANES].astype(dkn_ref.dtype)
            dkr_ref[...] += dk_acc[:, LANES:]
            dv_ref[...] = dv_acc[...].astype(dv_ref.dtype)

    in_specs = _attn_specs(L, T, tq) + [pl.BlockSpec((tq, LANES), lambda h, i: (i, h))]
    dqn, dqr, dkn, dv, dkr = pl.pallas_call(
        body, name=name, grid=(H, L // tq), in_specs=in_specs,
        out_specs=[pl.BlockSpec((tq, LANES), lambda h, i: (i, h)), pl.BlockSpec((tq, LANES), lambda h, i: (i, h)),
                   pl.BlockSpec((T, LANES), lambda h, i: (0, h)), pl.BlockSpec((T, LANES), lambda h, i: (0, h)),
                   pl.BlockSpec((T, LANES), lambda h, i: (0, 0))],
        out_shape=[jax.ShapeDtypeStruct((L, H * LANES), F32), jax.ShapeDtypeStruct((L, H * LANES), F32),
                   jax.ShapeDtypeStruct((T, H * LANES), BF16), jax.ShapeDtypeStruct((T, H * LANES), BF16),
                   jax.ShapeDtypeStruct((T, LANES), F32)],
        scratch_shapes=[pltpu.VMEM((T, 2 * LANES), F32), pltpu.VMEM((T, LANES), F32)],
        compiler_params=_cparams(("arbitrary", "arbitrary")),
    )(qq, qq, kv, kv, kr, do)
    return dqn, dqr, dkn, dv, dkr


def _adamw(w, g, m, v, *, name):
    c1 = 1.0 - ADAM_B1 ** ADAM_STEP
    c2 = 1.0 - ADAM_B2 ** ADAM_STEP

    def f(w, g, m, v):
        m = ADAM_B1 * m + (1.0 - ADAM_B1) * g
        v = ADAM_B2 * v + (1.0 - ADAM_B2) * jnp.square(g)
        delta = -ADAM_LR * ((m / c1) / (jnp.sqrt(v / c2) + ADAM_EPS) + ADAM_WD * w)
        return g, delta, m, v

    return _rw(f, [w, g, m, v], [], [F32] * 4, name=name)


def _slab_rows(rows, cols, n_arrays):
    return _pick(rows, max(16, (8 * 1024 * 1024) // (cols * 4 * n_arrays)), 16)


def _scalars(*vals):
    return jnp.stack([jnp.asarray(v, jnp.int32) for v in vals])


def _into_slot(src, slot, nslots, dtype, *, name):
    R, C = src.shape
    tr = _slab_rows(R, C, 2)

    def body(s_ref, x_ref, o_ref):
        o_ref[...] = x_ref[...].astype(o_ref.dtype)

    return pl.pallas_call(
        body, name=name,
        grid_spec=pltpu.PrefetchScalarGridSpec(
            num_scalar_prefetch=1, grid=(R // tr,),
            in_specs=[pl.BlockSpec((tr, C), lambda i, s: (i, 0))],
            out_specs=pl.BlockSpec((None, tr, C), lambda i, s: (s[0], i, 0))),
        out_shape=jax.ShapeDtypeStruct((nslots, R, C), dtype),
        compiler_params=_cparams(("arbitrary",)),
    )(_scalars(slot), src)


def _pair_sum(g, got, c, *, name):
    _, R, C = g.shape
    hr = R // 2
    tr = _slab_rows(hr, C, 3)
    nblk = hr // tr

    def body(s_ref, g_ref, r_ref, o_ref):
        o_ref[...] = (g_ref[...].astype(F32) + r_ref[...].astype(F32)).astype(o_ref.dtype)

    return pl.pallas_call(
        body, name=name,
        grid_spec=pltpu.PrefetchScalarGridSpec(
            num_scalar_prefetch=1, grid=(4, nblk),
            in_specs=[pl.BlockSpec((None, tr, C), lambda j, i, s: (j, s[0] * nblk + i, 0)),
                      pl.BlockSpec((None, tr, C), lambda j, i, s: (j, i, 0))],
            out_specs=pl.BlockSpec((None, tr, C), lambda j, i, s: (j, i, 0))),
        out_shape=jax.ShapeDtypeStruct((4, hr, C), g.dtype),
        compiler_params=_cparams(("arbitrary", "arbitrary")),
    )(_scalars(c), g, got)


def _chip_sum(p, landed, me_chip, c, *, name):
    _, hr, C = p.shape
    tr = _slab_rows(hr, C, 5)

    def body(s_ref, p_ref, l0_ref, l1_ref, l2_ref, o_ref):
        o_ref[...] = ((p_ref[...].astype(F32) + l0_ref[...].astype(F32)) + l1_ref[...].astype(F32)) + l2_ref[...].astype(F32)

    return pl.pallas_call(
        body, name=name,
        grid_spec=pltpu.PrefetchScalarGridSpec(
            num_scalar_prefetch=1, grid=(hr // tr,),
            in_specs=[pl.BlockSpec((None, tr, C), lambda i, s: (s[0], i, 0))]
            + [pl.BlockSpec((None, tr, C), functools.partial(lambda i, s, k: (k, i, 0), k=k)) for k in range(3)],
            out_specs=pl.BlockSpec((None, tr, C), lambda i, s: (s[1], i, 0))),
        out_shape=jax.ShapeDtypeStruct((2, hr, C), F32),
        compiler_params=_cparams(("arbitrary",)),
    )(_scalars(me_chip, c), p, landed, landed, landed)


def _place():
    return lax.axis_index("x"), lax.axis_index("y"), lax.axis_index("c")


def _other_chips(x, y):
    chips = [(1 - x, y), (x, 1 - y), (1 - x, 1 - y)]
    return chips, [2 * cx + cy for cx, cy in chips]


HBM = pl.BlockSpec(memory_space=pl.ANY)


def _allgather8(v, *, name):
    rows, cols = v.shape

    def body(v_ref, out_ref, send_sems, recv_sems):
        x, y, c = _place()
        me = 4 * x + 2 * y + c
        out_ref[me] = v_ref[...]
        copies = []
        for k in range(1, 8):
            bx, by, bc = (k >> 2) & 1, (k >> 1) & 1, k & 1
            px, py, pc = x ^ bx, y ^ by, c ^ bc
            cp = pltpu.make_async_remote_copy(
                src_ref=v_ref, dst_ref=out_ref.at[me], send_sem=send_sems.at[k - 1], recv_sem=recv_sems.at[k - 1],
                device_id=(px, py, pc), device_id_type=MESH)
            cp.start()
            copies.append((cp, 4 * px + 2 * py + pc))
        for k, (cp, peer) in enumerate(copies):
            pltpu.make_async_remote_copy(
                src_ref=v_ref, dst_ref=out_ref.at[peer], send_sem=send_sems.at[k], recv_sem=recv_sems.at[k],
                device_id=(x, y, c), device_id_type=MESH).wait_recv()
        for cp, _ in copies:
            cp.wait_send()

    return pl.pallas_call(
        body, name=name, out_shape=jax.ShapeDtypeStruct((8, rows, cols), v.dtype),
        in_specs=[pl.BlockSpec(memory_space=pltpu.VMEM)], out_specs=pl.BlockSpec(memory_space=pltpu.VMEM),
        scratch_shapes=[pltpu.SemaphoreType.DMA((7,)), pltpu.SemaphoreType.DMA((7,))],
        compiler_params=pltpu.CompilerParams(vmem_limit_bytes=VMEM_LIMIT),
    )(v)


def _allgather_shards(bufs, *, name):
    n = len(bufs)

    def body(*refs):
        outs = refs[n:2 * n]
        send_sems, recv_sems = refs[2 * n:]
        x, y, c = _place()
        me_chip = 2 * x + y
        sibling = (x, y, 1 - c)
        chips, chip_ids = _other_chips(x, y)

        def remote(k, j, blk, hf, to):
            hr = bufs[k].shape[1] // 2
            piece = outs[k].at[blk, pl.ds(pl.multiple_of(hf * hr, 16), hr), :]
            return pltpu.make_async_remote_copy(
                src_ref=piece, dst_ref=piece, send_sem=send_sems.at[6 * k + j], recv_sem=recv_sems.at[6 * k + j],
                device_id=to, device_id_type=MESH)

        sends = []
        for k in range(n):
            for j, chip in enumerate(chips):
                cp = remote(k, j, me_chip, c, (*chip, c))
                cp.start()
                sends.append(cp)
        for k in range(n):
            for j, chip in enumerate(chips):
                remote(k, j, chip_ids[j], c, (x, y, c)).wait_recv()
                cp = remote(k, 3 + j, chip_ids[j], c, sibling)
                cp.start()
                sends.append(cp)
        for k in range(n):
            for j in range(3):
                remote(k, 3 + j, chip_ids[j], 1 - c, (x, y, c)).wait_recv()
        for cp in sends:
            cp.wait_send()

    return list(pl.pallas_call(
        body, name=name, out_shape=[jax.ShapeDtypeStruct(b.shape, b.dtype) for b in bufs],
        in_specs=[HBM] * n, out_specs=[HBM] * n, input_output_aliases={k: k for k in range(n)},
        scratch_shapes=[pltpu.SemaphoreType.DMA((6 * n,)), pltpu.SemaphoreType.DMA((6 * n,))],
    )(*bufs))


def _pair_exchange(gs, *, name):
    n = len(gs)

    def body(*refs):
        ins, outs = refs[:n], refs[n:2 * n]
        send_sems, recv_sems = refs[2 * n:]
        x, y, c = _place()
        copies = []
        for k in range(n):
            hr = gs[k].shape[1] // 2
            src = ins[k].at[:, pl.ds(pl.multiple_of((1 - c) * hr, 16), hr), :]
            cp = pltpu.make_async_remote_copy(src_ref=src, dst_ref=outs[k], send_sem=send_sems.at[k], recv_sem=recv_sems.at[k],
                                              device_id=(x, y, 1 - c), device_id_type=MESH)
            cp.start()
            copies.append(cp)
        for cp in copies:
            cp.wait()

    return list(pl.pallas_call(
        body, name=name,
        out_shape=[jax.ShapeDtypeStruct((4, g.shape[1] // 2, g.shape[2]), g.dtype) for g in gs],
        in_specs=[HBM] * n, out_specs=[HBM] * n,
        scratch_shapes=[pltpu.SemaphoreType.DMA((n,)), pltpu.SemaphoreType.DMA((n,))],
    )(*gs))


def _pair_gather(bufs, *, name):
    n = len(bufs)

    def body(*refs):
        outs = refs[n:2 * n]
        send_sems, recv_sems = refs[2 * n:]
        x, y, c = _place()

        def remote(k, hf, to):
            return pltpu.make_async_remote_copy(src_ref=outs[k].at[hf], dst_ref=outs[k].at[hf], send_sem=send_sems.at[k],
                                                recv_sem=recv_sems.at[k], device_id=to, device_id_type=MESH)

        copies = [remote(k, c, (x, y, 1 - c)) for k in range(n)]
        for cp in copies:
            cp.start()
        for k, cp in enumerate(copies):
            cp.wait_send()
            remote(k, 1 - c, (x, y, c)).wait_recv()

    return list(pl.pallas_call(
        body, name=name, out_shape=[jax.ShapeDtypeStruct(b.shape, b.dtype) for b in bufs],
        in_specs=[HBM] * n, out_specs=[HBM] * n, input_output_aliases={k: k for k in range(n)},
        scratch_shapes=[pltpu.SemaphoreType.DMA((n,)), pltpu.SemaphoreType.DMA((n,))],
    )(*bufs))


HBM_SPEC = pl.BlockSpec(memory_space=pltpu.HBM)
SEM_SPEC = pl.BlockSpec(memory_space=pltpu.SEMAPHORE)
EFFECT = pltpu.SideEffectType.DATAFLOW_SIDE_EFFECTING
TOKEN = jax.ShapeDtypeStruct((SUBLANES, LANES), F32)


def _in_hbm(a):
    return pltpu.with_memory_space_constraint(a, pltpu.HBM)


def _ici_copies(srcs, dsts, send_sems, recv_sems, send):
    x, y, c = _place()
    me_chip = 2 * x + y
    chips, chip_ids = _other_chips(x, y)
    out = []
    for k, (src, dst) in enumerate(zip(srcs, dsts)):
        for j, chip in enumerate(chips):
            s_ref, d_ref = (src(k, me_chip, chip_ids[j], j), dst(k, me_chip, chip_ids[j], j))
            out.append(pltpu.make_async_remote_copy(
                src_ref=s_ref if send else d_ref, dst_ref=d_ref, send_sem=send_sems.at[3 * k + j],
                recv_sem=recv_sems.at[3 * k + j], device_id=(*chip, c) if send else (x, y, c), device_id_type=MESH))
    return out


def _half_rows(buf, hf):
    hr = buf.shape[1] // 2
    return pl.ds(pl.multiple_of(hf * hr, 16), hr)


def _ag_pieces(refs):
    c = lax.axis_index("c")
    src = [functools.partial(lambda k, me, other, j, r: r.at[me, _half_rows(r, c), :], r=r) for r in refs]
    dst_send = src
    dst_recv = [functools.partial(lambda k, me, other, j, r: r.at[other, _half_rows(r, c), :], r=r) for r in refs]
    return src, dst_send, dst_recv


def _ag_start(bufs, groups, *, name):
    n, ng = len(bufs), len(groups)

    def body(*refs):
        sems = refs[n:n + 2 * ng]
        thru = refs[n + 2 * ng:2 * n + 2 * ng]
        token = refs[-1]
        for g, ks in enumerate(groups):
            src, dst_send, _ = _ag_pieces([thru[k] for k in ks])
            for cp in _ici_copies(src, dst_send, sems[2 * g], sems[2 * g + 1], True):
                cp.start()
        token[...] = jnp.zeros_like(token)

    out_shape = tuple(pltpu.SemaphoreType.DMA((3 * len(ks),)) for ks in groups for _ in range(2))
    out_shape += tuple(pltpu.HBM(b.shape, b.dtype) for b in bufs) + (TOKEN,)
    res = pl.pallas_call(
        body, name=name, out_shape=out_shape, in_specs=(HBM_SPEC,) * n,
        out_specs=(SEM_SPEC,) * (2 * ng) + (HBM_SPEC,) * n + (pl.BlockSpec(memory_space=pltpu.VMEM),),
        input_output_aliases={k: 2 * ng + k for k in range(n)},
        compiler_params=pltpu.CompilerParams(has_side_effects=EFFECT),
    )(*[_in_hbm(b) for b in bufs])
    sems = [(res[2 * g], res[2 * g + 1]) for g in range(ng)]
    return sems, list(res[2 * ng:2 * ng + n]), res[-1]


def _ag_wait(bufs, send_sems, recv_sems, after, *, name):
    n = len(bufs)

    def body(*refs):
        ins = refs[:n]
        send, recv = refs[n], refs[n + 1]
        src, dst_send, dst_recv = _ag_pieces(ins)
        for cp in _ici_copies(src, dst_send, send, recv, True):
            cp.wait_send()
        for cp in _ici_copies(src, dst_recv, send, recv, False):
            cp.wait_recv()

    return list(pl.pallas_call(
        body, name=name, out_shape=tuple(pltpu.HBM(b.shape, b.dtype) for b in bufs),
        in_specs=(HBM_SPEC,) * n + (SEM_SPEC, SEM_SPEC, pl.BlockSpec(memory_space=pl.ANY)),
        out_specs=(HBM_SPEC,) * n, input_output_aliases={k: k for k in range(n)},
        compiler_params=pltpu.CompilerParams(has_side_effects=EFFECT),
    )(*bufs, send_sems, recv_sems, after))


def _ag_forward(bufs, *, name):
    n = len(bufs)

    def body(*refs):
        outs = refs[n:2 * n]
        send_sems, recv_sems = refs[2 * n:]
        x, y, c = _place()
        _, chip_ids = _other_chips(x, y)

        def remote(k, j, hf, to):
            piece = outs[k].at[chip_ids[j], _half_rows(outs[k], hf), :]
            return pltpu.make_async_remote_copy(src_ref=piece, dst_ref=piece, send_sem=send_sems.at[3 * k + j],
                                                recv_sem=recv_sems.at[3 * k + j], device_id=to, device_id_type=MESH)

        sends = [remote(k, j, c, (x, y, 1 - c)) for k in range(n) for j in range(3)]
        for cp in sends:
            cp.start()
        for k in range(n):
            for j in range(3):
                remote(k, j, 1 - c, (x, y, c)).wait_recv()
        for cp in sends:
            cp.wait_send()

    return list(pl.pallas_call(
        body, name=name, out_shape=[jax.ShapeDtypeStruct(b.shape, b.dtype) for b in bufs],
        in_specs=[HBM] * n, out_specs=[HBM] * n, input_output_aliases={k: k for k in range(n)},
        scratch_shapes=[pltpu.SemaphoreType.DMA((3 * n,)), pltpu.SemaphoreType.DMA((3 * n,))],
    )(*bufs))


def _rs_pieces(p_refs, l_refs):
    src = [functools.partial(lambda k, me, other, j, r: r.at[other], r=r) for r in p_refs]
    dst = [functools.partial(lambda k, me, other, j, r: r.at[j], r=r) for r in l_refs]
    return src, dst


def _rs_start(ps, *, name):
    n = len(ps)
    lands = [lax.empty((3,) + p.shape[1:], p.dtype) for p in ps]

    def body(*refs):
        send, recv = refs[2 * n], refs[2 * n + 1]
        p_thru = refs[2 * n + 2:3 * n + 2]
        l_thru = refs[3 * n + 2:4 * n + 2]
        token = refs[-1]
        src, dst = _rs_pieces(p_thru, l_thru)
        for cp in _ici_copies(src, dst, send, recv, True):
            cp.start()
        token[...] = jnp.zeros_like(token)

    out_shape = (pltpu.SemaphoreType.DMA((3 * n,)), pltpu.SemaphoreType.DMA((3 * n,)))
    out_shape += tuple(pltpu.HBM(a.shape, a.dtype) for a in list(ps) + lands) + (TOKEN,)
    res = pl.pallas_call(
        body, name=name, out_shape=out_shape, in_specs=(HBM_SPEC,) * (2 * n),
        out_specs=(SEM_SPEC, SEM_SPEC) + (HBM_SPEC,) * (2 * n) + (pl.BlockSpec(memory_space=pltpu.VMEM),),
        input_output_aliases={k: 2 + k for k in range(2 * n)},
        compiler_params=pltpu.CompilerParams(has_side_effects=EFFECT),
    )(*[_in_hbm(a) for a in list(ps) + lands])
    return (res[0], res[1]), list(res[2:2 + n]), list(res[2 + n:2 + 2 * n]), res[-1]


def _rs_wait(ps, lands, send_sems, recv_sems, after, *, name):
    n = len(ps)

    def body(*refs):
        p_in, l_in = refs[:n], refs[n:2 * n]
        send, recv = refs[2 * n], refs[2 * n + 1]
        src, dst = _rs_pieces(p_in, l_in)
        for cp in _ici_copies(src, dst, send, recv, True):
            cp.wait_send()
        for cp in _ici_copies(src, dst, send, recv, False):
            cp.wait_recv()

    res = pl.pallas_call(
        body, name=name, out_shape=tuple(pltpu.HBM(a.shape, a.dtype) for a in list(ps) + list(lands)),
        in_specs=(HBM_SPEC,) * (2 * n) + (SEM_SPEC, SEM_SPEC, pl.BlockSpec(memory_space=pl.ANY)),
        out_specs=(HBM_SPEC,) * (2 * n), input_output_aliases={k: k for k in range(2 * n)},
        compiler_params=pltpu.CompilerParams(has_side_effects=EFFECT),
    )(*ps, *lands, send_sems, recv_sems, after)
    return list(res[:n]), list(res[n:])


def _rs_begin(gs, tag):
    c = lax.axis_index("c")
    got = _pair_exchange(gs, name=f"rs_pair_exchange_{tag}")
    pair = [_pair_sum(g, r, c, name=f"rs_pair_sum_{tag}{k}") for k, (g, r) in enumerate(zip(gs, got))]
    sems, pair, lands, token = _rs_start(pair, name=f"rs_start_{tag}")
    return (sems, pair, lands), token


def _rs_end(handle, after, tag):
    x, y, c = _place()
    (send, recv), pair, lands = handle
    pair, lands = _rs_wait(pair, lands, send, recv, after, name=f"rs_wait_{tag}")
    halves = [_chip_sum(p, l, 2 * x + y, c, name=f"rs_chip_sum_{tag}{k}") for k, (p, l) in enumerate(zip(pair, lands))]
    full = _pair_gather(halves, name=f"rs_pair_gather_{tag}")
    return [f.reshape(2 * f.shape[1], f.shape[2]) for f in full]


def _to_segments(a):
    rows = a.shape[0]
    return a.reshape(N_SEG, rows // N_SEG, -1).transpose(1, 0, 2).reshape(rows, -1)


def _from_segments(a):
    rows = a.shape[0]
    return a.reshape(rows // N_SEG, N_SEG, -1).transpose(1, 0, 2).reshape(rows, -1)


def _rope_tables(L):
    t = jnp.arange(L, dtype=jnp.int32)
    row = (t // GRID_W).astype(F32)
    col = (t % GRID_W).astype(F32)
    n_freq = QK_ROPE // 4
    inv = ROPE_BASE ** (-jnp.arange(n_freq, dtype=F32) / n_freq)
    a0, a1 = row[:, None] * inv, col[:, None] * inv
    z = jnp.zeros((L, LANES - QK_ROPE), F32)
    cos = jnp.concatenate([jnp.cos(a0), jnp.cos(a0), jnp.cos(a1), jnp.cos(a1), z], axis=1)
    sin = jnp.concatenate([-jnp.sin(a0), jnp.sin(a0), -jnp.sin(a1), jnp.sin(a1), z], axis=1)
    return _to_segments(cos), _to_segments(sin)


def _col_blocks(w, nblk):
    r, c = w.shape
    return w.reshape(r, nblk, c // nblk).transpose(1, 0, 2)


def _from_col_blocks(w4):
    nblk, r, c = w4.shape
    return w4.transpose(1, 0, 2).reshape(r, nblk * c)


def _s5_discretize(a_re, a_im, log_dt, b_re, b_im):
    dt = jnp.exp(log_dt)[:, None]
    mag = jnp.exp(a_re * dt)
    ab_re, ab_im = mag * jnp.cos(a_im * dt), mag * jnp.sin(a_im * dt)
    den = a_re * a_re + a_im * a_im
    nr, ni = ab_re - 1.0, ab_im
    co_re = (nr * a_re + ni * a_im) / den
    co_im = (ni * a_re - nr * a_im) / den
    bb_re = co_re[..., None] * b_re - co_im[..., None] * b_im
    bb_im = co_re[..., None] * b_im + co_im[..., None] * b_re
    return ab_re, ab_im, bb_re, bb_im


def _diag_blocks_in(bb, gpb):
    G, N, P = bb.shape
    eye = jnp.eye(gpb, dtype=bb.dtype)
    t = jnp.einsum("jgnp,gh->jgphn", bb.reshape(G // gpb, gpb, N, P), eye)
    return t.reshape(G // gpb, gpb * P, gpb * N)


def _diag_blocks_out(cc, gpb):
    G, P, N = cc.shape
    eye = jnp.eye(gpb, dtype=cc.dtype)
    t = jnp.einsum("jgpn,gh->jgnhp", cc.reshape(G // gpb, gpb, P, N), eye)
    return t.reshape(G // gpb, gpb * N, gpb * P)


def _tr(ws):
    return [jnp.swapaxes(w, 1, 2) for w in ws]


WEIGHTS = ['c_ctx', 'w_mod', 'b_mod', 'norm1', 'norm2', 'w_in', 's5_a_re', 's5_a_im', 's5_log_dt', 's5_b_re', 's5_b_im',
           's5_c_re', 's5_c_im', 's5_d', 'w_glu', 'q_norm', 'kv_norm', 'w_uq', 'w_ukv', 'w_mla_o', 'w_out', 'w_ffn_in',
           'w_ffn_out', 'norm_f']
AG_GROUPS = [['w_in'], ['w_glu', 'w_uq', 'w_ukv', 'w_mla_o', 'w_out'], ['w_ffn_in', 'w_ffn_out']]
SMALL = ['norm1', 'norm2', 's5_a_re', 's5_a_im', 's5_log_dt', 's5_b_re', 's5_b_im', 's5_c_re', 's5_c_im', 's5_d',
         'q_norm', 'kv_norm', 'norm_f']


def _pad_rows(a, rows):
    return jnp.concatenate([a, jnp.zeros((rows - a.shape[0],) + a.shape[1:], a.dtype)], axis=0)


def _pack(vals, width, rows):
    flat = jnp.concatenate([v.reshape(-1).astype(F32) for v in vals])
    flat = jnp.concatenate([flat, jnp.zeros((rows * width - flat.shape[0],), F32)])
    return flat.reshape(rows, width)


def _unpack(buf, like):
    flat = buf.reshape(-1)
    out, pos = [], 0
    for v in like:
        out.append(flat[pos:pos + v.size].reshape(v.shape))
        pos += v.size
    return out


def _step(x, c, ctx, loss_target, w, m, v):
    px, py, pc = _place()
    me = 4 * px + 2 * py + pc
    me_chip = 2 * px + py
    L, D = x.shape[1], x.shape[2]
    Lc = ctx.shape[1]
    T = L + Lc
    SW = D // 2
    G = SW // S5_GROUP
    C = G * S5_STATE
    H = MLA_HEADS
    q_rank = w['q_norm'].shape[1]
    kv_rank = w['kv_norm'].shape[1]
    d_ff = w['w_ffn_out'].shape[1] * 4
    wa_used = SW + q_rank + kv_rank + QK_ROPE
    WA = -(-(SW + q_rank + kv_rank + LANES) // 512) * 512

    c_rows = _pad_rows(c.astype(F32), SUBLANES)
    c_all = _allgather8(c_rows, name="ag_cond")[:, 0, :]
    cond = jnp.concatenate([c_all, w['c_ctx'].reshape(1, D)], axis=0)
    cond = _pad_rows(cond, 16)
    (act,) = _rw(lambda t: (jax.nn.silu(t),), [cond], [], [F32], name="cond_silu")
    w_mod, cs_mod = w['w_mod'][0], w['w_mod'].shape[2]
    mod_part = _mm(act, w_mod, out_dtype=F32, name="mod_fwd")
    mod_all = _allgather8(mod_part, name="ag_mod")
    mod_full = jnp.concatenate([mod_all[0], mod_all[2], mod_all[4], mod_all[6]], axis=1) + w['b_mod']
    m_lat = lax.dynamic_slice_in_dim(mod_full, me, 1, axis=0).reshape(6, D)
    m_ctx = mod_full[8].reshape(6, D)
    sh1, sc1, g1, sh2, sc2, g2 = (m_lat[i:i + 1] for i in range(6))
    csh1, csc1 = m_ctx[0:1], m_ctx[1:2]

    names = [nme for grp in AG_GROUPS for nme in grp]
    bufs = [_into_slot(w[nme][0], me_chip, 4, BF16, name=f"cast_{nme}") for nme in names]
    group_idx, pos = [], 0
    for grp in AG_GROUPS:
        group_idx.append(list(range(pos, pos + len(grp))))
        pos += len(grp)
    ag_sems, bufs, ag_token = _ag_start(bufs, group_idx, name="ag_start")
    gathered = {}

    def arrive(g, after):
        got = _ag_wait([bufs[k] for k in group_idx[g]], *ag_sems[g], after, name=f"ag_wait_{g}")
        gathered.update(zip(AG_GROUPS[g], _ag_forward(got, name=f"ag_forward_{g}")))

    xs = _to_segments(x[0])
    cs = _to_segments(ctx[0])
    tgt = _to_segments(loss_target[0])
    cos, sin = _rope_tables(L)
    n1, n2, nf = w['norm1'], w['norm2'], w['norm_f'].reshape(1, D)
    qg, kvg = w['q_norm'], w['kv_norm']

    (xn_lat,) = _rw(_f_norm_mod, [xs], [n1 + ag_token[0, 0], sc1, sh1], [BF16], name="norm1_lat")
    (xn_ctx,) = _rw(_f_norm_mod, [cs], [n1, csc1, csh1], [BF16], name="norm1_ctx")
    xn = jnp.concatenate([xn_lat, xn_ctx], axis=0)
    arrive(0, xn)
    w_in = _from_col_blocks(gathered['w_in'])
    w_a = jnp.concatenate([w_in[:, :wa_used], jnp.zeros((D, WA - wa_used), BF16)], axis=1)
    w_g = w_in[:, wa_used:]
    ha = _mm(xn, w_a, out_dtype=F32, name="in_proj")
    ha_lat, ha_ctx = ha[:L], ha[L:]
    gt = _mm(xn_lat, w_g, out_dtype=F32, name="in_gates")
    f_post_lat = _make_f_post_in(SW, q_rank, kv_rank, True)
    f_post_ctx = _make_f_post_in(SW, q_rank, kv_rank, False)
    u_lat, cqn, ckvn_lat, kr_lat = _rw(f_post_lat, [ha_lat, cos, sin], [qg, kvg], [F32, BF16, BF16, BF16], name="post_in_lat")
    u_ctx, ckvn_ctx, kr_ctx = _rw(f_post_ctx, [ha_ctx], [kvg], [F32, BF16, BF16], name="post_in_ctx")

    gpb = min(S5_BLOCK_GROUPS, G)
    gpo = min(8, G)
    d_skip = w['s5_d'][0].reshape(1, SW)
    disc, vjp_disc, w_b, w_c = [], [], [], []
    for d in range(2):
        prm = (w['s5_a_re'][0, d], w['s5_a_im'][0, d], w['s5_log_dt'][0, d], w['s5_b_re'][0, d], w['s5_b_im'][0, d])

        def prep(a_re, a_im, log_dt, b_re, b_im):
            ab_re, ab_im, bb_re, bb_im = _s5_discretize(a_re, a_im, log_dt, b_re, b_im)
            return ab_re.reshape(1, C), ab_im.reshape(1, C), _diag_blocks_in(bb_re, gpb), _diag_blocks_in(bb_im, gpb)

        out, vj = jax.vjp(prep, *prm)
        disc.append(out)
        vjp_disc.append(vj)
        w_b += [out[2], out[3]]
        w_c += [_diag_blocks_out(w['s5_c_re'][0, d], gpo), -_diag_blocks_out(w['s5_c_im'][0, d], gpo)]
    nb_in = G // gpb
    nb_out = G // gpo
    bu_lat = _bd_fanout(u_lat, w_b, name="s5_bu_lat")
    bu_ctx = _bd_fanout(u_ctx, w_b, name="s5_bu_ctx")
    zero = jnp.zeros((1, C), F32)
    h_lat, h_ctx, hT_ctx = [], [], []
    for d, rev in enumerate((False, True)):
        lr, li = disc[d][0], disc[d][1]
        hcr, hci, tr, ti = _s5_scan(bu_ctx[2 * d], bu_ctx[2 * d + 1], lr, li, zero, zero, zero, zero, reverse=rev,
                                    name=f"s5_scan_ctx_{d}")
        hlr, hli, _, _ = _s5_scan(bu_lat[2 * d], bu_lat[2 * d + 1], lr, li, tr, ti, zero, zero, reverse=rev,
                                  name=f"s5_scan_lat_{d}")
        h_ctx += [hcr, hci]
        h_lat += [hlr, hli]
        hT_ctx += [tr, ti]
    r5 = _bd_fanin(h_lat, w_c, name="s5_readout")
    (z,) = _rw(_f_s5post, [u_lat, r5], [d_skip], [BF16], name="s5_post")

    arrive(1, z)
    w_glu, w_ukv, w_mla_o = (gathered[nme] for nme in ('w_glu', 'w_ukv', 'w_mla_o'))
    w_out = gathered['w_out'].reshape(D, D)
    uq3 = _from_col_blocks(gathered['w_uq']).reshape(q_rank, H, QK_NOPE + QK_ROPE)
    w_q2 = jnp.concatenate([
        uq3[:, :, :QK_NOPE].reshape(q_rank, H * QK_NOPE),
        jnp.concatenate([uq3[:, :, QK_NOPE:], jnp.zeros((q_rank, H, LANES - QK_ROPE), BF16)], axis=2).reshape(q_rank, H * LANES),
    ], axis=1)
    q2 = _mm(cqn, w_q2, out_dtype=F32, name="q_up")
    (qq,) = _rw(_f_qpost, [q2, cos, sin], [], [BF16], name="q_rope")
    kvn = jnp.concatenate([ckvn_lat, ckvn_ctx], axis=0)
    kr_all = jnp.concatenate([kr_lat, kr_ctx], axis=0)
    kv = _mm(kvn, w_ukv, b_shards=4, out_dtype=BF16, name="kv_up")
    o = _attn_fwd(qq, kv, kr_all, name="attn_fwd")

    ab = _mm(z, w_glu, b_shards=4, out_dtype=F32, name="glu_proj")
    bm = _mm(o, w_mla_o, b_shards=4, out_dtype=F32, name="mla_out")
    (mix,) = _rw(_f_merge, [ab, bm, gt], [], [BF16], name="merge")
    out1 = _mm(mix, w_out, out_dtype=F32, name="out_proj")
    x1, xn2 = _rw(_f_resid_norm, [xs, out1], [g1, n2, sc2, sh2], [F32, BF16], name="resid_norm2")
    arrive(2, xn2)
    w_ffn_in = gathered['w_ffn_in']
    w_ffn_out = gathered['w_ffn_out'].reshape(d_ff, D)
    ab2 = _mm(xn2, w_ffn_in, b_shards=4, out_dtype=F32, name="ffn_in")
    (hmid,) = _rw(_f_swiglu, [ab2], [], [BF16], name="ffn_act")
    f2 = _mm(hmid, w_ffn_out, out_dtype=F32, name="ffn_out")
    (row_loss,) = _rw(_f_final, [x1, f2, tgt], [g2, nf], [F32], name="final_loss")
    loss = lax.psum(jnp.sum(row_loss), ("x", "y", "c"))

    ones = jnp.ones((L, 1), F32)
    (dx1_a, df2), (dg2, dnf) = _rw_vjp(_f_final, [x1, f2, tgt], [g2, nf], [[ones]], [True, True, False], [True, True],
                                       [F32, BF16], name="final_loss_bwd")
    dhmid = _mm(df2, w_ffn_out, tb=True, out_dtype=F32, name="ffn_out_dx")
    gw_ffn_out = _mm(hmid, df2, ta=True, out_dtype=BF16, name="ffn_out_dw")
    (dab2,), _ = _rw_vjp(_f_swiglu, [ab2], [], [[dhmid]], [True], [], [BF16], name="ffn_act_bwd")
    dxn2 = _mm(dab2, w_ffn_in, tb=True, b_shards=4, out_dtype=F32, name="ffn_in_dx")
    gw_ffn_in = _mm(xn2, dab2, ta=True, out_shards=4, out_dtype=BF16, name="ffn_in_dw")
    rs_ffn, tok = _rs_begin([gw_ffn_out.reshape(4, -1, D), gw_ffn_in], "ffn")
    (dx_a, dout1), (dg1, dn2, dsc2, dsh2) = _rw_vjp(
        _f_resid_norm, [xs, out1], [g1, n2 + tok[0, 0], sc2, sh2], [[dx1_a], [dxn2]], [True, True], [True] * 4, [F32, BF16],
        name="resid_norm2_bwd")
    dmix = _mm(dout1, w_out, tb=True, out_dtype=F32, name="out_proj_dx")
    gw_out = _mm(mix, dout1, ta=True, out_dtype=BF16, name="out_proj_dw")
    (dab, dbm, dgt), _ = _rw_vjp(_f_merge, [ab, bm, gt], [], [[dmix]], [True] * 3, [], [BF16] * 3, name="merge_bwd")
    dz = _mm(dab, w_glu, tb=True, b_shards=4, out_dtype=F32, name="glu_proj_dx")
    gw_glu = _mm(z, dab, ta=True, out_shards=4, out_dtype=BF16, name="glu_proj_dw")
    do = _mm(dbm, w_mla_o, tb=True, b_shards=4, out_dtype=BF16, name="mla_out_dx")
    gw_mla_o = _mm(o, dbm, ta=True, out_shards=4, out_dtype=BF16, name="mla_out_dw")
    dxn_g = _mm(dgt, w_g, tb=True, out_dtype=F32, name="in_gates_dx")
    gw_g = _mm(xn_lat, dgt, ta=True, out_dtype=BF16, name="in_gates_dw")

    (du_a, dr5), (dd_skip,) = _rw_vjp(_f_s5post, [u_lat, r5], [d_skip], [[dz]], [True, True], [True], [F32, F32],
                                      name="s5_post_bwd")
    dh_lat = _bd_fanout(dr5, _tr(w_c), name="s5_readout_dx")
    dw_c = _bd_dw(h_lat, [dr5] * 4, nb_out, name="s5_readout_dw")
    zeros_ctx = jnp.zeros((Lc, C), F32)
    mu_lat, mu_ctx, dlam = [], [], []
    for d, rev in enumerate((False, True)):
        lr, li = disc[d][0], disc[d][1]
        mlr, mli, fr, fi = _s5_scan(dh_lat[2 * d], dh_lat[2 * d + 1], lr, -li, zero, zero, zero, zero, reverse=not rev,
                                    name=f"s5_adj_lat_{d}")
        dh0r, dh0i = _cmul(lr, -li, fr, fi)
        mcr, mci, _, _ = _s5_scan(zeros_ctx, zeros_ctx, lr, -li, zero, zero, dh0r, dh0i, reverse=not rev,
                                  name=f"s5_adj_ctx_{d}")
        dl_lat = _s5_dlam(mlr, mli, h_lat[2 * d], h_lat[2 * d + 1], hT_ctx[2 * d], hT_ctx[2 * d + 1], reverse=rev,
                          name=f"s5_dlam_lat_{d}")
        dl_ctx = _s5_dlam(mcr, mci, h_ctx[2 * d], h_ctx[2 * d + 1], zero, zero, reverse=rev, name=f"s5_dlam_ctx_{d}")
        mu_lat += [mlr, mli]
        mu_ctx += [mcr, mci]
        dlam.append((dl_lat[0] + dl_ctx[0], dl_lat[1] + dl_ctx[1]))
    du_b = _bd_fanin(mu_lat, _tr(w_b), name="s5_bu_lat_dx")
    du_ctx = _bd_fanin(mu_ctx, _tr(w_b), name="s5_bu_ctx_dx")
    dw_b_lat = _bd_dw([u_lat] * 4, mu_lat, nb_in, name="s5_bu_lat_dw")
    dw_b_ctx = _bd_dw([u_ctx] * 4, mu_ctx, nb_in, name="s5_bu_ctx_dw")
    g_s5 = {}
    for d in range(2):
        ct = (dlam[d][0], dlam[d][1], dw_b_lat[2 * d] + dw_b_ctx[2 * d], dw_b_lat[2 * d + 1] + dw_b_ctx[2 * d + 1])
        ga_re, ga_im, gdt, gb_re, gb_im = vjp_disc[d](ct)
        _, vj_c = jax.vjp(lambda cr, ci: (_diag_blocks_out(cr, gpo), -_diag_blocks_out(ci, gpo)),
                          w['s5_c_re'][0, d], w['s5_c_im'][0, d])
        gc_re, gc_im = vj_c((dw_c[2 * d], dw_c[2 * d + 1]))
        for nme, val in (('s5_a_re', ga_re), ('s5_a_im', ga_im), ('s5_log_dt', gdt), ('s5_b_re', gb_re),
                         ('s5_b_im', gb_im), ('s5_c_re', gc_re), ('s5_c_im', gc_im)):
            g_s5.setdefault(nme, []).append(val)
    g_small = {nme: jnp.stack(vals)[None] for nme, vals in g_s5.items()}
    g_small['s5_d'] = dd_skip.reshape(w['s5_d'].shape)

    dqn, dqr, dkn, dv, dkr = _attn_bwd(qq, kv, kr_all, do, name="attn_bwd")
    dqq = jnp.concatenate([dqn, dqr], axis=1)
    (dq2,), _ = _rw_vjp(_f_qpost, [q2, cos, sin], [], [[dqq]], [True, False, False], [], [BF16], name="q_rope_bwd")
    dcqn = _mm(dq2, w_q2, tb=True, out_dtype=F32, name="q_up_dx")
    gw_q2 = _mm(cqn, dq2, ta=True, out_dtype=BF16, name="q_up_dw")
    dkv = jnp.stack([dkn.reshape(T, H, LANES), dv.reshape(T, H, LANES)], axis=2).reshape(T, 2 * H * LANES)
    dckvn = _mm(dkv, w_ukv, tb=True, b_shards=4, out_dtype=F32, name="kv_up_dx")
    gw_ukv = _mm(kvn, dkv, ta=True, out_shards=4, out_dtype=BF16, name="kv_up_dw")
    uq_nope = gw_q2[:, :H * QK_NOPE].reshape(q_rank, H, QK_NOPE)
    uq_rope = gw_q2[:, H * QK_NOPE:].reshape(q_rank, H, LANES)[:, :, :QK_ROPE]
    gw_uq = jnp.concatenate([uq_nope, uq_rope], axis=2).reshape(q_rank, H * (QK_NOPE + QK_ROPE))
    rs_mix, tok = _rs_begin([gw_out.reshape(4, -1, D), gw_glu, gw_mla_o, _col_blocks(gw_uq, 4), gw_ukv], "mix")

    (dha_lat,), (dqg, dkvg_lat) = _rw_vjp(
        f_post_lat, [ha_lat, cos, sin], [qg, kvg + tok[0, 0]], [[du_a, du_b], [dcqn], [dckvn[:L]], [dkr[:L]]],
        [True, False, False], [True, True], [BF16], name="post_in_lat_bwd")
    (dha_ctx,), (dkvg_ctx,) = _rw_vjp(f_post_ctx, [ha_ctx], [kvg], [[du_ctx], [dckvn[L:]], [dkr[L:]]], [True], [True],
                                      [BF16], name="post_in_ctx_bwd")
    dha = jnp.concatenate([dha_lat, dha_ctx], axis=0)
    dxn = _mm(dha, w_a, tb=True, out_dtype=F32, name="in_proj_dx")
    gw_a = _mm(xn, dha, ta=True, out_dtype=BF16, name="in_proj_dw")
    (dx_seg,), (dn1_lat, dsc1, dsh1) = _rw_vjp(
        _f_norm_mod_keep, [xs], [n1, sc1, sh1], [[dxn[:L], dxn_g], [dx_a]], [True], [True] * 3, [F32], name="norm1_lat_bwd")
    _, (dn1_ctx, dcsc1, dcsh1) = _rw_vjp(_f_norm_mod, [cs], [n1, csc1, csh1], [[dxn[L:]]], [False], [True] * 3, [],
                                         name="norm1_ctx_bwd")
    grad_x = _from_segments(dx_seg)[None]
    g_small.update(norm1=dn1_lat + dn1_ctx, norm2=dn2, q_norm=dqg, kv_norm=dkvg_lat + dkvg_ctx, norm_f=dnf.reshape(D))

    zD = jnp.zeros((1, D), F32)
    dm = jnp.concatenate([
        jnp.concatenate([dsh1, dsc1, dg1, dsh2, dsc2, dg2], axis=1),
        jnp.concatenate([dcsh1, dcsc1, zD, zD, zD, zD], axis=1),
    ], axis=0)
    dm_all = _allgather8(_pad_rows(dm, SUBLANES), name="ag_dmod")
    dm_ctx = dm_all[0, 1]
    for k in range(1, 8):
        dm_ctx = dm_ctx + dm_all[k, 1]
    dmod = _pad_rows(jnp.concatenate([dm_all[:, 0, :], dm_ctx[None]], axis=0), 16)
    g_b_mod = jnp.sum(dmod, axis=0, keepdims=True)
    dmod_mine = lax.dynamic_slice_in_dim(dmod, me_chip * cs_mod, cs_mod, axis=1)
    g_w_mod = _mm(act, dmod_mine, ta=True, out_dtype=F32, name="mod_dw")
    dact_part = _mm(dmod_mine, w_mod, tb=True, out_dtype=F32, name="mod_dx")
    dact_all = _allgather8(dact_part, name="ag_dact")
    dact = dact_all[0] + dact_all[2] + dact_all[4] + dact_all[6]
    (dcond_rows,), _ = _rw_vjp(lambda t: (jax.nn.silu(t),), [cond], [], [[dact]], [True], [], [F32], name="cond_silu_bwd")
    g_c_ctx = dcond_rows[8]

    gw_in = jnp.concatenate([gw_a[:, :wa_used], gw_g], axis=1)
    small_vals = [g_small[nme] for nme in SMALL]
    n_small = sum(val.size for val in small_vals)
    small_rows = -(-n_small // (LANES * 4 * 32)) * 32
    rs_in, _ = _rs_begin([_col_blocks(gw_in, 4), _pack(small_vals, LANES, 4 * small_rows).reshape(4, small_rows, LANES)], "in")

    grads, delta, new_m, new_v = {}, {}, {}, {}

    def update(nme, red):
        res = _adamw(w[nme][0], red, m[nme][0], v[nme][0], name=f"adamw_{nme}")
        grads[nme], delta[nme], new_m[nme], new_v[nme] = (r.reshape(w[nme].shape) for r in res)
        return res[1]

    after = update('w_mod', g_w_mod)
    for handle, tag, members in ((rs_ffn, "ffn", ['w_ffn_out', 'w_ffn_in']),
                                 (rs_mix, "mix", ['w_out', 'w_glu', 'w_mla_o', 'w_uq', 'w_ukv']),
                                 (rs_in, "in", ['w_in'])):
        reduced = _rs_end(handle, after, tag)
        for nme, red in zip(members, reduced):
            after = update(nme, red)
    small_mine = reduced[-1]
    small_buf = _into_slot(small_mine, me_chip, 4, F32, name="small_grads_slot")
    small_all = _allgather_shards([small_buf], name="ag_small_grads")[0].reshape(4 * small_rows, LANES)
    g_small_red = dict(zip(SMALL, _unpack(small_all, [w[nme] for nme in SMALL])))
    rest = SMALL + ['c_ctx', 'b_mod']
    g_rest = dict(g_small_red, c_ctx=g_c_ctx, b_mod=g_b_mod)
    rows_rest = -(-sum(w[nme].size for nme in rest) // (LANES * 16)) * 16
    packed = [_pack([src[nme] for nme in rest], LANES, rows_rest) for src in (w, g_rest, m, v)]
    res = _adamw(*packed, name="adamw_small")
    for dst, buf in zip((grads, delta, new_m, new_v), res):
        dst.update(zip(rest, _unpack(buf, [w[nme] for nme in rest])))
    return (loss, grad_x, *[grads[nme] for nme in WEIGHTS], *[delta[nme] for nme in WEIGHTS],
            *[new_m[nme] for nme in WEIGHTS], *[new_v[nme] for nme in WEIGHTS])


def kernel(x, c, ctx, c_ctx, w_mod, b_mod, norm1, norm2, w_in, s5_a_re, s5_a_im, s5_log_dt, s5_b_re, s5_b_im, s5_c_re, s5_c_im, s5_d, w_glu, q_norm, kv_norm, w_uq, w_ukv, w_mla_o, w_out, w_ffn_in, w_ffn_out, norm_f, loss_target, m_c_ctx, m_w_mod, m_b_mod, m_norm1, m_norm2, m_w_in, m_s5_a_re, m_s5_a_im, m_s5_log_dt, m_s5_b_re, m_s5_b_im, m_s5_c_re, m_s5_c_im, m_s5_d, m_w_glu, m_q_norm, m_kv_norm, m_w_uq, m_w_ukv, m_w_mla_o, m_w_out, m_w_ffn_in, m_w_ffn_out, m_norm_f, v_c_ctx, v_w_mod, v_b_mod, v_norm1, v_norm2, v_w_in, v_s5_a_re, v_s5_a_im, v_s5_log_dt, v_s5_b_re, v_s5_b_im, v_s5_c_re, v_s5_c_im, v_s5_d, v_w_glu, v_q_norm, v_kv_norm, v_w_uq, v_w_ukv, v_w_mla_o, v_w_out, v_w_ffn_in, v_w_ffn_out, v_norm_f):
    w = dict(c_ctx=c_ctx, w_mod=w_mod, b_mod=b_mod, norm1=norm1, norm2=norm2, w_in=w_in, s5_a_re=s5_a_re, s5_a_im=s5_a_im,
             s5_log_dt=s5_log_dt, s5_b_re=s5_b_re, s5_b_im=s5_b_im, s5_c_re=s5_c_re, s5_c_im=s5_c_im, s5_d=s5_d, w_glu=w_glu,
             q_norm=q_norm, kv_norm=kv_norm, w_uq=w_uq, w_ukv=w_ukv, w_mla_o=w_mla_o, w_out=w_out, w_ffn_in=w_ffn_in,
             w_ffn_out=w_ffn_out, norm_f=norm_f)
    m = dict(c_ctx=m_c_ctx, w_mod=m_w_mod, b_mod=m_b_mod, norm1=m_norm1, norm2=m_norm2, w_in=m_w_in, s5_a_re=m_s5_a_re,
             s5_a_im=m_s5_a_im, s5_log_dt=m_s5_log_dt, s5_b_re=m_s5_b_re, s5_b_im=m_s5_b_im, s5_c_re=m_s5_c_re,
             s5_c_im=m_s5_c_im, s5_d=m_s5_d, w_glu=m_w_glu, q_norm=m_q_norm, kv_norm=m_kv_norm, w_uq=m_w_uq, w_ukv=m_w_ukv,
             w_mla_o=m_w_mla_o, w_out=m_w_out, w_ffn_in=m_w_ffn_in, w_ffn_out=m_w_ffn_out, norm_f=m_norm_f)
    v = dict(c_ctx=v_c_ctx, w_mod=v_w_mod, b_mod=v_b_mod, norm1=v_norm1, norm2=v_norm2, w_in=v_w_in, s5_a_re=v_s5_a_re,
             s5_a_im=v_s5_a_im, s5_log_dt=v_s5_log_dt, s5_b_re=v_s5_b_re, s5_b_im=v_s5_b_im, s5_c_re=v_s5_c_re,
             s5_c_im=v_s5_c_im, s5_d=v_s5_d, w_glu=v_w_glu, q_norm=v_q_norm, kv_norm=v_kv_norm, w_uq=v_w_uq, w_ukv=v_w_ukv,
             w_mla_o=v_w_mla_o, w_out=v_w_out, w_ffn_in=v_w_ffn_in, w_ffn_out=v_w_ffn_out, norm_f=v_norm_f)
    return _step(x, c, ctx, loss_target, w, m, v)
```

```python
import functools
import math

import jax
import jax.numpy as jnp
from jax import lax
from jax.experimental import pallas as pl
from jax.experimental.pallas import tpu as pltpu

F32 = jnp.float32
BF16 = jnp.bfloat16

EPS = 1e-6
GRID_W = 64
S5_GROUP = 16
S5_STATE = 64
MLA_HEADS = 8
QK_NOPE = 128
QK_ROPE = 64
V_DIM = 128
ROPE_BASE = 10000.0
ATTN_SCALE = (QK_NOPE + QK_ROPE) ** -0.5
ADAM_LR = 0.001
ADAM_B1 = 0.9
ADAM_B2 = 0.999
ADAM_EPS = 1e-08
ADAM_WD = 0.01
ADAM_STEP = 10

SUBLANES = 8
LANES = 128
V7X_VMEM_BYTES = 64 * 1024 * 1024
VMEM_LIMIT = (V7X_VMEM_BYTES * 7) // 8
N_SEG = SUBLANES
S5_BLOCK_GROUPS = 16
MESH = pl.DeviceIdType.MESH


def _pick(n, target, mult):
    best = None
    d = mult
    while d <= min(n, target):
        if n % d == 0:
            best = d
        d += mult
    return n if best is None else best


def _cparams(sem=None):
    return pltpu.CompilerParams(dimension_semantics=sem, vmem_limit_bytes=VMEM_LIMIT)


MM_VMEM_BUDGET = (V7X_VMEM_BYTES * 5) // 8


def _mm(a, b, *, ta=False, tb=False, out_dtype=F32, name, b_shards=1, out_shards=1):
    if ta:
        K, M = a.shape
    else:
        M, K = a.shape
    if tb:
        N, K2 = b.shape[-2], b.shape[-1] * b_shards
    else:
        K2, N = b.shape[-2], b.shape[-1] * b_shards
    assert K == K2, (a.shape, b.shape, ta, tb)
    n_unit = N // max(out_shards, 1 if tb else b_shards)
    k_unit = K // (b_shards if tb else 1)
    tn = _pick(n_unit, 1024, LANES)
    tm = _pick(M, 1024 if tn >= 512 else 2048, LANES if ta else 16)
    sa, sb, so = a.dtype.itemsize, b.dtype.itemsize, jnp.dtype(out_dtype).itemsize
    k_mult = LANES if (not ta or tb) else 16
    tk = k_mult if k_unit % k_mult == 0 else k_unit
    for cand in range(k_mult, k_unit + 1, k_mult):
        if k_unit % cand == 0 and 2 * cand * (tm * sa + tn * sb) + tm * tn * (4 + 2 * so) <= MM_VMEM_BUDGET:
            tk = cand
    nk = K // tk
    dims = (((0 if ta else 1,), (1 if tb else 0,)), ((), ()))

    def body(a_ref, b_ref, o_ref, *scratch):
        part = lax.dot_general(a_ref[...].astype(BF16), b_ref[...].astype(BF16), dims, preferred_element_type=F32)
        if nk == 1:
            o_ref[...] = part.astype(o_ref.dtype)
            return
        acc_ref, = scratch
        k = pl.program_id(2)

        @pl.when(k == 0)
        def _():
            acc_ref[...] = part

        @pl.when(k > 0)
        def _():
            acc_ref[...] += part

        @pl.when(k == nk - 1)
        def _():
            o_ref[...] = acc_ref[...].astype(o_ref.dtype)

    a_spec = pl.BlockSpec((tk, tm), lambda i, j, k: (k, i)) if ta else pl.BlockSpec((tm, tk), lambda i, j, k: (i, k))
    if b_shards == 1:
        b_spec = pl.BlockSpec((tn, tk), lambda i, j, k: (j, k)) if tb else pl.BlockSpec((tk, tn), lambda i, j, k: (k, j))
    elif tb:
        kpb = k_unit // tk
        b_spec = pl.BlockSpec((None, tn, tk), lambda i, j, k: (k // kpb, j, k % kpb))
    else:
        npb = n_unit // tn
        b_spec = pl.BlockSpec((None, tk, tn), lambda i, j, k: (j // npb, k, j % npb))
    if out_shards == 1:
        out_spec = pl.BlockSpec((tm, tn), lambda i, j, k: (i, j))
        out_shape = jax.ShapeDtypeStruct((M, N), out_dtype)
    else:
        opb = n_unit // tn
        out_spec = pl.BlockSpec((None, tm, tn), lambda i, j, k: (j // opb, i, j % opb))
        out_shape = jax.ShapeDtypeStruct((out_shards, M, N // out_shards), out_dtype)
    return pl.pallas_call(
        body, name=name, grid=(M // tm, N // tn, nk),
        in_specs=[a_spec, b_spec], out_specs=out_spec, out_shape=out_shape,
        scratch_shapes=[pltpu.VMEM((tm, tn), F32)] if nk > 1 else [],
        compiler_params=_cparams(("parallel", "parallel", "arbitrary")),
    )(a, b)


def _row_tile(tiled, extra_bytes=0):
    rows = tiled[0].shape[0]
    per_row = sum(a.shape[1] * 4 for a in tiled) + extra_bytes
    target = max(SUBLANES, (6 * 1024 * 1024) // max(per_row, 1))
    return _pick(rows, min(target, 512), 16)


def _rw(f, tiled, bcast, out_dtypes, *, name, anchor=None):
    nt, nb = len(tiled), len(bcast)
    rows = tiled[0].shape[0]
    outs_aval = jax.eval_shape(f, *[jax.ShapeDtypeStruct((16, a.shape[1]), F32) for a in tiled],
                               *[jax.ShapeDtypeStruct(b.shape, F32) for b in bcast])
    widths = [o.shape[1] for o in outs_aval]
    tm = _row_tile(tiled, sum(w * 4 for w in widths))

    extra = [] if anchor is None else [anchor]
    n_in = nt + nb + len(extra)

    def body(*refs):
        tin = [r[...].astype(F32) for r in refs[:nt]]
        bin_ = [r[...].astype(F32) for r in refs[nt:nt + nb]]
        outs = f(*tin, *bin_)
        for o_ref, o in zip(refs[n_in:], outs):
            o_ref[...] = o.astype(o_ref.dtype)

    in_specs = [pl.BlockSpec((tm, a.shape[1]), lambda i: (i, 0)) for a in tiled]
    in_specs += [pl.BlockSpec(b.shape, lambda i: (0, 0)) for b in bcast + extra]
    res = pl.pallas_call(
        body, name=name, grid=(rows // tm,), in_specs=in_specs,
        out_specs=[pl.BlockSpec((tm, w), lambda i: (i, 0)) for w in widths],
        out_shape=[jax.ShapeDtypeStruct((rows, w), dt) for w, dt in zip(widths, out_dtypes)],
        compiler_params=_cparams(("parallel",)),
    )(*tiled, *bcast, *extra)
    return list(res)


def _rw_vjp(f, tiled, bcast, cts, need_t, need_b, t_dtypes, *, name):
    nt, nb = len(tiled), len(bcast)
    rows = tiled[0].shape[0]
    flat_cts = [c for group in cts for c in group]
    t_idx = [i for i in range(nt) if need_t[i]]
    b_idx = [i for i in range(nb) if need_b[i]]
    tm = _row_tile(list(tiled) + flat_cts, sum(tiled[i].shape[1] * 4 for i in t_idx))
    nc = len(flat_cts)

    def body(*refs):
        i = pl.program_id(0)
        tin = [r[...].astype(F32) for r in refs[:nt]]
        bin_ = [r[...].astype(F32) for r in refs[nt:nt + nb]]
        ct_refs = refs[nt + nb:nt + nb + nc]
        out_refs = refs[nt + nb + nc:]
        outs, vjp_fn = jax.vjp(f, *tin, *bin_)
        ct_vals, pos = [], 0
        for o, group in zip(outs, cts):
            acc = jnp.zeros_like(o)
            for _ in group:
                acc = acc + ct_refs[pos][...].astype(F32)
                pos += 1
            ct_vals.append(acc)
        grads = vjp_fn(tuple(ct_vals))
        for o_ref, k in zip(out_refs[:len(t_idx)], t_idx):
            o_ref[...] = grads[k].astype(o_ref.dtype)
        for o_ref, k in zip(out_refs[len(t_idx):], b_idx):
            @pl.when(i == 0)
            def _(o_ref=o_ref):
                o_ref[...] = jnp.zeros_like(o_ref)

            o_ref[...] += grads[nt + k]

    in_specs = [pl.BlockSpec((tm, a.shape[1]), lambda i: (i, 0)) for a in tiled]
    in_specs += [pl.BlockSpec(b.shape, lambda i: (0, 0)) for b in bcast]
    in_specs += [pl.BlockSpec((tm, c.shape[1]), lambda i: (i, 0)) for c in flat_cts]
    out_specs = [pl.BlockSpec((tm, tiled[k].shape[1]), lambda i: (i, 0)) for k in t_idx]
    out_specs += [pl.BlockSpec(bcast[k].shape, lambda i: (0, 0)) for k in b_idx]
    out_shape = [jax.ShapeDtypeStruct(tiled[k].shape, dt) for k, dt in zip(t_idx, t_dtypes)]
    out_shape += [jax.ShapeDtypeStruct(bcast[k].shape, F32) for k in b_idx]
    res = pl.pallas_call(
        body, name=name, grid=(rows // tm,), in_specs=in_specs, out_specs=out_specs, out_shape=out_shape,
        compiler_params=_cparams(("arbitrary",)),
    )(*tiled, *bcast, *flat_cts)
    res = list(res)
    return res[:len(t_idx)], res[len(t_idx):]


def _rms(x, g):
    return x * lax.rsqrt(jnp.mean(x * x, axis=-1, keepdims=True) + EPS) * g


def _f_norm_mod(x, g, sc, sh):
    return (_rms(x, g) * (1.0 + sc) + sh,)


def _f_norm_mod_keep(x, g, sc, sh):
    return (_rms(x, g) * (1.0 + sc) + sh, x)


@jax.custom_vjp
def _swap16(x):
    w = x.shape[-1]
    lane = lax.broadcasted_iota(jnp.int32, x.shape, x.ndim - 1)
    return jnp.where((lane & 16) == 0, pltpu.roll(x, w - 16, x.ndim - 1), pltpu.roll(x, 16, x.ndim - 1))


_swap16.defvjp(lambda x: (_swap16(x), None), lambda _, g: (_swap16(g),))


def _rope(x, cos, sin):
    return x * cos + _swap16(x) * sin


def _make_f_post_in(sw, q_rank, kv_rank, with_q):
    o1, o2, o3 = sw, sw + q_rank, sw + q_rank + kv_rank

    if with_q:
        def f(ha, cos, sin, qg, kvg):
            u = ha[:, :o1]
            cqn = _rms(ha[:, o1:o2], qg)
            ckvn = _rms(ha[:, o2:o3], kvg)
            kr = _rope(ha[:, o3:o3 + LANES], cos, sin)
            return u, cqn, ckvn, kr
    else:
        def f(ha, kvg):
            return ha[:, :o1], _rms(ha[:, o2:o3], kvg), ha[:, o3:o3 + LANES]
    return f


def _f_qpost(q2, cos, sin):
    w = q2.shape[1] // 2
    reps = w // LANES
    qr = _rope(q2[:, w:], jnp.tile(cos, (1, reps)), jnp.tile(sin, (1, reps)))
    return (jnp.concatenate([q2[:, :w], qr], axis=1),)


def _f_s5post(u, r, d):
    return (jax.nn.gelu(d * u + r, approximate=True),)


def _f_merge(ab, bm, gt):
    d = bm.shape[1]
    br_s5 = ab[:, :d] * jax.nn.sigmoid(ab[:, d:])
    g = jax.nn.sigmoid(gt)
    return (g[:, :d] * br_s5 + g[:, d:] * bm,)


def _f_resid_norm(x, out, g1, n2, sc2, sh2):
    x1 = x + g1 * out
    return x1, _rms(x1, n2) * (1.0 + sc2) + sh2


def _f_swiglu(ab):
    d = ab.shape[1] // 2
    return (jax.nn.silu(ab[:, :d]) * ab[:, d:],)


def _f_final(x1, f, tgt, g2, nf):
    y = _rms(x1 + g2 * f, nf)
    return (0.5 * jnp.mean(jnp.square(y - tgt), axis=-1, keepdims=True),)


def _bd_fanout(x, ws, *, name):
    nw = len(ws)
    nb, kb, nn = ws[0].shape
    T = x.shape[0]
    tm = _pick(T, 512, 16)

    def body(*refs):
        xb = refs[0][...].astype(BF16)
        for w_ref, o_ref in zip(refs[1:1 + nw], refs[1 + nw:]):
            o_ref[...] = jnp.dot(xb, w_ref[0].astype(BF16), preferred_element_type=F32)

    return list(pl.pallas_call(
        body, name=name, grid=(nb, T // tm),
        in_specs=[pl.BlockSpec((tm, kb), lambda j, i: (i, j))] + [pl.BlockSpec((1, kb, nn), lambda j, i: (j, 0, 0))] * nw,
        out_specs=[pl.BlockSpec((tm, nn), lambda j, i: (i, j))] * nw,
        out_shape=[jax.ShapeDtypeStruct((T, nb * nn), F32)] * nw,
        compiler_params=_cparams(("parallel", "parallel")),
    )(x, *ws))


def _bd_fanin(xs, ws, *, name):
    nw = len(ws)
    nb, kb, nn = ws[0].shape
    T = xs[0].shape[0]
    tm = _pick(T, 512, 16)

    def body(*refs):
        acc = None
        for x_ref, w_ref in zip(refs[:nw], refs[nw:2 * nw]):
            t = jnp.dot(x_ref[...].astype(BF16), w_ref[0].astype(BF16), preferred_element_type=F32)
            acc = t if acc is None else acc + t
        refs[2 * nw][...] = acc

    return pl.pallas_call(
        body, name=name, grid=(nb, T // tm),
        in_specs=[pl.BlockSpec((tm, kb), lambda j, i: (i, j))] * nw + [pl.BlockSpec((1, kb, nn), lambda j, i: (j, 0, 0))] * nw,
        out_specs=pl.BlockSpec((tm, nn), lambda j, i: (i, j)),
        out_shape=jax.ShapeDtypeStruct((T, nb * nn), F32),
        compiler_params=_cparams(("parallel", "parallel")),
    )(*xs, *ws)


def _bd_dw(xs, dys, nb, *, name):
    npair = len(xs)
    T = xs[0].shape[0]
    kb = xs[0].shape[1] // nb
    nn = dys[0].shape[1] // nb
    tm = _pick(T, 512, 16)
    dims = (((0,), (0,)), ((), ()))

    def body(*refs):
        i = pl.program_id(1)
        for x_ref, d_ref, o_ref in zip(refs[:npair], refs[npair:2 * npair], refs[2 * npair:]):
            @pl.when(i == 0)
            def _(o_ref=o_ref):
                o_ref[...] = jnp.zeros_like(o_ref)

            o_ref[0] += lax.dot_general(x_ref[...].astype(BF16), d_ref[...].astype(BF16), dims,
                                        preferred_element_type=F32)

    return list(pl.pallas_call(
        body, name=name, grid=(nb, T // tm),
        in_specs=[pl.BlockSpec((tm, kb), lambda j, i: (i, j))] * npair + [pl.BlockSpec((tm, nn), lambda j, i: (i, j))] * npair,
        out_specs=[pl.BlockSpec((1, kb, nn), lambda j, i: (j, 0, 0))] * npair,
        out_shape=[jax.ShapeDtypeStruct((nb, kb, nn), F32)] * npair,
        compiler_params=_cparams(("parallel", "arbitrary")),
    )(*xs, *dys))


def _cmul(ar, ai, br, bi):
    return ar * br - ai * bi, ar * bi + ai * br


def _cpow(lr, li, n):
    rr, ri = None, None
    br, bi = lr, li
    while n:
        if n & 1:
            rr, ri = (br, bi) if rr is None else _cmul(rr, ri, br, bi)
        n >>= 1
        if n:
            br, bi = _cmul(br, bi, br, bi)
    return rr, ri


def _s5_scan(b_re, b_im, lam_re, lam_im, h0_re, h0_im, e0_re, e0_im, *, reverse, name):
    rows, C = b_re.shape
    n = rows // N_SEG
    cb = _pick(C, 512, LANES)
    seg_order = list(range(N_SEG))[::-1] if reverse else list(range(N_SEG))
    s_first, s_last = seg_order[0], seg_order[-1]

    def body(br_ref, bi_ref, lr_ref, li_ref, h0r_ref, h0i_ref, e0r_ref, e0i_ref, hr_ref, hi_ref, htr_ref, hti_ref):
        shape = (N_SEG, cb)
        lr = jnp.broadcast_to(lr_ref[...], shape)
        li = jnp.broadcast_to(li_ref[...], shape)
        row = lax.broadcasted_iota(jnp.int32, shape, 0)

        def step_of(k):
            return (n - 1 - k) if reverse else k

        def rows_of(k):
            return pl.ds(pl.multiple_of(step_of(k) * N_SEG, N_SEG), N_SEG)

        first = row == s_first
        hr = br_ref[rows_of(0), :] + jnp.where(first, e0r_ref[...], 0.0)
        hi = bi_ref[rows_of(0), :] + jnp.where(first, e0i_ref[...], 0.0)
        hr_ref[rows_of(0), :] = hr
        hi_ref[rows_of(0), :] = hi

        def pass1(k, carry):
            hr, hi = carry
            pr, pi = _cmul(lr, li, hr, hi)
            hr = pr + br_ref[rows_of(k), :]
            hi = pi + bi_ref[rows_of(k), :]
            hr_ref[rows_of(k), :] = hr
            hi_ref[rows_of(k), :] = hi
            return hr, hi

        er, ei = lax.fori_loop(1, n, pass1, (hr, hi))

        lnr, lni = _cpow(lr_ref[...], li_ref[...], n)
        cr, ci = h0r_ref[...], h0i_ref[...]
        cin_r = jnp.zeros(shape, F32)
        cin_i = jnp.zeros(shape, F32)
        for s in seg_order:
            cin_r = jnp.where(row == s, cr, cin_r)
            cin_i = jnp.where(row == s, ci, cin_i)
            if s != s_last:
                pr, pi = _cmul(lnr, lni, cr, ci)
                cr = pr + jnp.sum(jnp.where(row == s, er, 0.0), axis=0, keepdims=True)
                ci = pi + jnp.sum(jnp.where(row == s, ei, 0.0), axis=0, keepdims=True)

        def pass2(k, carry):
            pr, pi = carry
            ar, ai = _cmul(pr, pi, cin_r, cin_i)
            hr = hr_ref[rows_of(k), :] + ar
            hi = hi_ref[rows_of(k), :] + ai
            hr_ref[rows_of(k), :] = hr
            hi_ref[rows_of(k), :] = hi
            npr, npi = _cmul(pr, pi, lr, li)
            return npr, npi

        lax.fori_loop(0, n, pass2, (lr, li))
        last_r = hr_ref[rows_of(n - 1), :]
        last_i = hi_ref[rows_of(n - 1), :]
        htr_ref[...] = jnp.sum(jnp.where(row == s_last, last_r, 0.0), axis=0, keepdims=True)
        hti_ref[...] = jnp.sum(jnp.where(row == s_last, last_i, 0.0), axis=0, keepdims=True)

    big = pl.BlockSpec((rows, cb), lambda j: (0, j))
    vec = pl.BlockSpec((1, cb), lambda j: (0, j))
    return pl.pallas_call(
        body, name=name, grid=(C // cb,),
        in_specs=[big, big] + [vec] * 6,
        out_specs=[big, big, vec, vec],
        out_shape=[jax.ShapeDtypeStruct((rows, C), F32)] * 2 + [jax.ShapeDtypeStruct((1, C), F32)] * 2,
        compiler_params=_cparams(("parallel",)),
    )(b_re, b_im, lam_re, lam_im, h0_re, h0_im, e0_re, e0_im)


def _s5_dlam(mu_re, mu_im, h_re, h_im, h0_re, h0_im, *, reverse, name):
    rows, C = h_re.shape
    n = rows // N_SEG
    cb = _pick(C, 512, LANES)
    s_first = N_SEG - 1 if reverse else 0

    def body(mr_ref, mi_ref, hr_ref, hi_ref, h0r_ref, h0i_ref, dr_ref, di_ref):
        shape = (N_SEG, cb)
        row = lax.broadcasted_iota(jnp.int32, shape, 0)

        def rows_of(k):
            step = (n - 1 - k) if reverse else k
            return pl.ds(pl.multiple_of(step * N_SEG, N_SEG), N_SEG)

        def term(k, pr, pi):
            mr, mi = mr_ref[rows_of(k), :], mi_ref[rows_of(k), :]
            return mr * pr + mi * pi, mi * pr - mr * pi

        shift = N_SEG - 1 if reverse else 1
        pr = jnp.where(row == s_first, h0r_ref[...], pltpu.roll(hr_ref[rows_of(n - 1), :], shift, 0))
        pi = jnp.where(row == s_first, h0i_ref[...], pltpu.roll(hi_ref[rows_of(n - 1), :], shift, 0))
        acc = term(0, pr, pi)

        def loop(k, acc):
            tr, ti = term(k, hr_ref[rows_of(k - 1), :], hi_ref[rows_of(k - 1), :])
            return acc[0] + tr, acc[1] + ti

        ar, ai = lax.fori_loop(1, n, loop, acc)
        dr_ref[...] = jnp.sum(ar, axis=0, keepdims=True)
        di_ref[...] = jnp.sum(ai, axis=0, keepdims=True)

    big = pl.BlockSpec((rows, cb), lambda j: (0, j))
    vec = pl.BlockSpec((1, cb), lambda j: (0, j))
    return pl.pallas_call(
        body, name=name, grid=(C // cb,),
        in_specs=[big] * 4 + [vec] * 2, out_specs=[vec, vec],
        out_shape=[jax.ShapeDtypeStruct((1, C), F32)] * 2,
        compiler_params=_cparams(("parallel",)),
    )(mu_re, mu_im, h_re, h_im, h0_re, h0_im)


NT_DIMS = (((1,), (1,)), ((), ()))
TN_DIMS = (((0,), (0,)), ((), ()))


def _attn_exp(qn, qr, kn, kr):
    s = (lax.dot_general(qn, kn, NT_DIMS, preferred_element_type=F32)
         + lax.dot_general(qr, kr, NT_DIMS, preferred_element_type=F32))
    e = jnp.exp2((s - jnp.max(s, axis=-1, keepdims=True)) * (ATTN_SCALE * math.log2(math.e)))
    return e, jnp.sum(e, axis=-1, keepdims=True)


def _attn_specs(L, T, tq):
    H = MLA_HEADS
    return [
        pl.BlockSpec((tq, LANES), lambda h, i: (i, h)),
        pl.BlockSpec((tq, LANES), lambda h, i: (i, H + h)),
        pl.BlockSpec((T, LANES), lambda h, i: (0, 2 * h)),
        pl.BlockSpec((T, LANES), lambda h, i: (0, 2 * h + 1)),
        pl.BlockSpec((T, LANES), lambda h, i: (0, 0)),
    ]


def _attn_fwd(qq, kv, kr, *, name):
    L, T = qq.shape[0], kv.shape[0]
    tq = _pick(L, 256, 16)

    def body(qn_ref, qr_ref, kn_ref, v_ref, kr_ref, o_ref):
        e, l = _attn_exp(qn_ref[...], qr_ref[...], kn_ref[...], kr_ref[...])
        o_ref[...] = (jnp.dot(e.astype(BF16), v_ref[...], preferred_element_type=F32) * (1.0 / l)).astype(o_ref.dtype)

    return pl.pallas_call(
        body, name=name, grid=(MLA_HEADS, L // tq), in_specs=_attn_specs(L, T, tq),
        out_specs=pl.BlockSpec((tq, LANES), lambda h, i: (i, h)),
        out_shape=jax.ShapeDtypeStruct((L, MLA_HEADS * V_DIM), BF16),
        compiler_params=_cparams(("parallel", "parallel")),
    )(qq, qq, kv, kv, kr)


def _attn_bwd(qq, kv, kr, do, *, name):
    L, T = qq.shape[0], kv.shape[0]
    H = MLA_HEADS
    tq = _pick(L, 256, 16)
    nq = L // tq

    def body(qn_ref, qr_ref, kn_ref, v_ref, kr_ref, do_ref, dqn_ref, dqr_ref, dkn_ref, dv_ref, dkr_ref, dkn_acc, dv_acc):
        h, i = pl.program_id(0), pl.program_id(1)
        qn, qr, kn, v, krv, dov = qn_ref[...], qr_ref[...], kn_ref[...], v_ref[...], kr_ref[...], do_ref[...]
        e, l = _attn_exp(qn, qr, kn, krv)
        inv = 1.0 / l
        ps = e * (inv * ATTN_SCALE)
        t = lax.dot_general(dov, v, NT_DIMS, preferred_element_type=F32) * ps
        ds = (t - ps * (jnp.sum(t, axis=-1, keepdims=True) * (1.0 / ATTN_SCALE))).astype(BF16)
        dqn_ref[...] = jnp.dot(ds, kn, preferred_element_type=F32)
        dqr_ref[...] = jnp.dot(ds, krv, preferred_element_type=F32)

        @pl.when(i == 0)
        def _():
            dkn_acc[...] = jnp.zeros_like(dkn_acc)
            dv_acc[...] = jnp.zeros_like(dv_acc)

        @pl.when((i == 0) & (h == 0))
        def _():
            dkr_ref[...] = jnp.zeros_like(dkr_ref)

        dv_acc[...] += lax.dot_general(e.astype(BF16), (dov.astype(F32) * inv).astype(BF16), TN_DIMS,
                                       preferred_element_type=F32)
        dkn_acc[...] += lax.dot_general(ds, qn, TN_DIMS, preferred_element_type=F32)
        dkr_ref[...] += lax.dot_general(ds, qr, TN_DIMS, preferred_element_type=F32)

        @pl.when(i == nq - 1)
        def _():
            dkn_ref[...] = dkn_acc[...].astype(dkn_ref.dtype)
            dv_ref[...] = dv_acc[...].astype(dv_ref.dtype)

    in_specs = _attn_specs(L, T, tq) + [pl.BlockSpec((tq, LANES), lambda h, i: (i, h))]
    dqn, dqr, dkn, dv, dkr = pl.pallas_call(
        body, name=name, grid=(H, L // tq), in_specs=in_specs,
        out_specs=[pl.BlockSpec((tq, LANES), lambda h, i: (i, h)), pl.BlockSpec((tq, LANES), lambda h, i: (i, h)),
                   pl.BlockSpec((T, LANES), lambda h, i: (0, h)), pl.BlockSpec((T, LANES), lambda h, i: (0, h)),
                   pl.BlockSpec((T, LANES), lambda h, i: (0, 0))],
        out_shape=[jax.ShapeDtypeStruct((L, H * LANES), F32), jax.ShapeDtypeStruct((L, H * LANES), F32),
                   jax.ShapeDtypeStruct((T, H * LANES), BF16), jax.ShapeDtypeStruct((T, H * LANES), BF16),
                   jax.ShapeDtypeStruct((T, LANES), F32)],
        scratch_shapes=[pltpu.VMEM((T, LANES), F32), pltpu.VMEM((T, LANES), F32)],
        compiler_params=_cparams(("arbitrary", "arbitrary")),
    )(qq, qq, kv, kv, kr, do)
    return dqn, dqr, dkn, dv, dkr


def _adamw(w, g, m, v, *, name, anchor=None):
    c1 = 1.0 - ADAM_B1 ** ADAM_STEP
    c2 = 1.0 - ADAM_B2 ** ADAM_STEP

    def f(w, g, m, v):
        m = ADAM_B1 * m + (1.0 - ADAM_B1) * g
        v = ADAM_B2 * v + (1.0 - ADAM_B2) * jnp.square(g)
        delta = -ADAM_LR * ((m / c1) / (jnp.sqrt(v / c2) + ADAM_EPS) + ADAM_WD * w)
        return g, delta, m, v

    return _rw(f, [w, g, m, v], [], [F32] * 4, name=name, anchor=anchor)


def _slab_rows(rows, cols, n_arrays):
    return _pick(rows, max(16, (8 * 1024 * 1024) // (cols * 4 * n_arrays)), 16)


def _scalars(*vals):
    return jnp.stack([jnp.asarray(v, jnp.int32) for v in vals])


def _into_slot(src, slot, nslots, dtype, *, name):
    R, C = src.shape
    tr = _slab_rows(R, C, 2)

    def body(s_ref, x_ref, o_ref):
        o_ref[...] = x_ref[...].astype(o_ref.dtype)

    return pl.pallas_call(
        body, name=name,
        grid_spec=pltpu.PrefetchScalarGridSpec(
            num_scalar_prefetch=1, grid=(R // tr,),
            in_specs=[pl.BlockSpec((tr, C), lambda i, s: (i, 0))],
            out_specs=pl.BlockSpec((None, tr, C), lambda i, s: (s[0], i, 0))),
        out_shape=jax.ShapeDtypeStruct((nslots, R, C), dtype),
        compiler_params=_cparams(("arbitrary",)),
    )(_scalars(slot), src)


def _pair_sum(g, got, c, *, name):
    _, R, C = g.shape
    hr = R // 2
    tr = _slab_rows(hr, C, 3)
    nblk = hr // tr

    def body(s_ref, g_ref, r_ref, o_ref):
        o_ref[...] = (g_ref[...].astype(F32) + r_ref[...].astype(F32)).astype(o_ref.dtype)

    return pl.pallas_call(
        body, name=name,
        grid_spec=pltpu.PrefetchScalarGridSpec(
            num_scalar_prefetch=1, grid=(4, nblk),
            in_specs=[pl.BlockSpec((None, tr, C), lambda j, i, s: (j, s[0] * nblk + i, 0)),
                      pl.BlockSpec((None, tr, C), lambda j, i, s: (j, i, 0))],
            out_specs=pl.BlockSpec((None, tr, C), lambda j, i, s: (j, i, 0))),
        out_shape=jax.ShapeDtypeStruct((4, hr, C), g.dtype),
        compiler_params=_cparams(("arbitrary", "arbitrary")),
    )(_scalars(c), g, got)


def _chip_sum(p, landed, me_chip, c, *, name):
    _, hr, C = p.shape
    tr = _slab_rows(hr, C, 5)

    def body(s_ref, p_ref, l0_ref, l1_ref, l2_ref, o_ref):
        o_ref[...] = ((p_ref[...].astype(F32) + l0_ref[...].astype(F32)) + l1_ref[...].astype(F32)) + l2_ref[...].astype(F32)

    return pl.pallas_call(
        body, name=name,
        grid_spec=pltpu.PrefetchScalarGridSpec(
            num_scalar_prefetch=1, grid=(hr // tr,),
            in_specs=[pl.BlockSpec((None, tr, C), lambda i, s: (s[0], i, 0))]
            + [pl.BlockSpec((None, tr, C), functools.partial(lambda i, s, k: (k, i, 0), k=k)) for k in range(3)],
            out_specs=pl.BlockSpec((None, tr, C), lambda i, s: (s[1], i, 0))),
        out_shape=jax.ShapeDtypeStruct((2, hr, C), F32),
        compiler_params=_cparams(("arbitrary",)),
    )(_scalars(me_chip, c), p, landed, landed, landed)


def _place():
    return lax.axis_index("x"), lax.axis_index("y"), lax.axis_index("c")


def _other_chips(x, y):
    chips = [(1 - x, y), (x, 1 - y), (1 - x, 1 - y)]
    return chips, [2 * cx + cy for cx, cy in chips]


HBM = pl.BlockSpec(memory_space=pl.ANY)


def _allgather8(v, *, name):
    rows, cols = v.shape

    def body(v_ref, out_ref, send_sems, recv_sems):
        x, y, c = _place()
        me = 4 * x + 2 * y + c
        out_ref[me] = v_ref[...]
        copies = []
        for k in range(1, 8):
            bx, by, bc = (k >> 2) & 1, (k >> 1) & 1, k & 1
            px, py, pc = x ^ bx, y ^ by, c ^ bc
            cp = pltpu.make_async_remote_copy(
                src_ref=v_ref, dst_ref=out_ref.at[me], send_sem=send_sems.at[k - 1], recv_sem=recv_sems.at[k - 1],
                device_id=(px, py, pc), device_id_type=MESH)
            cp.start()
            copies.append((cp, 4 * px + 2 * py + pc))
        for k, (cp, peer) in enumerate(copies):
            pltpu.make_async_remote_copy(
                src_ref=v_ref, dst_ref=out_ref.at[peer], send_sem=send_sems.at[k], recv_sem=recv_sems.at[k],
                device_id=(x, y, c), device_id_type=MESH).wait_recv()
        for cp, _ in copies:
            cp.wait_send()

    return pl.pallas_call(
        body, name=name, out_shape=jax.ShapeDtypeStruct((8, rows, cols), v.dtype),
        in_specs=[pl.BlockSpec(memory_space=pltpu.VMEM)], out_specs=pl.BlockSpec(memory_space=pltpu.VMEM),
        scratch_shapes=[pltpu.SemaphoreType.DMA((7,)), pltpu.SemaphoreType.DMA((7,))],
        compiler_params=pltpu.CompilerParams(vmem_limit_bytes=VMEM_LIMIT),
    )(v)


def _allgather_shards(bufs, *, name):
    n = len(bufs)

    def body(*refs):
        outs = refs[n:2 * n]
        send_sems, recv_sems = refs[2 * n:]
        x, y, c = _place()
        me_chip = 2 * x + y
        sibling = (x, y, 1 - c)
        chips, chip_ids = _other_chips(x, y)

        def remote(k, j, blk, hf, to):
            hr = bufs[k].shape[1] // 2
            piece = outs[k].at[blk, pl.ds(pl.multiple_of(hf * hr, 16), hr), :]
            return pltpu.make_async_remote_copy(
                src_ref=piece, dst_ref=piece, send_sem=send_sems.at[6 * k + j], recv_sem=recv_sems.at[6 * k + j],
                device_id=to, device_id_type=MESH)

        sends = []
        for k in range(n):
            for j, chip in enumerate(chips):
                cp = remote(k, j, me_chip, c, (*chip, c))
                cp.start()
                sends.append(cp)
        for k in range(n):
            for j, chip in enumerate(chips):
                remote(k, j, chip_ids[j], c, (x, y, c)).wait_recv()
                cp = remote(k, 3 + j, chip_ids[j], c, sibling)
                cp.start()
                sends.append(cp)
        for k in range(n):
            for j in range(3):
                remote(k, 3 + j, chip_ids[j], 1 - c, (x, y, c)).wait_recv()
        for cp in sends:
            cp.wait_send()

    return list(pl.pallas_call(
        body, name=name, out_shape=[jax.ShapeDtypeStruct(b.shape, b.dtype) for b in bufs],
        in_specs=[HBM] * n, out_specs=[HBM] * n, input_output_aliases={k: k for k in range(n)},
        scratch_shapes=[pltpu.SemaphoreType.DMA((6 * n,)), pltpu.SemaphoreType.DMA((6 * n,))],
    )(*bufs))


def _pair_exchange(gs, *, name, anchor=None):
    n = len(gs)
    extra = [] if anchor is None else [anchor]
    n_in = n + len(extra)

    def body(*refs):
        ins, outs = refs[:n], refs[n_in:n_in + n]
        send_sems, recv_sems = refs[n_in + n:]
        x, y, c = _place()
        copies = []
        for k in range(n):
            hr = gs[k].shape[1] // 2
            src = ins[k].at[:, pl.ds(pl.multiple_of((1 - c) * hr, 16), hr), :]
            cp = pltpu.make_async_remote_copy(src_ref=src, dst_ref=outs[k], send_sem=send_sems.at[k], recv_sem=recv_sems.at[k],
                                              device_id=(x, y, 1 - c), device_id_type=MESH)
            cp.start()
            copies.append(cp)
        for cp in copies:
            cp.wait()

    return list(pl.pallas_call(
        body, name=name,
        out_shape=[jax.ShapeDtypeStruct((4, g.shape[1] // 2, g.shape[2]), g.dtype) for g in gs],
        in_specs=[HBM] * n_in, out_specs=[HBM] * n,
        scratch_shapes=[pltpu.SemaphoreType.DMA((n,)), pltpu.SemaphoreType.DMA((n,))],
    )(*gs, *extra))


def _pair_gather(bufs, *, name):
    n = len(bufs)

    def body(*refs):
        outs = refs[n:2 * n]
        send_sems, recv_sems = refs[2 * n:]
        x, y, c = _place()

        def remote(k, hf, to):
            return pltpu.make_async_remote_copy(src_ref=outs[k].at[hf], dst_ref=outs[k].at[hf], send_sem=send_sems.at[k],
                                                recv_sem=recv_sems.at[k], device_id=to, device_id_type=MESH)

        copies = [remote(k, c, (x, y, 1 - c)) for k in range(n)]
        for cp in copies:
            cp.start()
        for k, cp in enumerate(copies):
            cp.wait_send()
            remote(k, 1 - c, (x, y, c)).wait_recv()

    return list(pl.pallas_call(
        body, name=name, out_shape=[jax.ShapeDtypeStruct(b.shape, b.dtype) for b in bufs],
        in_specs=[HBM] * n, out_specs=[HBM] * n, input_output_aliases={k: k for k in range(n)},
        scratch_shapes=[pltpu.SemaphoreType.DMA((n,)), pltpu.SemaphoreType.DMA((n,))],
    )(*bufs))


HBM_SPEC = pl.BlockSpec(memory_space=pltpu.HBM)
SEM_SPEC = pl.BlockSpec(memory_space=pltpu.SEMAPHORE)
EFFECT = pltpu.SideEffectType.DATAFLOW_SIDE_EFFECTING
TOKEN = jax.ShapeDtypeStruct((SUBLANES, LANES), F32)


def _in_hbm(a):
    return pltpu.with_memory_space_constraint(a, pltpu.HBM)


def _ici_copies(srcs, dsts, send_sems, recv_sems, send):
    x, y, c = _place()
    me_chip = 2 * x + y
    chips, chip_ids = _other_chips(x, y)
    out = []
    for k, (src, dst) in enumerate(zip(srcs, dsts)):
        for j, chip in enumerate(chips):
            s_ref, d_ref = (src(k, me_chip, chip_ids[j], j), dst(k, me_chip, chip_ids[j], j))
            out.append(pltpu.make_async_remote_copy(
                src_ref=s_ref if send else d_ref, dst_ref=d_ref, send_sem=send_sems.at[3 * k + j],
                recv_sem=recv_sems.at[3 * k + j], device_id=(*chip, c) if send else (x, y, c), device_id_type=MESH))
    return out


def _half_rows(buf, hf):
    hr = buf.shape[1] // 2
    return pl.ds(pl.multiple_of(hf * hr, 16), hr)


def _ag_pieces(refs):
    c = lax.axis_index("c")
    src = [functools.partial(lambda k, me, other, j, r: r.at[me, _half_rows(r, c), :], r=r) for r in refs]
    dst_send = src
    dst_recv = [functools.partial(lambda k, me, other, j, r: r.at[other, _half_rows(r, c), :], r=r) for r in refs]
    return src, dst_send, dst_recv


def _ag_start(bufs, groups, *, name):
    n, ng = len(bufs), len(groups)

    def body(*refs):
        sems = refs[n:n + 2 * ng]
        thru = refs[n + 2 * ng:2 * n + 2 * ng]
        token = refs[-1]
        for g, ks in enumerate(groups):
            src, dst_send, _ = _ag_pieces([thru[k] for k in ks])
            for cp in _ici_copies(src, dst_send, sems[2 * g], sems[2 * g + 1], True):
                cp.start()
        token[...] = jnp.zeros_like(token)

    out_shape = tuple(pltpu.SemaphoreType.DMA((3 * len(ks),)) for ks in groups for _ in range(2))
    out_shape += tuple(pltpu.HBM(b.shape, b.dtype) for b in bufs) + (TOKEN,)
    res = pl.pallas_call(
        body, name=name, out_shape=out_shape, in_specs=(HBM_SPEC,) * n,
        out_specs=(SEM_SPEC,) * (2 * ng) + (HBM_SPEC,) * n + (pl.BlockSpec(memory_space=pltpu.VMEM),),
        input_output_aliases={k: 2 * ng + k for k in range(n)},
        compiler_params=pltpu.CompilerParams(has_side_effects=EFFECT),
    )(*[_in_hbm(b) for b in bufs])
    sems = [(res[2 * g], res[2 * g + 1]) for g in range(ng)]
    return sems, list(res[2 * ng:2 * ng + n]), res[-1]


def _ag_wait(bufs, send_sems, recv_sems, after, *, name):
    n = len(bufs)
    after = list(after)

    def body(*refs):
        ins = refs[:n]
        send, recv = refs[n], refs[n + 1]
        src, dst_send, dst_recv = _ag_pieces(ins)
        for cp in _ici_copies(src, dst_send, send, recv, True):
            cp.wait_send()
        for cp in _ici_copies(src, dst_recv, send, recv, False):
            cp.wait_recv()

    return list(pl.pallas_call(
        body, name=name, out_shape=tuple(pltpu.HBM(b.shape, b.dtype) for b in bufs),
        in_specs=(HBM_SPEC,) * n + (SEM_SPEC, SEM_SPEC) + (pl.BlockSpec(memory_space=pl.ANY),) * len(after),
        out_specs=(HBM_SPEC,) * n, input_output_aliases={k: k for k in range(n)},
        compiler_params=pltpu.CompilerParams(has_side_effects=EFFECT),
    )(*bufs, send_sems, recv_sems, *after))


def _ag_forward(bufs, *, name):
    n = len(bufs)

    def body(*refs):
        outs = refs[n:2 * n]
        send_sems, recv_sems = refs[2 * n:]
        x, y, c = _place()
        _, chip_ids = _other_chips(x, y)

        def remote(k, j, hf, to):
            piece = outs[k].at[chip_ids[j], _half_rows(outs[k], hf), :]
            return pltpu.make_async_remote_copy(src_ref=piece, dst_ref=piece, send_sem=send_sems.at[3 * k + j],
                                                recv_sem=recv_sems.at[3 * k + j], device_id=to, device_id_type=MESH)

        sends = [remote(k, j, c, (x, y, 1 - c)) for k in range(n) for j in range(3)]
        for cp in sends:
            cp.start()
        for k in range(n):
            for j in range(3):
                remote(k, j, 1 - c, (x, y, c)).wait_recv()
        for cp in sends:
            cp.wait_send()

    return list(pl.pallas_call(
        body, name=name, out_shape=[jax.ShapeDtypeStruct(b.shape, b.dtype) for b in bufs],
        in_specs=[HBM] * n, out_specs=[HBM] * n, input_output_aliases={k: k for k in range(n)},
        scratch_shapes=[pltpu.SemaphoreType.DMA((3 * n,)), pltpu.SemaphoreType.DMA((3 * n,))],
    )(*bufs))


def _rs_pieces(p_refs, l_refs):
    src = [functools.partial(lambda k, me, other, j, r: r.at[other], r=r) for r in p_refs]
    dst = [functools.partial(lambda k, me, other, j, r: r.at[j], r=r) for r in l_refs]
    return src, dst


def _rs_start(ps, *, name):
    n = len(ps)
    lands = [lax.empty((3,) + p.shape[1:], p.dtype) for p in ps]

    def body(*refs):
        send, recv = refs[2 * n], refs[2 * n + 1]
        p_thru = refs[2 * n + 2:3 * n + 2]
        l_thru = refs[3 * n + 2:4 * n + 2]
        token = refs[-1]
        src, dst = _rs_pieces(p_thru, l_thru)
        for cp in _ici_copies(src, dst, send, recv, True):
            cp.start()
        token[...] = jnp.zeros_like(token)

    out_shape = (pltpu.SemaphoreType.DMA((3 * n,)), pltpu.SemaphoreType.DMA((3 * n,)))
    out_shape += tuple(pltpu.HBM(a.shape, a.dtype) for a in list(ps) + lands) + (TOKEN,)
    res = pl.pallas_call(
        body, name=name, out_shape=out_shape, in_specs=(HBM_SPEC,) * (2 * n),
        out_specs=(SEM_SPEC, SEM_SPEC) + (HBM_SPEC,) * (2 * n) + (pl.BlockSpec(memory_space=pltpu.VMEM),),
        input_output_aliases={k: 2 + k for k in range(2 * n)},
        compiler_params=pltpu.CompilerParams(has_side_effects=EFFECT),
    )(*[_in_hbm(a) for a in list(ps) + lands])
    return (res[0], res[1]), list(res[2:2 + n]), list(res[2 + n:2 + 2 * n]), res[-1]


def _rs_wait(ps, lands, send_sems, recv_sems, after, *, name):
    n = len(ps)

    def body(*refs):
        p_in, l_in = refs[:n], refs[n:2 * n]
        send, recv = refs[2 * n], refs[2 * n + 1]
        src, dst = _rs_pieces(p_in, l_in)
        for cp in _ici_copies(src, dst, send, recv, True):
            cp.wait_send()
        for cp in _ici_copies(src, dst, send, recv, False):
            cp.wait_recv()

    res = pl.pallas_call(
        body, name=name, out_shape=tuple(pltpu.HBM(a.shape, a.dtype) for a in list(ps) + list(lands)),
        in_specs=(HBM_SPEC,) * (2 * n) + (SEM_SPEC, SEM_SPEC) + (pl.BlockSpec(memory_space=pl.ANY),) * len(after),
        out_specs=(HBM_SPEC,) * (2 * n), input_output_aliases={k: k for k in range(2 * n)},
        compiler_params=pltpu.CompilerParams(has_side_effects=EFFECT),
    )(*ps, *lands, send_sems, recv_sems, *after)
    return list(res[:n]), list(res[n:])


def _rs_begin(gs, tag, anchor=None):
    c = lax.axis_index("c")
    got = _pair_exchange(gs, name=f"rs_pair_exchange_{tag}", anchor=anchor)
    pair = [_pair_sum(g, r, c, name=f"rs_pair_sum_{tag}{k}") for k, (g, r) in enumerate(zip(gs, got))]
    sems, pair, lands, token = _rs_start(pair, name=f"rs_start_{tag}")
    return (sems, pair, lands), token


def _rs_end(handle, after, tag):
    x, y, c = _place()
    (send, recv), pair, lands = handle
    pair, lands = _rs_wait(pair, lands, send, recv, after, name=f"rs_wait_{tag}")
    halves = [_chip_sum(p, l, 2 * x + y, c, name=f"rs_chip_sum_{tag}{k}") for k, (p, l) in enumerate(zip(pair, lands))]
    full = _pair_gather(halves, name=f"rs_pair_gather_{tag}")
    return [f.reshape(2 * f.shape[1], f.shape[2]) for f in full]


def _to_segments(a):
    rows = a.shape[0]
    return a.reshape(N_SEG, rows // N_SEG, -1).transpose(1, 0, 2).reshape(rows, -1)


def _from_segments(a):
    rows = a.shape[0]
    return a.reshape(rows // N_SEG, N_SEG, -1).transpose(1, 0, 2).reshape(rows, -1)


def _rope_tables(L):
    t = jnp.arange(L, dtype=jnp.int32)
    row = (t // GRID_W).astype(F32)
    col = (t % GRID_W).astype(F32)
    n_freq = QK_ROPE // 4
    inv = ROPE_BASE ** (-jnp.arange(n_freq, dtype=F32) / n_freq)
    a0, a1 = row[:, None] * inv, col[:, None] * inv
    z = jnp.zeros((L, LANES - QK_ROPE), F32)
    cos = jnp.concatenate([jnp.cos(a0), jnp.cos(a0), jnp.cos(a1), jnp.cos(a1), z], axis=1)
    sin = jnp.concatenate([-jnp.sin(a0), jnp.sin(a0), -jnp.sin(a1), jnp.sin(a1), z], axis=1)
    return _to_segments(cos), _to_segments(sin)


def _col_blocks(w, nblk):
    r, c = w.shape
    return w.reshape(r, nblk, c // nblk).transpose(1, 0, 2)


def _from_col_blocks(w4):
    nblk, r, c = w4.shape
    return w4.transpose(1, 0, 2).reshape(r, nblk * c)


def _s5_discretize(a_re, a_im, log_dt, b_re, b_im):
    dt = jnp.exp(log_dt)[:, None]
    mag = jnp.exp(a_re * dt)
    ab_re, ab_im = mag * jnp.cos(a_im * dt), mag * jnp.sin(a_im * dt)
    den = a_re * a_re + a_im * a_im
    nr, ni = ab_re - 1.0, ab_im
    co_re = (nr * a_re + ni * a_im) / den
    co_im = (ni * a_re - nr * a_im) / den
    bb_re = co_re[..., None] * b_re - co_im[..., None] * b_im
    bb_im = co_re[..., None] * b_im + co_im[..., None] * b_re
    return ab_re, ab_im, bb_re, bb_im


def _diag_blocks_in(bb, gpb):
    G, N, P = bb.shape
    t = jnp.tile(jnp.swapaxes(bb, 1, 2).reshape(G // gpb, gpb * P, N), (1, 1, gpb))
    row = lax.broadcasted_iota(jnp.int32, t.shape, 1) // P
    col = lax.broadcasted_iota(jnp.int32, t.shape, 2) // N
    return jnp.where(row == col, t, 0.0)


def _diag_blocks_out(cc, gpb):
    G, P, N = cc.shape
    t = jnp.tile(jnp.swapaxes(cc, 1, 2).reshape(G // gpb, gpb * N, P), (1, 1, gpb))
    row = lax.broadcasted_iota(jnp.int32, t.shape, 1) // N
    col = lax.broadcasted_iota(jnp.int32, t.shape, 2) // P
    return jnp.where(row == col, t, 0.0)


def _tr(ws):
    return [jnp.swapaxes(w, 1, 2) for w in ws]


WEIGHTS = ['c_ctx', 'w_mod', 'b_mod', 'norm1', 'norm2', 'w_in', 's5_a_re', 's5_a_im', 's5_log_dt', 's5_b_re', 's5_b_im',
           's5_c_re', 's5_c_im', 's5_d', 'w_glu', 'q_norm', 'kv_norm', 'w_uq', 'w_ukv', 'w_mla_o', 'w_out', 'w_ffn_in',
           'w_ffn_out', 'norm_f']
AG_GROUPS = [['w_in'], ['w_glu', 'w_uq', 'w_ukv', 'w_mla_o', 'w_out'], ['w_ffn_in', 'w_ffn_out']]
SMALL = ['norm1', 'norm2', 's5_a_re', 's5_a_im', 's5_log_dt', 's5_b_re', 's5_b_im', 's5_c_re', 's5_c_im', 's5_d',
         'q_norm', 'kv_norm', 'norm_f']


def _pad_rows(a, rows):
    return jnp.concatenate([a, jnp.zeros((rows - a.shape[0],) + a.shape[1:], a.dtype)], axis=0)


def _pack(vals, width, rows):
    flat = jnp.concatenate([v.reshape(-1).astype(F32) for v in vals])
    flat = jnp.concatenate([flat, jnp.zeros((rows * width - flat.shape[0],), F32)])
    return flat.reshape(rows, width)


def _unpack(buf, like):
    flat = buf.reshape(-1)
    out, pos = [], 0
    for v in like:
        out.append(flat[pos:pos + v.size].reshape(v.shape))
        pos += v.size
    return out


def _step(x, c, ctx, loss_target, w, m, v):
    px, py, pc = _place()
    me = 4 * px + 2 * py + pc
    me_chip = 2 * px + py
    L, D = x.shape[1], x.shape[2]
    Lc = ctx.shape[1]
    T = L + Lc
    SW = D // 2
    G = SW // S5_GROUP
    C = G * S5_STATE
    H = MLA_HEADS
    q_rank = w['q_norm'].shape[1]
    kv_rank = w['kv_norm'].shape[1]
    d_ff = w['w_ffn_out'].shape[1] * 4
    wa_used = SW + q_rank + kv_rank + QK_ROPE
    WA = -(-(SW + q_rank + kv_rank + LANES) // 512) * 512

    c_rows = _pad_rows(c.astype(F32), SUBLANES)
    c_all = _allgather8(c_rows, name="ag_cond")[:, 0, :]
    cond = jnp.concatenate([c_all, w['c_ctx'].reshape(1, D)], axis=0)
    cond = _pad_rows(cond, 16)
    (act,) = _rw(lambda t: (jax.nn.silu(t),), [cond], [], [F32], name="cond_silu")
    w_mod, cs_mod = w['w_mod'][0], w['w_mod'].shape[2]
    mod_part = _mm(act, w_mod, out_dtype=F32, name="mod_fwd")
    mod_all = _allgather8(mod_part, name="ag_mod")
    mod_full = jnp.concatenate([mod_all[0], mod_all[2], mod_all[4], mod_all[6]], axis=1) + w['b_mod']
    m_lat = lax.dynamic_slice_in_dim(mod_full, me, 1, axis=0).reshape(6, D)
    m_ctx = mod_full[8].reshape(6, D)
    sh1, sc1, g1, sh2, sc2, g2 = (m_lat[i:i + 1] for i in range(6))
    csh1, csc1 = m_ctx[0:1], m_ctx[1:2]

    names = [nme for grp in AG_GROUPS for nme in grp]
    bufs = [_into_slot(w[nme][0], me_chip, 4, BF16, name=f"cast_{nme}") for nme in names]
    group_idx, pos = [], 0
    for grp in AG_GROUPS:
        group_idx.append(list(range(pos, pos + len(grp))))
        pos += len(grp)
    ag_sems, bufs, ag_token = _ag_start(bufs, group_idx, name="ag_start")
    gathered = {}

    def arrive(g, after):
        got = _ag_wait([bufs[k] for k in group_idx[g]], *ag_sems[g], after, name=f"ag_wait_{g}")
        gathered.update(zip(AG_GROUPS[g], _ag_forward(got, name=f"ag_forward_{g}")))

    xs = _to_segments(x[0])
    cs = _to_segments(ctx[0])
    tgt = _to_segments(loss_target[0])
    cos, sin = _rope_tables(L)
    n1, n2, nf = w['norm1'], w['norm2'], w['norm_f'].reshape(1, D)
    qg, kvg = w['q_norm'], w['kv_norm']

    (xn_lat,) = _rw(_f_norm_mod, [xs], [n1 + ag_token[0, 0], sc1, sh1], [BF16], name="norm1_lat")
    (xn_ctx,) = _rw(_f_norm_mod, [cs], [n1, csc1, csh1], [BF16], name="norm1_ctx")
    xn = jnp.concatenate([xn_lat, xn_ctx], axis=0)

    gpb = min(S5_BLOCK_GROUPS, G)
    gpo = min(8, G)
    d_skip = w['s5_d'][0].reshape(1, SW)
    disc, vjp_disc, w_b, w_c = [], [], [], []
    for d in range(2):
        prm = (w['s5_a_re'][0, d], w['s5_a_im'][0, d], w['s5_log_dt'][0, d], w['s5_b_re'][0, d], w['s5_b_im'][0, d])

        def prep(a_re, a_im, log_dt, b_re, b_im):
            ab_re, ab_im, bb_re, bb_im = _s5_discretize(a_re, a_im, log_dt, b_re, b_im)
            return ab_re.reshape(1, C), ab_im.reshape(1, C), _diag_blocks_in(bb_re, gpb), _diag_blocks_in(bb_im, gpb)

        out, vj = jax.vjp(prep, *prm)
        disc.append(out)
        vjp_disc.append(vj)
        w_b += [out[2], out[3]]
        w_c += [_diag_blocks_out(w['s5_c_re'][0, d], gpo), -_diag_blocks_out(w['s5_c_im'][0, d], gpo)]
    nb_in = G // gpb
    nb_out = G // gpo

    arrive(0, [xn, tgt] + w_b + w_c)
    w_in = _from_col_blocks(gathered['w_in'])
    w_a = jnp.concatenate([w_in[:, :wa_used], jnp.zeros((D, WA - wa_used), BF16)], axis=1)
    w_g = w_in[:, wa_used:]
    ha = _mm(xn, w_a, out_dtype=F32, name="in_proj")
    ha_lat, ha_ctx = ha[:L], ha[L:]
    gt = _mm(xn_lat, w_g, out_dtype=F32, name="in_gates")
    f_post_lat = _make_f_post_in(SW, q_rank, kv_rank, True)
    f_post_ctx = _make_f_post_in(SW, q_rank, kv_rank, False)
    u_lat, cqn, ckvn_lat, kr_lat = _rw(f_post_lat, [ha_lat, cos, sin], [qg, kvg], [F32, BF16, BF16, BF16], name="post_in_lat")
    u_ctx, ckvn_ctx, kr_ctx = _rw(f_post_ctx, [ha_ctx], [kvg], [F32, BF16, BF16], name="post_in_ctx")

    bu_lat = _bd_fanout(u_lat, w_b, name="s5_bu_lat")
    bu_ctx = _bd_fanout(u_ctx, w_b, name="s5_bu_ctx")
    zero = jnp.zeros((1, C), F32)
    h_lat, h_ctx, hT_ctx = [], [], []
    for d, rev in enumerate((False, True)):
        lr, li = disc[d][0], disc[d][1]
        hcr, hci, tr, ti = _s5_scan(bu_ctx[2 * d], bu_ctx[2 * d + 1], lr, li, zero, zero, zero, zero, reverse=rev,
                                    name=f"s5_scan_ctx_{d}")
        hlr, hli, _, _ = _s5_scan(bu_lat[2 * d], bu_lat[2 * d + 1], lr, li, tr, ti, zero, zero, reverse=rev,
                                  name=f"s5_scan_lat_{d}")
        h_ctx += [hcr, hci]
        h_lat += [hlr, hli]
        hT_ctx += [tr, ti]
    r5 = _bd_fanin(h_lat, w_c, name="s5_readout")
    (z,) = _rw(_f_s5post, [u_lat, r5], [d_skip], [BF16], name="s5_post")

    arrive(1, [z])
    w_glu, w_ukv, w_mla_o = (gathered[nme] for nme in ('w_glu', 'w_ukv', 'w_mla_o'))
    w_out = gathered['w_out'].reshape(D, D)
    uq3 = _from_col_blocks(gathered['w_uq']).reshape(q_rank, H, QK_NOPE + QK_ROPE)
    w_q2 = jnp.concatenate([
        uq3[:, :, :QK_NOPE].reshape(q_rank, H * QK_NOPE),
        jnp.concatenate([uq3[:, :, QK_NOPE:], jnp.zeros((q_rank, H, LANES - QK_ROPE), BF16)], axis=2).reshape(q_rank, H * LANES),
    ], axis=1)
    q2 = _mm(cqn, w_q2, out_dtype=F32, name="q_up")
    (qq,) = _rw(_f_qpost, [q2, cos, sin], [], [BF16], name="q_rope")
    kvn = jnp.concatenate([ckvn_lat, ckvn_ctx], axis=0)
    kr_all = jnp.concatenate([kr_lat, kr_ctx], axis=0)
    kv = _mm(kvn, w_ukv, b_shards=4, out_dtype=BF16, name="kv_up")
    o = _attn_fwd(qq, kv, kr_all, name="attn_fwd")

    ab = _mm(z, w_glu, b_shards=4, out_dtype=F32, name="glu_proj")
    bm = _mm(o, w_mla_o, b_shards=4, out_dtype=F32, name="mla_out")
    (mix,) = _rw(_f_merge, [ab, bm, gt], [], [BF16], name="merge")
    out1 = _mm(mix, w_out, out_dtype=F32, name="out_proj")
    x1, xn2 = _rw(_f_resid_norm, [xs, out1], [g1, n2, sc2, sh2], [F32, BF16], name="resid_norm2")
    arrive(2, [xn2])
    w_ffn_in = gathered['w_ffn_in']
    w_ffn_out = gathered['w_ffn_out'].reshape(d_ff, D)
    ab2 = _mm(xn2, w_ffn_in, b_shards=4, out_dtype=F32, name="ffn_in")
    (hmid,) = _rw(_f_swiglu, [ab2], [], [BF16], name="ffn_act")
    f2 = _mm(hmid, w_ffn_out, out_dtype=F32, name="ffn_out")
    (row_loss,) = _rw(_f_final, [x1, f2, tgt], [g2, nf], [F32], name="final_loss")
    loss = lax.psum(jnp.sum(row_loss), ("x", "y", "c"))

    ones = jnp.ones((L, 1), F32)
    (dx1_a, df2), (dg2, dnf) = _rw_vjp(_f_final, [x1, f2, tgt], [g2, nf], [[ones]], [True, True, False], [True, True],
                                       [F32, BF16], name="final_loss_bwd")
    dhmid = _mm(df2, w_ffn_out, tb=True, out_dtype=F32, name="ffn_out_dx")
    gw_ffn_out = _mm(hmid, df2, ta=True, out_dtype=BF16, name="ffn_out_dw")
    (dab2,), _ = _rw_vjp(_f_swiglu, [ab2], [], [[dhmid]], [True], [], [BF16], name="ffn_act_bwd")
    dxn2 = _mm(dab2, w_ffn_in, tb=True, b_shards=4, out_dtype=F32, name="ffn_in_dx")
    gw_ffn_in = _mm(xn2, dab2, ta=True, out_shards=4, out_dtype=BF16, name="ffn_in_dw")
    rs_ffn, tok = _rs_begin([gw_ffn_out.reshape(4, -1, D), gw_ffn_in], "ffn")
    (dx_a, dout1), (dg1, dn2, dsc2, dsh2) = _rw_vjp(
        _f_resid_norm, [xs, out1], [g1, n2 + tok[0, 0], sc2, sh2], [[dx1_a], [dxn2]], [True, True], [True] * 4, [F32, BF16],
        name="resid_norm2_bwd")
    dmix = _mm(dout1, w_out, tb=True, out_dtype=F32, name="out_proj_dx")
    gw_out = _mm(mix, dout1, ta=True, out_dtype=BF16, name="out_proj_dw")
    (dab, dbm, dgt), _ = _rw_vjp(_f_merge, [ab, bm, gt], [], [[dmix]], [True] * 3, [], [BF16] * 3, name="merge_bwd")
    dz = _mm(dab, w_glu, tb=True, b_shards=4, out_dtype=F32, name="glu_proj_dx")
    gw_glu = _mm(z, dab, ta=True, out_shards=4, out_dtype=BF16, name="glu_proj_dw")
    do = _mm(dbm, w_mla_o, tb=True, b_shards=4, out_dtype=BF16, name="mla_out_dx")
    gw_mla_o = _mm(o, dbm, ta=True, out_shards=4, out_dtype=BF16, name="mla_out_dw")
    dxn_g = _mm(dgt, w_g, tb=True, out_dtype=F32, name="in_gates_dx")
    gw_g = _mm(xn_lat, dgt, ta=True, out_dtype=BF16, name="in_gates_dw")

    (du_a, dr5), (dd_skip,) = _rw_vjp(_f_s5post, [u_lat, r5], [d_skip], [[dz]], [True, True], [True], [F32, F32],
                                      name="s5_post_bwd")
    dh_lat = _bd_fanout(dr5, _tr(w_c), name="s5_readout_dx")
    dw_c = _bd_dw(h_lat, [dr5] * 4, nb_out, name="s5_readout_dw")
    zeros_ctx = jnp.zeros((Lc, C), F32)
    mu_lat, mu_ctx, dlam = [], [], []
    for d, rev in enumerate((False, True)):
        lr, li = disc[d][0], disc[d][1]
        mlr, mli, fr, fi = _s5_scan(dh_lat[2 * d], dh_lat[2 * d + 1], lr, -li, zero, zero, zero, zero, reverse=not rev,
                                    name=f"s5_adj_lat_{d}")
        dh0r, dh0i = _cmul(lr, -li, fr, fi)
        mcr, mci, _, _ = _s5_scan(zeros_ctx, zeros_ctx, lr, -li, zero, zero, dh0r, dh0i, reverse=not rev,
                                  name=f"s5_adj_ctx_{d}")
        dl_lat = _s5_dlam(mlr, mli, h_lat[2 * d], h_lat[2 * d + 1], hT_ctx[2 * d], hT_ctx[2 * d + 1], reverse=rev,
                          name=f"s5_dlam_lat_{d}")
        dl_ctx = _s5_dlam(mcr, mci, h_ctx[2 * d], h_ctx[2 * d + 1], zero, zero, reverse=rev, name=f"s5_dlam_ctx_{d}")
        mu_lat += [mlr, mli]
        mu_ctx += [mcr, mci]
        dlam.append((dl_lat[0] + dl_ctx[0], dl_lat[1] + dl_ctx[1]))
    du_b = _bd_fanin(mu_lat, _tr(w_b), name="s5_bu_lat_dx")
    du_ctx = _bd_fanin(mu_ctx, _tr(w_b), name="s5_bu_ctx_dx")
    dw_b_lat = _bd_dw([u_lat] * 4, mu_lat, nb_in, name="s5_bu_lat_dw")
    dw_b_ctx = _bd_dw([u_ctx] * 4, mu_ctx, nb_in, name="s5_bu_ctx_dw")
    g_s5 = {}
    for d in range(2):
        ct = (dlam[d][0], dlam[d][1], dw_b_lat[2 * d] + dw_b_ctx[2 * d], dw_b_lat[2 * d + 1] + dw_b_ctx[2 * d + 1])
        ga_re, ga_im, gdt, gb_re, gb_im = vjp_disc[d](ct)
        _, vj_c = jax.vjp(lambda cr, ci: (_diag_blocks_out(cr, gpo), -_diag_blocks_out(ci, gpo)),
                          w['s5_c_re'][0, d], w['s5_c_im'][0, d])
        gc_re, gc_im = vj_c((dw_c[2 * d], dw_c[2 * d + 1]))
        for nme, val in (('s5_a_re', ga_re), ('s5_a_im', ga_im), ('s5_log_dt', gdt), ('s5_b_re', gb_re),
                         ('s5_b_im', gb_im), ('s5_c_re', gc_re), ('s5_c_im', gc_im)):
            g_s5.setdefault(nme, []).append(val)
    g_small = {nme: jnp.stack(vals)[None] for nme, vals in g_s5.items()}
    g_small['s5_d'] = dd_skip.reshape(w['s5_d'].shape)

    dqn, dqr, dkn, dv, dkr = _attn_bwd(qq, kv, kr_all, do, name="attn_bwd")
    dqq = jnp.concatenate([dqn, dqr], axis=1)
    (dq2,), _ = _rw_vjp(_f_qpost, [q2, cos, sin], [], [[dqq]], [True, False, False], [], [BF16], name="q_rope_bwd")
    dcqn = _mm(dq2, w_q2, tb=True, out_dtype=F32, name="q_up_dx")
    gw_q2 = _mm(cqn, dq2, ta=True, out_dtype=BF16, name="q_up_dw")
    dkv = jnp.stack([dkn.reshape(T, H, LANES), dv.reshape(T, H, LANES)], axis=2).reshape(T, 2 * H * LANES)
    dckvn = _mm(dkv, w_ukv, tb=True, b_shards=4, out_dtype=F32, name="kv_up_dx")
    gw_ukv = _mm(kvn, dkv, ta=True, out_shards=4, out_dtype=BF16, name="kv_up_dw")
    uq_nope = gw_q2[:, :H * QK_NOPE].reshape(q_rank, H, QK_NOPE)
    uq_rope = gw_q2[:, H * QK_NOPE:].reshape(q_rank, H, LANES)[:, :, :QK_ROPE]
    gw_uq = jnp.concatenate([uq_nope, uq_rope], axis=2).reshape(q_rank, H * (QK_NOPE + QK_ROPE))
    rs_mix, tok = _rs_begin([gw_out.reshape(4, -1, D), gw_glu, gw_mla_o, _col_blocks(gw_uq, 4), gw_ukv], "mix")

    (dha_lat,), (dqg, dkvg_lat) = _rw_vjp(
        f_post_lat, [ha_lat, cos, sin], [qg, kvg + tok[0, 0]], [[du_a, du_b], [dcqn], [dckvn[:L]], [dkr[:L]]],
        [True, False, False], [True, True], [BF16], name="post_in_lat_bwd")
    (dha_ctx,), (dkvg_ctx,) = _rw_vjp(f_post_ctx, [ha_ctx], [kvg], [[du_ctx], [dckvn[L:]], [dkr[L:]]], [True], [True],
                                      [BF16], name="post_in_ctx_bwd")
    dha = jnp.concatenate([dha_lat, dha_ctx], axis=0)
    dxn = _mm(dha, w_a, tb=True, out_dtype=F32, name="in_proj_dx")
    gw_a = _mm(xn, dha, ta=True, out_dtype=BF16, name="in_proj_dw")
    (dx_seg,), (dn1_lat, dsc1, dsh1) = _rw_vjp(
        _f_norm_mod_keep, [xs], [n1, sc1, sh1], [[dxn[:L], dxn_g], [dx_a]], [True], [True] * 3, [F32], name="norm1_lat_bwd")
    _, (dn1_ctx, dcsc1, dcsh1) = _rw_vjp(_f_norm_mod, [cs], [n1, csc1, csh1], [[dxn[L:]]], [False], [True] * 3, [],
                                         name="norm1_ctx_bwd")
    grad_x = _from_segments(dx_seg)[None]
    g_small.update(norm1=dn1_lat + dn1_ctx, norm2=dn2, q_norm=dqg, kv_norm=dkvg_lat + dkvg_ctx, norm_f=dnf.reshape(D))

    zD = jnp.zeros((1, D), F32)
    dm = jnp.concatenate([
        jnp.concatenate([dsh1, dsc1, dg1, dsh2, dsc2, dg2], axis=1),
        jnp.concatenate([dcsh1, dcsc1, zD, zD, zD, zD], axis=1),
    ], axis=0)
    dm_all = _allgather8(_pad_rows(dm, SUBLANES), name="ag_dmod")
    dm_ctx = dm_all[0, 1]
    for k in range(1, 8):
        dm_ctx = dm_ctx + dm_all[k, 1]
    dmod = _pad_rows(jnp.concatenate([dm_all[:, 0, :], dm_ctx[None]], axis=0), 16)
    g_b_mod = jnp.sum(dmod, axis=0, keepdims=True)
    dmod_mine = lax.dynamic_slice_in_dim(dmod, me_chip * cs_mod, cs_mod, axis=1)
    g_w_mod = _mm(act, dmod_mine, ta=True, out_dtype=F32, name="mod_dw")
    dact_part = _mm(dmod_mine, w_mod, tb=True, out_dtype=F32, name="mod_dx")
    dact_all = _allgather8(dact_part, name="ag_dact")
    dact = dact_all[0] + dact_all[2] + dact_all[4] + dact_all[6]
    (dcond_rows,), _ = _rw_vjp(lambda t: (jax.nn.silu(t),), [cond], [], [[dact]], [True], [], [F32], name="cond_silu_bwd")
    g_c_ctx = dcond_rows[8]

    gw_in = jnp.concatenate([gw_a[:, :wa_used], gw_g], axis=1)
    small_vals = [g_small[nme] for nme in SMALL]
    n_small = sum(val.size for val in small_vals)
    small_rows = -(-n_small // (LANES * 4 * 32)) * 32
    rs_in, tok = _rs_begin([_col_blocks(gw_in, 4), _pack(small_vals, LANES, 4 * small_rows).reshape(4, small_rows, LANES)],
                           "in", anchor=g_c_ctx)

    grads, delta, new_m, new_v = {}, {}, {}, {}

    def update(nme, red, anchor=None):
        res = _adamw(w[nme][0], red, m[nme][0], v[nme][0], name=f"adamw_{nme}", anchor=anchor)
        grads[nme], delta[nme], new_m[nme], new_v[nme] = (r.reshape(w[nme].shape) for r in res)
        return res[1]

    after = [update('w_mod', g_w_mod, anchor=tok)]
    for handle, tag, members in ((rs_ffn, "ffn", ['w_ffn_out', 'w_ffn_in']),
                                 (rs_mix, "mix", ['w_out', 'w_glu', 'w_mla_o', 'w_uq', 'w_ukv']),
                                 (rs_in, "in", ['w_in'])):
        reduced = _rs_end(handle, after, tag)
        for nme, red in zip(members, reduced):
            after.append(update(nme, red))
    small_mine = reduced[-1]
    small_buf = _into_slot(small_mine, me_chip, 4, F32, name="small_grads_slot")
    small_all = _allgather_shards([small_buf], name="ag_small_grads")[0].reshape(4 * small_rows, LANES)
    g_small_red = dict(zip(SMALL, _unpack(small_all, [w[nme] for nme in SMALL])))
    rest = SMALL + ['c_ctx', 'b_mod']
    g_rest = dict(g_small_red, c_ctx=g_c_ctx, b_mod=g_b_mod)
    rows_rest = -(-sum(w[nme].size for nme in rest) // (LANES * 16)) * 16
    packed = [_pack([src[nme] for nme in rest], LANES, rows_rest) for src in (w, g_rest, m, v)]
    res = _adamw(*packed, name="adamw_small")
    for dst, buf in zip((grads, delta, new_m, new_v), res):
        dst.update(zip(rest, _unpack(buf, [w[nme] for nme in rest])))
    return (loss, grad_x, *[grads[nme] for nme in WEIGHTS], *[delta[nme] for nme in WEIGHTS],
            *[new_m[nme] for nme in WEIGHTS], *[new_v[nme] for nme in WEIGHTS])


def kernel(x, c, ctx, c_ctx, w_mod, b_mod, norm1, norm2, w_in, s5_a_re, s5_a_im, s5_log_dt, s5_b_re, s5_b_im, s5_c_re, s5_c_im, s5_d, w_glu, q_norm, kv_norm, w_uq, w_ukv, w_mla_o, w_out, w_ffn_in, w_ffn_out, norm_f, loss_target, m_c_ctx, m_w_mod, m_b_mod, m_norm1, m_norm2, m_w_in, m_s5_a_re, m_s5_a_im, m_s5_log_dt, m_s5_b_re, m_s5_b_im, m_s5_c_re, m_s5_c_im, m_s5_d, m_w_glu, m_q_norm, m_kv_norm, m_w_uq, m_w_ukv, m_w_mla_o, m_w_out, m_w_ffn_in, m_w_ffn_out, m_norm_f, v_c_ctx, v_w_mod, v_b_mod, v_norm1, v_norm2, v_w_in, v_s5_a_re, v_s5_a_im, v_s5_log_dt, v_s5_b_re, v_s5_b_im, v_s5_c_re, v_s5_c_im, v_s5_d, v_w_glu, v_q_norm, v_kv_norm, v_w_uq, v_w_ukv, v_w_mla_o, v_w_out, v_w_ffn_in, v_w_ffn_out, v_norm_f):
    w = dict(c_ctx=c_ctx, w_mod=w_mod, b_mod=b_mod, norm1=norm1, norm2=norm2, w_in=w_in, s5_a_re=s5_a_re, s5_a_im=s5_a_im,
             s5_log_dt=s5_log_dt, s5_b_re=s5_b_re, s5_b_im=s5_b_im, s5_c_re=s5_c_re, s5_c_im=s5_c_im, s5_d=s5_d, w_glu=w_glu,
             q_norm=q_norm, kv_norm=kv_norm, w_uq=w_uq, w_ukv=w_ukv, w_mla_o=w_mla_o, w_out=w_out, w_ffn_in=w_ffn_in,
             w_ffn_out=w_ffn_out, norm_f=norm_f)
    m = dict(c_ctx=m_c_ctx, w_mod=m_w_mod, b_mod=m_b_mod, norm1=m_norm1, norm2=m_norm2, w_in=m_w_in, s5_a_re=m_s5_a_re,
             s5_a_im=m_s5_a_im, s5_log_dt=m_s5_log_dt, s5_b_re=m_s5_b_re, s5_b_im=m_s5_b_im, s5_c_re=m_s5_c_re,
             s5_c_im=m_s5_c_im, s5_d=m_s5_d, w_glu=m_w_glu, q_norm=m_q_norm, kv_norm=m_kv_norm, w_uq=m_w_uq, w_ukv=m_w_ukv,
             w_mla_o=m_w_mla_o, w_out=m_w_out, w_ffn_in=m_w_ffn_in, w_ffn_out=m_w_ffn_out, norm_f=m_norm_f)
    v = dict(c_ctx=v_c_ctx, w_mod=v_w_mod, b_mod=v_b_mod, norm1=v_norm1, norm2=v_norm2, w_in=v_w_in, s5_a_re=v_s5_a_re,
             s5_a_im=v_s5_a_im, s5_log_dt=v_s5_log_dt, s5_b_re=v_s5_b_re, s5_b_im=v_s5_b_im, s5_c_re=v_s5_c_re,
             s5_c_im=v_s5_c_im, s5_d=v_s5_d, w_glu=v_w_glu, q_norm=v_q_norm, kv_norm=v_kv_norm, w_uq=v_w_uq, w_ukv=v_w_ukv,
             w_mla_o=v_w_mla_o, w_out=v_w_out, w_ffn_in=v_w_ffn_in, w_ffn_out=v_w_ffn_out, norm_f=v_norm_f)
    return _step(x, c, ctx, loss_target, w, m, v)
```

```python
import functools
import math

import jax
import jax.numpy as jnp
from jax import lax
from jax.experimental import pallas as pl
from jax.experimental.pallas import tpu as pltpu

F32 = jnp.float32
BF16 = jnp.bfloat16

EPS = 1e-6
GRID_W = 64
S5_GROUP = 16
S5_STATE = 64
MLA_HEADS = 8
QK_NOPE = 128
QK_ROPE = 64
V_DIM = 128
ROPE_BASE = 10000.0
ATTN_SCALE = (QK_NOPE + QK_ROPE) ** -0.5
ADAM_LR = 0.001
ADAM_B1 = 0.9
ADAM_B2 = 0.999
ADAM_EPS = 1e-08
ADAM_WD = 0.01
ADAM_STEP = 10

SUBLANES = 8
LANES = 128
V7X_VMEM_BYTES = 64 * 1024 * 1024
VMEM_LIMIT = (V7X_VMEM_BYTES * 7) // 8
N_SEG = 2 * SUBLANES
S5_BLOCK_GROUPS = 16
MESH = pl.DeviceIdType.MESH


def _pick(n, target, mult):
    best = None
    d = mult
    while d <= min(n, target):
        if n % d == 0:
            best = d
        d += mult
    return n if best is None else best


def _cparams(sem=None):
    return pltpu.CompilerParams(dimension_semantics=sem, vmem_limit_bytes=VMEM_LIMIT)


MM_VMEM_BUDGET = (V7X_VMEM_BYTES * 5) // 8


def _mm(a, b, *, ta=False, tb=False, out_dtype=F32, name, b_shards=1, out_shards=1):
    if ta:
        K, M = a.shape
    else:
        M, K = a.shape
    if tb:
        N, K2 = b.shape[-2], b.shape[-1] * b_shards
    else:
        K2, N = b.shape[-2], b.shape[-1] * b_shards
    assert K == K2, (a.shape, b.shape, ta, tb)
    n_unit = N // max(out_shards, 1 if tb else b_shards)
    k_unit = K // (b_shards if tb else 1)
    tn = _pick(n_unit, 1024, LANES)
    tm = _pick(M, 1024 if tn >= 512 else 2048, LANES if ta else 16)
    sa, sb, so = a.dtype.itemsize, b.dtype.itemsize, jnp.dtype(out_dtype).itemsize
    k_mult = LANES if (not ta or tb) else 16
    tk = k_mult if k_unit % k_mult == 0 else k_unit
    for cand in range(k_mult, k_unit + 1, k_mult):
        if k_unit % cand == 0 and 2 * cand * (tm * sa + tn * sb) + tm * tn * (4 + 2 * so) <= MM_VMEM_BUDGET:
            tk = cand
    nk = K // tk
    dims = (((0 if ta else 1,), (1 if tb else 0,)), ((), ()))

    def body(a_ref, b_ref, o_ref, *scratch):
        part = lax.dot_general(a_ref[...].astype(BF16), b_ref[...].astype(BF16), dims, preferred_element_type=F32)
        if nk == 1:
            o_ref[...] = part.astype(o_ref.dtype)
            return
        acc_ref, = scratch
        k = pl.program_id(2)

        @pl.when(k == 0)
        def _():
            acc_ref[...] = part

        @pl.when(k > 0)
        def _():
            acc_ref[...] += part

        @pl.when(k == nk - 1)
        def _():
            o_ref[...] = acc_ref[...].astype(o_ref.dtype)

    a_spec = pl.BlockSpec((tk, tm), lambda i, j, k: (k, i)) if ta else pl.BlockSpec((tm, tk), lambda i, j, k: (i, k))
    if b_shards == 1:
        b_spec = pl.BlockSpec((tn, tk), lambda i, j, k: (j, k)) if tb else pl.BlockSpec((tk, tn), lambda i, j, k: (k, j))
    elif tb:
        kpb = k_unit // tk
        b_spec = pl.BlockSpec((None, tn, tk), lambda i, j, k: (k // kpb, j, k % kpb))
    else:
        npb = n_unit // tn
        b_spec = pl.BlockSpec((None, tk, tn), lambda i, j, k: (j // npb, k, j % npb))
    if out_shards == 1:
        out_spec = pl.BlockSpec((tm, tn), lambda i, j, k: (i, j))
        out_shape = jax.ShapeDtypeStruct((M, N), out_dtype)
    else:
        opb = n_unit // tn
        out_spec = pl.BlockSpec((None, tm, tn), lambda i, j, k: (j // opb, i, j % opb))
        out_shape = jax.ShapeDtypeStruct((out_shards, M, N // out_shards), out_dtype)
    return pl.pallas_call(
        body, name=name, grid=(M // tm, N // tn, nk),
        in_specs=[a_spec, b_spec], out_specs=out_spec, out_shape=out_shape,
        scratch_shapes=[pltpu.VMEM((tm, tn), F32)] if nk > 1 else [],
        compiler_params=_cparams(("parallel", "parallel", "arbitrary")),
    )(a, b)


def _row_tile(tiled, extra_bytes=0):
    rows = tiled[0].shape[0]
    per_row = sum(a.shape[1] * 4 for a in tiled) + extra_bytes
    target = max(SUBLANES, (6 * 1024 * 1024) // max(per_row, 1))
    return _pick(rows, min(target, 512), 16)


def _rw(f, tiled, bcast, out_dtypes, *, name, anchor=None):
    nt, nb = len(tiled), len(bcast)
    rows = tiled[0].shape[0]
    outs_aval = jax.eval_shape(f, *[jax.ShapeDtypeStruct((16, a.shape[1]), F32) for a in tiled],
                               *[jax.ShapeDtypeStruct(b.shape, F32) for b in bcast])
    widths = [o.shape[1] for o in outs_aval]
    tm = _row_tile(tiled, sum(w * 4 for w in widths))

    extra = [] if anchor is None else [anchor]
    n_in = nt + nb + len(extra)

    def body(*refs):
        tin = [r[...].astype(F32) for r in refs[:nt]]
        bin_ = [r[...].astype(F32) for r in refs[nt:nt + nb]]
        outs = f(*tin, *bin_)
        for o_ref, o in zip(refs[n_in:], outs):
            o_ref[...] = o.astype(o_ref.dtype)

    in_specs = [pl.BlockSpec((tm, a.shape[1]), lambda i: (i, 0)) for a in tiled]
    in_specs += [pl.BlockSpec(b.shape, lambda i: (0, 0)) for b in bcast + extra]
    res = pl.pallas_call(
        body, name=name, grid=(rows // tm,), in_specs=in_specs,
        out_specs=[pl.BlockSpec((tm, w), lambda i: (i, 0)) for w in widths],
        out_shape=[jax.ShapeDtypeStruct((rows, w), dt) for w, dt in zip(widths, out_dtypes)],
        compiler_params=_cparams(("parallel",)),
    )(*tiled, *bcast, *extra)
    return list(res)


def _rw_vjp(f, tiled, bcast, cts, need_t, need_b, t_dtypes, *, name):
    nt, nb = len(tiled), len(bcast)
    rows = tiled[0].shape[0]
    flat_cts = [c for group in cts for c in group]
    t_idx = [i for i in range(nt) if need_t[i]]
    b_idx = [i for i in range(nb) if need_b[i]]
    tm = _row_tile(list(tiled) + flat_cts, sum(tiled[i].shape[1] * 4 for i in t_idx))
    nc = len(flat_cts)

    def body(*refs):
        i = pl.program_id(0)
        tin = [r[...].astype(F32) for r in refs[:nt]]
        bin_ = [r[...].astype(F32) for r in refs[nt:nt + nb]]
        ct_refs = refs[nt + nb:nt + nb + nc]
        out_refs = refs[nt + nb + nc:]
        outs, vjp_fn = jax.vjp(f, *tin, *bin_)
        ct_vals, pos = [], 0
        for o, group in zip(outs, cts):
            acc = jnp.zeros_like(o)
            for _ in group:
                acc = acc + ct_refs[pos][...].astype(F32)
                pos += 1
            ct_vals.append(acc)
        grads = vjp_fn(tuple(ct_vals))
        for o_ref, k in zip(out_refs[:len(t_idx)], t_idx):
            o_ref[...] = grads[k].astype(o_ref.dtype)
        for o_ref, k in zip(out_refs[len(t_idx):], b_idx):
            @pl.when(i == 0)
            def _(o_ref=o_ref):
                o_ref[...] = jnp.zeros_like(o_ref)

            o_ref[...] += grads[nt + k]

    in_specs = [pl.BlockSpec((tm, a.shape[1]), lambda i: (i, 0)) for a in tiled]
    in_specs += [pl.BlockSpec(b.shape, lambda i: (0, 0)) for b in bcast]
    in_specs += [pl.BlockSpec((tm, c.shape[1]), lambda i: (i, 0)) for c in flat_cts]
    out_specs = [pl.BlockSpec((tm, tiled[k].shape[1]), lambda i: (i, 0)) for k in t_idx]
    out_specs += [pl.BlockSpec(bcast[k].shape, lambda i: (0, 0)) for k in b_idx]
    out_shape = [jax.ShapeDtypeStruct(tiled[k].shape, dt) for k, dt in zip(t_idx, t_dtypes)]
    out_shape += [jax.ShapeDtypeStruct(bcast[k].shape, F32) for k in b_idx]
    res = pl.pallas_call(
        body, name=name, grid=(rows // tm,), in_specs=in_specs, out_specs=out_specs, out_shape=out_shape,
        compiler_params=_cparams(("arbitrary",)),
    )(*tiled, *bcast, *flat_cts)
    res = list(res)
    return res[:len(t_idx)], res[len(t_idx):]


def _rms(x, g):
    return x * lax.rsqrt(jnp.mean(x * x, axis=-1, keepdims=True) + EPS) * g


def _f_norm_mod(x, g, sc, sh):
    return (_rms(x, g) * (1.0 + sc) + sh,)


def _f_norm_mod_keep(x, g, sc, sh):
    return (_rms(x, g) * (1.0 + sc) + sh, x)


@jax.custom_vjp
def _swap16(x):
    w = x.shape[-1]
    lane = lax.broadcasted_iota(jnp.int32, x.shape, x.ndim - 1)
    return jnp.where((lane & 16) == 0, pltpu.roll(x, w - 16, x.ndim - 1), pltpu.roll(x, 16, x.ndim - 1))


_swap16.defvjp(lambda x: (_swap16(x), None), lambda _, g: (_swap16(g),))


def _rope(x, cos, sin):
    return x * cos + _swap16(x) * sin


def _make_f_post_in(sw, q_rank, kv_rank, with_q):
    o1, o2, o3 = sw, sw + q_rank, sw + q_rank + kv_rank

    if with_q:
        def f(ha, cos, sin, qg, kvg):
            u = ha[:, :o1]
            cqn = _rms(ha[:, o1:o2], qg)
            ckvn = _rms(ha[:, o2:o3], kvg)
            kr = _rope(ha[:, o3:o3 + LANES], cos, sin)
            return u, cqn, ckvn, kr
    else:
        def f(ha, kvg):
            return ha[:, :o1], _rms(ha[:, o2:o3], kvg), ha[:, o3:o3 + LANES]
    return f


def _f_qpost(q2, cos, sin):
    w = q2.shape[1] // 2
    reps = w // LANES
    qr = _rope(q2[:, w:], jnp.tile(cos, (1, reps)), jnp.tile(sin, (1, reps)))
    return (jnp.concatenate([q2[:, :w], qr], axis=1),)


def _f_s5post(u, r, d):
    return (jax.nn.gelu(d * u + r, approximate=True),)


def _f_merge(ab, bm, gt):
    d = bm.shape[1]
    br_s5 = ab[:, :d] * jax.nn.sigmoid(ab[:, d:])
    g = jax.nn.sigmoid(gt)
    return (g[:, :d] * br_s5 + g[:, d:] * bm,)


def _f_resid_norm(x, out, g1, n2, sc2, sh2):
    x1 = x + g1 * out
    return x1, _rms(x1, n2) * (1.0 + sc2) + sh2


def _f_swiglu(ab):
    d = ab.shape[1] // 2
    return (jax.nn.silu(ab[:, :d]) * ab[:, d:],)


def _f_final(x1, f, tgt, g2, nf):
    y = _rms(x1 + g2 * f, nf)
    return (0.5 * jnp.mean(jnp.square(y - tgt), axis=-1, keepdims=True),)


def _bd_fanout(x, ws, *, name):
    nw = len(ws)
    nb, kb, nn = ws[0].shape
    T = x.shape[0]
    tm = _pick(T, 512, 16)

    def body(*refs):
        xb = refs[0][...].astype(BF16)
        for w_ref, o_ref in zip(refs[1:1 + nw], refs[1 + nw:]):
            o_ref[...] = jnp.dot(xb, w_ref[0].astype(BF16), preferred_element_type=F32).astype(o_ref.dtype)

    return list(pl.pallas_call(
        body, name=name, grid=(nb, T // tm),
        in_specs=[pl.BlockSpec((tm, kb), lambda j, i: (i, j))] + [pl.BlockSpec((1, kb, nn), lambda j, i: (j, 0, 0))] * nw,
        out_specs=[pl.BlockSpec((tm, nn), lambda j, i: (i, j))] * nw,
        out_shape=[jax.ShapeDtypeStruct((T, nb * nn), BF16)] * nw,
        compiler_params=_cparams(("parallel", "parallel")),
    )(x, *ws))


def _bd_fanin(xs, ws, *, name):
    nw = len(ws)
    nb, kb, nn = ws[0].shape
    T = xs[0].shape[0]
    tm = _pick(T, 512, 16)

    def body(*refs):
        acc = None
        for x_ref, w_ref in zip(refs[:nw], refs[nw:2 * nw]):
            t = jnp.dot(x_ref[...].astype(BF16), w_ref[0].astype(BF16), preferred_element_type=F32)
            acc = t if acc is None else acc + t
        refs[2 * nw][...] = acc

    return pl.pallas_call(
        body, name=name, grid=(nb, T // tm),
        in_specs=[pl.BlockSpec((tm, kb), lambda j, i: (i, j))] * nw + [pl.BlockSpec((1, kb, nn), lambda j, i: (j, 0, 0))] * nw,
        out_specs=pl.BlockSpec((tm, nn), lambda j, i: (i, j)),
        out_shape=jax.ShapeDtypeStruct((T, nb * nn), F32),
        compiler_params=_cparams(("parallel", "parallel")),
    )(*xs, *ws)


def _bd_dw(xs, dys, nb, *, name):
    npair = len(xs)
    T = xs[0].shape[0]
    kb = xs[0].shape[1] // nb
    nn = dys[0].shape[1] // nb
    tm = _pick(T, 512, 16)
    dims = (((0,), (0,)), ((), ()))

    def body(*refs):
        i = pl.program_id(1)
        for x_ref, d_ref, o_ref in zip(refs[:npair], refs[npair:2 * npair], refs[2 * npair:]):
            @pl.when(i == 0)
            def _(o_ref=o_ref):
                o_ref[...] = jnp.zeros_like(o_ref)

            o_ref[0] += lax.dot_general(x_ref[...].astype(BF16), d_ref[...].astype(BF16), dims,
                                        preferred_element_type=F32)

    return list(pl.pallas_call(
        body, name=name, grid=(nb, T // tm),
        in_specs=[pl.BlockSpec((tm, kb), lambda j, i: (i, j))] * npair + [pl.BlockSpec((tm, nn), lambda j, i: (i, j))] * npair,
        out_specs=[pl.BlockSpec((1, kb, nn), lambda j, i: (j, 0, 0))] * npair,
        out_shape=[jax.ShapeDtypeStruct((nb, kb, nn), F32)] * npair,
        compiler_params=_cparams(("parallel", "arbitrary")),
    )(*xs, *dys))


def _cmul(ar, ai, br, bi):
    return ar * br - ai * bi, ar * bi + ai * br


def _cpow(lr, li, n):
    rr, ri = None, None
    br, bi = lr, li
    while n:
        if n & 1:
            rr, ri = (br, bi) if rr is None else _cmul(rr, ri, br, bi)
        n >>= 1
        if n:
            br, bi = _cmul(br, bi, br, bi)
    return rr, ri


def _s5_scan(b_re, b_im, lam_re, lam_im, h0_re, h0_im, e0_re, e0_im, *, reverse, name):
    rows, C = b_re.shape
    n = rows // N_SEG
    cb = _pick(C, 256, LANES)
    seg_order = list(range(N_SEG))[::-1] if reverse else list(range(N_SEG))
    s_first, s_last = seg_order[0], seg_order[-1]

    def body(br_ref, bi_ref, lr_ref, li_ref, h0r_ref, h0i_ref, e0r_ref, e0i_ref, hr_ref, hi_ref, htr_ref, hti_ref,
             locr_ref, loci_ref):
        shape = (N_SEG, cb)
        lr = jnp.broadcast_to(lr_ref[...], shape)
        li = jnp.broadcast_to(li_ref[...], shape)
        row = lax.broadcasted_iota(jnp.int32, shape, 0)

        def step_of(k):
            return (n - 1 - k) if reverse else k

        def rows_of(k):
            return pl.ds(pl.multiple_of(step_of(k) * N_SEG, N_SEG), N_SEG)

        first = row == s_first
        hr = br_ref[rows_of(0), :].astype(F32) + jnp.where(first, e0r_ref[...], 0.0)
        hi = bi_ref[rows_of(0), :].astype(F32) + jnp.where(first, e0i_ref[...], 0.0)
        locr_ref[rows_of(0), :] = hr
        loci_ref[rows_of(0), :] = hi

        def pass1(k, carry):
            hr, hi = carry
            pr, pi = _cmul(lr, li, hr, hi)
            hr = pr + br_ref[rows_of(k), :].astype(F32)
            hi = pi + bi_ref[rows_of(k), :].astype(F32)
            locr_ref[rows_of(k), :] = hr
            loci_ref[rows_of(k), :] = hi
            return hr, hi

        er, ei = lax.fori_loop(1, n, pass1, (hr, hi))

        lnr, lni = _cpow(lr_ref[...], li_ref[...], n)
        cr, ci = h0r_ref[...], h0i_ref[...]
        cin_r = jnp.zeros(shape, F32)
        cin_i = jnp.zeros(shape, F32)
        for s in seg_order:
            cin_r = jnp.where(row == s, cr, cin_r)
            cin_i = jnp.where(row == s, ci, cin_i)
            if s != s_last:
                pr, pi = _cmul(lnr, lni, cr, ci)
                cr = pr + jnp.sum(jnp.where(row == s, er, 0.0), axis=0, keepdims=True)
                ci = pi + jnp.sum(jnp.where(row == s, ei, 0.0), axis=0, keepdims=True)

        def pass2(k, carry):
            pr, pi, _, _ = carry
            ar, ai = _cmul(pr, pi, cin_r, cin_i)
            hr = locr_ref[rows_of(k), :] + ar
            hi = loci_ref[rows_of(k), :] + ai
            hr_ref[rows_of(k), :] = hr.astype(hr_ref.dtype)
            hi_ref[rows_of(k), :] = hi.astype(hi_ref.dtype)
            npr, npi = _cmul(pr, pi, lr, li)
            return npr, npi, hr, hi

        _, _, last_r, last_i = lax.fori_loop(0, n, pass2, (lr, li, er, ei))
        htr_ref[...] = jnp.sum(jnp.where(row == s_last, last_r, 0.0), axis=0, keepdims=True)
        hti_ref[...] = jnp.sum(jnp.where(row == s_last, last_i, 0.0), axis=0, keepdims=True)

    big = pl.BlockSpec((rows, cb), lambda j: (0, j))
    vec = pl.BlockSpec((1, cb), lambda j: (0, j))
    return pl.pallas_call(
        body, name=name, grid=(C // cb,),
        in_specs=[big, big] + [vec] * 6,
        out_specs=[big, big, vec, vec],
        out_shape=[jax.ShapeDtypeStruct((rows, C), BF16)] * 2 + [jax.ShapeDtypeStruct((1, C), F32)] * 2,
        scratch_shapes=[pltpu.VMEM((rows, cb), F32)] * 2,
        compiler_params=_cparams(("parallel",)),
    )(b_re, b_im, lam_re, lam_im, h0_re, h0_im, e0_re, e0_im)


def _s5_dlam(mu_re, mu_im, h_re, h_im, h0_re, h0_im, *, reverse, name):
    rows, C = h_re.shape
    n = rows // N_SEG
    cb = _pick(C, 256, LANES)
    s_first = N_SEG - 1 if reverse else 0

    def body(mr_ref, mi_ref, hr_ref, hi_ref, h0r_ref, h0i_ref, dr_ref, di_ref):
        shape = (N_SEG, cb)
        row = lax.broadcasted_iota(jnp.int32, shape, 0)

        def rows_of(k):
            step = (n - 1 - k) if reverse else k
            return pl.ds(pl.multiple_of(step * N_SEG, N_SEG), N_SEG)

        def term(k, pr, pi):
            mr, mi = mr_ref[rows_of(k), :].astype(F32), mi_ref[rows_of(k), :].astype(F32)
            return mr * pr + mi * pi, mi * pr - mr * pi

        shift = N_SEG - 1 if reverse else 1
        pr = jnp.where(row == s_first, h0r_ref[...], pltpu.roll(hr_ref[rows_of(n - 1), :].astype(F32), shift, 0))
        pi = jnp.where(row == s_first, h0i_ref[...], pltpu.roll(hi_ref[rows_of(n - 1), :].astype(F32), shift, 0))
        acc = term(0, pr, pi)

        def loop(k, acc):
            tr, ti = term(k, hr_ref[rows_of(k - 1), :].astype(F32), hi_ref[rows_of(k - 1), :].astype(F32))
            return acc[0] + tr, acc[1] + ti

        ar, ai = lax.fori_loop(1, n, loop, acc)
        dr_ref[...] = jnp.sum(ar, axis=0, keepdims=True)
        di_ref[...] = jnp.sum(ai, axis=0, keepdims=True)

    big = pl.BlockSpec((rows, cb), lambda j: (0, j))
    vec = pl.BlockSpec((1, cb), lambda j: (0, j))
    return pl.pallas_call(
        body, name=name, grid=(C // cb,),
        in_specs=[big] * 4 + [vec] * 2, out_specs=[vec, vec],
        out_shape=[jax.ShapeDtypeStruct((1, C), F32)] * 2,
        compiler_params=_cparams(("parallel",)),
    )(mu_re, mu_im, h_re, h_im, h0_re, h0_im)


NT_DIMS = (((1,), (1,)), ((), ()))
TN_DIMS = (((0,), (0,)), ((), ()))


def _attn_exp(qn, qr, kn, kr):
    s = (lax.dot_general(qn, kn, NT_DIMS, preferred_element_type=F32)
         + lax.dot_general(qr, kr, NT_DIMS, preferred_element_type=F32))
    e = jnp.exp2((s - jnp.max(s, axis=-1, keepdims=True)) * (ATTN_SCALE * math.log2(math.e)))
    return e, jnp.sum(e, axis=-1, keepdims=True)


def _attn_specs(L, T, tq):
    H = MLA_HEADS
    return [
        pl.BlockSpec((tq, LANES), lambda h, i: (i, h)),
        pl.BlockSpec((tq, LANES), lambda h, i: (i, H + h)),
        pl.BlockSpec((T, LANES), lambda h, i: (0, 2 * h)),
        pl.BlockSpec((T, LANES), lambda h, i: (0, 2 * h + 1)),
        pl.BlockSpec((T, LANES), lambda h, i: (0, 0)),
    ]


def _attn_fwd(qq, kv, kr, *, name):
    L, T = qq.shape[0], kv.shape[0]
    tq = _pick(L, 256, 16)

    def body(qn_ref, qr_ref, kn_ref, v_ref, kr_ref, o_ref):
        e, l = _attn_exp(qn_ref[...], qr_ref[...], kn_ref[...], kr_ref[...])
        o_ref[...] = (jnp.dot(e.astype(BF16), v_ref[...], preferred_element_type=F32) * (1.0 / l)).astype(o_ref.dtype)

    return pl.pallas_call(
        body, name=name, grid=(MLA_HEADS, L // tq), in_specs=_attn_specs(L, T, tq),
        out_specs=pl.BlockSpec((tq, LANES), lambda h, i: (i, h)),
        out_shape=jax.ShapeDtypeStruct((L, MLA_HEADS * V_DIM), BF16),
        compiler_params=_cparams(("parallel", "parallel")),
    )(qq, qq, kv, kv, kr)


def _attn_bwd(qq, kv, kr, do, *, name):
    L, T = qq.shape[0], kv.shape[0]
    H = MLA_HEADS
    tq = _pick(L, 256, 16)
    nq = L // tq

    def body(qn_ref, qr_ref, kn_ref, v_ref, kr_ref, do_ref, dqn_ref, dqr_ref, dkn_ref, dv_ref, dkr_ref, dkn_acc, dv_acc):
        h, i = pl.program_id(0), pl.program_id(1)
        qn, qr, kn, v, krv, dov = qn_ref[...], qr_ref[...], kn_ref[...], v_ref[...], kr_ref[...], do_ref[...]
        e, l = _attn_exp(qn, qr, kn, krv)
        inv = 1.0 / l
        ps = e * (inv * ATTN_SCALE)
        t = lax.dot_general(dov, v, NT_DIMS, preferred_element_type=F32) * ps
        ds = (t - ps * (jnp.sum(t, axis=-1, keepdims=True) * (1.0 / ATTN_SCALE))).astype(BF16)
        dqn_ref[...] = jnp.dot(ds, kn, preferred_element_type=F32)
        dqr_ref[...] = jnp.dot(ds, krv, preferred_element_type=F32)

        @pl.when(i == 0)
        def _():
            dkn_acc[...] = jnp.zeros_like(dkn_acc)
            dv_acc[...] = jnp.zeros_like(dv_acc)

        @pl.when((i == 0) & (h == 0))
        def _():
            dkr_ref[...] = jnp.zeros_like(dkr_ref)

        dv_acc[...] += lax.dot_general(e.astype(BF16), (dov.astype(F32) * inv).astype(BF16), TN_DIMS,
                                       preferred_element_type=F32)
        dkn_acc[...] += lax.dot_general(ds, qn, TN_DIMS, preferred_element_type=F32)
        dkr_ref[...] += lax.dot_general(ds, qr, TN_DIMS, preferred_element_type=F32)

        @pl.when(i == nq - 1)
        def _():
            dkn_ref[...] = dkn_acc[...].astype(dkn_ref.dtype)
            dv_ref[...] = dv_acc[...].astype(dv_ref.dtype)

    in_specs = _attn_specs(L, T, tq) + [pl.BlockSpec((tq, LANES), lambda h, i: (i, h))]
    dqn, dqr, dkn, dv, dkr = pl.pallas_call(
        body, name=name, grid=(H, L // tq), in_specs=in_specs,
        out_specs=[pl.BlockSpec((tq, LANES), lambda h, i: (i, h)), pl.BlockSpec((tq, LANES), lambda h, i: (i, h)),
                   pl.BlockSpec((T, LANES), lambda h, i: (0, h)), pl.BlockSpec((T, LANES), lambda h, i: (0, h)),
                   pl.BlockSpec((T, LANES), lambda h, i: (0, 0))],
        out_shape=[jax.ShapeDtypeStruct((L, H * LANES), F32), jax.ShapeDtypeStruct((L, H * LANES), F32),
                   jax.ShapeDtypeStruct((T, H * LANES), BF16), jax.ShapeDtypeStruct((T, H * LANES), BF16),
                   jax.ShapeDtypeStruct((T, LANES), F32)],
        scratch_shapes=[pltpu.VMEM((T, LANES), F32), pltpu.VMEM((T, LANES), F32)],
        compiler_params=_cparams(("arbitrary", "arbitrary")),
    )(qq, qq, kv, kv, kr, do)
    return dqn, dqr, dkn, dv, dkr


def _adamw(w, g, m, v, *, name, anchor=None):
    c1 = 1.0 - ADAM_B1 ** ADAM_STEP
    c2 = 1.0 - ADAM_B2 ** ADAM_STEP

    def f(w, g, m, v):
        m = ADAM_B1 * m + (1.0 - ADAM_B1) * g
        v = ADAM_B2 * v + (1.0 - ADAM_B2) * jnp.square(g)
        delta = -ADAM_LR * ((m / c1) / (jnp.sqrt(v / c2) + ADAM_EPS) + ADAM_WD * w)
        return g, delta, m, v

    return _rw(f, [w, g, m, v], [], [F32] * 4, name=name, anchor=anchor)


def _slab_rows(rows, cols, n_arrays):
    return _pick(rows, max(16, (8 * 1024 * 1024) // (cols * 4 * n_arrays)), 16)


def _scalars(*vals):
    return jnp.stack([jnp.asarray(v, jnp.int32) for v in vals])


def _into_slot(src, slot, nslots, dtype, *, name):
    R, C = src.shape
    tr = _slab_rows(R, C, 2)

    def body(s_ref, x_ref, o_ref):
        o_ref[...] = x_ref[...].astype(o_ref.dtype)

    return pl.pallas_call(
        body, name=name,
        grid_spec=pltpu.PrefetchScalarGridSpec(
            num_scalar_prefetch=1, grid=(R // tr,),
            in_specs=[pl.BlockSpec((tr, C), lambda i, s: (i, 0))],
            out_specs=pl.BlockSpec((None, tr, C), lambda i, s: (s[0], i, 0))),
        out_shape=jax.ShapeDtypeStruct((nslots, R, C), dtype),
        compiler_params=_cparams(("arbitrary",)),
    )(_scalars(slot), src)


def _pair_sum(g, got, c, *, name):
    _, R, C = g.shape
    hr = R // 2
    tr = _slab_rows(hr, C, 3)
    nblk = hr // tr

    def body(s_ref, g_ref, r_ref, o_ref):
        o_ref[...] = (g_ref[...].astype(F32) + r_ref[...].astype(F32)).astype(o_ref.dtype)

    return pl.pallas_call(
        body, name=name,
        grid_spec=pltpu.PrefetchScalarGridSpec(
            num_scalar_prefetch=1, grid=(4, nblk),
            in_specs=[pl.BlockSpec((None, tr, C), lambda j, i, s: (j, s[0] * nblk + i, 0)),
                      pl.BlockSpec((None, tr, C), lambda j, i, s: (j, i, 0))],
            out_specs=pl.BlockSpec((None, tr, C), lambda j, i, s: (j, i, 0))),
        out_shape=jax.ShapeDtypeStruct((4, hr, C), g.dtype),
        compiler_params=_cparams(("arbitrary", "arbitrary")),
    )(_scalars(c), g, got)


def _chip_sum(p, landed, me_chip, c, *, name):
    _, hr, C = p.shape
    tr = _slab_rows(hr, C, 5)

    def body(s_ref, p_ref, l0_ref, l1_ref, l2_ref, o_ref):
        o_ref[...] = ((p_ref[...].astype(F32) + l0_ref[...].astype(F32)) + l1_ref[...].astype(F32)) + l2_ref[...].astype(F32)

    return pl.pallas_call(
        body, name=name,
        grid_spec=pltpu.PrefetchScalarGridSpec(
            num_scalar_prefetch=1, grid=(hr // tr,),
            in_specs=[pl.BlockSpec((None, tr, C), lambda i, s: (s[0], i, 0))]
            + [pl.BlockSpec((None, tr, C), functools.partial(lambda i, s, k: (k, i, 0), k=k)) for k in range(3)],
            out_specs=pl.BlockSpec((None, tr, C), lambda i, s: (s[1], i, 0))),
        out_shape=jax.ShapeDtypeStruct((2, hr, C), F32),
        compiler_params=_cparams(("arbitrary",)),
    )(_scalars(me_chip, c), p, landed, landed, landed)


def _place():
    return lax.axis_index("x"), lax.axis_index("y"), lax.axis_index("c")


def _other_chips(x, y):
    chips = [(1 - x, y), (x, 1 - y), (1 - x, 1 - y)]
    return chips, [2 * cx + cy for cx, cy in chips]


HBM = pl.BlockSpec(memory_space=pl.ANY)


def _allgather8(v, *, name):
    rows, cols = v.shape

    def body(v_ref, out_ref, send_sems, recv_sems):
        x, y, c = _place()
        me = 4 * x + 2 * y + c
        out_ref[me] = v_ref[...]
        copies = []
        for k in range(1, 8):
            bx, by, bc = (k >> 2) & 1, (k >> 1) & 1, k & 1
            px, py, pc = x ^ bx, y ^ by, c ^ bc
            cp = pltpu.make_async_remote_copy(
                src_ref=v_ref, dst_ref=out_ref.at[me], send_sem=send_sems.at[k - 1], recv_sem=recv_sems.at[k - 1],
                device_id=(px, py, pc), device_id_type=MESH)
            cp.start()
            copies.append((cp, 4 * px + 2 * py + pc))
        for k, (cp, peer) in enumerate(copies):
            pltpu.make_async_remote_copy(
                src_ref=v_ref, dst_ref=out_ref.at[peer], send_sem=send_sems.at[k], recv_sem=recv_sems.at[k],
                device_id=(x, y, c), device_id_type=MESH).wait_recv()
        for cp, _ in copies:
            cp.wait_send()

    return pl.pallas_call(
        body, name=name, out_shape=jax.ShapeDtypeStruct((8, rows, cols), v.dtype),
        in_specs=[pl.BlockSpec(memory_space=pltpu.VMEM)], out_specs=pl.BlockSpec(memory_space=pltpu.VMEM),
        scratch_shapes=[pltpu.SemaphoreType.DMA((7,)), pltpu.SemaphoreType.DMA((7,))],
        compiler_params=pltpu.CompilerParams(vmem_limit_bytes=VMEM_LIMIT),
    )(v)


def _allgather_shards(bufs, *, name):
    n = len(bufs)

    def body(*refs):
        outs = refs[n:2 * n]
        send_sems, recv_sems = refs[2 * n:]
        x, y, c = _place()
        me_chip = 2 * x + y
        sibling = (x, y, 1 - c)
        chips, chip_ids = _other_chips(x, y)

        def remote(k, j, blk, hf, to):
            hr = bufs[k].shape[1] // 2
            piece = outs[k].at[blk, pl.ds(pl.multiple_of(hf * hr, 16), hr), :]
            return pltpu.make_async_remote_copy(
                src_ref=piece, dst_ref=piece, send_sem=send_sems.at[6 * k + j], recv_sem=recv_sems.at[6 * k + j],
                device_id=to, device_id_type=MESH)

        sends = []
        for k in range(n):
            for j, chip in enumerate(chips):
                cp = remote(k, j, me_chip, c, (*chip, c))
                cp.start()
                sends.append(cp)
        for k in range(n):
            for j, chip in enumerate(chips):
                remote(k, j, chip_ids[j], c, (x, y, c)).wait_recv()
                cp = remote(k, 3 + j, chip_ids[j], c, sibling)
                cp.start()
                sends.append(cp)
        for k in range(n):
            for j in range(3):
                remote(k, 3 + j, chip_ids[j], 1 - c, (x, y, c)).wait_recv()
        for cp in sends:
            cp.wait_send()

    return list(pl.pallas_call(
        body, name=name, out_shape=[jax.ShapeDtypeStruct(b.shape, b.dtype) for b in bufs],
        in_specs=[HBM] * n, out_specs=[HBM] * n, input_output_aliases={k: k for k in range(n)},
        scratch_shapes=[pltpu.SemaphoreType.DMA((6 * n,)), pltpu.SemaphoreType.DMA((6 * n,))],
    )(*bufs))


def _pair_exchange(gs, *, name, anchor=None):
    n = len(gs)
    extra = [] if anchor is None else [anchor]
    n_in = n + len(extra)

    def body(*refs):
        ins, outs = refs[:n], refs[n_in:n_in + n]
        send_sems, recv_sems = refs[n_in + n:]
        x, y, c = _place()
        copies = []
        for k in range(n):
            hr = gs[k].shape[1] // 2
            src = ins[k].at[:, pl.ds(pl.multiple_of((1 - c) * hr, 16), hr), :]
            cp = pltpu.make_async_remote_copy(src_ref=src, dst_ref=outs[k], send_sem=send_sems.at[k], recv_sem=recv_sems.at[k],
                                              device_id=(x, y, 1 - c), device_id_type=MESH)
            cp.start()
            copies.append(cp)
        for cp in copies:
            cp.wait()

    return list(pl.pallas_call(
        body, name=name,
        out_shape=[jax.ShapeDtypeStruct((4, g.shape[1] // 2, g.shape[2]), g.dtype) for g in gs],
        in_specs=[HBM] * n_in, out_specs=[HBM] * n,
        scratch_shapes=[pltpu.SemaphoreType.DMA((n,)), pltpu.SemaphoreType.DMA((n,))],
    )(*gs, *extra))


def _pair_gather(bufs, *, name):
    n = len(bufs)

    def body(*refs):
        outs = refs[n:2 * n]
        send_sems, recv_sems = refs[2 * n:]
        x, y, c = _place()

        def remote(k, hf, to):
            return pltpu.make_async_remote_copy(src_ref=outs[k].at[hf], dst_ref=outs[k].at[hf], send_sem=send_sems.at[k],
                                                recv_sem=recv_sems.at[k], device_id=to, device_id_type=MESH)

        copies = [remote(k, c, (x, y, 1 - c)) for k in range(n)]
        for cp in copies:
            cp.start()
        for k, cp in enumerate(copies):
            cp.wait_send()
            remote(k, 1 - c, (x, y, c)).wait_recv()

    return list(pl.pallas_call(
        body, name=name, out_shape=[jax.ShapeDtypeStruct(b.shape, b.dtype) for b in bufs],
        in_specs=[HBM] * n, out_specs=[HBM] * n, input_output_aliases={k: k for k in range(n)},
        scratch_shapes=[pltpu.SemaphoreType.DMA((n,)), pltpu.SemaphoreType.DMA((n,))],
    )(*bufs))


HBM_SPEC = pl.BlockSpec(memory_space=pltpu.HBM)
SEM_SPEC = pl.BlockSpec(memory_space=pltpu.SEMAPHORE)
EFFECT = pltpu.SideEffectType.DATAFLOW_SIDE_EFFECTING
TOKEN = jax.ShapeDtypeStruct((SUBLANES, LANES), F32)


def _in_hbm(a):
    return pltpu.with_memory_space_constraint(a, pltpu.HBM)


def _ici_copies(srcs, dsts, send_sems, recv_sems, send):
    x, y, c = _place()
    me_chip = 2 * x + y
    chips, chip_ids = _other_chips(x, y)
    out = []
    for k, (src, dst) in enumerate(zip(srcs, dsts)):
        for j, chip in enumerate(chips):
            s_ref, d_ref = (src(k, me_chip, chip_ids[j], j), dst(k, me_chip, chip_ids[j], j))
            out.append(pltpu.make_async_remote_copy(
                src_ref=s_ref if send else d_ref, dst_ref=d_ref, send_sem=send_sems.at[3 * k + j],
                recv_sem=recv_sems.at[3 * k + j], device_id=(*chip, c) if send else (x, y, c), device_id_type=MESH))
    return out


def _half_rows(buf, hf):
    hr = buf.shape[1] // 2
    return pl.ds(pl.multiple_of(hf * hr, 16), hr)


def _ag_pieces(refs):
    c = lax.axis_index("c")
    src = [functools.partial(lambda k, me, other, j, r: r.at[me, _half_rows(r, c), :], r=r) for r in refs]
    dst_send = src
    dst_recv = [functools.partial(lambda k, me, other, j, r: r.at[other, _half_rows(r, c), :], r=r) for r in refs]
    return src, dst_send, dst_recv


def _ag_start(bufs, groups, *, name):
    n, ng = len(bufs), len(groups)

    def body(*refs):
        sems = refs[n:n + 2 * ng]
        thru = refs[n + 2 * ng:2 * n + 2 * ng]
        token = refs[-1]
        for g, ks in enumerate(groups):
            src, dst_send, _ = _ag_pieces([thru[k] for k in ks])
            for cp in _ici_copies(src, dst_send, sems[2 * g], sems[2 * g + 1], True):
                cp.start()
        token[...] = jnp.zeros_like(token)

    out_shape = tuple(pltpu.SemaphoreType.DMA((3 * len(ks),)) for ks in groups for _ in range(2))
    out_shape += tuple(pltpu.HBM(b.shape, b.dtype) for b in bufs) + (TOKEN,)
    res = pl.pallas_call(
        body, name=name, out_shape=out_shape, in_specs=(HBM_SPEC,) * n,
        out_specs=(SEM_SPEC,) * (2 * ng) + (HBM_SPEC,) * n + (pl.BlockSpec(memory_space=pltpu.VMEM),),
        input_output_aliases={k: 2 * ng + k for k in range(n)},
        compiler_params=pltpu.CompilerParams(has_side_effects=EFFECT),
    )(*[_in_hbm(b) for b in bufs])
    sems = [(res[2 * g], res[2 * g + 1]) for g in range(ng)]
    return sems, list(res[2 * ng:2 * ng + n]), res[-1]


def _ag_wait(bufs, send_sems, recv_sems, after, *, name):
    n = len(bufs)
    after = list(after)

    def body(*refs):
        ins = refs[:n]
        send, recv = refs[n], refs[n + 1]
        src, dst_send, dst_recv = _ag_pieces(ins)
        for cp in _ici_copies(src, dst_send, send, recv, True):
            cp.wait_send()
        for cp in _ici_copies(src, dst_recv, send, recv, False):
            cp.wait_recv()

    return list(pl.pallas_call(
        body, name=name, out_shape=tuple(pltpu.HBM(b.shape, b.dtype) for b in bufs),
        in_specs=(HBM_SPEC,) * n + (SEM_SPEC, SEM_SPEC) + (pl.BlockSpec(memory_space=pl.ANY),) * len(after),
        out_specs=(HBM_SPEC,) * n, input_output_aliases={k: k for k in range(n)},
        compiler_params=pltpu.CompilerParams(has_side_effects=EFFECT),
    )(*bufs, send_sems, recv_sems, *after))


def _ag_forward(bufs, *, name):
    n = len(bufs)

    def body(*refs):
        outs = refs[n:2 * n]
        send_sems, recv_sems = refs[2 * n:]
        x, y, c = _place()
        _, chip_ids = _other_chips(x, y)

        def remote(k, j, hf, to):
            piece = outs[k].at[chip_ids[j], _half_rows(outs[k], hf), :]
            return pltpu.make_async_remote_copy(src_ref=piece, dst_ref=piece, send_sem=send_sems.at[3 * k + j],
                                                recv_sem=recv_sems.at[3 * k + j], device_id=to, device_id_type=MESH)

        sends = [remote(k, j, c, (x, y, 1 - c)) for k in range(n) for j in range(3)]
        for cp in sends:
            cp.start()
        for k in range(n):
            for j in range(3):
                remote(k, j, 1 - c, (x, y, c)).wait_recv()
        for cp in sends:
            cp.wait_send()

    return list(pl.pallas_call(
        body, name=name, out_shape=[jax.ShapeDtypeStruct(b.shape, b.dtype) for b in bufs],
        in_specs=[HBM] * n, out_specs=[HBM] * n, input_output_aliases={k: k for k in range(n)},
        scratch_shapes=[pltpu.SemaphoreType.DMA((3 * n,)), pltpu.SemaphoreType.DMA((3 * n,))],
    )(*bufs))


def _rs_pieces(p_refs, l_refs):
    src = [functools.partial(lambda k, me, other, j, r: r.at[other], r=r) for r in p_refs]
    dst = [functools.partial(lambda k, me, other, j, r: r.at[j], r=r) for r in l_refs]
    return src, dst


def _rs_start(ps, *, name):
    n = len(ps)
    lands = [lax.empty((3,) + p.shape[1:], p.dtype) for p in ps]

    def body(*refs):
        send, recv = refs[2 * n], refs[2 * n + 1]
        p_thru = refs[2 * n + 2:3 * n + 2]
        l_thru = refs[3 * n + 2:4 * n + 2]
        token = refs[-1]
        src, dst = _rs_pieces(p_thru, l_thru)
        for cp in _ici_copies(src, dst, send, recv, True):
            cp.start()
        token[...] = jnp.zeros_like(token)

    out_shape = (pltpu.SemaphoreType.DMA((3 * n,)), pltpu.SemaphoreType.DMA((3 * n,)))
    out_shape += tuple(pltpu.HBM(a.shape, a.dtype) for a in list(ps) + lands) + (TOKEN,)
    res = pl.pallas_call(
        body, name=name, out_shape=out_shape, in_specs=(HBM_SPEC,) * (2 * n),
        out_specs=(SEM_SPEC, SEM_SPEC) + (HBM_SPEC,) * (2 * n) + (pl.BlockSpec(memory_space=pltpu.VMEM),),
        input_output_aliases={k: 2 + k for k in range(2 * n)},
        compiler_params=pltpu.CompilerParams(has_side_effects=EFFECT),
    )(*[_in_hbm(a) for a in list(ps) + lands])
    return (res[0], res[1]), list(res[2:2 + n]), list(res[2 + n:2 + 2 * n]), res[-1]


def _rs_wait(ps, lands, send_sems, recv_sems, after, *, name):
    n = len(ps)

    def body(*refs):
        p_in, l_in = refs[:n], refs[n:2 * n]
        send, recv = refs[2 * n], refs[2 * n + 1]
        src, dst = _rs_pieces(p_in, l_in)
        for cp in _ici_copies(src, dst, send, recv, True):
            cp.wait_send()
        for cp in _ici_copies(src, dst, send, recv, False):
            cp.wait_recv()

    res = pl.pallas_call(
        body, name=name, out_shape=tuple(pltpu.HBM(a.shape, a.dtype) for a in list(ps) + list(lands)),
        in_specs=(HBM_SPEC,) * (2 * n) + (SEM_SPEC, SEM_SPEC) + (pl.BlockSpec(memory_space=pl.ANY),) * len(after),
        out_specs=(HBM_SPEC,) * (2 * n), input_output_aliases={k: k for k in range(2 * n)},
        compiler_params=pltpu.CompilerParams(has_side_effects=EFFECT),
    )(*ps, *lands, send_sems, recv_sems, *after)
    return list(res[:n]), list(res[n:])


def _rs_begin(gs, tag, anchor=None):
    c = lax.axis_index("c")
    got = _pair_exchange(gs, name=f"rs_pair_exchange_{tag}", anchor=anchor)
    pair = [_pair_sum(g, r, c, name=f"rs_pair_sum_{tag}{k}") for k, (g, r) in enumerate(zip(gs, got))]
    sems, pair, lands, token = _rs_start(pair, name=f"rs_start_{tag}")
    return (sems, pair, lands), token


def _rs_end(handle, after, tag):
    x, y, c = _place()
    (send, recv), pair, lands = handle
    pair, lands = _rs_wait(pair, lands, send, recv, after, name=f"rs_wait_{tag}")
    halves = [_chip_sum(p, l, 2 * x + y, c, name=f"rs_chip_sum_{tag}{k}") for k, (p, l) in enumerate(zip(pair, lands))]
    full = _pair_gather(halves, name=f"rs_pair_gather_{tag}")
    return [f.reshape(2 * f.shape[1], f.shape[2]) for f in full]


def _to_segments(a):
    rows = a.shape[0]
    return a.reshape(N_SEG, rows // N_SEG, -1).transpose(1, 0, 2).reshape(rows, -1)


def _from_segments(a):
    rows = a.shape[0]
    return a.reshape(rows // N_SEG, N_SEG, -1).transpose(1, 0, 2).reshape(rows, -1)


def _rope_tables(L):
    t = jnp.arange(L, dtype=jnp.int32)
    row = (t // GRID_W).astype(F32)
    col = (t % GRID_W).astype(F32)
    n_freq = QK_ROPE // 4
    inv = ROPE_BASE ** (-jnp.arange(n_freq, dtype=F32) / n_freq)
    a0, a1 = row[:, None] * inv, col[:, None] * inv
    z = jnp.zeros((L, LANES - QK_ROPE), F32)
    cos = jnp.concatenate([jnp.cos(a0), jnp.cos(a0), jnp.cos(a1), jnp.cos(a1), z], axis=1)
    sin = jnp.concatenate([-jnp.sin(a0), jnp.sin(a0), -jnp.sin(a1), jnp.sin(a1), z], axis=1)
    return _to_segments(cos), _to_segments(sin)


def _col_blocks(w, nblk):
    r, c = w.shape
    return w.reshape(r, nblk, c // nblk).transpose(1, 0, 2)


def _from_col_blocks(w4):
    nblk, r, c = w4.shape
    return w4.transpose(1, 0, 2).reshape(r, nblk * c)


def _s5_discretize(a_re, a_im, log_dt, b_re, b_im):
    dt = jnp.exp(log_dt)[:, None]
    mag = jnp.exp(a_re * dt)
    ab_re, ab_im = mag * jnp.cos(a_im * dt), mag * jnp.sin(a_im * dt)
    den = a_re * a_re + a_im * a_im
    nr, ni = ab_re - 1.0, ab_im
    co_re = (nr * a_re + ni * a_im) / den
    co_im = (ni * a_re - nr * a_im) / den
    bb_re = co_re[..., None] * b_re - co_im[..., None] * b_im
    bb_im = co_re[..., None] * b_im + co_im[..., None] * b_re
    return ab_re, ab_im, bb_re, bb_im


def _diag_blocks_in(bb, gpb):
    G, N, P = bb.shape
    t = jnp.tile(jnp.swapaxes(bb, 1, 2).reshape(G // gpb, gpb * P, N), (1, 1, gpb))
    row = lax.broadcasted_iota(jnp.int32, t.shape, 1) // P
    col = lax.broadcasted_iota(jnp.int32, t.shape, 2) // N
    return jnp.where(row == col, t, 0.0)


def _diag_blocks_out(cc, gpb):
    G, P, N = cc.shape
    t = jnp.tile(jnp.swapaxes(cc, 1, 2).reshape(G // gpb, gpb * N, P), (1, 1, gpb))
    row = lax.broadcasted_iota(jnp.int32, t.shape, 1) // N
    col = lax.broadcasted_iota(jnp.int32, t.shape, 2) // P
    return jnp.where(row == col, t, 0.0)


def _tr(ws):
    return [jnp.swapaxes(w, 1, 2) for w in ws]


WEIGHTS = ['c_ctx', 'w_mod', 'b_mod', 'norm1', 'norm2', 'w_in', 's5_a_re', 's5_a_im', 's5_log_dt', 's5_b_re', 's5_b_im',
           's5_c_re', 's5_c_im', 's5_d', 'w_glu', 'q_norm', 'kv_norm', 'w_uq', 'w_ukv', 'w_mla_o', 'w_out', 'w_ffn_in',
           'w_ffn_out', 'norm_f']
AG_GROUPS = [['w_in'], ['w_glu', 'w_uq', 'w_ukv', 'w_mla_o', 'w_out'], ['w_ffn_in', 'w_ffn_out']]
SMALL = ['norm1', 'norm2', 's5_a_re', 's5_a_im', 's5_log_dt', 's5_b_re', 's5_b_im', 's5_c_re', 's5_c_im', 's5_d',
         'q_norm', 'kv_norm', 'norm_f']


def _pad_rows(a, rows):
    return jnp.concatenate([a, jnp.zeros((rows - a.shape[0],) + a.shape[1:], a.dtype)], axis=0)


def _pack(vals, width, rows):
    flat = jnp.concatenate([v.reshape(-1).astype(F32) for v in vals])
    flat = jnp.concatenate([flat, jnp.zeros((rows * width - flat.shape[0],), F32)])
    return flat.reshape(rows, width)


def _unpack(buf, like):
    flat = buf.reshape(-1)
    out, pos = [], 0
    for v in like:
        out.append(flat[pos:pos + v.size].reshape(v.shape))
        pos += v.size
    return out


def _step(x, c, ctx, loss_target, w, m, v):
    px, py, pc = _place()
    me = 4 * px + 2 * py + pc
    me_chip = 2 * px + py
    L, D = x.shape[1], x.shape[2]
    Lc = ctx.shape[1]
    T = L + Lc
    SW = D // 2
    G = SW // S5_GROUP
    C = G * S5_STATE
    H = MLA_HEADS
    q_rank = w['q_norm'].shape[1]
    kv_rank = w['kv_norm'].shape[1]
    d_ff = w['w_ffn_out'].shape[1] * 4
    wa_used = SW + q_rank + kv_rank + QK_ROPE
    WA = -(-(SW + q_rank + kv_rank + LANES) // 512) * 512

    c_rows = _pad_rows(c.astype(F32), SUBLANES)
    c_all = _allgather8(c_rows, name="ag_cond")[:, 0, :]
    cond = jnp.concatenate([c_all, w['c_ctx'].reshape(1, D)], axis=0)
    cond = _pad_rows(cond, 16)
    (act,) = _rw(lambda t: (jax.nn.silu(t),), [cond], [], [F32], name="cond_silu")
    w_mod, cs_mod = w['w_mod'][0], w['w_mod'].shape[2]
    mod_part = _mm(act, w_mod, out_dtype=F32, name="mod_fwd")
    mod_all = _allgather8(mod_part, name="ag_mod")
    mod_full = jnp.concatenate([mod_all[0], mod_all[2], mod_all[4], mod_all[6]], axis=1) + w['b_mod']
    m_lat = lax.dynamic_slice_in_dim(mod_full, me, 1, axis=0).reshape(6, D)
    m_ctx = mod_full[8].reshape(6, D)
    sh1, sc1, g1, sh2, sc2, g2 = (m_lat[i:i + 1] for i in range(6))
    csh1, csc1 = m_ctx[0:1], m_ctx[1:2]

    names = [nme for grp in AG_GROUPS for nme in grp]
    bufs = [_into_slot(w[nme][0], me_chip, 4, BF16, name=f"cast_{nme}") for nme in names]
    group_idx, pos = [], 0
    for grp in AG_GROUPS:
        group_idx.append(list(range(pos, pos + len(grp))))
        pos += len(grp)
    ag_sems, bufs, ag_token = _ag_start(bufs, group_idx, name="ag_start")
    gathered = {}

    def arrive(g, after):
        got = _ag_wait([bufs[k] for k in group_idx[g]], *ag_sems[g], after, name=f"ag_wait_{g}")
        gathered.update(zip(AG_GROUPS[g], _ag_forward(got, name=f"ag_forward_{g}")))

    xs = _to_segments(x[0])
    cs = _to_segments(ctx[0])
    tgt = _to_segments(loss_target[0])
    cos, sin = _rope_tables(L)
    n1, n2, nf = w['norm1'], w['norm2'], w['norm_f'].reshape(1, D)
    qg, kvg = w['q_norm'], w['kv_norm']

    (xn_lat,) = _rw(_f_norm_mod, [xs], [n1 + ag_token[0, 0], sc1, sh1], [BF16], name="norm1_lat")
    (xn_ctx,) = _rw(_f_norm_mod, [cs], [n1, csc1, csh1], [BF16], name="norm1_ctx")
    xn = jnp.concatenate([xn_lat, xn_ctx], axis=0)

    gpb = min(S5_BLOCK_GROUPS, G)
    gpo = min(8, G)
    d_skip = w['s5_d'][0].reshape(1, SW)
    disc, vjp_disc, w_b, w_c = [], [], [], []
    for d in range(2):
        prm = (w['s5_a_re'][0, d], w['s5_a_im'][0, d], w['s5_log_dt'][0, d], w['s5_b_re'][0, d], w['s5_b_im'][0, d])

        def prep(a_re, a_im, log_dt, b_re, b_im):
            ab_re, ab_im, bb_re, bb_im = _s5_discretize(a_re, a_im, log_dt, b_re, b_im)
            return ab_re.reshape(1, C), ab_im.reshape(1, C), _diag_blocks_in(bb_re, gpb), _diag_blocks_in(bb_im, gpb)

        out, vj = jax.vjp(prep, *prm)
        disc.append(out)
        vjp_disc.append(vj)
        w_b += [out[2], out[3]]
        w_c += [_diag_blocks_out(w['s5_c_re'][0, d], gpo), -_diag_blocks_out(w['s5_c_im'][0, d], gpo)]
    nb_in = G // gpb
    nb_out = G // gpo

    arrive(0, [xn, tgt] + w_b + w_c)
    w_in = _from_col_blocks(gathered['w_in'])
    w_a = jnp.concatenate([w_in[:, :wa_used], jnp.zeros((D, WA - wa_used), BF16)], axis=1)
    w_g = w_in[:, wa_used:]
    ha = _mm(xn, w_a, out_dtype=F32, name="in_proj")
    ha_lat, ha_ctx = ha[:L], ha[L:]
    gt = _mm(xn_lat, w_g, out_dtype=F32, name="in_gates")
    f_post_lat = _make_f_post_in(SW, q_rank, kv_rank, True)
    f_post_ctx = _make_f_post_in(SW, q_rank, kv_rank, False)
    u_lat, cqn, ckvn_lat, kr_lat = _rw(f_post_lat, [ha_lat, cos, sin], [qg, kvg], [F32, BF16, BF16, BF16], name="post_in_lat")
    u_ctx, ckvn_ctx, kr_ctx = _rw(f_post_ctx, [ha_ctx], [kvg], [F32, BF16, BF16], name="post_in_ctx")

    bu_lat = _bd_fanout(u_lat, w_b, name="s5_bu_lat")
    bu_ctx = _bd_fanout(u_ctx, w_b, name="s5_bu_ctx")
    zero = jnp.zeros((1, C), F32)
    h_lat, h_ctx, hT_ctx = [], [], []
    for d, rev in enumerate((False, True)):
        lr, li = disc[d][0], disc[d][1]
        hcr, hci, tr, ti = _s5_scan(bu_ctx[2 * d], bu_ctx[2 * d + 1], lr, li, zero, zero, zero, zero, reverse=rev,
                                    name=f"s5_scan_ctx_{d}")
        hlr, hli, _, _ = _s5_scan(bu_lat[2 * d], bu_lat[2 * d + 1], lr, li, tr, ti, zero, zero, reverse=rev,
                                  name=f"s5_scan_lat_{d}")
        h_ctx += [hcr, hci]
        h_lat += [hlr, hli]
        hT_ctx += [tr, ti]
    r5 = _bd_fanin(h_lat, w_c, name="s5_readout")
    (z,) = _rw(_f_s5post, [u_lat, r5], [d_skip], [BF16], name="s5_post")

    arrive(1, [z])
    w_glu, w_ukv, w_mla_o = (gathered[nme] for nme in ('w_glu', 'w_ukv', 'w_mla_o'))
    w_out = gathered['w_out'].reshape(D, D)
    uq3 = _from_col_blocks(gathered['w_uq']).reshape(q_rank, H, QK_NOPE + QK_ROPE)
    w_q2 = jnp.concatenate([
        uq3[:, :, :QK_NOPE].reshape(q_rank, H * QK_NOPE),
        jnp.concatenate([uq3[:, :, QK_NOPE:], jnp.zeros((q_rank, H, LANES - QK_ROPE), BF16)], axis=2).reshape(q_rank, H * LANES),
    ], axis=1)
    q2 = _mm(cqn, w_q2, out_dtype=F32, name="q_up")
    (qq,) = _rw(_f_qpost, [q2, cos, sin], [], [BF16], name="q_rope")
    kvn = jnp.concatenate([ckvn_lat, ckvn_ctx], axis=0)
    kr_all = jnp.concatenate([kr_lat, kr_ctx], axis=0)
    kv = _mm(kvn, w_ukv, b_shards=4, out_dtype=BF16, name="kv_up")
    o = _attn_fwd(qq, kv, kr_all, name="attn_fwd")

    ab = _mm(z, w_glu, b_shards=4, out_dtype=F32, name="glu_proj")
    bm = _mm(o, w_mla_o, b_shards=4, out_dtype=F32, name="mla_out")
    (mix,) = _rw(_f_merge, [ab, bm, gt], [], [BF16], name="merge")
    out1 = _mm(mix, w_out, out_dtype=F32, name="out_proj")
    x1, xn2 = _rw(_f_resid_norm, [xs, out1], [g1, n2, sc2, sh2], [F32, BF16], name="resid_norm2")
    arrive(2, [xn2])
    w_ffn_in = gathered['w_ffn_in']
    w_ffn_out = gathered['w_ffn_out'].reshape(d_ff, D)
    ab2 = _mm(xn2, w_ffn_in, b_shards=4, out_dtype=F32, name="ffn_in")
    (hmid,) = _rw(_f_swiglu, [ab2], [], [BF16], name="ffn_act")
    f2 = _mm(hmid, w_ffn_out, out_dtype=F32, name="ffn_out")
    (row_loss,) = _rw(_f_final, [x1, f2, tgt], [g2, nf], [F32], name="final_loss")
    loss = lax.psum(jnp.sum(row_loss), ("x", "y", "c"))

    ones = jnp.ones((L, 1), F32)
    (dx1_a, df2), (dg2, dnf) = _rw_vjp(_f_final, [x1, f2, tgt], [g2, nf], [[ones]], [True, True, False], [True, True],
                                       [F32, BF16], name="final_loss_bwd")
    dhmid = _mm(df2, w_ffn_out, tb=True, out_dtype=F32, name="ffn_out_dx")
    gw_ffn_out = _mm(hmid, df2, ta=True, out_dtype=BF16, name="ffn_out_dw")
    (dab2,), _ = _rw_vjp(_f_swiglu, [ab2], [], [[dhmid]], [True], [], [BF16], name="ffn_act_bwd")
    dxn2 = _mm(dab2, w_ffn_in, tb=True, b_shards=4, out_dtype=F32, name="ffn_in_dx")
    gw_ffn_in = _mm(xn2, dab2, ta=True, out_shards=4, out_dtype=BF16, name="ffn_in_dw")
    rs_ffn, tok = _rs_begin([gw_ffn_out.reshape(4, -1, D), gw_ffn_in], "ffn")
    (dx_a, dout1), (dg1, dn2, dsc2, dsh2) = _rw_vjp(
        _f_resid_norm, [xs, out1], [g1, n2 + tok[0, 0], sc2, sh2], [[dx1_a], [dxn2]], [True, True], [True] * 4, [F32, BF16],
        name="resid_norm2_bwd")
    dmix = _mm(dout1, w_out, tb=True, out_dtype=F32, name="out_proj_dx")
    gw_out = _mm(mix, dout1, ta=True, out_dtype=BF16, name="out_proj_dw")
    (dab, dbm, dgt), _ = _rw_vjp(_f_merge, [ab, bm, gt], [], [[dmix]], [True] * 3, [], [BF16] * 3, name="merge_bwd")
    dz = _mm(dab, w_glu, tb=True, b_shards=4, out_dtype=F32, name="glu_proj_dx")
    gw_glu = _mm(z, dab, ta=True, out_shards=4, out_dtype=BF16, name="glu_proj_dw")
    do = _mm(dbm, w_mla_o, tb=True, b_shards=4, out_dtype=BF16, name="mla_out_dx")
    gw_mla_o = _mm(o, dbm, ta=True, out_shards=4, out_dtype=BF16, name="mla_out_dw")
    dxn_g = _mm(dgt, w_g, tb=True, out_dtype=F32, name="in_gates_dx")
    gw_g = _mm(xn_lat, dgt, ta=True, out_dtype=BF16, name="in_gates_dw")

    (du_a, dr5), (dd_skip,) = _rw_vjp(_f_s5post, [u_lat, r5], [d_skip], [[dz]], [True, True], [True], [F32, F32],
                                      name="s5_post_bwd")
    dh_lat = _bd_fanout(dr5, _tr(w_c), name="s5_readout_dx")
    dw_c = _bd_dw(h_lat, [dr5] * 4, nb_out, name="s5_readout_dw")
    zeros_ctx = jnp.zeros((Lc, C), BF16)
    mu_lat, mu_ctx, dlam = [], [], []
    for d, rev in enumerate((False, True)):
        lr, li = disc[d][0], disc[d][1]
        mlr, mli, fr, fi = _s5_scan(dh_lat[2 * d], dh_lat[2 * d + 1], lr, -li, zero, zero, zero, zero, reverse=not rev,
                                    name=f"s5_adj_lat_{d}")
        dh0r, dh0i = _cmul(lr, -li, fr, fi)
        mcr, mci, _, _ = _s5_scan(zeros_ctx, zeros_ctx, lr, -li, zero, zero, dh0r, dh0i, reverse=not rev,
                                  name=f"s5_adj_ctx_{d}")
        dl_lat = _s5_dlam(mlr, mli, h_lat[2 * d], h_lat[2 * d + 1], hT_ctx[2 * d], hT_ctx[2 * d + 1], reverse=rev,
                          name=f"s5_dlam_lat_{d}")
        dl_ctx = _s5_dlam(mcr, mci, h_ctx[2 * d], h_ctx[2 * d + 1], zero, zero, reverse=rev, name=f"s5_dlam_ctx_{d}")
        mu_lat += [mlr, mli]
        mu_ctx += [mcr, mci]
        dlam.append((dl_lat[0] + dl_ctx[0], dl_lat[1] + dl_ctx[1]))
    du_b = _bd_fanin(mu_lat, _tr(w_b), name="s5_bu_lat_dx")
    du_ctx = _bd_fanin(mu_ctx, _tr(w_b), name="s5_bu_ctx_dx")
    dw_b_lat = _bd_dw([u_lat] * 4, mu_lat, nb_in, name="s5_bu_lat_dw")
    dw_b_ctx = _bd_dw([u_ctx] * 4, mu_ctx, nb_in, name="s5_bu_ctx_dw")
    g_s5 = {}
    for d in range(2):
        ct = (dlam[d][0], dlam[d][1], dw_b_lat[2 * d] + dw_b_ctx[2 * d], dw_b_lat[2 * d + 1] + dw_b_ctx[2 * d + 1])
        ga_re, ga_im, gdt, gb_re, gb_im = vjp_disc[d](ct)
        _, vj_c = jax.vjp(lambda cr, ci: (_diag_blocks_out(cr, gpo), -_diag_blocks_out(ci, gpo)),
                          w['s5_c_re'][0, d], w['s5_c_im'][0, d])
        gc_re, gc_im = vj_c((dw_c[2 * d], dw_c[2 * d + 1]))
        for nme, val in (('s5_a_re', ga_re), ('s5_a_im', ga_im), ('s5_log_dt', gdt), ('s5_b_re', gb_re),
                         ('s5_b_im', gb_im), ('s5_c_re', gc_re), ('s5_c_im', gc_im)):
            g_s5.setdefault(nme, []).append(val)
    g_small = {nme: jnp.stack(vals)[None] for nme, vals in g_s5.items()}
    g_small['s5_d'] = dd_skip.reshape(w['s5_d'].shape)

    dqn, dqr, dkn, dv, dkr = _attn_bwd(qq, kv, kr_all, do, name="attn_bwd")
    dqq = jnp.concatenate([dqn, dqr], axis=1)
    (dq2,), _ = _rw_vjp(_f_qpost, [q2, cos, sin], [], [[dqq]], [True, False, False], [], [BF16], name="q_rope_bwd")
    dcqn = _mm(dq2, w_q2, tb=True, out_dtype=F32, name="q_up_dx")
    gw_q2 = _mm(cqn, dq2, ta=True, out_dtype=BF16, name="q_up_dw")
    dkv = jnp.stack([dkn.reshape(T, H, LANES), dv.reshape(T, H, LANES)], axis=2).reshape(T, 2 * H * LANES)
    dckvn = _mm(dkv, w_ukv, tb=True, b_shards=4, out_dtype=F32, name="kv_up_dx")
    gw_ukv = _mm(kvn, dkv, ta=True, out_shards=4, out_dtype=BF16, name="kv_up_dw")
    uq_nope = gw_q2[:, :H * QK_NOPE].reshape(q_rank, H, QK_NOPE)
    uq_rope = gw_q2[:, H * QK_NOPE:].reshape(q_rank, H, LANES)[:, :, :QK_ROPE]
    gw_uq = jnp.concatenate([uq_nope, uq_rope], axis=2).reshape(q_rank, H * (QK_NOPE + QK_ROPE))
    rs_mix, tok = _rs_begin([gw_out.reshape(4, -1, D), gw_glu, gw_mla_o, _col_blocks(gw_uq, 4), gw_ukv], "mix")

    (dha_lat,), (dqg, dkvg_lat) = _rw_vjp(
        f_post_lat, [ha_lat, cos, sin], [qg, kvg + tok[0, 0]], [[du_a, du_b], [dcqn], [dckvn[:L]], [dkr[:L]]],
        [True, False, False], [True, True], [BF16], name="post_in_lat_bwd")
    (dha_ctx,), (dkvg_ctx,) = _rw_vjp(f_post_ctx, [ha_ctx], [kvg], [[du_ctx], [dckvn[L:]], [dkr[L:]]], [True], [True],
                                      [BF16], name="post_in_ctx_bwd")
    dha = jnp.concatenate([dha_lat, dha_ctx], axis=0)
    dxn = _mm(dha, w_a, tb=True, out_dtype=F32, name="in_proj_dx")
    gw_a = _mm(xn, dha, ta=True, out_dtype=BF16, name="in_proj_dw")
    (dx_seg,), (dn1_lat, dsc1, dsh1) = _rw_vjp(
        _f_norm_mod_keep, [xs], [n1, sc1, sh1], [[dxn[:L], dxn_g], [dx_a]], [True], [True] * 3, [F32], name="norm1_lat_bwd")
    _, (dn1_ctx, dcsc1, dcsh1) = _rw_vjp(_f_norm_mod, [cs], [n1, csc1, csh1], [[dxn[L:]]], [False], [True] * 3, [],
                                         name="norm1_ctx_bwd")
    grad_x = _from_segments(dx_seg)[None]
    g_small.update(norm1=dn1_lat + dn1_ctx, norm2=dn2, q_norm=dqg, kv_norm=dkvg_lat + dkvg_ctx, norm_f=dnf.reshape(D))

    zD = jnp.zeros((1, D), F32)
    dm = jnp.concatenate([
        jnp.concatenate([dsh1, dsc1, dg1, dsh2, dsc2, dg2], axis=1),
        jnp.concatenate([dcsh1, dcsc1, zD, zD, zD, zD], axis=1),
    ], axis=0)
    dm_all = _allgather8(_pad_rows(dm, SUBLANES), name="ag_dmod")
    dm_ctx = dm_all[0, 1]
    for k in range(1, 8):
        dm_ctx = dm_ctx + dm_all[k, 1]
    dmod = _pad_rows(jnp.concatenate([dm_all[:, 0, :], dm_ctx[None]], axis=0), 16)
    g_b_mod = jnp.sum(dmod, axis=0, keepdims=True)
    dmod_mine = lax.dynamic_slice_in_dim(dmod, me_chip * cs_mod, cs_mod, axis=1)
    g_w_mod = _mm(act, dmod_mine, ta=True, out_dtype=F32, name="mod_dw")
    dact_part = _mm(dmod_mine, w_mod, tb=True, out_dtype=F32, name="mod_dx")
    dact_all = _allgather8(dact_part, name="ag_dact")
    dact = dact_all[0] + dact_all[2] + dact_all[4] + dact_all[6]
    (dcond_rows,), _ = _rw_vjp(lambda t: (jax.nn.silu(t),), [cond], [], [[dact]], [True], [], [F32], name="cond_silu_bwd")
    g_c_ctx = dcond_rows[8]

    gw_in = jnp.concatenate([gw_a[:, :wa_used], gw_g], axis=1)
    small_vals = [g_small[nme] for nme in SMALL]
    n_small = sum(val.size for val in small_vals)
    small_rows = -(-n_small // (LANES * 4 * 32)) * 32
    rs_in, tok = _rs_begin([_col_blocks(gw_in, 4), _pack(small_vals, LANES, 4 * small_rows).reshape(4, small_rows, LANES)],
                           "in", anchor=g_c_ctx)

    grads, delta, new_m, new_v = {}, {}, {}, {}

    def update(nme, red, anchor=None):
        res = _adamw(w[nme][0], red, m[nme][0], v[nme][0], name=f"adamw_{nme}", anchor=anchor)
        grads[nme], delta[nme], new_m[nme], new_v[nme] = (r.reshape(w[nme].shape) for r in res)
        return res[1]

    after = [update('w_mod', g_w_mod, anchor=tok)]
    for handle, tag, members in ((rs_ffn, "ffn", ['w_ffn_out', 'w_ffn_in']),
                                 (rs_mix, "mix", ['w_out', 'w_glu', 'w_mla_o', 'w_uq', 'w_ukv']),
                                 (rs_in, "in", ['w_in'])):
        reduced = _rs_end(handle, after, tag)
        for nme, red in zip(members, reduced):
            after.append(update(nme, red))
    small_mine = reduced[-1]
    small_buf = _into_slot(small_mine, me_chip, 4, F32, name="small_grads_slot")
    small_all = _allgather_shards([small_buf], name="ag_small_grads")[0].reshape(4 * small_rows, LANES)
    g_small_red = dict(zip(SMALL, _unpack(small_all, [w[nme] for nme in SMALL])))
    rest = SMALL + ['c_ctx', 'b_mod']
    g_rest = dict(g_small_red, c_ctx=g_c_ctx, b_mod=g_b_mod)
    rows_rest = -(-sum(w[nme].size for nme in rest) // (LANES * 16)) * 16
    packed = [_pack([src[nme] for nme in rest], LANES, rows_rest) for src in (w, g_rest, m, v)]
    res = _adamw(*packed, name="adamw_small")
    for dst, buf in zip((grads, delta, new_m, new_v), res):
        dst.update(zip(rest, _unpack(buf, [w[nme] for nme in rest])))
    return (loss, grad_x, *[grads[nme] for nme in WEIGHTS], *[delta[nme] for nme in WEIGHTS],
            *[new_m[nme] for nme in WEIGHTS], *[new_v[nme] for nme in WEIGHTS])


def kernel(x, c, ctx, c_ctx, w_mod, b_mod, norm1, norm2, w_in, s5_a_re, s5_a_im, s5_log_dt, s5_b_re, s5_b_im, s5_c_re, s5_c_im, s5_d, w_glu, q_norm, kv_norm, w_uq, w_ukv, w_mla_o, w_out, w_ffn_in, w_ffn_out, norm_f, loss_target, m_c_ctx, m_w_mod, m_b_mod, m_norm1, m_norm2, m_w_in, m_s5_a_re, m_s5_a_im, m_s5_log_dt, m_s5_b_re, m_s5_b_im, m_s5_c_re, m_s5_c_im, m_s5_d, m_w_glu, m_q_norm, m_kv_norm, m_w_uq, m_w_ukv, m_w_mla_o, m_w_out, m_w_ffn_in, m_w_ffn_out, m_norm_f, v_c_ctx, v_w_mod, v_b_mod, v_norm1, v_norm2, v_w_in, v_s5_a_re, v_s5_a_im, v_s5_log_dt, v_s5_b_re, v_s5_b_im, v_s5_c_re, v_s5_c_im, v_s5_d, v_w_glu, v_q_norm, v_kv_norm, v_w_uq, v_w_ukv, v_w_mla_o, v_w_out, v_w_ffn_in, v_w_ffn_out, v_norm_f):
    w = dict(c_ctx=c_ctx, w_mod=w_mod, b_mod=b_mod, norm1=norm1, norm2=norm2, w_in=w_in, s5_a_re=s5_a_re, s5_a_im=s5_a_im,
             s5_log_dt=s5_log_dt, s5_b_re=s5_b_re, s5_b_im=s5_b_im, s5_c_re=s5_c_re, s5_c_im=s5_c_im, s5_d=s5_d, w_glu=w_glu,
             q_norm=q_norm, kv_norm=kv_norm, w_uq=w_uq, w_ukv=w_ukv, w_mla_o=w_mla_o, w_out=w_out, w_ffn_in=w_ffn_in,
             w_ffn_out=w_ffn_out, norm_f=norm_f)
    m = dict(c_ctx=m_c_ctx, w_mod=m_w_mod, b_mod=m_b_mod, norm1=m_norm1, norm2=m_norm2, w_in=m_w_in, s5_a_re=m_s5_a_re,
             s5_a_im=m_s5_a_im, s5_log_dt=m_s5_log_dt, s5_b_re=m_s5_b_re, s5_b_im=m_s5_b_im, s5_c_re=m_s5_c_re,
             s5_c_im=m_s5_c_im, s5_d=m_s5_d, w_glu=m_w_glu, q_norm=m_q_norm, kv_norm=m_kv_norm, w_uq=m_w_uq, w_ukv=m_w_ukv,
             w_mla_o=m_w_mla_o, w_out=m_w_out, w_ffn_in=m_w_ffn_in, w_ffn_out=m_w_ffn_out, norm_f=m_norm_f)
    v = dict(c_ctx=v_c_ctx, w_mod=v_w_mod, b_mod=v_b_mod, norm1=v_norm1, norm2=v_norm2, w_in=v_w_in, s5_a_re=v_s5_a_re,
             s5_a_im=v_s5_a_im, s5_log_dt=v_s5_log_dt, s5_b_re=v_s5_b_re, s5_b_im=v_s5_b_im, s5_c_re=v_s5_c_re,
             s5_c_im=v_s5_c_im, s5_d=v_s5_d, w_glu=v_w_glu, q_norm=v_q_norm, kv_norm=v_kv_norm, w_uq=v_w_uq, w_ukv=v_w_ukv,
             w_mla_o=v_w_mla_o, w_out=v_w_out, w_ffn_in=v_w_ffn_in, w_ffn_out=v_w_ffn_out, norm_f=v_norm_f)
    return _step(x, c, ctx, loss_target, w, m, v)
```

```python
import functools
import math

import jax
import jax.numpy as jnp
from jax import lax
from jax.experimental import pallas as pl
from jax.experimental.pallas import tpu as pltpu

F32 = jnp.float32
BF16 = jnp.bfloat16

EPS = 1e-6
GRID_W = 64
S5_GROUP = 16
S5_STATE = 64
MLA_HEADS = 8
QK_NOPE = 128
QK_ROPE = 64
V_DIM = 128
ROPE_BASE = 10000.0
ATTN_SCALE = (QK_NOPE + QK_ROPE) ** -0.5
ADAM_LR = 0.001
ADAM_B1 = 0.9
ADAM_B2 = 0.999
ADAM_EPS = 1e-08
ADAM_WD = 0.01
ADAM_STEP = 10

SUBLANES = 8
LANES = 128
V7X_VMEM_BYTES = 64 * 1024 * 1024
VMEM_LIMIT = (V7X_VMEM_BYTES * 7) // 8
N_SEG = 2 * SUBLANES
S5_BLOCK_GROUPS = 16
MESH = pl.DeviceIdType.MESH


def _pick(n, target, mult):
    best = None
    d = mult
    while d <= min(n, target):
        if n % d == 0:
            best = d
        d += mult
    return n if best is None else best


def _cparams(sem=None):
    return pltpu.CompilerParams(dimension_semantics=sem, vmem_limit_bytes=VMEM_LIMIT)


MM_VMEM_BUDGET = (V7X_VMEM_BYTES * 5) // 8


def _mm(a, b, *, ta=False, tb=False, out_dtype=F32, name, b_shards=1, out_shards=1):
    if ta:
        K, M = a.shape
    else:
        M, K = a.shape
    if tb:
        N, K2 = b.shape[-2], b.shape[-1] * b_shards
    else:
        K2, N = b.shape[-2], b.shape[-1] * b_shards
    assert K == K2, (a.shape, b.shape, ta, tb)
    n_unit = N // max(out_shards, 1 if tb else b_shards)
    k_unit = K // (b_shards if tb else 1)
    tn = _pick(n_unit, 1024, LANES)
    tm = _pick(M, 1024 if tn >= 512 else 2048, LANES if ta else 16)
    sa, sb, so = a.dtype.itemsize, b.dtype.itemsize, jnp.dtype(out_dtype).itemsize
    k_mult = LANES if (not ta or tb) else 16
    tk = k_mult if k_unit % k_mult == 0 else k_unit
    for cand in range(k_mult, k_unit + 1, k_mult):
        if k_unit % cand == 0 and 2 * cand * (tm * sa + tn * sb) + tm * tn * (4 + 2 * so) <= MM_VMEM_BUDGET:
            tk = cand
    nk = K // tk
    dims = (((0 if ta else 1,), (1 if tb else 0,)), ((), ()))

    def body(a_ref, b_ref, o_ref, *scratch):
        part = lax.dot_general(a_ref[...].astype(BF16), b_ref[...].astype(BF16), dims, preferred_element_type=F32)
        if nk == 1:
            o_ref[...] = part.astype(o_ref.dtype)
            return
        acc_ref, = scratch
        k = pl.program_id(2)

        @pl.when(k == 0)
        def _():
            acc_ref[...] = part

        @pl.when(k > 0)
        def _():
            acc_ref[...] += part

        @pl.when(k == nk - 1)
        def _():
            o_ref[...] = acc_ref[...].astype(o_ref.dtype)

    a_spec = pl.BlockSpec((tk, tm), lambda i, j, k: (k, i)) if ta else pl.BlockSpec((tm, tk), lambda i, j, k: (i, k))
    if b_shards == 1:
        b_spec = pl.BlockSpec((tn, tk), lambda i, j, k: (j, k)) if tb else pl.BlockSpec((tk, tn), lambda i, j, k: (k, j))
    elif tb:
        kpb = k_unit // tk
        b_spec = pl.BlockSpec((None, tn, tk), lambda i, j, k: (k // kpb, j, k % kpb))
    else:
        npb = n_unit // tn
        b_spec = pl.BlockSpec((None, tk, tn), lambda i, j, k: (j // npb, k, j % npb))
    if out_shards == 1:
        out_spec = pl.BlockSpec((tm, tn), lambda i, j, k: (i, j))
        out_shape = jax.ShapeDtypeStruct((M, N), out_dtype)
    else:
        opb = n_unit // tn
        out_spec = pl.BlockSpec((None, tm, tn), lambda i, j, k: (j // opb, i, j % opb))
        out_shape = jax.ShapeDtypeStruct((out_shards, M, N // out_shards), out_dtype)
    return pl.pallas_call(
        body, name=name, grid=(M // tm, N // tn, nk),
        in_specs=[a_spec, b_spec], out_specs=out_spec, out_shape=out_shape,
        scratch_shapes=[pltpu.VMEM((tm, tn), F32)] if nk > 1 else [],
        compiler_params=_cparams(("parallel", "parallel", "arbitrary")),
    )(a, b)


def _row_tile(tiled, extra_bytes=0):
    rows = tiled[0].shape[0]
    per_row = sum(a.shape[1] * 4 for a in tiled) + extra_bytes
    target = max(SUBLANES, (6 * 1024 * 1024) // max(per_row, 1))
    return _pick(rows, min(target, 512), 16)


def _rw(f, tiled, bcast, out_dtypes, *, name, anchor=None):
    nt, nb = len(tiled), len(bcast)
    rows = tiled[0].shape[0]
    outs_aval = jax.eval_shape(f, *[jax.ShapeDtypeStruct((16, a.shape[1]), F32) for a in tiled],
                               *[jax.ShapeDtypeStruct(b.shape, F32) for b in bcast])
    widths = [o.shape[1] for o in outs_aval]
    tm = _row_tile(tiled, sum(w * 4 for w in widths))

    extra = [] if anchor is None else [anchor]
    n_in = nt + nb + len(extra)

    def body(*refs):
        tin = [r[...].astype(F32) for r in refs[:nt]]
        bin_ = [r[...].astype(F32) for r in refs[nt:nt + nb]]
        outs = f(*tin, *bin_)
        for o_ref, o in zip(refs[n_in:], outs):
            o_ref[...] = o.astype(o_ref.dtype)

    in_specs = [pl.BlockSpec((tm, a.shape[1]), lambda i: (i, 0)) for a in tiled]
    in_specs += [pl.BlockSpec(b.shape, lambda i: (0, 0)) for b in bcast + extra]
    res = pl.pallas_call(
        body, name=name, grid=(rows // tm,), in_specs=in_specs,
        out_specs=[pl.BlockSpec((tm, w), lambda i: (i, 0)) for w in widths],
        out_shape=[jax.ShapeDtypeStruct((rows, w), dt) for w, dt in zip(widths, out_dtypes)],
        compiler_params=_cparams(("parallel",)),
    )(*tiled, *bcast, *extra)
    return list(res)


def _rw_vjp(f, tiled, bcast, cts, need_t, need_b, t_dtypes, *, name):
    nt, nb = len(tiled), len(bcast)
    rows = tiled[0].shape[0]
    flat_cts = [c for group in cts for c in group]
    t_idx = [i for i in range(nt) if need_t[i]]
    b_idx = [i for i in range(nb) if need_b[i]]
    tm = _row_tile(list(tiled) + flat_cts, sum(tiled[i].shape[1] * 4 for i in t_idx))
    nc = len(flat_cts)

    def body(*refs):
        i = pl.program_id(0)
        tin = [r[...].astype(F32) for r in refs[:nt]]
        bin_ = [r[...].astype(F32) for r in refs[nt:nt + nb]]
        ct_refs = refs[nt + nb:nt + nb + nc]
        out_refs = refs[nt + nb + nc:]
        outs, vjp_fn = jax.vjp(f, *tin, *bin_)
        ct_vals, pos = [], 0
        for o, group in zip(outs, cts):
            acc = jnp.zeros_like(o)
            for _ in group:
                acc = acc + ct_refs[pos][...].astype(F32)
                pos += 1
            ct_vals.append(acc)
        grads = vjp_fn(tuple(ct_vals))
        for o_ref, k in zip(out_refs[:len(t_idx)], t_idx):
            o_ref[...] = grads[k].astype(o_ref.dtype)
        for o_ref, k in zip(out_refs[len(t_idx):], b_idx):
            @pl.when(i == 0)
            def _(o_ref=o_ref):
                o_ref[...] = jnp.zeros_like(o_ref)

            o_ref[...] += grads[nt + k]

    in_specs = [pl.BlockSpec((tm, a.shape[1]), lambda i: (i, 0)) for a in tiled]
    in_specs += [pl.BlockSpec(b.shape, lambda i: (0, 0)) for b in bcast]
    in_specs += [pl.BlockSpec((tm, c.shape[1]), lambda i: (i, 0)) for c in flat_cts]
    out_specs = [pl.BlockSpec((tm, tiled[k].shape[1]), lambda i: (i, 0)) for k in t_idx]
    out_specs += [pl.BlockSpec(bcast[k].shape, lambda i: (0, 0)) for k in b_idx]
    out_shape = [jax.ShapeDtypeStruct(tiled[k].shape, dt) for k, dt in zip(t_idx, t_dtypes)]
    out_shape += [jax.ShapeDtypeStruct(bcast[k].shape, F32) for k in b_idx]
    res = pl.pallas_call(
        body, name=name, grid=(rows // tm,), in_specs=in_specs, out_specs=out_specs, out_shape=out_shape,
        compiler_params=_cparams(("arbitrary",)),
    )(*tiled, *bcast, *flat_cts)
    res = list(res)
    return res[:len(t_idx)], res[len(t_idx):]


def _rms(x, g):
    return x * lax.rsqrt(jnp.mean(x * x, axis=-1, keepdims=True) + EPS) * g


def _f_norm_mod(x, g, sc, sh):
    return (_rms(x, g) * (1.0 + sc) + sh,)


def _f_norm_mod_keep(x, g, sc, sh):
    return (_rms(x, g) * (1.0 + sc) + sh, x)


@jax.custom_vjp
def _swap16(x):
    w = x.shape[-1]
    lane = lax.broadcasted_iota(jnp.int32, x.shape, x.ndim - 1)
    return jnp.where((lane & 16) == 0, pltpu.roll(x, w - 16, x.ndim - 1), pltpu.roll(x, 16, x.ndim - 1))


_swap16.defvjp(lambda x: (_swap16(x), None), lambda _, g: (_swap16(g),))


def _rope(x, cos, sin):
    return x * cos + _swap16(x) * sin


def _make_f_post_in(sw, q_rank, kv_rank, with_q):
    o1, o2, o3 = sw, sw + q_rank, sw + q_rank + kv_rank

    if with_q:
        def f(ha, cos, sin, qg, kvg):
            u = ha[:, :o1]
            cqn = _rms(ha[:, o1:o2], qg)
            ckvn = _rms(ha[:, o2:o3], kvg)
            kr = _rope(ha[:, o3:o3 + LANES], cos, sin)
            return u, cqn, ckvn, kr
    else:
        def f(ha, kvg):
            return ha[:, :o1], _rms(ha[:, o2:o3], kvg), ha[:, o3:o3 + LANES]
    return f


def _f_qpost(q2, cos, sin):
    parts = []
    for h in range(q2.shape[1] // (2 * LANES)):
        o = 2 * LANES * h
        parts += [q2[:, o:o + LANES], _rope(q2[:, o + LANES:o + 2 * LANES], cos, sin)]
    return (jnp.concatenate(parts, axis=1),)


def _f_s5post(u, r, d):
    return (jax.nn.gelu(d * u + r, approximate=True),)


def _f_merge(ab, bm, gt):
    d = bm.shape[1]
    br_s5 = ab[:, :d] * jax.nn.sigmoid(ab[:, d:])
    g = jax.nn.sigmoid(gt)
    return (g[:, :d] * br_s5 + g[:, d:] * bm,)


def _f_resid_norm(x, out, g1, n2, sc2, sh2):
    x1 = x + g1 * out
    return x1, _rms(x1, n2) * (1.0 + sc2) + sh2


def _f_swiglu(ab):
    d = ab.shape[1] // 2
    return (jax.nn.silu(ab[:, :d]) * ab[:, d:],)


def _f_final(x1, f, tgt, g2, nf):
    y = _rms(x1 + g2 * f, nf)
    return (0.5 * jnp.mean(jnp.square(y - tgt), axis=-1, keepdims=True),)


def _bd_fanout(x, ws, *, name):
    nw = len(ws)
    nb, kb, nn = ws[0].shape
    T = x.shape[0]
    tm = _pick(T, 512, 16)

    def body(*refs):
        xb = refs[0][...].astype(BF16)
        for w_ref, o_ref in zip(refs[1:1 + nw], refs[1 + nw:]):
            o_ref[...] = jnp.dot(xb, w_ref[0].astype(BF16), preferred_element_type=F32).astype(o_ref.dtype)

    return list(pl.pallas_call(
        body, name=name, grid=(nb, T // tm),
        in_specs=[pl.BlockSpec((tm, kb), lambda j, i: (i, j))] + [pl.BlockSpec((1, kb, nn), lambda j, i: (j, 0, 0))] * nw,
        out_specs=[pl.BlockSpec((tm, nn), lambda j, i: (i, j))] * nw,
        out_shape=[jax.ShapeDtypeStruct((T, nb * nn), BF16)] * nw,
        compiler_params=_cparams(("parallel", "parallel")),
    )(x, *ws))


def _bd_fanin(xs, ws, *, name):
    nw = len(ws)
    nb, kb, nn = ws[0].shape
    T = xs[0].shape[0]
    tm = _pick(T, 512, 16)

    def body(*refs):
        acc = None
        for x_ref, w_ref in zip(refs[:nw], refs[nw:2 * nw]):
            t = jnp.dot(x_ref[...].astype(BF16), w_ref[0].astype(BF16), preferred_element_type=F32)
            acc = t if acc is None else acc + t
        refs[2 * nw][...] = acc

    return pl.pallas_call(
        body, name=name, grid=(nb, T // tm),
        in_specs=[pl.BlockSpec((tm, kb), lambda j, i: (i, j))] * nw + [pl.BlockSpec((1, kb, nn), lambda j, i: (j, 0, 0))] * nw,
        out_specs=pl.BlockSpec((tm, nn), lambda j, i: (i, j)),
        out_shape=jax.ShapeDtypeStruct((T, nb * nn), F32),
        compiler_params=_cparams(("parallel", "parallel")),
    )(*xs, *ws)


def _bd_dw(xs, dys, nb, *, name):
    npair = len(xs)
    T = xs[0].shape[0]
    kb = xs[0].shape[1] // nb
    nn = dys[0].shape[1] // nb
    tm = _pick(T, 512, 16)
    dims = (((0,), (0,)), ((), ()))

    def body(*refs):
        i = pl.program_id(1)
        for x_ref, d_ref, o_ref in zip(refs[:npair], refs[npair:2 * npair], refs[2 * npair:]):
            @pl.when(i == 0)
            def _(o_ref=o_ref):
                o_ref[...] = jnp.zeros_like(o_ref)

            o_ref[0] += lax.dot_general(x_ref[...].astype(BF16), d_ref[...].astype(BF16), dims,
                                        preferred_element_type=F32)

    return list(pl.pallas_call(
        body, name=name, grid=(nb, T // tm),
        in_specs=[pl.BlockSpec((tm, kb), lambda j, i: (i, j))] * npair + [pl.BlockSpec((tm, nn), lambda j, i: (i, j))] * npair,
        out_specs=[pl.BlockSpec((1, kb, nn), lambda j, i: (j, 0, 0))] * npair,
        out_shape=[jax.ShapeDtypeStruct((nb, kb, nn), F32)] * npair,
        compiler_params=_cparams(("parallel", "arbitrary")),
    )(*xs, *dys))


def _cmul(ar, ai, br, bi):
    return ar * br - ai * bi, ar * bi + ai * br


def _cpow(lr, li, n):
    rr, ri = None, None
    br, bi = lr, li
    while n:
        if n & 1:
            rr, ri = (br, bi) if rr is None else _cmul(rr, ri, br, bi)
        n >>= 1
        if n:
            br, bi = _cmul(br, bi, br, bi)
    return rr, ri


def _s5_scan(b_re, b_im, lam_re, lam_im, h0_re, h0_im, e0_re, e0_im, *, reverse, name):
    rows, C = b_re.shape
    n = rows // N_SEG
    cb = _pick(C, 512, LANES)
    seg_order = list(range(N_SEG))[::-1] if reverse else list(range(N_SEG))
    s_first, s_last = seg_order[0], seg_order[-1]

    def body(br_ref, bi_ref, lr_ref, li_ref, h0r_ref, h0i_ref, e0r_ref, e0i_ref, hr_ref, hi_ref, htr_ref, hti_ref,
             locr_ref, loci_ref):
        shape = (N_SEG, cb)
        lr = jnp.broadcast_to(lr_ref[...], shape)
        li = jnp.broadcast_to(li_ref[...], shape)
        row = lax.broadcasted_iota(jnp.int32, shape, 0)

        def step_of(k):
            return (n - 1 - k) if reverse else k

        def rows_of(k):
            return pl.ds(pl.multiple_of(step_of(k) * N_SEG, N_SEG), N_SEG)

        first = row == s_first
        hr = br_ref[rows_of(0), :].astype(F32) + jnp.where(first, e0r_ref[...], 0.0)
        hi = bi_ref[rows_of(0), :].astype(F32) + jnp.where(first, e0i_ref[...], 0.0)
        locr_ref[rows_of(0), :] = hr
        loci_ref[rows_of(0), :] = hi

        def pass1(k, carry):
            hr, hi = carry
            pr, pi = _cmul(lr, li, hr, hi)
            hr = pr + br_ref[rows_of(k), :].astype(F32)
            hi = pi + bi_ref[rows_of(k), :].astype(F32)
            locr_ref[rows_of(k), :] = hr
            loci_ref[rows_of(k), :] = hi
            return hr, hi

        er, ei = lax.fori_loop(1, n, pass1, (hr, hi))

        lnr, lni = _cpow(lr_ref[...], li_ref[...], n)
        cr, ci = h0r_ref[...], h0i_ref[...]
        cin_r = jnp.zeros(shape, F32)
        cin_i = jnp.zeros(shape, F32)
        for s in seg_order:
            cin_r = jnp.where(row == s, cr, cin_r)
            cin_i = jnp.where(row == s, ci, cin_i)
            if s != s_last:
                pr, pi = _cmul(lnr, lni, cr, ci)
                cr = pr + jnp.sum(jnp.where(row == s, er, 0.0), axis=0, keepdims=True)
                ci = pi + jnp.sum(jnp.where(row == s, ei, 0.0), axis=0, keepdims=True)

        def pass2(k, carry):
            pr, pi, _, _ = carry
            ar, ai = _cmul(pr, pi, cin_r, cin_i)
            hr = locr_ref[rows_of(k), :] + ar
            hi = loci_ref[rows_of(k), :] + ai
            hr_ref[rows_of(k), :] = hr.astype(hr_ref.dtype)
            hi_ref[rows_of(k), :] = hi.astype(hi_ref.dtype)
            npr, npi = _cmul(pr, pi, lr, li)
            return npr, npi, hr, hi

        _, _, last_r, last_i = lax.fori_loop(0, n, pass2, (lr, li, er, ei))
        htr_ref[...] = jnp.sum(jnp.where(row == s_last, last_r, 0.0), axis=0, keepdims=True)
        hti_ref[...] = jnp.sum(jnp.where(row == s_last, last_i, 0.0), axis=0, keepdims=True)

    big = pl.BlockSpec((rows, cb), lambda j: (0, j))
    vec = pl.BlockSpec((1, cb), lambda j: (0, j))
    return pl.pallas_call(
        body, name=name, grid=(C // cb,),
        in_specs=[big, big] + [vec] * 6,
        out_specs=[big, big, vec, vec],
        out_shape=[jax.ShapeDtypeStruct((rows, C), BF16)] * 2 + [jax.ShapeDtypeStruct((1, C), F32)] * 2,
        scratch_shapes=[pltpu.VMEM((rows, cb), F32)] * 2,
        compiler_params=_cparams(("parallel",)),
    )(b_re, b_im, lam_re, lam_im, h0_re, h0_im, e0_re, e0_im)


def _s5_dlam(mu_re, mu_im, h_re, h_im, h0_re, h0_im, *, reverse, name):
    rows, C = h_re.shape
    n = rows // N_SEG
    cb = _pick(C, 256, LANES)
    s_first = N_SEG - 1 if reverse else 0

    def body(mr_ref, mi_ref, hr_ref, hi_ref, h0r_ref, h0i_ref, dr_ref, di_ref):
        shape = (N_SEG, cb)
        row = lax.broadcasted_iota(jnp.int32, shape, 0)

        def rows_of(k):
            step = (n - 1 - k) if reverse else k
            return pl.ds(pl.multiple_of(step * N_SEG, N_SEG), N_SEG)

        def term(k, pr, pi):
            mr, mi = mr_ref[rows_of(k), :].astype(F32), mi_ref[rows_of(k), :].astype(F32)
            return mr * pr + mi * pi, mi * pr - mr * pi

        shift = N_SEG - 1 if reverse else 1
        pr = jnp.where(row == s_first, h0r_ref[...], pltpu.roll(hr_ref[rows_of(n - 1), :].astype(F32), shift, 0))
        pi = jnp.where(row == s_first, h0i_ref[...], pltpu.roll(hi_ref[rows_of(n - 1), :].astype(F32), shift, 0))
        acc = term(0, pr, pi)

        def loop(k, acc):
            tr, ti = term(k, hr_ref[rows_of(k - 1), :].astype(F32), hi_ref[rows_of(k - 1), :].astype(F32))
            return acc[0] + tr, acc[1] + ti

        ar, ai = lax.fori_loop(1, n, loop, acc)
        dr_ref[...] = jnp.sum(ar, axis=0, keepdims=True)
        di_ref[...] = jnp.sum(ai, axis=0, keepdims=True)

    big = pl.BlockSpec((rows, cb), lambda j: (0, j))
    vec = pl.BlockSpec((1, cb), lambda j: (0, j))
    return pl.pallas_call(
        body, name=name, grid=(C // cb,),
        in_specs=[big] * 4 + [vec] * 2, out_specs=[vec, vec],
        out_shape=[jax.ShapeDtypeStruct((1, C), F32)] * 2,
        compiler_params=_cparams(("parallel",)),
    )(mu_re, mu_im, h_re, h_im, h0_re, h0_im)


NT_DIMS = (((1,), (1,)), ((), ()))
TN_DIMS = (((0,), (0,)), ((), ()))


ATTN_Q_ROWS = 512


def _attn_exp(q, kvh, kr):
    s = (lax.dot_general(q[:, :LANES], kvh[:, :LANES], NT_DIMS, preferred_element_type=F32)
         + lax.dot_general(q[:, LANES:], kr, NT_DIMS, preferred_element_type=F32))
    e = jnp.exp2((s - jnp.max(s, axis=-1, keepdims=True)) * (ATTN_SCALE * math.log2(math.e)))
    return e, jnp.sum(e, axis=-1, keepdims=True)


def _attn_specs(L, T, tq):
    return [
        pl.BlockSpec((tq, 2 * LANES), lambda h, i: (i, h)),
        pl.BlockSpec((T, 2 * LANES), lambda h, i: (0, h)),
        pl.BlockSpec((T, LANES), lambda h, i: (0, 0)),
    ]


def _attn_fwd(qq, kv, kr, *, name):
    L, T = qq.shape[0], kv.shape[0]
    tq = _pick(L, ATTN_Q_ROWS, 16)

    def body(q_ref, kv_ref, kr_ref, o_ref):
        kvh = kv_ref[...]
        e, l = _attn_exp(q_ref[...], kvh, kr_ref[...])
        o_ref[...] = (jnp.dot(e.astype(BF16), kvh[:, LANES:], preferred_element_type=F32) * (1.0 / l)).astype(o_ref.dtype)

    return pl.pallas_call(
        body, name=name, grid=(MLA_HEADS, L // tq), in_specs=_attn_specs(L, T, tq),
        out_specs=pl.BlockSpec((tq, LANES), lambda h, i: (i, h)),
        out_shape=jax.ShapeDtypeStruct((L, MLA_HEADS * V_DIM), BF16),
        compiler_params=_cparams(("parallel", "parallel")),
    )(qq, kv, kr)


def _attn_bwd(qq, kv, kr, do, *, name):
    L, T = qq.shape[0], kv.shape[0]
    H = MLA_HEADS
    tq = _pick(L, ATTN_Q_ROWS, 16)
    nq = L // tq

    def body(q_ref, kv_ref, kr_ref, do_ref, dq_ref, dkv_ref, dkr_ref, dkn_acc, dv_acc):
        h, i = pl.program_id(0), pl.program_id(1)
        q, kvh, krv, dov = q_ref[...], kv_ref[...], kr_ref[...], do_ref[...]
        e, l = _attn_exp(q, kvh, krv)
        inv = 1.0 / l
        ps = e * (inv * ATTN_SCALE)
        t = lax.dot_general(dov, kvh[:, LANES:], NT_DIMS, preferred_element_type=F32) * ps
        ds = (t - ps * (jnp.sum(t, axis=-1, keepdims=True) * (1.0 / ATTN_SCALE))).astype(BF16)
        dq_ref[:, :LANES] = jnp.dot(ds, kvh[:, :LANES], preferred_element_type=F32)
        dq_ref[:, LANES:] = jnp.dot(ds, krv, preferred_element_type=F32)

        @pl.when(i == 0)
        def _():
            dkn_acc[...] = jnp.zeros_like(dkn_acc)
            dv_acc[...] = jnp.zeros_like(dv_acc)

        @pl.when((i == 0) & (h == 0))
        def _():
            dkr_ref[...] = jnp.zeros_like(dkr_ref)

        dv_acc[...] += lax.dot_general(e.astype(BF16), (dov.astype(F32) * inv).astype(BF16), TN_DIMS,
                                       preferred_element_type=F32)
        dkn_acc[...] += lax.dot_general(ds, q[:, :LANES], TN_DIMS, preferred_element_type=F32)
        dkr_ref[...] += lax.dot_general(ds, q[:, LANES:], TN_DIMS, preferred_element_type=F32)

        @pl.when(i == nq - 1)
        def _():
            dkv_ref[:, :LANES] = dkn_acc[...].astype(dkv_ref.dtype)
            dkv_ref[:, LANES:] = dv_acc[...].astype(dkv_ref.dtype)

    in_specs = _attn_specs(L, T, tq) + [pl.BlockSpec((tq, LANES), lambda h, i: (i, h))]
    return pl.pallas_call(
        body, name=name, grid=(H, L // tq), in_specs=in_specs,
        out_specs=[pl.BlockSpec((tq, 2 * LANES), lambda h, i: (i, h)), pl.BlockSpec((T, 2 * LANES), lambda h, i: (0, h)),
                   pl.BlockSpec((T, LANES), lambda h, i: (0, 0))],
        out_shape=[jax.ShapeDtypeStruct((L, H * 2 * LANES), F32), jax.ShapeDtypeStruct((T, H * 2 * LANES), BF16),
                   jax.ShapeDtypeStruct((T, LANES), F32)],
        scratch_shapes=[pltpu.VMEM((T, LANES), F32), pltpu.VMEM((T, LANES), F32)],
        compiler_params=_cparams(("arbitrary", "arbitrary")),
    )(qq, kv, kr, do)


def _adamw(w, g, m, v, *, name, anchor=None):
    c1 = 1.0 - ADAM_B1 ** ADAM_STEP
    c2 = 1.0 - ADAM_B2 ** ADAM_STEP

    def f(w, g, m, v):
        m = ADAM_B1 * m + (1.0 - ADAM_B1) * g
        v = ADAM_B2 * v + (1.0 - ADAM_B2) * jnp.square(g)
        delta = -ADAM_LR * ((m / c1) / (jnp.sqrt(v / c2) + ADAM_EPS) + ADAM_WD * w)
        return g, delta, m, v

    return _rw(f, [w, g, m, v], [], [F32] * 4, name=name, anchor=anchor)


def _slab_rows(rows, cols, n_arrays):
    return _pick(rows, max(16, (8 * 1024 * 1024) // (cols * 4 * n_arrays)), 16)


def _scalars(*vals):
    return jnp.stack([jnp.asarray(v, jnp.int32) for v in vals])


def _into_slot(src, slot, nslots, dtype, *, name):
    R, C = src.shape
    tr = _slab_rows(R, C, 2)

    def body(s_ref, x_ref, o_ref):
        o_ref[...] = x_ref[...].astype(o_ref.dtype)

    return pl.pallas_call(
        body, name=name,
        grid_spec=pltpu.PrefetchScalarGridSpec(
            num_scalar_prefetch=1, grid=(R // tr,),
            in_specs=[pl.BlockSpec((tr, C), lambda i, s: (i, 0))],
            out_specs=pl.BlockSpec((None, tr, C), lambda i, s: (s[0], i, 0))),
        out_shape=jax.ShapeDtypeStruct((nslots, R, C), dtype),
        compiler_params=_cparams(("arbitrary",)),
    )(_scalars(slot), src)


def _pair_sum(g, got, c, *, name):
    _, R, C = g.shape
    hr = R // 2
    tr = _slab_rows(hr, C, 3)
    nblk = hr // tr

    def body(s_ref, g_ref, r_ref, o_ref):
        o_ref[...] = (g_ref[...].astype(F32) + r_ref[...].astype(F32)).astype(o_ref.dtype)

    return pl.pallas_call(
        body, name=name,
        grid_spec=pltpu.PrefetchScalarGridSpec(
            num_scalar_prefetch=1, grid=(4, nblk),
            in_specs=[pl.BlockSpec((None, tr, C), lambda j, i, s: (j, s[0] * nblk + i, 0)),
                      pl.BlockSpec((None, tr, C), lambda j, i, s: (j, i, 0))],
            out_specs=pl.BlockSpec((None, tr, C), lambda j, i, s: (j, i, 0))),
        out_shape=jax.ShapeDtypeStruct((4, hr, C), g.dtype),
        compiler_params=_cparams(("arbitrary", "arbitrary")),
    )(_scalars(c), g, got)


def _chip_sum(p, landed, me_chip, c, *, name):
    _, hr, C = p.shape
    tr = _slab_rows(hr, C, 5)

    def body(s_ref, p_ref, l0_ref, l1_ref, l2_ref, o_ref):
        o_ref[...] = ((p_ref[...].astype(F32) + l0_ref[...].astype(F32)) + l1_ref[...].astype(F32)) + l2_ref[...].astype(F32)

    return pl.pallas_call(
        body, name=name,
        grid_spec=pltpu.PrefetchScalarGridSpec(
            num_scalar_prefetch=1, grid=(hr // tr,),
            in_specs=[pl.BlockSpec((None, tr, C), lambda i, s: (s[0], i, 0))]
            + [pl.BlockSpec((None, tr, C), functools.partial(lambda i, s, k: (k, i, 0), k=k)) for k in range(3)],
            out_specs=pl.BlockSpec((None, tr, C), lambda i, s: (s[1], i, 0))),
        out_shape=jax.ShapeDtypeStruct((2, hr, C), F32),
        compiler_params=_cparams(("arbitrary",)),
    )(_scalars(me_chip, c), p, landed, landed, landed)


def _place():
    return lax.axis_index("x"), lax.axis_index("y"), lax.axis_index("c")


def _other_chips(x, y):
    chips = [(1 - x, y), (x, 1 - y), (1 - x, 1 - y)]
    return chips, [2 * cx + cy for cx, cy in chips]


HBM = pl.BlockSpec(memory_space=pl.ANY)


def _allgather8(v, *, name):
    rows, cols = v.shape

    def body(v_ref, out_ref, send_sems, recv_sems):
        x, y, c = _place()
        me = 4 * x + 2 * y + c
        out_ref[me] = v_ref[...]
        copies = []
        for k in range(1, 8):
            bx, by, bc = (k >> 2) & 1, (k >> 1) & 1, k & 1
            px, py, pc = x ^ bx, y ^ by, c ^ bc
            cp = pltpu.make_async_remote_copy(
                src_ref=v_ref, dst_ref=out_ref.at[me], send_sem=send_sems.at[k - 1], recv_sem=recv_sems.at[k - 1],
                device_id=(px, py, pc), device_id_type=MESH)
            cp.start()
            copies.append((cp, 4 * px + 2 * py + pc))
        for k, (cp, peer) in enumerate(copies):
            pltpu.make_async_remote_copy(
                src_ref=v_ref, dst_ref=out_ref.at[peer], send_sem=send_sems.at[k], recv_sem=recv_sems.at[k],
                device_id=(x, y, c), device_id_type=MESH).wait_recv()
        for cp, _ in copies:
            cp.wait_send()

    return pl.pallas_call(
        body, name=name, out_shape=jax.ShapeDtypeStruct((8, rows, cols), v.dtype),
        in_specs=[pl.BlockSpec(memory_space=pltpu.VMEM)], out_specs=pl.BlockSpec(memory_space=pltpu.VMEM),
        scratch_shapes=[pltpu.SemaphoreType.DMA((7,)), pltpu.SemaphoreType.DMA((7,))],
        compiler_params=pltpu.CompilerParams(vmem_limit_bytes=VMEM_LIMIT),
    )(v)


def _allgather_shards(bufs, *, name):
    n = len(bufs)

    def body(*refs):
        outs = refs[n:2 * n]
        send_sems, recv_sems = refs[2 * n:]
        x, y, c = _place()
        me_chip = 2 * x + y
        sibling = (x, y, 1 - c)
        chips, chip_ids = _other_chips(x, y)

        def remote(k, j, blk, hf, to):
            hr = bufs[k].shape[1] // 2
            piece = outs[k].at[blk, pl.ds(pl.multiple_of(hf * hr, 16), hr), :]
            return pltpu.make_async_remote_copy(
                src_ref=piece, dst_ref=piece, send_sem=send_sems.at[6 * k + j], recv_sem=recv_sems.at[6 * k + j],
                device_id=to, device_id_type=MESH)

        sends = []
        for k in range(n):
            for j, chip in enumerate(chips):
                cp = remote(k, j, me_chip, c, (*chip, c))
                cp.start()
                sends.append(cp)
        for k in range(n):
            for j, chip in enumerate(chips):
                remote(k, j, chip_ids[j], c, (x, y, c)).wait_recv()
                cp = remote(k, 3 + j, chip_ids[j], c, sibling)
                cp.start()
                sends.append(cp)
        for k in range(n):
            for j in range(3):
                remote(k, 3 + j, chip_ids[j], 1 - c, (x, y, c)).wait_recv()
        for cp in sends:
            cp.wait_send()

    return list(pl.pallas_call(
        body, name=name, out_shape=[jax.ShapeDtypeStruct(b.shape, b.dtype) for b in bufs],
        in_specs=[HBM] * n, out_specs=[HBM] * n, input_output_aliases={k: k for k in range(n)},
        scratch_shapes=[pltpu.SemaphoreType.DMA((6 * n,)), pltpu.SemaphoreType.DMA((6 * n,))],
    )(*bufs))


def _pair_exchange(gs, *, name, anchor=None):
    n = len(gs)
    extra = [] if anchor is None else [anchor]
    n_in = n + len(extra)

    def body(*refs):
        ins, outs = refs[:n], refs[n_in:n_in + n]
        send_sems, recv_sems = refs[n_in + n:]
        x, y, c = _place()
        copies = []
        for k in range(n):
            hr = gs[k].shape[1] // 2
            src = ins[k].at[:, pl.ds(pl.multiple_of((1 - c) * hr, 16), hr), :]
            cp = pltpu.make_async_remote_copy(src_ref=src, dst_ref=outs[k], send_sem=send_sems.at[k], recv_sem=recv_sems.at[k],
                                              device_id=(x, y, 1 - c), device_id_type=MESH)
            cp.start()
            copies.append(cp)
        for cp in copies:
            cp.wait()

    return list(pl.pallas_call(
        body, name=name,
        out_shape=[jax.ShapeDtypeStruct((4, g.shape[1] // 2, g.shape[2]), g.dtype) for g in gs],
        in_specs=[HBM] * n_in, out_specs=[HBM] * n,
        scratch_shapes=[pltpu.SemaphoreType.DMA((n,)), pltpu.SemaphoreType.DMA((n,))],
    )(*gs, *extra))


def _pair_gather(bufs, *, name):
    n = len(bufs)

    def body(*refs):
        outs = refs[n:2 * n]
        send_sems, recv_sems = refs[2 * n:]
        x, y, c = _place()

        def remote(k, hf, to):
            return pltpu.make_async_remote_copy(src_ref=outs[k].at[hf], dst_ref=outs[k].at[hf], send_sem=send_sems.at[k],
                                                recv_sem=recv_sems.at[k], device_id=to, device_id_type=MESH)

        copies = [remote(k, c, (x, y, 1 - c)) for k in range(n)]
        for cp in copies:
            cp.start()
        for k, cp in enumerate(copies):
            cp.wait_send()
            remote(k, 1 - c, (x, y, c)).wait_recv()

    return list(pl.pallas_call(
        body, name=name, out_shape=[jax.ShapeDtypeStruct(b.shape, b.dtype) for b in bufs],
        in_specs=[HBM] * n, out_specs=[HBM] * n, input_output_aliases={k: k for k in range(n)},
        scratch_shapes=[pltpu.SemaphoreType.DMA((n,)), pltpu.SemaphoreType.DMA((n,))],
    )(*bufs))


HBM_SPEC = pl.BlockSpec(memory_space=pltpu.HBM)
SEM_SPEC = pl.BlockSpec(memory_space=pltpu.SEMAPHORE)
EFFECT = pltpu.SideEffectType.DATAFLOW_SIDE_EFFECTING
TOKEN = jax.ShapeDtypeStruct((SUBLANES, LANES), F32)


def _in_hbm(a):
    return pltpu.with_memory_space_constraint(a, pltpu.HBM)


def _ici_copies(srcs, dsts, send_sems, recv_sems, send):
    x, y, c = _place()
    me_chip = 2 * x + y
    chips, chip_ids = _other_chips(x, y)
    out = []
    for k, (src, dst) in enumerate(zip(srcs, dsts)):
        for j, chip in enumerate(chips):
            s_ref, d_ref = (src(k, me_chip, chip_ids[j], j), dst(k, me_chip, chip_ids[j], j))
            out.append(pltpu.make_async_remote_copy(
                src_ref=s_ref if send else d_ref, dst_ref=d_ref, send_sem=send_sems.at[3 * k + j],
                recv_sem=recv_sems.at[3 * k + j], device_id=(*chip, c) if send else (x, y, c), device_id_type=MESH))
    return out


def _half_rows(buf, hf):
    hr = buf.shape[1] // 2
    return pl.ds(pl.multiple_of(hf * hr, 16), hr)


def _ag_pieces(refs):
    c = lax.axis_index("c")
    src = [functools.partial(lambda k, me, other, j, r: r.at[me, _half_rows(r, c), :], r=r) for r in refs]
    dst_send = src
    dst_recv = [functools.partial(lambda k, me, other, j, r: r.at[other, _half_rows(r, c), :], r=r) for r in refs]
    return src, dst_send, dst_recv


def _ag_start(bufs, groups, *, name):
    n, ng = len(bufs), len(groups)

    def body(*refs):
        sems = refs[n:n + 2 * ng]
        thru = refs[n + 2 * ng:2 * n + 2 * ng]
        token = refs[-1]
        for g, ks in enumerate(groups):
            src, dst_send, _ = _ag_pieces([thru[k] for k in ks])
            for cp in _ici_copies(src, dst_send, sems[2 * g], sems[2 * g + 1], True):
                cp.start()
        token[...] = jnp.zeros_like(token)

    out_shape = tuple(pltpu.SemaphoreType.DMA((3 * len(ks),)) for ks in groups for _ in range(2))
    out_shape += tuple(pltpu.HBM(b.shape, b.dtype) for b in bufs) + (TOKEN,)
    res = pl.pallas_call(
        body, name=name, out_shape=out_shape, in_specs=(HBM_SPEC,) * n,
        out_specs=(SEM_SPEC,) * (2 * ng) + (HBM_SPEC,) * n + (pl.BlockSpec(memory_space=pltpu.VMEM),),
        input_output_aliases={k: 2 * ng + k for k in range(n)},
        compiler_params=pltpu.CompilerParams(has_side_effects=EFFECT),
    )(*[_in_hbm(b) for b in bufs])
    sems = [(res[2 * g], res[2 * g + 1]) for g in range(ng)]
    return sems, list(res[2 * ng:2 * ng + n]), res[-1]


def _ag_wait(bufs, send_sems, recv_sems, after, *, name):
    n = len(bufs)
    after = list(after)

    def body(*refs):
        ins = refs[:n]
        send, recv = refs[n], refs[n + 1]
        src, dst_send, dst_recv = _ag_pieces(ins)
        for cp in _ici_copies(src, dst_send, send, recv, True):
            cp.wait_send()
        for cp in _ici_copies(src, dst_recv, send, recv, False):
            cp.wait_recv()

    return list(pl.pallas_call(
        body, name=name, out_shape=tuple(pltpu.HBM(b.shape, b.dtype) for b in bufs),
        in_specs=(HBM_SPEC,) * n + (SEM_SPEC, SEM_SPEC) + (pl.BlockSpec(memory_space=pl.ANY),) * len(after),
        out_specs=(HBM_SPEC,) * n, input_output_aliases={k: k for k in range(n)},
        compiler_params=pltpu.CompilerParams(has_side_effects=EFFECT),
    )(*bufs, send_sems, recv_sems, *after))


def _ag_forward(bufs, *, name):
    n = len(bufs)

    def body(*refs):
        outs = refs[n:2 * n]
        send_sems, recv_sems = refs[2 * n:]
        x, y, c = _place()
        _, chip_ids = _other_chips(x, y)

        def remote(k, j, hf, to):
            piece = outs[k].at[chip_ids[j], _half_rows(outs[k], hf), :]
            return pltpu.make_async_remote_copy(src_ref=piece, dst_ref=piece, send_sem=send_sems.at[3 * k + j],
                                                recv_sem=recv_sems.at[3 * k + j], device_id=to, device_id_type=MESH)

        sends = [remote(k, j, c, (x, y, 1 - c)) for k in range(n) for j in range(3)]
        for cp in sends:
            cp.start()
        for k in range(n):
            for j in range(3):
                remote(k, j, 1 - c, (x, y, c)).wait_recv()
        for cp in sends:
            cp.wait_send()

    return list(pl.pallas_call(
        body, name=name, out_shape=[jax.ShapeDtypeStruct(b.shape, b.dtype) for b in bufs],
        in_specs=[HBM] * n, out_specs=[HBM] * n, input_output_aliases={k: k for k in range(n)},
        scratch_shapes=[pltpu.SemaphoreType.DMA((3 * n,)), pltpu.SemaphoreType.DMA((3 * n,))],
    )(*bufs))


def _rs_pieces(p_refs, l_refs):
    src = [functools.partial(lambda k, me, other, j, r: r.at[other], r=r) for r in p_refs]
    dst = [functools.partial(lambda k, me, other, j, r: r.at[j], r=r) for r in l_refs]
    return src, dst


def _rs_start(ps, *, name):
    n = len(ps)
    lands = [lax.empty((3,) + p.shape[1:], p.dtype) for p in ps]

    def body(*refs):
        send, recv = refs[2 * n], refs[2 * n + 1]
        p_thru = refs[2 * n + 2:3 * n + 2]
        l_thru = refs[3 * n + 2:4 * n + 2]
        token = refs[-1]
        src, dst = _rs_pieces(p_thru, l_thru)
        for cp in _ici_copies(src, dst, send, recv, True):
            cp.start()
        token[...] = jnp.zeros_like(token)

    out_shape = (pltpu.SemaphoreType.DMA((3 * n,)), pltpu.SemaphoreType.DMA((3 * n,)))
    out_shape += tuple(pltpu.HBM(a.shape, a.dtype) for a in list(ps) + lands) + (TOKEN,)
    res = pl.pallas_call(
        body, name=name, out_shape=out_shape, in_specs=(HBM_SPEC,) * (2 * n),
        out_specs=(SEM_SPEC, SEM_SPEC) + (HBM_SPEC,) * (2 * n) + (pl.BlockSpec(memory_space=pltpu.VMEM),),
        input_output_aliases={k: 2 + k for k in range(2 * n)},
        compiler_params=pltpu.CompilerParams(has_side_effects=EFFECT),
    )(*[_in_hbm(a) for a in list(ps) + lands])
    return (res[0], res[1]), list(res[2:2 + n]), list(res[2 + n:2 + 2 * n]), res[-1]


def _rs_wait(ps, lands, send_sems, recv_sems, after, *, name):
    n = len(ps)

    def body(*refs):
        p_in, l_in = refs[:n], refs[n:2 * n]
        send, recv = refs[2 * n], refs[2 * n + 1]
        src, dst = _rs_pieces(p_in, l_in)
        for cp in _ici_copies(src, dst, send, recv, True):
            cp.wait_send()
        for cp in _ici_copies(src, dst, send, recv, False):
            cp.wait_recv()

    res = pl.pallas_call(
        body, name=name, out_shape=tuple(pltpu.HBM(a.shape, a.dtype) for a in list(ps) + list(lands)),
        in_specs=(HBM_SPEC,) * (2 * n) + (SEM_SPEC, SEM_SPEC) + (pl.BlockSpec(memory_space=pl.ANY),) * len(after),
        out_specs=(HBM_SPEC,) * (2 * n), input_output_aliases={k: k for k in range(2 * n)},
        compiler_params=pltpu.CompilerParams(has_side_effects=EFFECT),
    )(*ps, *lands, send_sems, recv_sems, *after)
    return list(res[:n]), list(res[n:])


def _rs_begin(gs, tag, anchor=None):
    c = lax.axis_index("c")
    got = _pair_exchange(gs, name=f"rs_pair_exchange_{tag}", anchor=anchor)
    pair = [_pair_sum(g, r, c, name=f"rs_pair_sum_{tag}{k}") for k, (g, r) in enumerate(zip(gs, got))]
    sems, pair, lands, token = _rs_start(pair, name=f"rs_start_{tag}")
    return (sems, pair, lands), token


def _rs_end(handle, after, tag):
    x, y, c = _place()
    (send, recv), pair, lands = handle
    pair, lands = _rs_wait(pair, lands, send, recv, after, name=f"rs_wait_{tag}")
    halves = [_chip_sum(p, l, 2 * x + y, c, name=f"rs_chip_sum_{tag}{k}") for k, (p, l) in enumerate(zip(pair, lands))]
    full = _pair_gather(halves, name=f"rs_pair_gather_{tag}")
    return [f.reshape(2 * f.shape[1], f.shape[2]) for f in full]


def _to_segments(a):
    rows = a.shape[0]
    return a.reshape(N_SEG, rows // N_SEG, -1).transpose(1, 0, 2).reshape(rows, -1)


def _from_segments(a):
    rows = a.shape[0]
    return a.reshape(rows // N_SEG, N_SEG, -1).transpose(1, 0, 2).reshape(rows, -1)


def _rope_tables(L):
    t = jnp.arange(L, dtype=jnp.int32)
    row = (t // GRID_W).astype(F32)
    col = (t % GRID_W).astype(F32)
    n_freq = QK_ROPE // 4
    inv = ROPE_BASE ** (-jnp.arange(n_freq, dtype=F32) / n_freq)
    a0, a1 = row[:, None] * inv, col[:, None] * inv
    z = jnp.zeros((L, LANES - QK_ROPE), F32)
    cos = jnp.concatenate([jnp.cos(a0), jnp.cos(a0), jnp.cos(a1), jnp.cos(a1), z], axis=1)
    sin = jnp.concatenate([-jnp.sin(a0), jnp.sin(a0), -jnp.sin(a1), jnp.sin(a1), z], axis=1)
    return _to_segments(cos), _to_segments(sin)


def _col_blocks(w, nblk):
    r, c = w.shape
    return w.reshape(r, nblk, c // nblk).transpose(1, 0, 2)


def _from_col_blocks(w4):
    nblk, r, c = w4.shape
    return w4.transpose(1, 0, 2).reshape(r, nblk * c)


def _s5_discretize(a_re, a_im, log_dt, b_re, b_im):
    dt = jnp.exp(log_dt)[:, None]
    mag = jnp.exp(a_re * dt)
    ab_re, ab_im = mag * jnp.cos(a_im * dt), mag * jnp.sin(a_im * dt)
    den = a_re * a_re + a_im * a_im
    nr, ni = ab_re - 1.0, ab_im
    co_re = (nr * a_re + ni * a_im) / den
    co_im = (ni * a_re - nr * a_im) / den
    bb_re = co_re[..., None] * b_re - co_im[..., None] * b_im
    bb_im = co_re[..., None] * b_im + co_im[..., None] * b_re
    return ab_re, ab_im, bb_re, bb_im


def _diag_blocks_in(bb, gpb):
    G, N, P = bb.shape
    t = jnp.tile(jnp.swapaxes(bb, 1, 2).reshape(G // gpb, gpb * P, N), (1, 1, gpb))
    row = lax.broadcasted_iota(jnp.int32, t.shape, 1) // P
    col = lax.broadcasted_iota(jnp.int32, t.shape, 2) // N
    return jnp.where(row == col, t, 0.0)


def _diag_blocks_out(cc, gpb):
    G, P, N = cc.shape
    t = jnp.tile(jnp.swapaxes(cc, 1, 2).reshape(G // gpb, gpb * N, P), (1, 1, gpb))
    row = lax.broadcasted_iota(jnp.int32, t.shape, 1) // N
    col = lax.broadcasted_iota(jnp.int32, t.shape, 2) // P
    return jnp.where(row == col, t, 0.0)


def _tr(ws):
    return [jnp.swapaxes(w, 1, 2) for w in ws]


WEIGHTS = ['c_ctx', 'w_mod', 'b_mod', 'norm1', 'norm2', 'w_in', 's5_a_re', 's5_a_im', 's5_log_dt', 's5_b_re', 's5_b_im',
           's5_c_re', 's5_c_im', 's5_d', 'w_glu', 'q_norm', 'kv_norm', 'w_uq', 'w_ukv', 'w_mla_o', 'w_out', 'w_ffn_in',
           'w_ffn_out', 'norm_f']
AG_GROUPS = [['w_in'], ['w_glu', 'w_uq', 'w_ukv', 'w_mla_o', 'w_out'], ['w_ffn_in', 'w_ffn_out']]
SMALL = ['norm1', 'norm2', 's5_a_re', 's5_a_im', 's5_log_dt', 's5_b_re', 's5_b_im', 's5_c_re', 's5_c_im', 's5_d',
         'q_norm', 'kv_norm', 'norm_f']


def _pad_rows(a, rows):
    return jnp.concatenate([a, jnp.zeros((rows - a.shape[0],) + a.shape[1:], a.dtype)], axis=0)


def _pack(vals, width, rows):
    flat = jnp.concatenate([v.reshape(-1).astype(F32) for v in vals])
    flat = jnp.concatenate([flat, jnp.zeros((rows * width - flat.shape[0],), F32)])
    return flat.reshape(rows, width)


def _unpack(buf, like):
    flat = buf.reshape(-1)
    out, pos = [], 0
    for v in like:
        out.append(flat[pos:pos + v.size].reshape(v.shape))
        pos += v.size
    return out


def _step(x, c, ctx, loss_target, w, m, v):
    px, py, pc = _place()
    me = 4 * px + 2 * py + pc
    me_chip = 2 * px + py
    L, D = x.shape[1], x.shape[2]
    Lc = ctx.shape[1]
    T = L + Lc
    SW = D // 2
    G = SW // S5_GROUP
    C = G * S5_STATE
    H = MLA_HEADS
    q_rank = w['q_norm'].shape[1]
    kv_rank = w['kv_norm'].shape[1]
    d_ff = w['w_ffn_out'].shape[1] * 4
    wa_used = SW + q_rank + kv_rank + QK_ROPE
    WA = -(-(SW + q_rank + kv_rank + LANES) // 512) * 512

    c_rows = _pad_rows(c.astype(F32), SUBLANES)
    c_all = _allgather8(c_rows, name="ag_cond")[:, 0, :]
    cond = jnp.concatenate([c_all, w['c_ctx'].reshape(1, D)], axis=0)
    cond = _pad_rows(cond, 16)
    (act,) = _rw(lambda t: (jax.nn.silu(t),), [cond], [], [F32], name="cond_silu")
    w_mod, cs_mod = w['w_mod'][0], w['w_mod'].shape[2]
    mod_part = _mm(act, w_mod, out_dtype=F32, name="mod_fwd")
    mod_all = _allgather8(mod_part, name="ag_mod")
    mod_full = jnp.concatenate([mod_all[0], mod_all[2], mod_all[4], mod_all[6]], axis=1) + w['b_mod']
    m_lat = lax.dynamic_slice_in_dim(mod_full, me, 1, axis=0).reshape(6, D)
    m_ctx = mod_full[8].reshape(6, D)
    sh1, sc1, g1, sh2, sc2, g2 = (m_lat[i:i + 1] for i in range(6))
    csh1, csc1 = m_ctx[0:1], m_ctx[1:2]

    names = [nme for grp in AG_GROUPS for nme in grp]
    bufs = [_into_slot(w[nme][0], me_chip, 4, BF16, name=f"cast_{nme}") for nme in names]
    group_idx, pos = [], 0
    for grp in AG_GROUPS:
        group_idx.append(list(range(pos, pos + len(grp))))
        pos += len(grp)
    ag_sems, bufs, ag_token = _ag_start(bufs, group_idx, name="ag_start")
    gathered = {}

    def arrive(g, after):
        got = _ag_wait([bufs[k] for k in group_idx[g]], *ag_sems[g], after, name=f"ag_wait_{g}")
        gathered.update(zip(AG_GROUPS[g], _ag_forward(got, name=f"ag_forward_{g}")))

    xs = _to_segments(x[0])
    cs = _to_segments(ctx[0])
    tgt = _to_segments(loss_target[0])
    cos, sin = _rope_tables(L)
    n1, n2, nf = w['norm1'], w['norm2'], w['norm_f'].reshape(1, D)
    qg, kvg = w['q_norm'], w['kv_norm']

    (xn_lat,) = _rw(_f_norm_mod, [xs], [n1 + ag_token[0, 0], sc1, sh1], [BF16], name="norm1_lat")
    (xn_ctx,) = _rw(_f_norm_mod, [cs], [n1, csc1, csh1], [BF16], name="norm1_ctx")
    xn = jnp.concatenate([xn_lat, xn_ctx], axis=0)

    gpb = min(S5_BLOCK_GROUPS, G)
    gpo = min(8, G)
    d_skip = w['s5_d'][0].reshape(1, SW)
    disc, vjp_disc, w_b, w_c = [], [], [], []
    for d in range(2):
        prm = (w['s5_a_re'][0, d], w['s5_a_im'][0, d], w['s5_log_dt'][0, d], w['s5_b_re'][0, d], w['s5_b_im'][0, d])

        def prep(a_re, a_im, log_dt, b_re, b_im):
            ab_re, ab_im, bb_re, bb_im = _s5_discretize(a_re, a_im, log_dt, b_re, b_im)
            return ab_re.reshape(1, C), ab_im.reshape(1, C), _diag_blocks_in(bb_re, gpb), _diag_blocks_in(bb_im, gpb)

        out, vj = jax.vjp(prep, *prm)
        disc.append(out)
        vjp_disc.append(vj)
        w_b += [out[2], out[3]]
        w_c += [_diag_blocks_out(w['s5_c_re'][0, d], gpo), -_diag_blocks_out(w['s5_c_im'][0, d], gpo)]
    nb_in = G // gpb
    nb_out = G // gpo

    arrive(0, [xn, tgt] + w_b + w_c)
    w_in = _from_col_blocks(gathered['w_in'])
    w_a = jnp.concatenate([w_in[:, :wa_used], jnp.zeros((D, WA - wa_used), BF16)], axis=1)
    w_g = w_in[:, wa_used:]
    ha = _mm(xn, w_a, out_dtype=F32, name="in_proj")
    ha_lat, ha_ctx = ha[:L], ha[L:]
    gt = _mm(xn_lat, w_g, out_dtype=F32, name="in_gates")
    f_post_lat = _make_f_post_in(SW, q_rank, kv_rank, True)
    f_post_ctx = _make_f_post_in(SW, q_rank, kv_rank, False)
    u_lat, cqn, ckvn_lat, kr_lat = _rw(f_post_lat, [ha_lat, cos, sin], [qg, kvg], [F32, BF16, BF16, BF16], name="post_in_lat")
    u_ctx, ckvn_ctx, kr_ctx = _rw(f_post_ctx, [ha_ctx], [kvg], [F32, BF16, BF16], name="post_in_ctx")

    bu_lat = _bd_fanout(u_lat, w_b, name="s5_bu_lat")
    bu_ctx = _bd_fanout(u_ctx, w_b, name="s5_bu_ctx")
    zero = jnp.zeros((1, C), F32)
    h_lat, h_ctx, hT_ctx = [], [], []
    for d, rev in enumerate((False, True)):
        lr, li = disc[d][0], disc[d][1]
        hcr, hci, tr, ti = _s5_scan(bu_ctx[2 * d], bu_ctx[2 * d + 1], lr, li, zero, zero, zero, zero, reverse=rev,
                                    name=f"s5_scan_ctx_{d}")
        hlr, hli, _, _ = _s5_scan(bu_lat[2 * d], bu_lat[2 * d + 1], lr, li, tr, ti, zero, zero, reverse=rev,
                                  name=f"s5_scan_lat_{d}")
        h_ctx += [hcr, hci]
        h_lat += [hlr, hli]
        hT_ctx += [tr, ti]
    r5 = _bd_fanin(h_lat, w_c, name="s5_readout")
    (z,) = _rw(_f_s5post, [u_lat, r5], [d_skip], [BF16], name="s5_post")

    arrive(1, [z])
    w_glu, w_ukv, w_mla_o = (gathered[nme] for nme in ('w_glu', 'w_ukv', 'w_mla_o'))
    w_out = gathered['w_out'].reshape(D, D)
    uq3 = _from_col_blocks(gathered['w_uq']).reshape(q_rank, H, QK_NOPE + QK_ROPE)
    w_q2 = jnp.concatenate([uq3, jnp.zeros((q_rank, H, LANES - QK_ROPE), BF16)], axis=2).reshape(q_rank, H * 2 * LANES)
    q2 = _mm(cqn, w_q2, out_dtype=F32, name="q_up")
    (qq,) = _rw(_f_qpost, [q2, cos, sin], [], [BF16], name="q_rope")
    kvn = jnp.concatenate([ckvn_lat, ckvn_ctx], axis=0)
    kr_all = jnp.concatenate([kr_lat, kr_ctx], axis=0)
    kv = _mm(kvn, w_ukv, b_shards=4, out_dtype=BF16, name="kv_up")
    o = _attn_fwd(qq, kv, kr_all, name="attn_fwd")

    ab = _mm(z, w_glu, b_shards=4, out_dtype=F32, name="glu_proj")
    bm = _mm(o, w_mla_o, b_shards=4, out_dtype=F32, name="mla_out")
    (mix,) = _rw(_f_merge, [ab, bm, gt], [], [BF16], name="merge")
    out1 = _mm(mix, w_out, out_dtype=F32, name="out_proj")
    x1, xn2 = _rw(_f_resid_norm, [xs, out1], [g1, n2, sc2, sh2], [F32, BF16], name="resid_norm2")
    arrive(2, [xn2])
    w_ffn_in = gathered['w_ffn_in']
    w_ffn_out = gathered['w_ffn_out'].reshape(d_ff, D)
    ab2 = _mm(xn2, w_ffn_in, b_shards=4, out_dtype=F32, name="ffn_in")
    (hmid,) = _rw(_f_swiglu, [ab2], [], [BF16], name="ffn_act")
    f2 = _mm(hmid, w_ffn_out, out_dtype=F32, name="ffn_out")
    (row_loss,) = _rw(_f_final, [x1, f2, tgt], [g2, nf], [F32], name="final_loss")
    loss = lax.psum(jnp.sum(row_loss), ("x", "y", "c"))

    ones = jnp.ones((L, 1), F32)
    (dx1_a, df2), (dg2, dnf) = _rw_vjp(_f_final, [x1, f2, tgt], [g2, nf], [[ones]], [True, True, False], [True, True],
                                       [F32, BF16], name="final_loss_bwd")
    dhmid = _mm(df2, w_ffn_out, tb=True, out_dtype=F32, name="ffn_out_dx")
    gw_ffn_out = _mm(hmid, df2, ta=True, out_dtype=BF16, name="ffn_out_dw")
    (dab2,), _ = _rw_vjp(_f_swiglu, [ab2], [], [[dhmid]], [True], [], [BF16], name="ffn_act_bwd")
    dxn2 = _mm(dab2, w_ffn_in, tb=True, b_shards=4, out_dtype=F32, name="ffn_in_dx")
    gw_ffn_in = _mm(xn2, dab2, ta=True, out_shards=4, out_dtype=BF16, name="ffn_in_dw")
    rs_ffn, tok = _rs_begin([gw_ffn_out.reshape(4, -1, D), gw_ffn_in], "ffn")
    (dx_a, dout1), (dg1, dn2, dsc2, dsh2) = _rw_vjp(
        _f_resid_norm, [xs, out1], [g1, n2 + tok[0, 0], sc2, sh2], [[dx1_a], [dxn2]], [True, True], [True] * 4, [F32, BF16],
        name="resid_norm2_bwd")
    dmix = _mm(dout1, w_out, tb=True, out_dtype=F32, name="out_proj_dx")
    gw_out = _mm(mix, dout1, ta=True, out_dtype=BF16, name="out_proj_dw")
    (dab, dbm, dgt), _ = _rw_vjp(_f_merge, [ab, bm, gt], [], [[dmix]], [True] * 3, [], [BF16] * 3, name="merge_bwd")
    dz = _mm(dab, w_glu, tb=True, b_shards=4, out_dtype=F32, name="glu_proj_dx")
    gw_glu = _mm(z, dab, ta=True, out_shards=4, out_dtype=BF16, name="glu_proj_dw")
    do = _mm(dbm, w_mla_o, tb=True, b_shards=4, out_dtype=BF16, name="mla_out_dx")
    gw_mla_o = _mm(o, dbm, ta=True, out_shards=4, out_dtype=BF16, name="mla_out_dw")
    dxn_g = _mm(dgt, w_g, tb=True, out_dtype=F32, name="in_gates_dx")
    gw_g = _mm(xn_lat, dgt, ta=True, out_dtype=BF16, name="in_gates_dw")

    (du_a, dr5), (dd_skip,) = _rw_vjp(_f_s5post, [u_lat, r5], [d_skip], [[dz]], [True, True], [True], [F32, F32],
                                      name="s5_post_bwd")
    dh_lat = _bd_fanout(dr5, _tr(w_c), name="s5_readout_dx")
    dw_c = _bd_dw(h_lat, [dr5] * 4, nb_out, name="s5_readout_dw")
    zeros_ctx = jnp.zeros((Lc, C), BF16)
    mu_lat, mu_ctx, dlam = [], [], []
    for d, rev in enumerate((False, True)):
        lr, li = disc[d][0], disc[d][1]
        mlr, mli, fr, fi = _s5_scan(dh_lat[2 * d], dh_lat[2 * d + 1], lr, -li, zero, zero, zero, zero, reverse=not rev,
                                    name=f"s5_adj_lat_{d}")
        dh0r, dh0i = _cmul(lr, -li, fr, fi)
        mcr, mci, _, _ = _s5_scan(zeros_ctx, zeros_ctx, lr, -li, zero, zero, dh0r, dh0i, reverse=not rev,
                                  name=f"s5_adj_ctx_{d}")
        dl_lat = _s5_dlam(mlr, mli, h_lat[2 * d], h_lat[2 * d + 1], hT_ctx[2 * d], hT_ctx[2 * d + 1], reverse=rev,
                          name=f"s5_dlam_lat_{d}")
        dl_ctx = _s5_dlam(mcr, mci, h_ctx[2 * d], h_ctx[2 * d + 1], zero, zero, reverse=rev, name=f"s5_dlam_ctx_{d}")
        mu_lat += [mlr, mli]
        mu_ctx += [mcr, mci]
        dlam.append((dl_lat[0] + dl_ctx[0], dl_lat[1] + dl_ctx[1]))
    du_b = _bd_fanin(mu_lat, _tr(w_b), name="s5_bu_lat_dx")
    du_ctx = _bd_fanin(mu_ctx, _tr(w_b), name="s5_bu_ctx_dx")
    dw_b_lat = _bd_dw([u_lat] * 4, mu_lat, nb_in, name="s5_bu_lat_dw")
    dw_b_ctx = _bd_dw([u_ctx] * 4, mu_ctx, nb_in, name="s5_bu_ctx_dw")
    g_s5 = {}
    for d in range(2):
        ct = (dlam[d][0], dlam[d][1], dw_b_lat[2 * d] + dw_b_ctx[2 * d], dw_b_lat[2 * d + 1] + dw_b_ctx[2 * d + 1])
        ga_re, ga_im, gdt, gb_re, gb_im = vjp_disc[d](ct)
        _, vj_c = jax.vjp(lambda cr, ci: (_diag_blocks_out(cr, gpo), -_diag_blocks_out(ci, gpo)),
                          w['s5_c_re'][0, d], w['s5_c_im'][0, d])
        gc_re, gc_im = vj_c((dw_c[2 * d], dw_c[2 * d + 1]))
        for nme, val in (('s5_a_re', ga_re), ('s5_a_im', ga_im), ('s5_log_dt', gdt), ('s5_b_re', gb_re),
                         ('s5_b_im', gb_im), ('s5_c_re', gc_re), ('s5_c_im', gc_im)):
            g_s5.setdefault(nme, []).append(val)
    g_small = {nme: jnp.stack(vals)[None] for nme, vals in g_s5.items()}
    g_small['s5_d'] = dd_skip.reshape(w['s5_d'].shape)

    dqq, dkv, dkr = _attn_bwd(qq, kv, kr_all, do, name="attn_bwd")
    (dq2,), _ = _rw_vjp(_f_qpost, [q2, cos, sin], [], [[dqq]], [True, False, False], [], [BF16], name="q_rope_bwd")
    dcqn = _mm(dq2, w_q2, tb=True, out_dtype=F32, name="q_up_dx")
    gw_q2 = _mm(cqn, dq2, ta=True, out_dtype=BF16, name="q_up_dw")
    dckvn = _mm(dkv, w_ukv, tb=True, b_shards=4, out_dtype=F32, name="kv_up_dx")
    gw_ukv = _mm(kvn, dkv, ta=True, out_shards=4, out_dtype=BF16, name="kv_up_dw")
    gw_uq = gw_q2.reshape(q_rank, H, 2 * LANES)[:, :, :QK_NOPE + QK_ROPE].reshape(q_rank, H * (QK_NOPE + QK_ROPE))
    rs_mix, tok = _rs_begin([gw_out.reshape(4, -1, D), gw_glu, gw_mla_o, _col_blocks(gw_uq, 4), gw_ukv], "mix")

    (dha_lat,), (dqg, dkvg_lat) = _rw_vjp(
        f_post_lat, [ha_lat, cos, sin], [qg, kvg + tok[0, 0]], [[du_a, du_b], [dcqn], [dckvn[:L]], [dkr[:L]]],
        [True, False, False], [True, True], [BF16], name="post_in_lat_bwd")
    (dha_ctx,), (dkvg_ctx,) = _rw_vjp(f_post_ctx, [ha_ctx], [kvg], [[du_ctx], [dckvn[L:]], [dkr[L:]]], [True], [True],
                                      [BF16], name="post_in_ctx_bwd")
    dha = jnp.concatenate([dha_lat, dha_ctx], axis=0)
    dxn = _mm(dha, w_a, tb=True, out_dtype=F32, name="in_proj_dx")
    gw_a = _mm(xn, dha, ta=True, out_dtype=BF16, name="in_proj_dw")
    (dx_seg,), (dn1_lat, dsc1, dsh1) = _rw_vjp(
        _f_norm_mod_keep, [xs], [n1, sc1, sh1], [[dxn[:L], dxn_g], [dx_a]], [True], [True] * 3, [F32], name="norm1_lat_bwd")
    _, (dn1_ctx, dcsc1, dcsh1) = _rw_vjp(_f_norm_mod, [cs], [n1, csc1, csh1], [[dxn[L:]]], [False], [True] * 3, [],
                                         name="norm1_ctx_bwd")
    grad_x = _from_segments(dx_seg)[None]
    g_small.update(norm1=dn1_lat + dn1_ctx, norm2=dn2, q_norm=dqg, kv_norm=dkvg_lat + dkvg_ctx, norm_f=dnf.reshape(D))

    zD = jnp.zeros((1, D), F32)
    dm = jnp.concatenate([
        jnp.concatenate([dsh1, dsc1, dg1, dsh2, dsc2, dg2], axis=1),
        jnp.concatenate([dcsh1, dcsc1, zD, zD, zD, zD], axis=1),
    ], axis=0)
    dm_all = _allgather8(_pad_rows(dm, SUBLANES), name="ag_dmod")
    dm_ctx = dm_all[0, 1]
    for k in range(1, 8):
        dm_ctx = dm_ctx + dm_all[k, 1]
    dmod = _pad_rows(jnp.concatenate([dm_all[:, 0, :], dm_ctx[None]], axis=0), 16)
    g_b_mod = jnp.sum(dmod, axis=0, keepdims=True)
    dmod_mine = lax.dynamic_slice_in_dim(dmod, me_chip * cs_mod, cs_mod, axis=1)
    g_w_mod = _mm(act, dmod_mine, ta=True, out_dtype=F32, name="mod_dw")
    dact_part = _mm(dmod_mine, w_mod, tb=True, out_dtype=F32, name="mod_dx")
    dact_all = _allgather8(dact_part, name="ag_dact")
    dact = dact_all[0] + dact_all[2] + dact_all[4] + dact_all[6]
    (dcond_rows,), _ = _rw_vjp(lambda t: (jax.nn.silu(t),), [cond], [], [[dact]], [True], [], [F32], name="cond_silu_bwd")
    g_c_ctx = dcond_rows[8]

    gw_in = jnp.concatenate([gw_a[:, :wa_used], gw_g], axis=1)
    small_vals = [g_small[nme] for nme in SMALL]
    n_small = sum(val.size for val in small_vals)
    small_rows = -(-n_small // (LANES * 4 * 32)) * 32
    rs_in, tok = _rs_begin([_col_blocks(gw_in, 4), _pack(small_vals, LANES, 4 * small_rows).reshape(4, small_rows, LANES)],
                           "in", anchor=g_c_ctx)

    grads, delta, new_m, new_v = {}, {}, {}, {}

    def update(nme, red, anchor=None):
        res = _adamw(w[nme][0], red, m[nme][0], v[nme][0], name=f"adamw_{nme}", anchor=anchor)
        grads[nme], delta[nme], new_m[nme], new_v[nme] = (r.reshape(w[nme].shape) for r in res)
        return res[1]

    after = [update('w_mod', g_w_mod, anchor=tok)]
    for handle, tag, members in ((rs_ffn, "ffn", ['w_ffn_out', 'w_ffn_in']),
                                 (rs_mix, "mix", ['w_out', 'w_glu', 'w_mla_o', 'w_uq', 'w_ukv']),
                                 (rs_in, "in", ['w_in'])):
        reduced = _rs_end(handle, after, tag)
        for nme, red in zip(members, reduced):
            after.append(update(nme, red))
    small_mine = reduced[-1]
    small_buf = _into_slot(small_mine, me_chip, 4, F32, name="small_grads_slot")
    small_all = _allgather_shards([small_buf], name="ag_small_grads")[0].reshape(4 * small_rows, LANES)
    g_small_red = dict(zip(SMALL, _unpack(small_all, [w[nme] for nme in SMALL])))
    rest = SMALL + ['c_ctx', 'b_mod']
    g_rest = dict(g_small_red, c_ctx=g_c_ctx, b_mod=g_b_mod)
    rows_rest = -(-sum(w[nme].size for nme in rest) // (LANES * 16)) * 16
    packed = [_pack([src[nme] for nme in rest], LANES, rows_rest) for src in (w, g_rest, m, v)]
    res = _adamw(*packed, name="adamw_small")
    for dst, buf in zip((grads, delta, new_m, new_v), res):
        dst.update(zip(rest, _unpack(buf, [w[nme] for nme in rest])))
    return (loss, grad_x, *[grads[nme] for nme in WEIGHTS], *[delta[nme] for nme in WEIGHTS],
            *[new_m[nme] for nme in WEIGHTS], *[new_v[nme] for nme in WEIGHTS])


def kernel(x, c, ctx, c_ctx, w_mod, b_mod, norm1, norm2, w_in, s5_a_re, s5_a_im, s5_log_dt, s5_b_re, s5_b_im, s5_c_re, s5_c_im, s5_d, w_glu, q_norm, kv_norm, w_uq, w_ukv, w_mla_o, w_out, w_ffn_in, w_ffn_out, norm_f, loss_target, m_c_ctx, m_w_mod, m_b_mod, m_norm1, m_norm2, m_w_in, m_s5_a_re, m_s5_a_im, m_s5_log_dt, m_s5_b_re, m_s5_b_im, m_s5_c_re, m_s5_c_im, m_s5_d, m_w_glu, m_q_norm, m_kv_norm, m_w_uq, m_w_ukv, m_w_mla_o, m_w_out, m_w_ffn_in, m_w_ffn_out, m_norm_f, v_c_ctx, v_w_mod, v_b_mod, v_norm1, v_norm2, v_w_in, v_s5_a_re, v_s5_a_im, v_s5_log_dt, v_s5_b_re, v_s5_b_im, v_s5_c_re, v_s5_c_im, v_s5_d, v_w_glu, v_q_norm, v_kv_norm, v_w_uq, v_w_ukv, v_w_mla_o, v_w_out, v_w_ffn_in, v_w_ffn_out, v_norm_f):
    w = dict(c_ctx=c_ctx, w_mod=w_mod, b_mod=b_mod, norm1=norm1, norm2=norm2, w_in=w_in, s5_a_re=s5_a_re, s5_a_im=s5_a_im,
             s5_log_dt=s5_log_dt, s5_b_re=s5_b_re, s5_b_im=s5_b_im, s5_c_re=s5_c_re, s5_c_im=s5_c_im, s5_d=s5_d, w_glu=w_glu,
             q_norm=q_norm, kv_norm=kv_norm, w_uq=w_uq, w_ukv=w_ukv, w_mla_o=w_mla_o, w_out=w_out, w_ffn_in=w_ffn_in,
             w_ffn_out=w_ffn_out, norm_f=norm_f)
    m = dict(c_ctx=m_c_ctx, w_mod=m_w_mod, b_mod=m_b_mod, norm1=m_norm1, norm2=m_norm2, w_in=m_w_in, s5_a_re=m_s5_a_re,
             s5_a_im=m_s5_a_im, s5_log_dt=m_s5_log_dt, s5_b_re=m_s5_b_re, s5_b_im=m_s5_b_im, s5_c_re=m_s5_c_re,
             s5_c_im=m_s5_c_im, s5_d=m_s5_d, w_glu=m_w_glu, q_norm=m_q_norm, kv_norm=m_kv_norm, w_uq=m_w_uq, w_ukv=m_w_ukv,
             w_mla_o=m_w_mla_o, w_out=m_w_out, w_ffn_in=m_w_ffn_in, w_ffn_out=m_w_ffn_out, norm_f=m_norm_f)
    v = dict(c_ctx=v_c_ctx, w_mod=v_w_mod, b_mod=v_b_mod, norm1=v_norm1, norm2=v_norm2, w_in=v_w_in, s5_a_re=v_s5_a_re,
             s5_a_im=v_s5_a_im, s5_log_dt=v_s5_log_dt, s5_b_re=v_s5_b_re, s5_b_im=v_s5_b_im, s5_c_re=v_s5_c_re,
             s5_c_im=v_s5_c_im, s5_d=v_s5_d, w_glu=v_w_glu, q_norm=v_q_norm, kv_norm=v_kv_norm, w_uq=v_w_uq, w_ukv=v_w_ukv,
             w_mla_o=v_w_mla_o, w_out=v_w_out, w_ffn_in=v_w_ffn_in, w_ffn_out=v_w_ffn_out, norm_f=v_norm_f)
    return _step(x, c, ctx, loss_target, w, m, v)
```

```python
import functools
import math

import jax
import jax.numpy as jnp
from jax import lax
from jax.experimental import pallas as pl
from jax.experimental.pallas import tpu as pltpu

F32 = jnp.float32
BF16 = jnp.bfloat16

EPS = 1e-6
GRID_W = 64
S5_GROUP = 16
S5_STATE = 64
MLA_HEADS = 8
QK_NOPE = 128
QK_ROPE = 64
V_DIM = 128
ROPE_BASE = 10000.0
ATTN_SCALE = (QK_NOPE + QK_ROPE) ** -0.5
ADAM_LR = 0.001
ADAM_B1 = 0.9
ADAM_B2 = 0.999
ADAM_EPS = 1e-08
ADAM_WD = 0.01
ADAM_STEP = 10

SUBLANES = 8
LANES = 128
V7X_VMEM_BYTES = 64 * 1024 * 1024
VMEM_LIMIT = (V7X_VMEM_BYTES * 7) // 8
N_SEG = 2 * SUBLANES
S5_BLOCK_GROUPS = 8
MESH = pl.DeviceIdType.MESH


def _pick(n, target, mult):
    best = None
    d = mult
    while d <= min(n, target):
        if n % d == 0:
            best = d
        d += mult
    return n if best is None else best


def _cparams(sem=None):
    return pltpu.CompilerParams(dimension_semantics=sem, vmem_limit_bytes=VMEM_LIMIT)


MM_VMEM_BUDGET = (V7X_VMEM_BYTES * 5) // 8


def _mm(a, b, *, ta=False, tb=False, out_dtype=F32, name, b_shards=1, out_shards=1):
    if ta:
        K, M = a.shape
    else:
        M, K = a.shape
    if tb:
        N, K2 = b.shape[-2], b.shape[-1] * b_shards
    else:
        K2, N = b.shape[-2], b.shape[-1] * b_shards
    assert K == K2, (a.shape, b.shape, ta, tb)
    n_unit = N // max(out_shards, 1 if tb else b_shards)
    k_unit = K // (b_shards if tb else 1)
    tn = _pick(n_unit, 1024, LANES)
    tm = _pick(M, 1024 if tn >= 512 else 2048, LANES if ta else 16)
    sa, sb, so = a.dtype.itemsize, b.dtype.itemsize, jnp.dtype(out_dtype).itemsize
    k_mult = LANES if (not ta or tb) else 16
    tk = k_mult if k_unit % k_mult == 0 else k_unit
    for cand in range(k_mult, k_unit + 1, k_mult):
        if k_unit % cand == 0 and 2 * cand * (tm * sa + tn * sb) + tm * tn * (4 + 2 * so) <= MM_VMEM_BUDGET:
            tk = cand
    nk = K // tk
    dims = (((0 if ta else 1,), (1 if tb else 0,)), ((), ()))

    def body(a_ref, b_ref, o_ref, *scratch):
        part = lax.dot_general(a_ref[...].astype(BF16), b_ref[...].astype(BF16), dims, preferred_element_type=F32)
        if nk == 1:
            o_ref[...] = part.astype(o_ref.dtype)
            return
        acc_ref, = scratch
        k = pl.program_id(2)

        @pl.when(k == 0)
        def _():
            acc_ref[...] = part

        @pl.when(k > 0)
        def _():
            acc_ref[...] += part

        @pl.when(k == nk - 1)
        def _():
            o_ref[...] = acc_ref[...].astype(o_ref.dtype)

    a_spec = pl.BlockSpec((tk, tm), lambda i, j, k: (k, i)) if ta else pl.BlockSpec((tm, tk), lambda i, j, k: (i, k))
    if b_shards == 1:
        b_spec = pl.BlockSpec((tn, tk), lambda i, j, k: (j, k)) if tb else pl.BlockSpec((tk, tn), lambda i, j, k: (k, j))
    elif tb:
        kpb = k_unit // tk
        b_spec = pl.BlockSpec((None, tn, tk), lambda i, j, k: (k // kpb, j, k % kpb))
    else:
        npb = n_unit // tn
        b_spec = pl.BlockSpec((None, tk, tn), lambda i, j, k: (j // npb, k, j % npb))
    if out_shards == 1:
        out_spec = pl.BlockSpec((tm, tn), lambda i, j, k: (i, j))
        out_shape = jax.ShapeDtypeStruct((M, N), out_dtype)
    else:
        opb = n_unit // tn
        out_spec = pl.BlockSpec((None, tm, tn), lambda i, j, k: (j // opb, i, j % opb))
        out_shape = jax.ShapeDtypeStruct((out_shards, M, N // out_shards), out_dtype)
    return pl.pallas_call(
        body, name=name, grid=(M // tm, N // tn, nk),
        in_specs=[a_spec, b_spec], out_specs=out_spec, out_shape=out_shape,
        scratch_shapes=[pltpu.VMEM((tm, tn), F32)] if nk > 1 else [],
        compiler_params=_cparams(("parallel", "parallel", "arbitrary")),
    )(a, b)


def _row_tile(tiled, extra_bytes=0):
    rows = tiled[0].shape[0]
    per_row = sum(a.shape[1] * 4 for a in tiled) + extra_bytes
    target = max(SUBLANES, (6 * 1024 * 1024) // max(per_row, 1))
    return _pick(rows, min(target, 512), 16)


def _rw(f, tiled, bcast, out_dtypes, *, name, anchor=None):
    nt, nb = len(tiled), len(bcast)
    rows = tiled[0].shape[0]
    outs_aval = jax.eval_shape(f, *[jax.ShapeDtypeStruct((16, a.shape[1]), F32) for a in tiled],
                               *[jax.ShapeDtypeStruct(b.shape, F32) for b in bcast])
    widths = [o.shape[1] for o in outs_aval]
    tm = _row_tile(tiled, sum(w * 4 for w in widths))

    extra = [] if anchor is None else [anchor]
    n_in = nt + nb + len(extra)

    def body(*refs):
        tin = [r[...].astype(F32) for r in refs[:nt]]
        bin_ = [r[...].astype(F32) for r in refs[nt:nt + nb]]
        outs = f(*tin, *bin_)
        for o_ref, o in zip(refs[n_in:], outs):
            o_ref[...] = o.astype(o_ref.dtype)

    in_specs = [pl.BlockSpec((tm, a.shape[1]), lambda i: (i, 0)) for a in tiled]
    in_specs += [pl.BlockSpec(b.shape, lambda i: (0, 0)) for b in bcast + extra]
    res = pl.pallas_call(
        body, name=name, grid=(rows // tm,), in_specs=in_specs,
        out_specs=[pl.BlockSpec((tm, w), lambda i: (i, 0)) for w in widths],
        out_shape=[jax.ShapeDtypeStruct((rows, w), dt) for w, dt in zip(widths, out_dtypes)],
        compiler_params=_cparams(("parallel",)),
    )(*tiled, *bcast, *extra)
    return list(res)


def _rw_vjp(f, tiled, bcast, cts, need_t, need_b, t_dtypes, *, name):
    nt, nb = len(tiled), len(bcast)
    rows = tiled[0].shape[0]
    flat_cts = [c for group in cts for c in group]
    t_idx = [i for i in range(nt) if need_t[i]]
    b_idx = [i for i in range(nb) if need_b[i]]
    tm = _row_tile(list(tiled) + flat_cts, sum(tiled[i].shape[1] * 4 for i in t_idx))
    nc = len(flat_cts)

    def body(*refs):
        i = pl.program_id(0)
        tin = [r[...].astype(F32) for r in refs[:nt]]
        bin_ = [r[...].astype(F32) for r in refs[nt:nt + nb]]
        ct_refs = refs[nt + nb:nt + nb + nc]
        out_refs = refs[nt + nb + nc:]
        outs, vjp_fn = jax.vjp(f, *tin, *bin_)
        ct_vals, pos = [], 0
        for o, group in zip(outs, cts):
            acc = jnp.zeros_like(o)
            for _ in group:
                acc = acc + ct_refs[pos][...].astype(F32)
                pos += 1
            ct_vals.append(acc)
        grads = vjp_fn(tuple(ct_vals))
        for o_ref, k in zip(out_refs[:len(t_idx)], t_idx):
            o_ref[...] = grads[k].astype(o_ref.dtype)
        for o_ref, k in zip(out_refs[len(t_idx):], b_idx):
            @pl.when(i == 0)
            def _(o_ref=o_ref):
                o_ref[...] = jnp.zeros_like(o_ref)

            o_ref[...] += grads[nt + k]

    in_specs = [pl.BlockSpec((tm, a.shape[1]), lambda i: (i, 0)) for a in tiled]
    in_specs += [pl.BlockSpec(b.shape, lambda i: (0, 0)) for b in bcast]
    in_specs += [pl.BlockSpec((tm, c.shape[1]), lambda i: (i, 0)) for c in flat_cts]
    out_specs = [pl.BlockSpec((tm, tiled[k].shape[1]), lambda i: (i, 0)) for k in t_idx]
    out_specs += [pl.BlockSpec(bcast[k].shape, lambda i: (0, 0)) for k in b_idx]
    out_shape = [jax.ShapeDtypeStruct(tiled[k].shape, dt) for k, dt in zip(t_idx, t_dtypes)]
    out_shape += [jax.ShapeDtypeStruct(bcast[k].shape, F32) for k in b_idx]
    res = pl.pallas_call(
        body, name=name, grid=(rows // tm,), in_specs=in_specs, out_specs=out_specs, out_shape=out_shape,
        compiler_params=_cparams(("arbitrary",)),
    )(*tiled, *bcast, *flat_cts)
    res = list(res)
    return res[:len(t_idx)], res[len(t_idx):]


def _rms(x, g):
    return x * lax.rsqrt(jnp.mean(x * x, axis=-1, keepdims=True) + EPS) * g


def _f_norm_mod(x, g, sc, sh):
    return (_rms(x, g) * (1.0 + sc) + sh,)


def _f_norm_mod_keep(x, g, sc, sh):
    return (_rms(x, g) * (1.0 + sc) + sh, x)


@jax.custom_vjp
def _swap16(x):
    w = x.shape[-1]
    lane = lax.broadcasted_iota(jnp.int32, x.shape, x.ndim - 1)
    return jnp.where((lane & 16) == 0, pltpu.roll(x, w - 16, x.ndim - 1), pltpu.roll(x, 16, x.ndim - 1))


_swap16.defvjp(lambda x: (_swap16(x), None), lambda _, g: (_swap16(g),))


def _rope(x, cos, sin):
    return x * cos + _swap16(x) * sin


def _make_f_post_in(sw, q_rank, kv_rank, with_q):
    o1, o2, o3 = sw, sw + q_rank, sw + q_rank + kv_rank

    if with_q:
        def f(ha, cos, sin, qg, kvg):
            u = ha[:, :o1]
            cqn = _rms(ha[:, o1:o2], qg)
            ckvn = _rms(ha[:, o2:o3], kvg)
            kr = _rope(ha[:, o3:o3 + LANES], cos, sin)
            return u, cqn, ckvn, kr
    else:
        def f(ha, kvg):
            return ha[:, :o1], _rms(ha[:, o2:o3], kvg), ha[:, o3:o3 + LANES]
    return f


def _f_qpost(q2, cos, sin):
    parts = []
    for h in range(q2.shape[1] // (2 * LANES)):
        o = 2 * LANES * h
        parts += [q2[:, o:o + LANES], _rope(q2[:, o + LANES:o + 2 * LANES], cos, sin)]
    return (jnp.concatenate(parts, axis=1),)


def _f_s5post(u, r, d):
    return (jax.nn.gelu(d * u + r, approximate=True),)


def _f_merge(ab, bm, gt):
    d = bm.shape[1]
    br_s5 = ab[:, :d] * jax.nn.sigmoid(ab[:, d:])
    g = jax.nn.sigmoid(gt)
    return (g[:, :d] * br_s5 + g[:, d:] * bm,)


def _f_resid_norm(x, out, g1, n2, sc2, sh2):
    x1 = x + g1 * out
    return x1, _rms(x1, n2) * (1.0 + sc2) + sh2


def _f_swiglu(ab):
    d = ab.shape[1] // 2
    return (jax.nn.silu(ab[:, :d]) * ab[:, d:],)


def _f_final(x1, f, tgt, g2, nf):
    y = _rms(x1 + g2 * f, nf)
    return (0.5 * jnp.mean(jnp.square(y - tgt), axis=-1, keepdims=True),)


def _bd_fanin(xs, ws, *, name):
    nw = len(ws)
    nb, kb, nn = ws[0].shape
    T = xs[0].shape[0]
    tm = _pick(T, 512, 16)

    def body(*refs):
        acc = None
        for x_ref, w_ref in zip(refs[:nw], refs[nw:2 * nw]):
            t = jnp.dot(x_ref[...].astype(BF16), w_ref[0].astype(BF16), preferred_element_type=F32)
            acc = t if acc is None else acc + t
        refs[2 * nw][...] = acc

    return pl.pallas_call(
        body, name=name, grid=(nb, T // tm),
        in_specs=[pl.BlockSpec((tm, kb), lambda j, i: (i, j))] * nw + [pl.BlockSpec((1, kb, nn), lambda j, i: (j, 0, 0))] * nw,
        out_specs=pl.BlockSpec((tm, nn), lambda j, i: (i, j)),
        out_shape=jax.ShapeDtypeStruct((T, nb * nn), F32),
        compiler_params=_cparams(("parallel", "parallel")),
    )(*xs, *ws)


def _bd_dw(xs, dys, nb, *, name):
    npair = len(xs)
    T = xs[0].shape[0]
    kb = xs[0].shape[1] // nb
    nn = dys[0].shape[1] // nb
    tm = _pick(T, 512, 16)
    dims = (((0,), (0,)), ((), ()))

    def body(*refs):
        i = pl.program_id(1)
        for x_ref, d_ref, o_ref in zip(refs[:npair], refs[npair:2 * npair], refs[2 * npair:]):
            @pl.when(i == 0)
            def _(o_ref=o_ref):
                o_ref[...] = jnp.zeros_like(o_ref)

            o_ref[0] += lax.dot_general(x_ref[...].astype(BF16), d_ref[...].astype(BF16), dims,
                                        preferred_element_type=F32)

    return list(pl.pallas_call(
        body, name=name, grid=(nb, T // tm),
        in_specs=[pl.BlockSpec((tm, kb), lambda j, i: (i, j))] * npair + [pl.BlockSpec((tm, nn), lambda j, i: (i, j))] * npair,
        out_specs=[pl.BlockSpec((1, kb, nn), lambda j, i: (j, 0, 0))] * npair,
        out_shape=[jax.ShapeDtypeStruct((nb, kb, nn), F32)] * npair,
        compiler_params=_cparams(("parallel", "arbitrary")),
    )(*xs, *dys))


def _cmul(ar, ai, br, bi):
    return ar * br - ai * bi, ar * bi + ai * br


def _cpow(lr, li, n):
    rr, ri = None, None
    br, bi = lr, li
    while n:
        if n & 1:
            rr, ri = (br, bi) if rr is None else _cmul(rr, ri, br, bi)
        n >>= 1
        if n:
            br, bi = _cmul(br, bi, br, bi)
    return rr, ri


SCAN_MM_ROWS = 512


def _s5_scan(x, w_re, w_im, lam_re, lam_im, h0_re, h0_im, e0_re, e0_im, *, reverse, name):
    rows = x.shape[0]
    nb, kb, cb = w_re.shape
    C = nb * cb
    n = rows // N_SEG
    mm_rows = _pick(rows, SCAN_MM_ROWS, 16)
    seg_order = list(range(N_SEG))[::-1] if reverse else list(range(N_SEG))
    s_first, s_last = seg_order[0], seg_order[-1]

    def body(x_ref, wr_ref, wi_ref, lr_ref, li_ref, h0r_ref, h0i_ref, e0r_ref, e0i_ref, hr_ref, hi_ref, htr_ref, hti_ref,
             locr_ref, loci_ref):
        shape = (N_SEG, cb)
        lr = jnp.broadcast_to(lr_ref[...], shape)
        li = jnp.broadcast_to(li_ref[...], shape)
        row = lax.broadcasted_iota(jnp.int32, shape, 0)

        def step_of(k):
            return (n - 1 - k) if reverse else k

        def rows_of(k):
            return pl.ds(pl.multiple_of(step_of(k) * N_SEG, N_SEG), N_SEG)

        wr, wi = wr_ref[...].astype(BF16), wi_ref[...].astype(BF16)
        for r0 in range(0, rows, mm_rows):
            xb = x_ref[r0:r0 + mm_rows, :].astype(BF16)
            locr_ref[r0:r0 + mm_rows, :] = jnp.dot(xb, wr, preferred_element_type=F32)
            loci_ref[r0:r0 + mm_rows, :] = jnp.dot(xb, wi, preferred_element_type=F32)

        first = row == s_first
        hr = locr_ref[rows_of(0), :] + jnp.where(first, e0r_ref[...], 0.0)
        hi = loci_ref[rows_of(0), :] + jnp.where(first, e0i_ref[...], 0.0)
        locr_ref[rows_of(0), :] = hr
        loci_ref[rows_of(0), :] = hi

        def pass1(k, carry):
            hr, hi = carry
            pr, pi = _cmul(lr, li, hr, hi)
            hr = pr + locr_ref[rows_of(k), :]
            hi = pi + loci_ref[rows_of(k), :]
            locr_ref[rows_of(k), :] = hr
            loci_ref[rows_of(k), :] = hi
            return hr, hi

        er, ei = lax.fori_loop(1, n, pass1, (hr, hi))

        lnr, lni = _cpow(lr_ref[...], li_ref[...], n)
        cr, ci = h0r_ref[...], h0i_ref[...]
        cin_r = jnp.zeros(shape, F32)
        cin_i = jnp.zeros(shape, F32)
        for s in seg_order:
            cin_r = jnp.where(row == s, cr, cin_r)
            cin_i = jnp.where(row == s, ci, cin_i)
            if s != s_last:
                pr, pi = _cmul(lnr, lni, cr, ci)
                cr = pr + jnp.sum(jnp.where(row == s, er, 0.0), axis=0, keepdims=True)
                ci = pi + jnp.sum(jnp.where(row == s, ei, 0.0), axis=0, keepdims=True)

        def pass2(k, carry):
            pr, pi, _, _ = carry
            ar, ai = _cmul(pr, pi, cin_r, cin_i)
            hr = locr_ref[rows_of(k), :] + ar
            hi = loci_ref[rows_of(k), :] + ai
            hr_ref[rows_of(k), :] = hr.astype(hr_ref.dtype)
            hi_ref[rows_of(k), :] = hi.astype(hi_ref.dtype)
            npr, npi = _cmul(pr, pi, lr, li)
            return npr, npi, hr, hi

        _, _, last_r, last_i = lax.fori_loop(0, n, pass2, (lr, li, er, ei))
        htr_ref[...] = jnp.sum(jnp.where(row == s_last, last_r, 0.0), axis=0, keepdims=True)
        hti_ref[...] = jnp.sum(jnp.where(row == s_last, last_i, 0.0), axis=0, keepdims=True)

    big = pl.BlockSpec((rows, cb), lambda j: (0, j))
    vec = pl.BlockSpec((1, cb), lambda j: (0, j))
    wspec = pl.BlockSpec((None, kb, cb), lambda j: (j, 0, 0))
    return pl.pallas_call(
        body, name=name, grid=(nb,),
        in_specs=[pl.BlockSpec((rows, kb), lambda j: (0, j)), wspec, wspec] + [vec] * 6,
        out_specs=[big, big, vec, vec],
        out_shape=[jax.ShapeDtypeStruct((rows, C), BF16)] * 2 + [jax.ShapeDtypeStruct((1, C), F32)] * 2,
        scratch_shapes=[pltpu.VMEM((rows, cb), F32)] * 2,
        compiler_params=_cparams(("parallel",)),
    )(x, w_re, w_im, lam_re, lam_im, h0_re, h0_im, e0_re, e0_im)


def _s5_dlam(mu_re, mu_im, h_re, h_im, h0_re, h0_im, *, reverse, name):
    rows, C = h_re.shape
    n = rows // N_SEG
    cb = _pick(C, 256, LANES)
    s_first = N_SEG - 1 if reverse else 0

    def body(mr_ref, mi_ref, hr_ref, hi_ref, h0r_ref, h0i_ref, dr_ref, di_ref):
        shape = (N_SEG, cb)
        row = lax.broadcasted_iota(jnp.int32, shape, 0)

        def rows_of(k):
            step = (n - 1 - k) if reverse else k
            return pl.ds(pl.multiple_of(step * N_SEG, N_SEG), N_SEG)

        def term(k, pr, pi):
            mr, mi = mr_ref[rows_of(k), :].astype(F32), mi_ref[rows_of(k), :].astype(F32)
            return mr * pr + mi * pi, mi * pr - mr * pi

        shift = N_SEG - 1 if reverse else 1
        pr = jnp.where(row == s_first, h0r_ref[...], pltpu.roll(hr_ref[rows_of(n - 1), :].astype(F32), shift, 0))
        pi = jnp.where(row == s_first, h0i_ref[...], pltpu.roll(hi_ref[rows_of(n - 1), :].astype(F32), shift, 0))
        acc = term(0, pr, pi)

        def loop(k, acc):
            tr, ti = term(k, hr_ref[rows_of(k - 1), :].astype(F32), hi_ref[rows_of(k - 1), :].astype(F32))
            return acc[0] + tr, acc[1] + ti

        ar, ai = lax.fori_loop(1, n, loop, acc)
        dr_ref[...] = jnp.sum(ar, axis=0, keepdims=True)
        di_ref[...] = jnp.sum(ai, axis=0, keepdims=True)

    big = pl.BlockSpec((rows, cb), lambda j: (0, j))
    vec = pl.BlockSpec((1, cb), lambda j: (0, j))
    return pl.pallas_call(
        body, name=name, grid=(C // cb,),
        in_specs=[big] * 4 + [vec] * 2, out_specs=[vec, vec],
        out_shape=[jax.ShapeDtypeStruct((1, C), F32)] * 2,
        compiler_params=_cparams(("parallel",)),
    )(mu_re, mu_im, h_re, h_im, h0_re, h0_im)


NT_DIMS = (((1,), (1,)), ((), ()))
TN_DIMS = (((0,), (0,)), ((), ()))


ATTN_Q_ROWS = 512


def _attn_exp(q, kvh, kr):
    s = (lax.dot_general(q[:, :LANES], kvh[:, :LANES], NT_DIMS, preferred_element_type=F32)
         + lax.dot_general(q[:, LANES:], kr, NT_DIMS, preferred_element_type=F32))
    e = jnp.exp2((s - jnp.max(s, axis=-1, keepdims=True)) * (ATTN_SCALE * math.log2(math.e)))
    return e, jnp.sum(e, axis=-1, keepdims=True)


def _attn_specs(L, T, tq):
    return [
        pl.BlockSpec((tq, 2 * LANES), lambda h, i: (i, h)),
        pl.BlockSpec((T, 2 * LANES), lambda h, i: (0, h)),
        pl.BlockSpec((T, LANES), lambda h, i: (0, 0)),
    ]


def _attn_fwd(qq, kv, kr, *, name):
    L, T = qq.shape[0], kv.shape[0]
    tq = _pick(L, ATTN_Q_ROWS // 2, 16)

    def body(q_ref, kv_ref, kr_ref, o_ref):
        kvh = kv_ref[...]
        e, l = _attn_exp(q_ref[...], kvh, kr_ref[...])
        o_ref[...] = (jnp.dot(e.astype(BF16), kvh[:, LANES:], preferred_element_type=F32) * (1.0 / l)).astype(o_ref.dtype)

    return pl.pallas_call(
        body, name=name, grid=(MLA_HEADS, L // tq), in_specs=_attn_specs(L, T, tq),
        out_specs=pl.BlockSpec((tq, LANES), lambda h, i: (i, h)),
        out_shape=jax.ShapeDtypeStruct((L, MLA_HEADS * V_DIM), BF16),
        compiler_params=_cparams(("parallel", "parallel")),
    )(qq, kv, kr)


def _attn_bwd(qq, kv, kr, do, *, name):
    L, T = qq.shape[0], kv.shape[0]
    H = MLA_HEADS
    tq = _pick(L, ATTN_Q_ROWS, 16)
    nq = L // tq

    def body(q_ref, kv_ref, kr_ref, do_ref, dq_ref, dkv_ref, dkr_ref, dkn_acc, dv_acc):
        h, i = pl.program_id(0), pl.program_id(1)
        q, kvh, krv, dov = q_ref[...], kv_ref[...], kr_ref[...], do_ref[...]
        e, l = _attn_exp(q, kvh, krv)
        inv = 1.0 / l
        ps = e * (inv * ATTN_SCALE)
        t = lax.dot_general(dov, kvh[:, LANES:], NT_DIMS, preferred_element_type=F32) * ps
        ds = (t - ps * (jnp.sum(t, axis=-1, keepdims=True) * (1.0 / ATTN_SCALE))).astype(BF16)
        dq_ref[:, :LANES] = jnp.dot(ds, kvh[:, :LANES], preferred_element_type=F32)
        dq_ref[:, LANES:] = jnp.dot(ds, krv, preferred_element_type=F32)

        @pl.when(i == 0)
        def _():
            dkn_acc[...] = jnp.zeros_like(dkn_acc)
            dv_acc[...] = jnp.zeros_like(dv_acc)

        @pl.when((i == 0) & (h == 0))
        def _():
            dkr_ref[...] = jnp.zeros_like(dkr_ref)

        dv_acc[...] += lax.dot_general(e.astype(BF16), (dov.astype(F32) * inv).astype(BF16), TN_DIMS,
                                       preferred_element_type=F32)
        dkn_acc[...] += lax.dot_general(ds, q[:, :LANES], TN_DIMS, preferred_element_type=F32)
        dkr_ref[...] += lax.dot_general(ds, q[:, LANES:], TN_DIMS, preferred_element_type=F32)

        @pl.when(i == nq - 1)
        def _():
            dkv_ref[:, :LANES] = dkn_acc[...].astype(dkv_ref.dtype)
            dkv_ref[:, LANES:] = dv_acc[...].astype(dkv_ref.dtype)

    in_specs = _attn_specs(L, T, tq) + [pl.BlockSpec((tq, LANES), lambda h, i: (i, h))]
    return pl.pallas_call(
        body, name=name, grid=(H, L // tq), in_specs=in_specs,
        out_specs=[pl.BlockSpec((tq, 2 * LANES), lambda h, i: (i, h)), pl.BlockSpec((T, 2 * LANES), lambda h, i: (0, h)),
                   pl.BlockSpec((T, LANES), lambda h, i: (0, 0))],
        out_shape=[jax.ShapeDtypeStruct((L, H * 2 * LANES), F32), jax.ShapeDtypeStruct((T, H * 2 * LANES), BF16),
                   jax.ShapeDtypeStruct((T, LANES), F32)],
        scratch_shapes=[pltpu.VMEM((T, LANES), F32), pltpu.VMEM((T, LANES), F32)],
        compiler_params=_cparams(("arbitrary", "arbitrary")),
    )(qq, kv, kr, do)


def _adamw(w, g, m, v, *, name, anchor=None):
    c1 = 1.0 - ADAM_B1 ** ADAM_STEP
    c2 = 1.0 - ADAM_B2 ** ADAM_STEP

    def f(w, g, m, v):
        m = ADAM_B1 * m + (1.0 - ADAM_B1) * g
        v = ADAM_B2 * v + (1.0 - ADAM_B2) * jnp.square(g)
        delta = -ADAM_LR * ((m / c1) / (jnp.sqrt(v / c2) + ADAM_EPS) + ADAM_WD * w)
        return g, delta, m, v

    return _rw(f, [w, g, m, v], [], [F32] * 4, name=name, anchor=anchor)


def _slab_rows(rows, cols, n_arrays):
    return _pick(rows, max(16, (8 * 1024 * 1024) // (cols * 4 * n_arrays)), 16)


def _scalars(*vals):
    return jnp.stack([jnp.asarray(v, jnp.int32) for v in vals])


def _into_slot(src, slot, nslots, dtype, *, name):
    R, C = src.shape
    tr = _slab_rows(R, C, 2)

    def body(s_ref, x_ref, o_ref):
        o_ref[...] = x_ref[...].astype(o_ref.dtype)

    return pl.pallas_call(
        body, name=name,
        grid_spec=pltpu.PrefetchScalarGridSpec(
            num_scalar_prefetch=1, grid=(R // tr,),
            in_specs=[pl.BlockSpec((tr, C), lambda i, s: (i, 0))],
            out_specs=pl.BlockSpec((None, tr, C), lambda i, s: (s[0], i, 0))),
        out_shape=jax.ShapeDtypeStruct((nslots, R, C), dtype),
        compiler_params=_cparams(("arbitrary",)),
    )(_scalars(slot), src)


def _pair_sum(g, got, c, *, name):
    _, R, C = g.shape
    hr = R // 2
    tr = _slab_rows(hr, C, 3)
    nblk = hr // tr

    def body(s_ref, g_ref, r_ref, o_ref):
        o_ref[...] = (g_ref[...].astype(F32) + r_ref[...].astype(F32)).astype(o_ref.dtype)

    return pl.pallas_call(
        body, name=name,
        grid_spec=pltpu.PrefetchScalarGridSpec(
            num_scalar_prefetch=1, grid=(4, nblk),
            in_specs=[pl.BlockSpec((None, tr, C), lambda j, i, s: (j, s[0] * nblk + i, 0)),
                      pl.BlockSpec((None, tr, C), lambda j, i, s: (j, i, 0))],
            out_specs=pl.BlockSpec((None, tr, C), lambda j, i, s: (j, i, 0))),
        out_shape=jax.ShapeDtypeStruct((4, hr, C), g.dtype),
        compiler_params=_cparams(("arbitrary", "arbitrary")),
    )(_scalars(c), g, got)


def _chip_sum(p, landed, me_chip, c, *, name):
    _, hr, C = p.shape
    tr = _slab_rows(hr, C, 5)

    def body(s_ref, p_ref, l0_ref, l1_ref, l2_ref, o_ref):
        o_ref[...] = ((p_ref[...].astype(F32) + l0_ref[...].astype(F32)) + l1_ref[...].astype(F32)) + l2_ref[...].astype(F32)

    return pl.pallas_call(
        body, name=name,
        grid_spec=pltpu.PrefetchScalarGridSpec(
            num_scalar_prefetch=1, grid=(hr // tr,),
            in_specs=[pl.BlockSpec((None, tr, C), lambda i, s: (s[0], i, 0))]
            + [pl.BlockSpec((None, tr, C), functools.partial(lambda i, s, k: (k, i, 0), k=k)) for k in range(3)],
            out_specs=pl.BlockSpec((None, tr, C), lambda i, s: (s[1], i, 0))),
        out_shape=jax.ShapeDtypeStruct((2, hr, C), F32),
        compiler_params=_cparams(("arbitrary",)),
    )(_scalars(me_chip, c), p, landed, landed, landed)


def _place():
    return lax.axis_index("x"), lax.axis_index("y"), lax.axis_index("c")


def _other_chips(x, y):
    chips = [(1 - x, y), (x, 1 - y), (1 - x, 1 - y)]
    return chips, [2 * cx + cy for cx, cy in chips]


HBM = pl.BlockSpec(memory_space=pl.ANY)


def _allgather8(v, *, name):
    rows, cols = v.shape

    def body(v_ref, out_ref, send_sems, recv_sems):
        x, y, c = _place()
        me = 4 * x + 2 * y + c
        out_ref[me] = v_ref[...]
        copies = []
        for k in range(1, 8):
            bx, by, bc = (k >> 2) & 1, (k >> 1) & 1, k & 1
            px, py, pc = x ^ bx, y ^ by, c ^ bc
            cp = pltpu.make_async_remote_copy(
                src_ref=v_ref, dst_ref=out_ref.at[me], send_sem=send_sems.at[k - 1], recv_sem=recv_sems.at[k - 1],
                device_id=(px, py, pc), device_id_type=MESH)
            cp.start()
            copies.append((cp, 4 * px + 2 * py + pc))
        for k, (cp, peer) in enumerate(copies):
            pltpu.make_async_remote_copy(
                src_ref=v_ref, dst_ref=out_ref.at[peer], send_sem=send_sems.at[k], recv_sem=recv_sems.at[k],
                device_id=(x, y, c), device_id_type=MESH).wait_recv()
        for cp, _ in copies:
            cp.wait_send()

    return pl.pallas_call(
        body, name=name, out_shape=jax.ShapeDtypeStruct((8, rows, cols), v.dtype),
        in_specs=[pl.BlockSpec(memory_space=pltpu.VMEM)], out_specs=pl.BlockSpec(memory_space=pltpu.VMEM),
        scratch_shapes=[pltpu.SemaphoreType.DMA((7,)), pltpu.SemaphoreType.DMA((7,))],
        compiler_params=pltpu.CompilerParams(vmem_limit_bytes=VMEM_LIMIT),
    )(v)


def _allgather_shards(bufs, *, name):
    n = len(bufs)

    def body(*refs):
        outs = refs[n:2 * n]
        send_sems, recv_sems = refs[2 * n:]
        x, y, c = _place()
        me_chip = 2 * x + y
        sibling = (x, y, 1 - c)
        chips, chip_ids = _other_chips(x, y)

        def remote(k, j, blk, hf, to):
            hr = bufs[k].shape[1] // 2
            piece = outs[k].at[blk, pl.ds(pl.multiple_of(hf * hr, 16), hr), :]
            return pltpu.make_async_remote_copy(
                src_ref=piece, dst_ref=piece, send_sem=send_sems.at[6 * k + j], recv_sem=recv_sems.at[6 * k + j],
                device_id=to, device_id_type=MESH)

        sends = []
        for k in range(n):
            for j, chip in enumerate(chips):
                cp = remote(k, j, me_chip, c, (*chip, c))
                cp.start()
                sends.append(cp)
        for k in range(n):
            for j, chip in enumerate(chips):
                remote(k, j, chip_ids[j], c, (x, y, c)).wait_recv()
                cp = remote(k, 3 + j, chip_ids[j], c, sibling)
                cp.start()
                sends.append(cp)
        for k in range(n):
            for j in range(3):
                remote(k, 3 + j, chip_ids[j], 1 - c, (x, y, c)).wait_recv()
        for cp in sends:
            cp.wait_send()

    return list(pl.pallas_call(
        body, name=name, out_shape=[jax.ShapeDtypeStruct(b.shape, b.dtype) for b in bufs],
        in_specs=[HBM] * n, out_specs=[HBM] * n, input_output_aliases={k: k for k in range(n)},
        scratch_shapes=[pltpu.SemaphoreType.DMA((6 * n,)), pltpu.SemaphoreType.DMA((6 * n,))],
    )(*bufs))


def _pair_exchange(gs, *, name, anchor=None):
    n = len(gs)
    extra = [] if anchor is None else [anchor]
    n_in = n + len(extra)

    def body(*refs):
        ins, outs = refs[:n], refs[n_in:n_in + n]
        send_sems, recv_sems = refs[n_in + n:]
        x, y, c = _place()
        copies = []
        for k in range(n):
            hr = gs[k].shape[1] // 2
            src = ins[k].at[:, pl.ds(pl.multiple_of((1 - c) * hr, 16), hr), :]
            cp = pltpu.make_async_remote_copy(src_ref=src, dst_ref=outs[k], send_sem=send_sems.at[k], recv_sem=recv_sems.at[k],
                                              device_id=(x, y, 1 - c), device_id_type=MESH)
            cp.start()
            copies.append(cp)
        for cp in copies:
            cp.wait()

    return list(pl.pallas_call(
        body, name=name,
        out_shape=[jax.ShapeDtypeStruct((4, g.shape[1] // 2, g.shape[2]), g.dtype) for g in gs],
        in_specs=[HBM] * n_in, out_specs=[HBM] * n,
        scratch_shapes=[pltpu.SemaphoreType.DMA((n,)), pltpu.SemaphoreType.DMA((n,))],
    )(*gs, *extra))


def _pair_gather(bufs, *, name):
    n = len(bufs)

    def body(*refs):
        outs = refs[n:2 * n]
        send_sems, recv_sems = refs[2 * n:]
        x, y, c = _place()

        def remote(k, hf, to):
            return pltpu.make_async_remote_copy(src_ref=outs[k].at[hf], dst_ref=outs[k].at[hf], send_sem=send_sems.at[k],
                                                recv_sem=recv_sems.at[k], device_id=to, device_id_type=MESH)

        copies = [remote(k, c, (x, y, 1 - c)) for k in range(n)]
        for cp in copies:
            cp.start()
        for k, cp in enumerate(copies):
            cp.wait_send()
            remote(k, 1 - c, (x, y, c)).wait_recv()

    return list(pl.pallas_call(
        body, name=name, out_shape=[jax.ShapeDtypeStruct(b.shape, b.dtype) for b in bufs],
        in_specs=[HBM] * n, out_specs=[HBM] * n, input_output_aliases={k: k for k in range(n)},
        scratch_shapes=[pltpu.SemaphoreType.DMA((n,)), pltpu.SemaphoreType.DMA((n,))],
    )(*bufs))


HBM_SPEC = pl.BlockSpec(memory_space=pltpu.HBM)
SEM_SPEC = pl.BlockSpec(memory_space=pltpu.SEMAPHORE)
EFFECT = pltpu.SideEffectType.DATAFLOW_SIDE_EFFECTING
TOKEN = jax.ShapeDtypeStruct((SUBLANES, LANES), F32)


def _in_hbm(a):
    return pltpu.with_memory_space_constraint(a, pltpu.HBM)


def _ici_copies(srcs, dsts, send_sems, recv_sems, send):
    x, y, c = _place()
    me_chip = 2 * x + y
    chips, chip_ids = _other_chips(x, y)
    out = []
    for k, (src, dst) in enumerate(zip(srcs, dsts)):
        for j, chip in enumerate(chips):
            s_ref, d_ref = (src(k, me_chip, chip_ids[j], j), dst(k, me_chip, chip_ids[j], j))
            out.append(pltpu.make_async_remote_copy(
                src_ref=s_ref if send else d_ref, dst_ref=d_ref, send_sem=send_sems.at[3 * k + j],
                recv_sem=recv_sems.at[3 * k + j], device_id=(*chip, c) if send else (x, y, c), device_id_type=MESH))
    return out


def _half_rows(buf, hf):
    hr = buf.shape[1] // 2
    return pl.ds(pl.multiple_of(hf * hr, 16), hr)


def _ag_pieces(refs):
    c = lax.axis_index("c")
    src = [functools.partial(lambda k, me, other, j, r: r.at[me, _half_rows(r, c), :], r=r) for r in refs]
    dst_send = src
    dst_recv = [functools.partial(lambda k, me, other, j, r: r.at[other, _half_rows(r, c), :], r=r) for r in refs]
    return src, dst_send, dst_recv


def _ag_start(bufs, groups, *, name):
    n, ng = len(bufs), len(groups)

    def body(*refs):
        sems = refs[n:n + 2 * ng]
        thru = refs[n + 2 * ng:2 * n + 2 * ng]
        token = refs[-1]
        for g, ks in enumerate(groups):
            src, dst_send, _ = _ag_pieces([thru[k] for k in ks])
            for cp in _ici_copies(src, dst_send, sems[2 * g], sems[2 * g + 1], True):
                cp.start()
        token[...] = jnp.zeros_like(token)

    out_shape = tuple(pltpu.SemaphoreType.DMA((3 * len(ks),)) for ks in groups for _ in range(2))
    out_shape += tuple(pltpu.HBM(b.shape, b.dtype) for b in bufs) + (TOKEN,)
    res = pl.pallas_call(
        body, name=name, out_shape=out_shape, in_specs=(HBM_SPEC,) * n,
        out_specs=(SEM_SPEC,) * (2 * ng) + (HBM_SPEC,) * n + (pl.BlockSpec(memory_space=pltpu.VMEM),),
        input_output_aliases={k: 2 * ng + k for k in range(n)},
        compiler_params=pltpu.CompilerParams(has_side_effects=EFFECT),
    )(*[_in_hbm(b) for b in bufs])
    sems = [(res[2 * g], res[2 * g + 1]) for g in range(ng)]
    return sems, list(res[2 * ng:2 * ng + n]), res[-1]


def _ag_wait(bufs, send_sems, recv_sems, after, *, name):
    n = len(bufs)
    after = list(after)

    def body(*refs):
        ins = refs[:n]
        send, recv = refs[n], refs[n + 1]
        src, dst_send, dst_recv = _ag_pieces(ins)
        for cp in _ici_copies(src, dst_send, send, recv, True):
            cp.wait_send()
        for cp in _ici_copies(src, dst_recv, send, recv, False):
            cp.wait_recv()

    return list(pl.pallas_call(
        body, name=name, out_shape=tuple(pltpu.HBM(b.shape, b.dtype) for b in bufs),
        in_specs=(HBM_SPEC,) * n + (SEM_SPEC, SEM_SPEC) + (pl.BlockSpec(memory_space=pl.ANY),) * len(after),
        out_specs=(HBM_SPEC,) * n, input_output_aliases={k: k for k in range(n)},
        compiler_params=pltpu.CompilerParams(has_side_effects=EFFECT),
    )(*bufs, send_sems, recv_sems, *after))


def _ag_forward(bufs, *, name):
    n = len(bufs)

    def body(*refs):
        outs = refs[n:2 * n]
        send_sems, recv_sems = refs[2 * n:]
        x, y, c = _place()
        _, chip_ids = _other_chips(x, y)

        def remote(k, j, hf, to):
            piece = outs[k].at[chip_ids[j], _half_rows(outs[k], hf), :]
            return pltpu.make_async_remote_copy(src_ref=piece, dst_ref=piece, send_sem=send_sems.at[3 * k + j],
                                                recv_sem=recv_sems.at[3 * k + j], device_id=to, device_id_type=MESH)

        sends = [remote(k, j, c, (x, y, 1 - c)) for k in range(n) for j in range(3)]
        for cp in sends:
            cp.start()
        for k in range(n):
            for j in range(3):
                remote(k, j, 1 - c, (x, y, c)).wait_recv()
        for cp in sends:
            cp.wait_send()

    return list(pl.pallas_call(
        body, name=name, out_shape=[jax.ShapeDtypeStruct(b.shape, b.dtype) for b in bufs],
        in_specs=[HBM] * n, out_specs=[HBM] * n, input_output_aliases={k: k for k in range(n)},
        scratch_shapes=[pltpu.SemaphoreType.DMA((3 * n,)), pltpu.SemaphoreType.DMA((3 * n,))],
    )(*bufs))


def _rs_pieces(p_refs, l_refs):
    src = [functools.partial(lambda k, me, other, j, r: r.at[other], r=r) for r in p_refs]
    dst = [functools.partial(lambda k, me, other, j, r: r.at[j], r=r) for r in l_refs]
    return src, dst


def _rs_start(ps, *, name):
    n = len(ps)
    lands = [lax.empty((3,) + p.shape[1:], p.dtype) for p in ps]

    def body(*refs):
        send, recv = refs[2 * n], refs[2 * n + 1]
        p_thru = refs[2 * n + 2:3 * n + 2]
        l_thru = refs[3 * n + 2:4 * n + 2]
        token = refs[-1]
        src, dst = _rs_pieces(p_thru, l_thru)
        for cp in _ici_copies(src, dst, send, recv, True):
            cp.start()
        token[...] = jnp.zeros_like(token)

    out_shape = (pltpu.SemaphoreType.DMA((3 * n,)), pltpu.SemaphoreType.DMA((3 * n,)))
    out_shape += tuple(pltpu.HBM(a.shape, a.dtype) for a in list(ps) + lands) + (TOKEN,)
    res = pl.pallas_call(
        body, name=name, out_shape=out_shape, in_specs=(HBM_SPEC,) * (2 * n),
        out_specs=(SEM_SPEC, SEM_SPEC) + (HBM_SPEC,) * (2 * n) + (pl.BlockSpec(memory_space=pltpu.VMEM),),
        input_output_aliases={k: 2 + k for k in range(2 * n)},
        compiler_params=pltpu.CompilerParams(has_side_effects=EFFECT),
    )(*[_in_hbm(a) for a in list(ps) + lands])
    return (res[0], res[1]), list(res[2:2 + n]), list(res[2 + n:2 + 2 * n]), res[-1]


def _rs_wait(ps, lands, send_sems, recv_sems, after, *, name):
    n = len(ps)

    def body(*refs):
        p_in, l_in = refs[:n], refs[n:2 * n]
        send, recv = refs[2 * n], refs[2 * n + 1]
        src, dst = _rs_pieces(p_in, l_in)
        for cp in _ici_copies(src, dst, send, recv, True):
            cp.wait_send()
        for cp in _ici_copies(src, dst, send, recv, False):
            cp.wait_recv()

    res = pl.pallas_call(
        body, name=name, out_shape=tuple(pltpu.HBM(a.shape, a.dtype) for a in list(ps) + list(lands)),
        in_specs=(HBM_SPEC,) * (2 * n) + (SEM_SPEC, SEM_SPEC) + (pl.BlockSpec(memory_space=pl.ANY),) * len(after),
        out_specs=(HBM_SPEC,) * (2 * n), input_output_aliases={k: k for k in range(2 * n)},
        compiler_params=pltpu.CompilerParams(has_side_effects=EFFECT),
    )(*ps, *lands, send_sems, recv_sems, *after)
    return list(res[:n]), list(res[n:])


def _rs_begin(gs, tag, anchor=None):
    c = lax.axis_index("c")
    got = _pair_exchange(gs, name=f"rs_pair_exchange_{tag}", anchor=anchor)
    pair = [_pair_sum(g, r, c, name=f"rs_pair_sum_{tag}{k}") for k, (g, r) in enumerate(zip(gs, got))]
    sems, pair, lands, token = _rs_start(pair, name=f"rs_start_{tag}")
    return (sems, pair, lands), token


def _rs_end(handle, after, tag):
    x, y, c = _place()
    (send, recv), pair, lands = handle
    pair, lands = _rs_wait(pair, lands, send, recv, after, name=f"rs_wait_{tag}")
    halves = [_chip_sum(p, l, 2 * x + y, c, name=f"rs_chip_sum_{tag}{k}") for k, (p, l) in enumerate(zip(pair, lands))]
    full = _pair_gather(halves, name=f"rs_pair_gather_{tag}")
    return [f.reshape(2 * f.shape[1], f.shape[2]) for f in full]


def _to_segments(a):
    rows = a.shape[0]
    return a.reshape(N_SEG, rows // N_SEG, -1).transpose(1, 0, 2).reshape(rows, -1)


def _from_segments(a):
    rows = a.shape[0]
    return a.reshape(rows // N_SEG, N_SEG, -1).transpose(1, 0, 2).reshape(rows, -1)


def _rope_tables(L):
    t = jnp.arange(L, dtype=jnp.int32)
    row = (t // GRID_W).astype(F32)
    col = (t % GRID_W).astype(F32)
    n_freq = QK_ROPE // 4
    inv = ROPE_BASE ** (-jnp.arange(n_freq, dtype=F32) / n_freq)
    a0, a1 = row[:, None] * inv, col[:, None] * inv
    z = jnp.zeros((L, LANES - QK_ROPE), F32)
    cos = jnp.concatenate([jnp.cos(a0), jnp.cos(a0), jnp.cos(a1), jnp.cos(a1), z], axis=1)
    sin = jnp.concatenate([-jnp.sin(a0), jnp.sin(a0), -jnp.sin(a1), jnp.sin(a1), z], axis=1)
    return _to_segments(cos), _to_segments(sin)


def _col_blocks(w, nblk):
    r, c = w.shape
    return w.reshape(r, nblk, c // nblk).transpose(1, 0, 2)


def _from_col_blocks(w4):
    nblk, r, c = w4.shape
    return w4.transpose(1, 0, 2).reshape(r, nblk * c)


def _s5_discretize(a_re, a_im, log_dt, b_re, b_im):
    dt = jnp.exp(log_dt)[:, None]
    mag = jnp.exp(a_re * dt)
    ab_re, ab_im = mag * jnp.cos(a_im * dt), mag * jnp.sin(a_im * dt)
    den = a_re * a_re + a_im * a_im
    nr, ni = ab_re - 1.0, ab_im
    co_re = (nr * a_re + ni * a_im) / den
    co_im = (ni * a_re - nr * a_im) / den
    bb_re = co_re[..., None] * b_re - co_im[..., None] * b_im
    bb_im = co_re[..., None] * b_im + co_im[..., None] * b_re
    return ab_re, ab_im, bb_re, bb_im


def _diag_blocks_in(bb, gpb):
    G, N, P = bb.shape
    t = jnp.tile(jnp.swapaxes(bb, 1, 2).reshape(G // gpb, gpb * P, N), (1, 1, gpb))
    row = lax.broadcasted_iota(jnp.int32, t.shape, 1) // P
    col = lax.broadcasted_iota(jnp.int32, t.shape, 2) // N
    return jnp.where(row == col, t, 0.0)


def _diag_blocks_out(cc, gpb):
    G, P, N = cc.shape
    t = jnp.tile(jnp.swapaxes(cc, 1, 2).reshape(G // gpb, gpb * N, P), (1, 1, gpb))
    row = lax.broadcasted_iota(jnp.int32, t.shape, 1) // N
    col = lax.broadcasted_iota(jnp.int32, t.shape, 2) // P
    return jnp.where(row == col, t, 0.0)


def _tr(ws):
    return [jnp.swapaxes(w, 1, 2) for w in ws]


WEIGHTS = ['c_ctx', 'w_mod', 'b_mod', 'norm1', 'norm2', 'w_in', 's5_a_re', 's5_a_im', 's5_log_dt', 's5_b_re', 's5_b_im',
           's5_c_re', 's5_c_im', 's5_d', 'w_glu', 'q_norm', 'kv_norm', 'w_uq', 'w_ukv', 'w_mla_o', 'w_out', 'w_ffn_in',
           'w_ffn_out', 'norm_f']
AG_GROUPS = [['w_in'], ['w_glu', 'w_uq', 'w_ukv', 'w_mla_o', 'w_out'], ['w_ffn_in', 'w_ffn_out']]
SMALL = ['norm1', 'norm2', 's5_a_re', 's5_a_im', 's5_log_dt', 's5_b_re', 's5_b_im', 's5_c_re', 's5_c_im', 's5_d',
         'q_norm', 'kv_norm', 'norm_f']


def _pad_rows(a, rows):
    return jnp.concatenate([a, jnp.zeros((rows - a.shape[0],) + a.shape[1:], a.dtype)], axis=0)


def _pack(vals, width, rows):
    flat = jnp.concatenate([v.reshape(-1).astype(F32) for v in vals])
    flat = jnp.concatenate([flat, jnp.zeros((rows * width - flat.shape[0],), F32)])
    return flat.reshape(rows, width)


def _unpack(buf, like):
    flat = buf.reshape(-1)
    out, pos = [], 0
    for v in like:
        out.append(flat[pos:pos + v.size].reshape(v.shape))
        pos += v.size
    return out


def _step(x, c, ctx, loss_target, w, m, v):
    px, py, pc = _place()
    me = 4 * px + 2 * py + pc
    me_chip = 2 * px + py
    L, D = x.shape[1], x.shape[2]
    Lc = ctx.shape[1]
    T = L + Lc
    SW = D // 2
    G = SW // S5_GROUP
    C = G * S5_STATE
    H = MLA_HEADS
    q_rank = w['q_norm'].shape[1]
    kv_rank = w['kv_norm'].shape[1]
    d_ff = w['w_ffn_out'].shape[1] * 4
    wa_used = SW + q_rank + kv_rank + QK_ROPE
    WA = -(-(SW + q_rank + kv_rank + LANES) // 512) * 512

    c_rows = _pad_rows(c.astype(F32), SUBLANES)
    c_all = _allgather8(c_rows, name="ag_cond")[:, 0, :]
    cond = jnp.concatenate([c_all, w['c_ctx'].reshape(1, D)], axis=0)
    cond = _pad_rows(cond, 16)
    (act,) = _rw(lambda t: (jax.nn.silu(t),), [cond], [], [F32], name="cond_silu")
    w_mod, cs_mod = w['w_mod'][0], w['w_mod'].shape[2]
    mod_part = _mm(act, w_mod, out_dtype=F32, name="mod_fwd")
    mod_all = _allgather8(mod_part, name="ag_mod")
    mod_full = jnp.concatenate([mod_all[0], mod_all[2], mod_all[4], mod_all[6]], axis=1) + w['b_mod']
    m_lat = lax.dynamic_slice_in_dim(mod_full, me, 1, axis=0).reshape(6, D)
    m_ctx = mod_full[8].reshape(6, D)
    sh1, sc1, g1, sh2, sc2, g2 = (m_lat[i:i + 1] for i in range(6))
    csh1, csc1 = m_ctx[0:1], m_ctx[1:2]

    names = [nme for grp in AG_GROUPS for nme in grp]
    bufs = [_into_slot(w[nme][0], me_chip, 4, BF16, name=f"cast_{nme}") for nme in names]
    group_idx, pos = [], 0
    for grp in AG_GROUPS:
        group_idx.append(list(range(pos, pos + len(grp))))
        pos += len(grp)
    ag_sems, bufs, ag_token = _ag_start(bufs, group_idx, name="ag_start")
    gathered = {}

    def arrive(g, after):
        got = _ag_wait([bufs[k] for k in group_idx[g]], *ag_sems[g], after, name=f"ag_wait_{g}")
        gathered.update(zip(AG_GROUPS[g], _ag_forward(got, name=f"ag_forward_{g}")))

    xs = _to_segments(x[0])
    cs = _to_segments(ctx[0])
    tgt = _to_segments(loss_target[0])
    cos, sin = _rope_tables(L)
    n1, n2, nf = w['norm1'], w['norm2'], w['norm_f'].reshape(1, D)
    qg, kvg = w['q_norm'], w['kv_norm']

    (xn_lat,) = _rw(_f_norm_mod, [xs], [n1 + ag_token[0, 0], sc1, sh1], [BF16], name="norm1_lat")
    (xn_ctx,) = _rw(_f_norm_mod, [cs], [n1, csc1, csh1], [BF16], name="norm1_ctx")
    xn = jnp.concatenate([xn_lat, xn_ctx], axis=0)

    gpb = min(S5_BLOCK_GROUPS, G)
    gpo = min(8, G)
    d_skip = w['s5_d'][0].reshape(1, SW)
    disc, vjp_disc, w_b, w_c = [], [], [], []
    for d in range(2):
        prm = (w['s5_a_re'][0, d], w['s5_a_im'][0, d], w['s5_log_dt'][0, d], w['s5_b_re'][0, d], w['s5_b_im'][0, d])

        def prep(a_re, a_im, log_dt, b_re, b_im):
            ab_re, ab_im, bb_re, bb_im = _s5_discretize(a_re, a_im, log_dt, b_re, b_im)
            return ab_re.reshape(1, C), ab_im.reshape(1, C), _diag_blocks_in(bb_re, gpb), _diag_blocks_in(bb_im, gpb)

        out, vj = jax.vjp(prep, *prm)
        disc.append(out)
        vjp_disc.append(vj)
        w_b += [out[2], out[3]]
        w_c += [_diag_blocks_out(w['s5_c_re'][0, d], gpo), -_diag_blocks_out(w['s5_c_im'][0, d], gpo)]
    nb_in = G // gpb
    nb_out = G // gpo

    arrive(0, [xn, tgt] + w_b + w_c)
    w_in = _from_col_blocks(gathered['w_in'])
    w_a = jnp.concatenate([w_in[:, :wa_used], jnp.zeros((D, WA - wa_used), BF16)], axis=1)
    w_g = w_in[:, wa_used:]
    ha = _mm(xn, w_a, out_dtype=F32, name="in_proj")
    ha_lat, ha_ctx = ha[:L], ha[L:]
    gt = _mm(xn_lat, w_g, out_dtype=F32, name="in_gates")
    f_post_lat = _make_f_post_in(SW, q_rank, kv_rank, True)
    f_post_ctx = _make_f_post_in(SW, q_rank, kv_rank, False)
    u_lat, cqn, ckvn_lat, kr_lat = _rw(f_post_lat, [ha_lat, cos, sin], [qg, kvg], [F32, BF16, BF16, BF16], name="post_in_lat")
    u_ctx, ckvn_ctx, kr_ctx = _rw(f_post_ctx, [ha_ctx], [kvg], [F32, BF16, BF16], name="post_in_ctx")

    zero = jnp.zeros((1, C), F32)
    h_lat, h_ctx, hT_ctx = [], [], []
    for d, rev in enumerate((False, True)):
        lr, li = disc[d][0], disc[d][1]
        hcr, hci, tr, ti = _s5_scan(u_ctx, w_b[2 * d], w_b[2 * d + 1], lr, li, zero, zero, zero, zero, reverse=rev,
                                    name=f"s5_scan_ctx_{d}")
        hlr, hli, _, _ = _s5_scan(u_lat, w_b[2 * d], w_b[2 * d + 1], lr, li, tr, ti, zero, zero, reverse=rev,
                                  name=f"s5_scan_lat_{d}")
        h_ctx += [hcr, hci]
        h_lat += [hlr, hli]
        hT_ctx += [tr, ti]
    r5 = _bd_fanin(h_lat, w_c, name="s5_readout")
    (z,) = _rw(_f_s5post, [u_lat, r5], [d_skip], [BF16], name="s5_post")

    arrive(1, [z])
    w_glu, w_ukv, w_mla_o = (gathered[nme] for nme in ('w_glu', 'w_ukv', 'w_mla_o'))
    w_out = gathered['w_out'].reshape(D, D)
    uq3 = _from_col_blocks(gathered['w_uq']).reshape(q_rank, H, QK_NOPE + QK_ROPE)
    w_q2 = jnp.concatenate([uq3, jnp.zeros((q_rank, H, LANES - QK_ROPE), BF16)], axis=2).reshape(q_rank, H * 2 * LANES)
    q2 = _mm(cqn, w_q2, out_dtype=F32, name="q_up")
    (qq,) = _rw(_f_qpost, [q2, cos, sin], [], [BF16], name="q_rope")
    kvn = jnp.concatenate([ckvn_lat, ckvn_ctx], axis=0)
    kr_all = jnp.concatenate([kr_lat, kr_ctx], axis=0)
    kv = _mm(kvn, w_ukv, b_shards=4, out_dtype=BF16, name="kv_up")
    o = _attn_fwd(qq, kv, kr_all, name="attn_fwd")

    ab = _mm(z, w_glu, b_shards=4, out_dtype=F32, name="glu_proj")
    bm = _mm(o, w_mla_o, b_shards=4, out_dtype=F32, name="mla_out")
    (mix,) = _rw(_f_merge, [ab, bm, gt], [], [BF16], name="merge")
    out1 = _mm(mix, w_out, out_dtype=F32, name="out_proj")
    x1, xn2 = _rw(_f_resid_norm, [xs, out1], [g1, n2, sc2, sh2], [F32, BF16], name="resid_norm2")
    arrive(2, [xn2])
    w_ffn_in = gathered['w_ffn_in']
    w_ffn_out = gathered['w_ffn_out'].reshape(d_ff, D)
    ab2 = _mm(xn2, w_ffn_in, b_shards=4, out_dtype=F32, name="ffn_in")
    (hmid,) = _rw(_f_swiglu, [ab2], [], [BF16], name="ffn_act")
    f2 = _mm(hmid, w_ffn_out, out_dtype=F32, name="ffn_out")
    (row_loss,) = _rw(_f_final, [x1, f2, tgt], [g2, nf], [F32], name="final_loss")
    loss = lax.psum(jnp.sum(row_loss), ("x", "y", "c"))

    ones = jnp.ones((L, 1), F32)
    (dx1_a, df2), (dg2, dnf) = _rw_vjp(_f_final, [x1, f2, tgt], [g2, nf], [[ones]], [True, True, False], [True, True],
                                       [F32, BF16], name="final_loss_bwd")
    dhmid = _mm(df2, w_ffn_out, tb=True, out_dtype=F32, name="ffn_out_dx")
    gw_ffn_out = _mm(hmid, df2, ta=True, out_dtype=BF16, name="ffn_out_dw")
    (dab2,), _ = _rw_vjp(_f_swiglu, [ab2], [], [[dhmid]], [True], [], [BF16], name="ffn_act_bwd")
    dxn2 = _mm(dab2, w_ffn_in, tb=True, b_shards=4, out_dtype=F32, name="ffn_in_dx")
    gw_ffn_in = _mm(xn2, dab2, ta=True, out_shards=4, out_dtype=BF16, name="ffn_in_dw")
    rs_ffn, tok = _rs_begin([gw_ffn_out.reshape(4, -1, D), gw_ffn_in], "ffn")
    (dx_a, dout1), (dg1, dn2, dsc2, dsh2) = _rw_vjp(
        _f_resid_norm, [xs, out1], [g1, n2 + tok[0, 0], sc2, sh2], [[dx1_a], [dxn2]], [True, True], [True] * 4, [F32, BF16],
        name="resid_norm2_bwd")
    dmix = _mm(dout1, w_out, tb=True, out_dtype=F32, name="out_proj_dx")
    gw_out = _mm(mix, dout1, ta=True, out_dtype=BF16, name="out_proj_dw")
    (dab, dbm, dgt), _ = _rw_vjp(_f_merge, [ab, bm, gt], [], [[dmix]], [True] * 3, [], [BF16] * 3, name="merge_bwd")
    dz = _mm(dab, w_glu, tb=True, b_shards=4, out_dtype=F32, name="glu_proj_dx")
    gw_glu = _mm(z, dab, ta=True, out_shards=4, out_dtype=BF16, name="glu_proj_dw")
    do = _mm(dbm, w_mla_o, tb=True, b_shards=4, out_dtype=BF16, name="mla_out_dx")
    gw_mla_o = _mm(o, dbm, ta=True, out_shards=4, out_dtype=BF16, name="mla_out_dw")
    dxn_g = _mm(dgt, w_g, tb=True, out_dtype=F32, name="in_gates_dx")
    gw_g = _mm(xn_lat, dgt, ta=True, out_dtype=BF16, name="in_gates_dw")

    (du_a, dr5), (dd_skip,) = _rw_vjp(_f_s5post, [u_lat, r5], [d_skip], [[dz]], [True, True], [True], [F32, F32],
                                      name="s5_post_bwd")
    dw_c = _bd_dw(h_lat, [dr5] * 4, nb_out, name="s5_readout_dw")
    w_ct = _tr(w_c)
    zeros_ctx = jnp.zeros((Lc, SW), BF16)
    mu_lat, mu_ctx, dlam = [], [], []
    for d, rev in enumerate((False, True)):
        lr, li = disc[d][0], disc[d][1]
        mlr, mli, fr, fi = _s5_scan(dr5, w_ct[2 * d], w_ct[2 * d + 1], lr, -li, zero, zero, zero, zero, reverse=not rev,
                                    name=f"s5_adj_lat_{d}")
        dh0r, dh0i = _cmul(lr, -li, fr, fi)
        mcr, mci, _, _ = _s5_scan(zeros_ctx, w_ct[2 * d], w_ct[2 * d + 1], lr, -li, zero, zero, dh0r, dh0i,
                                  reverse=not rev, name=f"s5_adj_ctx_{d}")
        dl_lat = _s5_dlam(mlr, mli, h_lat[2 * d], h_lat[2 * d + 1], hT_ctx[2 * d], hT_ctx[2 * d + 1], reverse=rev,
                          name=f"s5_dlam_lat_{d}")
        dl_ctx = _s5_dlam(mcr, mci, h_ctx[2 * d], h_ctx[2 * d + 1], zero, zero, reverse=rev, name=f"s5_dlam_ctx_{d}")
        mu_lat += [mlr, mli]
        mu_ctx += [mcr, mci]
        dlam.append((dl_lat[0] + dl_ctx[0], dl_lat[1] + dl_ctx[1]))
    du_b = _bd_fanin(mu_lat, _tr(w_b), name="s5_bu_lat_dx")
    du_ctx = _bd_fanin(mu_ctx, _tr(w_b), name="s5_bu_ctx_dx")
    dw_b_lat = _bd_dw([u_lat] * 4, mu_lat, nb_in, name="s5_bu_lat_dw")
    dw_b_ctx = _bd_dw([u_ctx] * 4, mu_ctx, nb_in, name="s5_bu_ctx_dw")
    g_s5 = {}
    for d in range(2):
        ct = (dlam[d][0], dlam[d][1], dw_b_lat[2 * d] + dw_b_ctx[2 * d], dw_b_lat[2 * d + 1] + dw_b_ctx[2 * d + 1])
        ga_re, ga_im, gdt, gb_re, gb_im = vjp_disc[d](ct)
        _, vj_c = jax.vjp(lambda cr, ci: (_diag_blocks_out(cr, gpo), -_diag_blocks_out(ci, gpo)),
                          w['s5_c_re'][0, d], w['s5_c_im'][0, d])
        gc_re, gc_im = vj_c((dw_c[2 * d], dw_c[2 * d + 1]))
        for nme, val in (('s5_a_re', ga_re), ('s5_a_im', ga_im), ('s5_log_dt', gdt), ('s5_b_re', gb_re),
                         ('s5_b_im', gb_im), ('s5_c_re', gc_re), ('s5_c_im', gc_im)):
            g_s5.setdefault(nme, []).append(val)
    g_small = {nme: jnp.stack(vals)[None] for nme, vals in g_s5.items()}
    g_small['s5_d'] = dd_skip.reshape(w['s5_d'].shape)

    dqq, dkv, dkr = _attn_bwd(qq, kv, kr_all, do, name="attn_bwd")
    (dq2,), _ = _rw_vjp(_f_qpost, [q2, cos, sin], [], [[dqq]], [True, False, False], [], [BF16], name="q_rope_bwd")
    dcqn = _mm(dq2, w_q2, tb=True, out_dtype=F32, name="q_up_dx")
    gw_q2 = _mm(cqn, dq2, ta=True, out_dtype=BF16, name="q_up_dw")
    dckvn = _mm(dkv, w_ukv, tb=True, b_shards=4, out_dtype=F32, name="kv_up_dx")
    gw_ukv = _mm(kvn, dkv, ta=True, out_shards=4, out_dtype=BF16, name="kv_up_dw")
    gw_uq = gw_q2.reshape(q_rank, H, 2 * LANES)[:, :, :QK_NOPE + QK_ROPE].reshape(q_rank, H * (QK_NOPE + QK_ROPE))
    rs_mix, tok = _rs_begin([gw_out.reshape(4, -1, D), gw_glu, gw_mla_o, _col_blocks(gw_uq, 4), gw_ukv], "mix")

    (dha_lat,), (dqg, dkvg_lat) = _rw_vjp(
        f_post_lat, [ha_lat, cos, sin], [qg, kvg + tok[0, 0]], [[du_a, du_b], [dcqn], [dckvn[:L]], [dkr[:L]]],
        [True, False, False], [True, True], [BF16], name="post_in_lat_bwd")
    (dha_ctx,), (dkvg_ctx,) = _rw_vjp(f_post_ctx, [ha_ctx], [kvg], [[du_ctx], [dckvn[L:]], [dkr[L:]]], [True], [True],
                                      [BF16], name="post_in_ctx_bwd")
    dha = jnp.concatenate([dha_lat, dha_ctx], axis=0)
    dxn = _mm(dha, w_a, tb=True, out_dtype=F32, name="in_proj_dx")
    gw_a = _mm(xn, dha, ta=True, out_dtype=BF16, name="in_proj_dw")
    (dx_seg,), (dn1_lat, dsc1, dsh1) = _rw_vjp(
        _f_norm_mod_keep, [xs], [n1, sc1, sh1], [[dxn[:L], dxn_g], [dx_a]], [True], [True] * 3, [F32], name="norm1_lat_bwd")
    _, (dn1_ctx, dcsc1, dcsh1) = _rw_vjp(_f_norm_mod, [cs], [n1, csc1, csh1], [[dxn[L:]]], [False], [True] * 3, [],
                                         name="norm1_ctx_bwd")
    grad_x = _from_segments(dx_seg)[None]
    g_small.update(norm1=dn1_lat + dn1_ctx, norm2=dn2, q_norm=dqg, kv_norm=dkvg_lat + dkvg_ctx, norm_f=dnf.reshape(D))

    zD = jnp.zeros((1, D), F32)
    dm = jnp.concatenate([
        jnp.concatenate([dsh1, dsc1, dg1, dsh2, dsc2, dg2], axis=1),
        jnp.concatenate([dcsh1, dcsc1, zD, zD, zD, zD], axis=1),
    ], axis=0)
    dm_all = _allgather8(_pad_rows(dm, SUBLANES), name="ag_dmod")
    dm_ctx = dm_all[0, 1]
    for k in range(1, 8):
        dm_ctx = dm_ctx + dm_all[k, 1]
    dmod = _pad_rows(jnp.concatenate([dm_all[:, 0, :], dm_ctx[None]], axis=0), 16)
    g_b_mod = jnp.sum(dmod, axis=0, keepdims=True)
    dmod_mine = lax.dynamic_slice_in_dim(dmod, me_chip * cs_mod, cs_mod, axis=1)
    g_w_mod = _mm(act, dmod_mine, ta=True, out_dtype=F32, name="mod_dw")
    dact_part = _mm(dmod_mine, w_mod, tb=True, out_dtype=F32, name="mod_dx")
    dact_all = _allgather8(dact_part, name="ag_dact")
    dact = dact_all[0] + dact_all[2] + dact_all[4] + dact_all[6]
    (dcond_rows,), _ = _rw_vjp(lambda t: (jax.nn.silu(t),), [cond], [], [[dact]], [True], [], [F32], name="cond_silu_bwd")
    g_c_ctx = dcond_rows[8]

    gw_in = jnp.concatenate([gw_a[:, :wa_used], gw_g], axis=1)
    small_vals = [g_small[nme] for nme in SMALL]
    n_small = sum(val.size for val in small_vals)
    small_rows = -(-n_small // (LANES * 4 * 32)) * 32
    rs_in, tok = _rs_begin([_col_blocks(gw_in, 4), _pack(small_vals, LANES, 4 * small_rows).reshape(4, small_rows, LANES)],
                           "in", anchor=g_c_ctx)

    grads, delta, new_m, new_v = {}, {}, {}, {}

    def update(nme, red, anchor=None):
        res = _adamw(w[nme][0], red, m[nme][0], v[nme][0], name=f"adamw_{nme}", anchor=anchor)
        grads[nme], delta[nme], new_m[nme], new_v[nme] = (r.reshape(w[nme].shape) for r in res)
        return res[1]

    after = [update('w_mod', g_w_mod, anchor=tok)]
    for handle, tag, members in ((rs_ffn, "ffn", ['w_ffn_out', 'w_ffn_in']),
                                 (rs_mix, "mix", ['w_out', 'w_glu', 'w_mla_o', 'w_uq', 'w_ukv']),
                                 (rs_in, "in", ['w_in'])):
        reduced = _rs_end(handle, after, tag)
        for nme, red in zip(members, reduced):
            after.append(update(nme, red))
    small_mine = reduced[-1]
    small_buf = _into_slot(small_mine, me_chip, 4, F32, name="small_grads_slot")
    small_all = _allgather_shards([small_buf], name="ag_small_grads")[0].reshape(4 * small_rows, LANES)
    g_small_red = dict(zip(SMALL, _unpack(small_all, [w[nme] for nme in SMALL])))
    rest = SMALL + ['c_ctx', 'b_mod']
    g_rest = dict(g_small_red, c_ctx=g_c_ctx, b_mod=g_b_mod)
    rows_rest = -(-sum(w[nme].size for nme in rest) // (LANES * 16)) * 16
    packed = [_pack([src[nme] for nme in rest], LANES, rows_rest) for src in (w, g_rest, m, v)]
    res = _adamw(*packed, name="adamw_small")
    for dst, buf in zip((grads, delta, new_m, new_v), res):
        dst.update(zip(rest, _unpack(buf, [w[nme] for nme in rest])))
    return (loss, grad_x, *[grads[nme] for nme in WEIGHTS], *[delta[nme] for nme in WEIGHTS],
            *[new_m[nme] for nme in WEIGHTS], *[new_v[nme] for nme in WEIGHTS])


def kernel(x, c, ctx, c_ctx, w_mod, b_mod, norm1, norm2, w_in, s5_a_re, s5_a_im, s5_log_dt, s5_b_re, s5_b_im, s5_c_re, s5_c_im, s5_d, w_glu, q_norm, kv_norm, w_uq, w_ukv, w_mla_o, w_out, w_ffn_in, w_ffn_out, norm_f, loss_target, m_c_ctx, m_w_mod, m_b_mod, m_norm1, m_norm2, m_w_in, m_s5_a_re, m_s5_a_im, m_s5_log_dt, m_s5_b_re, m_s5_b_im, m_s5_c_re, m_s5_c_im, m_s5_d, m_w_glu, m_q_norm, m_kv_norm, m_w_uq, m_w_ukv, m_w_mla_o, m_w_out, m_w_ffn_in, m_w_ffn_out, m_norm_f, v_c_ctx, v_w_mod, v_b_mod, v_norm1, v_norm2, v_w_in, v_s5_a_re, v_s5_a_im, v_s5_log_dt, v_s5_b_re, v_s5_b_im, v_s5_c_re, v_s5_c_im, v_s5_d, v_w_glu, v_q_norm, v_kv_norm, v_w_uq, v_w_ukv, v_w_mla_o, v_w_out, v_w_ffn_in, v_w_ffn_out, v_norm_f):
    w = dict(c_ctx=c_ctx, w_mod=w_mod, b_mod=b_mod, norm1=norm1, norm2=norm2, w_in=w_in, s5_a_re=s5_a_re, s5_a_im=s5_a_im,
             s5_log_dt=s5_log_dt, s5_b_re=s5_b_re, s5_b_im=s5_b_im, s5_c_re=s5_c_re, s5_c_im=s5_c_im, s5_d=s5_d, w_glu=w_glu,
             q_norm=q_norm, kv_norm=kv_norm, w_uq=w_uq, w_ukv=w_ukv, w_mla_o=w_mla_o, w_out=w_out, w_ffn_in=w_ffn_in,
             w_ffn_out=w_ffn_out, norm_f=norm_f)
    m = dict(c_ctx=m_c_ctx, w_mod=m_w_mod, b_mod=m_b_mod, norm1=m_norm1, norm2=m_norm2, w_in=m_w_in, s5_a_re=m_s5_a_re,
             s5_a_im=m_s5_a_im, s5_log_dt=m_s5_log_dt, s5_b_re=m_s5_b_re, s5_b_im=m_s5_b_im, s5_c_re=m_s5_c_re,
             s5_c_im=m_s5_c_im, s5_d=m_s5_d, w_glu=m_w_glu, q_norm=m_q_norm, kv_norm=m_kv_norm, w_uq=m_w_uq, w_ukv=m_w_ukv,
             w_mla_o=m_w_mla_o, w_out=m_w_out, w_ffn_in=m_w_ffn_in, w_ffn_out=m_w_ffn_out, norm_f=m_norm_f)
    v = dict(c_ctx=v_c_ctx, w_mod=v_w_mod, b_mod=v_b_mod, norm1=v_norm1, norm2=v_norm2, w_in=v_w_in, s5_a_re=v_s5_a_re,
             s5_a_im=v_s5_a_im, s5_log_dt=v_s5_log_dt, s5_b_re=v_s5_b_re, s5_b_im=v_s5_b_im, s5_c_re=v_s5_c_re,
             s5_c_im=v_s5_c_im, s5_d=v_s5_d, w_glu=v_w_glu, q_norm=v_q_norm, kv_norm=v_kv_norm, w_uq=v_w_uq, w_ukv=v_w_ukv,
             w_mla_o=v_w_mla_o, w_out=v_w_out, w_ffn_in=v_w_ffn_in, w_ffn_out=v_w_ffn_out, norm_f=v_norm_f)
    return _step(x, c, ctx, loss_target, w, m, v)
```

```python
import functools
import math

import jax
import jax.numpy as jnp
from jax import lax
from jax.experimental import pallas as pl
from jax.experimental.pallas import tpu as pltpu

F32 = jnp.float32
BF16 = jnp.bfloat16

EPS = 1e-6
GRID_W = 64
S5_GROUP = 16
S5_STATE = 64
MLA_HEADS = 8
QK_NOPE = 128
QK_ROPE = 64
V_DIM = 128
ROPE_BASE = 10000.0
ATTN_SCALE = (QK_NOPE + QK_ROPE) ** -0.5
ADAM_LR = 0.001
ADAM_B1 = 0.9
ADAM_B2 = 0.999
ADAM_EPS = 1e-08
ADAM_WD = 0.01
ADAM_STEP = 10

SUBLANES = 8
LANES = 128
V7X_VMEM_BYTES = 64 * 1024 * 1024
VMEM_LIMIT = (V7X_VMEM_BYTES * 7) // 8
N_SEG = 2 * SUBLANES
S5_BLOCK_GROUPS = 8
MESH = pl.DeviceIdType.MESH


def _pick(n, target, mult):
    best = None
    d = mult
    while d <= min(n, target):
        if n % d == 0:
            best = d
        d += mult
    return n if best is None else best


def _cparams(sem=None):
    return pltpu.CompilerParams(dimension_semantics=sem, vmem_limit_bytes=VMEM_LIMIT)


MM_VMEM_BUDGET = (V7X_VMEM_BYTES * 5) // 8


def _mm(a, b, *, ta=False, tb=False, out_dtype=F32, name, b_shards=1, out_shards=1):
    if ta:
        K, M = a.shape
    else:
        M, K = a.shape
    if tb:
        N, K2 = b.shape[-2], b.shape[-1] * b_shards
    else:
        K2, N = b.shape[-2], b.shape[-1] * b_shards
    assert K == K2, (a.shape, b.shape, ta, tb)
    n_unit = N // max(out_shards, 1 if tb else b_shards)
    k_unit = K // (b_shards if tb else 1)
    tn = _pick(n_unit, 1024, LANES)
    tm = _pick(M, 1024 if tn >= 512 else 2048, LANES if ta else 16)
    sa, sb, so = a.dtype.itemsize, b.dtype.itemsize, jnp.dtype(out_dtype).itemsize
    k_mult = LANES if (not ta or tb) else 16
    tk = k_mult if k_unit % k_mult == 0 else k_unit
    for cand in range(k_mult, k_unit + 1, k_mult):
        if k_unit % cand == 0 and 2 * cand * (tm * sa + tn * sb) + tm * tn * (4 + 2 * so) <= MM_VMEM_BUDGET:
            tk = cand
    nk = K // tk
    dims = (((0 if ta else 1,), (1 if tb else 0,)), ((), ()))

    def body(a_ref, b_ref, o_ref, *scratch):
        part = lax.dot_general(a_ref[...].astype(BF16), b_ref[...].astype(BF16), dims, preferred_element_type=F32)
        if nk == 1:
            o_ref[...] = part.astype(o_ref.dtype)
            return
        acc_ref, = scratch
        k = pl.program_id(2)

        @pl.when(k == 0)
        def _():
            acc_ref[...] = part

        @pl.when(k > 0)
        def _():
            acc_ref[...] += part

        @pl.when(k == nk - 1)
        def _():
            o_ref[...] = acc_ref[...].astype(o_ref.dtype)

    a_spec = pl.BlockSpec((tk, tm), lambda i, j, k: (k, i)) if ta else pl.BlockSpec((tm, tk), lambda i, j, k: (i, k))
    if b_shards == 1:
        b_spec = pl.BlockSpec((tn, tk), lambda i, j, k: (j, k)) if tb else pl.BlockSpec((tk, tn), lambda i, j, k: (k, j))
    elif tb:
        kpb = k_unit // tk
        b_spec = pl.BlockSpec((None, tn, tk), lambda i, j, k: (k // kpb, j, k % kpb))
    else:
        npb = n_unit // tn
        b_spec = pl.BlockSpec((None, tk, tn), lambda i, j, k: (j // npb, k, j % npb))
    if out_shards == 1:
        out_spec = pl.BlockSpec((tm, tn), lambda i, j, k: (i, j))
        out_shape = jax.ShapeDtypeStruct((M, N), out_dtype)
    else:
        opb = n_unit // tn
        out_spec = pl.BlockSpec((None, tm, tn), lambda i, j, k: (j // opb, i, j % opb))
        out_shape = jax.ShapeDtypeStruct((out_shards, M, N // out_shards), out_dtype)
    return pl.pallas_call(
        body, name=name, grid=(M // tm, N // tn, nk),
        in_specs=[a_spec, b_spec], out_specs=out_spec, out_shape=out_shape,
        scratch_shapes=[pltpu.VMEM((tm, tn), F32)] if nk > 1 else [],
        compiler_params=_cparams(("parallel", "parallel", "arbitrary")),
    )(a, b)


def _row_tile(tiled, extra_bytes=0):
    rows = tiled[0].shape[0]
    per_row = sum(a.shape[1] * 4 for a in tiled) + extra_bytes
    target = max(SUBLANES, (6 * 1024 * 1024) // max(per_row, 1))
    return _pick(rows, min(target, 512), 16)


def _rw(f, tiled, bcast, out_dtypes, *, name, anchor=None):
    nt, nb = len(tiled), len(bcast)
    rows = tiled[0].shape[0]
    outs_aval = jax.eval_shape(f, *[jax.ShapeDtypeStruct((16, a.shape[1]), F32) for a in tiled],
                               *[jax.ShapeDtypeStruct(b.shape, F32) for b in bcast])
    widths = [o.shape[1] for o in outs_aval]
    tm = _row_tile(tiled, sum(w * 4 for w in widths))

    extra = [] if anchor is None else [anchor]
    n_in = nt + nb + len(extra)

    def body(*refs):
        tin = [r[...].astype(F32) for r in refs[:nt]]
        bin_ = [r[...].astype(F32) for r in refs[nt:nt + nb]]
        outs = f(*tin, *bin_)
        for o_ref, o in zip(refs[n_in:], outs):
            o_ref[...] = o.astype(o_ref.dtype)

    in_specs = [pl.BlockSpec((tm, a.shape[1]), lambda i: (i, 0)) for a in tiled]
    in_specs += [pl.BlockSpec(b.shape, lambda i: (0, 0)) for b in bcast + extra]
    res = pl.pallas_call(
        body, name=name, grid=(rows // tm,), in_specs=in_specs,
        out_specs=[pl.BlockSpec((tm, w), lambda i: (i, 0)) for w in widths],
        out_shape=[jax.ShapeDtypeStruct((rows, w), dt) for w, dt in zip(widths, out_dtypes)],
        compiler_params=_cparams(("parallel",)),
    )(*tiled, *bcast, *extra)
    return list(res)


def _rw_vjp(f, tiled, bcast, cts, need_t, need_b, t_dtypes, *, name):
    nt, nb = len(tiled), len(bcast)
    rows = tiled[0].shape[0]
    flat_cts = [c for group in cts for c in group]
    t_idx = [i for i in range(nt) if need_t[i]]
    b_idx = [i for i in range(nb) if need_b[i]]
    tm = _row_tile(list(tiled) + flat_cts, sum(tiled[i].shape[1] * 4 for i in t_idx))
    nc = len(flat_cts)

    def body(*refs):
        i = pl.program_id(0)
        tin = [r[...].astype(F32) for r in refs[:nt]]
        bin_ = [r[...].astype(F32) for r in refs[nt:nt + nb]]
        ct_refs = refs[nt + nb:nt + nb + nc]
        out_refs = refs[nt + nb + nc:]
        outs, vjp_fn = jax.vjp(f, *tin, *bin_)
        ct_vals, pos = [], 0
        for o, group in zip(outs, cts):
            acc = jnp.zeros_like(o)
            for _ in group:
                acc = acc + ct_refs[pos][...].astype(F32)
                pos += 1
            ct_vals.append(acc)
        grads = vjp_fn(tuple(ct_vals))
        for o_ref, k in zip(out_refs[:len(t_idx)], t_idx):
            o_ref[...] = grads[k].astype(o_ref.dtype)
        for o_ref, k in zip(out_refs[len(t_idx):], b_idx):
            @pl.when(i == 0)
            def _(o_ref=o_ref):
                o_ref[...] = jnp.zeros_like(o_ref)

            o_ref[...] += grads[nt + k]

    in_specs = [pl.BlockSpec((tm, a.shape[1]), lambda i: (i, 0)) for a in tiled]
    in_specs += [pl.BlockSpec(b.shape, lambda i: (0, 0)) for b in bcast]
    in_specs += [pl.BlockSpec((tm, c.shape[1]), lambda i: (i, 0)) for c in flat_cts]
    out_specs = [pl.BlockSpec((tm, tiled[k].shape[1]), lambda i: (i, 0)) for k in t_idx]
    out_specs += [pl.BlockSpec(bcast[k].shape, lambda i: (0, 0)) for k in b_idx]
    out_shape = [jax.ShapeDtypeStruct(tiled[k].shape, dt) for k, dt in zip(t_idx, t_dtypes)]
    out_shape += [jax.ShapeDtypeStruct(bcast[k].shape, F32) for k in b_idx]
    res = pl.pallas_call(
        body, name=name, grid=(rows // tm,), in_specs=in_specs, out_specs=out_specs, out_shape=out_shape,
        compiler_params=_cparams(("arbitrary",)),
    )(*tiled, *bcast, *flat_cts)
    res = list(res)
    return res[:len(t_idx)], res[len(t_idx):]


def _rms(x, g):
    return x * lax.rsqrt(jnp.mean(x * x, axis=-1, keepdims=True) + EPS) * g


def _f_norm_mod(x, g, sc, sh):
    return (_rms(x, g) * (1.0 + sc) + sh,)


def _f_norm_mod_keep(x, g, sc, sh):
    return (_rms(x, g) * (1.0 + sc) + sh, x)


@jax.custom_vjp
def _swap16(x):
    w = x.shape[-1]
    lane = lax.broadcasted_iota(jnp.int32, x.shape, x.ndim - 1)
    return jnp.where((lane & 16) == 0, pltpu.roll(x, w - 16, x.ndim - 1), pltpu.roll(x, 16, x.ndim - 1))


_swap16.defvjp(lambda x: (_swap16(x), None), lambda _, g: (_swap16(g),))


def _rope(x, cos, sin):
    return x * cos + _swap16(x) * sin


def _make_f_post_in(sw, q_rank, kv_rank, with_q):
    o1, o2, o3 = sw, sw + q_rank, sw + q_rank + kv_rank

    if with_q:
        def f(ha, cos, sin, qg, kvg):
            u = ha[:, :o1]
            cqn = _rms(ha[:, o1:o2], qg)
            ckvn = _rms(ha[:, o2:o3], kvg)
            kr = _rope(ha[:, o3:o3 + LANES], cos, sin)
            return u, cqn, ckvn, kr
    else:
        def f(ha, kvg):
            return ha[:, :o1], _rms(ha[:, o2:o3], kvg), ha[:, o3:o3 + LANES]
    return f


def _f_qpost(q2, cos, sin):
    parts = []
    for h in range(q2.shape[1] // (2 * LANES)):
        o = 2 * LANES * h
        parts += [q2[:, o:o + LANES], _rope(q2[:, o + LANES:o + 2 * LANES], cos, sin)]
    return (jnp.concatenate(parts, axis=1),)


def _f_s5post(u, r, d):
    return (jax.nn.gelu(d * u + r, approximate=True),)


def _f_merge(ab, bm, gt):
    d = bm.shape[1]
    br_s5 = ab[:, :d] * jax.nn.sigmoid(ab[:, d:])
    g = jax.nn.sigmoid(gt)
    return (g[:, :d] * br_s5 + g[:, d:] * bm,)


def _f_resid_norm(x, out, g1, n2, sc2, sh2):
    x1 = x + g1 * out
    return x1, _rms(x1, n2) * (1.0 + sc2) + sh2


def _f_swiglu(ab):
    d = ab.shape[1] // 2
    return (jax.nn.silu(ab[:, :d]) * ab[:, d:],)


def _f_final(x1, f, tgt, g2, nf):
    y = _rms(x1 + g2 * f, nf)
    return (0.5 * jnp.mean(jnp.square(y - tgt), axis=-1, keepdims=True),)


def _bd_fanin(xs, ws, *, name):
    nw = len(ws)
    nb, kb, nn = ws[0].shape
    T = xs[0].shape[0]
    tm = _pick(T, 512, 16)

    def body(*refs):
        acc = None
        for x_ref, w_ref in zip(refs[:nw], refs[nw:2 * nw]):
            t = jnp.dot(x_ref[...].astype(BF16), w_ref[0].astype(BF16), preferred_element_type=F32)
            acc = t if acc is None else acc + t
        refs[2 * nw][...] = acc

    return pl.pallas_call(
        body, name=name, grid=(nb, T // tm),
        in_specs=[pl.BlockSpec((tm, kb), lambda j, i: (i, j))] * nw + [pl.BlockSpec((1, kb, nn), lambda j, i: (j, 0, 0))] * nw,
        out_specs=pl.BlockSpec((tm, nn), lambda j, i: (i, j)),
        out_shape=jax.ShapeDtypeStruct((T, nb * nn), F32),
        compiler_params=_cparams(("parallel", "parallel")),
    )(*xs, *ws)


def _bd_dw(xs, dys, nb, *, name):
    npair = len(xs)
    T = xs[0].shape[0]
    kb = xs[0].shape[1] // nb
    nn = dys[0].shape[1] // nb
    tm = _pick(T, 512, 16)
    dims = (((0,), (0,)), ((), ()))

    def body(*refs):
        i = pl.program_id(1)
        for x_ref, d_ref, o_ref in zip(refs[:npair], refs[npair:2 * npair], refs[2 * npair:]):
            @pl.when(i == 0)
            def _(o_ref=o_ref):
                o_ref[...] = jnp.zeros_like(o_ref)

            o_ref[0] += lax.dot_general(x_ref[...].astype(BF16), d_ref[...].astype(BF16), dims,
                                        preferred_element_type=F32)

    return list(pl.pallas_call(
        body, name=name, grid=(nb, T // tm),
        in_specs=[pl.BlockSpec((tm, kb), lambda j, i: (i, j))] * npair + [pl.BlockSpec((tm, nn), lambda j, i: (i, j))] * npair,
        out_specs=[pl.BlockSpec((1, kb, nn), lambda j, i: (j, 0, 0))] * npair,
        out_shape=[jax.ShapeDtypeStruct((nb, kb, nn), F32)] * npair,
        compiler_params=_cparams(("parallel", "arbitrary")),
    )(*xs, *dys))


def _cmul(ar, ai, br, bi):
    return ar * br - ai * bi, ar * bi + ai * br


def _cpow(lr, li, n):
    rr, ri = None, None
    br, bi = lr, li
    while n:
        if n & 1:
            rr, ri = (br, bi) if rr is None else _cmul(rr, ri, br, bi)
        n >>= 1
        if n:
            br, bi = _cmul(br, bi, br, bi)
    return rr, ri


SCAN_MM_ROWS = 512


def _s5_scan(x, w_re, w_im, lam_re, lam_im, h0_re, h0_im, e0_re, e0_im, *, reverse, name):
    rows = x.shape[0]
    nb, kb, cb = w_re.shape
    C = nb * cb
    n = rows // N_SEG
    mm_rows = _pick(rows, SCAN_MM_ROWS, 16)
    seg_order = list(range(N_SEG))[::-1] if reverse else list(range(N_SEG))
    s_first, s_last = seg_order[0], seg_order[-1]

    def body(x_ref, wr_ref, wi_ref, lr_ref, li_ref, h0r_ref, h0i_ref, e0r_ref, e0i_ref, hr_ref, hi_ref, htr_ref, hti_ref,
             locr_ref, loci_ref):
        shape = (N_SEG, cb)
        lr = jnp.broadcast_to(lr_ref[...], shape)
        li = jnp.broadcast_to(li_ref[...], shape)
        row = lax.broadcasted_iota(jnp.int32, shape, 0)

        def step_of(k):
            return (n - 1 - k) if reverse else k

        def rows_of(k):
            return pl.ds(pl.multiple_of(step_of(k) * N_SEG, N_SEG), N_SEG)

        wr, wi = wr_ref[...].astype(BF16), wi_ref[...].astype(BF16)
        for r0 in range(0, rows, mm_rows):
            xb = x_ref[r0:r0 + mm_rows, :].astype(BF16)
            locr_ref[r0:r0 + mm_rows, :] = jnp.dot(xb, wr, preferred_element_type=F32)
            loci_ref[r0:r0 + mm_rows, :] = jnp.dot(xb, wi, preferred_element_type=F32)

        first = row == s_first
        hr = locr_ref[rows_of(0), :] + jnp.where(first, e0r_ref[...], 0.0)
        hi = loci_ref[rows_of(0), :] + jnp.where(first, e0i_ref[...], 0.0)
        locr_ref[rows_of(0), :] = hr
        loci_ref[rows_of(0), :] = hi

        def pass1(k, carry):
            hr, hi = carry
            pr, pi = _cmul(lr, li, hr, hi)
            hr = pr + locr_ref[rows_of(k), :]
            hi = pi + loci_ref[rows_of(k), :]
            locr_ref[rows_of(k), :] = hr
            loci_ref[rows_of(k), :] = hi
            return hr, hi

        er, ei = lax.fori_loop(1, n, pass1, (hr, hi))

        lnr, lni = _cpow(lr_ref[...], li_ref[...], n)
        cr, ci = h0r_ref[...], h0i_ref[...]
        cin_r = jnp.zeros(shape, F32)
        cin_i = jnp.zeros(shape, F32)
        for s in seg_order:
            cin_r = jnp.where(row == s, cr, cin_r)
            cin_i = jnp.where(row == s, ci, cin_i)
            if s != s_last:
                pr, pi = _cmul(lnr, lni, cr, ci)
                cr = pr + jnp.sum(jnp.where(row == s, er, 0.0), axis=0, keepdims=True)
                ci = pi + jnp.sum(jnp.where(row == s, ei, 0.0), axis=0, keepdims=True)

        def pass2(k, carry):
            pr, pi, _, _ = carry
            ar, ai = _cmul(pr, pi, cin_r, cin_i)
            hr = locr_ref[rows_of(k), :] + ar
            hi = loci_ref[rows_of(k), :] + ai
            hr_ref[rows_of(k), :] = hr.astype(hr_ref.dtype)
            hi_ref[rows_of(k), :] = hi.astype(hi_ref.dtype)
            npr, npi = _cmul(pr, pi, lr, li)
            return npr, npi, hr, hi

        _, _, last_r, last_i = lax.fori_loop(0, n, pass2, (lr, li, er, ei))
        htr_ref[...] = jnp.sum(jnp.where(row == s_last, last_r, 0.0), axis=0, keepdims=True)
        hti_ref[...] = jnp.sum(jnp.where(row == s_last, last_i, 0.0), axis=0, keepdims=True)

    big = pl.BlockSpec((rows, cb), lambda j: (0, j))
    vec = pl.BlockSpec((1, cb), lambda j: (0, j))
    wspec = pl.BlockSpec((None, kb, cb), lambda j: (j, 0, 0))
    return pl.pallas_call(
        body, name=name, grid=(nb,),
        in_specs=[pl.BlockSpec((rows, kb), lambda j: (0, j)), wspec, wspec] + [vec] * 6,
        out_specs=[big, big, vec, vec],
        out_shape=[jax.ShapeDtypeStruct((rows, C), BF16)] * 2 + [jax.ShapeDtypeStruct((1, C), F32)] * 2,
        scratch_shapes=[pltpu.VMEM((rows, cb), F32)] * 2,
        compiler_params=_cparams(("parallel",)),
    )(x, w_re, w_im, lam_re, lam_im, h0_re, h0_im, e0_re, e0_im)


def _s5_dlam(mu_re, mu_im, h_re, h_im, h0_re, h0_im, *, reverse, name):
    rows, C = h_re.shape
    n = rows // N_SEG
    cb = _pick(C, 256, LANES)
    s_first = N_SEG - 1 if reverse else 0

    def body(mr_ref, mi_ref, hr_ref, hi_ref, h0r_ref, h0i_ref, dr_ref, di_ref):
        shape = (N_SEG, cb)
        row = lax.broadcasted_iota(jnp.int32, shape, 0)

        def rows_of(k):
            step = (n - 1 - k) if reverse else k
            return pl.ds(pl.multiple_of(step * N_SEG, N_SEG), N_SEG)

        def term(k, pr, pi):
            mr, mi = mr_ref[rows_of(k), :].astype(F32), mi_ref[rows_of(k), :].astype(F32)
            return mr * pr + mi * pi, mi * pr - mr * pi

        shift = N_SEG - 1 if reverse else 1
        pr = jnp.where(row == s_first, h0r_ref[...], pltpu.roll(hr_ref[rows_of(n - 1), :].astype(F32), shift, 0))
        pi = jnp.where(row == s_first, h0i_ref[...], pltpu.roll(hi_ref[rows_of(n - 1), :].astype(F32), shift, 0))
        acc = term(0, pr, pi)

        def loop(k, acc):
            tr, ti = term(k, hr_ref[rows_of(k - 1), :].astype(F32), hi_ref[rows_of(k - 1), :].astype(F32))
            return acc[0] + tr, acc[1] + ti

        ar, ai = lax.fori_loop(1, n, loop, acc)
        dr_ref[...] = jnp.sum(ar, axis=0, keepdims=True)
        di_ref[...] = jnp.sum(ai, axis=0, keepdims=True)

    big = pl.BlockSpec((rows, cb), lambda j: (0, j))
    vec = pl.BlockSpec((1, cb), lambda j: (0, j))
    return pl.pallas_call(
        body, name=name, grid=(C // cb,),
        in_specs=[big] * 4 + [vec] * 2, out_specs=[vec, vec],
        out_shape=[jax.ShapeDtypeStruct((1, C), F32)] * 2,
        compiler_params=_cparams(("parallel",)),
    )(mu_re, mu_im, h_re, h_im, h0_re, h0_im)


NT_DIMS = (((1,), (1,)), ((), ()))
TN_DIMS = (((0,), (0,)), ((), ()))


ATTN_Q_ROWS = 512


def _attn_exp(q, kvh, kr):
    s = (lax.dot_general(q[:, :LANES], kvh[:, :LANES], NT_DIMS, preferred_element_type=F32)
         + lax.dot_general(q[:, LANES:], kr, NT_DIMS, preferred_element_type=F32))
    e = jnp.exp2((s - jnp.max(s, axis=-1, keepdims=True)) * (ATTN_SCALE * math.log2(math.e)))
    return e, jnp.sum(e, axis=-1, keepdims=True)


def _attn_specs(L, T, tq):
    return [
        pl.BlockSpec((tq, 2 * LANES), lambda h, i: (i, h)),
        pl.BlockSpec((T, 2 * LANES), lambda h, i: (0, h)),
        pl.BlockSpec((T, LANES), lambda h, i: (0, 0)),
    ]


def _attn_fwd(qq, kv, kr, *, name):
    L, T = qq.shape[0], kv.shape[0]
    tq = _pick(L, ATTN_Q_ROWS // 2, 16)

    def body(q_ref, kv_ref, kr_ref, o_ref):
        kvh = kv_ref[...]
        e, l = _attn_exp(q_ref[...], kvh, kr_ref[...])
        o_ref[...] = (jnp.dot(e.astype(BF16), kvh[:, LANES:], preferred_element_type=F32) * (1.0 / l)).astype(o_ref.dtype)

    return pl.pallas_call(
        body, name=name, grid=(MLA_HEADS, L // tq), in_specs=_attn_specs(L, T, tq),
        out_specs=pl.BlockSpec((tq, LANES), lambda h, i: (i, h)),
        out_shape=jax.ShapeDtypeStruct((L, MLA_HEADS * V_DIM), BF16),
        compiler_params=_cparams(("parallel", "parallel")),
    )(qq, kv, kr)


def _attn_bwd(qq, kv, kr, do, *, name):
    L, T = qq.shape[0], kv.shape[0]
    H = MLA_HEADS
    tq = _pick(L, ATTN_Q_ROWS, 16)
    nq = L // tq

    def body(q_ref, kv_ref, kr_ref, do_ref, dq_ref, dkv_ref, dkr_ref, dkn_acc, dv_acc):
        h, i = pl.program_id(0), pl.program_id(1)
        q, kvh, krv, dov = q_ref[...], kv_ref[...], kr_ref[...], do_ref[...]
        e, l = _attn_exp(q, kvh, krv)
        inv = 1.0 / l
        ps = e * (inv * ATTN_SCALE)
        t = lax.dot_general(dov, kvh[:, LANES:], NT_DIMS, preferred_element_type=F32) * ps
        ds = (t - ps * (jnp.sum(t, axis=-1, keepdims=True) * (1.0 / ATTN_SCALE))).astype(BF16)
        dq_ref[:, :LANES] = jnp.dot(ds, kvh[:, :LANES], preferred_element_type=F32)
        dq_ref[:, LANES:] = jnp.dot(ds, krv, preferred_element_type=F32)

        @pl.when(i == 0)
        def _():
            dkn_acc[...] = jnp.zeros_like(dkn_acc)
            dv_acc[...] = jnp.zeros_like(dv_acc)

        @pl.when((i == 0) & (h == 0))
        def _():
            dkr_ref[...] = jnp.zeros_like(dkr_ref)

        dv_acc[...] += lax.dot_general(e.astype(BF16), (dov.astype(F32) * inv).astype(BF16), TN_DIMS,
                                       preferred_element_type=F32)
        dkn_acc[...] += lax.dot_general(ds, q[:, :LANES], TN_DIMS, preferred_element_type=F32)
        dkr_ref[...] += lax.dot_general(ds, q[:, LANES:], TN_DIMS, preferred_element_type=F32)

        @pl.when(i == nq - 1)
        def _():
            dkv_ref[:, :LANES] = dkn_acc[...].astype(dkv_ref.dtype)
            dkv_ref[:, LANES:] = dv_acc[...].astype(dkv_ref.dtype)

    in_specs = _attn_specs(L, T, tq) + [pl.BlockSpec((tq, LANES), lambda h, i: (i, h))]
    return pl.pallas_call(
        body, name=name, grid=(H, L // tq), in_specs=in_specs,
        out_specs=[pl.BlockSpec((tq, 2 * LANES), lambda h, i: (i, h)), pl.BlockSpec((T, 2 * LANES), lambda h, i: (0, h)),
                   pl.BlockSpec((T, LANES), lambda h, i: (0, 0))],
        out_shape=[jax.ShapeDtypeStruct((L, H * 2 * LANES), F32), jax.ShapeDtypeStruct((T, H * 2 * LANES), BF16),
                   jax.ShapeDtypeStruct((T, LANES), F32)],
        scratch_shapes=[pltpu.VMEM((T, LANES), F32), pltpu.VMEM((T, LANES), F32)],
        compiler_params=_cparams(("arbitrary", "arbitrary")),
    )(qq, kv, kr, do)


def _adamw(w, g, m, v, *, name, anchor=None):
    c1 = 1.0 - ADAM_B1 ** ADAM_STEP
    c2 = 1.0 - ADAM_B2 ** ADAM_STEP

    def f(w, g, m, v):
        m = ADAM_B1 * m + (1.0 - ADAM_B1) * g
        v = ADAM_B2 * v + (1.0 - ADAM_B2) * jnp.square(g)
        delta = -ADAM_LR * ((m / c1) / (jnp.sqrt(v / c2) + ADAM_EPS) + ADAM_WD * w)
        return g, delta, m, v

    return _rw(f, [w, g, m, v], [], [F32] * 4, name=name, anchor=anchor)


def _slab_rows(rows, cols, n_arrays):
    return _pick(rows, max(16, (8 * 1024 * 1024) // (cols * 4 * n_arrays)), 16)


def _scalars(*vals):
    return jnp.stack([jnp.asarray(v, jnp.int32) for v in vals])


def _into_slot(src, slot, nslots, dtype, *, name):
    R, C = src.shape
    tr = _slab_rows(R, C, 2)

    def body(s_ref, x_ref, o_ref):
        o_ref[...] = x_ref[...].astype(o_ref.dtype)

    return pl.pallas_call(
        body, name=name,
        grid_spec=pltpu.PrefetchScalarGridSpec(
            num_scalar_prefetch=1, grid=(R // tr,),
            in_specs=[pl.BlockSpec((tr, C), lambda i, s: (i, 0))],
            out_specs=pl.BlockSpec((None, tr, C), lambda i, s: (s[0], i, 0))),
        out_shape=jax.ShapeDtypeStruct((nslots, R, C), dtype),
        compiler_params=_cparams(("arbitrary",)),
    )(_scalars(slot), src)


def _pair_sum(g, got, c, *, name):
    _, R, C = g.shape
    hr = R // 2
    tr = _slab_rows(hr, C, 3)
    nblk = hr // tr

    def body(s_ref, g_ref, r_ref, o_ref):
        o_ref[...] = (g_ref[...].astype(F32) + r_ref[...].astype(F32)).astype(o_ref.dtype)

    return pl.pallas_call(
        body, name=name,
        grid_spec=pltpu.PrefetchScalarGridSpec(
            num_scalar_prefetch=1, grid=(4, nblk),
            in_specs=[pl.BlockSpec((None, tr, C), lambda j, i, s: (j, s[0] * nblk + i, 0)),
                      pl.BlockSpec((None, tr, C), lambda j, i, s: (j, i, 0))],
            out_specs=pl.BlockSpec((None, tr, C), lambda j, i, s: (j, i, 0))),
        out_shape=jax.ShapeDtypeStruct((4, hr, C), g.dtype),
        compiler_params=_cparams(("arbitrary", "arbitrary")),
    )(_scalars(c), g, got)


def _chip_sum(p, landed, me_chip, c, *, name):
    _, hr, C = p.shape
    tr = _slab_rows(hr, C, 5)

    def body(s_ref, p_ref, l0_ref, l1_ref, l2_ref, o_ref):
        o_ref[...] = ((p_ref[...].astype(F32) + l0_ref[...].astype(F32)) + l1_ref[...].astype(F32)) + l2_ref[...].astype(F32)

    return pl.pallas_call(
        body, name=name,
        grid_spec=pltpu.PrefetchScalarGridSpec(
            num_scalar_prefetch=1, grid=(hr // tr,),
            in_specs=[pl.BlockSpec((None, tr, C), lambda i, s: (s[0], i, 0))]
            + [pl.BlockSpec((None, tr, C), functools.partial(lambda i, s, k: (k, i, 0), k=k)) for k in range(3)],
            out_specs=pl.BlockSpec((None, tr, C), lambda i, s: (s[1], i, 0))),
        out_shape=jax.ShapeDtypeStruct((2, hr, C), F32),
        compiler_params=_cparams(("arbitrary",)),
    )(_scalars(me_chip, c), p, landed, landed, landed)


def _place():
    return lax.axis_index("x"), lax.axis_index("y"), lax.axis_index("c")


def _other_chips(x, y):
    chips = [(1 - x, y), (x, 1 - y), (1 - x, 1 - y)]
    return chips, [2 * cx + cy for cx, cy in chips]


HBM = pl.BlockSpec(memory_space=pl.ANY)


def _allgather8(v, *, name):
    rows, cols = v.shape

    def body(v_ref, out_ref, send_sems, recv_sems):
        x, y, c = _place()
        me = 4 * x + 2 * y + c
        out_ref[me] = v_ref[...]
        copies = []
        for k in range(1, 8):
            bx, by, bc = (k >> 2) & 1, (k >> 1) & 1, k & 1
            px, py, pc = x ^ bx, y ^ by, c ^ bc
            cp = pltpu.make_async_remote_copy(
                src_ref=v_ref, dst_ref=out_ref.at[me], send_sem=send_sems.at[k - 1], recv_sem=recv_sems.at[k - 1],
                device_id=(px, py, pc), device_id_type=MESH)
            cp.start()
            copies.append((cp, 4 * px + 2 * py + pc))
        for k, (cp, peer) in enumerate(copies):
            pltpu.make_async_remote_copy(
                src_ref=v_ref, dst_ref=out_ref.at[peer], send_sem=send_sems.at[k], recv_sem=recv_sems.at[k],
                device_id=(x, y, c), device_id_type=MESH).wait_recv()
        for cp, _ in copies:
            cp.wait_send()

    return pl.pallas_call(
        body, name=name, out_shape=jax.ShapeDtypeStruct((8, rows, cols), v.dtype),
        in_specs=[pl.BlockSpec(memory_space=pltpu.VMEM)], out_specs=pl.BlockSpec(memory_space=pltpu.VMEM),
        scratch_shapes=[pltpu.SemaphoreType.DMA((7,)), pltpu.SemaphoreType.DMA((7,))],
        compiler_params=pltpu.CompilerParams(vmem_limit_bytes=VMEM_LIMIT),
    )(v)


def _allgather_shards(bufs, *, name):
    n = len(bufs)

    def body(*refs):
        outs = refs[n:2 * n]
        send_sems, recv_sems = refs[2 * n:]
        x, y, c = _place()
        me_chip = 2 * x + y
        sibling = (x, y, 1 - c)
        chips, chip_ids = _other_chips(x, y)

        def remote(k, j, blk, hf, to):
            hr = bufs[k].shape[1] // 2
            piece = outs[k].at[blk, pl.ds(pl.multiple_of(hf * hr, 16), hr), :]
            return pltpu.make_async_remote_copy(
                src_ref=piece, dst_ref=piece, send_sem=send_sems.at[6 * k + j], recv_sem=recv_sems.at[6 * k + j],
                device_id=to, device_id_type=MESH)

        sends = []
        for k in range(n):
            for j, chip in enumerate(chips):
                cp = remote(k, j, me_chip, c, (*chip, c))
                cp.start()
                sends.append(cp)
        for k in range(n):
            for j, chip in enumerate(chips):
                remote(k, j, chip_ids[j], c, (x, y, c)).wait_recv()
                cp = remote(k, 3 + j, chip_ids[j], c, sibling)
                cp.start()
                sends.append(cp)
        for k in range(n):
            for j in range(3):
                remote(k, 3 + j, chip_ids[j], 1 - c, (x, y, c)).wait_recv()
        for cp in sends:
            cp.wait_send()

    return list(pl.pallas_call(
        body, name=name, out_shape=[jax.ShapeDtypeStruct(b.shape, b.dtype) for b in bufs],
        in_specs=[HBM] * n, out_specs=[HBM] * n, input_output_aliases={k: k for k in range(n)},
        scratch_shapes=[pltpu.SemaphoreType.DMA((6 * n,)), pltpu.SemaphoreType.DMA((6 * n,))],
    )(*bufs))


HBM_SPEC = pl.BlockSpec(memory_space=pltpu.HBM)
SEM_SPEC = pl.BlockSpec(memory_space=pltpu.SEMAPHORE)
EFFECT = pltpu.SideEffectType.DATAFLOW_SIDE_EFFECTING
TOKEN = jax.ShapeDtypeStruct((SUBLANES, LANES), F32)


def _in_hbm(a):
    return pltpu.with_memory_space_constraint(a, pltpu.HBM)


def _half_rows(buf, hf):
    hr = buf.shape[1] // 2
    return pl.ds(pl.multiple_of(hf * hr, 16), hr)


def _plan_ag_ici(refs):
    x, y, c = _place()
    chips, ids = _other_chips(x, y)
    out = []
    for r in refs:
        mine = r.at[2 * x + y, _half_rows(r, c), :]
        out += [(mine, mine, r.at[ids[j], _half_rows(r, c), :], (*chip, c)) for j, chip in enumerate(chips)]
    return out


def _plan_ag_pair(refs):
    x, y, c = _place()
    _, ids = _other_chips(x, y)
    out = []
    for r in refs:
        for j in range(3):
            piece = r.at[ids[j], _half_rows(r, c), :]
            out.append((piece, piece, r.at[ids[j], _half_rows(r, 1 - c), :], (x, y, 1 - c)))
    return out


def _plan_rs_ici(refs):
    x, y, c = _place()
    chips, ids = _other_chips(x, y)
    n = len(refs) // 2
    return [(refs[k].at[ids[j]], refs[n + k].at[j], refs[n + k].at[j], (*chip, c))
            for k in range(n) for j, chip in enumerate(chips)]


def _plan_pair_exchange(refs):
    x, y, c = _place()
    n = len(refs) // 2
    return [(refs[k].at[:, _half_rows(refs[k], 1 - c), :], refs[n + k], refs[n + k], (x, y, 1 - c)) for k in range(n)]


def _plan_pair_gather(refs):
    x, y, c = _place()
    return [(r.at[c], r.at[c], r.at[1 - c], (x, y, 1 - c)) for r in refs]


def _remote(src, dst, send_sem, recv_sem, target):
    return pltpu.make_async_remote_copy(src_ref=src, dst_ref=dst, send_sem=send_sem, recv_sem=recv_sem,
                                        device_id=target, device_id_type=MESH)


def _copy_start(groups, *, name):
    flat = [a for arrays, _, _ in groups for a in arrays]
    n, ng = len(flat), len(groups)

    def body(*refs):
        sems = refs[n:n + 2 * ng]
        thru = refs[n + 2 * ng:2 * n + 2 * ng]
        token = refs[-1]
        pos = 0
        for g, (arrays, plan, n_copies) in enumerate(groups):
            copies = plan(thru[pos:pos + len(arrays)])
            pos += len(arrays)
            assert len(copies) == n_copies
            for i, (src, dst, _, target) in enumerate(copies):
                _remote(src, dst, sems[2 * g].at[i], sems[2 * g + 1].at[i], target).start()
        token[...] = jnp.zeros_like(token)

    out_shape = tuple(pltpu.SemaphoreType.DMA((n_copies,)) for _, _, n_copies in groups for _ in range(2))
    out_shape += tuple(pltpu.HBM(a.shape, a.dtype) for a in flat) + (TOKEN,)
    res = pl.pallas_call(
        body, name=name, out_shape=out_shape, in_specs=(HBM_SPEC,) * n,
        out_specs=(SEM_SPEC,) * (2 * ng) + (HBM_SPEC,) * n + (pl.BlockSpec(memory_space=pltpu.VMEM),),
        input_output_aliases={k: 2 * ng + k for k in range(n)},
        compiler_params=pltpu.CompilerParams(has_side_effects=EFFECT),
    )(*[_in_hbm(a) for a in flat])
    sems = [(res[2 * g], res[2 * g + 1]) for g in range(ng)]
    thru, pos = [], 2 * ng
    for arrays, _, _ in groups:
        thru.append(list(res[pos:pos + len(arrays)]))
        pos += len(arrays)
    return sems, thru, res[-1]


def _copy_wait(arrays, sems, plan, n_copies, after, *, name):
    n = len(arrays)
    after = list(after)

    def body(*refs):
        send, recv = refs[n], refs[n + 1]
        x, y, c = _place()
        copies = plan(refs[:n])
        assert len(copies) == n_copies
        for i, (src, dst, landing, target) in enumerate(copies):
            _remote(src, dst, send.at[i], recv.at[i], target).wait_send()
            _remote(landing, landing, send.at[i], recv.at[i], (x, y, c)).wait_recv()

    return list(pl.pallas_call(
        body, name=name, out_shape=tuple(pltpu.HBM(a.shape, a.dtype) for a in arrays),
        in_specs=(HBM_SPEC,) * n + (SEM_SPEC, SEM_SPEC) + (pl.BlockSpec(memory_space=pl.ANY),) * len(after),
        out_specs=(HBM_SPEC,) * n, input_output_aliases={k: k for k in range(n)},
        compiler_params=pltpu.CompilerParams(has_side_effects=EFFECT),
    )(*arrays, *sems, *after))


def _rs_stage1(gs, tag):
    n = len(gs)
    lands = [lax.empty((4, g.shape[1] // 2, g.shape[2]), g.dtype) for g in gs]
    sems, (arrays,), token = _copy_start([(list(gs) + lands, _plan_pair_exchange, n)], name=f"rs_pair_start_{tag}")
    return (sems[0], arrays), token


def _rs_stage2(handle, after, tag):
    sems, arrays = handle
    n = len(arrays) // 2
    arrays = _copy_wait(arrays, sems, _plan_pair_exchange, n, after, name=f"rs_pair_wait_{tag}")
    c = lax.axis_index("c")
    pair = [_pair_sum(g, r, c, name=f"rs_pair_sum_{tag}{k}") for k, (g, r) in enumerate(zip(arrays[:n], arrays[n:]))]
    lands = [lax.empty((3,) + p.shape[1:], p.dtype) for p in pair]
    sems, (arrays,), token = _copy_start([(pair + lands, _plan_rs_ici, 3 * n)], name=f"rs_start_{tag}")
    return (sems[0], arrays), token


def _rs_stage3(handle, after, tag):
    sems, arrays = handle
    n = len(arrays) // 2
    arrays = _copy_wait(arrays, sems, _plan_rs_ici, 3 * n, after, name=f"rs_wait_{tag}")
    x, y, c = _place()
    halves = [_chip_sum(p, l, 2 * x + y, c, name=f"rs_chip_sum_{tag}{k}") for k, (p, l) in enumerate(zip(arrays[:n], arrays[n:]))]
    sems, (halves,), token = _copy_start([(halves, _plan_pair_gather, n)], name=f"rs_gather_start_{tag}")
    return (sems[0], halves), token


def _rs_stage4(handle, after, tag):
    sems, halves = handle
    full = _copy_wait(halves, sems, _plan_pair_gather, len(halves), after, name=f"rs_gather_wait_{tag}")
    return [f.reshape(2 * f.shape[1], f.shape[2]) for f in full]


def _to_segments(a):
    rows = a.shape[0]
    return a.reshape(N_SEG, rows // N_SEG, -1).transpose(1, 0, 2).reshape(rows, -1)


def _from_segments(a):
    rows = a.shape[0]
    return a.reshape(rows // N_SEG, N_SEG, -1).transpose(1, 0, 2).reshape(rows, -1)


def _rope_tables(L):
    t = jnp.arange(L, dtype=jnp.int32)
    row = (t // GRID_W).astype(F32)
    col = (t % GRID_W).astype(F32)
    n_freq = QK_ROPE // 4
    inv = ROPE_BASE ** (-jnp.arange(n_freq, dtype=F32) / n_freq)
    a0, a1 = row[:, None] * inv, col[:, None] * inv
    z = jnp.zeros((L, LANES - QK_ROPE), F32)
    cos = jnp.concatenate([jnp.cos(a0), jnp.cos(a0), jnp.cos(a1), jnp.cos(a1), z], axis=1)
    sin = jnp.concatenate([-jnp.sin(a0), jnp.sin(a0), -jnp.sin(a1), jnp.sin(a1), z], axis=1)
    return _to_segments(cos), _to_segments(sin)


def _col_blocks(w, nblk):
    r, c = w.shape
    return w.reshape(r, nblk, c // nblk).transpose(1, 0, 2)


def _from_col_blocks(w4):
    nblk, r, c = w4.shape
    return w4.transpose(1, 0, 2).reshape(r, nblk * c)


def _s5_discretize(a_re, a_im, log_dt, b_re, b_im):
    dt = jnp.exp(log_dt)[:, None]
    mag = jnp.exp(a_re * dt)
    ab_re, ab_im = mag * jnp.cos(a_im * dt), mag * jnp.sin(a_im * dt)
    den = a_re * a_re + a_im * a_im
    nr, ni = ab_re - 1.0, ab_im
    co_re = (nr * a_re + ni * a_im) / den
    co_im = (ni * a_re - nr * a_im) / den
    bb_re = co_re[..., None] * b_re - co_im[..., None] * b_im
    bb_im = co_re[..., None] * b_im + co_im[..., None] * b_re
    return ab_re, ab_im, bb_re, bb_im


def _diag_blocks_in(bb, gpb):
    G, N, P = bb.shape
    t = jnp.tile(jnp.swapaxes(bb, 1, 2).reshape(G // gpb, gpb * P, N), (1, 1, gpb))
    row = lax.broadcasted_iota(jnp.int32, t.shape, 1) // P
    col = lax.broadcasted_iota(jnp.int32, t.shape, 2) // N
    return jnp.where(row == col, t, 0.0)


def _diag_blocks_out(cc, gpb):
    G, P, N = cc.shape
    t = jnp.tile(jnp.swapaxes(cc, 1, 2).reshape(G // gpb, gpb * N, P), (1, 1, gpb))
    row = lax.broadcasted_iota(jnp.int32, t.shape, 1) // N
    col = lax.broadcasted_iota(jnp.int32, t.shape, 2) // P
    return jnp.where(row == col, t, 0.0)


def _tr(ws):
    return [jnp.swapaxes(w, 1, 2) for w in ws]


WEIGHTS = ['c_ctx', 'w_mod', 'b_mod', 'norm1', 'norm2', 'w_in', 's5_a_re', 's5_a_im', 's5_log_dt', 's5_b_re', 's5_b_im',
           's5_c_re', 's5_c_im', 's5_d', 'w_glu', 'q_norm', 'kv_norm', 'w_uq', 'w_ukv', 'w_mla_o', 'w_out', 'w_ffn_in',
           'w_ffn_out', 'norm_f']
AG_GROUPS = [['w_in'], ['w_glu', 'w_uq', 'w_ukv', 'w_mla_o', 'w_out'], ['w_ffn_in', 'w_ffn_out']]
SMALL = ['norm1', 'norm2', 's5_a_re', 's5_a_im', 's5_log_dt', 's5_b_re', 's5_b_im', 's5_c_re', 's5_c_im', 's5_d',
         'q_norm', 'kv_norm', 'norm_f']


def _pad_rows(a, rows):
    return jnp.concatenate([a, jnp.zeros((rows - a.shape[0],) + a.shape[1:], a.dtype)], axis=0)


def _pack(vals, width, rows):
    flat = jnp.concatenate([v.reshape(-1).astype(F32) for v in vals])
    flat = jnp.concatenate([flat, jnp.zeros((rows * width - flat.shape[0],), F32)])
    return flat.reshape(rows, width)


def _unpack(buf, like):
    flat = buf.reshape(-1)
    out, pos = [], 0
    for v in like:
        out.append(flat[pos:pos + v.size].reshape(v.shape))
        pos += v.size
    return out


def _step(x, c, ctx, loss_target, w, m, v):
    px, py, pc = _place()
    me = 4 * px + 2 * py + pc
    me_chip = 2 * px + py
    L, D = x.shape[1], x.shape[2]
    Lc = ctx.shape[1]
    T = L + Lc
    SW = D // 2
    G = SW // S5_GROUP
    C = G * S5_STATE
    H = MLA_HEADS
    q_rank = w['q_norm'].shape[1]
    kv_rank = w['kv_norm'].shape[1]
    d_ff = w['w_ffn_out'].shape[1] * 4
    wa_used = SW + q_rank + kv_rank + QK_ROPE
    WA = -(-(SW + q_rank + kv_rank + LANES) // 512) * 512

    c_rows = _pad_rows(c.astype(F32), SUBLANES)
    c_all = _allgather8(c_rows, name="ag_cond")[:, 0, :]
    cond = jnp.concatenate([c_all, w['c_ctx'].reshape(1, D)], axis=0)
    cond = _pad_rows(cond, 16)
    (act,) = _rw(lambda t: (jax.nn.silu(t),), [cond], [], [F32], name="cond_silu")
    w_mod, cs_mod = w['w_mod'][0], w['w_mod'].shape[2]
    mod_part = _mm(act, w_mod, out_dtype=F32, name="mod_fwd")
    mod_all = _allgather8(mod_part, name="ag_mod")
    mod_full = jnp.concatenate([mod_all[0], mod_all[2], mod_all[4], mod_all[6]], axis=1) + w['b_mod']
    m_lat = lax.dynamic_slice_in_dim(mod_full, me, 1, axis=0).reshape(6, D)
    m_ctx = mod_full[8].reshape(6, D)
    sh1, sc1, g1, sh2, sc2, g2 = (m_lat[i:i + 1] for i in range(6))
    csh1, csc1 = m_ctx[0:1], m_ctx[1:2]

    ag_groups = [([_into_slot(w[nme][0], me_chip, 4, BF16, name=f"cast_{nme}") for nme in grp], _plan_ag_ici, 3 * len(grp))
                 for grp in AG_GROUPS]
    ag_sems, ag_bufs, ag_token = _copy_start(ag_groups, name="ag_start")
    gathered, ag_pair = {}, {}

    def landed(g, after):
        n_cp = 3 * len(AG_GROUPS[g])
        got = _copy_wait(ag_bufs[g], ag_sems[g], _plan_ag_ici, n_cp, after, name=f"ag_wait_{g}")
        sems, (got,), token = _copy_start([(got, _plan_ag_pair, n_cp)], name=f"ag_pair_start_{g}")
        ag_pair[g] = (sems[0], got)
        return token[0, 0]

    def arrive(g, after):
        sems, got = ag_pair[g]
        got = _copy_wait(got, sems, _plan_ag_pair, 3 * len(AG_GROUPS[g]), after, name=f"ag_pair_wait_{g}")
        gathered.update(zip(AG_GROUPS[g], got))

    xs = _to_segments(x[0])
    cs = _to_segments(ctx[0])
    tgt = _to_segments(loss_target[0])
    cos, sin = _rope_tables(L)
    n1, n2, nf = w['norm1'], w['norm2'], w['norm_f'].reshape(1, D)
    qg, kvg = w['q_norm'], w['kv_norm']

    (xn_lat,) = _rw(_f_norm_mod, [xs], [n1 + ag_token[0, 0], sc1, sh1], [BF16], name="norm1_lat")
    (xn_ctx,) = _rw(_f_norm_mod, [cs], [n1, csc1, csh1], [BF16], name="norm1_ctx")
    xn = jnp.concatenate([xn_lat, xn_ctx], axis=0)
    landed(0, [xn])

    gpb = min(S5_BLOCK_GROUPS, G)
    gpo = min(8, G)
    d_skip = w['s5_d'][0].reshape(1, SW)
    disc, vjp_disc, w_b, w_c = [], [], [], []
    for d in range(2):
        prm = (w['s5_a_re'][0, d], w['s5_a_im'][0, d], w['s5_log_dt'][0, d], w['s5_b_re'][0, d], w['s5_b_im'][0, d])

        def prep(a_re, a_im, log_dt, b_re, b_im):
            ab_re, ab_im, bb_re, bb_im = _s5_discretize(a_re, a_im, log_dt, b_re, b_im)
            return ab_re.reshape(1, C), ab_im.reshape(1, C), _diag_blocks_in(bb_re, gpb), _diag_blocks_in(bb_im, gpb)

        out, vj = jax.vjp(prep, *prm)
        disc.append(out)
        vjp_disc.append(vj)
        w_b += [out[2], out[3]]
        w_c += [_diag_blocks_out(w['s5_c_re'][0, d], gpo), -_diag_blocks_out(w['s5_c_im'][0, d], gpo)]
    nb_in = G // gpb
    nb_out = G // gpo

    arrive(0, [xn, tgt] + w_b + w_c)
    w_in = _from_col_blocks(gathered['w_in'])
    w_a = jnp.concatenate([w_in[:, :wa_used], jnp.zeros((D, WA - wa_used), BF16)], axis=1)
    w_g = w_in[:, wa_used:]
    ha = _mm(xn, w_a, out_dtype=F32, name="in_proj")
    ha_lat, ha_ctx = ha[:L], ha[L:]
    gt = _mm(xn_lat, w_g, out_dtype=F32, name="in_gates")
    f_post_lat = _make_f_post_in(SW, q_rank, kv_rank, True)
    f_post_ctx = _make_f_post_in(SW, q_rank, kv_rank, False)
    u_lat, cqn, ckvn_lat, kr_lat = _rw(f_post_lat, [ha_lat, cos, sin], [qg, kvg], [F32, BF16, BF16, BF16], name="post_in_lat")
    u_ctx, ckvn_ctx, kr_ctx = _rw(f_post_ctx, [ha_ctx], [kvg], [F32, BF16, BF16], name="post_in_ctx")
    zero = jnp.zeros((1, C), F32) + landed(1, [u_lat, u_ctx])

    h_lat, h_ctx, hT_ctx = [], [], []
    for d, rev in enumerate((False, True)):
        lr, li = disc[d][0], disc[d][1]
        hcr, hci, tr, ti = _s5_scan(u_ctx, w_b[2 * d], w_b[2 * d + 1], lr, li, zero, zero, zero, zero, reverse=rev,
                                    name=f"s5_scan_ctx_{d}")
        hlr, hli, _, _ = _s5_scan(u_lat, w_b[2 * d], w_b[2 * d + 1], lr, li, tr, ti, zero, zero, reverse=rev,
                                  name=f"s5_scan_lat_{d}")
        h_ctx += [hcr, hci]
        h_lat += [hlr, hli]
        hT_ctx += [tr, ti]
    r5 = _bd_fanin(h_lat, w_c, name="s5_readout")
    (z,) = _rw(_f_s5post, [u_lat, r5], [d_skip], [BF16], name="s5_post")

    arrive(1, [z])
    w_glu, w_ukv, w_mla_o = (gathered[nme] for nme in ('w_glu', 'w_ukv', 'w_mla_o'))
    w_out = gathered['w_out'].reshape(D, D)
    uq3 = _from_col_blocks(gathered['w_uq']).reshape(q_rank, H, QK_NOPE + QK_ROPE)
    w_q2 = jnp.concatenate([uq3, jnp.zeros((q_rank, H, LANES - QK_ROPE), BF16)], axis=2).reshape(q_rank, H * 2 * LANES)
    q2 = _mm(cqn, w_q2, out_dtype=F32, name="q_up")
    (qq,) = _rw(_f_qpost, [q2, cos, sin], [], [BF16], name="q_rope")
    kvn = jnp.concatenate([ckvn_lat, ckvn_ctx], axis=0)
    kr_all = jnp.concatenate([kr_lat, kr_ctx], axis=0)
    kv = _mm(kvn, w_ukv, b_shards=4, out_dtype=BF16, name="kv_up")
    kr_all = kr_all + landed(2, [kv, qq]).astype(BF16)
    o = _attn_fwd(qq, kv, kr_all, name="attn_fwd")

    ab = _mm(z, w_glu, b_shards=4, out_dtype=F32, name="glu_proj")
    bm = _mm(o, w_mla_o, b_shards=4, out_dtype=F32, name="mla_out")
    (mix,) = _rw(_f_merge, [ab, bm, gt], [], [BF16], name="merge")
    out1 = _mm(mix, w_out, out_dtype=F32, name="out_proj")
    x1, xn2 = _rw(_f_resid_norm, [xs, out1], [g1, n2, sc2, sh2], [F32, BF16], name="resid_norm2")
    arrive(2, [xn2])
    w_ffn_in = gathered['w_ffn_in']
    w_ffn_out = gathered['w_ffn_out'].reshape(d_ff, D)
    ab2 = _mm(xn2, w_ffn_in, b_shards=4, out_dtype=F32, name="ffn_in")
    (hmid,) = _rw(_f_swiglu, [ab2], [], [BF16], name="ffn_act")
    f2 = _mm(hmid, w_ffn_out, out_dtype=F32, name="ffn_out")
    (row_loss,) = _rw(_f_final, [x1, f2, tgt], [g2, nf], [F32], name="final_loss")
    loss = lax.psum(jnp.sum(row_loss), ("x", "y", "c"))

    ones = jnp.ones((L, 1), F32)
    (dx1_a, df2), (dg2, dnf) = _rw_vjp(_f_final, [x1, f2, tgt], [g2, nf], [[ones]], [True, True, False], [True, True],
                                       [F32, BF16], name="final_loss_bwd")
    dhmid = _mm(df2, w_ffn_out, tb=True, out_dtype=F32, name="ffn_out_dx")
    gw_ffn_out = _mm(hmid, df2, ta=True, out_dtype=BF16, name="ffn_out_dw")
    (dab2,), _ = _rw_vjp(_f_swiglu, [ab2], [], [[dhmid]], [True], [], [BF16], name="ffn_act_bwd")
    dxn2 = _mm(dab2, w_ffn_in, tb=True, b_shards=4, out_dtype=F32, name="ffn_in_dx")
    gw_ffn_in = _mm(xn2, dab2, ta=True, out_shards=4, out_dtype=BF16, name="ffn_in_dw")
    rs_ffn, tok = _rs_stage1([gw_ffn_out.reshape(4, -1, D), gw_ffn_in], "ffn")
    (dx_a, dout1), (dg1, dn2, dsc2, dsh2) = _rw_vjp(
        _f_resid_norm, [xs, out1], [g1, n2 + tok[0, 0], sc2, sh2], [[dx1_a], [dxn2]], [True, True], [True] * 4, [F32, BF16],
        name="resid_norm2_bwd")
    dmix = _mm(dout1, w_out, tb=True, out_dtype=F32, name="out_proj_dx")
    gw_out = _mm(mix, dout1, ta=True, out_dtype=BF16, name="out_proj_dw")
    (dab, dbm, dgt), _ = _rw_vjp(_f_merge, [ab, bm, gt], [], [[dmix]], [True] * 3, [], [BF16] * 3, name="merge_bwd")
    dz = _mm(dab, w_glu, tb=True, b_shards=4, out_dtype=F32, name="glu_proj_dx")
    gw_glu = _mm(z, dab, ta=True, out_shards=4, out_dtype=BF16, name="glu_proj_dw")
    do = _mm(dbm, w_mla_o, tb=True, b_shards=4, out_dtype=BF16, name="mla_out_dx")
    gw_mla_o = _mm(o, dbm, ta=True, out_shards=4, out_dtype=BF16, name="mla_out_dw")
    dxn_g = _mm(dgt, w_g, tb=True, out_dtype=F32, name="in_gates_dx")
    gw_g = _mm(xn_lat, dgt, ta=True, out_dtype=BF16, name="in_gates_dw")
    rs_ffn, tok = _rs_stage2(rs_ffn, [gw_g], "ffn")

    (du_a, dr5), (dd_skip,) = _rw_vjp(_f_s5post, [u_lat, r5], [d_skip + tok[0, 0]], [[dz]], [True, True], [True], [F32, F32],
                                      name="s5_post_bwd")
    dw_c = _bd_dw(h_lat, [dr5] * 4, nb_out, name="s5_readout_dw")
    w_ct = _tr(w_c)
    zeros_ctx = jnp.zeros((Lc, SW), BF16)
    mu_lat, mu_ctx, dlam = [], [], []
    for d, rev in enumerate((False, True)):
        lr, li = disc[d][0], disc[d][1]
        mlr, mli, fr, fi = _s5_scan(dr5, w_ct[2 * d], w_ct[2 * d + 1], lr, -li, zero, zero, zero, zero, reverse=not rev,
                                    name=f"s5_adj_lat_{d}")
        dh0r, dh0i = _cmul(lr, -li, fr, fi)
        mcr, mci, _, _ = _s5_scan(zeros_ctx, w_ct[2 * d], w_ct[2 * d + 1], lr, -li, zero, zero, dh0r, dh0i,
                                  reverse=not rev, name=f"s5_adj_ctx_{d}")
        dl_lat = _s5_dlam(mlr, mli, h_lat[2 * d], h_lat[2 * d + 1], hT_ctx[2 * d], hT_ctx[2 * d + 1], reverse=rev,
                          name=f"s5_dlam_lat_{d}")
        dl_ctx = _s5_dlam(mcr, mci, h_ctx[2 * d], h_ctx[2 * d + 1], zero, zero, reverse=rev, name=f"s5_dlam_ctx_{d}")
        mu_lat += [mlr, mli]
        mu_ctx += [mcr, mci]
        dlam.append((dl_lat[0] + dl_ctx[0], dl_lat[1] + dl_ctx[1]))
    du_b = _bd_fanin(mu_lat, _tr(w_b), name="s5_bu_lat_dx")
    du_ctx = _bd_fanin(mu_ctx, _tr(w_b), name="s5_bu_ctx_dx")
    dw_b_lat = _bd_dw([u_lat] * 4, mu_lat, nb_in, name="s5_bu_lat_dw")
    dw_b_ctx = _bd_dw([u_ctx] * 4, mu_ctx, nb_in, name="s5_bu_ctx_dw")
    g_s5 = {}
    for d in range(2):
        ct = (dlam[d][0], dlam[d][1], dw_b_lat[2 * d] + dw_b_ctx[2 * d], dw_b_lat[2 * d + 1] + dw_b_ctx[2 * d + 1])
        ga_re, ga_im, gdt, gb_re, gb_im = vjp_disc[d](ct)
        _, vj_c = jax.vjp(lambda cr, ci: (_diag_blocks_out(cr, gpo), -_diag_blocks_out(ci, gpo)),
                          w['s5_c_re'][0, d], w['s5_c_im'][0, d])
        gc_re, gc_im = vj_c((dw_c[2 * d], dw_c[2 * d + 1]))
        for nme, val in (('s5_a_re', ga_re), ('s5_a_im', ga_im), ('s5_log_dt', gdt), ('s5_b_re', gb_re),
                         ('s5_b_im', gb_im), ('s5_c_re', gc_re), ('s5_c_im', gc_im)):
            g_s5.setdefault(nme, []).append(val)
    g_small = {nme: jnp.stack(vals)[None] for nme, vals in g_s5.items()}
    g_small['s5_d'] = dd_skip.reshape(w['s5_d'].shape)

    dqq, dkv, dkr = _attn_bwd(qq, kv, kr_all, do, name="attn_bwd")
    (dq2,), _ = _rw_vjp(_f_qpost, [q2, cos, sin], [], [[dqq]], [True, False, False], [], [BF16], name="q_rope_bwd")
    dcqn = _mm(dq2, w_q2, tb=True, out_dtype=F32, name="q_up_dx")
    gw_q2 = _mm(cqn, dq2, ta=True, out_dtype=BF16, name="q_up_dw")
    dckvn = _mm(dkv, w_ukv, tb=True, b_shards=4, out_dtype=F32, name="kv_up_dx")
    gw_ukv = _mm(kvn, dkv, ta=True, out_shards=4, out_dtype=BF16, name="kv_up_dw")
    gw_uq = gw_q2.reshape(q_rank, H, 2 * LANES)[:, :, :QK_NOPE + QK_ROPE].reshape(q_rank, H * (QK_NOPE + QK_ROPE))
    rs_mix, tok = _rs_stage1([gw_out.reshape(4, -1, D), gw_glu, gw_mla_o, _col_blocks(gw_uq, 4), gw_ukv], "mix")

    (dha_lat,), (dqg, dkvg_lat) = _rw_vjp(
        f_post_lat, [ha_lat, cos, sin], [qg, kvg + tok[0, 0]], [[du_a, du_b], [dcqn], [dckvn[:L]], [dkr[:L]]],
        [True, False, False], [True, True], [BF16], name="post_in_lat_bwd")
    (dha_ctx,), (dkvg_ctx,) = _rw_vjp(f_post_ctx, [ha_ctx], [kvg], [[du_ctx], [dckvn[L:]], [dkr[L:]]], [True], [True],
                                      [BF16], name="post_in_ctx_bwd")
    dha = jnp.concatenate([dha_lat, dha_ctx], axis=0)
    dxn = _mm(dha, w_a, tb=True, out_dtype=F32, name="in_proj_dx")
    gw_a = _mm(xn, dha, ta=True, out_dtype=BF16, name="in_proj_dw")
    rs_mix, tok = _rs_stage2(rs_mix, [gw_a], "mix")
    (dx_seg,), (dn1_lat, dsc1, dsh1) = _rw_vjp(
        _f_norm_mod_keep, [xs], [n1 + tok[0, 0], sc1, sh1], [[dxn[:L], dxn_g], [dx_a]], [True], [True] * 3, [F32],
        name="norm1_lat_bwd")
    _, (dn1_ctx, dcsc1, dcsh1) = _rw_vjp(_f_norm_mod, [cs], [n1, csc1, csh1], [[dxn[L:]]], [False], [True] * 3, [],
                                         name="norm1_ctx_bwd")
    grad_x = _from_segments(dx_seg)[None]
    g_small.update(norm1=dn1_lat + dn1_ctx, norm2=dn2, q_norm=dqg, kv_norm=dkvg_lat + dkvg_ctx, norm_f=dnf.reshape(D))
    gw_in = jnp.concatenate([gw_a[:, :wa_used], gw_g], axis=1)
    small_vals = [g_small[nme] for nme in SMALL]
    n_small = sum(val.size for val in small_vals)
    small_rows = -(-n_small // (LANES * 4 * 32)) * 32
    rs_in, tok = _rs_stage1([_col_blocks(gw_in, 4), _pack(small_vals, LANES, 4 * small_rows).reshape(4, small_rows, LANES)],
                            "in")

    zD = jnp.zeros((1, D), F32)
    dm = jnp.concatenate([
        jnp.concatenate([dsh1, dsc1, dg1, dsh2, dsc2, dg2], axis=1),
        jnp.concatenate([dcsh1, dcsc1, zD, zD, zD, zD], axis=1),
    ], axis=0) + tok[0, 0]
    dm_all = _allgather8(_pad_rows(dm, SUBLANES), name="ag_dmod")
    dm_ctx = dm_all[0, 1]
    for k in range(1, 8):
        dm_ctx = dm_ctx + dm_all[k, 1]
    dmod = _pad_rows(jnp.concatenate([dm_all[:, 0, :], dm_ctx[None]], axis=0), 16)
    g_b_mod = jnp.sum(dmod, axis=0, keepdims=True)
    dmod_mine = lax.dynamic_slice_in_dim(dmod, me_chip * cs_mod, cs_mod, axis=1)
    g_w_mod = _mm(act, dmod_mine, ta=True, out_dtype=F32, name="mod_dw")
    dact_part = _mm(dmod_mine, w_mod, tb=True, out_dtype=F32, name="mod_dx")
    dact_all = _allgather8(dact_part, name="ag_dact")
    dact = dact_all[0] + dact_all[2] + dact_all[4] + dact_all[6]
    (dcond_rows,), _ = _rw_vjp(lambda t: (jax.nn.silu(t),), [cond], [], [[dact]], [True], [], [F32], name="cond_silu_bwd")
    g_c_ctx = dcond_rows[8]

    rs_in, tok = _rs_stage2(rs_in, [g_c_ctx], "in")

    grads, delta, new_m, new_v = {}, {}, {}, {}

    def update(members, reds, anchor):
        deltas = []
        for nme, red in zip(members, reds):
            res = _adamw(w[nme][0], red, m[nme][0], v[nme][0], name=f"adamw_{nme}", anchor=anchor)
            grads[nme], delta[nme], new_m[nme], new_v[nme] = (r.reshape(w[nme].shape) for r in res)
            deltas.append(res[1])
            anchor = None
        return deltas

    rs_ffn, tok = _rs_stage3(rs_ffn, [tok], "ffn")
    done = update(['w_mod'], [g_w_mod], tok)
    red_ffn = _rs_stage4(rs_ffn, done, "ffn")
    rs_mix, tok = _rs_stage3(rs_mix, red_ffn[:1], "mix")
    done = update(['w_ffn_out', 'w_ffn_in'], red_ffn, tok)
    red_mix = _rs_stage4(rs_mix, done, "mix")
    rs_in, tok = _rs_stage3(rs_in, red_mix[:1], "in")
    done = update(['w_out', 'w_glu', 'w_mla_o', 'w_uq', 'w_ukv'], red_mix, tok)
    red_in = _rs_stage4(rs_in, done, "in")
    update(['w_in'], red_in[:1], None)
    small_mine = red_in[-1]
    small_buf = _into_slot(small_mine, me_chip, 4, F32, name="small_grads_slot")
    small_all = _allgather_shards([small_buf], name="ag_small_grads")[0].reshape(4 * small_rows, LANES)
    g_small_red = dict(zip(SMALL, _unpack(small_all, [w[nme] for nme in SMALL])))
    rest = SMALL + ['c_ctx', 'b_mod']
    g_rest = dict(g_small_red, c_ctx=g_c_ctx, b_mod=g_b_mod)
    rows_rest = -(-sum(w[nme].size for nme in rest) // (LANES * 16)) * 16
    packed = [_pack([src[nme] for nme in rest], LANES, rows_rest) for src in (w, g_rest, m, v)]
    res = _adamw(*packed, name="adamw_small")
    for dst, buf in zip((grads, delta, new_m, new_v), res):
        dst.update(zip(rest, _unpack(buf, [w[nme] for nme in rest])))
    return (loss, grad_x, *[grads[nme] for nme in WEIGHTS], *[delta[nme] for nme in WEIGHTS],
            *[new_m[nme] for nme in WEIGHTS], *[new_v[nme] for nme in WEIGHTS])


def kernel(x, c, ctx, c_ctx, w_mod, b_mod, norm1, norm2, w_in, s5_a_re, s5_a_im, s5_log_dt, s5_b_re, s5_b_im, s5_c_re, s5_c_im, s5_d, w_glu, q_norm, kv_norm, w_uq, w_ukv, w_mla_o, w_out, w_ffn_in, w_ffn_out, norm_f, loss_target, m_c_ctx, m_w_mod, m_b_mod, m_norm1, m_norm2, m_w_in, m_s5_a_re, m_s5_a_im, m_s5_log_dt, m_s5_b_re, m_s5_b_im, m_s5_c_re, m_s5_c_im, m_s5_d, m_w_glu, m_q_norm, m_kv_norm, m_w_uq, m_w_ukv, m_w_mla_o, m_w_out, m_w_ffn_in, m_w_ffn_out, m_norm_f, v_c_ctx, v_w_mod, v_b_mod, v_norm1, v_norm2, v_w_in, v_s5_a_re, v_s5_a_im, v_s5_log_dt, v_s5_b_re, v_s5_b_im, v_s5_c_re, v_s5_c_im, v_s5_d, v_w_glu, v_q_norm, v_kv_norm, v_w_uq, v_w_ukv, v_w_mla_o, v_w_out, v_w_ffn_in, v_w_ffn_out, v_norm_f):
    w = dict(c_ctx=c_ctx, w_mod=w_mod, b_mod=b_mod, norm1=norm1, norm2=norm2, w_in=w_in, s5_a_re=s5_a_re, s5_a_im=s5_a_im,
             s5_log_dt=s5_log_dt, s5_b_re=s5_b_re, s5_b_im=s5_b_im, s5_c_re=s5_c_re, s5_c_im=s5_c_im, s5_d=s5_d, w_glu=w_glu,
             q_norm=q_norm, kv_norm=kv_norm, w_uq=w_uq, w_ukv=w_ukv, w_mla_o=w_mla_o, w_out=w_out, w_ffn_in=w_ffn_in,
             w_ffn_out=w_ffn_out, norm_f=norm_f)
    m = dict(c_ctx=m_c_ctx, w_mod=m_w_mod, b_mod=m_b_mod, norm1=m_norm1, norm2=m_norm2, w_in=m_w_in, s5_a_re=m_s5_a_re,
             s5_a_im=m_s5_a_im, s5_log_dt=m_s5_log_dt, s5_b_re=m_s5_b_re, s5_b_im=m_s5_b_im, s5_c_re=m_s5_c_re,
             s5_c_im=m_s5_c_im, s5_d=m_s5_d, w_glu=m_w_glu, q_norm=m_q_norm, kv_norm=m_kv_norm, w_uq=m_w_uq, w_ukv=m_w_ukv,
             w_mla_o=m_w_mla_o, w_out=m_w_out, w_ffn_in=m_w_ffn_in, w_ffn_out=m_w_ffn_out, norm_f=m_norm_f)
    v = dict(c_ctx=v_c_ctx, w_mod=v_w_mod, b_mod=v_b_mod, norm1=v_norm1, norm2=v_norm2, w_in=v_w_in, s5_a_re=v_s5_a_re,
             s5_a_im=v_s5_a_im, s5_log_dt=v_s5_log_dt, s5_b_re=v_s5_b_re, s5_b_im=v_s5_b_im, s5_c_re=v_s5_c_re,
             s5_c_im=v_s5_c_im, s5_d=v_s5_d, w_glu=v_w_glu, q_norm=v_q_norm, kv_norm=v_kv_norm, w_uq=v_w_uq, w_ukv=v_w_ukv,
             w_mla_o=v_w_mla_o, w_out=v_w_out, w_ffn_in=v_w_ffn_in, w_ffn_out=v_w_ffn_out, norm_f=v_norm_f)
    return _step(x, c, ctx, loss_target, w, m, v)
```

```python
import functools
import math

import jax
import jax.numpy as jnp
from jax import lax
from jax.experimental import pallas as pl
from jax.experimental.pallas import tpu as pltpu

F32 = jnp.float32
BF16 = jnp.bfloat16

EPS = 1e-6
GRID_W = 64
S5_GROUP = 16
S5_STATE = 64
MLA_HEADS = 8
QK_NOPE = 128
QK_ROPE = 64
V_DIM = 128
ROPE_BASE = 10000.0
ATTN_SCALE = (QK_NOPE + QK_ROPE) ** -0.5
ADAM_LR = 0.001
ADAM_B1 = 0.9
ADAM_B2 = 0.999
ADAM_EPS = 1e-08
ADAM_WD = 0.01
ADAM_STEP = 10

SUBLANES = 8
LANES = 128
V7X_VMEM_BYTES = 64 * 1024 * 1024
VMEM_LIMIT = (V7X_VMEM_BYTES * 7) // 8
N_SEG = 2 * SUBLANES
S5_BLOCK_GROUPS = 8
MESH = pl.DeviceIdType.MESH


def _pick(n, target, mult):
    best = None
    d = mult
    while d <= min(n, target):
        if n % d == 0:
            best = d
        d += mult
    return n if best is None else best


def _cparams(sem=None):
    return pltpu.CompilerParams(dimension_semantics=sem, vmem_limit_bytes=VMEM_LIMIT)


MM_VMEM_BUDGET = (V7X_VMEM_BYTES * 5) // 8


def _mm(a, b, *, ta=False, tb=False, out_dtype=F32, name, a_shards=1, b_shards=1, out_shards=1):
    if ta:
        K, M = a.shape
    else:
        M, K = a.shape[-2], a.shape[-1] * a_shards
    if tb:
        N, K2 = b.shape[-2], b.shape[-1] * b_shards
    else:
        K2, N = b.shape[-2], b.shape[-1] * b_shards
    assert K == K2, (a.shape, b.shape, ta, tb)
    n_unit = N // max(out_shards, 1 if tb else b_shards)
    k_unit = K // max(a_shards, b_shards if tb else 1)
    tn = _pick(n_unit, 1024, LANES)
    tm = _pick(M, 1024 if tn >= 512 else 2048, LANES if ta else 16)
    sa, sb, so = a.dtype.itemsize, b.dtype.itemsize, jnp.dtype(out_dtype).itemsize
    k_mult = LANES if (not ta or tb) else 16
    tk = k_mult if k_unit % k_mult == 0 else k_unit
    for cand in range(k_mult, k_unit + 1, k_mult):
        if k_unit % cand == 0 and 2 * cand * (tm * sa + tn * sb) + tm * tn * (4 + 2 * so) <= MM_VMEM_BUDGET:
            tk = cand
    nk = K // tk
    dims = (((0 if ta else 1,), (1 if tb else 0,)), ((), ()))

    def body(a_ref, b_ref, o_ref, *scratch):
        part = lax.dot_general(a_ref[...].astype(BF16), b_ref[...].astype(BF16), dims, preferred_element_type=F32)
        if nk == 1:
            o_ref[...] = part.astype(o_ref.dtype)
            return
        acc_ref, = scratch
        k = pl.program_id(2)

        @pl.when(k == 0)
        def _():
            acc_ref[...] = part

        @pl.when(k > 0)
        def _():
            acc_ref[...] += part

        @pl.when(k == nk - 1)
        def _():
            o_ref[...] = acc_ref[...].astype(o_ref.dtype)

    if ta:
        a_spec = pl.BlockSpec((tk, tm), lambda i, j, k: (k, i))
    elif a_shards == 1:
        a_spec = pl.BlockSpec((tm, tk), lambda i, j, k: (i, k))
    else:
        akb = (K // a_shards) // tk
        a_spec = pl.BlockSpec((None, tm, tk), lambda i, j, k: (k // akb, i, k % akb))
    if b_shards == 1:
        b_spec = pl.BlockSpec((tn, tk), lambda i, j, k: (j, k)) if tb else pl.BlockSpec((tk, tn), lambda i, j, k: (k, j))
    elif tb:
        kpb = (K // b_shards) // tk
        b_spec = pl.BlockSpec((None, tn, tk), lambda i, j, k: (k // kpb, j, k % kpb))
    else:
        npb = (N // b_shards) // tn
        b_spec = pl.BlockSpec((None, tk, tn), lambda i, j, k: (j // npb, k, j % npb))
    if out_shards == 1:
        out_spec = pl.BlockSpec((tm, tn), lambda i, j, k: (i, j))
        out_shape = jax.ShapeDtypeStruct((M, N), out_dtype)
    else:
        opb = (N // out_shards) // tn
        out_spec = pl.BlockSpec((None, tm, tn), lambda i, j, k: (j // opb, i, j % opb))
        out_shape = jax.ShapeDtypeStruct((out_shards, M, N // out_shards), out_dtype)
    return pl.pallas_call(
        body, name=name, grid=(M // tm, N // tn, nk),
        in_specs=[a_spec, b_spec], out_specs=out_spec, out_shape=out_shape,
        scratch_shapes=[pltpu.VMEM((tm, tn), F32)] if nk > 1 else [],
        compiler_params=_cparams(("parallel", "parallel", "arbitrary")),
    )(a, b)


FFN_TILE_ROWS = 1024


def _ffn_in_swiglu(x, w4, *, name):
    M, K = x.shape
    S, _, ns = w4.shape
    half = S * ns // 2
    tn = _pick(ns, 512, LANES)
    tm = _pick(M, FFN_TILE_ROWS, 16)
    npb = ns // tn

    def body(x_ref, wa_ref, wb_ref, h_ref, ab_ref):
        xb = x_ref[...].astype(BF16)
        a = jnp.dot(xb, wa_ref[...].astype(BF16), preferred_element_type=F32)
        b = jnp.dot(xb, wb_ref[...].astype(BF16), preferred_element_type=F32)
        h_ref[...] = (jax.nn.silu(a) * b).astype(h_ref.dtype)
        ab_ref[0] = a.astype(ab_ref.dtype)
        ab_ref[1] = b.astype(ab_ref.dtype)

    return pl.pallas_call(
        body, name=name, grid=(M // tm, half // tn),
        in_specs=[pl.BlockSpec((tm, K), lambda i, j: (i, 0)),
                  pl.BlockSpec((None, K, tn), lambda i, j: (j // npb, 0, j % npb)),
                  pl.BlockSpec((None, K, tn), lambda i, j: (S // 2 + j // npb, 0, j % npb))],
        out_specs=[pl.BlockSpec((tm, tn), lambda i, j: (i, j)), pl.BlockSpec((2, tm, tn), lambda i, j: (0, i, j))],
        out_shape=[jax.ShapeDtypeStruct((M, half), BF16), jax.ShapeDtypeStruct((2, M, half), BF16)],
        compiler_params=_cparams(("parallel", "parallel")),
    )(x, w4, w4)


def _ffn_out_dx_swiglu(dy, w, ab, *, name):
    M, D = dy.shape
    n2 = w.shape[0]
    tn = _pick(n2, 512, LANES)
    tm = _pick(M, FFN_TILE_ROWS, 16)

    def body(dy_ref, w_ref, ab_ref, o_ref):
        dh = lax.dot_general(dy_ref[...].astype(BF16), w_ref[...].astype(BF16), NT_DIMS, preferred_element_type=F32)
        a, b = ab_ref[0].astype(F32), ab_ref[1].astype(F32)
        s = jax.nn.sigmoid(a)
        o_ref[0] = (dh * b * (s * (1.0 + a * (1.0 - s)))).astype(o_ref.dtype)
        o_ref[1] = (dh * (a * s)).astype(o_ref.dtype)

    return pl.pallas_call(
        body, name=name, grid=(M // tm, n2 // tn),
        in_specs=[pl.BlockSpec((tm, D), lambda i, j: (i, 0)), pl.BlockSpec((tn, D), lambda i, j: (j, 0)),
                  pl.BlockSpec((2, tm, tn), lambda i, j: (0, i, j))],
        out_specs=pl.BlockSpec((2, tm, tn), lambda i, j: (0, i, j)),
        out_shape=jax.ShapeDtypeStruct((2, M, n2), BF16),
        compiler_params=_cparams(("parallel", "parallel")),
    )(dy, w, ab)


def _row_tile(tiled, extra_bytes=0):
    rows = tiled[0].shape[0]
    per_row = sum(a.shape[1] * 4 for a in tiled) + extra_bytes
    target = max(SUBLANES, (6 * 1024 * 1024) // max(per_row, 1))
    return _pick(rows, min(target, 512), 16)


def _rw(f, tiled, bcast, out_dtypes, *, name, anchor=None):
    nt, nb = len(tiled), len(bcast)
    rows = tiled[0].shape[0]
    outs_aval = jax.eval_shape(f, *[jax.ShapeDtypeStruct((16, a.shape[1]), F32) for a in tiled],
                               *[jax.ShapeDtypeStruct(b.shape, F32) for b in bcast])
    widths = [o.shape[1] for o in outs_aval]
    tm = _row_tile(tiled, sum(w * 4 for w in widths))

    extra = [] if anchor is None else [anchor]
    n_in = nt + nb + len(extra)

    def body(*refs):
        tin = [r[...].astype(F32) for r in refs[:nt]]
        bin_ = [r[...].astype(F32) for r in refs[nt:nt + nb]]
        outs = f(*tin, *bin_)
        for o_ref, o in zip(refs[n_in:], outs):
            o_ref[...] = o.astype(o_ref.dtype)

    in_specs = [pl.BlockSpec((tm, a.shape[1]), lambda i: (i, 0)) for a in tiled]
    in_specs += [pl.BlockSpec(b.shape, lambda i: (0, 0)) for b in bcast + extra]
    res = pl.pallas_call(
        body, name=name, grid=(rows // tm,), in_specs=in_specs,
        out_specs=[pl.BlockSpec((tm, w), lambda i: (i, 0)) for w in widths],
        out_shape=[jax.ShapeDtypeStruct((rows, w), dt) for w, dt in zip(widths, out_dtypes)],
        compiler_params=_cparams(("parallel",)),
    )(*tiled, *bcast, *extra)
    return list(res)


def _rw_vjp(f, tiled, bcast, cts, need_t, need_b, t_dtypes, *, name):
    nt, nb = len(tiled), len(bcast)
    rows = tiled[0].shape[0]
    flat_cts = [c for group in cts for c in group]
    t_idx = [i for i in range(nt) if need_t[i]]
    b_idx = [i for i in range(nb) if need_b[i]]
    tm = _row_tile(list(tiled) + flat_cts, sum(tiled[i].shape[1] * 4 for i in t_idx))
    nc = len(flat_cts)

    def body(*refs):
        i = pl.program_id(0)
        tin = [r[...].astype(F32) for r in refs[:nt]]
        bin_ = [r[...].astype(F32) for r in refs[nt:nt + nb]]
        ct_refs = refs[nt + nb:nt + nb + nc]
        out_refs = refs[nt + nb + nc:]
        outs, vjp_fn = jax.vjp(f, *tin, *bin_)
        ct_vals, pos = [], 0
        for o, group in zip(outs, cts):
            acc = jnp.zeros_like(o)
            for _ in group:
                acc = acc + ct_refs[pos][...].astype(F32)
                pos += 1
            ct_vals.append(acc)
        grads = vjp_fn(tuple(ct_vals))
        for o_ref, k in zip(out_refs[:len(t_idx)], t_idx):
            o_ref[...] = grads[k].astype(o_ref.dtype)
        for o_ref, k in zip(out_refs[len(t_idx):], b_idx):
            @pl.when(i == 0)
            def _(o_ref=o_ref):
                o_ref[...] = jnp.zeros_like(o_ref)

            o_ref[...] += grads[nt + k]

    in_specs = [pl.BlockSpec((tm, a.shape[1]), lambda i: (i, 0)) for a in tiled]
    in_specs += [pl.BlockSpec(b.shape, lambda i: (0, 0)) for b in bcast]
    in_specs += [pl.BlockSpec((tm, c.shape[1]), lambda i: (i, 0)) for c in flat_cts]
    out_specs = [pl.BlockSpec((tm, tiled[k].shape[1]), lambda i: (i, 0)) for k in t_idx]
    out_specs += [pl.BlockSpec(bcast[k].shape, lambda i: (0, 0)) for k in b_idx]
    out_shape = [jax.ShapeDtypeStruct(tiled[k].shape, dt) for k, dt in zip(t_idx, t_dtypes)]
    out_shape += [jax.ShapeDtypeStruct(bcast[k].shape, F32) for k in b_idx]
    res = pl.pallas_call(
        body, name=name, grid=(rows // tm,), in_specs=in_specs, out_specs=out_specs, out_shape=out_shape,
        compiler_params=_cparams(("arbitrary",)),
    )(*tiled, *bcast, *flat_cts)
    res = list(res)
    return res[:len(t_idx)], res[len(t_idx):]


def _rms(x, g):
    return x * lax.rsqrt(jnp.mean(x * x, axis=-1, keepdims=True) + EPS) * g


def _f_norm_mod(x, g, sc, sh):
    return (_rms(x, g) * (1.0 + sc) + sh,)


def _f_norm_mod_keep(x, g, sc, sh):
    return (_rms(x, g) * (1.0 + sc) + sh, x)


@jax.custom_vjp
def _swap16(x):
    w = x.shape[-1]
    lane = lax.broadcasted_iota(jnp.int32, x.shape, x.ndim - 1)
    return jnp.where((lane & 16) == 0, pltpu.roll(x, w - 16, x.ndim - 1), pltpu.roll(x, 16, x.ndim - 1))


_swap16.defvjp(lambda x: (_swap16(x), None), lambda _, g: (_swap16(g),))


def _rope(x, cos, sin):
    return x * cos + _swap16(x) * sin


def _make_f_post_in(sw, q_rank, kv_rank, with_q):
    o1, o2, o3 = sw, sw + q_rank, sw + q_rank + kv_rank

    if with_q:
        def f(ha, cos, sin, qg, kvg):
            u = ha[:, :o1]
            cqn = _rms(ha[:, o1:o2], qg)
            ckvn = _rms(ha[:, o2:o3], kvg)
            kr = _rope(ha[:, o3:o3 + LANES], cos, sin)
            return u, cqn, ckvn, kr
    else:
        def f(ha, kvg):
            return ha[:, :o1], _rms(ha[:, o2:o3], kvg), ha[:, o3:o3 + LANES]
    return f


def _f_qpost(q2, cos, sin):
    parts = []
    for h in range(q2.shape[1] // (2 * LANES)):
        o = 2 * LANES * h
        parts += [q2[:, o:o + LANES], _rope(q2[:, o + LANES:o + 2 * LANES], cos, sin)]
    return (jnp.concatenate(parts, axis=1),)


def _f_s5post(u, r, d):
    return (jax.nn.gelu(d * u + r, approximate=True),)


def _f_merge(ab, bm, gt):
    d = bm.shape[1]
    br_s5 = ab[:, :d] * jax.nn.sigmoid(ab[:, d:])
    g = jax.nn.sigmoid(gt)
    return (g[:, :d] * br_s5 + g[:, d:] * bm,)


def _f_resid_norm(x, out, g1, n2, sc2, sh2):
    x1 = x + g1 * out
    return x1, _rms(x1, n2) * (1.0 + sc2) + sh2


def _f_final(x1, f, tgt, g2, nf):
    y = _rms(x1 + g2 * f, nf)
    return (0.5 * jnp.mean(jnp.square(y - tgt), axis=-1, keepdims=True),)


def _bd_fanin(xs, ws, *, name):
    nw = len(ws)
    nb, kb, nn = ws[0].shape
    T = xs[0].shape[0]
    tm = _pick(T, 512, 16)

    def body(*refs):
        acc = None
        for x_ref, w_ref in zip(refs[:nw], refs[nw:2 * nw]):
            t = jnp.dot(x_ref[...].astype(BF16), w_ref[0].astype(BF16), preferred_element_type=F32)
            acc = t if acc is None else acc + t
        refs[2 * nw][...] = acc

    return pl.pallas_call(
        body, name=name, grid=(nb, T // tm),
        in_specs=[pl.BlockSpec((tm, kb), lambda j, i: (i, j))] * nw + [pl.BlockSpec((1, kb, nn), lambda j, i: (j, 0, 0))] * nw,
        out_specs=pl.BlockSpec((tm, nn), lambda j, i: (i, j)),
        out_shape=jax.ShapeDtypeStruct((T, nb * nn), F32),
        compiler_params=_cparams(("parallel", "parallel")),
    )(*xs, *ws)


def _bd_dw(xs, dys, nb, *, name):
    npair = len(xs)
    T = xs[0].shape[0]
    kb = xs[0].shape[1] // nb
    nn = dys[0].shape[1] // nb
    tm = _pick(T, 512, 16)
    dims = (((0,), (0,)), ((), ()))

    def body(*refs):
        i = pl.program_id(1)
        for x_ref, d_ref, o_ref in zip(refs[:npair], refs[npair:2 * npair], refs[2 * npair:]):
            @pl.when(i == 0)
            def _(o_ref=o_ref):
                o_ref[...] = jnp.zeros_like(o_ref)

            o_ref[0] += lax.dot_general(x_ref[...].astype(BF16), d_ref[...].astype(BF16), dims,
                                        preferred_element_type=F32)

    return list(pl.pallas_call(
        body, name=name, grid=(nb, T // tm),
        in_specs=[pl.BlockSpec((tm, kb), lambda j, i: (i, j))] * npair + [pl.BlockSpec((tm, nn), lambda j, i: (i, j))] * npair,
        out_specs=[pl.BlockSpec((1, kb, nn), lambda j, i: (j, 0, 0))] * npair,
        out_shape=[jax.ShapeDtypeStruct((nb, kb, nn), F32)] * npair,
        compiler_params=_cparams(("parallel", "arbitrary")),
    )(*xs, *dys))


def _cmul(ar, ai, br, bi):
    return ar * br - ai * bi, ar * bi + ai * br


def _cpow(lr, li, n):
    rr, ri = None, None
    br, bi = lr, li
    while n:
        if n & 1:
            rr, ri = (br, bi) if rr is None else _cmul(rr, ri, br, bi)
        n >>= 1
        if n:
            br, bi = _cmul(br, bi, br, bi)
    return rr, ri


SCAN_MM_ROWS = 512


def _s5_scan(x, w_re, w_im, lam_re, lam_im, h0_re, h0_im, e0_re, e0_im, *, reverse, name):
    rows = x.shape[0]
    nb, kb, cb = w_re.shape
    C = nb * cb
    n = rows // N_SEG
    mm_rows = _pick(rows, SCAN_MM_ROWS, 16)
    seg_order = list(range(N_SEG))[::-1] if reverse else list(range(N_SEG))
    s_first, s_last = seg_order[0], seg_order[-1]

    def body(x_ref, wr_ref, wi_ref, lr_ref, li_ref, h0r_ref, h0i_ref, e0r_ref, e0i_ref, hr_ref, hi_ref, htr_ref, hti_ref,
             locr_ref, loci_ref):
        shape = (N_SEG, cb)
        lr = jnp.broadcast_to(lr_ref[...], shape)
        li = jnp.broadcast_to(li_ref[...], shape)
        row = lax.broadcasted_iota(jnp.int32, shape, 0)

        def step_of(k):
            return (n - 1 - k) if reverse else k

        def rows_of(k):
            return pl.ds(pl.multiple_of(step_of(k) * N_SEG, N_SEG), N_SEG)

        wr, wi = wr_ref[...].astype(BF16), wi_ref[...].astype(BF16)
        for r0 in range(0, rows, mm_rows):
            xb = x_ref[r0:r0 + mm_rows, :].astype(BF16)
            locr_ref[r0:r0 + mm_rows, :] = jnp.dot(xb, wr, preferred_element_type=F32)
            loci_ref[r0:r0 + mm_rows, :] = jnp.dot(xb, wi, preferred_element_type=F32)

        first = row == s_first
        hr = locr_ref[rows_of(0), :] + jnp.where(first, e0r_ref[...], 0.0)
        hi = loci_ref[rows_of(0), :] + jnp.where(first, e0i_ref[...], 0.0)
        locr_ref[rows_of(0), :] = hr
        loci_ref[rows_of(0), :] = hi

        def pass1(k, carry):
            hr, hi = carry
            pr, pi = _cmul(lr, li, hr, hi)
            hr = pr + locr_ref[rows_of(k), :]
            hi = pi + loci_ref[rows_of(k), :]
            locr_ref[rows_of(k), :] = hr
            loci_ref[rows_of(k), :] = hi
            return hr, hi

        er, ei = lax.fori_loop(1, n, pass1, (hr, hi))

        lnr, lni = _cpow(lr_ref[...], li_ref[...], n)
        cr, ci = h0r_ref[...], h0i_ref[...]
        cin_r = jnp.zeros(shape, F32)
        cin_i = jnp.zeros(shape, F32)
        for s in seg_order:
            cin_r = jnp.where(row == s, cr, cin_r)
            cin_i = jnp.where(row == s, ci, cin_i)
            if s != s_last:
                pr, pi = _cmul(lnr, lni, cr, ci)
                cr = pr + jnp.sum(jnp.where(row == s, er, 0.0), axis=0, keepdims=True)
                ci = pi + jnp.sum(jnp.where(row == s, ei, 0.0), axis=0, keepdims=True)

        def pass2(k, carry):
            pr, pi, _, _ = carry
            ar, ai = _cmul(pr, pi, cin_r, cin_i)
            hr = locr_ref[rows_of(k), :] + ar
            hi = loci_ref[rows_of(k), :] + ai
            hr_ref[rows_of(k), :] = hr.astype(hr_ref.dtype)
            hi_ref[rows_of(k), :] = hi.astype(hi_ref.dtype)
            npr, npi = _cmul(pr, pi, lr, li)
            return npr, npi, hr, hi

        _, _, last_r, last_i = lax.fori_loop(0, n, pass2, (lr, li, er, ei))
        htr_ref[...] = jnp.sum(jnp.where(row == s_last, last_r, 0.0), axis=0, keepdims=True)
        hti_ref[...] = jnp.sum(jnp.where(row == s_last, last_i, 0.0), axis=0, keepdims=True)

    big = pl.BlockSpec((rows, cb), lambda j: (0, j))
    vec = pl.BlockSpec((1, cb), lambda j: (0, j))
    wspec = pl.BlockSpec((None, kb, cb), lambda j: (j, 0, 0))
    return pl.pallas_call(
        body, name=name, grid=(nb,),
        in_specs=[pl.BlockSpec((rows, kb), lambda j: (0, j)), wspec, wspec] + [vec] * 6,
        out_specs=[big, big, vec, vec],
        out_shape=[jax.ShapeDtypeStruct((rows, C), BF16)] * 2 + [jax.ShapeDtypeStruct((1, C), F32)] * 2,
        scratch_shapes=[pltpu.VMEM((rows, cb), F32)] * 2,
        compiler_params=_cparams(("parallel",)),
    )(x, w_re, w_im, lam_re, lam_im, h0_re, h0_im, e0_re, e0_im)


def _s5_dlam(mu_re, mu_im, h_re, h_im, h0_re, h0_im, *, reverse, name):
    rows, C = h_re.shape
    n = rows // N_SEG
    cb = _pick(C, 256, LANES)
    s_first = N_SEG - 1 if reverse else 0

    def body(mr_ref, mi_ref, hr_ref, hi_ref, h0r_ref, h0i_ref, dr_ref, di_ref):
        shape = (N_SEG, cb)
        row = lax.broadcasted_iota(jnp.int32, shape, 0)

        def rows_of(k):
            step = (n - 1 - k) if reverse else k
            return pl.ds(pl.multiple_of(step * N_SEG, N_SEG), N_SEG)

        def term(k, pr, pi):
            mr, mi = mr_ref[rows_of(k), :].astype(F32), mi_ref[rows_of(k), :].astype(F32)
            return mr * pr + mi * pi, mi * pr - mr * pi

        shift = N_SEG - 1 if reverse else 1
        pr = jnp.where(row == s_first, h0r_ref[...], pltpu.roll(hr_ref[rows_of(n - 1), :].astype(F32), shift, 0))
        pi = jnp.where(row == s_first, h0i_ref[...], pltpu.roll(hi_ref[rows_of(n - 1), :].astype(F32), shift, 0))
        acc = term(0, pr, pi)

        def loop(k, acc):
            tr, ti = term(k, hr_ref[rows_of(k - 1), :].astype(F32), hi_ref[rows_of(k - 1), :].astype(F32))
            return acc[0] + tr, acc[1] + ti

        ar, ai = lax.fori_loop(1, n, loop, acc)
        dr_ref[...] = jnp.sum(ar, axis=0, keepdims=True)
        di_ref[...] = jnp.sum(ai, axis=0, keepdims=True)

    big = pl.BlockSpec((rows, cb), lambda j: (0, j))
    vec = pl.BlockSpec((1, cb), lambda j: (0, j))
    return pl.pallas_call(
        body, name=name, grid=(C // cb,),
        in_specs=[big] * 4 + [vec] * 2, out_specs=[vec, vec],
        out_shape=[jax.ShapeDtypeStruct((1, C), F32)] * 2,
        compiler_params=_cparams(("parallel",)),
    )(mu_re, mu_im, h_re, h_im, h0_re, h0_im)


NT_DIMS = (((1,), (1,)), ((), ()))
TN_DIMS = (((0,), (0,)), ((), ()))


ATTN_Q_ROWS = 512


def _attn_exp(q, kvh, kr):
    s = (lax.dot_general(q[:, :LANES], kvh[:, :LANES], NT_DIMS, preferred_element_type=F32)
         + lax.dot_general(q[:, LANES:], kr, NT_DIMS, preferred_element_type=F32))
    e = jnp.exp2((s - jnp.max(s, axis=-1, keepdims=True)) * (ATTN_SCALE * math.log2(math.e)))
    return e, jnp.sum(e, axis=-1, keepdims=True)


def _attn_specs(L, T, tq):
    return [
        pl.BlockSpec((tq, 2 * LANES), lambda h, i: (i, h)),
        pl.BlockSpec((T, 2 * LANES), lambda h, i: (0, h)),
        pl.BlockSpec((T, LANES), lambda h, i: (0, 0)),
    ]


def _attn_fwd(qq, kv, kr, *, name):
    L, T = qq.shape[0], kv.shape[0]
    tq = _pick(L, ATTN_Q_ROWS // 2, 16)

    def body(q_ref, kv_ref, kr_ref, o_ref):
        kvh = kv_ref[...]
        e, l = _attn_exp(q_ref[...], kvh, kr_ref[...])
        o_ref[...] = (jnp.dot(e.astype(BF16), kvh[:, LANES:], preferred_element_type=F32) * (1.0 / l)).astype(o_ref.dtype)

    return pl.pallas_call(
        body, name=name, grid=(MLA_HEADS, L // tq), in_specs=_attn_specs(L, T, tq),
        out_specs=pl.BlockSpec((tq, LANES), lambda h, i: (i, h)),
        out_shape=jax.ShapeDtypeStruct((L, MLA_HEADS * V_DIM), BF16),
        compiler_params=_cparams(("parallel", "parallel")),
    )(qq, kv, kr)


def _attn_bwd(qq, kv, kr, do, *, name):
    L, T = qq.shape[0], kv.shape[0]
    H = MLA_HEADS
    tq = _pick(L, ATTN_Q_ROWS, 16)
    nq = L // tq

    def body(q_ref, kv_ref, kr_ref, do_ref, dq_ref, dkv_ref, dkr_ref, dkn_acc, dv_acc):
        h, i = pl.program_id(0), pl.program_id(1)
        q, kvh, krv, dov = q_ref[...], kv_ref[...], kr_ref[...], do_ref[...]
        e, l = _attn_exp(q, kvh, krv)
        inv = 1.0 / l
        ps = e * (inv * ATTN_SCALE)
        t = lax.dot_general(dov, kvh[:, LANES:], NT_DIMS, preferred_element_type=F32) * ps
        ds = (t - ps * (jnp.sum(t, axis=-1, keepdims=True) * (1.0 / ATTN_SCALE))).astype(BF16)
        dq_ref[:, :LANES] = jnp.dot(ds, kvh[:, :LANES], preferred_element_type=F32)
        dq_ref[:, LANES:] = jnp.dot(ds, krv, preferred_element_type=F32)

        @pl.when(i == 0)
        def _():
            dkn_acc[...] = jnp.zeros_like(dkn_acc)
            dv_acc[...] = jnp.zeros_like(dv_acc)

        @pl.when((i == 0) & (h == 0))
        def _():
            dkr_ref[...] = jnp.zeros_like(dkr_ref)

        dv_acc[...] += lax.dot_general(e.astype(BF16), (dov.astype(F32) * inv).astype(BF16), TN_DIMS,
                                       preferred_element_type=F32)
        dkn_acc[...] += lax.dot_general(ds, q[:, :LANES], TN_DIMS, preferred_element_type=F32)
        dkr_ref[...] += lax.dot_general(ds, q[:, LANES:], TN_DIMS, preferred_element_type=F32)

        @pl.when(i == nq - 1)
        def _():
            dkv_ref[:, :LANES] = dkn_acc[...].astype(dkv_ref.dtype)
            dkv_ref[:, LANES:] = dv_acc[...].astype(dkv_ref.dtype)

    in_specs = _attn_specs(L, T, tq) + [pl.BlockSpec((tq, LANES), lambda h, i: (i, h))]
    return pl.pallas_call(
        body, name=name, grid=(H, L // tq), in_specs=in_specs,
        out_specs=[pl.BlockSpec((tq, 2 * LANES), lambda h, i: (i, h)), pl.BlockSpec((T, 2 * LANES), lambda h, i: (0, h)),
                   pl.BlockSpec((T, LANES), lambda h, i: (0, 0))],
        out_shape=[jax.ShapeDtypeStruct((L, H * 2 * LANES), F32), jax.ShapeDtypeStruct((T, H * 2 * LANES), BF16),
                   jax.ShapeDtypeStruct((T, LANES), F32)],
        scratch_shapes=[pltpu.VMEM((T, LANES), F32), pltpu.VMEM((T, LANES), F32)],
        compiler_params=_cparams(("arbitrary", "arbitrary")),
    )(qq, kv, kr, do)


def _adamw(w, g, m, v, *, name, anchor=None):
    c1 = 1.0 - ADAM_B1 ** ADAM_STEP
    c2 = 1.0 - ADAM_B2 ** ADAM_STEP

    def f(w, g, m, v):
        m = ADAM_B1 * m + (1.0 - ADAM_B1) * g
        v = ADAM_B2 * v + (1.0 - ADAM_B2) * jnp.square(g)
        delta = -ADAM_LR * ((m / c1) / (jnp.sqrt(v / c2) + ADAM_EPS) + ADAM_WD * w)
        return g, delta, m, v

    return _rw(f, [w, g, m, v], [], [F32] * 4, name=name, anchor=anchor)


def _slab_rows(rows, cols, n_arrays):
    return _pick(rows, max(16, (8 * 1024 * 1024) // (cols * 4 * n_arrays)), 16)


def _scalars(*vals):
    return jnp.stack([jnp.asarray(v, jnp.int32) for v in vals])


def _into_slot(src, slot, nslots, dtype, *, name):
    R, C = src.shape
    tr = _slab_rows(R, C, 2)

    def body(s_ref, x_ref, o_ref):
        o_ref[...] = x_ref[...].astype(o_ref.dtype)

    return pl.pallas_call(
        body, name=name,
        grid_spec=pltpu.PrefetchScalarGridSpec(
            num_scalar_prefetch=1, grid=(R // tr,),
            in_specs=[pl.BlockSpec((tr, C), lambda i, s: (i, 0))],
            out_specs=pl.BlockSpec((None, tr, C), lambda i, s: (s[0], i, 0))),
        out_shape=jax.ShapeDtypeStruct((nslots, R, C), dtype),
        compiler_params=_cparams(("arbitrary",)),
    )(_scalars(slot), src)


def _pair_sum(g, got, c, *, name):
    _, R, C = g.shape
    hr = R // 2
    tr = _slab_rows(hr, C, 3)
    nblk = hr // tr

    def body(s_ref, g_ref, r_ref, o_ref):
        o_ref[...] = (g_ref[...].astype(F32) + r_ref[...].astype(F32)).astype(o_ref.dtype)

    return pl.pallas_call(
        body, name=name,
        grid_spec=pltpu.PrefetchScalarGridSpec(
            num_scalar_prefetch=1, grid=(4, nblk),
            in_specs=[pl.BlockSpec((None, tr, C), lambda j, i, s: (j, s[0] * nblk + i, 0)),
                      pl.BlockSpec((None, tr, C), lambda j, i, s: (j, i, 0))],
            out_specs=pl.BlockSpec((None, tr, C), lambda j, i, s: (j, i, 0))),
        out_shape=jax.ShapeDtypeStruct((4, hr, C), g.dtype),
        compiler_params=_cparams(("arbitrary", "arbitrary")),
    )(_scalars(c), g, got)


def _chip_sum(p, landed, me_chip, c, *, name):
    _, hr, C = p.shape
    tr = _slab_rows(hr, C, 5)

    def body(s_ref, p_ref, l0_ref, l1_ref, l2_ref, o_ref):
        o_ref[...] = ((p_ref[...].astype(F32) + l0_ref[...].astype(F32)) + l1_ref[...].astype(F32)) + l2_ref[...].astype(F32)

    return pl.pallas_call(
        body, name=name,
        grid_spec=pltpu.PrefetchScalarGridSpec(
            num_scalar_prefetch=1, grid=(hr // tr,),
            in_specs=[pl.BlockSpec((None, tr, C), lambda i, s: (s[0], i, 0))]
            + [pl.BlockSpec((None, tr, C), functools.partial(lambda i, s, k: (k, i, 0), k=k)) for k in range(3)],
            out_specs=pl.BlockSpec((None, tr, C), lambda i, s: (s[1], i, 0))),
        out_shape=jax.ShapeDtypeStruct((2, hr, C), F32),
        compiler_params=_cparams(("arbitrary",)),
    )(_scalars(me_chip, c), p, landed, landed, landed)


def _place():
    return lax.axis_index("x"), lax.axis_index("y"), lax.axis_index("c")


def _other_chips(x, y):
    chips = [(1 - x, y), (x, 1 - y), (1 - x, 1 - y)]
    return chips, [2 * cx + cy for cx, cy in chips]


HBM = pl.BlockSpec(memory_space=pl.ANY)


def _allgather8(v, *, name):
    rows, cols = v.shape

    def body(v_ref, out_ref, send_sems, recv_sems):
        x, y, c = _place()
        me = 4 * x + 2 * y + c
        out_ref[me] = v_ref[...]
        copies = []
        for k in range(1, 8):
            bx, by, bc = (k >> 2) & 1, (k >> 1) & 1, k & 1
            px, py, pc = x ^ bx, y ^ by, c ^ bc
            cp = pltpu.make_async_remote_copy(
                src_ref=v_ref, dst_ref=out_ref.at[me], send_sem=send_sems.at[k - 1], recv_sem=recv_sems.at[k - 1],
                device_id=(px, py, pc), device_id_type=MESH)
            cp.start()
            copies.append((cp, 4 * px + 2 * py + pc))
        for k, (cp, peer) in enumerate(copies):
            pltpu.make_async_remote_copy(
                src_ref=v_ref, dst_ref=out_ref.at[peer], send_sem=send_sems.at[k], recv_sem=recv_sems.at[k],
                device_id=(x, y, c), device_id_type=MESH).wait_recv()
        for cp, _ in copies:
            cp.wait_send()

    return pl.pallas_call(
        body, name=name, out_shape=jax.ShapeDtypeStruct((8, rows, cols), v.dtype),
        in_specs=[pl.BlockSpec(memory_space=pltpu.VMEM)], out_specs=pl.BlockSpec(memory_space=pltpu.VMEM),
        scratch_shapes=[pltpu.SemaphoreType.DMA((7,)), pltpu.SemaphoreType.DMA((7,))],
        compiler_params=pltpu.CompilerParams(vmem_limit_bytes=VMEM_LIMIT),
    )(v)


def _allgather_shards(bufs, *, name):
    n = len(bufs)

    def body(*refs):
        outs = refs[n:2 * n]
        send_sems, recv_sems = refs[2 * n:]
        x, y, c = _place()
        me_chip = 2 * x + y
        sibling = (x, y, 1 - c)
        chips, chip_ids = _other_chips(x, y)

        def remote(k, j, blk, hf, to):
            hr = bufs[k].shape[1] // 2
            piece = outs[k].at[blk, pl.ds(pl.multiple_of(hf * hr, 16), hr), :]
            return pltpu.make_async_remote_copy(
                src_ref=piece, dst_ref=piece, send_sem=send_sems.at[6 * k + j], recv_sem=recv_sems.at[6 * k + j],
                device_id=to, device_id_type=MESH)

        sends = []
        for k in range(n):
            for j, chip in enumerate(chips):
                cp = remote(k, j, me_chip, c, (*chip, c))
                cp.start()
                sends.append(cp)
        for k in range(n):
            for j, chip in enumerate(chips):
                remote(k, j, chip_ids[j], c, (x, y, c)).wait_recv()
                cp = remote(k, 3 + j, chip_ids[j], c, sibling)
                cp.start()
                sends.append(cp)
        for k in range(n):
            for j in range(3):
                remote(k, 3 + j, chip_ids[j], 1 - c, (x, y, c)).wait_recv()
        for cp in sends:
            cp.wait_send()

    return list(pl.pallas_call(
        body, name=name, out_shape=[jax.ShapeDtypeStruct(b.shape, b.dtype) for b in bufs],
        in_specs=[HBM] * n, out_specs=[HBM] * n, input_output_aliases={k: k for k in range(n)},
        scratch_shapes=[pltpu.SemaphoreType.DMA((6 * n,)), pltpu.SemaphoreType.DMA((6 * n,))],
    )(*bufs))


HBM_SPEC = pl.BlockSpec(memory_space=pltpu.HBM)
SEM_SPEC = pl.BlockSpec(memory_space=pltpu.SEMAPHORE)
EFFECT = pltpu.SideEffectType.DATAFLOW_SIDE_EFFECTING
TOKEN = jax.ShapeDtypeStruct((SUBLANES, LANES), F32)


def _in_hbm(a):
    return pltpu.with_memory_space_constraint(a, pltpu.HBM)


def _half_rows(buf, hf):
    hr = buf.shape[1] // 2
    return pl.ds(pl.multiple_of(hf * hr, 16), hr)


def _plan_ag_ici(refs):
    x, y, c = _place()
    chips, ids = _other_chips(x, y)
    out = []
    for r in refs:
        mine = r.at[2 * x + y, _half_rows(r, c), :]
        out += [(mine, mine, r.at[ids[j], _half_rows(r, c), :], (*chip, c)) for j, chip in enumerate(chips)]
    return out


def _plan_ag_pair(refs):
    x, y, c = _place()
    _, ids = _other_chips(x, y)
    out = []
    for r in refs:
        for j in range(3):
            piece = r.at[ids[j], _half_rows(r, c), :]
            out.append((piece, piece, r.at[ids[j], _half_rows(r, 1 - c), :], (x, y, 1 - c)))
    return out


def _plan_rs_ici(refs):
    x, y, c = _place()
    chips, ids = _other_chips(x, y)
    n = len(refs) // 2
    return [(refs[k].at[ids[j]], refs[n + k].at[j], refs[n + k].at[j], (*chip, c))
            for k in range(n) for j, chip in enumerate(chips)]


def _plan_pair_exchange(refs):
    x, y, c = _place()
    n = len(refs) // 2
    return [(refs[k].at[:, _half_rows(refs[k], 1 - c), :], refs[n + k], refs[n + k], (x, y, 1 - c)) for k in range(n)]


def _plan_pair_gather(refs):
    x, y, c = _place()
    return [(r.at[c], r.at[c], r.at[1 - c], (x, y, 1 - c)) for r in refs]


def _remote(src, dst, send_sem, recv_sem, target):
    return pltpu.make_async_remote_copy(src_ref=src, dst_ref=dst, send_sem=send_sem, recv_sem=recv_sem,
                                        device_id=target, device_id_type=MESH)


def _copy_start(groups, *, name, after=()):
    flat = [a for arrays, _, _ in groups for a in arrays]
    n, ng = len(flat), len(groups)
    after = list(after)
    n_in = n + len(after)

    def body(*refs):
        sems = refs[n_in:n_in + 2 * ng]
        thru = refs[n_in + 2 * ng:n_in + 2 * ng + n]
        token = refs[-1]
        pos = 0
        for g, (arrays, plan, n_copies) in enumerate(groups):
            copies = plan(thru[pos:pos + len(arrays)])
            pos += len(arrays)
            assert len(copies) == n_copies
            for i, (src, dst, _, target) in enumerate(copies):
                _remote(src, dst, sems[2 * g].at[i], sems[2 * g + 1].at[i], target).start()
        token[...] = jnp.zeros_like(token)

    out_shape = tuple(pltpu.SemaphoreType.DMA((n_copies,)) for _, _, n_copies in groups for _ in range(2))
    out_shape += tuple(pltpu.HBM(a.shape, a.dtype) for a in flat) + (TOKEN,)
    res = pl.pallas_call(
        body, name=name, out_shape=out_shape,
        in_specs=(HBM_SPEC,) * n + (pl.BlockSpec(memory_space=pl.ANY),) * len(after),
        out_specs=(SEM_SPEC,) * (2 * ng) + (HBM_SPEC,) * n + (pl.BlockSpec(memory_space=pltpu.VMEM),),
        input_output_aliases={k: 2 * ng + k for k in range(n)},
        compiler_params=pltpu.CompilerParams(has_side_effects=EFFECT),
    )(*[_in_hbm(a) for a in flat], *after)
    sems = [(res[2 * g], res[2 * g + 1]) for g in range(ng)]
    thru, pos = [], 2 * ng
    for arrays, _, _ in groups:
        thru.append(list(res[pos:pos + len(arrays)]))
        pos += len(arrays)
    return sems, thru, res[-1]


def _copy_wait(arrays, sems, plan, n_copies, after, *, name):
    n = len(arrays)
    after = list(after)

    def body(*refs):
        send, recv = refs[n], refs[n + 1]
        x, y, c = _place()
        copies = plan(refs[:n])
        assert len(copies) == n_copies
        for i, (src, dst, landing, target) in enumerate(copies):
            _remote(src, dst, send.at[i], recv.at[i], target).wait_send()
            _remote(landing, landing, send.at[i], recv.at[i], (x, y, c)).wait_recv()

    return list(pl.pallas_call(
        body, name=name, out_shape=tuple(pltpu.HBM(a.shape, a.dtype) for a in arrays),
        in_specs=(HBM_SPEC,) * n + (SEM_SPEC, SEM_SPEC) + (pl.BlockSpec(memory_space=pl.ANY),) * len(after),
        out_specs=(HBM_SPEC,) * n, input_output_aliases={k: k for k in range(n)},
        compiler_params=pltpu.CompilerParams(has_side_effects=EFFECT),
    )(*arrays, *sems, *after))


def _rs_stage1(gs, tag, after=()):
    n = len(gs)
    lands = [lax.empty((4, g.shape[1] // 2, g.shape[2]), g.dtype) for g in gs]
    sems, (arrays,), token = _copy_start([(list(gs) + lands, _plan_pair_exchange, n)], name=f"rs_pair_start_{tag}",
                                         after=after)
    return (sems[0], arrays), token


def _rs_stage2(handle, after, tag):
    sems, arrays = handle
    n = len(arrays) // 2
    arrays = _copy_wait(arrays, sems, _plan_pair_exchange, n, after, name=f"rs_pair_wait_{tag}")
    c = lax.axis_index("c")
    pair = [_pair_sum(g, r, c, name=f"rs_pair_sum_{tag}{k}") for k, (g, r) in enumerate(zip(arrays[:n], arrays[n:]))]
    lands = [lax.empty((3,) + p.shape[1:], p.dtype) for p in pair]
    sems, (arrays,), token = _copy_start([(pair + lands, _plan_rs_ici, 3 * n)], name=f"rs_start_{tag}")
    return (sems[0], arrays), token


def _rs_stage3(handle, after, tag):
    sems, arrays = handle
    n = len(arrays) // 2
    arrays = _copy_wait(arrays, sems, _plan_rs_ici, 3 * n, after, name=f"rs_wait_{tag}")
    x, y, c = _place()
    halves = [_chip_sum(p, l, 2 * x + y, c, name=f"rs_chip_sum_{tag}{k}") for k, (p, l) in enumerate(zip(arrays[:n], arrays[n:]))]
    sems, (halves,), token = _copy_start([(halves, _plan_pair_gather, n)], name=f"rs_gather_start_{tag}")
    return (sems[0], halves), token


def _rs_stage4(handle, after, tag):
    sems, halves = handle
    full = _copy_wait(halves, sems, _plan_pair_gather, len(halves), after, name=f"rs_gather_wait_{tag}")
    return [f.reshape(2 * f.shape[1], f.shape[2]) for f in full]


def _to_segments(a):
    rows = a.shape[0]
    return a.reshape(N_SEG, rows // N_SEG, -1).transpose(1, 0, 2).reshape(rows, -1)


def _from_segments(a):
    rows = a.shape[0]
    return a.reshape(rows // N_SEG, N_SEG, -1).transpose(1, 0, 2).reshape(rows, -1)


def _rope_tables(L):
    t = jnp.arange(L, dtype=jnp.int32)
    row = (t // GRID_W).astype(F32)
    col = (t % GRID_W).astype(F32)
    n_freq = QK_ROPE // 4
    inv = ROPE_BASE ** (-jnp.arange(n_freq, dtype=F32) / n_freq)
    a0, a1 = row[:, None] * inv, col[:, None] * inv
    z = jnp.zeros((L, LANES - QK_ROPE), F32)
    cos = jnp.concatenate([jnp.cos(a0), jnp.cos(a0), jnp.cos(a1), jnp.cos(a1), z], axis=1)
    sin = jnp.concatenate([-jnp.sin(a0), jnp.sin(a0), -jnp.sin(a1), jnp.sin(a1), z], axis=1)
    return _to_segments(cos), _to_segments(sin)


def _col_blocks(w, nblk):
    r, c = w.shape
    return w.reshape(r, nblk, c // nblk).transpose(1, 0, 2)


def _from_col_blocks(w4):
    nblk, r, c = w4.shape
    return w4.transpose(1, 0, 2).reshape(r, nblk * c)


def _s5_discretize(a_re, a_im, log_dt, b_re, b_im):
    dt = jnp.exp(log_dt)[:, None]
    mag = jnp.exp(a_re * dt)
    ab_re, ab_im = mag * jnp.cos(a_im * dt), mag * jnp.sin(a_im * dt)
    den = a_re * a_re + a_im * a_im
    nr, ni = ab_re - 1.0, ab_im
    co_re = (nr * a_re + ni * a_im) / den
    co_im = (ni * a_re - nr * a_im) / den
    bb_re = co_re[..., None] * b_re - co_im[..., None] * b_im
    bb_im = co_re[..., None] * b_im + co_im[..., None] * b_re
    return ab_re, ab_im, bb_re, bb_im


def _diag_blocks_in(bb, gpb):
    G, N, P = bb.shape
    t = jnp.tile(jnp.swapaxes(bb, 1, 2).reshape(G // gpb, gpb * P, N), (1, 1, gpb))
    row = lax.broadcasted_iota(jnp.int32, t.shape, 1) // P
    col = lax.broadcasted_iota(jnp.int32, t.shape, 2) // N
    return jnp.where(row == col, t, 0.0)


def _diag_blocks_out(cc, gpb):
    G, P, N = cc.shape
    t = jnp.tile(jnp.swapaxes(cc, 1, 2).reshape(G // gpb, gpb * N, P), (1, 1, gpb))
    row = lax.broadcasted_iota(jnp.int32, t.shape, 1) // N
    col = lax.broadcasted_iota(jnp.int32, t.shape, 2) // P
    return jnp.where(row == col, t, 0.0)


def _tr(ws):
    return [jnp.swapaxes(w, 1, 2) for w in ws]


WEIGHTS = ['c_ctx', 'w_mod', 'b_mod', 'norm1', 'norm2', 'w_in', 's5_a_re', 's5_a_im', 's5_log_dt', 's5_b_re', 's5_b_im',
           's5_c_re', 's5_c_im', 's5_d', 'w_glu', 'q_norm', 'kv_norm', 'w_uq', 'w_ukv', 'w_mla_o', 'w_out', 'w_ffn_in',
           'w_ffn_out', 'norm_f']
AG_GROUPS = [['w_in'], ['w_glu', 'w_uq', 'w_ukv', 'w_mla_o', 'w_out'], ['w_ffn_in', 'w_ffn_out']]
SMALL = ['norm1', 'norm2', 's5_a_re', 's5_a_im', 's5_log_dt', 's5_b_re', 's5_b_im', 's5_c_re', 's5_c_im', 's5_d',
         'q_norm', 'kv_norm', 'norm_f']


def _pad_rows(a, rows):
    return jnp.concatenate([a, jnp.zeros((rows - a.shape[0],) + a.shape[1:], a.dtype)], axis=0)


def _pack(vals, width, rows):
    flat = jnp.concatenate([v.reshape(-1).astype(F32) for v in vals])
    flat = jnp.concatenate([flat, jnp.zeros((rows * width - flat.shape[0],), F32)])
    return flat.reshape(rows, width)


def _unpack(buf, like):
    flat = buf.reshape(-1)
    out, pos = [], 0
    for v in like:
        out.append(flat[pos:pos + v.size].reshape(v.shape))
        pos += v.size
    return out


def _step(x, c, ctx, loss_target, w, m, v):
    px, py, pc = _place()
    me = 4 * px + 2 * py + pc
    me_chip = 2 * px + py
    L, D = x.shape[1], x.shape[2]
    Lc = ctx.shape[1]
    T = L + Lc
    SW = D // 2
    G = SW // S5_GROUP
    C = G * S5_STATE
    H = MLA_HEADS
    q_rank = w['q_norm'].shape[1]
    kv_rank = w['kv_norm'].shape[1]
    d_ff = w['w_ffn_out'].shape[1] * 4
    wa_used = SW + q_rank + kv_rank + QK_ROPE
    WA = -(-(SW + q_rank + kv_rank + LANES) // 512) * 512

    c_rows = _pad_rows(c.astype(F32), SUBLANES)
    c_all = _allgather8(c_rows, name="ag_cond")[:, 0, :]
    cond = jnp.concatenate([c_all, w['c_ctx'].reshape(1, D)], axis=0)
    cond = _pad_rows(cond, 16)
    (act,) = _rw(lambda t: (jax.nn.silu(t),), [cond], [], [F32], name="cond_silu")
    w_mod, cs_mod = w['w_mod'][0], w['w_mod'].shape[2]
    mod_part = _mm(act, w_mod, out_dtype=F32, name="mod_fwd")
    mod_all = _allgather8(mod_part, name="ag_mod")
    mod_full = jnp.concatenate([mod_all[0], mod_all[2], mod_all[4], mod_all[6]], axis=1) + w['b_mod']
    m_lat = lax.dynamic_slice_in_dim(mod_full, me, 1, axis=0).reshape(6, D)
    m_ctx = mod_full[8].reshape(6, D)
    sh1, sc1, g1, sh2, sc2, g2 = (m_lat[i:i + 1] for i in range(6))
    csh1, csc1 = m_ctx[0:1], m_ctx[1:2]

    ag_groups = [([_into_slot(w[nme][0], me_chip, 4, BF16, name=f"cast_{nme}") for nme in grp], _plan_ag_ici, 3 * len(grp))
                 for grp in AG_GROUPS]
    ag_sems, ag_bufs, ag_token = _copy_start(ag_groups, name="ag_start")
    gathered, ag_pair = {}, {}

    def landed(g, after):
        n_cp = 3 * len(AG_GROUPS[g])
        got = _copy_wait(ag_bufs[g], ag_sems[g], _plan_ag_ici, n_cp, after, name=f"ag_wait_{g}")
        sems, (got,), token = _copy_start([(got, _plan_ag_pair, n_cp)], name=f"ag_pair_start_{g}")
        ag_pair[g] = (sems[0], got)
        return token[0, 0]

    def arrive(g, after):
        sems, got = ag_pair[g]
        got = _copy_wait(got, sems, _plan_ag_pair, 3 * len(AG_GROUPS[g]), after, name=f"ag_pair_wait_{g}")
        gathered.update(zip(AG_GROUPS[g], got))

    xs = _to_segments(x[0])
    cs = _to_segments(ctx[0])
    tgt = _to_segments(loss_target[0])
    cos, sin = _rope_tables(L)
    n1, n2, nf = w['norm1'], w['norm2'], w['norm_f'].reshape(1, D)
    qg, kvg = w['q_norm'], w['kv_norm']

    (xn_lat,) = _rw(_f_norm_mod, [xs], [n1 + ag_token[0, 0], sc1, sh1], [BF16], name="norm1_lat")
    (xn_ctx,) = _rw(_f_norm_mod, [cs], [n1, csc1, csh1], [BF16], name="norm1_ctx")
    xn = jnp.concatenate([xn_lat, xn_ctx], axis=0)
    landed(0, [xn])

    gpb = min(S5_BLOCK_GROUPS, G)
    gpo = min(8, G)
    d_skip = w['s5_d'][0].reshape(1, SW)
    disc, vjp_disc, w_b, w_c = [], [], [], []
    for d in range(2):
        prm = (w['s5_a_re'][0, d], w['s5_a_im'][0, d], w['s5_log_dt'][0, d], w['s5_b_re'][0, d], w['s5_b_im'][0, d])

        def prep(a_re, a_im, log_dt, b_re, b_im):
            ab_re, ab_im, bb_re, bb_im = _s5_discretize(a_re, a_im, log_dt, b_re, b_im)
            return ab_re.reshape(1, C), ab_im.reshape(1, C), _diag_blocks_in(bb_re, gpb), _diag_blocks_in(bb_im, gpb)

        out, vj = jax.vjp(prep, *prm)
        disc.append(out)
        vjp_disc.append(vj)
        w_b += [out[2], out[3]]
        w_c += [_diag_blocks_out(w['s5_c_re'][0, d], gpo), -_diag_blocks_out(w['s5_c_im'][0, d], gpo)]
    nb_in = G // gpb
    nb_out = G // gpo

    arrive(0, [xn, tgt] + w_b + w_c)
    w_in = _from_col_blocks(gathered['w_in'])
    w_a = jnp.concatenate([w_in[:, :wa_used], jnp.zeros((D, WA - wa_used), BF16)], axis=1)
    w_g = w_in[:, wa_used:]
    ha = _mm(xn, w_a, out_dtype=F32, name="in_proj")
    ha_lat, ha_ctx = ha[:L], ha[L:]
    gt = _mm(xn_lat, w_g, out_dtype=F32, name="in_gates")
    f_post_lat = _make_f_post_in(SW, q_rank, kv_rank, True)
    f_post_ctx = _make_f_post_in(SW, q_rank, kv_rank, False)
    u_lat, cqn, ckvn_lat, kr_lat = _rw(f_post_lat, [ha_lat, cos, sin], [qg, kvg], [F32, BF16, BF16, BF16], name="post_in_lat")
    u_ctx, ckvn_ctx, kr_ctx = _rw(f_post_ctx, [ha_ctx], [kvg], [F32, BF16, BF16], name="post_in_ctx")
    zero = jnp.zeros((1, C), F32) + landed(1, [u_lat, u_ctx])

    h_lat, h_ctx, hT_ctx = [], [], []
    for d, rev in enumerate((False, True)):
        lr, li = disc[d][0], disc[d][1]
        hcr, hci, tr, ti = _s5_scan(u_ctx, w_b[2 * d], w_b[2 * d + 1], lr, li, zero, zero, zero, zero, reverse=rev,
                                    name=f"s5_scan_ctx_{d}")
        hlr, hli, _, _ = _s5_scan(u_lat, w_b[2 * d], w_b[2 * d + 1], lr, li, tr, ti, zero, zero, reverse=rev,
                                  name=f"s5_scan_lat_{d}")
        h_ctx += [hcr, hci]
        h_lat += [hlr, hli]
        hT_ctx += [tr, ti]
    r5 = _bd_fanin(h_lat, w_c, name="s5_readout")
    (z,) = _rw(_f_s5post, [u_lat, r5], [d_skip], [BF16], name="s5_post")

    arrive(1, [z])
    w_glu, w_ukv, w_mla_o = (gathered[nme] for nme in ('w_glu', 'w_ukv', 'w_mla_o'))
    w_out = gathered['w_out'].reshape(D, D)
    uq3 = _from_col_blocks(gathered['w_uq']).reshape(q_rank, H, QK_NOPE + QK_ROPE)
    w_q2 = jnp.concatenate([uq3, jnp.zeros((q_rank, H, LANES - QK_ROPE), BF16)], axis=2).reshape(q_rank, H * 2 * LANES)
    q2 = _mm(cqn, w_q2, out_dtype=F32, name="q_up")
    (qq,) = _rw(_f_qpost, [q2, cos, sin], [], [BF16], name="q_rope")
    kvn = jnp.concatenate([ckvn_lat, ckvn_ctx], axis=0)
    kr_all = jnp.concatenate([kr_lat, kr_ctx], axis=0)
    kv = _mm(kvn, w_ukv, b_shards=4, out_dtype=BF16, name="kv_up")
    kr_all = kr_all + landed(2, [kv, qq]).astype(BF16)
    o = _attn_fwd(qq, kv, kr_all, name="attn_fwd")

    ab = _mm(z, w_glu, b_shards=4, out_dtype=F32, name="glu_proj")
    bm = _mm(o, w_mla_o, b_shards=4, out_dtype=F32, name="mla_out")
    (mix,) = _rw(_f_merge, [ab, bm, gt], [], [BF16], name="merge")
    out1 = _mm(mix, w_out, out_dtype=F32, name="out_proj")
    x1, xn2 = _rw(_f_resid_norm, [xs, out1], [g1, n2, sc2, sh2], [F32, BF16], name="resid_norm2")
    arrive(2, [xn2])
    w_ffn_in = gathered['w_ffn_in']
    w_ffn_out = gathered['w_ffn_out'].reshape(d_ff, D)
    hmid, ab2 = _ffn_in_swiglu(xn2, w_ffn_in, name="ffn_in")
    f2 = _mm(hmid, w_ffn_out, out_dtype=F32, name="ffn_out")
    (row_loss,) = _rw(_f_final, [x1, f2, tgt], [g2, nf], [F32], name="final_loss")
    loss = lax.psum(jnp.sum(row_loss), ("x", "y", "c"))

    ones = jnp.ones((L, 1), F32)
    (dx1_a, df2), (dg2, dnf) = _rw_vjp(_f_final, [x1, f2, tgt], [g2, nf], [[ones]], [True, True, False], [True, True],
                                       [F32, BF16], name="final_loss_bwd")
    gw_ffn_out = _mm(hmid, df2, ta=True, out_dtype=BF16, name="ffn_out_dw")
    dab2 = _ffn_out_dx_swiglu(df2, w_ffn_out, ab2, name="ffn_out_dx")
    dxn2 = _mm(dab2, w_ffn_in, tb=True, a_shards=2, b_shards=4, out_dtype=F32, name="ffn_in_dx")
    gw_ffn_in = _mm(xn2, dab2, ta=True, b_shards=2, out_shards=4, out_dtype=BF16, name="ffn_in_dw")
    rs_ffn, tok = _rs_stage1([gw_ffn_out.reshape(4, -1, D), gw_ffn_in], "ffn")
    (dx_a, dout1), (dg1, dn2, dsc2, dsh2) = _rw_vjp(
        _f_resid_norm, [xs, out1], [g1, n2 + tok[0, 0], sc2, sh2], [[dx1_a], [dxn2]], [True, True], [True] * 4, [F32, BF16],
        name="resid_norm2_bwd")
    dmix = _mm(dout1, w_out, tb=True, out_dtype=F32, name="out_proj_dx")
    gw_out = _mm(mix, dout1, ta=True, out_dtype=BF16, name="out_proj_dw")
    (dab, dbm, dgt), _ = _rw_vjp(_f_merge, [ab, bm, gt], [], [[dmix]], [True] * 3, [], [BF16] * 3, name="merge_bwd")
    dz = _mm(dab, w_glu, tb=True, b_shards=4, out_dtype=F32, name="glu_proj_dx")
    gw_glu = _mm(z, dab, ta=True, out_shards=4, out_dtype=BF16, name="glu_proj_dw")
    do = _mm(dbm, w_mla_o, tb=True, b_shards=4, out_dtype=BF16, name="mla_out_dx")
    gw_mla_o = _mm(o, dbm, ta=True, out_shards=4, out_dtype=BF16, name="mla_out_dw")
    dxn_g = _mm(dgt, w_g, tb=True, out_dtype=F32, name="in_gates_dx")
    gw_g = _mm(xn_lat, dgt, ta=True, out_dtype=BF16, name="in_gates_dw")
    rs_ffn, tok = _rs_stage2(rs_ffn, [gw_g], "ffn")

    (du_a, dr5), (dd_skip,) = _rw_vjp(_f_s5post, [u_lat, r5], [d_skip + tok[0, 0]], [[dz]], [True, True], [True], [F32, F32],
                                      name="s5_post_bwd")
    dw_c = _bd_dw(h_lat, [dr5] * 4, nb_out, name="s5_readout_dw")
    w_ct = _tr(w_c)
    zeros_ctx = jnp.zeros((Lc, SW), BF16)
    mu_lat, mu_ctx, dlam = [], [], []
    for d, rev in enumerate((False, True)):
        lr, li = disc[d][0], disc[d][1]
        mlr, mli, fr, fi = _s5_scan(dr5, w_ct[2 * d], w_ct[2 * d + 1], lr, -li, zero, zero, zero, zero, reverse=not rev,
                                    name=f"s5_adj_lat_{d}")
        dh0r, dh0i = _cmul(lr, -li, fr, fi)
        mcr, mci, _, _ = _s5_scan(zeros_ctx, w_ct[2 * d], w_ct[2 * d + 1], lr, -li, zero, zero, dh0r, dh0i,
                                  reverse=not rev, name=f"s5_adj_ctx_{d}")
        dl_lat = _s5_dlam(mlr, mli, h_lat[2 * d], h_lat[2 * d + 1], hT_ctx[2 * d], hT_ctx[2 * d + 1], reverse=rev,
                          name=f"s5_dlam_lat_{d}")
        dl_ctx = _s5_dlam(mcr, mci, h_ctx[2 * d], h_ctx[2 * d + 1], zero, zero, reverse=rev, name=f"s5_dlam_ctx_{d}")
        mu_lat += [mlr, mli]
        mu_ctx += [mcr, mci]
        dlam.append((dl_lat[0] + dl_ctx[0], dl_lat[1] + dl_ctx[1]))
    du_b = _bd_fanin(mu_lat, _tr(w_b), name="s5_bu_lat_dx")
    du_ctx = _bd_fanin(mu_ctx, _tr(w_b), name="s5_bu_ctx_dx")
    dw_b_lat = _bd_dw([u_lat] * 4, mu_lat, nb_in, name="s5_bu_lat_dw")
    dw_b_ctx = _bd_dw([u_ctx] * 4, mu_ctx, nb_in, name="s5_bu_ctx_dw")
    g_s5 = {}
    for d in range(2):
        ct = (dlam[d][0], dlam[d][1], dw_b_lat[2 * d] + dw_b_ctx[2 * d], dw_b_lat[2 * d + 1] + dw_b_ctx[2 * d + 1])
        ga_re, ga_im, gdt, gb_re, gb_im = vjp_disc[d](ct)
        _, vj_c = jax.vjp(lambda cr, ci: (_diag_blocks_out(cr, gpo), -_diag_blocks_out(ci, gpo)),
                          w['s5_c_re'][0, d], w['s5_c_im'][0, d])
        gc_re, gc_im = vj_c((dw_c[2 * d], dw_c[2 * d + 1]))
        for nme, val in (('s5_a_re', ga_re), ('s5_a_im', ga_im), ('s5_log_dt', gdt), ('s5_b_re', gb_re),
                         ('s5_b_im', gb_im), ('s5_c_re', gc_re), ('s5_c_im', gc_im)):
            g_s5.setdefault(nme, []).append(val)
    g_small = {nme: jnp.stack(vals)[None] for nme, vals in g_s5.items()}
    g_small['s5_d'] = dd_skip.reshape(w['s5_d'].shape)

    dqq, dkv, dkr = _attn_bwd(qq, kv, kr_all, do, name="attn_bwd")
    (dq2,), _ = _rw_vjp(_f_qpost, [q2, cos, sin], [], [[dqq]], [True, False, False], [], [BF16], name="q_rope_bwd")
    dcqn = _mm(dq2, w_q2, tb=True, out_dtype=F32, name="q_up_dx")
    gw_q2 = _mm(cqn, dq2, ta=True, out_dtype=BF16, name="q_up_dw")
    dckvn = _mm(dkv, w_ukv, tb=True, b_shards=4, out_dtype=F32, name="kv_up_dx")
    gw_ukv = _mm(kvn, dkv, ta=True, out_shards=4, out_dtype=BF16, name="kv_up_dw")
    gw_uq = gw_q2.reshape(q_rank, H, 2 * LANES)[:, :, :QK_NOPE + QK_ROPE].reshape(q_rank, H * (QK_NOPE + QK_ROPE))
    rs_mix, tok = _rs_stage1([gw_out.reshape(4, -1, D), gw_glu, gw_mla_o, _col_blocks(gw_uq, 4), gw_ukv], "mix")

    (dha_lat,), (dqg, dkvg_lat) = _rw_vjp(
        f_post_lat, [ha_lat, cos, sin], [qg, kvg + tok[0, 0]], [[du_a, du_b], [dcqn], [dckvn[:L]], [dkr[:L]]],
        [True, False, False], [True, True], [BF16], name="post_in_lat_bwd")
    (dha_ctx,), (dkvg_ctx,) = _rw_vjp(f_post_ctx, [ha_ctx], [kvg], [[du_ctx], [dckvn[L:]], [dkr[L:]]], [True], [True],
                                      [BF16], name="post_in_ctx_bwd")
    dha = jnp.concatenate([dha_lat, dha_ctx], axis=0)
    dxn = _mm(dha, w_a, tb=True, out_dtype=F32, name="in_proj_dx")
    gw_a = _mm(xn, dha, ta=True, out_dtype=BF16, name="in_proj_dw")
    rs_mix, tok = _rs_stage2(rs_mix, [gw_a], "mix")
    (dx_seg,), (dn1_lat, dsc1, dsh1) = _rw_vjp(
        _f_norm_mod_keep, [xs], [n1 + tok[0, 0], sc1, sh1], [[dxn[:L], dxn_g], [dx_a]], [True], [True] * 3, [F32],
        name="norm1_lat_bwd")
    _, (dn1_ctx, dcsc1, dcsh1) = _rw_vjp(_f_norm_mod, [cs], [n1, csc1, csh1], [[dxn[L:]]], [False], [True] * 3, [],
                                         name="norm1_ctx_bwd")
    grad_x = _from_segments(dx_seg)[None]
    g_small.update(norm1=dn1_lat + dn1_ctx, norm2=dn2, q_norm=dqg, kv_norm=dkvg_lat + dkvg_ctx, norm_f=dnf.reshape(D))
    gw_in = jnp.concatenate([gw_a[:, :wa_used], gw_g], axis=1)
    small_vals = [g_small[nme] for nme in SMALL]
    n_small = sum(val.size for val in small_vals)
    small_rows = -(-n_small // (LANES * 4 * 32)) * 32

    zD = jnp.zeros((1, D), F32)
    dm = jnp.concatenate([
        jnp.concatenate([dsh1, dsc1, dg1, dsh2, dsc2, dg2], axis=1),
        jnp.concatenate([dcsh1, dcsc1, zD, zD, zD, zD], axis=1),
    ], axis=0)
    dm_all = _allgather8(_pad_rows(dm, SUBLANES), name="ag_dmod")
    rs_in, _ = _rs_stage1([_col_blocks(gw_in, 4), _pack(small_vals, LANES, 4 * small_rows).reshape(4, small_rows, LANES)],
                          "in", after=[dm_all])
    dm_ctx = dm_all[0, 1]
    for k in range(1, 8):
        dm_ctx = dm_ctx + dm_all[k, 1]
    dmod = _pad_rows(jnp.concatenate([dm_all[:, 0, :], dm_ctx[None]], axis=0), 16)
    g_b_mod = jnp.sum(dmod, axis=0, keepdims=True)
    dmod_mine = lax.dynamic_slice_in_dim(dmod, me_chip * cs_mod, cs_mod, axis=1)
    g_w_mod = _mm(act, dmod_mine, ta=True, out_dtype=F32, name="mod_dw")
    dact_part = _mm(dmod_mine, w_mod, tb=True, out_dtype=F32, name="mod_dx")
    dact_all = _allgather8(dact_part, name="ag_dact")
    dact = dact_all[0] + dact_all[2] + dact_all[4] + dact_all[6]
    (dcond_rows,), _ = _rw_vjp(lambda t: (jax.nn.silu(t),), [cond], [], [[dact]], [True], [], [F32], name="cond_silu_bwd")
    g_c_ctx = dcond_rows[8]

    rs_in, tok = _rs_stage2(rs_in, [g_c_ctx], "in")

    grads, delta, new_m, new_v = {}, {}, {}, {}

    def update(members, reds, anchor):
        deltas = []
        for nme, red in zip(members, reds):
            res = _adamw(w[nme][0], red, m[nme][0], v[nme][0], name=f"adamw_{nme}", anchor=anchor)
            grads[nme], delta[nme], new_m[nme], new_v[nme] = (r.reshape(w[nme].shape) for r in res)
            deltas.append(res[1])
            anchor = None
        return deltas

    rs_ffn, tok = _rs_stage3(rs_ffn, [tok], "ffn")
    done = update(['w_mod'], [g_w_mod], tok)
    red_ffn = _rs_stage4(rs_ffn, done, "ffn")
    rs_mix, tok = _rs_stage3(rs_mix, red_ffn[:1], "mix")
    done = update(['w_ffn_out', 'w_ffn_in'], red_ffn, tok)
    red_mix = _rs_stage4(rs_mix, done, "mix")
    rs_in, tok = _rs_stage3(rs_in, red_mix[:1], "in")
    done = update(['w_out', 'w_glu', 'w_mla_o', 'w_uq', 'w_ukv'], red_mix, tok)
    red_in = _rs_stage4(rs_in, done, "in")
    update(['w_in'], red_in[:1], None)
    small_mine = red_in[-1]
    small_buf = _into_slot(small_mine, me_chip, 4, F32, name="small_grads_slot")
    small_all = _allgather_shards([small_buf], name="ag_small_grads")[0].reshape(4 * small_rows, LANES)
    g_small_red = dict(zip(SMALL, _unpack(small_all, [w[nme] for nme in SMALL])))
    rest = SMALL + ['c_ctx', 'b_mod']
    g_rest = dict(g_small_red, c_ctx=g_c_ctx, b_mod=g_b_mod)
    rows_rest = -(-sum(w[nme].size for nme in rest) // (LANES * 16)) * 16
    packed = [_pack([src[nme] for nme in rest], LANES, rows_rest) for src in (w, g_rest, m, v)]
    res = _adamw(*packed, name="adamw_small")
    for dst, buf in zip((grads, delta, new_m, new_v), res):
        dst.update(zip(rest, _unpack(buf, [w[nme] for nme in rest])))
    return (loss, grad_x, *[grads[nme] for nme in WEIGHTS], *[delta[nme] for nme in WEIGHTS],
            *[new_m[nme] for nme in WEIGHTS], *[new_v[nme] for nme in WEIGHTS])


def kernel(x, c, ctx, c_ctx, w_mod, b_mod, norm1, norm2, w_in, s5_a_re, s5_a_im, s5_log_dt, s5_b_re, s5_b_im, s5_c_re, s5_c_im, s5_d, w_glu, q_norm, kv_norm, w_uq, w_ukv, w_mla_o, w_out, w_ffn_in, w_ffn_out, norm_f, loss_target, m_c_ctx, m_w_mod, m_b_mod, m_norm1, m_norm2, m_w_in, m_s5_a_re, m_s5_a_im, m_s5_log_dt, m_s5_b_re, m_s5_b_im, m_s5_c_re, m_s5_c_im, m_s5_d, m_w_glu, m_q_norm, m_kv_norm, m_w_uq, m_w_ukv, m_w_mla_o, m_w_out, m_w_ffn_in, m_w_ffn_out, m_norm_f, v_c_ctx, v_w_mod, v_b_mod, v_norm1, v_norm2, v_w_in, v_s5_a_re, v_s5_a_im, v_s5_log_dt, v_s5_b_re, v_s5_b_im, v_s5_c_re, v_s5_c_im, v_s5_d, v_w_glu, v_q_norm, v_kv_norm, v_w_uq, v_w_ukv, v_w_mla_o, v_w_out, v_w_ffn_in, v_w_ffn_out, v_norm_f):
    w = dict(c_ctx=c_ctx, w_mod=w_mod, b_mod=b_mod, norm1=norm1, norm2=norm2, w_in=w_in, s5_a_re=s5_a_re, s5_a_im=s5_a_im,
             s5_log_dt=s5_log_dt, s5_b_re=s5_b_re, s5_b_im=s5_b_im, s5_c_re=s5_c_re, s5_c_im=s5_c_im, s5_d=s5_d, w_glu=w_glu,
             q_norm=q_norm, kv_norm=kv_norm, w_uq=w_uq, w_ukv=w_ukv, w_mla_o=w_mla_o, w_out=w_out, w_ffn_in=w_ffn_in,
             w_ffn_out=w_ffn_out, norm_f=norm_f)
    m = dict(c_ctx=m_c_ctx, w_mod=m_w_mod, b_mod=m_b_mod, norm1=m_norm1, norm2=m_norm2, w_in=m_w_in, s5_a_re=m_s5_a_re,
             s5_a_im=m_s5_a_im, s5_log_dt=m_s5_log_dt, s5_b_re=m_s5_b_re, s5_b_im=m_s5_b_im, s5_c_re=m_s5_c_re,
             s5_c_im=m_s5_c_im, s5_d=m_s5_d, w_glu=m_w_glu, q_norm=m_q_norm, kv_norm=m_kv_norm, w_uq=m_w_uq, w_ukv=m_w_ukv,
             w_mla_o=m_w_mla_o, w_out=m_w_out, w_ffn_in=m_w_ffn_in, w_ffn_out=m_w_ffn_out, norm_f=m_norm_f)
    v = dict(c_ctx=v_c_ctx, w_mod=v_w_mod, b_mod=v_b_mod, norm1=v_norm1, norm2=v_norm2, w_in=v_w_in, s5_a_re=v_s5_a_re,
             s5_a_im=v_s5_a_im, s5_log_dt=v_s5_log_dt, s5_b_re=v_s5_b_re, s5_b_im=v_s5_b_im, s5_c_re=v_s5_c_re,
             s5_c_im=v_s5_c_im, s5_d=v_s5_d, w_glu=v_w_glu, q_norm=v_q_norm, kv_norm=v_kv_norm, w_uq=v_w_uq, w_ukv=v_w_ukv,
             w_mla_o=v_w_mla_o, w_out=v_w_out, w_ffn_in=v_w_ffn_in, w_ffn_out=v_w_ffn_out, norm_f=v_norm_f)
    return _step(x, c, ctx, loss_target, w, m, v)
```

```python
import functools
import math

import jax
import jax.numpy as jnp
from jax import lax
from jax.experimental import pallas as pl
from jax.experimental.pallas import tpu as pltpu

F32 = jnp.float32
BF16 = jnp.bfloat16

EPS = 1e-6
GRID_W = 64
S5_GROUP = 16
S5_STATE = 64
MLA_HEADS = 8
QK_NOPE = 128
QK_ROPE = 64
V_DIM = 128
ROPE_BASE = 10000.0
ATTN_SCALE = (QK_NOPE + QK_ROPE) ** -0.5
ADAM_LR = 0.001
ADAM_B1 = 0.9
ADAM_B2 = 0.999
ADAM_EPS = 1e-08
ADAM_WD = 0.01
ADAM_STEP = 10

SUBLANES = 8
LANES = 128
V7X_VMEM_BYTES = 64 * 1024 * 1024
VMEM_LIMIT = (V7X_VMEM_BYTES * 7) // 8
N_SEG = 2 * SUBLANES
S5_BLOCK_GROUPS = 8
MESH = pl.DeviceIdType.MESH


def _pick(n, target, mult):
    best = None
    d = mult
    while d <= min(n, target):
        if n % d == 0:
            best = d
        d += mult
    return n if best is None else best


def _cparams(sem=None):
    return pltpu.CompilerParams(dimension_semantics=sem, vmem_limit_bytes=VMEM_LIMIT)


MM_VMEM_BUDGET = (V7X_VMEM_BYTES * 5) // 8


def _mm(a, b, *, ta=False, tb=False, out_dtype=F32, name, a_shards=1, b_shards=1, out_shards=1):
    if ta:
        K, M = a.shape
    else:
        M, K = a.shape[-2], a.shape[-1] * a_shards
    if tb:
        N, K2 = b.shape[-2], b.shape[-1] * b_shards
    else:
        K2, N = b.shape[-2], b.shape[-1] * b_shards
    assert K == K2, (a.shape, b.shape, ta, tb)
    n_unit = N // max(out_shards, 1 if tb else b_shards)
    k_unit = K // max(a_shards, b_shards if tb else 1)
    tn = _pick(n_unit, 1024, LANES)
    tm = _pick(M, 1024 if tn >= 512 else 2048, LANES if ta else 16)
    sa, sb, so = a.dtype.itemsize, b.dtype.itemsize, jnp.dtype(out_dtype).itemsize
    k_mult = LANES if (not ta or tb) else 16
    tk = k_mult if k_unit % k_mult == 0 else k_unit
    for cand in range(k_mult, k_unit + 1, k_mult):
        if k_unit % cand == 0 and 2 * cand * (tm * sa + tn * sb) + tm * tn * (4 + 2 * so) <= MM_VMEM_BUDGET:
            tk = cand
    nk = K // tk
    dims = (((0 if ta else 1,), (1 if tb else 0,)), ((), ()))

    def body(a_ref, b_ref, o_ref, *scratch):
        part = lax.dot_general(a_ref[...].astype(BF16), b_ref[...].astype(BF16), dims, preferred_element_type=F32)
        if nk == 1:
            o_ref[...] = part.astype(o_ref.dtype)
            return
        acc_ref, = scratch
        k = pl.program_id(2)

        @pl.when(k == 0)
        def _():
            acc_ref[...] = part

        @pl.when(k > 0)
        def _():
            acc_ref[...] += part

        @pl.when(k == nk - 1)
        def _():
            o_ref[...] = acc_ref[...].astype(o_ref.dtype)

    if ta:
        a_spec = pl.BlockSpec((tk, tm), lambda i, j, k: (k, i))
    elif a_shards == 1:
        a_spec = pl.BlockSpec((tm, tk), lambda i, j, k: (i, k))
    else:
        akb = (K // a_shards) // tk
        a_spec = pl.BlockSpec((None, tm, tk), lambda i, j, k: (k // akb, i, k % akb))
    if b_shards == 1:
        b_spec = pl.BlockSpec((tn, tk), lambda i, j, k: (j, k)) if tb else pl.BlockSpec((tk, tn), lambda i, j, k: (k, j))
    elif tb:
        kpb = (K // b_shards) // tk
        b_spec = pl.BlockSpec((None, tn, tk), lambda i, j, k: (k // kpb, j, k % kpb))
    else:
        npb = (N // b_shards) // tn
        b_spec = pl.BlockSpec((None, tk, tn), lambda i, j, k: (j // npb, k, j % npb))
    if out_shards == 1:
        out_spec = pl.BlockSpec((tm, tn), lambda i, j, k: (i, j))
        out_shape = jax.ShapeDtypeStruct((M, N), out_dtype)
    else:
        opb = (N // out_shards) // tn
        out_spec = pl.BlockSpec((None, tm, tn), lambda i, j, k: (j // opb, i, j % opb))
        out_shape = jax.ShapeDtypeStruct((out_shards, M, N // out_shards), out_dtype)
    return pl.pallas_call(
        body, name=name, grid=(M // tm, N // tn, nk),
        in_specs=[a_spec, b_spec], out_specs=out_spec, out_shape=out_shape,
        scratch_shapes=[pltpu.VMEM((tm, tn), F32)] if nk > 1 else [],
        compiler_params=_cparams(("parallel", "parallel", "arbitrary")),
    )(a, b)


FFN_TILE_ROWS = 1024


def _ffn_in_swiglu(x, w4, *, name):
    M, K = x.shape
    S, _, ns = w4.shape
    half = S * ns // 2
    tn = _pick(ns, 512, LANES)
    tm = _pick(M, FFN_TILE_ROWS, 16)
    npb = ns // tn

    def body(x_ref, wa_ref, wb_ref, h_ref, ab_ref):
        xb = x_ref[...].astype(BF16)
        a = jnp.dot(xb, wa_ref[...].astype(BF16), preferred_element_type=F32)
        b = jnp.dot(xb, wb_ref[...].astype(BF16), preferred_element_type=F32)
        h_ref[...] = (jax.nn.silu(a) * b).astype(h_ref.dtype)
        ab_ref[0] = a.astype(ab_ref.dtype)
        ab_ref[1] = b.astype(ab_ref.dtype)

    return pl.pallas_call(
        body, name=name, grid=(M // tm, half // tn),
        in_specs=[pl.BlockSpec((tm, K), lambda i, j: (i, 0)),
                  pl.BlockSpec((None, K, tn), lambda i, j: (j // npb, 0, j % npb)),
                  pl.BlockSpec((None, K, tn), lambda i, j: (S // 2 + j // npb, 0, j % npb))],
        out_specs=[pl.BlockSpec((tm, tn), lambda i, j: (i, j)), pl.BlockSpec((2, tm, tn), lambda i, j: (0, i, j))],
        out_shape=[jax.ShapeDtypeStruct((M, half), BF16), jax.ShapeDtypeStruct((2, M, half), BF16)],
        compiler_params=_cparams(("parallel", "parallel")),
    )(x, w4, w4)


def _ffn_out_dx_swiglu(dy, w, ab, *, name):
    M, D = dy.shape
    n2 = w.shape[0]
    tn = _pick(n2, 512, LANES)
    tm = _pick(M, FFN_TILE_ROWS, 16)

    def body(dy_ref, w_ref, ab_ref, o_ref):
        dh = lax.dot_general(dy_ref[...].astype(BF16), w_ref[...].astype(BF16), NT_DIMS, preferred_element_type=F32)
        a, b = ab_ref[0].astype(F32), ab_ref[1].astype(F32)
        s = jax.nn.sigmoid(a)
        o_ref[0] = (dh * b * (s * (1.0 + a * (1.0 - s)))).astype(o_ref.dtype)
        o_ref[1] = (dh * (a * s)).astype(o_ref.dtype)

    return pl.pallas_call(
        body, name=name, grid=(M // tm, n2 // tn),
        in_specs=[pl.BlockSpec((tm, D), lambda i, j: (i, 0)), pl.BlockSpec((tn, D), lambda i, j: (j, 0)),
                  pl.BlockSpec((2, tm, tn), lambda i, j: (0, i, j))],
        out_specs=pl.BlockSpec((2, tm, tn), lambda i, j: (0, i, j)),
        out_shape=jax.ShapeDtypeStruct((2, M, n2), BF16),
        compiler_params=_cparams(("parallel", "parallel")),
    )(dy, w, ab)


def _row_tile(tiled, extra_bytes=0):
    rows = tiled[0].shape[0]
    per_row = sum(a.shape[1] * 4 for a in tiled) + extra_bytes
    target = max(SUBLANES, (6 * 1024 * 1024) // max(per_row, 1))
    return _pick(rows, min(target, 512), 16)


def _rw(f, tiled, bcast, out_dtypes, *, name, anchor=None):
    nt, nb = len(tiled), len(bcast)
    rows = tiled[0].shape[0]
    outs_aval = jax.eval_shape(f, *[jax.ShapeDtypeStruct((16, a.shape[1]), F32) for a in tiled],
                               *[jax.ShapeDtypeStruct(b.shape, F32) for b in bcast])
    widths = [o.shape[1] for o in outs_aval]
    tm = _row_tile(tiled, sum(w * 4 for w in widths))

    extra = [] if anchor is None else [anchor]
    n_in = nt + nb + len(extra)

    def body(*refs):
        tin = [r[...].astype(F32) for r in refs[:nt]]
        bin_ = [r[...].astype(F32) for r in refs[nt:nt + nb]]
        outs = f(*tin, *bin_)
        for o_ref, o in zip(refs[n_in:], outs):
            o_ref[...] = o.astype(o_ref.dtype)

    in_specs = [pl.BlockSpec((tm, a.shape[1]), lambda i: (i, 0)) for a in tiled]
    in_specs += [pl.BlockSpec(b.shape, lambda i: (0, 0)) for b in bcast + extra]
    res = pl.pallas_call(
        body, name=name, grid=(rows // tm,), in_specs=in_specs,
        out_specs=[pl.BlockSpec((tm, w), lambda i: (i, 0)) for w in widths],
        out_shape=[jax.ShapeDtypeStruct((rows, w), dt) for w, dt in zip(widths, out_dtypes)],
        compiler_params=_cparams(("parallel",)),
    )(*tiled, *bcast, *extra)
    return list(res)


def _rw_vjp(f, tiled, bcast, cts, need_t, need_b, t_dtypes, *, name):
    nt, nb = len(tiled), len(bcast)
    rows = tiled[0].shape[0]
    flat_cts = [c for group in cts for c in group]
    t_idx = [i for i in range(nt) if need_t[i]]
    b_idx = [i for i in range(nb) if need_b[i]]
    tm = _row_tile(list(tiled) + flat_cts, sum(tiled[i].shape[1] * 4 for i in t_idx))
    nc = len(flat_cts)

    def body(*refs):
        i = pl.program_id(0)
        tin = [r[...].astype(F32) for r in refs[:nt]]
        bin_ = [r[...].astype(F32) for r in refs[nt:nt + nb]]
        ct_refs = refs[nt + nb:nt + nb + nc]
        out_refs = refs[nt + nb + nc:]
        outs, vjp_fn = jax.vjp(f, *tin, *bin_)
        ct_vals, pos = [], 0
        for o, group in zip(outs, cts):
            acc = jnp.zeros_like(o)
            for _ in group:
                acc = acc + ct_refs[pos][...].astype(F32)
                pos += 1
            ct_vals.append(acc)
        grads = vjp_fn(tuple(ct_vals))
        for o_ref, k in zip(out_refs[:len(t_idx)], t_idx):
            o_ref[...] = grads[k].astype(o_ref.dtype)
        for o_ref, k in zip(out_refs[len(t_idx):], b_idx):
            @pl.when(i == 0)
            def _(o_ref=o_ref):
                o_ref[...] = jnp.zeros_like(o_ref)

            o_ref[...] += grads[nt + k]

    in_specs = [pl.BlockSpec((tm, a.shape[1]), lambda i: (i, 0)) for a in tiled]
    in_specs += [pl.BlockSpec(b.shape, lambda i: (0, 0)) for b in bcast]
    in_specs += [pl.BlockSpec((tm, c.shape[1]), lambda i: (i, 0)) for c in flat_cts]
    out_specs = [pl.BlockSpec((tm, tiled[k].shape[1]), lambda i: (i, 0)) for k in t_idx]
    out_specs += [pl.BlockSpec(bcast[k].shape, lambda i: (0, 0)) for k in b_idx]
    out_shape = [jax.ShapeDtypeStruct(tiled[k].shape, dt) for k, dt in zip(t_idx, t_dtypes)]
    out_shape += [jax.ShapeDtypeStruct(bcast[k].shape, F32) for k in b_idx]
    res = pl.pallas_call(
        body, name=name, grid=(rows // tm,), in_specs=in_specs, out_specs=out_specs, out_shape=out_shape,
        compiler_params=_cparams(("arbitrary",)),
    )(*tiled, *bcast, *flat_cts)
    res = list(res)
    return res[:len(t_idx)], res[len(t_idx):]


def _rms(x, g):
    return x * lax.rsqrt(jnp.mean(x * x, axis=-1, keepdims=True) + EPS) * g


def _f_norm_mod(x, g, sc, sh):
    return (_rms(x, g) * (1.0 + sc) + sh,)


def _f_norm_mod_keep(x, g, sc, sh):
    return (_rms(x, g) * (1.0 + sc) + sh, x)


@jax.custom_vjp
def _swap16(x):
    w = x.shape[-1]
    lane = lax.broadcasted_iota(jnp.int32, x.shape, x.ndim - 1)
    return jnp.where((lane & 16) == 0, pltpu.roll(x, w - 16, x.ndim - 1), pltpu.roll(x, 16, x.ndim - 1))


_swap16.defvjp(lambda x: (_swap16(x), None), lambda _, g: (_swap16(g),))


def _rope(x, cos, sin):
    return x * cos + _swap16(x) * sin


def _make_f_post_in(sw, q_rank, kv_rank, with_q):
    o1, o2, o3 = sw, sw + q_rank, sw + q_rank + kv_rank

    if with_q:
        def f(ha, cos, sin, qg, kvg):
            u = ha[:, :o1]
            cqn = _rms(ha[:, o1:o2], qg)
            ckvn = _rms(ha[:, o2:o3], kvg)
            kr = _rope(ha[:, o3:o3 + LANES], cos, sin)
            return u, cqn, ckvn, kr
    else:
        def f(ha, kvg):
            return ha[:, :o1], _rms(ha[:, o2:o3], kvg), ha[:, o3:o3 + LANES]
    return f


def _f_qpost(q2, cos, sin):
    parts = []
    for h in range(q2.shape[1] // (2 * LANES)):
        o = 2 * LANES * h
        parts += [q2[:, o:o + LANES], _rope(q2[:, o + LANES:o + 2 * LANES], cos, sin)]
    return (jnp.concatenate(parts, axis=1),)


def _f_s5post(u, r, d):
    return (jax.nn.gelu(d * u + r, approximate=True),)


def _f_merge(ab, bm, gt):
    d = bm.shape[1]
    br_s5 = ab[:, :d] * jax.nn.sigmoid(ab[:, d:])
    g = jax.nn.sigmoid(gt)
    return (g[:, :d] * br_s5 + g[:, d:] * bm,)


def _f_resid_norm(x, out, g1, n2, sc2, sh2):
    x1 = x + g1 * out
    return x1, _rms(x1, n2) * (1.0 + sc2) + sh2


def _f_final(x1, f, tgt, g2, nf):
    y = _rms(x1 + g2 * f, nf)
    return (0.5 * jnp.mean(jnp.square(y - tgt), axis=-1, keepdims=True),)


def _bd_fanin(xs, ws, *, name):
    nw = len(ws)
    nb, kb, nn = ws[0].shape
    T = xs[0].shape[0]
    tm = _pick(T, 512, 16)

    def body(*refs):
        acc = None
        for x_ref, w_ref in zip(refs[:nw], refs[nw:2 * nw]):
            t = jnp.dot(x_ref[...].astype(BF16), w_ref[0].astype(BF16), preferred_element_type=F32)
            acc = t if acc is None else acc + t
        refs[2 * nw][...] = acc

    return pl.pallas_call(
        body, name=name, grid=(nb, T // tm),
        in_specs=[pl.BlockSpec((tm, kb), lambda j, i: (i, j))] * nw + [pl.BlockSpec((1, kb, nn), lambda j, i: (j, 0, 0))] * nw,
        out_specs=pl.BlockSpec((tm, nn), lambda j, i: (i, j)),
        out_shape=jax.ShapeDtypeStruct((T, nb * nn), F32),
        compiler_params=_cparams(("parallel", "parallel")),
    )(*xs, *ws)


def _bd_dw(xs, dys, nb, *, name):
    npair = len(xs)
    T = xs[0].shape[0]
    kb = xs[0].shape[1] // nb
    nn = dys[0].shape[1] // nb
    tm = _pick(T, 512, 16)
    dims = (((0,), (0,)), ((), ()))

    def body(*refs):
        i = pl.program_id(1)
        for x_ref, d_ref, o_ref in zip(refs[:npair], refs[npair:2 * npair], refs[2 * npair:]):
            @pl.when(i == 0)
            def _(o_ref=o_ref):
                o_ref[...] = jnp.zeros_like(o_ref)

            o_ref[0] += lax.dot_general(x_ref[...].astype(BF16), d_ref[...].astype(BF16), dims,
                                        preferred_element_type=F32)

    return list(pl.pallas_call(
        body, name=name, grid=(nb, T // tm),
        in_specs=[pl.BlockSpec((tm, kb), lambda j, i: (i, j))] * npair + [pl.BlockSpec((tm, nn), lambda j, i: (i, j))] * npair,
        out_specs=[pl.BlockSpec((1, kb, nn), lambda j, i: (j, 0, 0))] * npair,
        out_shape=[jax.ShapeDtypeStruct((nb, kb, nn), F32)] * npair,
        compiler_params=_cparams(("parallel", "arbitrary")),
    )(*xs, *dys))


def _cmul(ar, ai, br, bi):
    return ar * br - ai * bi, ar * bi + ai * br


def _cpow(lr, li, n):
    rr, ri = None, None
    br, bi = lr, li
    while n:
        if n & 1:
            rr, ri = (br, bi) if rr is None else _cmul(rr, ri, br, bi)
        n >>= 1
        if n:
            br, bi = _cmul(br, bi, br, bi)
    return rr, ri


SCAN_MM_ROWS = 512


def _s5_scan(x, w_re, w_im, lam_re, lam_im, h0_re, h0_im, e0_re, e0_im, *, reverse, name):
    rows = x.shape[0]
    nb, kb, cb = w_re.shape
    C = nb * cb
    n = rows // N_SEG
    mm_rows = _pick(rows, SCAN_MM_ROWS, 16)
    seg_order = list(range(N_SEG))[::-1] if reverse else list(range(N_SEG))
    s_first, s_last = seg_order[0], seg_order[-1]

    def body(x_ref, wr_ref, wi_ref, lr_ref, li_ref, h0r_ref, h0i_ref, e0r_ref, e0i_ref, hr_ref, hi_ref, htr_ref, hti_ref,
             locr_ref, loci_ref):
        shape = (N_SEG, cb)
        lr = jnp.broadcast_to(lr_ref[...], shape)
        li = jnp.broadcast_to(li_ref[...], shape)
        row = lax.broadcasted_iota(jnp.int32, shape, 0)

        def step_of(k):
            return (n - 1 - k) if reverse else k

        def rows_of(k):
            return pl.ds(pl.multiple_of(step_of(k) * N_SEG, N_SEG), N_SEG)

        wr, wi = wr_ref[...].astype(BF16), wi_ref[...].astype(BF16)
        for r0 in range(0, rows, mm_rows):
            xb = x_ref[r0:r0 + mm_rows, :].astype(BF16)
            locr_ref[r0:r0 + mm_rows, :] = jnp.dot(xb, wr, preferred_element_type=F32)
            loci_ref[r0:r0 + mm_rows, :] = jnp.dot(xb, wi, preferred_element_type=F32)

        first = row == s_first
        hr = locr_ref[rows_of(0), :] + jnp.where(first, e0r_ref[...], 0.0)
        hi = loci_ref[rows_of(0), :] + jnp.where(first, e0i_ref[...], 0.0)
        locr_ref[rows_of(0), :] = hr
        loci_ref[rows_of(0), :] = hi

        def pass1(k, carry):
            hr, hi = carry
            pr, pi = _cmul(lr, li, hr, hi)
            hr = pr + locr_ref[rows_of(k), :]
            hi = pi + loci_ref[rows_of(k), :]
            locr_ref[rows_of(k), :] = hr
            loci_ref[rows_of(k), :] = hi
            return hr, hi

        er, ei = lax.fori_loop(1, n, pass1, (hr, hi))

        lnr, lni = _cpow(lr_ref[...], li_ref[...], n)
        cr, ci = h0r_ref[...], h0i_ref[...]
        cin_r = jnp.zeros(shape, F32)
        cin_i = jnp.zeros(shape, F32)
        for s in seg_order:
            cin_r = jnp.where(row == s, cr, cin_r)
            cin_i = jnp.where(row == s, ci, cin_i)
            if s != s_last:
                pr, pi = _cmul(lnr, lni, cr, ci)
                cr = pr + jnp.sum(jnp.where(row == s, er, 0.0), axis=0, keepdims=True)
                ci = pi + jnp.sum(jnp.where(row == s, ei, 0.0), axis=0, keepdims=True)

        def pass2(k, carry):
            pr, pi, _, _ = carry
            ar, ai = _cmul(pr, pi, cin_r, cin_i)
            hr = locr_ref[rows_of(k), :] + ar
            hi = loci_ref[rows_of(k), :] + ai
            hr_ref[rows_of(k), :] = hr.astype(hr_ref.dtype)
            hi_ref[rows_of(k), :] = hi.astype(hi_ref.dtype)
            npr, npi = _cmul(pr, pi, lr, li)
            return npr, npi, hr, hi

        _, _, last_r, last_i = lax.fori_loop(0, n, pass2, (lr, li, er, ei))
        htr_ref[...] = jnp.sum(jnp.where(row == s_last, last_r, 0.0), axis=0, keepdims=True)
        hti_ref[...] = jnp.sum(jnp.where(row == s_last, last_i, 0.0), axis=0, keepdims=True)

    big = pl.BlockSpec((rows, cb), lambda j: (0, j))
    vec = pl.BlockSpec((1, cb), lambda j: (0, j))
    wspec = pl.BlockSpec((None, kb, cb), lambda j: (j, 0, 0))
    return pl.pallas_call(
        body, name=name, grid=(nb,),
        in_specs=[pl.BlockSpec((rows, kb), lambda j: (0, j)), wspec, wspec] + [vec] * 6,
        out_specs=[big, big, vec, vec],
        out_shape=[jax.ShapeDtypeStruct((rows, C), BF16)] * 2 + [jax.ShapeDtypeStruct((1, C), F32)] * 2,
        scratch_shapes=[pltpu.VMEM((rows, cb), F32)] * 2,
        compiler_params=_cparams(("parallel",)),
    )(x, w_re, w_im, lam_re, lam_im, h0_re, h0_im, e0_re, e0_im)


def _s5_dlam(mu_re, mu_im, h_re, h_im, h0_re, h0_im, *, reverse, name):
    rows, C = h_re.shape
    n = rows // N_SEG
    cb = _pick(C, 256, LANES)
    s_first = N_SEG - 1 if reverse else 0

    def body(mr_ref, mi_ref, hr_ref, hi_ref, h0r_ref, h0i_ref, dr_ref, di_ref):
        shape = (N_SEG, cb)
        row = lax.broadcasted_iota(jnp.int32, shape, 0)

        def rows_of(k):
            step = (n - 1 - k) if reverse else k
            return pl.ds(pl.multiple_of(step * N_SEG, N_SEG), N_SEG)

        def term(k, pr, pi):
            mr, mi = mr_ref[rows_of(k), :].astype(F32), mi_ref[rows_of(k), :].astype(F32)
            return mr * pr + mi * pi, mi * pr - mr * pi

        shift = N_SEG - 1 if reverse else 1
        pr = jnp.where(row == s_first, h0r_ref[...], pltpu.roll(hr_ref[rows_of(n - 1), :].astype(F32), shift, 0))
        pi = jnp.where(row == s_first, h0i_ref[...], pltpu.roll(hi_ref[rows_of(n - 1), :].astype(F32), shift, 0))
        acc = term(0, pr, pi)

        def loop(k, acc):
            tr, ti = term(k, hr_ref[rows_of(k - 1), :].astype(F32), hi_ref[rows_of(k - 1), :].astype(F32))
            return acc[0] + tr, acc[1] + ti

        ar, ai = lax.fori_loop(1, n, loop, acc)
        dr_ref[...] = jnp.sum(ar, axis=0, keepdims=True)
        di_ref[...] = jnp.sum(ai, axis=0, keepdims=True)

    big = pl.BlockSpec((rows, cb), lambda j: (0, j))
    vec = pl.BlockSpec((1, cb), lambda j: (0, j))
    return pl.pallas_call(
        body, name=name, grid=(C // cb,),
        in_specs=[big] * 4 + [vec] * 2, out_specs=[vec, vec],
        out_shape=[jax.ShapeDtypeStruct((1, C), F32)] * 2,
        compiler_params=_cparams(("parallel",)),
    )(mu_re, mu_im, h_re, h_im, h0_re, h0_im)


NT_DIMS = (((1,), (1,)), ((), ()))
TN_DIMS = (((0,), (0,)), ((), ()))


ATTN_Q_ROWS = 512


def _attn_exp(q, kvh, kr):
    s = (lax.dot_general(q[:, :LANES], kvh[:, :LANES], NT_DIMS, preferred_element_type=F32)
         + lax.dot_general(q[:, LANES:], kr, NT_DIMS, preferred_element_type=F32))
    e = jnp.exp2((s - jnp.max(s, axis=-1, keepdims=True)) * (ATTN_SCALE * math.log2(math.e)))
    return e, jnp.sum(e, axis=-1, keepdims=True)


def _attn_specs(L, T, tq):
    return [
        pl.BlockSpec((tq, 2 * LANES), lambda h, i: (i, h)),
        pl.BlockSpec((T, 2 * LANES), lambda h, i: (0, h)),
        pl.BlockSpec((T, LANES), lambda h, i: (0, 0)),
    ]


def _attn_fwd(qq, kv, kr, *, name):
    L, T = qq.shape[0], kv.shape[0]
    tq = _pick(L, ATTN_Q_ROWS // 2, 16)

    def body(q_ref, kv_ref, kr_ref, o_ref):
        kvh = kv_ref[...]
        e, l = _attn_exp(q_ref[...], kvh, kr_ref[...])
        o_ref[...] = (jnp.dot(e.astype(BF16), kvh[:, LANES:], preferred_element_type=F32) * (1.0 / l)).astype(o_ref.dtype)

    return pl.pallas_call(
        body, name=name, grid=(MLA_HEADS, L // tq), in_specs=_attn_specs(L, T, tq),
        out_specs=pl.BlockSpec((tq, LANES), lambda h, i: (i, h)),
        out_shape=jax.ShapeDtypeStruct((L, MLA_HEADS * V_DIM), BF16),
        compiler_params=_cparams(("parallel", "parallel")),
    )(qq, kv, kr)


def _attn_bwd(qq, kv, kr, do, *, name):
    L, T = qq.shape[0], kv.shape[0]
    H = MLA_HEADS
    tq = _pick(L, ATTN_Q_ROWS, 16)
    nq = L // tq

    def body(q_ref, kv_ref, kr_ref, do_ref, dq_ref, dkv_ref, dkr_ref, dkn_acc, dv_acc):
        h, i = pl.program_id(0), pl.program_id(1)
        q, kvh, krv, dov = q_ref[...], kv_ref[...], kr_ref[...], do_ref[...]
        e, l = _attn_exp(q, kvh, krv)
        inv = 1.0 / l
        ps = e * (inv * ATTN_SCALE)
        t = lax.dot_general(dov, kvh[:, LANES:], NT_DIMS, preferred_element_type=F32) * ps
        ds = (t - ps * (jnp.sum(t, axis=-1, keepdims=True) * (1.0 / ATTN_SCALE))).astype(BF16)
        dq_ref[:, :LANES] = jnp.dot(ds, kvh[:, :LANES], preferred_element_type=F32)
        dq_ref[:, LANES:] = jnp.dot(ds, krv, preferred_element_type=F32)

        @pl.when(i == 0)
        def _():
            dkn_acc[...] = jnp.zeros_like(dkn_acc)
            dv_acc[...] = jnp.zeros_like(dv_acc)

        @pl.when((i == 0) & (h == 0))
        def _():
            dkr_ref[...] = jnp.zeros_like(dkr_ref)

        dv_acc[...] += lax.dot_general(e.astype(BF16), (dov.astype(F32) * inv).astype(BF16), TN_DIMS,
                                       preferred_element_type=F32)
        dkn_acc[...] += lax.dot_general(ds, q[:, :LANES], TN_DIMS, preferred_element_type=F32)
        dkr_ref[...] += lax.dot_general(ds, q[:, LANES:], TN_DIMS, preferred_element_type=F32)

        @pl.when(i == nq - 1)
        def _():
            dkv_ref[:, :LANES] = dkn_acc[...].astype(dkv_ref.dtype)
            dkv_ref[:, LANES:] = dv_acc[...].astype(dkv_ref.dtype)

    in_specs = _attn_specs(L, T, tq) + [pl.BlockSpec((tq, LANES), lambda h, i: (i, h))]
    return pl.pallas_call(
        body, name=name, grid=(H, L // tq), in_specs=in_specs,
        out_specs=[pl.BlockSpec((tq, 2 * LANES), lambda h, i: (i, h)), pl.BlockSpec((T, 2 * LANES), lambda h, i: (0, h)),
                   pl.BlockSpec((T, LANES), lambda h, i: (0, 0))],
        out_shape=[jax.ShapeDtypeStruct((L, H * 2 * LANES), F32), jax.ShapeDtypeStruct((T, H * 2 * LANES), BF16),
                   jax.ShapeDtypeStruct((T, LANES), F32)],
        scratch_shapes=[pltpu.VMEM((T, LANES), F32), pltpu.VMEM((T, LANES), F32)],
        compiler_params=_cparams(("arbitrary", "arbitrary")),
    )(qq, kv, kr, do)


def _adamw(w, g, m, v, *, name, anchor=None):
    c1 = 1.0 - ADAM_B1 ** ADAM_STEP
    c2 = 1.0 - ADAM_B2 ** ADAM_STEP

    def f(w, g, m, v):
        m = ADAM_B1 * m + (1.0 - ADAM_B1) * g
        v = ADAM_B2 * v + (1.0 - ADAM_B2) * jnp.square(g)
        delta = -ADAM_LR * ((m / c1) / (jnp.sqrt(v / c2) + ADAM_EPS) + ADAM_WD * w)
        return g, delta, m, v

    return _rw(f, [w, g, m, v], [], [F32] * 4, name=name, anchor=anchor)


def _slab_rows(rows, cols, n_arrays):
    return _pick(rows, max(16, (8 * 1024 * 1024) // (cols * 4 * n_arrays)), 16)


def _scalars(*vals):
    return jnp.stack([jnp.asarray(v, jnp.int32) for v in vals])


def _into_slot(src, slot, nslots, dtype, *, name):
    R, C = src.shape
    tr = _slab_rows(R, C, 2)

    def body(s_ref, x_ref, o_ref):
        o_ref[...] = x_ref[...].astype(o_ref.dtype)

    return pl.pallas_call(
        body, name=name,
        grid_spec=pltpu.PrefetchScalarGridSpec(
            num_scalar_prefetch=1, grid=(R // tr,),
            in_specs=[pl.BlockSpec((tr, C), lambda i, s: (i, 0))],
            out_specs=pl.BlockSpec((None, tr, C), lambda i, s: (s[0], i, 0))),
        out_shape=jax.ShapeDtypeStruct((nslots, R, C), dtype),
        compiler_params=_cparams(("arbitrary",)),
    )(_scalars(slot), src)


def _pair_sum(g, got, c, *, name):
    _, R, C = g.shape
    hr = R // 2
    tr = _slab_rows(hr, C, 3)
    nblk = hr // tr

    def body(s_ref, g_ref, r_ref, o_ref):
        o_ref[...] = (g_ref[...].astype(F32) + r_ref[...].astype(F32)).astype(o_ref.dtype)

    return pl.pallas_call(
        body, name=name,
        grid_spec=pltpu.PrefetchScalarGridSpec(
            num_scalar_prefetch=1, grid=(4, nblk),
            in_specs=[pl.BlockSpec((None, tr, C), lambda j, i, s: (j, s[0] * nblk + i, 0)),
                      pl.BlockSpec((None, tr, C), lambda j, i, s: (j, i, 0))],
            out_specs=pl.BlockSpec((None, tr, C), lambda j, i, s: (j, i, 0))),
        out_shape=jax.ShapeDtypeStruct((4, hr, C), g.dtype),
        compiler_params=_cparams(("arbitrary", "arbitrary")),
    )(_scalars(c), g, got)


def _chip_sum(p, landed, me_chip, c, *, name):
    _, hr, C = p.shape
    tr = _slab_rows(hr, C, 5)

    def body(s_ref, p_ref, l0_ref, l1_ref, l2_ref, o_ref):
        o_ref[...] = ((p_ref[...].astype(F32) + l0_ref[...].astype(F32)) + l1_ref[...].astype(F32)) + l2_ref[...].astype(F32)

    return pl.pallas_call(
        body, name=name,
        grid_spec=pltpu.PrefetchScalarGridSpec(
            num_scalar_prefetch=1, grid=(hr // tr,),
            in_specs=[pl.BlockSpec((None, tr, C), lambda i, s: (s[0], i, 0))]
            + [pl.BlockSpec((None, tr, C), functools.partial(lambda i, s, k: (k, i, 0), k=k)) for k in range(3)],
            out_specs=pl.BlockSpec((None, tr, C), lambda i, s: (s[1], i, 0))),
        out_shape=jax.ShapeDtypeStruct((2, hr, C), F32),
        compiler_params=_cparams(("arbitrary",)),
    )(_scalars(me_chip, c), p, landed, landed, landed)


def _place():
    return lax.axis_index("x"), lax.axis_index("y"), lax.axis_index("c")


def _other_chips(x, y):
    chips = [(1 - x, y), (x, 1 - y), (1 - x, 1 - y)]
    return chips, [2 * cx + cy for cx, cy in chips]


HBM = pl.BlockSpec(memory_space=pl.ANY)


def _allgather8(v, *, name):
    rows, cols = v.shape

    def body(v_ref, out_ref, send_sems, recv_sems):
        x, y, c = _place()
        me = 4 * x + 2 * y + c
        out_ref[me] = v_ref[...]
        copies = []
        for k in range(1, 8):
            bx, by, bc = (k >> 2) & 1, (k >> 1) & 1, k & 1
            px, py, pc = x ^ bx, y ^ by, c ^ bc
            cp = pltpu.make_async_remote_copy(
                src_ref=v_ref, dst_ref=out_ref.at[me], send_sem=send_sems.at[k - 1], recv_sem=recv_sems.at[k - 1],
                device_id=(px, py, pc), device_id_type=MESH)
            cp.start()
            copies.append((cp, 4 * px + 2 * py + pc))
        for k, (cp, peer) in enumerate(copies):
            pltpu.make_async_remote_copy(
                src_ref=v_ref, dst_ref=out_ref.at[peer], send_sem=send_sems.at[k], recv_sem=recv_sems.at[k],
                device_id=(x, y, c), device_id_type=MESH).wait_recv()
        for cp, _ in copies:
            cp.wait_send()

    return pl.pallas_call(
        body, name=name, out_shape=jax.ShapeDtypeStruct((8, rows, cols), v.dtype),
        in_specs=[pl.BlockSpec(memory_space=pltpu.VMEM)], out_specs=pl.BlockSpec(memory_space=pltpu.VMEM),
        scratch_shapes=[pltpu.SemaphoreType.DMA((7,)), pltpu.SemaphoreType.DMA((7,))],
        compiler_params=pltpu.CompilerParams(vmem_limit_bytes=VMEM_LIMIT),
    )(v)


def _allgather_shards(bufs, *, name):
    n = len(bufs)

    def body(*refs):
        outs = refs[n:2 * n]
        send_sems, recv_sems = refs[2 * n:]
        x, y, c = _place()
        me_chip = 2 * x + y
        sibling = (x, y, 1 - c)
        chips, chip_ids = _other_chips(x, y)

        def remote(k, j, blk, hf, to):
            hr = bufs[k].shape[1] // 2
            piece = outs[k].at[blk, pl.ds(pl.multiple_of(hf * hr, 16), hr), :]
            return pltpu.make_async_remote_copy(
                src_ref=piece, dst_ref=piece, send_sem=send_sems.at[6 * k + j], recv_sem=recv_sems.at[6 * k + j],
                device_id=to, device_id_type=MESH)

        sends = []
        for k in range(n):
            for j, chip in enumerate(chips):
                cp = remote(k, j, me_chip, c, (*chip, c))
                cp.start()
                sends.append(cp)
        for k in range(n):
            for j, chip in enumerate(chips):
                remote(k, j, chip_ids[j], c, (x, y, c)).wait_recv()
                cp = remote(k, 3 + j, chip_ids[j], c, sibling)
                cp.start()
                sends.append(cp)
        for k in range(n):
            for j in range(3):
                remote(k, 3 + j, chip_ids[j], 1 - c, (x, y, c)).wait_recv()
        for cp in sends:
            cp.wait_send()

    return list(pl.pallas_call(
        body, name=name, out_shape=[jax.ShapeDtypeStruct(b.shape, b.dtype) for b in bufs],
        in_specs=[HBM] * n, out_specs=[HBM] * n, input_output_aliases={k: k for k in range(n)},
        scratch_shapes=[pltpu.SemaphoreType.DMA((6 * n,)), pltpu.SemaphoreType.DMA((6 * n,))],
    )(*bufs))


HBM_SPEC = pl.BlockSpec(memory_space=pltpu.HBM)
SEM_SPEC = pl.BlockSpec(memory_space=pltpu.SEMAPHORE)
EFFECT = pltpu.SideEffectType.DATAFLOW_SIDE_EFFECTING
TOKEN = jax.ShapeDtypeStruct((SUBLANES, LANES), F32)


def _in_hbm(a):
    return pltpu.with_memory_space_constraint(a, pltpu.HBM)


def _half_rows(buf, hf):
    hr = buf.shape[1] // 2
    return pl.ds(pl.multiple_of(hf * hr, 16), hr)


def _plan_ag_ici(refs):
    x, y, c = _place()
    chips, ids = _other_chips(x, y)
    out = []
    for r in refs:
        mine = r.at[2 * x + y, _half_rows(r, c), :]
        out += [(mine, mine, r.at[ids[j], _half_rows(r, c), :], (*chip, c)) for j, chip in enumerate(chips)]
    return out


def _plan_ag_pair(refs):
    x, y, c = _place()
    _, ids = _other_chips(x, y)
    out = []
    for r in refs:
        for j in range(3):
            piece = r.at[ids[j], _half_rows(r, c), :]
            out.append((piece, piece, r.at[ids[j], _half_rows(r, 1 - c), :], (x, y, 1 - c)))
    return out


def _plan_rs_ici(refs):
    x, y, c = _place()
    chips, ids = _other_chips(x, y)
    n = len(refs) // 2
    return [(refs[k].at[ids[j]], refs[n + k].at[j], refs[n + k].at[j], (*chip, c))
            for k in range(n) for j, chip in enumerate(chips)]


def _plan_pair_exchange(refs):
    x, y, c = _place()
    n = len(refs) // 2
    return [(refs[k].at[:, _half_rows(refs[k], 1 - c), :], refs[n + k], refs[n + k], (x, y, 1 - c)) for k in range(n)]


def _plan_pair_gather(refs):
    x, y, c = _place()
    return [(r.at[c], r.at[c], r.at[1 - c], (x, y, 1 - c)) for r in refs]


def _remote(src, dst, send_sem, recv_sem, target):
    return pltpu.make_async_remote_copy(src_ref=src, dst_ref=dst, send_sem=send_sem, recv_sem=recv_sem,
                                        device_id=target, device_id_type=MESH)


def _copy_start(groups, *, name, after=()):
    flat = [a for arrays, _, _ in groups for a in arrays]
    n, ng = len(flat), len(groups)
    after = list(after)
    n_in = n + len(after)

    def body(*refs):
        sems = refs[n_in:n_in + 2 * ng]
        thru = refs[n_in + 2 * ng:n_in + 2 * ng + n]
        token = refs[-1]
        pos = 0
        for g, (arrays, plan, n_copies) in enumerate(groups):
            copies = plan(thru[pos:pos + len(arrays)])
            pos += len(arrays)
            assert len(copies) == n_copies
            for i, (src, dst, _, target) in enumerate(copies):
                _remote(src, dst, sems[2 * g].at[i], sems[2 * g + 1].at[i], target).start()
        token[...] = jnp.zeros_like(token)

    out_shape = tuple(pltpu.SemaphoreType.DMA((n_copies,)) for _, _, n_copies in groups for _ in range(2))
    out_shape += tuple(pltpu.HBM(a.shape, a.dtype) for a in flat) + (TOKEN,)
    res = pl.pallas_call(
        body, name=name, out_shape=out_shape,
        in_specs=(HBM_SPEC,) * n + (pl.BlockSpec(memory_space=pl.ANY),) * len(after),
        out_specs=(SEM_SPEC,) * (2 * ng) + (HBM_SPEC,) * n + (pl.BlockSpec(memory_space=pltpu.VMEM),),
        input_output_aliases={k: 2 * ng + k for k in range(n)},
        compiler_params=pltpu.CompilerParams(has_side_effects=EFFECT),
    )(*[_in_hbm(a) for a in flat], *after)
    sems = [(res[2 * g], res[2 * g + 1]) for g in range(ng)]
    thru, pos = [], 2 * ng
    for arrays, _, _ in groups:
        thru.append(list(res[pos:pos + len(arrays)]))
        pos += len(arrays)
    return sems, thru, res[-1]


def _copy_wait(arrays, sems, plan, n_copies, after, *, name):
    n = len(arrays)
    after = list(after)

    def body(*refs):
        send, recv = refs[n], refs[n + 1]
        x, y, c = _place()
        copies = plan(refs[:n])
        assert len(copies) == n_copies
        for i, (src, dst, landing, target) in enumerate(copies):
            _remote(src, dst, send.at[i], recv.at[i], target).wait_send()
            _remote(landing, landing, send.at[i], recv.at[i], (x, y, c)).wait_recv()

    return list(pl.pallas_call(
        body, name=name, out_shape=tuple(pltpu.HBM(a.shape, a.dtype) for a in arrays),
        in_specs=(HBM_SPEC,) * n + (SEM_SPEC, SEM_SPEC) + (pl.BlockSpec(memory_space=pl.ANY),) * len(after),
        out_specs=(HBM_SPEC,) * n, input_output_aliases={k: k for k in range(n)},
        compiler_params=pltpu.CompilerParams(has_side_effects=EFFECT),
    )(*arrays, *sems, *after))


def _rs_stage1(gs, tag, after=()):
    n = len(gs)
    lands = [lax.empty((4, g.shape[1] // 2, g.shape[2]), g.dtype) for g in gs]
    sems, (arrays,), token = _copy_start([(list(gs) + lands, _plan_pair_exchange, n)], name=f"rs_pair_start_{tag}",
                                         after=after)
    return (sems[0], arrays), token


def _rs_stage2(handle, after, tag):
    sems, arrays = handle
    n = len(arrays) // 2
    arrays = _copy_wait(arrays, sems, _plan_pair_exchange, n, after, name=f"rs_pair_wait_{tag}")
    c = lax.axis_index("c")
    pair = [_pair_sum(g, r, c, name=f"rs_pair_sum_{tag}{k}") for k, (g, r) in enumerate(zip(arrays[:n], arrays[n:]))]
    lands = [lax.empty((3,) + p.shape[1:], p.dtype) for p in pair]
    sems, (arrays,), token = _copy_start([(pair + lands, _plan_rs_ici, 3 * n)], name=f"rs_start_{tag}")
    return (sems[0], arrays), token


def _rs_stage3(handle, after, tag):
    sems, arrays = handle
    n = len(arrays) // 2
    arrays = _copy_wait(arrays, sems, _plan_rs_ici, 3 * n, after, name=f"rs_wait_{tag}")
    x, y, c = _place()
    halves = [_chip_sum(p, l, 2 * x + y, c, name=f"rs_chip_sum_{tag}{k}") for k, (p, l) in enumerate(zip(arrays[:n], arrays[n:]))]
    sems, (halves,), token = _copy_start([(halves, _plan_pair_gather, n)], name=f"rs_gather_start_{tag}")
    return (sems[0], halves), token


def _rs_stage4(handle, after, tag):
    sems, halves = handle
    full = _copy_wait(halves, sems, _plan_pair_gather, len(halves), after, name=f"rs_gather_wait_{tag}")
    return [f.reshape(2 * f.shape[1], f.shape[2]) for f in full]


def _to_segments(a):
    rows = a.shape[0]
    return a.reshape(N_SEG, rows // N_SEG, -1).transpose(1, 0, 2).reshape(rows, -1)


def _from_segments(a):
    rows = a.shape[0]
    return a.reshape(rows // N_SEG, N_SEG, -1).transpose(1, 0, 2).reshape(rows, -1)


def _rope_tables(L):
    t = jnp.arange(L, dtype=jnp.int32)
    row = (t // GRID_W).astype(F32)
    col = (t % GRID_W).astype(F32)
    n_freq = QK_ROPE // 4
    inv = ROPE_BASE ** (-jnp.arange(n_freq, dtype=F32) / n_freq)
    a0, a1 = row[:, None] * inv, col[:, None] * inv
    z = jnp.zeros((L, LANES - QK_ROPE), F32)
    cos = jnp.concatenate([jnp.cos(a0), jnp.cos(a0), jnp.cos(a1), jnp.cos(a1), z], axis=1)
    sin = jnp.concatenate([-jnp.sin(a0), jnp.sin(a0), -jnp.sin(a1), jnp.sin(a1), z], axis=1)
    return _to_segments(cos), _to_segments(sin)


def _col_blocks(w, nblk):
    r, c = w.shape
    return w.reshape(r, nblk, c // nblk).transpose(1, 0, 2)


def _from_col_blocks(w4):
    nblk, r, c = w4.shape
    return w4.transpose(1, 0, 2).reshape(r, nblk * c)


def _s5_discretize(a_re, a_im, log_dt, b_re, b_im):
    dt = jnp.exp(log_dt)[:, None]
    mag = jnp.exp(a_re * dt)
    ab_re, ab_im = mag * jnp.cos(a_im * dt), mag * jnp.sin(a_im * dt)
    den = a_re * a_re + a_im * a_im
    nr, ni = ab_re - 1.0, ab_im
    co_re = (nr * a_re + ni * a_im) / den
    co_im = (ni * a_re - nr * a_im) / den
    bb_re = co_re[..., None] * b_re - co_im[..., None] * b_im
    bb_im = co_re[..., None] * b_im + co_im[..., None] * b_re
    return ab_re, ab_im, bb_re, bb_im


def _diag_blocks_in(bb, gpb):
    G, N, P = bb.shape
    t = jnp.tile(jnp.swapaxes(bb, 1, 2).reshape(G // gpb, gpb * P, N), (1, 1, gpb))
    row = lax.broadcasted_iota(jnp.int32, t.shape, 1) // P
    col = lax.broadcasted_iota(jnp.int32, t.shape, 2) // N
    return jnp.where(row == col, t, 0.0)


def _diag_blocks_out(cc, gpb):
    G, P, N = cc.shape
    t = jnp.tile(jnp.swapaxes(cc, 1, 2).reshape(G // gpb, gpb * N, P), (1, 1, gpb))
    row = lax.broadcasted_iota(jnp.int32, t.shape, 1) // N
    col = lax.broadcasted_iota(jnp.int32, t.shape, 2) // P
    return jnp.where(row == col, t, 0.0)


def _tr(ws):
    return [jnp.swapaxes(w, 1, 2) for w in ws]


WEIGHTS = ['c_ctx', 'w_mod', 'b_mod', 'norm1', 'norm2', 'w_in', 's5_a_re', 's5_a_im', 's5_log_dt', 's5_b_re', 's5_b_im',
           's5_c_re', 's5_c_im', 's5_d', 'w_glu', 'q_norm', 'kv_norm', 'w_uq', 'w_ukv', 'w_mla_o', 'w_out', 'w_ffn_in',
           'w_ffn_out', 'norm_f']
AG_GROUPS = [['w_in'], ['w_glu', 'w_uq', 'w_ukv', 'w_mla_o', 'w_out'], ['w_ffn_in', 'w_ffn_out']]
SMALL = ['norm1', 'norm2', 's5_a_re', 's5_a_im', 's5_log_dt', 's5_b_re', 's5_b_im', 's5_c_re', 's5_c_im', 's5_d',
         'q_norm', 'kv_norm', 'norm_f']


def _pad_rows(a, rows):
    return jnp.concatenate([a, jnp.zeros((rows - a.shape[0],) + a.shape[1:], a.dtype)], axis=0)


def _pack(vals, width, rows):
    flat = jnp.concatenate([v.reshape(-1).astype(F32) for v in vals])
    flat = jnp.concatenate([flat, jnp.zeros((rows * width - flat.shape[0],), F32)])
    return flat.reshape(rows, width)


def _unpack(buf, like):
    flat = buf.reshape(-1)
    out, pos = [], 0
    for v in like:
        out.append(flat[pos:pos + v.size].reshape(v.shape))
        pos += v.size
    return out


def _step(x, c, ctx, loss_target, w, m, v):
    px, py, pc = _place()
    me = 4 * px + 2 * py + pc
    me_chip = 2 * px + py
    L, D = x.shape[1], x.shape[2]
    Lc = ctx.shape[1]
    T = L + Lc
    SW = D // 2
    G = SW // S5_GROUP
    C = G * S5_STATE
    H = MLA_HEADS
    q_rank = w['q_norm'].shape[1]
    kv_rank = w['kv_norm'].shape[1]
    d_ff = w['w_ffn_out'].shape[1] * 4
    wa_used = SW + q_rank + kv_rank + QK_ROPE
    WA = -(-(SW + q_rank + kv_rank + LANES) // 512) * 512

    c_rows = _pad_rows(c.astype(F32), SUBLANES)
    c_all = _allgather8(c_rows, name="ag_cond")[:, 0, :]
    cond = jnp.concatenate([c_all, w['c_ctx'].reshape(1, D)], axis=0)
    cond = _pad_rows(cond, 16)
    (act,) = _rw(lambda t: (jax.nn.silu(t),), [cond], [], [F32], name="cond_silu")
    w_mod, cs_mod = w['w_mod'][0], w['w_mod'].shape[2]
    mod_part = _mm(act, w_mod, out_dtype=F32, name="mod_fwd")
    mod_all = _allgather8(mod_part, name="ag_mod")
    mod_full = jnp.concatenate([mod_all[0], mod_all[2], mod_all[4], mod_all[6]], axis=1) + w['b_mod']
    m_lat = lax.dynamic_slice_in_dim(mod_full, me, 1, axis=0).reshape(6, D)
    m_ctx = mod_full[8].reshape(6, D)
    sh1, sc1, g1, sh2, sc2, g2 = (m_lat[i:i + 1] for i in range(6))
    csh1, csc1 = m_ctx[0:1], m_ctx[1:2]

    ag_groups = [([_into_slot(w[nme][0], me_chip, 4, BF16, name=f"cast_{nme}") for nme in grp], _plan_ag_ici, 3 * len(grp))
                 for grp in AG_GROUPS]
    ag_sems, ag_bufs, ag_token = _copy_start(ag_groups, name="ag_start", after=[mod_full])
    gathered, ag_pair = {}, {}

    def landed(g, after):
        n_cp = 3 * len(AG_GROUPS[g])
        got = _copy_wait(ag_bufs[g], ag_sems[g], _plan_ag_ici, n_cp, after, name=f"ag_wait_{g}")
        sems, (got,), token = _copy_start([(got, _plan_ag_pair, n_cp)], name=f"ag_pair_start_{g}")
        ag_pair[g] = (sems[0], got)
        return token[0, 0]

    def arrive(g, after):
        sems, got = ag_pair[g]
        got = _copy_wait(got, sems, _plan_ag_pair, 3 * len(AG_GROUPS[g]), after, name=f"ag_pair_wait_{g}")
        gathered.update(zip(AG_GROUPS[g], got))

    xs = _to_segments(x[0])
    cs = _to_segments(ctx[0])
    tgt = _to_segments(loss_target[0])
    cos, sin = _rope_tables(L)
    n1, n2, nf = w['norm1'], w['norm2'], w['norm_f'].reshape(1, D)
    qg, kvg = w['q_norm'], w['kv_norm']

    (xn_lat,) = _rw(_f_norm_mod, [xs], [n1 + ag_token[0, 0], sc1, sh1], [BF16], name="norm1_lat")
    (xn_ctx,) = _rw(_f_norm_mod, [cs], [n1, csc1, csh1], [BF16], name="norm1_ctx")
    xn = jnp.concatenate([xn_lat, xn_ctx], axis=0)
    landed(0, [xn])

    gpb = min(S5_BLOCK_GROUPS, G)
    gpo = min(8, G)
    d_skip = w['s5_d'][0].reshape(1, SW)
    disc, vjp_disc, w_b, w_c = [], [], [], []
    for d in range(2):
        prm = (w['s5_a_re'][0, d], w['s5_a_im'][0, d], w['s5_log_dt'][0, d], w['s5_b_re'][0, d], w['s5_b_im'][0, d])

        def prep(a_re, a_im, log_dt, b_re, b_im):
            ab_re, ab_im, bb_re, bb_im = _s5_discretize(a_re, a_im, log_dt, b_re, b_im)
            return ab_re.reshape(1, C), ab_im.reshape(1, C), _diag_blocks_in(bb_re, gpb), _diag_blocks_in(bb_im, gpb)

        out, vj = jax.vjp(prep, *prm)
        disc.append(out)
        vjp_disc.append(vj)
        w_b += [out[2], out[3]]
        w_c += [_diag_blocks_out(w['s5_c_re'][0, d], gpo), -_diag_blocks_out(w['s5_c_im'][0, d], gpo)]
    nb_in = G // gpb
    nb_out = G // gpo

    arrive(0, [xn, tgt] + w_b + w_c)
    w_in = _from_col_blocks(gathered['w_in'])
    w_a = jnp.concatenate([w_in[:, :wa_used], jnp.zeros((D, WA - wa_used), BF16)], axis=1)
    w_g = w_in[:, wa_used:]
    ha = _mm(xn, w_a, out_dtype=F32, name="in_proj")
    ha_lat, ha_ctx = ha[:L], ha[L:]
    gt = _mm(xn_lat, w_g, out_dtype=F32, name="in_gates")
    f_post_lat = _make_f_post_in(SW, q_rank, kv_rank, True)
    f_post_ctx = _make_f_post_in(SW, q_rank, kv_rank, False)
    u_lat, cqn, ckvn_lat, kr_lat = _rw(f_post_lat, [ha_lat, cos, sin], [qg, kvg], [F32, BF16, BF16, BF16], name="post_in_lat")
    u_ctx, ckvn_ctx, kr_ctx = _rw(f_post_ctx, [ha_ctx], [kvg], [F32, BF16, BF16], name="post_in_ctx")
    zero = jnp.zeros((1, C), F32) + landed(1, [u_lat, u_ctx])

    h_lat, h_ctx, hT_ctx = [], [], []
    for d, rev in enumerate((False, True)):
        lr, li = disc[d][0], disc[d][1]
        hcr, hci, tr, ti = _s5_scan(u_ctx, w_b[2 * d], w_b[2 * d + 1], lr, li, zero, zero, zero, zero, reverse=rev,
                                    name=f"s5_scan_ctx_{d}")
        hlr, hli, _, _ = _s5_scan(u_lat, w_b[2 * d], w_b[2 * d + 1], lr, li, tr, ti, zero, zero, reverse=rev,
                                  name=f"s5_scan_lat_{d}")
        h_ctx += [hcr, hci]
        h_lat += [hlr, hli]
        hT_ctx += [tr, ti]
    r5 = _bd_fanin(h_lat, w_c, name="s5_readout")
    (z,) = _rw(_f_s5post, [u_lat, r5], [d_skip], [BF16], name="s5_post")

    arrive(1, [z])
    w_glu, w_ukv, w_mla_o = (gathered[nme] for nme in ('w_glu', 'w_ukv', 'w_mla_o'))
    w_out = gathered['w_out'].reshape(D, D)
    uq3 = _from_col_blocks(gathered['w_uq']).reshape(q_rank, H, QK_NOPE + QK_ROPE)
    w_q2 = jnp.concatenate([uq3, jnp.zeros((q_rank, H, LANES - QK_ROPE), BF16)], axis=2).reshape(q_rank, H * 2 * LANES)
    q2 = _mm(cqn, w_q2, out_dtype=F32, name="q_up")
    (qq,) = _rw(_f_qpost, [q2, cos, sin], [], [BF16], name="q_rope")
    kvn = jnp.concatenate([ckvn_lat, ckvn_ctx], axis=0)
    kr_all = jnp.concatenate([kr_lat, kr_ctx], axis=0)
    kv = _mm(kvn, w_ukv, b_shards=4, out_dtype=BF16, name="kv_up")
    kr_all = kr_all + landed(2, [kv, qq]).astype(BF16)
    o = _attn_fwd(qq, kv, kr_all, name="attn_fwd")

    ab = _mm(z, w_glu, b_shards=4, out_dtype=F32, name="glu_proj")
    bm = _mm(o, w_mla_o, b_shards=4, out_dtype=F32, name="mla_out")
    (mix,) = _rw(_f_merge, [ab, bm, gt], [], [BF16], name="merge")
    out1 = _mm(mix, w_out, out_dtype=F32, name="out_proj")
    x1, xn2 = _rw(_f_resid_norm, [xs, out1], [g1, n2, sc2, sh2], [F32, BF16], name="resid_norm2")
    arrive(2, [xn2])
    w_ffn_in = gathered['w_ffn_in']
    w_ffn_out = gathered['w_ffn_out'].reshape(d_ff, D)
    hmid, ab2 = _ffn_in_swiglu(xn2, w_ffn_in, name="ffn_in")
    f2 = _mm(hmid, w_ffn_out, out_dtype=F32, name="ffn_out")
    (row_loss,) = _rw(_f_final, [x1, f2, tgt], [g2, nf], [F32], name="final_loss")
    loss = lax.psum(jnp.sum(row_loss), ("x", "y", "c"))

    ones = jnp.ones((L, 1), F32)
    (dx1_a, df2), (dg2, dnf) = _rw_vjp(_f_final, [x1, f2, tgt], [g2, nf], [[ones]], [True, True, False], [True, True],
                                       [F32, BF16], name="final_loss_bwd")
    gw_ffn_out = _mm(hmid, df2, ta=True, out_dtype=BF16, name="ffn_out_dw")
    dab2 = _ffn_out_dx_swiglu(df2, w_ffn_out, ab2, name="ffn_out_dx")
    dxn2 = _mm(dab2, w_ffn_in, tb=True, a_shards=2, b_shards=4, out_dtype=F32, name="ffn_in_dx")
    gw_ffn_in = _mm(xn2, dab2, ta=True, b_shards=2, out_shards=4, out_dtype=BF16, name="ffn_in_dw")
    (dx_a, dout1), (dg1, dn2, dsc2, dsh2) = _rw_vjp(
        _f_resid_norm, [xs, out1], [g1, n2, sc2, sh2], [[dx1_a], [dxn2]], [True, True], [True] * 4, [F32, BF16],
        name="resid_norm2_bwd")
    dmix = _mm(dout1, w_out, tb=True, out_dtype=F32, name="out_proj_dx")
    gw_out = _mm(mix, dout1, ta=True, out_dtype=BF16, name="out_proj_dw")
    (dab, dbm, dgt), _ = _rw_vjp(_f_merge, [ab, bm, gt], [], [[dmix]], [True] * 3, [], [BF16] * 3, name="merge_bwd")
    dz = _mm(dab, w_glu, tb=True, b_shards=4, out_dtype=F32, name="glu_proj_dx")
    gw_glu = _mm(z, dab, ta=True, out_shards=4, out_dtype=BF16, name="glu_proj_dw")
    do = _mm(dbm, w_mla_o, tb=True, b_shards=4, out_dtype=BF16, name="mla_out_dx")
    gw_mla_o = _mm(o, dbm, ta=True, out_shards=4, out_dtype=BF16, name="mla_out_dw")
    dxn_g = _mm(dgt, w_g, tb=True, out_dtype=F32, name="in_gates_dx")
    gw_g = _mm(xn_lat, dgt, ta=True, out_dtype=BF16, name="in_gates_dw")
    rs_big, tok = _rs_stage1([gw_ffn_out.reshape(4, -1, D), gw_ffn_in, gw_out.reshape(4, -1, D), gw_glu, gw_mla_o], "big",
                             after=[gw_g])

    (du_a, dr5), (dd_skip,) = _rw_vjp(_f_s5post, [u_lat, r5], [d_skip + tok[0, 0]], [[dz]], [True, True], [True], [F32, F32],
                                      name="s5_post_bwd")
    dw_c = _bd_dw(h_lat, [dr5] * 4, nb_out, name="s5_readout_dw")
    rs_big, tok = _rs_stage2(rs_big, dw_c[:1], "big")
    zero = zero + tok[0, 0]
    w_ct = _tr(w_c)
    zeros_ctx = jnp.zeros((Lc, SW), BF16)
    mu_lat, mu_ctx, dlam = [], [], []
    for d, rev in enumerate((False, True)):
        lr, li = disc[d][0], disc[d][1]
        mlr, mli, fr, fi = _s5_scan(dr5, w_ct[2 * d], w_ct[2 * d + 1], lr, -li, zero, zero, zero, zero, reverse=not rev,
                                    name=f"s5_adj_lat_{d}")
        dh0r, dh0i = _cmul(lr, -li, fr, fi)
        mcr, mci, _, _ = _s5_scan(zeros_ctx, w_ct[2 * d], w_ct[2 * d + 1], lr, -li, zero, zero, dh0r, dh0i,
                                  reverse=not rev, name=f"s5_adj_ctx_{d}")
        dl_lat = _s5_dlam(mlr, mli, h_lat[2 * d], h_lat[2 * d + 1], hT_ctx[2 * d], hT_ctx[2 * d + 1], reverse=rev,
                          name=f"s5_dlam_lat_{d}")
        dl_ctx = _s5_dlam(mcr, mci, h_ctx[2 * d], h_ctx[2 * d + 1], zero, zero, reverse=rev, name=f"s5_dlam_ctx_{d}")
        mu_lat += [mlr, mli]
        mu_ctx += [mcr, mci]
        dlam.append((dl_lat[0] + dl_ctx[0], dl_lat[1] + dl_ctx[1]))
    du_b = _bd_fanin(mu_lat, _tr(w_b), name="s5_bu_lat_dx")
    du_ctx = _bd_fanin(mu_ctx, _tr(w_b), name="s5_bu_ctx_dx")
    dw_b_lat = _bd_dw([u_lat] * 4, mu_lat, nb_in, name="s5_bu_lat_dw")
    dw_b_ctx = _bd_dw([u_ctx] * 4, mu_ctx, nb_in, name="s5_bu_ctx_dw")
    g_s5 = {}
    for d in range(2):
        ct = (dlam[d][0], dlam[d][1], dw_b_lat[2 * d] + dw_b_ctx[2 * d], dw_b_lat[2 * d + 1] + dw_b_ctx[2 * d + 1])
        ga_re, ga_im, gdt, gb_re, gb_im = vjp_disc[d](ct)
        _, vj_c = jax.vjp(lambda cr, ci: (_diag_blocks_out(cr, gpo), -_diag_blocks_out(ci, gpo)),
                          w['s5_c_re'][0, d], w['s5_c_im'][0, d])
        gc_re, gc_im = vj_c((dw_c[2 * d], dw_c[2 * d + 1]))
        for nme, val in (('s5_a_re', ga_re), ('s5_a_im', ga_im), ('s5_log_dt', gdt), ('s5_b_re', gb_re),
                         ('s5_b_im', gb_im), ('s5_c_re', gc_re), ('s5_c_im', gc_im)):
            g_s5.setdefault(nme, []).append(val)
    g_small = {nme: jnp.stack(vals)[None] for nme, vals in g_s5.items()}
    g_small['s5_d'] = dd_skip.reshape(w['s5_d'].shape)

    dqq, dkv, dkr = _attn_bwd(qq, kv, kr_all, do, name="attn_bwd")
    (dq2,), _ = _rw_vjp(_f_qpost, [q2, cos, sin], [], [[dqq]], [True, False, False], [], [BF16], name="q_rope_bwd")
    dcqn = _mm(dq2, w_q2, tb=True, out_dtype=F32, name="q_up_dx")
    gw_q2 = _mm(cqn, dq2, ta=True, out_dtype=BF16, name="q_up_dw")
    dckvn = _mm(dkv, w_ukv, tb=True, b_shards=4, out_dtype=F32, name="kv_up_dx")
    gw_ukv = _mm(kvn, dkv, ta=True, out_shards=4, out_dtype=BF16, name="kv_up_dw")
    gw_uq = gw_q2.reshape(q_rank, H, 2 * LANES)[:, :, :QK_NOPE + QK_ROPE].reshape(q_rank, H * (QK_NOPE + QK_ROPE))
    rs_kv, tok = _rs_stage1([_col_blocks(gw_uq, 4), gw_ukv], "kv")

    (dha_lat,), (dqg, dkvg_lat) = _rw_vjp(
        f_post_lat, [ha_lat, cos, sin], [qg, kvg + tok[0, 0]], [[du_a, du_b], [dcqn], [dckvn[:L]], [dkr[:L]]],
        [True, False, False], [True, True], [BF16], name="post_in_lat_bwd")
    (dha_ctx,), (dkvg_ctx,) = _rw_vjp(f_post_ctx, [ha_ctx], [kvg], [[du_ctx], [dckvn[L:]], [dkr[L:]]], [True], [True],
                                      [BF16], name="post_in_ctx_bwd")
    dha = jnp.concatenate([dha_lat, dha_ctx], axis=0)
    dxn = _mm(dha, w_a, tb=True, out_dtype=F32, name="in_proj_dx")
    gw_a = _mm(xn, dha, ta=True, out_dtype=BF16, name="in_proj_dw")
    rs_kv, tok = _rs_stage2(rs_kv, [gw_a], "kv")
    (dx_seg,), (dn1_lat, dsc1, dsh1) = _rw_vjp(
        _f_norm_mod_keep, [xs], [n1 + tok[0, 0], sc1, sh1], [[dxn[:L], dxn_g], [dx_a]], [True], [True] * 3, [F32],
        name="norm1_lat_bwd")
    _, (dn1_ctx, dcsc1, dcsh1) = _rw_vjp(_f_norm_mod, [cs], [n1, csc1, csh1], [[dxn[L:]]], [False], [True] * 3, [],
                                         name="norm1_ctx_bwd")
    grad_x = _from_segments(dx_seg)[None]
    g_small.update(norm1=dn1_lat + dn1_ctx, norm2=dn2, q_norm=dqg, kv_norm=dkvg_lat + dkvg_ctx, norm_f=dnf.reshape(D))
    gw_in = jnp.concatenate([gw_a[:, :wa_used], gw_g], axis=1)
    small_vals = [g_small[nme] for nme in SMALL]
    n_small = sum(val.size for val in small_vals)
    small_rows = -(-n_small // (LANES * 4 * 32)) * 32

    zD = jnp.zeros((1, D), F32)
    dm = jnp.concatenate([
        jnp.concatenate([dsh1, dsc1, dg1, dsh2, dsc2, dg2], axis=1),
        jnp.concatenate([dcsh1, dcsc1, zD, zD, zD, zD], axis=1),
    ], axis=0)
    dm_all = _allgather8(_pad_rows(dm, SUBLANES), name="ag_dmod")
    rs_in, _ = _rs_stage1([_col_blocks(gw_in, 4), _pack(small_vals, LANES, 4 * small_rows).reshape(4, small_rows, LANES)],
                          "in", after=[dm_all])
    dm_ctx = dm_all[0, 1]
    for k in range(1, 8):
        dm_ctx = dm_ctx + dm_all[k, 1]
    dmod = _pad_rows(jnp.concatenate([dm_all[:, 0, :], dm_ctx[None]], axis=0), 16)
    g_b_mod = jnp.sum(dmod, axis=0, keepdims=True)
    dmod_mine = lax.dynamic_slice_in_dim(dmod, me_chip * cs_mod, cs_mod, axis=1)
    g_w_mod = _mm(act, dmod_mine, ta=True, out_dtype=F32, name="mod_dw")
    dact_part = _mm(dmod_mine, w_mod, tb=True, out_dtype=F32, name="mod_dx")
    dact_all = _allgather8(dact_part, name="ag_dact")
    dact = dact_all[0] + dact_all[2] + dact_all[4] + dact_all[6]
    (dcond_rows,), _ = _rw_vjp(lambda t: (jax.nn.silu(t),), [cond], [], [[dact]], [True], [], [F32], name="cond_silu_bwd")
    g_c_ctx = dcond_rows[8]

    rs_in, tok = _rs_stage2(rs_in, [g_c_ctx], "in")

    grads, delta, new_m, new_v = {}, {}, {}, {}

    def update(members, reds, anchor):
        deltas = []
        for nme, red in zip(members, reds):
            res = _adamw(w[nme][0], red, m[nme][0], v[nme][0], name=f"adamw_{nme}", anchor=anchor)
            grads[nme], delta[nme], new_m[nme], new_v[nme] = (r.reshape(w[nme].shape) for r in res)
            deltas.append(res[1])
            anchor = None
        return deltas

    rs_big, tok = _rs_stage3(rs_big, [tok], "big")
    done = update(['w_mod'], [g_w_mod], tok)
    red_big = _rs_stage4(rs_big, done, "big")
    rs_kv, tok = _rs_stage3(rs_kv, red_big[:1], "kv")
    done = update(['w_ffn_out', 'w_ffn_in', 'w_out', 'w_glu', 'w_mla_o'], red_big, tok)
    red_kv = _rs_stage4(rs_kv, done, "kv")
    rs_in, tok = _rs_stage3(rs_in, red_kv[:1], "in")
    done = update(['w_uq', 'w_ukv'], red_kv, tok)
    red_in = _rs_stage4(rs_in, done, "in")
    update(['w_in'], red_in[:1], None)
    small_mine = red_in[-1]
    small_buf = _into_slot(small_mine, me_chip, 4, F32, name="small_grads_slot")
    small_all = _allgather_shards([small_buf], name="ag_small_grads")[0].reshape(4 * small_rows, LANES)
    g_small_red = dict(zip(SMALL, _unpack(small_all, [w[nme] for nme in SMALL])))
    rest = SMALL + ['c_ctx', 'b_mod']
    g_rest = dict(g_small_red, c_ctx=g_c_ctx, b_mod=g_b_mod)
    rows_rest = -(-sum(w[nme].size for nme in rest) // (LANES * 16)) * 16
    packed = [_pack([src[nme] for nme in rest], LANES, rows_rest) for src in (w, g_rest, m, v)]
    res = _adamw(*packed, name="adamw_small")
    for dst, buf in zip((grads, delta, new_m, new_v), res):
        dst.update(zip(rest, _unpack(buf, [w[nme] for nme in rest])))
    return (loss, grad_x, *[grads[nme] for nme in WEIGHTS], *[delta[nme] for nme in WEIGHTS],
            *[new_m[nme] for nme in WEIGHTS], *[new_v[nme] for nme in WEIGHTS])


def kernel(x, c, ctx, c_ctx, w_mod, b_mod, norm1, norm2, w_in, s5_a_re, s5_a_im, s5_log_dt, s5_b_re, s5_b_im, s5_c_re, s5_c_im, s5_d, w_glu, q_norm, kv_norm, w_uq, w_ukv, w_mla_o, w_out, w_ffn_in, w_ffn_out, norm_f, loss_target, m_c_ctx, m_w_mod, m_b_mod, m_norm1, m_norm2, m_w_in, m_s5_a_re, m_s5_a_im, m_s5_log_dt, m_s5_b_re, m_s5_b_im, m_s5_c_re, m_s5_c_im, m_s5_d, m_w_glu, m_q_norm, m_kv_norm, m_w_uq, m_w_ukv, m_w_mla_o, m_w_out, m_w_ffn_in, m_w_ffn_out, m_norm_f, v_c_ctx, v_w_mod, v_b_mod, v_norm1, v_norm2, v_w_in, v_s5_a_re, v_s5_a_im, v_s5_log_dt, v_s5_b_re, v_s5_b_im, v_s5_c_re, v_s5_c_im, v_s5_d, v_w_glu, v_q_norm, v_kv_norm, v_w_uq, v_w_ukv, v_w_mla_o, v_w_out, v_w_ffn_in, v_w_ffn_out, v_norm_f):
    w = dict(c_ctx=c_ctx, w_mod=w_mod, b_mod=b_mod, norm1=norm1, norm2=norm2, w_in=w_in, s5_a_re=s5_a_re, s5_a_im=s5_a_im,
             s5_log_dt=s5_log_dt, s5_b_re=s5_b_re, s5_b_im=s5_b_im, s5_c_re=s5_c_re, s5_c_im=s5_c_im, s5_d=s5_d, w_glu=w_glu,
             q_norm=q_norm, kv_norm=kv_norm, w_uq=w_uq, w_ukv=w_ukv, w_mla_o=w_mla_o, w_out=w_out, w_ffn_in=w_ffn_in,
             w_ffn_out=w_ffn_out, norm_f=norm_f)
    m = dict(c_ctx=m_c_ctx, w_mod=m_w_mod, b_mod=m_b_mod, norm1=m_norm1, norm2=m_norm2, w_in=m_w_in, s5_a_re=m_s5_a_re,
             s5_a_im=m_s5_a_im, s5_log_dt=m_s5_log_dt, s5_b_re=m_s5_b_re, s5_b_im=m_s5_b_im, s5_c_re=m_s5_c_re,
             s5_c_im=m_s5_c_im, s5_d=m_s5_d, w_glu=m_w_glu, q_norm=m_q_norm, kv_norm=m_kv_norm, w_uq=m_w_uq, w_ukv=m_w_ukv,
             w_mla_o=m_w_mla_o, w_out=m_w_out, w_ffn_in=m_w_ffn_in, w_ffn_out=m_w_ffn_out, norm_f=m_norm_f)
    v = dict(c_ctx=v_c_ctx, w_mod=v_w_mod, b_mod=v_b_mod, norm1=v_norm1, norm2=v_norm2, w_in=v_w_in, s5_a_re=v_s5_a_re,
             s5_a_im=v_s5_a_im, s5_log_dt=v_s5_log_dt, s5_b_re=v_s5_b_re, s5_b_im=v_s5_b_im, s5_c_re=v_s5_c_re,
             s5_c_im=v_s5_c_im, s5_d=v_s5_d, w_glu=v_w_glu, q_norm=v_q_norm, kv_norm=v_kv_norm, w_uq=v_w_uq, w_ukv=v_w_ukv,
             w_mla_o=v_w_mla_o, w_out=v_w_out, w_ffn_in=v_w_ffn_in, w_ffn_out=v_w_ffn_out, norm_f=v_norm_f)
    return _step(x, c, ctx, loss_target, w, m, v)
```

```python
import functools
import math

import jax
import jax.numpy as jnp
from jax import lax
from jax.experimental import pallas as pl
from jax.experimental.pallas import tpu as pltpu

F32 = jnp.float32
BF16 = jnp.bfloat16

EPS = 1e-6
GRID_W = 64
S5_GROUP = 16
S5_STATE = 64
MLA_HEADS = 8
QK_NOPE = 128
QK_ROPE = 64
V_DIM = 128
ROPE_BASE = 10000.0
ATTN_SCALE = (QK_NOPE + QK_ROPE) ** -0.5
ADAM_LR = 0.001
ADAM_B1 = 0.9
ADAM_B2 = 0.999
ADAM_EPS = 1e-08
ADAM_WD = 0.01
ADAM_STEP = 10

SUBLANES = 8
LANES = 128
V7X_VMEM_BYTES = 64 * 1024 * 1024
VMEM_LIMIT = (V7X_VMEM_BYTES * 7) // 8
N_SEG = 2 * SUBLANES
S5_BLOCK_GROUPS = 8
MESH = pl.DeviceIdType.MESH


def _pick(n, target, mult):
    best = None
    d = mult
    while d <= min(n, target):
        if n % d == 0:
            best = d
        d += mult
    return n if best is None else best


def _cparams(sem=None):
    return pltpu.CompilerParams(dimension_semantics=sem, vmem_limit_bytes=VMEM_LIMIT)


MM_VMEM_BUDGET = (V7X_VMEM_BYTES * 5) // 8


def _mm(a, b, *, ta=False, tb=False, out_dtype=F32, name, a_shards=1, b_shards=1, out_shards=1):
    if ta:
        K, M = a.shape
    else:
        M, K = a.shape[-2], a.shape[-1] * a_shards
    if tb:
        N, K2 = b.shape[-2], b.shape[-1] * b_shards
    else:
        K2, N = b.shape[-2], b.shape[-1] * b_shards
    assert K == K2, (a.shape, b.shape, ta, tb)
    n_unit = N // max(out_shards, 1 if tb else b_shards)
    k_unit = K // max(a_shards, b_shards if tb else 1)
    tn = _pick(n_unit, 1024, LANES)
    tm = _pick(M, 1024 if tn >= 512 else 2048, LANES if ta else 16)
    sa, sb, so = a.dtype.itemsize, b.dtype.itemsize, jnp.dtype(out_dtype).itemsize
    k_mult = LANES if (not ta or tb) else 16
    tk = k_mult if k_unit % k_mult == 0 else k_unit
    for cand in range(k_mult, k_unit + 1, k_mult):
        if k_unit % cand == 0 and 2 * cand * (tm * sa + tn * sb) + tm * tn * (4 + 2 * so) <= MM_VMEM_BUDGET:
            tk = cand
    nk = K // tk
    dims = (((0 if ta else 1,), (1 if tb else 0,)), ((), ()))

    def body(a_ref, b_ref, o_ref, *scratch):
        part = lax.dot_general(a_ref[...].astype(BF16), b_ref[...].astype(BF16), dims, preferred_element_type=F32)
        if nk == 1:
            o_ref[...] = part.astype(o_ref.dtype)
            return
        acc_ref, = scratch
        k = pl.program_id(2)

        @pl.when(k == 0)
        def _():
            acc_ref[...] = part

        @pl.when(k > 0)
        def _():
            acc_ref[...] += part

        @pl.when(k == nk - 1)
        def _():
            o_ref[...] = acc_ref[...].astype(o_ref.dtype)

    if ta:
        a_spec = pl.BlockSpec((tk, tm), lambda i, j, k: (k, i))
    elif a_shards == 1:
        a_spec = pl.BlockSpec((tm, tk), lambda i, j, k: (i, k))
    else:
        akb = (K // a_shards) // tk
        a_spec = pl.BlockSpec((None, tm, tk), lambda i, j, k: (k // akb, i, k % akb))
    if b_shards == 1:
        b_spec = pl.BlockSpec((tn, tk), lambda i, j, k: (j, k)) if tb else pl.BlockSpec((tk, tn), lambda i, j, k: (k, j))
    elif tb:
        kpb = (K // b_shards) // tk
        b_spec = pl.BlockSpec((None, tn, tk), lambda i, j, k: (k // kpb, j, k % kpb))
    else:
        npb = (N // b_shards) // tn
        b_spec = pl.BlockSpec((None, tk, tn), lambda i, j, k: (j // npb, k, j % npb))
    if out_shards == 1:
        out_spec = pl.BlockSpec((tm, tn), lambda i, j, k: (i, j))
        out_shape = jax.ShapeDtypeStruct((M, N), out_dtype)
    else:
        opb = (N // out_shards) // tn
        out_spec = pl.BlockSpec((None, tm, tn), lambda i, j, k: (j // opb, i, j % opb))
        out_shape = jax.ShapeDtypeStruct((out_shards, M, N // out_shards), out_dtype)
    return pl.pallas_call(
        body, name=name, grid=(M // tm, N // tn, nk),
        in_specs=[a_spec, b_spec], out_specs=out_spec, out_shape=out_shape,
        scratch_shapes=[pltpu.VMEM((tm, tn), F32)] if nk > 1 else [],
        compiler_params=_cparams(("parallel", "parallel", "arbitrary")),
    )(a, b)


FFN_TILE_ROWS = 1024


def _ffn_in_swiglu(x, w4, *, name):
    M, K = x.shape
    S, _, ns = w4.shape
    half = S * ns // 2
    tn = _pick(ns, 512, LANES)
    tm = _pick(M, FFN_TILE_ROWS, 16)
    npb = ns // tn

    def body(x_ref, wa_ref, wb_ref, h_ref, ab_ref):
        xb = x_ref[...].astype(BF16)
        a = jnp.dot(xb, wa_ref[...].astype(BF16), preferred_element_type=F32)
        b = jnp.dot(xb, wb_ref[...].astype(BF16), preferred_element_type=F32)
        h_ref[...] = (jax.nn.silu(a) * b).astype(h_ref.dtype)
        ab_ref[0] = a.astype(ab_ref.dtype)
        ab_ref[1] = b.astype(ab_ref.dtype)

    return pl.pallas_call(
        body, name=name, grid=(M // tm, half // tn),
        in_specs=[pl.BlockSpec((tm, K), lambda i, j: (i, 0)),
                  pl.BlockSpec((None, K, tn), lambda i, j: (j // npb, 0, j % npb)),
                  pl.BlockSpec((None, K, tn), lambda i, j: (S // 2 + j // npb, 0, j % npb))],
        out_specs=[pl.BlockSpec((tm, tn), lambda i, j: (i, j)), pl.BlockSpec((2, tm, tn), lambda i, j: (0, i, j))],
        out_shape=[jax.ShapeDtypeStruct((M, half), BF16), jax.ShapeDtypeStruct((2, M, half), BF16)],
        compiler_params=_cparams(("parallel", "parallel")),
    )(x, w4, w4)


def _ffn_out_dx_swiglu(dy, w, ab, *, name):
    M, D = dy.shape
    n2 = w.shape[0]
    tn = _pick(n2, 512, LANES)
    tm = _pick(M, FFN_TILE_ROWS, 16)

    def body(dy_ref, w_ref, ab_ref, o_ref):
        dh = lax.dot_general(dy_ref[...].astype(BF16), w_ref[...].astype(BF16), NT_DIMS, preferred_element_type=F32)
        a, b = ab_ref[0].astype(F32), ab_ref[1].astype(F32)
        s = jax.nn.sigmoid(a)
        o_ref[0] = (dh * b * (s * (1.0 + a * (1.0 - s)))).astype(o_ref.dtype)
        o_ref[1] = (dh * (a * s)).astype(o_ref.dtype)

    return pl.pallas_call(
        body, name=name, grid=(M // tm, n2 // tn),
        in_specs=[pl.BlockSpec((tm, D), lambda i, j: (i, 0)), pl.BlockSpec((tn, D), lambda i, j: (j, 0)),
                  pl.BlockSpec((2, tm, tn), lambda i, j: (0, i, j))],
        out_specs=pl.BlockSpec((2, tm, tn), lambda i, j: (0, i, j)),
        out_shape=jax.ShapeDtypeStruct((2, M, n2), BF16),
        compiler_params=_cparams(("parallel", "parallel")),
    )(dy, w, ab)


def _row_tile(tiled, extra_bytes=0):
    rows = tiled[0].shape[0]
    per_row = sum(a.shape[1] * 4 for a in tiled) + extra_bytes
    target = max(SUBLANES, (6 * 1024 * 1024) // max(per_row, 1))
    return _pick(rows, min(target, 512), 16)


def _rw(f, tiled, bcast, out_dtypes, *, name, anchor=None):
    nt, nb = len(tiled), len(bcast)
    rows = tiled[0].shape[0]
    outs_aval = jax.eval_shape(f, *[jax.ShapeDtypeStruct((16, a.shape[1]), F32) for a in tiled],
                               *[jax.ShapeDtypeStruct(b.shape, F32) for b in bcast])
    widths = [o.shape[1] for o in outs_aval]
    tm = _row_tile(tiled, sum(w * 4 for w in widths))

    extra = [] if anchor is None else [anchor]
    n_in = nt + nb + len(extra)

    def body(*refs):
        tin = [r[...].astype(F32) for r in refs[:nt]]
        bin_ = [r[...].astype(F32) for r in refs[nt:nt + nb]]
        outs = f(*tin, *bin_)
        for o_ref, o in zip(refs[n_in:], outs):
            o_ref[...] = o.astype(o_ref.dtype)

    in_specs = [pl.BlockSpec((tm, a.shape[1]), lambda i: (i, 0)) for a in tiled]
    in_specs += [pl.BlockSpec(b.shape, lambda i: (0, 0)) for b in bcast + extra]
    res = pl.pallas_call(
        body, name=name, grid=(rows // tm,), in_specs=in_specs,
        out_specs=[pl.BlockSpec((tm, w), lambda i: (i, 0)) for w in widths],
        out_shape=[jax.ShapeDtypeStruct((rows, w), dt) for w, dt in zip(widths, out_dtypes)],
        compiler_params=_cparams(("parallel",)),
    )(*tiled, *bcast, *extra)
    return list(res)


def _rw_vjp(f, tiled, bcast, cts, need_t, need_b, t_dtypes, *, name):
    nt, nb = len(tiled), len(bcast)
    rows = tiled[0].shape[0]
    flat_cts = [c for group in cts for c in group]
    t_idx = [i for i in range(nt) if need_t[i]]
    b_idx = [i for i in range(nb) if need_b[i]]
    tm = _row_tile(list(tiled) + flat_cts, sum(tiled[i].shape[1] * 4 for i in t_idx))
    nc = len(flat_cts)

    def body(*refs):
        i = pl.program_id(0)
        tin = [r[...].astype(F32) for r in refs[:nt]]
        bin_ = [r[...].astype(F32) for r in refs[nt:nt + nb]]
        ct_refs = refs[nt + nb:nt + nb + nc]
        out_refs = refs[nt + nb + nc:]
        outs, vjp_fn = jax.vjp(f, *tin, *bin_)
        ct_vals, pos = [], 0
        for o, group in zip(outs, cts):
            acc = jnp.zeros_like(o)
            for _ in group:
                acc = acc + ct_refs[pos][...].astype(F32)
                pos += 1
            ct_vals.append(acc)
        grads = vjp_fn(tuple(ct_vals))
        for o_ref, k in zip(out_refs[:len(t_idx)], t_idx):
            o_ref[...] = grads[k].astype(o_ref.dtype)
        for o_ref, k in zip(out_refs[len(t_idx):], b_idx):
            @pl.when(i == 0)
            def _(o_ref=o_ref):
                o_ref[...] = jnp.zeros_like(o_ref)

            o_ref[...] += grads[nt + k]

    in_specs = [pl.BlockSpec((tm, a.shape[1]), lambda i: (i, 0)) for a in tiled]
    in_specs += [pl.BlockSpec(b.shape, lambda i: (0, 0)) for b in bcast]
    in_specs += [pl.BlockSpec((tm, c.shape[1]), lambda i: (i, 0)) for c in flat_cts]
    out_specs = [pl.BlockSpec((tm, tiled[k].shape[1]), lambda i: (i, 0)) for k in t_idx]
    out_specs += [pl.BlockSpec(bcast[k].shape, lambda i: (0, 0)) for k in b_idx]
    out_shape = [jax.ShapeDtypeStruct(tiled[k].shape, dt) for k, dt in zip(t_idx, t_dtypes)]
    out_shape += [jax.ShapeDtypeStruct(bcast[k].shape, F32) for k in b_idx]
    res = pl.pallas_call(
        body, name=name, grid=(rows // tm,), in_specs=in_specs, out_specs=out_specs, out_shape=out_shape,
        compiler_params=_cparams(("arbitrary",)),
    )(*tiled, *bcast, *flat_cts)
    res = list(res)
    return res[:len(t_idx)], res[len(t_idx):]


def _rms(x, g):
    return x * lax.rsqrt(jnp.mean(x * x, axis=-1, keepdims=True) + EPS) * g


def _f_norm_mod(x, g, sc, sh):
    return (_rms(x, g) * (1.0 + sc) + sh,)


def _f_norm_mod_keep(x, g, sc, sh):
    return (_rms(x, g) * (1.0 + sc) + sh, x)


@jax.custom_vjp
def _swap16(x):
    w = x.shape[-1]
    lane = lax.broadcasted_iota(jnp.int32, x.shape, x.ndim - 1)
    return jnp.where((lane & 16) == 0, pltpu.roll(x, w - 16, x.ndim - 1), pltpu.roll(x, 16, x.ndim - 1))


_swap16.defvjp(lambda x: (_swap16(x), None), lambda _, g: (_swap16(g),))


def _rope(x, cos, sin):
    return x * cos + _swap16(x) * sin


def _make_f_post_in(sw, q_rank, kv_rank, with_q):
    o1, o2, o3 = sw, sw + q_rank, sw + q_rank + kv_rank

    if with_q:
        def f(ha, cos, sin, qg, kvg):
            u = ha[:, :o1]
            cqn = _rms(ha[:, o1:o2], qg)
            ckvn = _rms(ha[:, o2:o3], kvg)
            kr = _rope(ha[:, o3:o3 + LANES], cos, sin)
            return u, cqn, ckvn, kr
    else:
        def f(ha, kvg):
            return ha[:, :o1], _rms(ha[:, o2:o3], kvg), ha[:, o3:o3 + LANES]
    return f


def _f_qpost(q2, cos, sin):
    parts = []
    for h in range(q2.shape[1] // (2 * LANES)):
        o = 2 * LANES * h
        parts += [q2[:, o:o + LANES], _rope(q2[:, o + LANES:o + 2 * LANES], cos, sin)]
    return (jnp.concatenate(parts, axis=1),)


def _f_s5post(u, r, d):
    return (jax.nn.gelu(d * u + r, approximate=True),)


def _f_merge(ab, bm, gt):
    d = bm.shape[1]
    br_s5 = ab[:, :d] * jax.nn.sigmoid(ab[:, d:])
    g = jax.nn.sigmoid(gt)
    return (g[:, :d] * br_s5 + g[:, d:] * bm,)


def _f_resid_norm(x, out, g1, n2, sc2, sh2):
    x1 = x + g1 * out
    return x1, _rms(x1, n2) * (1.0 + sc2) + sh2


def _f_final(x1, f, tgt, g2, nf):
    y = _rms(x1 + g2 * f, nf)
    return (0.5 * jnp.mean(jnp.square(y - tgt), axis=-1, keepdims=True),)


def _bd_fanin(xs, ws, *, name):
    nw = len(ws)
    nb, kb, nn = ws[0].shape
    T = xs[0].shape[0]
    tm = _pick(T, 512, 16)

    def body(*refs):
        acc = None
        for x_ref, w_ref in zip(refs[:nw], refs[nw:2 * nw]):
            t = jnp.dot(x_ref[...].astype(BF16), w_ref[0].astype(BF16), preferred_element_type=F32)
            acc = t if acc is None else acc + t
        refs[2 * nw][...] = acc

    return pl.pallas_call(
        body, name=name, grid=(nb, T // tm),
        in_specs=[pl.BlockSpec((tm, kb), lambda j, i: (i, j))] * nw + [pl.BlockSpec((1, kb, nn), lambda j, i: (j, 0, 0))] * nw,
        out_specs=pl.BlockSpec((tm, nn), lambda j, i: (i, j)),
        out_shape=jax.ShapeDtypeStruct((T, nb * nn), F32),
        compiler_params=_cparams(("parallel", "parallel")),
    )(*xs, *ws)


def _bd_dw(xs, dys, nb, *, name):
    npair = len(xs)
    T = xs[0].shape[0]
    kb = xs[0].shape[1] // nb
    nn = dys[0].shape[1] // nb
    tm = _pick(T, 512, 16)
    dims = (((0,), (0,)), ((), ()))

    def body(*refs):
        i = pl.program_id(1)
        for x_ref, d_ref, o_ref in zip(refs[:npair], refs[npair:2 * npair], refs[2 * npair:]):
            @pl.when(i == 0)
            def _(o_ref=o_ref):
                o_ref[...] = jnp.zeros_like(o_ref)

            o_ref[0] += lax.dot_general(x_ref[...].astype(BF16), d_ref[...].astype(BF16), dims,
                                        preferred_element_type=F32)

    return list(pl.pallas_call(
        body, name=name, grid=(nb, T // tm),
        in_specs=[pl.BlockSpec((tm, kb), lambda j, i: (i, j))] * npair + [pl.BlockSpec((tm, nn), lambda j, i: (i, j))] * npair,
        out_specs=[pl.BlockSpec((1, kb, nn), lambda j, i: (j, 0, 0))] * npair,
        out_shape=[jax.ShapeDtypeStruct((nb, kb, nn), F32)] * npair,
        compiler_params=_cparams(("parallel", "arbitrary")),
    )(*xs, *dys))


def _cmul(ar, ai, br, bi):
    return ar * br - ai * bi, ar * bi + ai * br


def _cpow(lr, li, n):
    rr, ri = None, None
    br, bi = lr, li
    while n:
        if n & 1:
            rr, ri = (br, bi) if rr is None else _cmul(rr, ri, br, bi)
        n >>= 1
        if n:
            br, bi = _cmul(br, bi, br, bi)
    return rr, ri


SCAN_MM_ROWS = 512


def _s5_scan(x, w_re, w_im, lam_re, lam_im, h0_re, h0_im, e0_re, e0_im, *, reverse, name):
    rows = x.shape[0]
    nb, kb, cb = w_re.shape
    C = nb * cb
    n = rows // N_SEG
    mm_rows = _pick(rows, SCAN_MM_ROWS, 16)
    seg_order = list(range(N_SEG))[::-1] if reverse else list(range(N_SEG))
    s_first, s_last = seg_order[0], seg_order[-1]

    def body(x_ref, wr_ref, wi_ref, lr_ref, li_ref, h0r_ref, h0i_ref, e0r_ref, e0i_ref, hr_ref, hi_ref, htr_ref, hti_ref,
             locr_ref, loci_ref):
        shape = (N_SEG, cb)
        lr = jnp.broadcast_to(lr_ref[...], shape)
        li = jnp.broadcast_to(li_ref[...], shape)
        row = lax.broadcasted_iota(jnp.int32, shape, 0)

        def step_of(k):
            return (n - 1 - k) if reverse else k

        def rows_of(k):
            return pl.ds(pl.multiple_of(step_of(k) * N_SEG, N_SEG), N_SEG)

        wr, wi = wr_ref[...].astype(BF16), wi_ref[...].astype(BF16)
        for r0 in range(0, rows, mm_rows):
            xb = x_ref[r0:r0 + mm_rows, :].astype(BF16)
            locr_ref[r0:r0 + mm_rows, :] = jnp.dot(xb, wr, preferred_element_type=F32)
            loci_ref[r0:r0 + mm_rows, :] = jnp.dot(xb, wi, preferred_element_type=F32)

        first = row == s_first
        hr = locr_ref[rows_of(0), :] + jnp.where(first, e0r_ref[...], 0.0)
        hi = loci_ref[rows_of(0), :] + jnp.where(first, e0i_ref[...], 0.0)
        locr_ref[rows_of(0), :] = hr
        loci_ref[rows_of(0), :] = hi

        def pass1(k, carry):
            hr, hi = carry
            pr, pi = _cmul(lr, li, hr, hi)
            hr = pr + locr_ref[rows_of(k), :]
            hi = pi + loci_ref[rows_of(k), :]
            locr_ref[rows_of(k), :] = hr
            loci_ref[rows_of(k), :] = hi
            return hr, hi

        er, ei = lax.fori_loop(1, n, pass1, (hr, hi))

        lnr, lni = _cpow(lr_ref[...], li_ref[...], n)
        cr, ci = h0r_ref[...], h0i_ref[...]
        cin_r = jnp.zeros(shape, F32)
        cin_i = jnp.zeros(shape, F32)
        for s in seg_order:
            cin_r = jnp.where(row == s, cr, cin_r)
            cin_i = jnp.where(row == s, ci, cin_i)
            if s != s_last:
                pr, pi = _cmul(lnr, lni, cr, ci)
                cr = pr + jnp.sum(jnp.where(row == s, er, 0.0), axis=0, keepdims=True)
                ci = pi + jnp.sum(jnp.where(row == s, ei, 0.0), axis=0, keepdims=True)

        def pass2(k, carry):
            pr, pi, _, _ = carry
            ar, ai = _cmul(pr, pi, cin_r, cin_i)
            hr = locr_ref[rows_of(k), :] + ar
            hi = loci_ref[rows_of(k), :] + ai
            hr_ref[rows_of(k), :] = hr.astype(hr_ref.dtype)
            hi_ref[rows_of(k), :] = hi.astype(hi_ref.dtype)
            npr, npi = _cmul(pr, pi, lr, li)
            return npr, npi, hr, hi

        _, _, last_r, last_i = lax.fori_loop(0, n, pass2, (lr, li, er, ei))
        htr_ref[...] = jnp.sum(jnp.where(row == s_last, last_r, 0.0), axis=0, keepdims=True)
        hti_ref[...] = jnp.sum(jnp.where(row == s_last, last_i, 0.0), axis=0, keepdims=True)

    big = pl.BlockSpec((rows, cb), lambda j: (0, j))
    vec = pl.BlockSpec((1, cb), lambda j: (0, j))
    wspec = pl.BlockSpec((None, kb, cb), lambda j: (j, 0, 0))
    return pl.pallas_call(
        body, name=name, grid=(nb,),
        in_specs=[pl.BlockSpec((rows, kb), lambda j: (0, j)), wspec, wspec] + [vec] * 6,
        out_specs=[big, big, vec, vec],
        out_shape=[jax.ShapeDtypeStruct((rows, C), BF16)] * 2 + [jax.ShapeDtypeStruct((1, C), F32)] * 2,
        scratch_shapes=[pltpu.VMEM((rows, cb), F32)] * 2,
        compiler_params=_cparams(("parallel",)),
    )(x, w_re, w_im, lam_re, lam_im, h0_re, h0_im, e0_re, e0_im)


def _s5_dlam(mu_re, mu_im, h_re, h_im, h0_re, h0_im, *, reverse, name):
    rows, C = h_re.shape
    n = rows // N_SEG
    cb = _pick(C, 256, LANES)
    s_first = N_SEG - 1 if reverse else 0

    def body(mr_ref, mi_ref, hr_ref, hi_ref, h0r_ref, h0i_ref, dr_ref, di_ref):
        shape = (N_SEG, cb)
        row = lax.broadcasted_iota(jnp.int32, shape, 0)

        def rows_of(k):
            step = (n - 1 - k) if reverse else k
            return pl.ds(pl.multiple_of(step * N_SEG, N_SEG), N_SEG)

        def term(k, pr, pi):
            mr, mi = mr_ref[rows_of(k), :].astype(F32), mi_ref[rows_of(k), :].astype(F32)
            return mr * pr + mi * pi, mi * pr - mr * pi

        shift = N_SEG - 1 if reverse else 1
        pr = jnp.where(row == s_first, h0r_ref[...], pltpu.roll(hr_ref[rows_of(n - 1), :].astype(F32), shift, 0))
        pi = jnp.where(row == s_first, h0i_ref[...], pltpu.roll(hi_ref[rows_of(n - 1), :].astype(F32), shift, 0))
        acc = term(0, pr, pi)

        def loop(k, acc):
            tr, ti = term(k, hr_ref[rows_of(k - 1), :].astype(F32), hi_ref[rows_of(k - 1), :].astype(F32))
            return acc[0] + tr, acc[1] + ti

        ar, ai = lax.fori_loop(1, n, loop, acc)
        dr_ref[...] = jnp.sum(ar, axis=0, keepdims=True)
        di_ref[...] = jnp.sum(ai, axis=0, keepdims=True)

    big = pl.BlockSpec((rows, cb), lambda j: (0, j))
    vec = pl.BlockSpec((1, cb), lambda j: (0, j))
    return pl.pallas_call(
        body, name=name, grid=(C // cb,),
        in_specs=[big] * 4 + [vec] * 2, out_specs=[vec, vec],
        out_shape=[jax.ShapeDtypeStruct((1, C), F32)] * 2,
        compiler_params=_cparams(("parallel",)),
    )(mu_re, mu_im, h_re, h_im, h0_re, h0_im)


NT_DIMS = (((1,), (1,)), ((), ()))
TN_DIMS = (((0,), (0,)), ((), ()))


ATTN_Q_ROWS = 512


def _attn_exp(q, kvh, kr):
    s = (lax.dot_general(q[:, :LANES], kvh[:, :LANES], NT_DIMS, preferred_element_type=F32)
         + lax.dot_general(q[:, LANES:], kr, NT_DIMS, preferred_element_type=F32))
    e = jnp.exp2((s - jnp.max(s, axis=-1, keepdims=True)) * (ATTN_SCALE * math.log2(math.e)))
    return e, jnp.sum(e, axis=-1, keepdims=True)


def _attn_specs(L, T, tq):
    return [
        pl.BlockSpec((tq, 2 * LANES), lambda h, i: (i, h)),
        pl.BlockSpec((T, 2 * LANES), lambda h, i: (0, h)),
        pl.BlockSpec((T, LANES), lambda h, i: (0, 0)),
    ]


def _attn_fwd(qq, kv, kr, *, name):
    L, T = qq.shape[0], kv.shape[0]
    tq = _pick(L, ATTN_Q_ROWS // 2, 16)

    def body(q_ref, kv_ref, kr_ref, o_ref):
        kvh = kv_ref[...]
        e, l = _attn_exp(q_ref[...], kvh, kr_ref[...])
        o_ref[...] = (jnp.dot(e.astype(BF16), kvh[:, LANES:], preferred_element_type=F32) * (1.0 / l)).astype(o_ref.dtype)

    return pl.pallas_call(
        body, name=name, grid=(MLA_HEADS, L // tq), in_specs=_attn_specs(L, T, tq),
        out_specs=pl.BlockSpec((tq, LANES), lambda h, i: (i, h)),
        out_shape=jax.ShapeDtypeStruct((L, MLA_HEADS * V_DIM), BF16),
        compiler_params=_cparams(("parallel", "parallel")),
    )(qq, kv, kr)


def _attn_bwd(qq, kv, kr, do, *, name):
    L, T = qq.shape[0], kv.shape[0]
    H = MLA_HEADS
    tq = _pick(L, ATTN_Q_ROWS, 16)
    nq = L // tq

    def body(q_ref, kv_ref, kr_ref, do_ref, dq_ref, dkv_ref, dkr_ref, dkn_acc, dv_acc):
        h, i = pl.program_id(0), pl.program_id(1)
        q, kvh, krv, dov = q_ref[...], kv_ref[...], kr_ref[...], do_ref[...]
        e, l = _attn_exp(q, kvh, krv)
        inv = 1.0 / l
        ps = e * (inv * ATTN_SCALE)
        t = lax.dot_general(dov, kvh[:, LANES:], NT_DIMS, preferred_element_type=F32) * ps
        ds = (t - ps * (jnp.sum(t, axis=-1, keepdims=True) * (1.0 / ATTN_SCALE))).astype(BF16)
        dq_ref[:, :LANES] = jnp.dot(ds, kvh[:, :LANES], preferred_element_type=F32)
        dq_ref[:, LANES:] = jnp.dot(ds, krv, preferred_element_type=F32)

        @pl.when(i == 0)
        def _():
            dkn_acc[...] = jnp.zeros_like(dkn_acc)
            dv_acc[...] = jnp.zeros_like(dv_acc)

        @pl.when((i == 0) & (h == 0))
        def _():
            dkr_ref[...] = jnp.zeros_like(dkr_ref)

        dv_acc[...] += lax.dot_general(e.astype(BF16), (dov.astype(F32) * inv).astype(BF16), TN_DIMS,
                                       preferred_element_type=F32)
        dkn_acc[...] += lax.dot_general(ds, q[:, :LANES], TN_DIMS, preferred_element_type=F32)
        dkr_ref[...] += lax.dot_general(ds, q[:, LANES:], TN_DIMS, preferred_element_type=F32)

        @pl.when(i == nq - 1)
        def _():
            dkv_ref[:, :LANES] = dkn_acc[...].astype(dkv_ref.dtype)
            dkv_ref[:, LANES:] = dv_acc[...].astype(dkv_ref.dtype)

    in_specs = _attn_specs(L, T, tq) + [pl.BlockSpec((tq, LANES), lambda h, i: (i, h))]
    return pl.pallas_call(
        body, name=name, grid=(H, L // tq), in_specs=in_specs,
        out_specs=[pl.BlockSpec((tq, 2 * LANES), lambda h, i: (i, h)), pl.BlockSpec((T, 2 * LANES), lambda h, i: (0, h)),
                   pl.BlockSpec((T, LANES), lambda h, i: (0, 0))],
        out_shape=[jax.ShapeDtypeStruct((L, H * 2 * LANES), F32), jax.ShapeDtypeStruct((T, H * 2 * LANES), BF16),
                   jax.ShapeDtypeStruct((T, LANES), F32)],
        scratch_shapes=[pltpu.VMEM((T, LANES), F32), pltpu.VMEM((T, LANES), F32)],
        compiler_params=_cparams(("arbitrary", "arbitrary")),
    )(qq, kv, kr, do)


def _adamw(w, g, m, v, *, name, anchor=None):
    c1 = 1.0 - ADAM_B1 ** ADAM_STEP
    c2 = 1.0 - ADAM_B2 ** ADAM_STEP

    def f(w, g, m, v):
        m = ADAM_B1 * m + (1.0 - ADAM_B1) * g
        v = ADAM_B2 * v + (1.0 - ADAM_B2) * jnp.square(g)
        delta = -ADAM_LR * ((m / c1) / (jnp.sqrt(v / c2) + ADAM_EPS) + ADAM_WD * w)
        return g, delta, m, v

    return _rw(f, [w, g, m, v], [], [F32] * 4, name=name, anchor=anchor)


def _slab_rows(rows, cols, n_arrays):
    return _pick(rows, max(16, (8 * 1024 * 1024) // (cols * 4 * n_arrays)), 16)


def _scalars(*vals):
    return jnp.stack([jnp.asarray(v, jnp.int32) for v in vals])


def _into_slot(src, slot, nslots, dtype, *, name):
    R, C = src.shape
    tr = _slab_rows(R, C, 2)

    def body(s_ref, x_ref, o_ref):
        o_ref[...] = x_ref[...].astype(o_ref.dtype)

    return pl.pallas_call(
        body, name=name,
        grid_spec=pltpu.PrefetchScalarGridSpec(
            num_scalar_prefetch=1, grid=(R // tr,),
            in_specs=[pl.BlockSpec((tr, C), lambda i, s: (i, 0))],
            out_specs=pl.BlockSpec((None, tr, C), lambda i, s: (s[0], i, 0))),
        out_shape=jax.ShapeDtypeStruct((nslots, R, C), dtype),
        compiler_params=_cparams(("arbitrary",)),
    )(_scalars(slot), src)


def _pair_sum(g, got, c, *, name):
    _, R, C = g.shape
    hr = R // 2
    tr = _slab_rows(hr, C, 3)
    nblk = hr // tr

    def body(s_ref, g_ref, r_ref, o_ref):
        o_ref[...] = (g_ref[...].astype(F32) + r_ref[...].astype(F32)).astype(o_ref.dtype)

    return pl.pallas_call(
        body, name=name,
        grid_spec=pltpu.PrefetchScalarGridSpec(
            num_scalar_prefetch=1, grid=(4, nblk),
            in_specs=[pl.BlockSpec((None, tr, C), lambda j, i, s: (j, s[0] * nblk + i, 0)),
                      pl.BlockSpec((None, tr, C), lambda j, i, s: (j, i, 0))],
            out_specs=pl.BlockSpec((None, tr, C), lambda j, i, s: (j, i, 0))),
        out_shape=jax.ShapeDtypeStruct((4, hr, C), g.dtype),
        compiler_params=_cparams(("arbitrary", "arbitrary")),
    )(_scalars(c), g, got)


def _chip_sum(p, landed, me_chip, c, *, name):
    _, hr, C = p.shape
    tr = _slab_rows(hr, C, 5)

    def body(s_ref, p_ref, l0_ref, l1_ref, l2_ref, o_ref):
        o_ref[...] = ((p_ref[...].astype(F32) + l0_ref[...].astype(F32)) + l1_ref[...].astype(F32)) + l2_ref[...].astype(F32)

    return pl.pallas_call(
        body, name=name,
        grid_spec=pltpu.PrefetchScalarGridSpec(
            num_scalar_prefetch=1, grid=(hr // tr,),
            in_specs=[pl.BlockSpec((None, tr, C), lambda i, s: (s[0], i, 0))]
            + [pl.BlockSpec((None, tr, C), functools.partial(lambda i, s, k: (k, i, 0), k=k)) for k in range(3)],
            out_specs=pl.BlockSpec((None, tr, C), lambda i, s: (s[1], i, 0))),
        out_shape=jax.ShapeDtypeStruct((2, hr, C), F32),
        compiler_params=_cparams(("arbitrary",)),
    )(_scalars(me_chip, c), p, landed, landed, landed)


def _place():
    return lax.axis_index("x"), lax.axis_index("y"), lax.axis_index("c")


def _other_chips(x, y):
    chips = [(1 - x, y), (x, 1 - y), (1 - x, 1 - y)]
    return chips, [2 * cx + cy for cx, cy in chips]


HBM = pl.BlockSpec(memory_space=pl.ANY)


def _allgather8(v, *, name):
    rows, cols = v.shape

    def body(v_ref, out_ref, send_sems, recv_sems):
        x, y, c = _place()
        me = 4 * x + 2 * y + c
        out_ref[me] = v_ref[...]
        copies = []
        for k in range(1, 8):
            bx, by, bc = (k >> 2) & 1, (k >> 1) & 1, k & 1
            px, py, pc = x ^ bx, y ^ by, c ^ bc
            cp = pltpu.make_async_remote_copy(
                src_ref=v_ref, dst_ref=out_ref.at[me], send_sem=send_sems.at[k - 1], recv_sem=recv_sems.at[k - 1],
                device_id=(px, py, pc), device_id_type=MESH)
            cp.start()
            copies.append((cp, 4 * px + 2 * py + pc))
        for k, (cp, peer) in enumerate(copies):
            pltpu.make_async_remote_copy(
                src_ref=v_ref, dst_ref=out_ref.at[peer], send_sem=send_sems.at[k], recv_sem=recv_sems.at[k],
                device_id=(x, y, c), device_id_type=MESH).wait_recv()
        for cp, _ in copies:
            cp.wait_send()

    return pl.pallas_call(
        body, name=name, out_shape=jax.ShapeDtypeStruct((8, rows, cols), v.dtype),
        in_specs=[pl.BlockSpec(memory_space=pltpu.VMEM)], out_specs=pl.BlockSpec(memory_space=pltpu.VMEM),
        scratch_shapes=[pltpu.SemaphoreType.DMA((7,)), pltpu.SemaphoreType.DMA((7,))],
        compiler_params=pltpu.CompilerParams(vmem_limit_bytes=VMEM_LIMIT),
    )(v)


def _allgather_shards(bufs, *, name):
    n = len(bufs)

    def body(*refs):
        outs = refs[n:2 * n]
        send_sems, recv_sems = refs[2 * n:]
        x, y, c = _place()
        me_chip = 2 * x + y
        sibling = (x, y, 1 - c)
        chips, chip_ids = _other_chips(x, y)

        def remote(k, j, blk, hf, to):
            hr = bufs[k].shape[1] // 2
            piece = outs[k].at[blk, pl.ds(pl.multiple_of(hf * hr, 16), hr), :]
            return pltpu.make_async_remote_copy(
                src_ref=piece, dst_ref=piece, send_sem=send_sems.at[6 * k + j], recv_sem=recv_sems.at[6 * k + j],
                device_id=to, device_id_type=MESH)

        sends = []
        for k in range(n):
            for j, chip in enumerate(chips):
                cp = remote(k, j, me_chip, c, (*chip, c))
                cp.start()
                sends.append(cp)
        for k in range(n):
            for j, chip in enumerate(chips):
                remote(k, j, chip_ids[j], c, (x, y, c)).wait_recv()
                cp = remote(k, 3 + j, chip_ids[j], c, sibling)
                cp.start()
                sends.append(cp)
        for k in range(n):
            for j in range(3):
                remote(k, 3 + j, chip_ids[j], 1 - c, (x, y, c)).wait_recv()
        for cp in sends:
            cp.wait_send()

    return list(pl.pallas_call(
        body, name=name, out_shape=[jax.ShapeDtypeStruct(b.shape, b.dtype) for b in bufs],
        in_specs=[HBM] * n, out_specs=[HBM] * n, input_output_aliases={k: k for k in range(n)},
        scratch_shapes=[pltpu.SemaphoreType.DMA((6 * n,)), pltpu.SemaphoreType.DMA((6 * n,))],
    )(*bufs))


HBM_SPEC = pl.BlockSpec(memory_space=pltpu.HBM)
SEM_SPEC = pl.BlockSpec(memory_space=pltpu.SEMAPHORE)
EFFECT = pltpu.SideEffectType.DATAFLOW_SIDE_EFFECTING
TOKEN = jax.ShapeDtypeStruct((SUBLANES, LANES), F32)


def _in_hbm(a):
    return pltpu.with_memory_space_constraint(a, pltpu.HBM)


def _half_rows(buf, hf):
    hr = buf.shape[1] // 2
    return pl.ds(pl.multiple_of(hf * hr, 16), hr)


def _plan_ag_ici(refs):
    x, y, c = _place()
    chips, ids = _other_chips(x, y)
    out = []
    for r in refs:
        mine = r.at[2 * x + y, _half_rows(r, c), :]
        out += [(mine, mine, r.at[ids[j], _half_rows(r, c), :], (*chip, c)) for j, chip in enumerate(chips)]
    return out


def _plan_ag_pair(refs):
    x, y, c = _place()
    _, ids = _other_chips(x, y)
    out = []
    for r in refs:
        for j in range(3):
            piece = r.at[ids[j], _half_rows(r, c), :]
            out.append((piece, piece, r.at[ids[j], _half_rows(r, 1 - c), :], (x, y, 1 - c)))
    return out


def _plan_rs_ici(refs):
    x, y, c = _place()
    chips, ids = _other_chips(x, y)
    n = len(refs) // 2
    return [(refs[k].at[ids[j]], refs[n + k].at[j], refs[n + k].at[j], (*chip, c))
            for k in range(n) for j, chip in enumerate(chips)]


def _plan_pair_exchange(refs):
    x, y, c = _place()
    n = len(refs) // 2
    return [(refs[k].at[:, _half_rows(refs[k], 1 - c), :], refs[n + k], refs[n + k], (x, y, 1 - c)) for k in range(n)]


def _plan_pair_gather(refs):
    x, y, c = _place()
    return [(r.at[c], r.at[c], r.at[1 - c], (x, y, 1 - c)) for r in refs]


def _remote(src, dst, send_sem, recv_sem, target):
    return pltpu.make_async_remote_copy(src_ref=src, dst_ref=dst, send_sem=send_sem, recv_sem=recv_sem,
                                        device_id=target, device_id_type=MESH)


def _copy_start(groups, *, name, after=()):
    flat = [a for arrays, _, _ in groups for a in arrays]
    n, ng = len(flat), len(groups)
    after = list(after)
    n_in = n + len(after)

    def body(*refs):
        sems = refs[n_in:n_in + 2 * ng]
        thru = refs[n_in + 2 * ng:n_in + 2 * ng + n]
        token = refs[-1]
        pos = 0
        for g, (arrays, plan, n_copies) in enumerate(groups):
            copies = plan(thru[pos:pos + len(arrays)])
            pos += len(arrays)
            assert len(copies) == n_copies
            for i, (src, dst, _, target) in enumerate(copies):
                _remote(src, dst, sems[2 * g].at[i], sems[2 * g + 1].at[i], target).start()
        token[...] = jnp.zeros_like(token)

    out_shape = tuple(pltpu.SemaphoreType.DMA((n_copies,)) for _, _, n_copies in groups for _ in range(2))
    out_shape += tuple(pltpu.HBM(a.shape, a.dtype) for a in flat) + (TOKEN,)
    res = pl.pallas_call(
        body, name=name, out_shape=out_shape,
        in_specs=(HBM_SPEC,) * n + (pl.BlockSpec(memory_space=pl.ANY),) * len(after),
        out_specs=(SEM_SPEC,) * (2 * ng) + (HBM_SPEC,) * n + (pl.BlockSpec(memory_space=pltpu.VMEM),),
        input_output_aliases={k: 2 * ng + k for k in range(n)},
        compiler_params=pltpu.CompilerParams(has_side_effects=EFFECT),
    )(*[_in_hbm(a) for a in flat], *after)
    sems = [(res[2 * g], res[2 * g + 1]) for g in range(ng)]
    thru, pos = [], 2 * ng
    for arrays, _, _ in groups:
        thru.append(list(res[pos:pos + len(arrays)]))
        pos += len(arrays)
    return sems, thru, res[-1]


def _copy_wait(arrays, sems, plan, n_copies, after, *, name):
    n = len(arrays)
    after = list(after)

    def body(*refs):
        send, recv = refs[n], refs[n + 1]
        x, y, c = _place()
        copies = plan(refs[:n])
        assert len(copies) == n_copies
        for i, (src, dst, landing, target) in enumerate(copies):
            _remote(src, dst, send.at[i], recv.at[i], target).wait_send()
            _remote(landing, landing, send.at[i], recv.at[i], (x, y, c)).wait_recv()

    return list(pl.pallas_call(
        body, name=name, out_shape=tuple(pltpu.HBM(a.shape, a.dtype) for a in arrays),
        in_specs=(HBM_SPEC,) * n + (SEM_SPEC, SEM_SPEC) + (pl.BlockSpec(memory_space=pl.ANY),) * len(after),
        out_specs=(HBM_SPEC,) * n, input_output_aliases={k: k for k in range(n)},
        compiler_params=pltpu.CompilerParams(has_side_effects=EFFECT),
    )(*arrays, *sems, *after))


def _rs_stage1(gs, tag, after=()):
    n = len(gs)
    lands = [lax.empty((4, g.shape[1] // 2, g.shape[2]), g.dtype) for g in gs]
    sems, (arrays,), token = _copy_start([(list(gs) + lands, _plan_pair_exchange, n)], name=f"rs_pair_start_{tag}",
                                         after=after)
    return (sems[0], arrays), token


def _rs_stage2(handle, after, tag):
    sems, arrays = handle
    n = len(arrays) // 2
    arrays = _copy_wait(arrays, sems, _plan_pair_exchange, n, after, name=f"rs_pair_wait_{tag}")
    c = lax.axis_index("c")
    pair = [_pair_sum(g, r, c, name=f"rs_pair_sum_{tag}{k}") for k, (g, r) in enumerate(zip(arrays[:n], arrays[n:]))]
    lands = [lax.empty((3,) + p.shape[1:], p.dtype) for p in pair]
    sems, (arrays,), token = _copy_start([(pair + lands, _plan_rs_ici, 3 * n)], name=f"rs_start_{tag}")
    return (sems[0], arrays), token


def _rs_stage3(handle, after, tag):
    sems, arrays = handle
    n = len(arrays) // 2
    arrays = _copy_wait(arrays, sems, _plan_rs_ici, 3 * n, after, name=f"rs_wait_{tag}")
    x, y, c = _place()
    halves = [_chip_sum(p, l, 2 * x + y, c, name=f"rs_chip_sum_{tag}{k}") for k, (p, l) in enumerate(zip(arrays[:n], arrays[n:]))]
    sems, (halves,), token = _copy_start([(halves, _plan_pair_gather, n)], name=f"rs_gather_start_{tag}")
    return (sems[0], halves), token


def _rs_stage4(handle, after, tag):
    sems, halves = handle
    full = _copy_wait(halves, sems, _plan_pair_gather, len(halves), after, name=f"rs_gather_wait_{tag}")
    return [f.reshape(2 * f.shape[1], f.shape[2]) for f in full]


def _to_segments(a):
    rows = a.shape[0]
    return a.reshape(N_SEG, rows // N_SEG, -1).transpose(1, 0, 2).reshape(rows, -1)


def _from_segments(a):
    rows = a.shape[0]
    return a.reshape(rows // N_SEG, N_SEG, -1).transpose(1, 0, 2).reshape(rows, -1)


def _rope_tables(L):
    t = jnp.arange(L, dtype=jnp.int32)
    row = (t // GRID_W).astype(F32)
    col = (t % GRID_W).astype(F32)
    n_freq = QK_ROPE // 4
    inv = ROPE_BASE ** (-jnp.arange(n_freq, dtype=F32) / n_freq)
    a0, a1 = row[:, None] * inv, col[:, None] * inv
    z = jnp.zeros((L, LANES - QK_ROPE), F32)
    cos = jnp.concatenate([jnp.cos(a0), jnp.cos(a0), jnp.cos(a1), jnp.cos(a1), z], axis=1)
    sin = jnp.concatenate([-jnp.sin(a0), jnp.sin(a0), -jnp.sin(a1), jnp.sin(a1), z], axis=1)
    return _to_segments(cos), _to_segments(sin)


def _col_blocks(w, nblk):
    r, c = w.shape
    return w.reshape(r, nblk, c // nblk).transpose(1, 0, 2)


def _from_col_blocks(w4):
    nblk, r, c = w4.shape
    return w4.transpose(1, 0, 2).reshape(r, nblk * c)


def _s5_discretize(a_re, a_im, log_dt, b_re, b_im):
    dt = jnp.exp(log_dt)[:, None]
    mag = jnp.exp(a_re * dt)
    ab_re, ab_im = mag * jnp.cos(a_im * dt), mag * jnp.sin(a_im * dt)
    den = a_re * a_re + a_im * a_im
    nr, ni = ab_re - 1.0, ab_im
    co_re = (nr * a_re + ni * a_im) / den
    co_im = (ni * a_re - nr * a_im) / den
    bb_re = co_re[..., None] * b_re - co_im[..., None] * b_im
    bb_im = co_re[..., None] * b_im + co_im[..., None] * b_re
    return ab_re, ab_im, bb_re, bb_im


def _diag_blocks_in(bb, gpb):
    G, N, P = bb.shape
    t = jnp.tile(jnp.swapaxes(bb, 1, 2).reshape(G // gpb, gpb * P, N), (1, 1, gpb))
    row = lax.broadcasted_iota(jnp.int32, t.shape, 1) // P
    col = lax.broadcasted_iota(jnp.int32, t.shape, 2) // N
    return jnp.where(row == col, t, 0.0)


def _diag_blocks_out(cc, gpb):
    G, P, N = cc.shape
    t = jnp.tile(jnp.swapaxes(cc, 1, 2).reshape(G // gpb, gpb * N, P), (1, 1, gpb))
    row = lax.broadcasted_iota(jnp.int32, t.shape, 1) // N
    col = lax.broadcasted_iota(jnp.int32, t.shape, 2) // P
    return jnp.where(row == col, t, 0.0)


def _tr(ws):
    return [jnp.swapaxes(w, 1, 2) for w in ws]


WEIGHTS = ['c_ctx', 'w_mod', 'b_mod', 'norm1', 'norm2', 'w_in', 's5_a_re', 's5_a_im', 's5_log_dt', 's5_b_re', 's5_b_im',
           's5_c_re', 's5_c_im', 's5_d', 'w_glu', 'q_norm', 'kv_norm', 'w_uq', 'w_ukv', 'w_mla_o', 'w_out', 'w_ffn_in',
           'w_ffn_out', 'norm_f']
AG_GROUPS = [['w_in'], ['w_glu', 'w_uq', 'w_ukv', 'w_mla_o', 'w_out'], ['w_ffn_in', 'w_ffn_out']]
SMALL = ['norm1', 'norm2', 's5_a_re', 's5_a_im', 's5_log_dt', 's5_b_re', 's5_b_im', 's5_c_re', 's5_c_im', 's5_d',
         'q_norm', 'kv_norm', 'norm_f']


def _pad_rows(a, rows):
    return jnp.concatenate([a, jnp.zeros((rows - a.shape[0],) + a.shape[1:], a.dtype)], axis=0)


def _pack(vals, width, rows):
    flat = jnp.concatenate([v.reshape(-1).astype(F32) for v in vals])
    flat = jnp.concatenate([flat, jnp.zeros((rows * width - flat.shape[0],), F32)])
    return flat.reshape(rows, width)


def _unpack(buf, like):
    flat = buf.reshape(-1)
    out, pos = [], 0
    for v in like:
        out.append(flat[pos:pos + v.size].reshape(v.shape))
        pos += v.size
    return out


def _step(x, c, ctx, loss_target, w, m, v):
    px, py, pc = _place()
    me = 4 * px + 2 * py + pc
    me_chip = 2 * px + py
    L, D = x.shape[1], x.shape[2]
    Lc = ctx.shape[1]
    T = L + Lc
    SW = D // 2
    G = SW // S5_GROUP
    C = G * S5_STATE
    H = MLA_HEADS
    q_rank = w['q_norm'].shape[1]
    kv_rank = w['kv_norm'].shape[1]
    d_ff = w['w_ffn_out'].shape[1] * 4
    wa_used = SW + q_rank + kv_rank + QK_ROPE
    WA = -(-(SW + q_rank + kv_rank + LANES) // 512) * 512

    c_rows = _pad_rows(c.astype(F32), SUBLANES)
    c_all = _allgather8(c_rows, name="ag_cond")[:, 0, :]
    cond = jnp.concatenate([c_all, w['c_ctx'].reshape(1, D)], axis=0)
    cond = _pad_rows(cond, 16)
    (act,) = _rw(lambda t: (jax.nn.silu(t),), [cond], [], [F32], name="cond_silu")
    w_mod, cs_mod = w['w_mod'][0], w['w_mod'].shape[2]
    mod_part = _mm(act, w_mod, out_dtype=F32, name="mod_fwd")
    mod_all = _allgather8(mod_part, name="ag_mod")
    mod_full = jnp.concatenate([mod_all[0], mod_all[2], mod_all[4], mod_all[6]], axis=1) + w['b_mod']
    m_lat = lax.dynamic_slice_in_dim(mod_full, me, 1, axis=0).reshape(6, D)
    m_ctx = mod_full[8].reshape(6, D)
    sh1, sc1, g1, sh2, sc2, g2 = (m_lat[i:i + 1] for i in range(6))
    csh1, csc1 = m_ctx[0:1], m_ctx[1:2]

    ag_groups = [([_into_slot(w[nme][0], me_chip, 4, BF16, name=f"cast_{nme}") for nme in grp], _plan_ag_ici, 3 * len(grp))
                 for grp in AG_GROUPS]
    ag_sems, ag_bufs, ag_token = _copy_start(ag_groups, name="ag_start", after=[mod_full])
    gathered, ag_pair = {}, {}

    def landed(g, after):
        n_cp = 3 * len(AG_GROUPS[g])
        got = _copy_wait(ag_bufs[g], ag_sems[g], _plan_ag_ici, n_cp, after, name=f"ag_wait_{g}")
        sems, (got,), token = _copy_start([(got, _plan_ag_pair, n_cp)], name=f"ag_pair_start_{g}")
        ag_pair[g] = (sems[0], got)
        return token[0, 0]

    def arrive(g, after):
        sems, got = ag_pair[g]
        got = _copy_wait(got, sems, _plan_ag_pair, 3 * len(AG_GROUPS[g]), after, name=f"ag_pair_wait_{g}")
        gathered.update(zip(AG_GROUPS[g], got))

    xs = _to_segments(x[0])
    cs = _to_segments(ctx[0])
    tgt = _to_segments(loss_target[0])
    cos, sin = _rope_tables(L)
    n1, n2, nf = w['norm1'], w['norm2'], w['norm_f'].reshape(1, D)
    qg, kvg = w['q_norm'], w['kv_norm']

    (xn_lat,) = _rw(_f_norm_mod, [xs], [n1 + ag_token[0, 0], sc1, sh1], [BF16], name="norm1_lat")
    (xn_ctx,) = _rw(_f_norm_mod, [cs], [n1, csc1, csh1], [BF16], name="norm1_ctx")
    xn = jnp.concatenate([xn_lat, xn_ctx], axis=0)
    landed(0, [xn])

    gpb = min(S5_BLOCK_GROUPS, G)
    gpo = min(8, G)
    d_skip = w['s5_d'][0].reshape(1, SW)
    disc, vjp_disc, w_b, w_c = [], [], [], []
    for d in range(2):
        prm = (w['s5_a_re'][0, d], w['s5_a_im'][0, d], w['s5_log_dt'][0, d], w['s5_b_re'][0, d], w['s5_b_im'][0, d])

        def prep(a_re, a_im, log_dt, b_re, b_im):
            ab_re, ab_im, bb_re, bb_im = _s5_discretize(a_re, a_im, log_dt, b_re, b_im)
            return ab_re.reshape(1, C), ab_im.reshape(1, C), _diag_blocks_in(bb_re, gpb), _diag_blocks_in(bb_im, gpb)

        out, vj = jax.vjp(prep, *prm)
        disc.append(out)
        vjp_disc.append(vj)
        w_b += [out[2], out[3]]
        w_c += [_diag_blocks_out(w['s5_c_re'][0, d], gpo), -_diag_blocks_out(w['s5_c_im'][0, d], gpo)]
    nb_in = G // gpb
    nb_out = G // gpo

    arrive(0, [xn, tgt] + w_b + w_c)
    w_in = _from_col_blocks(gathered['w_in'])
    w_a = jnp.concatenate([w_in[:, :wa_used], jnp.zeros((D, WA - wa_used), BF16)], axis=1)
    w_g = w_in[:, wa_used:]
    ha = _mm(xn, w_a, out_dtype=F32, name="in_proj")
    ha_lat, ha_ctx = ha[:L], ha[L:]
    gt = _mm(xn_lat, w_g, out_dtype=F32, name="in_gates")
    f_post_lat = _make_f_post_in(SW, q_rank, kv_rank, True)
    f_post_ctx = _make_f_post_in(SW, q_rank, kv_rank, False)
    u_lat, cqn, ckvn_lat, kr_lat = _rw(f_post_lat, [ha_lat, cos, sin], [qg, kvg], [F32, BF16, BF16, BF16], name="post_in_lat")
    u_ctx, ckvn_ctx, kr_ctx = _rw(f_post_ctx, [ha_ctx], [kvg], [F32, BF16, BF16], name="post_in_ctx")
    zero = jnp.zeros((1, C), F32) + landed(1, [u_lat, u_ctx])

    h_lat, h_ctx, hT_ctx = [], [], []
    for d, rev in enumerate((False, True)):
        lr, li = disc[d][0], disc[d][1]
        hcr, hci, tr, ti = _s5_scan(u_ctx, w_b[2 * d], w_b[2 * d + 1], lr, li, zero, zero, zero, zero, reverse=rev,
                                    name=f"s5_scan_ctx_{d}")
        hlr, hli, _, _ = _s5_scan(u_lat, w_b[2 * d], w_b[2 * d + 1], lr, li, tr, ti, zero, zero, reverse=rev,
                                  name=f"s5_scan_lat_{d}")
        h_ctx += [hcr, hci]
        h_lat += [hlr, hli]
        hT_ctx += [tr, ti]
    r5 = _bd_fanin(h_lat, w_c, name="s5_readout")
    (z,) = _rw(_f_s5post, [u_lat, r5], [d_skip], [BF16], name="s5_post")

    arrive(1, [z])
    w_glu, w_ukv, w_mla_o = (gathered[nme] for nme in ('w_glu', 'w_ukv', 'w_mla_o'))
    w_out = gathered['w_out'].reshape(D, D)
    uq3 = _from_col_blocks(gathered['w_uq']).reshape(q_rank, H, QK_NOPE + QK_ROPE)
    w_q2 = jnp.concatenate([uq3, jnp.zeros((q_rank, H, LANES - QK_ROPE), BF16)], axis=2).reshape(q_rank, H * 2 * LANES)
    q2 = _mm(cqn, w_q2, out_dtype=F32, name="q_up")
    (qq,) = _rw(_f_qpost, [q2, cos, sin], [], [BF16], name="q_rope")
    kvn = jnp.concatenate([ckvn_lat, ckvn_ctx], axis=0)
    kr_all = jnp.concatenate([kr_lat, kr_ctx], axis=0)
    kv = _mm(kvn, w_ukv, b_shards=4, out_dtype=BF16, name="kv_up")
    kr_all = kr_all + landed(2, [kv, qq]).astype(BF16)
    o = _attn_fwd(qq, kv, kr_all, name="attn_fwd")

    ab = _mm(z, w_glu, b_shards=4, out_dtype=F32, name="glu_proj")
    bm = _mm(o, w_mla_o, b_shards=4, out_dtype=F32, name="mla_out")
    (mix,) = _rw(_f_merge, [ab, bm, gt], [], [BF16], name="merge")
    out1 = _mm(mix, w_out, out_dtype=F32, name="out_proj")
    x1, xn2 = _rw(_f_resid_norm, [xs, out1], [g1, n2, sc2, sh2], [F32, BF16], name="resid_norm2")
    arrive(2, [xn2])
    w_ffn_in = gathered['w_ffn_in']
    w_ffn_out = gathered['w_ffn_out'].reshape(d_ff, D)
    hmid, ab2 = _ffn_in_swiglu(xn2, w_ffn_in, name="ffn_in")
    f2 = _mm(hmid, w_ffn_out, out_dtype=F32, name="ffn_out")
    (row_loss,) = _rw(_f_final, [x1, f2, tgt], [g2, nf], [F32], name="final_loss")
    loss = lax.psum(jnp.sum(row_loss), ("x", "y", "c"))

    ones = jnp.ones((L, 1), F32)
    (dx1_a, df2), (dg2, dnf) = _rw_vjp(_f_final, [x1, f2, tgt], [g2, nf], [[ones]], [True, True, False], [True, True],
                                       [F32, BF16], name="final_loss_bwd")
    gw_ffn_out = _mm(hmid, df2, ta=True, out_dtype=BF16, name="ffn_out_dw")
    dab2 = _ffn_out_dx_swiglu(df2, w_ffn_out, ab2, name="ffn_out_dx")
    dxn2 = _mm(dab2, w_ffn_in, tb=True, a_shards=2, b_shards=4, out_dtype=F32, name="ffn_in_dx")
    gw_ffn_in = _mm(xn2, dab2, ta=True, b_shards=2, out_shards=4, out_dtype=BF16, name="ffn_in_dw")
    rs_ffn, tok = _rs_stage1([gw_ffn_out.reshape(4, -1, D), gw_ffn_in], "ffn")
    (dx_a, dout1), (dg1, dn2, dsc2, dsh2) = _rw_vjp(
        _f_resid_norm, [xs, out1], [g1, n2 + tok[0, 0], sc2, sh2], [[dx1_a], [dxn2]], [True, True], [True] * 4, [F32, BF16],
        name="resid_norm2_bwd")
    dmix = _mm(dout1, w_out, tb=True, out_dtype=F32, name="out_proj_dx")
    rs_ffn, _ = _rs_stage2(rs_ffn, [dmix], "ffn")
    gw_out = _mm(mix, dout1, ta=True, out_dtype=BF16, name="out_proj_dw")
    (dab, dbm, dgt), _ = _rw_vjp(_f_merge, [ab, bm, gt], [], [[dmix]], [True] * 3, [], [BF16] * 3, name="merge_bwd")
    dz = _mm(dab, w_glu, tb=True, b_shards=4, out_dtype=F32, name="glu_proj_dx")
    gw_glu = _mm(z, dab, ta=True, out_shards=4, out_dtype=BF16, name="glu_proj_dw")
    do = _mm(dbm, w_mla_o, tb=True, b_shards=4, out_dtype=BF16, name="mla_out_dx")
    gw_mla_o = _mm(o, dbm, ta=True, out_shards=4, out_dtype=BF16, name="mla_out_dw")
    dxn_g = _mm(dgt, w_g, tb=True, out_dtype=F32, name="in_gates_dx")
    gw_g = _mm(xn_lat, dgt, ta=True, out_dtype=BF16, name="in_gates_dw")
    rs_mid, tok = _rs_stage1([gw_out.reshape(4, -1, D), gw_glu, gw_mla_o], "mid", after=[gw_g])

    (du_a, dr5), (dd_skip,) = _rw_vjp(_f_s5post, [u_lat, r5], [d_skip + tok[0, 0]], [[dz]], [True, True], [True], [F32, F32],
                                      name="s5_post_bwd")
    dw_c = _bd_dw(h_lat, [dr5] * 4, nb_out, name="s5_readout_dw")
    rs_mid, tok = _rs_stage2(rs_mid, dw_c[:1], "mid")
    zero = zero + tok[0, 0]
    w_ct = _tr(w_c)
    zeros_ctx = jnp.zeros((Lc, SW), BF16)
    mu_lat, mu_ctx, dlam = [], [], []
    for d, rev in enumerate((False, True)):
        lr, li = disc[d][0], disc[d][1]
        mlr, mli, fr, fi = _s5_scan(dr5, w_ct[2 * d], w_ct[2 * d + 1], lr, -li, zero, zero, zero, zero, reverse=not rev,
                                    name=f"s5_adj_lat_{d}")
        dh0r, dh0i = _cmul(lr, -li, fr, fi)
        mcr, mci, _, _ = _s5_scan(zeros_ctx, w_ct[2 * d], w_ct[2 * d + 1], lr, -li, zero, zero, dh0r, dh0i,
                                  reverse=not rev, name=f"s5_adj_ctx_{d}")
        dl_lat = _s5_dlam(mlr, mli, h_lat[2 * d], h_lat[2 * d + 1], hT_ctx[2 * d], hT_ctx[2 * d + 1], reverse=rev,
                          name=f"s5_dlam_lat_{d}")
        dl_ctx = _s5_dlam(mcr, mci, h_ctx[2 * d], h_ctx[2 * d + 1], zero, zero, reverse=rev, name=f"s5_dlam_ctx_{d}")
        mu_lat += [mlr, mli]
        mu_ctx += [mcr, mci]
        dlam.append((dl_lat[0] + dl_ctx[0], dl_lat[1] + dl_ctx[1]))
    du_b = _bd_fanin(mu_lat, _tr(w_b), name="s5_bu_lat_dx")
    du_ctx = _bd_fanin(mu_ctx, _tr(w_b), name="s5_bu_ctx_dx")
    dw_b_lat = _bd_dw([u_lat] * 4, mu_lat, nb_in, name="s5_bu_lat_dw")
    dw_b_ctx = _bd_dw([u_ctx] * 4, mu_ctx, nb_in, name="s5_bu_ctx_dw")
    g_s5 = {}
    for d in range(2):
        ct = (dlam[d][0], dlam[d][1], dw_b_lat[2 * d] + dw_b_ctx[2 * d], dw_b_lat[2 * d + 1] + dw_b_ctx[2 * d + 1])
        ga_re, ga_im, gdt, gb_re, gb_im = vjp_disc[d](ct)
        _, vj_c = jax.vjp(lambda cr, ci: (_diag_blocks_out(cr, gpo), -_diag_blocks_out(ci, gpo)),
                          w['s5_c_re'][0, d], w['s5_c_im'][0, d])
        gc_re, gc_im = vj_c((dw_c[2 * d], dw_c[2 * d + 1]))
        for nme, val in (('s5_a_re', ga_re), ('s5_a_im', ga_im), ('s5_log_dt', gdt), ('s5_b_re', gb_re),
                         ('s5_b_im', gb_im), ('s5_c_re', gc_re), ('s5_c_im', gc_im)):
            g_s5.setdefault(nme, []).append(val)
    g_small = {nme: jnp.stack(vals)[None] for nme, vals in g_s5.items()}
    g_small['s5_d'] = dd_skip.reshape(w['s5_d'].shape)

    dqq, dkv, dkr = _attn_bwd(qq, kv, kr_all, do, name="attn_bwd")
    (dq2,), _ = _rw_vjp(_f_qpost, [q2, cos, sin], [], [[dqq]], [True, False, False], [], [BF16], name="q_rope_bwd")
    dcqn = _mm(dq2, w_q2, tb=True, out_dtype=F32, name="q_up_dx")
    gw_q2 = _mm(cqn, dq2, ta=True, out_dtype=BF16, name="q_up_dw")
    dckvn = _mm(dkv, w_ukv, tb=True, b_shards=4, out_dtype=F32, name="kv_up_dx")
    gw_ukv = _mm(kvn, dkv, ta=True, out_shards=4, out_dtype=BF16, name="kv_up_dw")
    gw_uq = gw_q2.reshape(q_rank, H, 2 * LANES)[:, :, :QK_NOPE + QK_ROPE].reshape(q_rank, H * (QK_NOPE + QK_ROPE))
    rs_kv, tok = _rs_stage1([_col_blocks(gw_uq, 4), gw_ukv], "kv")

    (dha_lat,), (dqg, dkvg_lat) = _rw_vjp(
        f_post_lat, [ha_lat, cos, sin], [qg, kvg + tok[0, 0]], [[du_a, du_b], [dcqn], [dckvn[:L]], [dkr[:L]]],
        [True, False, False], [True, True], [BF16], name="post_in_lat_bwd")
    (dha_ctx,), (dkvg_ctx,) = _rw_vjp(f_post_ctx, [ha_ctx], [kvg], [[du_ctx], [dckvn[L:]], [dkr[L:]]], [True], [True],
                                      [BF16], name="post_in_ctx_bwd")
    dha = jnp.concatenate([dha_lat, dha_ctx], axis=0)
    dxn = _mm(dha, w_a, tb=True, out_dtype=F32, name="in_proj_dx")
    gw_a = _mm(xn, dha, ta=True, out_dtype=BF16, name="in_proj_dw")
    rs_kv, tok = _rs_stage2(rs_kv, [gw_a], "kv")
    (dx_seg,), (dn1_lat, dsc1, dsh1) = _rw_vjp(
        _f_norm_mod_keep, [xs], [n1 + tok[0, 0], sc1, sh1], [[dxn[:L], dxn_g], [dx_a]], [True], [True] * 3, [F32],
        name="norm1_lat_bwd")
    _, (dn1_ctx, dcsc1, dcsh1) = _rw_vjp(_f_norm_mod, [cs], [n1, csc1, csh1], [[dxn[L:]]], [False], [True] * 3, [],
                                         name="norm1_ctx_bwd")
    grad_x = _from_segments(dx_seg)[None]
    g_small.update(norm1=dn1_lat + dn1_ctx, norm2=dn2, q_norm=dqg, kv_norm=dkvg_lat + dkvg_ctx, norm_f=dnf.reshape(D))
    gw_in = jnp.concatenate([gw_a[:, :wa_used], gw_g], axis=1)
    small_vals = [g_small[nme] for nme in SMALL]
    n_small = sum(val.size for val in small_vals)
    small_rows = -(-n_small // (LANES * 4 * 32)) * 32

    zD = jnp.zeros((1, D), F32)
    dm = jnp.concatenate([
        jnp.concatenate([dsh1, dsc1, dg1, dsh2, dsc2, dg2], axis=1),
        jnp.concatenate([dcsh1, dcsc1, zD, zD, zD, zD], axis=1),
    ], axis=0)
    dm_all = _allgather8(_pad_rows(dm, SUBLANES), name="ag_dmod")
    rs_in, _ = _rs_stage1([_col_blocks(gw_in, 4), _pack(small_vals, LANES, 4 * small_rows).reshape(4, small_rows, LANES)],
                          "in", after=[dm_all])
    dm_ctx = dm_all[0, 1]
    for k in range(1, 8):
        dm_ctx = dm_ctx + dm_all[k, 1]
    dmod = _pad_rows(jnp.concatenate([dm_all[:, 0, :], dm_ctx[None]], axis=0), 16)
    g_b_mod = jnp.sum(dmod, axis=0, keepdims=True)
    dmod_mine = lax.dynamic_slice_in_dim(dmod, me_chip * cs_mod, cs_mod, axis=1)
    g_w_mod = _mm(act, dmod_mine, ta=True, out_dtype=F32, name="mod_dw")
    dact_part = _mm(dmod_mine, w_mod, tb=True, out_dtype=F32, name="mod_dx")
    dact_all = _allgather8(dact_part, name="ag_dact")
    dact = dact_all[0] + dact_all[2] + dact_all[4] + dact_all[6]
    (dcond_rows,), _ = _rw_vjp(lambda t: (jax.nn.silu(t),), [cond], [], [[dact]], [True], [], [F32], name="cond_silu_bwd")
    g_c_ctx = dcond_rows[8]

    rs_in, tok = _rs_stage2(rs_in, [g_c_ctx], "in")

    grads, delta, new_m, new_v = {}, {}, {}, {}

    def update(members, reds, anchor):
        deltas = []
        for nme, red in zip(members, reds):
            res = _adamw(w[nme][0], red, m[nme][0], v[nme][0], name=f"adamw_{nme}", anchor=anchor)
            grads[nme], delta[nme], new_m[nme], new_v[nme] = (r.reshape(w[nme].shape) for r in res)
            deltas.append(res[1])
            anchor = None
        return deltas

    rs_ffn, tok = _rs_stage3(rs_ffn, [tok], "ffn")
    done = update(['w_mod'], [g_w_mod], tok)
    red_ffn = _rs_stage4(rs_ffn, done, "ffn")
    rs_mid, tok = _rs_stage3(rs_mid, red_ffn[:1], "mid")
    done = update(['w_ffn_out', 'w_ffn_in'], red_ffn, tok)
    red_mid = _rs_stage4(rs_mid, done, "mid")
    rs_kv, tok = _rs_stage3(rs_kv, red_mid[:1], "kv")
    done = update(['w_out', 'w_glu', 'w_mla_o'], red_mid, tok)
    red_kv = _rs_stage4(rs_kv, done, "kv")
    rs_in, tok = _rs_stage3(rs_in, red_kv[:1], "in")
    done = update(['w_uq', 'w_ukv'], red_kv, tok)
    red_in = _rs_stage4(rs_in, done, "in")
    update(['w_in'], red_in[:1], None)
    small_mine = red_in[-1]
    small_buf = _into_slot(small_mine, me_chip, 4, F32, name="small_grads_slot")
    small_all = _allgather_shards([small_buf], name="ag_small_grads")[0].reshape(4 * small_rows, LANES)
    g_small_red = dict(zip(SMALL, _unpack(small_all, [w[nme] for nme in SMALL])))
    rest = SMALL + ['c_ctx', 'b_mod']
    g_rest = dict(g_small_red, c_ctx=g_c_ctx, b_mod=g_b_mod)
    rows_rest = -(-sum(w[nme].size for nme in rest) // (LANES * 16)) * 16
    packed = [_pack([src[nme] for nme in rest], LANES, rows_rest) for src in (w, g_rest, m, v)]
    res = _adamw(*packed, name="adamw_small")
    for dst, buf in zip((grads, delta, new_m, new_v), res):
        dst.update(zip(rest, _unpack(buf, [w[nme] for nme in rest])))
    return (loss, grad_x, *[grads[nme] for nme in WEIGHTS], *[delta[nme] for nme in WEIGHTS],
            *[new_m[nme] for nme in WEIGHTS], *[new_v[nme] for nme in WEIGHTS])


def kernel(x, c, ctx, c_ctx, w_mod, b_mod, norm1, norm2, w_in, s5_a_re, s5_a_im, s5_log_dt, s5_b_re, s5_b_im, s5_c_re, s5_c_im, s5_d, w_glu, q_norm, kv_norm, w_uq, w_ukv, w_mla_o, w_out, w_ffn_in, w_ffn_out, norm_f, loss_target, m_c_ctx, m_w_mod, m_b_mod, m_norm1, m_norm2, m_w_in, m_s5_a_re, m_s5_a_im, m_s5_log_dt, m_s5_b_re, m_s5_b_im, m_s5_c_re, m_s5_c_im, m_s5_d, m_w_glu, m_q_norm, m_kv_norm, m_w_uq, m_w_ukv, m_w_mla_o, m_w_out, m_w_ffn_in, m_w_ffn_out, m_norm_f, v_c_ctx, v_w_mod, v_b_mod, v_norm1, v_norm2, v_w_in, v_s5_a_re, v_s5_a_im, v_s5_log_dt, v_s5_b_re, v_s5_b_im, v_s5_c_re, v_s5_c_im, v_s5_d, v_w_glu, v_q_norm, v_kv_norm, v_w_uq, v_w_ukv, v_w_mla_o, v_w_out, v_w_ffn_in, v_w_ffn_out, v_norm_f):
    w = dict(c_ctx=c_ctx, w_mod=w_mod, b_mod=b_mod, norm1=norm1, norm2=norm2, w_in=w_in, s5_a_re=s5_a_re, s5_a_im=s5_a_im,
             s5_log_dt=s5_log_dt, s5_b_re=s5_b_re, s5_b_im=s5_b_im, s5_c_re=s5_c_re, s5_c_im=s5_c_im, s5_d=s5_d, w_glu=w_glu,
             q_norm=q_norm, kv_norm=kv_norm, w_uq=w_uq, w_ukv=w_ukv, w_mla_o=w_mla_o, w_out=w_out, w_ffn_in=w_ffn_in,
             w_ffn_out=w_ffn_out, norm_f=norm_f)
    m = dict(c_ctx=m_c_ctx, w_mod=m_w_mod, b_mod=m_b_mod, norm1=m_norm1, norm2=m_norm2, w_in=m_w_in, s5_a_re=m_s5_a_re,
             s5_a_im=m_s5_a_im, s5_log_dt=m_s5_log_dt, s5_b_re=m_s5_b_re, s5_b_im=m_s5_b_im, s5_c_re=m_s5_c_re,
             s5_c_im=m_s5_c_im, s5_d=m_s5_d, w_glu=m_w_glu, q_norm=m_q_norm, kv_norm=m_kv_norm, w_uq=m_w_uq, w_ukv=m_w_ukv,
             w_mla_o=m_w_mla_o, w_out=m_w_out, w_ffn_in=m_w_ffn_in, w_ffn_out=m_w_ffn_out, norm_f=m_norm_f)
    v = dict(c_ctx=v_c_ctx, w_mod=v_w_mod, b_mod=v_b_mod, norm1=v_norm1, norm2=v_norm2, w_in=v_w_in, s5_a_re=v_s5_a_re,
             s5_a_im=v_s5_a_im, s5_log_dt=v_s5_log_dt, s5_b_re=v_s5_b_re, s5_b_im=v_s5_b_im, s5_c_re=v_s5_c_re,
             s5_c_im=v_s5_c_im, s5_d=v_s5_d, w_glu=v_w_glu, q_norm=v_q_norm, kv_norm=v_kv_norm, w_uq=v_w_uq, w_ukv=v_w_ukv,
             w_mla_o=v_w_mla_o, w_out=v_w_out, w_ffn_in=v_w_ffn_in, w_ffn_out=v_w_ffn_out, norm_f=v_norm_f)
    return _step(x, c, ctx, loss_target, w, m, v)
```

```python
import functools
import math

import jax
import jax.numpy as jnp
from jax import lax
from jax.experimental import pallas as pl
from jax.experimental.pallas import tpu as pltpu

F32 = jnp.float32
BF16 = jnp.bfloat16

EPS = 1e-6
GRID_W = 64
S5_GROUP = 16
S5_STATE = 64
MLA_HEADS = 8
QK_NOPE = 128
QK_ROPE = 64
V_DIM = 128
ROPE_BASE = 10000.0
ATTN_SCALE = (QK_NOPE + QK_ROPE) ** -0.5
ADAM_LR = 0.001
ADAM_B1 = 0.9
ADAM_B2 = 0.999
ADAM_EPS = 1e-08
ADAM_WD = 0.01
ADAM_STEP = 10

SUBLANES = 8
LANES = 128
V7X_VMEM_BYTES = 64 * 1024 * 1024
VMEM_LIMIT = (V7X_VMEM_BYTES * 7) // 8
N_SEG = 2 * SUBLANES
S5_BLOCK_GROUPS = 8
MESH = pl.DeviceIdType.MESH


def _pick(n, target, mult):
    best = None
    d = mult
    while d <= min(n, target):
        if n % d == 0:
            best = d
        d += mult
    return n if best is None else best


def _cparams(sem=None):
    return pltpu.CompilerParams(dimension_semantics=sem, vmem_limit_bytes=VMEM_LIMIT)


MM_VMEM_BUDGET = (V7X_VMEM_BYTES * 5) // 8


def _mm(a, b, *, ta=False, tb=False, out_dtype=F32, name, a_shards=1, b_shards=1, out_shards=1):
    if ta:
        K, M = a.shape
    else:
        M, K = a.shape[-2], a.shape[-1] * a_shards
    if tb:
        N, K2 = b.shape[-2], b.shape[-1] * b_shards
    else:
        K2, N = b.shape[-2], b.shape[-1] * b_shards
    assert K == K2, (a.shape, b.shape, ta, tb)
    n_unit = N // max(out_shards, 1 if tb else b_shards)
    k_unit = K // max(a_shards, b_shards if tb else 1)
    tn = _pick(n_unit, 1024, LANES)
    tm = _pick(M, 1024 if tn >= 512 else 2048, LANES if ta else 16)
    sa, sb, so = a.dtype.itemsize, b.dtype.itemsize, jnp.dtype(out_dtype).itemsize
    k_mult = LANES if (not ta or tb) else 16
    tk = k_mult if k_unit % k_mult == 0 else k_unit
    for cand in range(k_mult, k_unit + 1, k_mult):
        if k_unit % cand == 0 and 2 * cand * (tm * sa + tn * sb) + tm * tn * (4 + 2 * so) <= MM_VMEM_BUDGET:
            tk = cand
    nk = K // tk
    dims = (((0 if ta else 1,), (1 if tb else 0,)), ((), ()))

    def body(a_ref, b_ref, o_ref, *scratch):
        part = lax.dot_general(a_ref[...].astype(BF16), b_ref[...].astype(BF16), dims, preferred_element_type=F32)
        if nk == 1:
            o_ref[...] = part.astype(o_ref.dtype)
            return
        acc_ref, = scratch
        k = pl.program_id(2)

        @pl.when(k == 0)
        def _():
            acc_ref[...] = part

        @pl.when(k > 0)
        def _():
            acc_ref[...] += part

        @pl.when(k == nk - 1)
        def _():
            o_ref[...] = acc_ref[...].astype(o_ref.dtype)

    if ta:
        a_spec = pl.BlockSpec((tk, tm), lambda i, j, k: (k, i))
    elif a_shards == 1:
        a_spec = pl.BlockSpec((tm, tk), lambda i, j, k: (i, k))
    else:
        akb = (K // a_shards) // tk
        a_spec = pl.BlockSpec((None, tm, tk), lambda i, j, k: (k // akb, i, k % akb))
    if b_shards == 1:
        b_spec = pl.BlockSpec((tn, tk), lambda i, j, k: (j, k)) if tb else pl.BlockSpec((tk, tn), lambda i, j, k: (k, j))
    elif tb:
        kpb = (K // b_shards) // tk
        b_spec = pl.BlockSpec((None, tn, tk), lambda i, j, k: (k // kpb, j, k % kpb))
    else:
        npb = (N // b_shards) // tn
        b_spec = pl.BlockSpec((None, tk, tn), lambda i, j, k: (j // npb, k, j % npb))
    if out_shards == 1:
        out_spec = pl.BlockSpec((tm, tn), lambda i, j, k: (i, j))
        out_shape = jax.ShapeDtypeStruct((M, N), out_dtype)
    else:
        opb = (N // out_shards) // tn
        out_spec = pl.BlockSpec((None, tm, tn), lambda i, j, k: (j // opb, i, j % opb))
        out_shape = jax.ShapeDtypeStruct((out_shards, M, N // out_shards), out_dtype)
    return pl.pallas_call(
        body, name=name, grid=(M // tm, N // tn, nk),
        in_specs=[a_spec, b_spec], out_specs=out_spec, out_shape=out_shape,
        scratch_shapes=[pltpu.VMEM((tm, tn), F32)] if nk > 1 else [],
        compiler_params=_cparams(("parallel", "parallel", "arbitrary")),
    )(a, b)


FFN_TILE_ROWS = 1024


def _ffn_in_swiglu(x, w4, *, name):
    M, K = x.shape
    S, _, ns = w4.shape
    half = S * ns // 2
    tn = _pick(ns, 512, LANES)
    tm = _pick(M, FFN_TILE_ROWS, 16)
    npb = ns // tn

    def body(x_ref, wa_ref, wb_ref, h_ref, ab_ref):
        xb = x_ref[...].astype(BF16)
        a = jnp.dot(xb, wa_ref[...].astype(BF16), preferred_element_type=F32)
        b = jnp.dot(xb, wb_ref[...].astype(BF16), preferred_element_type=F32)
        h_ref[...] = (jax.nn.silu(a) * b).astype(h_ref.dtype)
        ab_ref[0] = a.astype(ab_ref.dtype)
        ab_ref[1] = b.astype(ab_ref.dtype)

    return pl.pallas_call(
        body, name=name, grid=(M // tm, half // tn),
        in_specs=[pl.BlockSpec((tm, K), lambda i, j: (i, 0)),
                  pl.BlockSpec((None, K, tn), lambda i, j: (j // npb, 0, j % npb)),
                  pl.BlockSpec((None, K, tn), lambda i, j: (S // 2 + j // npb, 0, j % npb))],
        out_specs=[pl.BlockSpec((tm, tn), lambda i, j: (i, j)), pl.BlockSpec((2, tm, tn), lambda i, j: (0, i, j))],
        out_shape=[jax.ShapeDtypeStruct((M, half), BF16), jax.ShapeDtypeStruct((2, M, half), BF16)],
        compiler_params=_cparams(("parallel", "parallel")),
    )(x, w4, w4)


def _ffn_out_dx_swiglu(dy, w, ab, *, name):
    M, D = dy.shape
    n2 = w.shape[0]
    tn = _pick(n2, 512, LANES)
    tm = _pick(M, FFN_TILE_ROWS, 16)

    def body(dy_ref, w_ref, ab_ref, o_ref):
        dh = lax.dot_general(dy_ref[...].astype(BF16), w_ref[...].astype(BF16), NT_DIMS, preferred_element_type=F32)
        a, b = ab_ref[0].astype(F32), ab_ref[1].astype(F32)
        s = jax.nn.sigmoid(a)
        o_ref[0] = (dh * b * (s * (1.0 + a * (1.0 - s)))).astype(o_ref.dtype)
        o_ref[1] = (dh * (a * s)).astype(o_ref.dtype)

    return pl.pallas_call(
        body, name=name, grid=(M // tm, n2 // tn),
        in_specs=[pl.BlockSpec((tm, D), lambda i, j: (i, 0)), pl.BlockSpec((tn, D), lambda i, j: (j, 0)),
                  pl.BlockSpec((2, tm, tn), lambda i, j: (0, i, j))],
        out_specs=pl.BlockSpec((2, tm, tn), lambda i, j: (0, i, j)),
        out_shape=jax.ShapeDtypeStruct((2, M, n2), BF16),
        compiler_params=_cparams(("parallel", "parallel")),
    )(dy, w, ab)


def _row_tile(tiled, extra_bytes=0):
    rows = tiled[0].shape[0]
    per_row = sum(a.shape[1] * 4 for a in tiled) + extra_bytes
    target = max(SUBLANES, (6 * 1024 * 1024) // max(per_row, 1))
    return _pick(rows, min(target, 512), 16)


def _rw(f, tiled, bcast, out_dtypes, *, name, anchor=None):
    nt, nb = len(tiled), len(bcast)
    rows = tiled[0].shape[0]
    outs_aval = jax.eval_shape(f, *[jax.ShapeDtypeStruct((16, a.shape[1]), F32) for a in tiled],
                               *[jax.ShapeDtypeStruct(b.shape, F32) for b in bcast])
    widths = [o.shape[1] for o in outs_aval]
    tm = _row_tile(tiled, sum(w * 4 for w in widths))

    extra = [] if anchor is None else [anchor]
    n_in = nt + nb + len(extra)

    def body(*refs):
        tin = [r[...].astype(F32) for r in refs[:nt]]
        bin_ = [r[...].astype(F32) for r in refs[nt:nt + nb]]
        outs = f(*tin, *bin_)
        for o_ref, o in zip(refs[n_in:], outs):
            o_ref[...] = o.astype(o_ref.dtype)

    in_specs = [pl.BlockSpec((tm, a.shape[1]), lambda i: (i, 0)) for a in tiled]
    in_specs += [pl.BlockSpec(b.shape, lambda i: (0, 0)) for b in bcast + extra]
    res = pl.pallas_call(
        body, name=name, grid=(rows // tm,), in_specs=in_specs,
        out_specs=[pl.BlockSpec((tm, w), lambda i: (i, 0)) for w in widths],
        out_shape=[jax.ShapeDtypeStruct((rows, w), dt) for w, dt in zip(widths, out_dtypes)],
        compiler_params=_cparams(("parallel",)),
    )(*tiled, *bcast, *extra)
    return list(res)


def _rw_vjp(f, tiled, bcast, cts, need_t, need_b, t_dtypes, *, name, anchor=None):
    nt, nb = len(tiled), len(bcast)
    rows = tiled[0].shape[0]
    flat_cts = [c for group in cts for c in group]
    t_idx = [i for i in range(nt) if need_t[i]]
    b_idx = [i for i in range(nb) if need_b[i]]
    tm = _row_tile(list(tiled) + flat_cts, sum(tiled[i].shape[1] * 4 for i in t_idx))
    nc = len(flat_cts)
    extra = [] if anchor is None else [anchor]

    def body(*refs):
        i = pl.program_id(0)
        tin = [r[...].astype(F32) for r in refs[:nt]]
        bin_ = [r[...].astype(F32) for r in refs[nt:nt + nb]]
        ct_refs = refs[nt + nb:nt + nb + nc]
        out_refs = refs[nt + nb + nc + len(extra):]
        outs, vjp_fn = jax.vjp(f, *tin, *bin_)
        ct_vals, pos = [], 0
        for o, group in zip(outs, cts):
            acc = jnp.zeros_like(o)
            for _ in group:
                acc = acc + ct_refs[pos][...].astype(F32)
                pos += 1
            ct_vals.append(acc)
        grads = vjp_fn(tuple(ct_vals))
        for o_ref, k in zip(out_refs[:len(t_idx)], t_idx):
            o_ref[...] = grads[k].astype(o_ref.dtype)
        for o_ref, k in zip(out_refs[len(t_idx):], b_idx):
            @pl.when(i == 0)
            def _(o_ref=o_ref):
                o_ref[...] = jnp.zeros_like(o_ref)

            o_ref[...] += grads[nt + k]

    in_specs = [pl.BlockSpec((tm, a.shape[1]), lambda i: (i, 0)) for a in tiled]
    in_specs += [pl.BlockSpec(b.shape, lambda i: (0, 0)) for b in bcast]
    in_specs += [pl.BlockSpec((tm, c.shape[1]), lambda i: (i, 0)) for c in flat_cts]
    in_specs += [pl.BlockSpec(e.shape, lambda i: (0, 0)) for e in extra]
    out_specs = [pl.BlockSpec((tm, tiled[k].shape[1]), lambda i: (i, 0)) for k in t_idx]
    out_specs += [pl.BlockSpec(bcast[k].shape, lambda i: (0, 0)) for k in b_idx]
    out_shape = [jax.ShapeDtypeStruct(tiled[k].shape, dt) for k, dt in zip(t_idx, t_dtypes)]
    out_shape += [jax.ShapeDtypeStruct(bcast[k].shape, F32) for k in b_idx]
    res = pl.pallas_call(
        body, name=name, grid=(rows // tm,), in_specs=in_specs, out_specs=out_specs, out_shape=out_shape,
        compiler_params=_cparams(("arbitrary",)),
    )(*tiled, *bcast, *flat_cts, *extra)
    res = list(res)
    return res[:len(t_idx)], res[len(t_idx):]


def _rms(x, g):
    return x * lax.rsqrt(jnp.mean(x * x, axis=-1, keepdims=True) + EPS) * g


def _f_norm_mod(x, g, sc, sh):
    return (_rms(x, g) * (1.0 + sc) + sh,)


def _f_norm_mod_keep(x, g, sc, sh):
    return (_rms(x, g) * (1.0 + sc) + sh, x)


@jax.custom_vjp
def _swap16(x):
    w = x.shape[-1]
    lane = lax.broadcasted_iota(jnp.int32, x.shape, x.ndim - 1)
    return jnp.where((lane & 16) == 0, pltpu.roll(x, w - 16, x.ndim - 1), pltpu.roll(x, 16, x.ndim - 1))


_swap16.defvjp(lambda x: (_swap16(x), None), lambda _, g: (_swap16(g),))


def _rope(x, cos, sin):
    return x * cos + _swap16(x) * sin


def _make_f_post_in(sw, q_rank, kv_rank, with_q):
    o1, o2, o3 = sw, sw + q_rank, sw + q_rank + kv_rank

    if with_q:
        def f(ha, cos, sin, qg, kvg):
            u = ha[:, :o1]
            cqn = _rms(ha[:, o1:o2], qg)
            ckvn = _rms(ha[:, o2:o3], kvg)
            kr = _rope(ha[:, o3:o3 + LANES], cos, sin)
            return u, cqn, ckvn, kr
    else:
        def f(ha, kvg):
            return ha[:, :o1], _rms(ha[:, o2:o3], kvg), ha[:, o3:o3 + LANES]
    return f


def _f_qpost(q2, cos, sin):
    parts = []
    for h in range(q2.shape[1] // (2 * LANES)):
        o = 2 * LANES * h
        parts += [q2[:, o:o + LANES], _rope(q2[:, o + LANES:o + 2 * LANES], cos, sin)]
    return (jnp.concatenate(parts, axis=1),)


def _f_s5post(u, r, d):
    return (jax.nn.gelu(d * u + r, approximate=True),)


def _f_merge(ab, bm, gt):
    d = bm.shape[1]
    br_s5 = ab[:, :d] * jax.nn.sigmoid(ab[:, d:])
    g = jax.nn.sigmoid(gt)
    return (g[:, :d] * br_s5 + g[:, d:] * bm,)


def _f_resid_norm(x, out, g1, n2, sc2, sh2):
    x1 = x + g1 * out
    return x1, _rms(x1, n2) * (1.0 + sc2) + sh2


def _f_final(x1, f, tgt, g2, nf):
    y = _rms(x1 + g2 * f, nf)
    return (0.5 * jnp.mean(jnp.square(y - tgt), axis=-1, keepdims=True),)


def _bd_fanin(xs, ws, *, name):
    nw = len(ws)
    nb, kb, nn = ws[0].shape
    T = xs[0].shape[0]
    tm = _pick(T, 512, 16)

    def body(*refs):
        acc = None
        for x_ref, w_ref in zip(refs[:nw], refs[nw:2 * nw]):
            t = jnp.dot(x_ref[...].astype(BF16), w_ref[0].astype(BF16), preferred_element_type=F32)
            acc = t if acc is None else acc + t
        refs[2 * nw][...] = acc

    return pl.pallas_call(
        body, name=name, grid=(nb, T // tm),
        in_specs=[pl.BlockSpec((tm, kb), lambda j, i: (i, j))] * nw + [pl.BlockSpec((1, kb, nn), lambda j, i: (j, 0, 0))] * nw,
        out_specs=pl.BlockSpec((tm, nn), lambda j, i: (i, j)),
        out_shape=jax.ShapeDtypeStruct((T, nb * nn), F32),
        compiler_params=_cparams(("parallel", "parallel")),
    )(*xs, *ws)


def _bd_dw(xs, dys, nb, *, name):
    npair = len(xs)
    T = xs[0].shape[0]
    kb = xs[0].shape[1] // nb
    nn = dys[0].shape[1] // nb
    tm = _pick(T, 512, 16)
    dims = (((0,), (0,)), ((), ()))

    def body(*refs):
        i = pl.program_id(1)
        for x_ref, d_ref, o_ref in zip(refs[:npair], refs[npair:2 * npair], refs[2 * npair:]):
            @pl.when(i == 0)
            def _(o_ref=o_ref):
                o_ref[...] = jnp.zeros_like(o_ref)

            o_ref[0] += lax.dot_general(x_ref[...].astype(BF16), d_ref[...].astype(BF16), dims,
                                        preferred_element_type=F32)

    return list(pl.pallas_call(
        body, name=name, grid=(nb, T // tm),
        in_specs=[pl.BlockSpec((tm, kb), lambda j, i: (i, j))] * npair + [pl.BlockSpec((tm, nn), lambda j, i: (i, j))] * npair,
        out_specs=[pl.BlockSpec((1, kb, nn), lambda j, i: (j, 0, 0))] * npair,
        out_shape=[jax.ShapeDtypeStruct((nb, kb, nn), F32)] * npair,
        compiler_params=_cparams(("parallel", "arbitrary")),
    )(*xs, *dys))


def _cmul(ar, ai, br, bi):
    return ar * br - ai * bi, ar * bi + ai * br


def _cpow(lr, li, n):
    rr, ri = None, None
    br, bi = lr, li
    while n:
        if n & 1:
            rr, ri = (br, bi) if rr is None else _cmul(rr, ri, br, bi)
        n >>= 1
        if n:
            br, bi = _cmul(br, bi, br, bi)
    return rr, ri


SCAN_MM_ROWS = 512


def _s5_scan(x, w_re, w_im, lam_re, lam_im, h0_re, h0_im, e0_re, e0_im, *, reverse, name):
    rows = x.shape[0]
    nb, kb, cb = w_re.shape
    C = nb * cb
    n = rows // N_SEG
    mm_rows = _pick(rows, SCAN_MM_ROWS, 16)
    seg_order = list(range(N_SEG))[::-1] if reverse else list(range(N_SEG))
    s_first, s_last = seg_order[0], seg_order[-1]

    def body(x_ref, wr_ref, wi_ref, lr_ref, li_ref, h0r_ref, h0i_ref, e0r_ref, e0i_ref, hr_ref, hi_ref, htr_ref, hti_ref,
             locr_ref, loci_ref):
        shape = (N_SEG, cb)
        lr = jnp.broadcast_to(lr_ref[...], shape)
        li = jnp.broadcast_to(li_ref[...], shape)
        row = lax.broadcasted_iota(jnp.int32, shape, 0)

        def step_of(k):
            return (n - 1 - k) if reverse else k

        def rows_of(k):
            return pl.ds(pl.multiple_of(step_of(k) * N_SEG, N_SEG), N_SEG)

        wr, wi = wr_ref[...].astype(BF16), wi_ref[...].astype(BF16)
        for r0 in range(0, rows, mm_rows):
            xb = x_ref[r0:r0 + mm_rows, :].astype(BF16)
            locr_ref[r0:r0 + mm_rows, :] = jnp.dot(xb, wr, preferred_element_type=F32)
            loci_ref[r0:r0 + mm_rows, :] = jnp.dot(xb, wi, preferred_element_type=F32)

        first = row == s_first
        hr = locr_ref[rows_of(0), :] + jnp.where(first, e0r_ref[...], 0.0)
        hi = loci_ref[rows_of(0), :] + jnp.where(first, e0i_ref[...], 0.0)
        locr_ref[rows_of(0), :] = hr
        loci_ref[rows_of(0), :] = hi

        def pass1(k, carry):
            hr, hi = carry
            pr, pi = _cmul(lr, li, hr, hi)
            hr = pr + locr_ref[rows_of(k), :]
            hi = pi + loci_ref[rows_of(k), :]
            locr_ref[rows_of(k), :] = hr
            loci_ref[rows_of(k), :] = hi
            return hr, hi

        er, ei = lax.fori_loop(1, n, pass1, (hr, hi))

        lnr, lni = _cpow(lr_ref[...], li_ref[...], n)
        cr, ci = h0r_ref[...], h0i_ref[...]
        cin_r = jnp.zeros(shape, F32)
        cin_i = jnp.zeros(shape, F32)
        for s in seg_order:
            cin_r = jnp.where(row == s, cr, cin_r)
            cin_i = jnp.where(row == s, ci, cin_i)
            if s != s_last:
                pr, pi = _cmul(lnr, lni, cr, ci)
                cr = pr + jnp.sum(jnp.where(row == s, er, 0.0), axis=0, keepdims=True)
                ci = pi + jnp.sum(jnp.where(row == s, ei, 0.0), axis=0, keepdims=True)

        def pass2(k, carry):
            pr, pi, _, _ = carry
            ar, ai = _cmul(pr, pi, cin_r, cin_i)
            hr = locr_ref[rows_of(k), :] + ar
            hi = loci_ref[rows_of(k), :] + ai
            hr_ref[rows_of(k), :] = hr.astype(hr_ref.dtype)
            hi_ref[rows_of(k), :] = hi.astype(hi_ref.dtype)
            npr, npi = _cmul(pr, pi, lr, li)
            return npr, npi, hr, hi

        _, _, last_r, last_i = lax.fori_loop(0, n, pass2, (lr, li, er, ei))
        htr_ref[...] = jnp.sum(jnp.where(row == s_last, last_r, 0.0), axis=0, keepdims=True)
        hti_ref[...] = jnp.sum(jnp.where(row == s_last, last_i, 0.0), axis=0, keepdims=True)

    big = pl.BlockSpec((rows, cb), lambda j: (0, j))
    vec = pl.BlockSpec((1, cb), lambda j: (0, j))
    wspec = pl.BlockSpec((None, kb, cb), lambda j: (j, 0, 0))
    return pl.pallas_call(
        body, name=name, grid=(nb,),
        in_specs=[pl.BlockSpec((rows, kb), lambda j: (0, j)), wspec, wspec] + [vec] * 6,
        out_specs=[big, big, vec, vec],
        out_shape=[jax.ShapeDtypeStruct((rows, C), BF16)] * 2 + [jax.ShapeDtypeStruct((1, C), F32)] * 2,
        scratch_shapes=[pltpu.VMEM((rows, cb), F32)] * 2,
        compiler_params=_cparams(("parallel",)),
    )(x, w_re, w_im, lam_re, lam_im, h0_re, h0_im, e0_re, e0_im)


def _s5_dlam(mu_re, mu_im, h_re, h_im, h0_re, h0_im, *, reverse, name):
    rows, C = h_re.shape
    n = rows // N_SEG
    cb = _pick(C, 256, LANES)
    s_first = N_SEG - 1 if reverse else 0

    def body(mr_ref, mi_ref, hr_ref, hi_ref, h0r_ref, h0i_ref, dr_ref, di_ref):
        shape = (N_SEG, cb)
        row = lax.broadcasted_iota(jnp.int32, shape, 0)

        def rows_of(k):
            step = (n - 1 - k) if reverse else k
            return pl.ds(pl.multiple_of(step * N_SEG, N_SEG), N_SEG)

        def term(k, pr, pi):
            mr, mi = mr_ref[rows_of(k), :].astype(F32), mi_ref[rows_of(k), :].astype(F32)
            return mr * pr + mi * pi, mi * pr - mr * pi

        shift = N_SEG - 1 if reverse else 1
        pr = jnp.where(row == s_first, h0r_ref[...], pltpu.roll(hr_ref[rows_of(n - 1), :].astype(F32), shift, 0))
        pi = jnp.where(row == s_first, h0i_ref[...], pltpu.roll(hi_ref[rows_of(n - 1), :].astype(F32), shift, 0))
        acc = term(0, pr, pi)

        def loop(k, acc):
            tr, ti = term(k, hr_ref[rows_of(k - 1), :].astype(F32), hi_ref[rows_of(k - 1), :].astype(F32))
            return acc[0] + tr, acc[1] + ti

        ar, ai = lax.fori_loop(1, n, loop, acc)
        dr_ref[...] = jnp.sum(ar, axis=0, keepdims=True)
        di_ref[...] = jnp.sum(ai, axis=0, keepdims=True)

    big = pl.BlockSpec((rows, cb), lambda j: (0, j))
    vec = pl.BlockSpec((1, cb), lambda j: (0, j))
    return pl.pallas_call(
        body, name=name, grid=(C // cb,),
        in_specs=[big] * 4 + [vec] * 2, out_specs=[vec, vec],
        out_shape=[jax.ShapeDtypeStruct((1, C), F32)] * 2,
        compiler_params=_cparams(("parallel",)),
    )(mu_re, mu_im, h_re, h_im, h0_re, h0_im)


NT_DIMS = (((1,), (1,)), ((), ()))
TN_DIMS = (((0,), (0,)), ((), ()))


ATTN_Q_ROWS = 512


def _attn_exp(q, kvh, kr):
    s = (lax.dot_general(q[:, :LANES], kvh[:, :LANES], NT_DIMS, preferred_element_type=F32)
         + lax.dot_general(q[:, LANES:], kr, NT_DIMS, preferred_element_type=F32))
    e = jnp.exp2((s - jnp.max(s, axis=-1, keepdims=True)) * (ATTN_SCALE * math.log2(math.e)))
    return e, jnp.sum(e, axis=-1, keepdims=True)


def _attn_specs(L, T, tq):
    return [
        pl.BlockSpec((tq, 2 * LANES), lambda h, i: (i, h)),
        pl.BlockSpec((T, 2 * LANES), lambda h, i: (0, h)),
        pl.BlockSpec((T, LANES), lambda h, i: (0, 0)),
    ]


def _attn_fwd(qq, kv, kr, *, name):
    L, T = qq.shape[0], kv.shape[0]
    tq = _pick(L, ATTN_Q_ROWS // 2, 16)

    def body(q_ref, kv_ref, kr_ref, o_ref):
        kvh = kv_ref[...]
        e, l = _attn_exp(q_ref[...], kvh, kr_ref[...])
        o_ref[...] = (jnp.dot(e.astype(BF16), kvh[:, LANES:], preferred_element_type=F32) * (1.0 / l)).astype(o_ref.dtype)

    return pl.pallas_call(
        body, name=name, grid=(MLA_HEADS, L // tq), in_specs=_attn_specs(L, T, tq),
        out_specs=pl.BlockSpec((tq, LANES), lambda h, i: (i, h)),
        out_shape=jax.ShapeDtypeStruct((L, MLA_HEADS * V_DIM), BF16),
        compiler_params=_cparams(("parallel", "parallel")),
    )(qq, kv, kr)


def _attn_bwd(qq, kv, kr, do, *, name):
    L, T = qq.shape[0], kv.shape[0]
    H = MLA_HEADS
    tq = _pick(L, ATTN_Q_ROWS, 16)
    nq = L // tq

    def body(q_ref, kv_ref, kr_ref, do_ref, dq_ref, dkv_ref, dkr_ref, dkn_acc, dv_acc):
        h, i = pl.program_id(0), pl.program_id(1)
        q, kvh, krv, dov = q_ref[...], kv_ref[...], kr_ref[...], do_ref[...]
        e, l = _attn_exp(q, kvh, krv)
        inv = 1.0 / l
        ps = e * (inv * ATTN_SCALE)
        t = lax.dot_general(dov, kvh[:, LANES:], NT_DIMS, preferred_element_type=F32) * ps
        ds = (t - ps * (jnp.sum(t, axis=-1, keepdims=True) * (1.0 / ATTN_SCALE))).astype(BF16)
        dq_ref[:, :LANES] = jnp.dot(ds, kvh[:, :LANES], preferred_element_type=F32)
        dq_ref[:, LANES:] = jnp.dot(ds, krv, preferred_element_type=F32)

        @pl.when(i == 0)
        def _():
            dkn_acc[...] = jnp.zeros_like(dkn_acc)
            dv_acc[...] = jnp.zeros_like(dv_acc)

        @pl.when((i == 0) & (h == 0))
        def _():
            dkr_ref[...] = jnp.zeros_like(dkr_ref)

        dv_acc[...] += lax.dot_general(e.astype(BF16), (dov.astype(F32) * inv).astype(BF16), TN_DIMS,
                                       preferred_element_type=F32)
        dkn_acc[...] += lax.dot_general(ds, q[:, :LANES], TN_DIMS, preferred_element_type=F32)
        dkr_ref[...] += lax.dot_general(ds, q[:, LANES:], TN_DIMS, preferred_element_type=F32)

        @pl.when(i == nq - 1)
        def _():
            dkv_ref[:, :LANES] = dkn_acc[...].astype(dkv_ref.dtype)
            dkv_ref[:, LANES:] = dv_acc[...].astype(dkv_ref.dtype)

    in_specs = _attn_specs(L, T, tq) + [pl.BlockSpec((tq, LANES), lambda h, i: (i, h))]
    return pl.pallas_call(
        body, name=name, grid=(H, L // tq), in_specs=in_specs,
        out_specs=[pl.BlockSpec((tq, 2 * LANES), lambda h, i: (i, h)), pl.BlockSpec((T, 2 * LANES), lambda h, i: (0, h)),
                   pl.BlockSpec((T, LANES), lambda h, i: (0, 0))],
        out_shape=[jax.ShapeDtypeStruct((L, H * 2 * LANES), F32), jax.ShapeDtypeStruct((T, H * 2 * LANES), BF16),
                   jax.ShapeDtypeStruct((T, LANES), F32)],
        scratch_shapes=[pltpu.VMEM((T, LANES), F32), pltpu.VMEM((T, LANES), F32)],
        compiler_params=_cparams(("arbitrary", "arbitrary")),
    )(qq, kv, kr, do)


def _adamw(w, g, m, v, *, name, anchor=None):
    c1 = 1.0 - ADAM_B1 ** ADAM_STEP
    c2 = 1.0 - ADAM_B2 ** ADAM_STEP

    def f(w, g, m, v):
        m = ADAM_B1 * m + (1.0 - ADAM_B1) * g
        v = ADAM_B2 * v + (1.0 - ADAM_B2) * jnp.square(g)
        delta = -ADAM_LR * ((m / c1) / (jnp.sqrt(v / c2) + ADAM_EPS) + ADAM_WD * w)
        return g, delta, m, v

    return _rw(f, [w, g, m, v], [], [F32] * 4, name=name, anchor=anchor)


def _slab_rows(rows, cols, n_arrays):
    return _pick(rows, max(16, (8 * 1024 * 1024) // (cols * 4 * n_arrays)), 16)


def _scalars(*vals):
    return jnp.stack([jnp.asarray(v, jnp.int32) for v in vals])


def _into_slot(src, slot, nslots, dtype, *, name):
    R, C = src.shape
    tr = _slab_rows(R, C, 2)

    def body(s_ref, x_ref, o_ref):
        o_ref[...] = x_ref[...].astype(o_ref.dtype)

    return pl.pallas_call(
        body, name=name,
        grid_spec=pltpu.PrefetchScalarGridSpec(
            num_scalar_prefetch=1, grid=(R // tr,),
            in_specs=[pl.BlockSpec((tr, C), lambda i, s: (i, 0))],
            out_specs=pl.BlockSpec((None, tr, C), lambda i, s: (s[0], i, 0))),
        out_shape=jax.ShapeDtypeStruct((nslots, R, C), dtype),
        compiler_params=_cparams(("arbitrary",)),
    )(_scalars(slot), src)


def _pair_sum(g, got, c, *, name):
    _, R, C = g.shape
    hr = R // 2
    tr = _slab_rows(hr, C, 3)
    nblk = hr // tr

    def body(s_ref, g_ref, r_ref, o_ref):
        o_ref[...] = (g_ref[...].astype(F32) + r_ref[...].astype(F32)).astype(o_ref.dtype)

    return pl.pallas_call(
        body, name=name,
        grid_spec=pltpu.PrefetchScalarGridSpec(
            num_scalar_prefetch=1, grid=(4, nblk),
            in_specs=[pl.BlockSpec((None, tr, C), lambda j, i, s: (j, s[0] * nblk + i, 0)),
                      pl.BlockSpec((None, tr, C), lambda j, i, s: (j, i, 0))],
            out_specs=pl.BlockSpec((None, tr, C), lambda j, i, s: (j, i, 0))),
        out_shape=jax.ShapeDtypeStruct((4, hr, C), g.dtype),
        compiler_params=_cparams(("arbitrary", "arbitrary")),
    )(_scalars(c), g, got)


def _chip_sum(p, landed, me_chip, c, *, name):
    _, hr, C = p.shape
    tr = _slab_rows(hr, C, 5)

    def body(s_ref, p_ref, l0_ref, l1_ref, l2_ref, o_ref):
        o_ref[...] = ((p_ref[...].astype(F32) + l0_ref[...].astype(F32)) + l1_ref[...].astype(F32)) + l2_ref[...].astype(F32)

    return pl.pallas_call(
        body, name=name,
        grid_spec=pltpu.PrefetchScalarGridSpec(
            num_scalar_prefetch=1, grid=(hr // tr,),
            in_specs=[pl.BlockSpec((None, tr, C), lambda i, s: (s[0], i, 0))]
            + [pl.BlockSpec((None, tr, C), functools.partial(lambda i, s, k: (k, i, 0), k=k)) for k in range(3)],
            out_specs=pl.BlockSpec((None, tr, C), lambda i, s: (s[1], i, 0))),
        out_shape=jax.ShapeDtypeStruct((2, hr, C), F32),
        compiler_params=_cparams(("arbitrary",)),
    )(_scalars(me_chip, c), p, landed, landed, landed)


def _place():
    return lax.axis_index("x"), lax.axis_index("y"), lax.axis_index("c")


def _other_chips(x, y):
    chips = [(1 - x, y), (x, 1 - y), (1 - x, 1 - y)]
    return chips, [2 * cx + cy for cx, cy in chips]


HBM = pl.BlockSpec(memory_space=pl.ANY)


def _allgather8(v, *, name):
    rows, cols = v.shape

    def body(v_ref, out_ref, send_sems, recv_sems):
        x, y, c = _place()
        me = 4 * x + 2 * y + c
        out_ref[me] = v_ref[...]
        copies = []
        for k in range(1, 8):
            bx, by, bc = (k >> 2) & 1, (k >> 1) & 1, k & 1
            px, py, pc = x ^ bx, y ^ by, c ^ bc
            cp = pltpu.make_async_remote_copy(
                src_ref=v_ref, dst_ref=out_ref.at[me], send_sem=send_sems.at[k - 1], recv_sem=recv_sems.at[k - 1],
                device_id=(px, py, pc), device_id_type=MESH)
            cp.start()
            copies.append((cp, 4 * px + 2 * py + pc))
        for k, (cp, peer) in enumerate(copies):
            pltpu.make_async_remote_copy(
                src_ref=v_ref, dst_ref=out_ref.at[peer], send_sem=send_sems.at[k], recv_sem=recv_sems.at[k],
                device_id=(x, y, c), device_id_type=MESH).wait_recv()
        for cp, _ in copies:
            cp.wait_send()

    return pl.pallas_call(
        body, name=name, out_shape=jax.ShapeDtypeStruct((8, rows, cols), v.dtype),
        in_specs=[pl.BlockSpec(memory_space=pltpu.VMEM)], out_specs=pl.BlockSpec(memory_space=pltpu.VMEM),
        scratch_shapes=[pltpu.SemaphoreType.DMA((7,)), pltpu.SemaphoreType.DMA((7,))],
        compiler_params=pltpu.CompilerParams(vmem_limit_bytes=VMEM_LIMIT),
    )(v)


def _allgather_shards(bufs, *, name):
    n = len(bufs)

    def body(*refs):
        outs = refs[n:2 * n]
        send_sems, recv_sems = refs[2 * n:]
        x, y, c = _place()
        me_chip = 2 * x + y
        sibling = (x, y, 1 - c)
        chips, chip_ids = _other_chips(x, y)

        def remote(k, j, blk, hf, to):
            hr = bufs[k].shape[1] // 2
            piece = outs[k].at[blk, pl.ds(pl.multiple_of(hf * hr, 16), hr), :]
            return pltpu.make_async_remote_copy(
                src_ref=piece, dst_ref=piece, send_sem=send_sems.at[6 * k + j], recv_sem=recv_sems.at[6 * k + j],
                device_id=to, device_id_type=MESH)

        sends = []
        for k in range(n):
            for j, chip in enumerate(chips):
                cp = remote(k, j, me_chip, c, (*chip, c))
                cp.start()
                sends.append(cp)
        for k in range(n):
            for j, chip in enumerate(chips):
                remote(k, j, chip_ids[j], c, (x, y, c)).wait_recv()
                cp = remote(k, 3 + j, chip_ids[j], c, sibling)
                cp.start()
                sends.append(cp)
        for k in range(n):
            for j in range(3):
                remote(k, 3 + j, chip_ids[j], 1 - c, (x, y, c)).wait_recv()
        for cp in sends:
            cp.wait_send()

    return list(pl.pallas_call(
        body, name=name, out_shape=[jax.ShapeDtypeStruct(b.shape, b.dtype) for b in bufs],
        in_specs=[HBM] * n, out_specs=[HBM] * n, input_output_aliases={k: k for k in range(n)},
        scratch_shapes=[pltpu.SemaphoreType.DMA((6 * n,)), pltpu.SemaphoreType.DMA((6 * n,))],
    )(*bufs))


HBM_SPEC = pl.BlockSpec(memory_space=pltpu.HBM)
SEM_SPEC = pl.BlockSpec(memory_space=pltpu.SEMAPHORE)
EFFECT = pltpu.SideEffectType.DATAFLOW_SIDE_EFFECTING
TOKEN = jax.ShapeDtypeStruct((SUBLANES, LANES), F32)


def _in_hbm(a):
    return pltpu.with_memory_space_constraint(a, pltpu.HBM)


def _half_rows(buf, hf):
    hr = buf.shape[1] // 2
    return pl.ds(pl.multiple_of(hf * hr, 16), hr)


def _plan_ag_ici(refs):
    x, y, c = _place()
    chips, ids = _other_chips(x, y)
    out = []
    for r in refs:
        mine = r.at[2 * x + y, _half_rows(r, c), :]
        out += [(mine, mine, r.at[ids[j], _half_rows(r, c), :], (*chip, c)) for j, chip in enumerate(chips)]
    return out


def _plan_ag_pair(refs):
    x, y, c = _place()
    _, ids = _other_chips(x, y)
    out = []
    for r in refs:
        for j in range(3):
            piece = r.at[ids[j], _half_rows(r, c), :]
            out.append((piece, piece, r.at[ids[j], _half_rows(r, 1 - c), :], (x, y, 1 - c)))
    return out


def _plan_rs_ici(refs):
    x, y, c = _place()
    chips, ids = _other_chips(x, y)
    n = len(refs) // 2
    return [(refs[k].at[ids[j]], refs[n + k].at[j], refs[n + k].at[j], (*chip, c))
            for k in range(n) for j, chip in enumerate(chips)]


def _plan_pair_exchange(refs):
    x, y, c = _place()
    n = len(refs) // 2
    return [(refs[k].at[:, _half_rows(refs[k], 1 - c), :], refs[n + k], refs[n + k], (x, y, 1 - c)) for k in range(n)]


def _plan_pair_gather(refs):
    x, y, c = _place()
    return [(r.at[c], r.at[c], r.at[1 - c], (x, y, 1 - c)) for r in refs]


def _remote(src, dst, send_sem, recv_sem, target):
    return pltpu.make_async_remote_copy(src_ref=src, dst_ref=dst, send_sem=send_sem, recv_sem=recv_sem,
                                        device_id=target, device_id_type=MESH)


def _copy_start(groups, *, name, after=()):
    flat = [a for arrays, _, _ in groups for a in arrays]
    n, ng = len(flat), len(groups)
    after = list(after)
    n_in = n + len(after)

    def body(*refs):
        sems = refs[n_in:n_in + 2 * ng]
        thru = refs[n_in + 2 * ng:n_in + 2 * ng + n]
        token = refs[-1]
        pos = 0
        for g, (arrays, plan, n_copies) in enumerate(groups):
            copies = plan(thru[pos:pos + len(arrays)])
            pos += len(arrays)
            assert len(copies) == n_copies
            for i, (src, dst, _, target) in enumerate(copies):
                _remote(src, dst, sems[2 * g].at[i], sems[2 * g + 1].at[i], target).start()
        token[...] = jnp.zeros_like(token)

    out_shape = tuple(pltpu.SemaphoreType.DMA((n_copies,)) for _, _, n_copies in groups for _ in range(2))
    out_shape += tuple(pltpu.HBM(a.shape, a.dtype) for a in flat) + (TOKEN,)
    res = pl.pallas_call(
        body, name=name, out_shape=out_shape,
        in_specs=(HBM_SPEC,) * n + (pl.BlockSpec(memory_space=pl.ANY),) * len(after),
        out_specs=(SEM_SPEC,) * (2 * ng) + (HBM_SPEC,) * n + (pl.BlockSpec(memory_space=pltpu.VMEM),),
        input_output_aliases={k: 2 * ng + k for k in range(n)},
        compiler_params=pltpu.CompilerParams(has_side_effects=EFFECT),
    )(*[_in_hbm(a) for a in flat], *after)
    sems = [(res[2 * g], res[2 * g + 1]) for g in range(ng)]
    thru, pos = [], 2 * ng
    for arrays, _, _ in groups:
        thru.append(list(res[pos:pos + len(arrays)]))
        pos += len(arrays)
    return sems, thru, res[-1]


def _copy_wait(arrays, sems, plan, n_copies, after, *, name):
    n = len(arrays)
    after = list(after)

    def body(*refs):
        send, recv = refs[n], refs[n + 1]
        x, y, c = _place()
        copies = plan(refs[:n])
        assert len(copies) == n_copies
        for i, (src, dst, landing, target) in enumerate(copies):
            _remote(src, dst, send.at[i], recv.at[i], target).wait_send()
            _remote(landing, landing, send.at[i], recv.at[i], (x, y, c)).wait_recv()

    return list(pl.pallas_call(
        body, name=name, out_shape=tuple(pltpu.HBM(a.shape, a.dtype) for a in arrays),
        in_specs=(HBM_SPEC,) * n + (SEM_SPEC, SEM_SPEC) + (pl.BlockSpec(memory_space=pl.ANY),) * len(after),
        out_specs=(HBM_SPEC,) * n, input_output_aliases={k: k for k in range(n)},
        compiler_params=pltpu.CompilerParams(has_side_effects=EFFECT),
    )(*arrays, *sems, *after))


def _rs_stage1(gs, tag, after=()):
    n = len(gs)
    lands = [lax.empty((4, g.shape[1] // 2, g.shape[2]), g.dtype) for g in gs]
    sems, (arrays,), token = _copy_start([(list(gs) + lands, _plan_pair_exchange, n)], name=f"rs_pair_start_{tag}",
                                         after=after)
    return (sems[0], arrays), token


def _rs_stage2(handle, after, tag):
    sems, arrays = handle
    n = len(arrays) // 2
    arrays = _copy_wait(arrays, sems, _plan_pair_exchange, n, after, name=f"rs_pair_wait_{tag}")
    c = lax.axis_index("c")
    pair = [_pair_sum(g, r, c, name=f"rs_pair_sum_{tag}{k}") for k, (g, r) in enumerate(zip(arrays[:n], arrays[n:]))]
    lands = [lax.empty((3,) + p.shape[1:], p.dtype) for p in pair]
    sems, (arrays,), token = _copy_start([(pair + lands, _plan_rs_ici, 3 * n)], name=f"rs_start_{tag}")
    return (sems[0], arrays), token


def _rs_stage3(handle, after, tag):
    sems, arrays = handle
    n = len(arrays) // 2
    arrays = _copy_wait(arrays, sems, _plan_rs_ici, 3 * n, after, name=f"rs_wait_{tag}")
    x, y, c = _place()
    halves = [_chip_sum(p, l, 2 * x + y, c, name=f"rs_chip_sum_{tag}{k}") for k, (p, l) in enumerate(zip(arrays[:n], arrays[n:]))]
    sems, (halves,), token = _copy_start([(halves, _plan_pair_gather, n)], name=f"rs_gather_start_{tag}")
    return (sems[0], halves), token


def _rs_stage4(handle, after, tag):
    sems, halves = handle
    full = _copy_wait(halves, sems, _plan_pair_gather, len(halves), after, name=f"rs_gather_wait_{tag}")
    return [f.reshape(2 * f.shape[1], f.shape[2]) for f in full]


def _to_segments(a):
    rows = a.shape[0]
    return a.reshape(N_SEG, rows // N_SEG, -1).transpose(1, 0, 2).reshape(rows, -1)


def _from_segments(a):
    rows = a.shape[0]
    return a.reshape(rows // N_SEG, N_SEG, -1).transpose(1, 0, 2).reshape(rows, -1)


def _rope_tables(L):
    t = jnp.arange(L, dtype=jnp.int32)
    row = (t // GRID_W).astype(F32)
    col = (t % GRID_W).astype(F32)
    n_freq = QK_ROPE // 4
    inv = ROPE_BASE ** (-jnp.arange(n_freq, dtype=F32) / n_freq)
    a0, a1 = row[:, None] * inv, col[:, None] * inv
    z = jnp.zeros((L, LANES - QK_ROPE), F32)
    cos = jnp.concatenate([jnp.cos(a0), jnp.cos(a0), jnp.cos(a1), jnp.cos(a1), z], axis=1)
    sin = jnp.concatenate([-jnp.sin(a0), jnp.sin(a0), -jnp.sin(a1), jnp.sin(a1), z], axis=1)
    return _to_segments(cos), _to_segments(sin)


def _col_blocks(w, nblk):
    r, c = w.shape
    return w.reshape(r, nblk, c // nblk).transpose(1, 0, 2)


def _from_col_blocks(w4):
    nblk, r, c = w4.shape
    return w4.transpose(1, 0, 2).reshape(r, nblk * c)


def _s5_discretize(a_re, a_im, log_dt, b_re, b_im):
    dt = jnp.exp(log_dt)[:, None]
    mag = jnp.exp(a_re * dt)
    ab_re, ab_im = mag * jnp.cos(a_im * dt), mag * jnp.sin(a_im * dt)
    den = a_re * a_re + a_im * a_im
    nr, ni = ab_re - 1.0, ab_im
    co_re = (nr * a_re + ni * a_im) / den
    co_im = (ni * a_re - nr * a_im) / den
    bb_re = co_re[..., None] * b_re - co_im[..., None] * b_im
    bb_im = co_re[..., None] * b_im + co_im[..., None] * b_re
    return ab_re, ab_im, bb_re, bb_im


def _diag_blocks_in(bb, gpb):
    G, N, P = bb.shape
    t = jnp.tile(jnp.swapaxes(bb, 1, 2).reshape(G // gpb, gpb * P, N), (1, 1, gpb))
    row = lax.broadcasted_iota(jnp.int32, t.shape, 1) // P
    col = lax.broadcasted_iota(jnp.int32, t.shape, 2) // N
    return jnp.where(row == col, t, 0.0)


def _diag_blocks_out(cc, gpb):
    G, P, N = cc.shape
    t = jnp.tile(jnp.swapaxes(cc, 1, 2).reshape(G // gpb, gpb * N, P), (1, 1, gpb))
    row = lax.broadcasted_iota(jnp.int32, t.shape, 1) // N
    col = lax.broadcasted_iota(jnp.int32, t.shape, 2) // P
    return jnp.where(row == col, t, 0.0)


def _tr(ws):
    return [jnp.swapaxes(w, 1, 2) for w in ws]


WEIGHTS = ['c_ctx', 'w_mod', 'b_mod', 'norm1', 'norm2', 'w_in', 's5_a_re', 's5_a_im', 's5_log_dt', 's5_b_re', 's5_b_im',
           's5_c_re', 's5_c_im', 's5_d', 'w_glu', 'q_norm', 'kv_norm', 'w_uq', 'w_ukv', 'w_mla_o', 'w_out', 'w_ffn_in',
           'w_ffn_out', 'norm_f']
AG_GROUPS = [['w_in'], ['w_glu', 'w_uq', 'w_ukv', 'w_mla_o', 'w_out'], ['w_ffn_in', 'w_ffn_out']]
SMALL = ['norm1', 'norm2', 's5_a_re', 's5_a_im', 's5_log_dt', 's5_b_re', 's5_b_im', 's5_c_re', 's5_c_im', 's5_d',
         'q_norm', 'kv_norm', 'norm_f']


def _pad_rows(a, rows):
    return jnp.concatenate([a, jnp.zeros((rows - a.shape[0],) + a.shape[1:], a.dtype)], axis=0)


def _pack(vals, width, rows):
    flat = jnp.concatenate([v.reshape(-1).astype(F32) for v in vals])
    flat = jnp.concatenate([flat, jnp.zeros((rows * width - flat.shape[0],), F32)])
    return flat.reshape(rows, width)


def _unpack(buf, like):
    flat = buf.reshape(-1)
    out, pos = [], 0
    for v in like:
        out.append(flat[pos:pos + v.size].reshape(v.shape))
        pos += v.size
    return out


def _step(x, c, ctx, loss_target, w, m, v):
    px, py, pc = _place()
    me = 4 * px + 2 * py + pc
    me_chip = 2 * px + py
    L, D = x.shape[1], x.shape[2]
    Lc = ctx.shape[1]
    T = L + Lc
    SW = D // 2
    G = SW // S5_GROUP
    C = G * S5_STATE
    H = MLA_HEADS
    q_rank = w['q_norm'].shape[1]
    kv_rank = w['kv_norm'].shape[1]
    d_ff = w['w_ffn_out'].shape[1] * 4
    wa_used = SW + q_rank + kv_rank + QK_ROPE
    WA = -(-(SW + q_rank + kv_rank + LANES) // 512) * 512

    c_rows = _pad_rows(c.astype(F32), SUBLANES)
    c_all = _allgather8(c_rows, name="ag_cond")[:, 0, :]
    cond = jnp.concatenate([c_all, w['c_ctx'].reshape(1, D)], axis=0)
    cond = _pad_rows(cond, 16)
    (act,) = _rw(lambda t: (jax.nn.silu(t),), [cond], [], [F32], name="cond_silu")
    w_mod, cs_mod = w['w_mod'][0], w['w_mod'].shape[2]
    mod_part = _mm(act, w_mod, out_dtype=F32, name="mod_fwd")
    mod_all = _allgather8(mod_part, name="ag_mod")
    mod_full = jnp.concatenate([mod_all[0], mod_all[2], mod_all[4], mod_all[6]], axis=1) + w['b_mod']
    m_lat = lax.dynamic_slice_in_dim(mod_full, me, 1, axis=0).reshape(6, D)
    m_ctx = mod_full[8].reshape(6, D)
    sh1, sc1, g1, sh2, sc2, g2 = (m_lat[i:i + 1] for i in range(6))
    csh1, csc1 = m_ctx[0:1], m_ctx[1:2]

    ag_groups = [([_into_slot(w[nme][0], me_chip, 4, BF16, name=f"cast_{nme}") for nme in grp], _plan_ag_ici, 3 * len(grp))
                 for grp in AG_GROUPS]
    ag_sems, ag_bufs, ag_token = _copy_start(ag_groups, name="ag_start", after=[mod_full])
    gathered, ag_pair = {}, {}

    def landed(g, after):
        n_cp = 3 * len(AG_GROUPS[g])
        got = _copy_wait(ag_bufs[g], ag_sems[g], _plan_ag_ici, n_cp, after, name=f"ag_wait_{g}")
        sems, (got,), token = _copy_start([(got, _plan_ag_pair, n_cp)], name=f"ag_pair_start_{g}")
        ag_pair[g] = (sems[0], got)
        return token[0, 0]

    def arrive(g, after):
        sems, got = ag_pair[g]
        got = _copy_wait(got, sems, _plan_ag_pair, 3 * len(AG_GROUPS[g]), after, name=f"ag_pair_wait_{g}")
        gathered.update(zip(AG_GROUPS[g], got))

    xs = _to_segments(x[0])
    cs = _to_segments(ctx[0])
    tgt = _to_segments(loss_target[0])
    cos, sin = _rope_tables(L)
    n1, n2, nf = w['norm1'], w['norm2'], w['norm_f'].reshape(1, D)
    qg, kvg = w['q_norm'], w['kv_norm']

    (xn_lat,) = _rw(_f_norm_mod, [xs], [n1 + ag_token[0, 0], sc1, sh1], [BF16], name="norm1_lat")
    (xn_ctx,) = _rw(_f_norm_mod, [cs], [n1, csc1, csh1], [BF16], name="norm1_ctx")
    xn = jnp.concatenate([xn_lat, xn_ctx], axis=0)
    landed(0, [xn])

    gpb = min(S5_BLOCK_GROUPS, G)
    gpo = min(8, G)
    d_skip = w['s5_d'][0].reshape(1, SW)
    disc, vjp_disc, w_b, w_c = [], [], [], []
    for d in range(2):
        prm = (w['s5_a_re'][0, d], w['s5_a_im'][0, d], w['s5_log_dt'][0, d], w['s5_b_re'][0, d], w['s5_b_im'][0, d])

        def prep(a_re, a_im, log_dt, b_re, b_im):
            ab_re, ab_im, bb_re, bb_im = _s5_discretize(a_re, a_im, log_dt, b_re, b_im)
            return ab_re.reshape(1, C), ab_im.reshape(1, C), _diag_blocks_in(bb_re, gpb), _diag_blocks_in(bb_im, gpb)

        out, vj = jax.vjp(prep, *prm)
        disc.append(out)
        vjp_disc.append(vj)
        w_b += [out[2], out[3]]
        w_c += [_diag_blocks_out(w['s5_c_re'][0, d], gpo), -_diag_blocks_out(w['s5_c_im'][0, d], gpo)]
    nb_in = G // gpb
    nb_out = G // gpo

    arrive(0, [xn, tgt] + w_b + w_c)
    w_in = _from_col_blocks(gathered['w_in'])
    w_a = jnp.concatenate([w_in[:, :wa_used], jnp.zeros((D, WA - wa_used), BF16)], axis=1)
    w_g = w_in[:, wa_used:]
    ha = _mm(xn, w_a, out_dtype=F32, name="in_proj")
    ha_lat, ha_ctx = ha[:L], ha[L:]
    gt = _mm(xn_lat, w_g, out_dtype=F32, name="in_gates")
    f_post_lat = _make_f_post_in(SW, q_rank, kv_rank, True)
    f_post_ctx = _make_f_post_in(SW, q_rank, kv_rank, False)
    u_lat, cqn, ckvn_lat, kr_lat = _rw(f_post_lat, [ha_lat, cos, sin], [qg, kvg], [F32, BF16, BF16, BF16], name="post_in_lat")
    u_ctx, ckvn_ctx, kr_ctx = _rw(f_post_ctx, [ha_ctx], [kvg], [F32, BF16, BF16], name="post_in_ctx")
    zero = jnp.zeros((1, C), F32) + landed(1, [u_lat, u_ctx])

    h_lat, h_ctx, hT_ctx = [], [], []
    for d, rev in enumerate((False, True)):
        lr, li = disc[d][0], disc[d][1]
        hcr, hci, tr, ti = _s5_scan(u_ctx, w_b[2 * d], w_b[2 * d + 1], lr, li, zero, zero, zero, zero, reverse=rev,
                                    name=f"s5_scan_ctx_{d}")
        hlr, hli, _, _ = _s5_scan(u_lat, w_b[2 * d], w_b[2 * d + 1], lr, li, tr, ti, zero, zero, reverse=rev,
                                  name=f"s5_scan_lat_{d}")
        h_ctx += [hcr, hci]
        h_lat += [hlr, hli]
        hT_ctx += [tr, ti]
    r5 = _bd_fanin(h_lat, w_c, name="s5_readout")
    (z,) = _rw(_f_s5post, [u_lat, r5], [d_skip], [BF16], name="s5_post")

    arrive(1, [z])
    w_glu, w_ukv, w_mla_o = (gathered[nme] for nme in ('w_glu', 'w_ukv', 'w_mla_o'))
    w_out = gathered['w_out'].reshape(D, D)
    uq3 = _from_col_blocks(gathered['w_uq']).reshape(q_rank, H, QK_NOPE + QK_ROPE)
    w_q2 = jnp.concatenate([uq3, jnp.zeros((q_rank, H, LANES - QK_ROPE), BF16)], axis=2).reshape(q_rank, H * 2 * LANES)
    q2 = _mm(cqn, w_q2, out_dtype=F32, name="q_up")
    (qq,) = _rw(_f_qpost, [q2, cos, sin], [], [BF16], name="q_rope")
    kvn = jnp.concatenate([ckvn_lat, ckvn_ctx], axis=0)
    kr_all = jnp.concatenate([kr_lat, kr_ctx], axis=0)
    kv = _mm(kvn, w_ukv, b_shards=4, out_dtype=BF16, name="kv_up")
    kr_all = kr_all + landed(2, [kv, qq]).astype(BF16)
    o = _attn_fwd(qq, kv, kr_all, name="attn_fwd")

    ab = _mm(z, w_glu, b_shards=4, out_dtype=F32, name="glu_proj")
    bm = _mm(o, w_mla_o, b_shards=4, out_dtype=F32, name="mla_out")
    (mix,) = _rw(_f_merge, [ab, bm, gt], [], [BF16], name="merge")
    out1 = _mm(mix, w_out, out_dtype=F32, name="out_proj")
    x1, xn2 = _rw(_f_resid_norm, [xs, out1], [g1, n2, sc2, sh2], [F32, BF16], name="resid_norm2")
    arrive(2, [xn2])
    w_ffn_in = gathered['w_ffn_in']
    w_ffn_out = gathered['w_ffn_out'].reshape(d_ff, D)
    hmid, ab2 = _ffn_in_swiglu(xn2, w_ffn_in, name="ffn_in")
    f2 = _mm(hmid, w_ffn_out, out_dtype=F32, name="ffn_out")
    (row_loss,) = _rw(_f_final, [x1, f2, tgt], [g2, nf], [F32], name="final_loss")
    loss = lax.psum(jnp.sum(row_loss), ("x", "y", "c"))

    ones = jnp.ones((L, 1), F32)
    (dx1_a, df2), (dg2, dnf) = _rw_vjp(_f_final, [x1, f2, tgt], [g2, nf], [[ones]], [True, True, False], [True, True],
                                       [F32, BF16], name="final_loss_bwd")
    gw_ffn_out = _mm(hmid, df2, ta=True, out_dtype=BF16, name="ffn_out_dw")
    dab2 = _ffn_out_dx_swiglu(df2, w_ffn_out, ab2, name="ffn_out_dx")
    dxn2 = _mm(dab2, w_ffn_in, tb=True, a_shards=2, b_shards=4, out_dtype=F32, name="ffn_in_dx")
    gw_ffn_in = _mm(xn2, dab2, ta=True, b_shards=2, out_shards=4, out_dtype=BF16, name="ffn_in_dw")
    rs_ffn, tok = _rs_stage1([gw_ffn_out.reshape(4, -1, D), gw_ffn_in], "ffn")
    (dx_a, dout1), (dg1, dn2, dsc2, dsh2) = _rw_vjp(
        _f_resid_norm, [xs, out1], [g1, n2 + tok[0, 0], sc2, sh2], [[dx1_a], [dxn2]], [True, True], [True] * 4, [F32, BF16],
        name="resid_norm2_bwd")
    dmix = _mm(dout1, w_out, tb=True, out_dtype=F32, name="out_proj_dx")
    rs_ffn, tok = _rs_stage2(rs_ffn, [dmix], "ffn")
    gw_out = _mm(mix, dout1, ta=True, out_dtype=BF16, name="out_proj_dw")
    (dab, dbm, dgt), _ = _rw_vjp(_f_merge, [ab, bm, gt], [], [[dmix]], [True] * 3, [], [BF16] * 3, name="merge_bwd",
                                 anchor=tok)
    dz = _mm(dab, w_glu, tb=True, b_shards=4, out_dtype=F32, name="glu_proj_dx")
    gw_glu = _mm(z, dab, ta=True, out_shards=4, out_dtype=BF16, name="glu_proj_dw")
    do = _mm(dbm, w_mla_o, tb=True, b_shards=4, out_dtype=BF16, name="mla_out_dx")
    gw_mla_o = _mm(o, dbm, ta=True, out_shards=4, out_dtype=BF16, name="mla_out_dw")
    dxn_g = _mm(dgt, w_g, tb=True, out_dtype=F32, name="in_gates_dx")
    gw_g = _mm(xn_lat, dgt, ta=True, out_dtype=BF16, name="in_gates_dw")
    rs_mid, tok = _rs_stage1([gw_out.reshape(4, -1, D), gw_glu, gw_mla_o], "mid", after=[gw_g])

    (du_a, dr5), (dd_skip,) = _rw_vjp(_f_s5post, [u_lat, r5], [d_skip + tok[0, 0]], [[dz]], [True, True], [True], [F32, F32],
                                      name="s5_post_bwd")
    dw_c = _bd_dw(h_lat, [dr5] * 4, nb_out, name="s5_readout_dw")
    rs_mid, tok = _rs_stage2(rs_mid, dw_c[:1], "mid")
    zero = zero + tok[0, 0]
    w_ct = _tr(w_c)
    zeros_ctx = jnp.zeros((Lc, SW), BF16)
    mu_lat, mu_ctx, dlam = [], [], []
    for d, rev in enumerate((False, True)):
        lr, li = disc[d][0], disc[d][1]
        mlr, mli, fr, fi = _s5_scan(dr5, w_ct[2 * d], w_ct[2 * d + 1], lr, -li, zero, zero, zero, zero, reverse=not rev,
                                    name=f"s5_adj_lat_{d}")
        dh0r, dh0i = _cmul(lr, -li, fr, fi)
        mcr, mci, _, _ = _s5_scan(zeros_ctx, w_ct[2 * d], w_ct[2 * d + 1], lr, -li, zero, zero, dh0r, dh0i,
                                  reverse=not rev, name=f"s5_adj_ctx_{d}")
        dl_lat = _s5_dlam(mlr, mli, h_lat[2 * d], h_lat[2 * d + 1], hT_ctx[2 * d], hT_ctx[2 * d + 1], reverse=rev,
                          name=f"s5_dlam_lat_{d}")
        dl_ctx = _s5_dlam(mcr, mci, h_ctx[2 * d], h_ctx[2 * d + 1], zero, zero, reverse=rev, name=f"s5_dlam_ctx_{d}")
        mu_lat += [mlr, mli]
        mu_ctx += [mcr, mci]
        dlam.append((dl_lat[0] + dl_ctx[0], dl_lat[1] + dl_ctx[1]))
    du_b = _bd_fanin(mu_lat, _tr(w_b), name="s5_bu_lat_dx")
    du_ctx = _bd_fanin(mu_ctx, _tr(w_b), name="s5_bu_ctx_dx")
    dw_b_lat = _bd_dw([u_lat] * 4, mu_lat, nb_in, name="s5_bu_lat_dw")
    dw_b_ctx = _bd_dw([u_ctx] * 4, mu_ctx, nb_in, name="s5_bu_ctx_dw")
    g_s5 = {}
    for d in range(2):
        ct = (dlam[d][0], dlam[d][1], dw_b_lat[2 * d] + dw_b_ctx[2 * d], dw_b_lat[2 * d + 1] + dw_b_ctx[2 * d + 1])
        ga_re, ga_im, gdt, gb_re, gb_im = vjp_disc[d](ct)
        _, vj_c = jax.vjp(lambda cr, ci: (_diag_blocks_out(cr, gpo), -_diag_blocks_out(ci, gpo)),
                          w['s5_c_re'][0, d], w['s5_c_im'][0, d])
        gc_re, gc_im = vj_c((dw_c[2 * d], dw_c[2 * d + 1]))
        for nme, val in (('s5_a_re', ga_re), ('s5_a_im', ga_im), ('s5_log_dt', gdt), ('s5_b_re', gb_re),
                         ('s5_b_im', gb_im), ('s5_c_re', gc_re), ('s5_c_im', gc_im)):
            g_s5.setdefault(nme, []).append(val)
    g_small = {nme: jnp.stack(vals)[None] for nme, vals in g_s5.items()}
    g_small['s5_d'] = dd_skip.reshape(w['s5_d'].shape)

    dqq, dkv, dkr = _attn_bwd(qq, kv, kr_all, do, name="attn_bwd")
    (dq2,), _ = _rw_vjp(_f_qpost, [q2, cos, sin], [], [[dqq]], [True, False, False], [], [BF16], name="q_rope_bwd")
    dcqn = _mm(dq2, w_q2, tb=True, out_dtype=F32, name="q_up_dx")
    gw_q2 = _mm(cqn, dq2, ta=True, out_dtype=BF16, name="q_up_dw")
    dckvn = _mm(dkv, w_ukv, tb=True, b_shards=4, out_dtype=F32, name="kv_up_dx")
    gw_ukv = _mm(kvn, dkv, ta=True, out_shards=4, out_dtype=BF16, name="kv_up_dw")
    gw_uq = gw_q2.reshape(q_rank, H, 2 * LANES)[:, :, :QK_NOPE + QK_ROPE].reshape(q_rank, H * (QK_NOPE + QK_ROPE))
    rs_kv, tok = _rs_stage1([_col_blocks(gw_uq, 4), gw_ukv], "kv")

    (dha_lat,), (dqg, dkvg_lat) = _rw_vjp(
        f_post_lat, [ha_lat, cos, sin], [qg, kvg + tok[0, 0]], [[du_a, du_b], [dcqn], [dckvn[:L]], [dkr[:L]]],
        [True, False, False], [True, True], [BF16], name="post_in_lat_bwd")
    (dha_ctx,), (dkvg_ctx,) = _rw_vjp(f_post_ctx, [ha_ctx], [kvg], [[du_ctx], [dckvn[L:]], [dkr[L:]]], [True], [True],
                                      [BF16], name="post_in_ctx_bwd")
    dha = jnp.concatenate([dha_lat, dha_ctx], axis=0)
    dxn = _mm(dha, w_a, tb=True, out_dtype=F32, name="in_proj_dx")
    gw_a = _mm(xn, dha, ta=True, out_dtype=BF16, name="in_proj_dw")
    rs_kv, tok = _rs_stage2(rs_kv, [gw_a], "kv")
    (dx_seg,), (dn1_lat, dsc1, dsh1) = _rw_vjp(
        _f_norm_mod_keep, [xs], [n1 + tok[0, 0], sc1, sh1], [[dxn[:L], dxn_g], [dx_a]], [True], [True] * 3, [F32],
        name="norm1_lat_bwd")
    _, (dn1_ctx, dcsc1, dcsh1) = _rw_vjp(_f_norm_mod, [cs], [n1, csc1, csh1], [[dxn[L:]]], [False], [True] * 3, [],
                                         name="norm1_ctx_bwd")
    grad_x = _from_segments(dx_seg)[None]
    g_small.update(norm1=dn1_lat + dn1_ctx, norm2=dn2, q_norm=dqg, kv_norm=dkvg_lat + dkvg_ctx, norm_f=dnf.reshape(D))
    gw_in = jnp.concatenate([gw_a[:, :wa_used], gw_g], axis=1)
    small_vals = [g_small[nme] for nme in SMALL]
    n_small = sum(val.size for val in small_vals)
    small_rows = -(-n_small // (LANES * 4 * 32)) * 32

    zD = jnp.zeros((1, D), F32)
    dm = jnp.concatenate([
        jnp.concatenate([dsh1, dsc1, dg1, dsh2, dsc2, dg2], axis=1),
        jnp.concatenate([dcsh1, dcsc1, zD, zD, zD, zD], axis=1),
    ], axis=0)
    dm_all = _allgather8(_pad_rows(dm, SUBLANES), name="ag_dmod")
    rs_in, tok = _rs_stage1([_col_blocks(gw_in, 4), _pack(small_vals, LANES, 4 * small_rows).reshape(4, small_rows, LANES)],
                            "in", after=[dm_all])
    dm_ctx = dm_all[0, 1] + tok[0, 0]
    for k in range(1, 8):
        dm_ctx = dm_ctx + dm_all[k, 1]
    dmod = _pad_rows(jnp.concatenate([dm_all[:, 0, :], dm_ctx[None]], axis=0), 16)
    g_b_mod = jnp.sum(dmod, axis=0, keepdims=True)
    dmod_mine = lax.dynamic_slice_in_dim(dmod, me_chip * cs_mod, cs_mod, axis=1)
    g_w_mod = _mm(act, dmod_mine, ta=True, out_dtype=F32, name="mod_dw")
    dact_part = _mm(dmod_mine, w_mod, tb=True, out_dtype=F32, name="mod_dx")
    dact_all = _allgather8(dact_part, name="ag_dact")
    dact = dact_all[0] + dact_all[2] + dact_all[4] + dact_all[6]
    (dcond_rows,), _ = _rw_vjp(lambda t: (jax.nn.silu(t),), [cond], [], [[dact]], [True], [], [F32], name="cond_silu_bwd")
    g_c_ctx = dcond_rows[8]

    rs_in, tok = _rs_stage2(rs_in, [g_c_ctx], "in")

    grads, delta, new_m, new_v = {}, {}, {}, {}

    def update(members, reds, anchor):
        deltas = []
        for nme, red in zip(members, reds):
            res = _adamw(w[nme][0], red, m[nme][0], v[nme][0], name=f"adamw_{nme}", anchor=anchor)
            grads[nme], delta[nme], new_m[nme], new_v[nme] = (r.reshape(w[nme].shape) for r in res)
            deltas.append(res[1])
            anchor = None
        return deltas

    rs_ffn, tok = _rs_stage3(rs_ffn, [tok], "ffn")
    done = update(['w_mod'], [g_w_mod], tok)
    red_ffn = _rs_stage4(rs_ffn, done, "ffn")
    rs_mid, tok = _rs_stage3(rs_mid, red_ffn[:1], "mid")
    done = update(['w_ffn_out', 'w_ffn_in'], red_ffn, tok)
    red_mid = _rs_stage4(rs_mid, done, "mid")
    rs_kv, tok = _rs_stage3(rs_kv, red_mid[:1], "kv")
    done = update(['w_out', 'w_glu', 'w_mla_o'], red_mid, tok)
    red_kv = _rs_stage4(rs_kv, done, "kv")
    rs_in, tok = _rs_stage3(rs_in, red_kv[:1], "in")
    done = update(['w_uq', 'w_ukv'], red_kv, tok)
    red_in = _rs_stage4(rs_in, done, "in")
    update(['w_in'], red_in[:1], None)
    small_mine = red_in[-1]
    small_buf = _into_slot(small_mine, me_chip, 4, F32, name="small_grads_slot")
    small_all = _allgather_shards([small_buf], name="ag_small_grads")[0].reshape(4 * small_rows, LANES)
    g_small_red = dict(zip(SMALL, _unpack(small_all, [w[nme] for nme in SMALL])))
    rest = SMALL + ['c_ctx', 'b_mod']
    g_rest = dict(g_small_red, c_ctx=g_c_ctx, b_mod=g_b_mod)
    rows_rest = -(-sum(w[nme].size for nme in rest) // (LANES * 16)) * 16
    packed = [_pack([src[nme] for nme in rest], LANES, rows_rest) for src in (w, g_rest, m, v)]
    res = _adamw(*packed, name="adamw_small")
    for dst, buf in zip((grads, delta, new_m, new_v), res):
        dst.update(zip(rest, _unpack(buf, [w[nme] for nme in rest])))
    return (loss, grad_x, *[grads[nme] for nme in WEIGHTS], *[delta[nme] for nme in WEIGHTS],
            *[new_m[nme] for nme in WEIGHTS], *[new_v[nme] for nme in WEIGHTS])


def kernel(x, c, ctx, c_ctx, w_mod, b_mod, norm1, norm2, w_in, s5_a_re, s5_a_im, s5_log_dt, s5_b_re, s5_b_im, s5_c_re, s5_c_im, s5_d, w_glu, q_norm, kv_norm, w_uq, w_ukv, w_mla_o, w_out, w_ffn_in, w_ffn_out, norm_f, loss_target, m_c_ctx, m_w_mod, m_b_mod, m_norm1, m_norm2, m_w_in, m_s5_a_re, m_s5_a_im, m_s5_log_dt, m_s5_b_re, m_s5_b_im, m_s5_c_re, m_s5_c_im, m_s5_d, m_w_glu, m_q_norm, m_kv_norm, m_w_uq, m_w_ukv, m_w_mla_o, m_w_out, m_w_ffn_in, m_w_ffn_out, m_norm_f, v_c_ctx, v_w_mod, v_b_mod, v_norm1, v_norm2, v_w_in, v_s5_a_re, v_s5_a_im, v_s5_log_dt, v_s5_b_re, v_s5_b_im, v_s5_c_re, v_s5_c_im, v_s5_d, v_w_glu, v_q_norm, v_kv_norm, v_w_uq, v_w_ukv, v_w_mla_o, v_w_out, v_w_ffn_in, v_w_ffn_out, v_norm_f):
    w = dict(c_ctx=c_ctx, w_mod=w_mod, b_mod=b_mod, norm1=norm1, norm2=norm2, w_in=w_in, s5_a_re=s5_a_re, s5_a_im=s5_a_im,
             s5_log_dt=s5_log_dt, s5_b_re=s5_b_re, s5_b_im=s5_b_im, s5_c_re=s5_c_re, s5_c_im=s5_c_im, s5_d=s5_d, w_glu=w_glu,
             q_norm=q_norm, kv_norm=kv_norm, w_uq=w_uq, w_ukv=w_ukv, w_mla_o=w_mla_o, w_out=w_out, w_ffn_in=w_ffn_in,
             w_ffn_out=w_ffn_out, norm_f=norm_f)
    m = dict(c_ctx=m_c_ctx, w_mod=m_w_mod, b_mod=m_b_mod, norm1=m_norm1, norm2=m_norm2, w_in=m_w_in, s5_a_re=m_s5_a_re,
             s5_a_im=m_s5_a_im, s5_log_dt=m_s5_log_dt, s5_b_re=m_s5_b_re, s5_b_im=m_s5_b_im, s5_c_re=m_s5_c_re,
             s5_c_im=m_s5_c_im, s5_d=m_s5_d, w_glu=m_w_glu, q_norm=m_q_norm, kv_norm=m_kv_norm, w_uq=m_w_uq, w_ukv=m_w_ukv,
             w_mla_o=m_w_mla_o, w_out=m_w_out, w_ffn_in=m_w_ffn_in, w_ffn_out=m_w_ffn_out, norm_f=m_norm_f)
    v = dict(c_ctx=v_c_ctx, w_mod=v_w_mod, b_mod=v_b_mod, norm1=v_norm1, norm2=v_norm2, w_in=v_w_in, s5_a_re=v_s5_a_re,
             s5_a_im=v_s5_a_im, s5_log_dt=v_s5_log_dt, s5_b_re=v_s5_b_re, s5_b_im=v_s5_b_im, s5_c_re=v_s5_c_re,
             s5_c_im=v_s5_c_im, s5_d=v_s5_d, w_glu=v_w_glu, q_norm=v_q_norm, kv_norm=v_kv_norm, w_uq=v_w_uq, w_ukv=v_w_ukv,
             w_mla_o=v_w_mla_o, w_out=v_w_out, w_ffn_in=v_w_ffn_in, w_ffn_out=v_w_ffn_out, norm_f=v_norm_f)
    return _step(x, c, ctx, loss_target, w, m, v)
```

```python
import functools
import math

import jax
import jax.numpy as jnp
from jax import lax
from jax.experimental import pallas as pl
from jax.experimental.pallas import tpu as pltpu

F32 = jnp.float32
BF16 = jnp.bfloat16

EPS = 1e-6
GRID_W = 64
S5_GROUP = 16
S5_STATE = 64
MLA_HEADS = 8
QK_NOPE = 128
QK_ROPE = 64
V_DIM = 128
ROPE_BASE = 10000.0
ATTN_SCALE = (QK_NOPE + QK_ROPE) ** -0.5
ADAM_LR = 0.001
ADAM_B1 = 0.9
ADAM_B2 = 0.999
ADAM_EPS = 1e-08
ADAM_WD = 0.01
ADAM_STEP = 10

SUBLANES = 8
LANES = 128
V7X_VMEM_BYTES = 64 * 1024 * 1024
VMEM_LIMIT = (V7X_VMEM_BYTES * 7) // 8
N_SEG = 2 * SUBLANES
S5_BLOCK_GROUPS = 8
MESH = pl.DeviceIdType.MESH


def _pick(n, target, mult):
    best = None
    d = mult
    while d <= min(n, target):
        if n % d == 0:
            best = d
        d += mult
    return n if best is None else best


def _cparams(sem=None):
    return pltpu.CompilerParams(dimension_semantics=sem, vmem_limit_bytes=VMEM_LIMIT)


MM_VMEM_BUDGET = (V7X_VMEM_BYTES * 5) // 8


def _mm(a, b, *, ta=False, tb=False, out_dtype=F32, name, a_shards=1, b_shards=1, out_shards=1):
    if ta:
        K, M = a.shape
    else:
        M, K = a.shape[-2], a.shape[-1] * a_shards
    if tb:
        N, K2 = b.shape[-2], b.shape[-1] * b_shards
    else:
        K2, N = b.shape[-2], b.shape[-1] * b_shards
    assert K == K2, (a.shape, b.shape, ta, tb)
    n_unit = N // max(out_shards, 1 if tb else b_shards)
    k_unit = K // max(a_shards, b_shards if tb else 1)
    tn = _pick(n_unit, 1024, LANES)
    tm = _pick(M, 1024 if tn >= 512 else 2048, LANES if ta else 16)
    sa, sb, so = a.dtype.itemsize, b.dtype.itemsize, jnp.dtype(out_dtype).itemsize
    k_mult = LANES if (not ta or tb) else 16
    tk = k_mult if k_unit % k_mult == 0 else k_unit
    for cand in range(k_mult, k_unit + 1, k_mult):
        if k_unit % cand == 0 and 2 * cand * (tm * sa + tn * sb) + tm * tn * (4 + 2 * so) <= MM_VMEM_BUDGET:
            tk = cand
    nk = K // tk
    dims = (((0 if ta else 1,), (1 if tb else 0,)), ((), ()))

    def body(a_ref, b_ref, o_ref, *scratch):
        part = lax.dot_general(a_ref[...].astype(BF16), b_ref[...].astype(BF16), dims, preferred_element_type=F32)
        if nk == 1:
            o_ref[...] = part.astype(o_ref.dtype)
            return
        acc_ref, = scratch
        k = pl.program_id(2)

        @pl.when(k == 0)
        def _():
            acc_ref[...] = part

        @pl.when(k > 0)
        def _():
            acc_ref[...] += part

        @pl.when(k == nk - 1)
        def _():
            o_ref[...] = acc_ref[...].astype(o_ref.dtype)

    if ta:
        a_spec = pl.BlockSpec((tk, tm), lambda i, j, k: (k, i))
    elif a_shards == 1:
        a_spec = pl.BlockSpec((tm, tk), lambda i, j, k: (i, k))
    else:
        akb = (K // a_shards) // tk
        a_spec = pl.BlockSpec((None, tm, tk), lambda i, j, k: (k // akb, i, k % akb))
    if b_shards == 1:
        b_spec = pl.BlockSpec((tn, tk), lambda i, j, k: (j, k)) if tb else pl.BlockSpec((tk, tn), lambda i, j, k: (k, j))
    elif tb:
        kpb = (K // b_shards) // tk
        b_spec = pl.BlockSpec((None, tn, tk), lambda i, j, k: (k // kpb, j, k % kpb))
    else:
        npb = (N // b_shards) // tn
        b_spec = pl.BlockSpec((None, tk, tn), lambda i, j, k: (j // npb, k, j % npb))
    if out_shards == 1:
        out_spec = pl.BlockSpec((tm, tn), lambda i, j, k: (i, j))
        out_shape = jax.ShapeDtypeStruct((M, N), out_dtype)
    else:
        opb = (N // out_shards) // tn
        out_spec = pl.BlockSpec((None, tm, tn), lambda i, j, k: (j // opb, i, j % opb))
        out_shape = jax.ShapeDtypeStruct((out_shards, M, N // out_shards), out_dtype)
    return pl.pallas_call(
        body, name=name, grid=(M // tm, N // tn, nk),
        in_specs=[a_spec, b_spec], out_specs=out_spec, out_shape=out_shape,
        scratch_shapes=[pltpu.VMEM((tm, tn), F32)] if nk > 1 else [],
        compiler_params=_cparams(("parallel", "parallel", "arbitrary")),
    )(a, b)


FFN_TILE_ROWS = 1024


def _ffn_in_swiglu(x, w4, *, name):
    M, K = x.shape
    S, _, ns = w4.shape
    half = S * ns // 2
    tn = _pick(ns, 512, LANES)
    tm = _pick(M, FFN_TILE_ROWS, 16)
    npb = ns // tn

    def body(x_ref, wa_ref, wb_ref, h_ref, ab_ref):
        xb = x_ref[...].astype(BF16)
        a = jnp.dot(xb, wa_ref[...].astype(BF16), preferred_element_type=F32)
        b = jnp.dot(xb, wb_ref[...].astype(BF16), preferred_element_type=F32)
        h_ref[...] = (jax.nn.silu(a) * b).astype(h_ref.dtype)
        ab_ref[0] = a.astype(ab_ref.dtype)
        ab_ref[1] = b.astype(ab_ref.dtype)

    return pl.pallas_call(
        body, name=name, grid=(M // tm, half // tn),
        in_specs=[pl.BlockSpec((tm, K), lambda i, j: (i, 0)),
                  pl.BlockSpec((None, K, tn), lambda i, j: (j // npb, 0, j % npb)),
                  pl.BlockSpec((None, K, tn), lambda i, j: (S // 2 + j // npb, 0, j % npb))],
        out_specs=[pl.BlockSpec((tm, tn), lambda i, j: (i, j)), pl.BlockSpec((2, tm, tn), lambda i, j: (0, i, j))],
        out_shape=[jax.ShapeDtypeStruct((M, half), BF16), jax.ShapeDtypeStruct((2, M, half), BF16)],
        compiler_params=_cparams(("parallel", "parallel")),
    )(x, w4, w4)


def _ffn_out_dx_swiglu(dy, w, ab, *, name):
    M, D = dy.shape
    n2 = w.shape[0]
    tn = _pick(n2, 512, LANES)
    tm = _pick(M, FFN_TILE_ROWS, 16)

    def body(dy_ref, w_ref, ab_ref, o_ref):
        dh = lax.dot_general(dy_ref[...].astype(BF16), w_ref[...].astype(BF16), NT_DIMS, preferred_element_type=F32)
        a, b = ab_ref[0].astype(F32), ab_ref[1].astype(F32)
        s = jax.nn.sigmoid(a)
        o_ref[0] = (dh * b * (s * (1.0 + a * (1.0 - s)))).astype(o_ref.dtype)
        o_ref[1] = (dh * (a * s)).astype(o_ref.dtype)

    return pl.pallas_call(
        body, name=name, grid=(M // tm, n2 // tn),
        in_specs=[pl.BlockSpec((tm, D), lambda i, j: (i, 0)), pl.BlockSpec((tn, D), lambda i, j: (j, 0)),
                  pl.BlockSpec((2, tm, tn), lambda i, j: (0, i, j))],
        out_specs=pl.BlockSpec((2, tm, tn), lambda i, j: (0, i, j)),
        out_shape=jax.ShapeDtypeStruct((2, M, n2), BF16),
        compiler_params=_cparams(("parallel", "parallel")),
    )(dy, w, ab)


ROW_TILE_BYTES = 6 * 1024 * 1024
STREAM_TILE_BYTES = 14 * 1024 * 1024


def _row_tile(tiled, extra_bytes=0, budget=ROW_TILE_BYTES):
    rows = tiled[0].shape[0]
    per_row = sum(a.shape[1] * 4 for a in tiled) + extra_bytes
    target = max(SUBLANES, budget // max(per_row, 1))
    return _pick(rows, min(target, 512), 16)


def _rw(f, tiled, bcast, out_dtypes, *, name, anchor=None, tile_bytes=ROW_TILE_BYTES):
    nt, nb = len(tiled), len(bcast)
    rows = tiled[0].shape[0]
    outs_aval = jax.eval_shape(f, *[jax.ShapeDtypeStruct((16, a.shape[1]), F32) for a in tiled],
                               *[jax.ShapeDtypeStruct(b.shape, F32) for b in bcast])
    widths = [o.shape[1] for o in outs_aval]
    tm = _row_tile(tiled, sum(w * 4 for w in widths), tile_bytes)

    extra = [] if anchor is None else [anchor]
    n_in = nt + nb + len(extra)

    def body(*refs):
        tin = [r[...].astype(F32) for r in refs[:nt]]
        bin_ = [r[...].astype(F32) for r in refs[nt:nt + nb]]
        outs = f(*tin, *bin_)
        for o_ref, o in zip(refs[n_in:], outs):
            o_ref[...] = o.astype(o_ref.dtype)

    in_specs = [pl.BlockSpec((tm, a.shape[1]), lambda i: (i, 0)) for a in tiled]
    in_specs += [pl.BlockSpec(b.shape, lambda i: (0, 0)) for b in bcast + extra]
    res = pl.pallas_call(
        body, name=name, grid=(rows // tm,), in_specs=in_specs,
        out_specs=[pl.BlockSpec((tm, w), lambda i: (i, 0)) for w in widths],
        out_shape=[jax.ShapeDtypeStruct((rows, w), dt) for w, dt in zip(widths, out_dtypes)],
        compiler_params=_cparams(("parallel",)),
    )(*tiled, *bcast, *extra)
    return list(res)


def _rw_vjp(f, tiled, bcast, cts, need_t, need_b, t_dtypes, *, name, anchor=None):
    nt, nb = len(tiled), len(bcast)
    rows = tiled[0].shape[0]
    flat_cts = [c for group in cts for c in group]
    t_idx = [i for i in range(nt) if need_t[i]]
    b_idx = [i for i in range(nb) if need_b[i]]
    tm = _row_tile(list(tiled) + flat_cts, sum(tiled[i].shape[1] * 4 for i in t_idx))
    nc = len(flat_cts)
    extra = [] if anchor is None else [anchor]

    def body(*refs):
        i = pl.program_id(0)
        tin = [r[...].astype(F32) for r in refs[:nt]]
        bin_ = [r[...].astype(F32) for r in refs[nt:nt + nb]]
        ct_refs = refs[nt + nb:nt + nb + nc]
        out_refs = refs[nt + nb + nc + len(extra):]
        outs, vjp_fn = jax.vjp(f, *tin, *bin_)
        ct_vals, pos = [], 0
        for o, group in zip(outs, cts):
            acc = jnp.zeros_like(o)
            for _ in group:
                acc = acc + ct_refs[pos][...].astype(F32)
                pos += 1
            ct_vals.append(acc)
        grads = vjp_fn(tuple(ct_vals))
        for o_ref, k in zip(out_refs[:len(t_idx)], t_idx):
            o_ref[...] = grads[k].astype(o_ref.dtype)
        for o_ref, k in zip(out_refs[len(t_idx):], b_idx):
            @pl.when(i == 0)
            def _(o_ref=o_ref):
                o_ref[...] = jnp.zeros_like(o_ref)

            o_ref[...] += grads[nt + k]

    in_specs = [pl.BlockSpec((tm, a.shape[1]), lambda i: (i, 0)) for a in tiled]
    in_specs += [pl.BlockSpec(b.shape, lambda i: (0, 0)) for b in bcast]
    in_specs += [pl.BlockSpec((tm, c.shape[1]), lambda i: (i, 0)) for c in flat_cts]
    in_specs += [pl.BlockSpec(e.shape, lambda i: (0, 0)) for e in extra]
    out_specs = [pl.BlockSpec((tm, tiled[k].shape[1]), lambda i: (i, 0)) for k in t_idx]
    out_specs += [pl.BlockSpec(bcast[k].shape, lambda i: (0, 0)) for k in b_idx]
    out_shape = [jax.ShapeDtypeStruct(tiled[k].shape, dt) for k, dt in zip(t_idx, t_dtypes)]
    out_shape += [jax.ShapeDtypeStruct(bcast[k].shape, F32) for k in b_idx]
    res = pl.pallas_call(
        body, name=name, grid=(rows // tm,), in_specs=in_specs, out_specs=out_specs, out_shape=out_shape,
        compiler_params=_cparams(("arbitrary",)),
    )(*tiled, *bcast, *flat_cts, *extra)
    res = list(res)
    return res[:len(t_idx)], res[len(t_idx):]


def _rms(x, g):
    return x * lax.rsqrt(jnp.mean(x * x, axis=-1, keepdims=True) + EPS) * g


def _f_norm_mod(x, g, sc, sh):
    return (_rms(x, g) * (1.0 + sc) + sh,)


def _f_norm_mod_keep(x, g, sc, sh):
    return (_rms(x, g) * (1.0 + sc) + sh, x)


@jax.custom_vjp
def _swap16(x):
    w = x.shape[-1]
    lane = lax.broadcasted_iota(jnp.int32, x.shape, x.ndim - 1)
    return jnp.where((lane & 16) == 0, pltpu.roll(x, w - 16, x.ndim - 1), pltpu.roll(x, 16, x.ndim - 1))


_swap16.defvjp(lambda x: (_swap16(x), None), lambda _, g: (_swap16(g),))


def _rope(x, cos, sin):
    return x * cos + _swap16(x) * sin


def _make_f_post_in(sw, q_rank, kv_rank, with_q):
    o1, o2, o3 = sw, sw + q_rank, sw + q_rank + kv_rank

    if with_q:
        def f(ha, cos, sin, qg, kvg):
            u = ha[:, :o1]
            cqn = _rms(ha[:, o1:o2], qg)
            ckvn = _rms(ha[:, o2:o3], kvg)
            kr = _rope(ha[:, o3:o3 + LANES], cos, sin)
            return u, cqn, ckvn, kr
    else:
        def f(ha, kvg):
            return ha[:, :o1], _rms(ha[:, o2:o3], kvg), ha[:, o3:o3 + LANES]
    return f


def _f_qpost(q2, cos, sin):
    parts = []
    for h in range(q2.shape[1] // (2 * LANES)):
        o = 2 * LANES * h
        parts += [q2[:, o:o + LANES], _rope(q2[:, o + LANES:o + 2 * LANES], cos, sin)]
    return (jnp.concatenate(parts, axis=1),)


def _f_s5post(u, r, d):
    return (jax.nn.gelu(d * u + r, approximate=True),)


def _f_merge(ab, bm, gt):
    d = bm.shape[1]
    br_s5 = ab[:, :d] * jax.nn.sigmoid(ab[:, d:])
    g = jax.nn.sigmoid(gt)
    return (g[:, :d] * br_s5 + g[:, d:] * bm,)


def _f_resid_norm(x, out, g1, n2, sc2, sh2):
    x1 = x + g1 * out
    return x1, _rms(x1, n2) * (1.0 + sc2) + sh2


def _f_final(x1, f, tgt, g2, nf):
    y = _rms(x1 + g2 * f, nf)
    return (0.5 * jnp.mean(jnp.square(y - tgt), axis=-1, keepdims=True),)


def _bd_fanin(xs, ws, *, name):
    nw = len(ws)
    nb, kb, nn = ws[0].shape
    T = xs[0].shape[0]
    tm = _pick(T, 512, 16)

    def body(*refs):
        acc = None
        for x_ref, w_ref in zip(refs[:nw], refs[nw:2 * nw]):
            t = jnp.dot(x_ref[...].astype(BF16), w_ref[0].astype(BF16), preferred_element_type=F32)
            acc = t if acc is None else acc + t
        refs[2 * nw][...] = acc

    return pl.pallas_call(
        body, name=name, grid=(nb, T // tm),
        in_specs=[pl.BlockSpec((tm, kb), lambda j, i: (i, j))] * nw + [pl.BlockSpec((1, kb, nn), lambda j, i: (j, 0, 0))] * nw,
        out_specs=pl.BlockSpec((tm, nn), lambda j, i: (i, j)),
        out_shape=jax.ShapeDtypeStruct((T, nb * nn), F32),
        compiler_params=_cparams(("parallel", "parallel")),
    )(*xs, *ws)


def _bd_dw(xs, dys, nb, *, name):
    npair = len(xs)
    T = xs[0].shape[0]
    kb = xs[0].shape[1] // nb
    nn = dys[0].shape[1] // nb
    tm = _pick(T, 512, 16)
    dims = (((0,), (0,)), ((), ()))

    def body(*refs):
        i = pl.program_id(1)
        for x_ref, d_ref, o_ref in zip(refs[:npair], refs[npair:2 * npair], refs[2 * npair:]):
            @pl.when(i == 0)
            def _(o_ref=o_ref):
                o_ref[...] = jnp.zeros_like(o_ref)

            o_ref[0] += lax.dot_general(x_ref[...].astype(BF16), d_ref[...].astype(BF16), dims,
                                        preferred_element_type=F32)

    return list(pl.pallas_call(
        body, name=name, grid=(nb, T // tm),
        in_specs=[pl.BlockSpec((tm, kb), lambda j, i: (i, j))] * npair + [pl.BlockSpec((tm, nn), lambda j, i: (i, j))] * npair,
        out_specs=[pl.BlockSpec((1, kb, nn), lambda j, i: (j, 0, 0))] * npair,
        out_shape=[jax.ShapeDtypeStruct((nb, kb, nn), F32)] * npair,
        compiler_params=_cparams(("parallel", "arbitrary")),
    )(*xs, *dys))


def _cmul(ar, ai, br, bi):
    return ar * br - ai * bi, ar * bi + ai * br


def _cpow(lr, li, n):
    rr, ri = None, None
    br, bi = lr, li
    while n:
        if n & 1:
            rr, ri = (br, bi) if rr is None else _cmul(rr, ri, br, bi)
        n >>= 1
        if n:
            br, bi = _cmul(br, bi, br, bi)
    return rr, ri


SCAN_MM_ROWS = 512


def _s5_scan(x, w_re, w_im, lam_re, lam_im, h0_re, h0_im, e0_re, e0_im, *, reverse, name):
    rows = x.shape[0]
    nb, kb, cb = w_re.shape
    C = nb * cb
    n = rows // N_SEG
    mm_rows = _pick(rows, SCAN_MM_ROWS, 16)
    seg_order = list(range(N_SEG))[::-1] if reverse else list(range(N_SEG))
    s_first, s_last = seg_order[0], seg_order[-1]

    def body(x_ref, wr_ref, wi_ref, lr_ref, li_ref, h0r_ref, h0i_ref, e0r_ref, e0i_ref, hr_ref, hi_ref, htr_ref, hti_ref,
             locr_ref, loci_ref):
        shape = (N_SEG, cb)
        lr = jnp.broadcast_to(lr_ref[...], shape)
        li = jnp.broadcast_to(li_ref[...], shape)
        row = lax.broadcasted_iota(jnp.int32, shape, 0)

        def step_of(k):
            return (n - 1 - k) if reverse else k

        def rows_of(k):
            return pl.ds(pl.multiple_of(step_of(k) * N_SEG, N_SEG), N_SEG)

        wr, wi = wr_ref[...].astype(BF16), wi_ref[...].astype(BF16)
        for r0 in range(0, rows, mm_rows):
            xb = x_ref[r0:r0 + mm_rows, :].astype(BF16)
            locr_ref[r0:r0 + mm_rows, :] = jnp.dot(xb, wr, preferred_element_type=F32)
            loci_ref[r0:r0 + mm_rows, :] = jnp.dot(xb, wi, preferred_element_type=F32)

        first = row == s_first
        hr = locr_ref[rows_of(0), :] + jnp.where(first, e0r_ref[...], 0.0)
        hi = loci_ref[rows_of(0), :] + jnp.where(first, e0i_ref[...], 0.0)
        locr_ref[rows_of(0), :] = hr
        loci_ref[rows_of(0), :] = hi

        def pass1(k, carry):
            hr, hi = carry
            pr, pi = _cmul(lr, li, hr, hi)
            hr = pr + locr_ref[rows_of(k), :]
            hi = pi + loci_ref[rows_of(k), :]
            locr_ref[rows_of(k), :] = hr
            loci_ref[rows_of(k), :] = hi
            return hr, hi

        er, ei = lax.fori_loop(1, n, pass1, (hr, hi))

        lnr, lni = _cpow(lr_ref[...], li_ref[...], n)
        cr, ci = h0r_ref[...], h0i_ref[...]
        cin_r = jnp.zeros(shape, F32)
        cin_i = jnp.zeros(shape, F32)
        for s in seg_order:
            cin_r = jnp.where(row == s, cr, cin_r)
            cin_i = jnp.where(row == s, ci, cin_i)
            if s != s_last:
                pr, pi = _cmul(lnr, lni, cr, ci)
                cr = pr + jnp.sum(jnp.where(row == s, er, 0.0), axis=0, keepdims=True)
                ci = pi + jnp.sum(jnp.where(row == s, ei, 0.0), axis=0, keepdims=True)

        def pass2(k, carry):
            pr, pi, _, _ = carry
            ar, ai = _cmul(pr, pi, cin_r, cin_i)
            hr = locr_ref[rows_of(k), :] + ar
            hi = loci_ref[rows_of(k), :] + ai
            hr_ref[rows_of(k), :] = hr.astype(hr_ref.dtype)
            hi_ref[rows_of(k), :] = hi.astype(hi_ref.dtype)
            npr, npi = _cmul(pr, pi, lr, li)
            return npr, npi, hr, hi

        _, _, last_r, last_i = lax.fori_loop(0, n, pass2, (lr, li, er, ei))
        htr_ref[...] = jnp.sum(jnp.where(row == s_last, last_r, 0.0), axis=0, keepdims=True)
        hti_ref[...] = jnp.sum(jnp.where(row == s_last, last_i, 0.0), axis=0, keepdims=True)

    big = pl.BlockSpec((rows, cb), lambda j: (0, j))
    vec = pl.BlockSpec((1, cb), lambda j: (0, j))
    wspec = pl.BlockSpec((None, kb, cb), lambda j: (j, 0, 0))
    return pl.pallas_call(
        body, name=name, grid=(nb,),
        in_specs=[pl.BlockSpec((rows, kb), lambda j: (0, j)), wspec, wspec] + [vec] * 6,
        out_specs=[big, big, vec, vec],
        out_shape=[jax.ShapeDtypeStruct((rows, C), BF16)] * 2 + [jax.ShapeDtypeStruct((1, C), F32)] * 2,
        scratch_shapes=[pltpu.VMEM((rows, cb), F32)] * 2,
        compiler_params=_cparams(("parallel",)),
    )(x, w_re, w_im, lam_re, lam_im, h0_re, h0_im, e0_re, e0_im)


def _s5_dlam(mu_re, mu_im, h_re, h_im, h0_re, h0_im, *, reverse, name):
    rows, C = h_re.shape
    n = rows // N_SEG
    cb = _pick(C, 256, LANES)
    s_first = N_SEG - 1 if reverse else 0

    def body(mr_ref, mi_ref, hr_ref, hi_ref, h0r_ref, h0i_ref, dr_ref, di_ref):
        shape = (N_SEG, cb)
        row = lax.broadcasted_iota(jnp.int32, shape, 0)

        def rows_of(k):
            step = (n - 1 - k) if reverse else k
            return pl.ds(pl.multiple_of(step * N_SEG, N_SEG), N_SEG)

        def term(k, pr, pi):
            mr, mi = mr_ref[rows_of(k), :].astype(F32), mi_ref[rows_of(k), :].astype(F32)
            return mr * pr + mi * pi, mi * pr - mr * pi

        shift = N_SEG - 1 if reverse else 1
        pr = jnp.where(row == s_first, h0r_ref[...], pltpu.roll(hr_ref[rows_of(n - 1), :].astype(F32), shift, 0))
        pi = jnp.where(row == s_first, h0i_ref[...], pltpu.roll(hi_ref[rows_of(n - 1), :].astype(F32), shift, 0))
        acc = term(0, pr, pi)

        def loop(k, acc):
            tr, ti = term(k, hr_ref[rows_of(k - 1), :].astype(F32), hi_ref[rows_of(k - 1), :].astype(F32))
            return acc[0] + tr, acc[1] + ti

        ar, ai = lax.fori_loop(1, n, loop, acc)
        dr_ref[...] = jnp.sum(ar, axis=0, keepdims=True)
        di_ref[...] = jnp.sum(ai, axis=0, keepdims=True)

    big = pl.BlockSpec((rows, cb), lambda j: (0, j))
    vec = pl.BlockSpec((1, cb), lambda j: (0, j))
    return pl.pallas_call(
        body, name=name, grid=(C // cb,),
        in_specs=[big] * 4 + [vec] * 2, out_specs=[vec, vec],
        out_shape=[jax.ShapeDtypeStruct((1, C), F32)] * 2,
        compiler_params=_cparams(("parallel",)),
    )(mu_re, mu_im, h_re, h_im, h0_re, h0_im)


NT_DIMS = (((1,), (1,)), ((), ()))
TN_DIMS = (((0,), (0,)), ((), ()))


ATTN_Q_ROWS = 512


def _attn_exp(q, kvh, kr):
    s = (lax.dot_general(q[:, :LANES], kvh[:, :LANES], NT_DIMS, preferred_element_type=F32)
         + lax.dot_general(q[:, LANES:], kr, NT_DIMS, preferred_element_type=F32))
    e = jnp.exp2((s - jnp.max(s, axis=-1, keepdims=True)) * (ATTN_SCALE * math.log2(math.e)))
    return e, jnp.sum(e, axis=-1, keepdims=True)


def _attn_specs(L, T, tq):
    return [
        pl.BlockSpec((tq, 2 * LANES), lambda h, i: (i, h)),
        pl.BlockSpec((T, 2 * LANES), lambda h, i: (0, h)),
        pl.BlockSpec((T, LANES), lambda h, i: (0, 0)),
    ]


def _attn_fwd(qq, kv, kr, *, name):
    L, T = qq.shape[0], kv.shape[0]
    tq = _pick(L, ATTN_Q_ROWS // 2, 16)

    def body(q_ref, kv_ref, kr_ref, o_ref):
        kvh = kv_ref[...]
        e, l = _attn_exp(q_ref[...], kvh, kr_ref[...])
        o_ref[...] = (jnp.dot(e.astype(BF16), kvh[:, LANES:], preferred_element_type=F32) * (1.0 / l)).astype(o_ref.dtype)

    return pl.pallas_call(
        body, name=name, grid=(MLA_HEADS, L // tq), in_specs=_attn_specs(L, T, tq),
        out_specs=pl.BlockSpec((tq, LANES), lambda h, i: (i, h)),
        out_shape=jax.ShapeDtypeStruct((L, MLA_HEADS * V_DIM), BF16),
        compiler_params=_cparams(("parallel", "parallel")),
    )(qq, kv, kr)


def _attn_bwd(qq, kv, kr, do, *, name):
    L, T = qq.shape[0], kv.shape[0]
    H = MLA_HEADS
    tq = _pick(L, ATTN_Q_ROWS, 16)
    nq = L // tq

    def body(q_ref, kv_ref, kr_ref, do_ref, dq_ref, dkv_ref, dkr_ref, dkn_acc, dv_acc):
        h, i = pl.program_id(0), pl.program_id(1)
        q, kvh, krv, dov = q_ref[...], kv_ref[...], kr_ref[...], do_ref[...]
        e, l = _attn_exp(q, kvh, krv)
        inv = 1.0 / l
        ps = e * (inv * ATTN_SCALE)
        t = lax.dot_general(dov, kvh[:, LANES:], NT_DIMS, preferred_element_type=F32) * ps
        ds = (t - ps * (jnp.sum(t, axis=-1, keepdims=True) * (1.0 / ATTN_SCALE))).astype(BF16)
        dq_ref[:, :LANES] = jnp.dot(ds, kvh[:, :LANES], preferred_element_type=F32)
        dq_ref[:, LANES:] = jnp.dot(ds, krv, preferred_element_type=F32)

        @pl.when(i == 0)
        def _():
            dkn_acc[...] = jnp.zeros_like(dkn_acc)
            dv_acc[...] = jnp.zeros_like(dv_acc)

        @pl.when((i == 0) & (h == 0))
        def _():
            dkr_ref[...] = jnp.zeros_like(dkr_ref)

        dv_acc[...] += lax.dot_general(e.astype(BF16), (dov.astype(F32) * inv).astype(BF16), TN_DIMS,
                                       preferred_element_type=F32)
        dkn_acc[...] += lax.dot_general(ds, q[:, :LANES], TN_DIMS, preferred_element_type=F32)
        dkr_ref[...] += lax.dot_general(ds, q[:, LANES:], TN_DIMS, preferred_element_type=F32)

        @pl.when(i == nq - 1)
        def _():
            dkv_ref[:, :LANES] = dkn_acc[...].astype(dkv_ref.dtype)
            dkv_ref[:, LANES:] = dv_acc[...].astype(dkv_ref.dtype)

    in_specs = _attn_specs(L, T, tq) + [pl.BlockSpec((tq, LANES), lambda h, i: (i, h))]
    return pl.pallas_call(
        body, name=name, grid=(H, L // tq), in_specs=in_specs,
        out_specs=[pl.BlockSpec((tq, 2 * LANES), lambda h, i: (i, h)), pl.BlockSpec((T, 2 * LANES), lambda h, i: (0, h)),
                   pl.BlockSpec((T, LANES), lambda h, i: (0, 0))],
        out_shape=[jax.ShapeDtypeStruct((L, H * 2 * LANES), F32), jax.ShapeDtypeStruct((T, H * 2 * LANES), BF16),
                   jax.ShapeDtypeStruct((T, LANES), F32)],
        scratch_shapes=[pltpu.VMEM((T, LANES), F32), pltpu.VMEM((T, LANES), F32)],
        compiler_params=_cparams(("arbitrary", "arbitrary")),
    )(qq, kv, kr, do)


def _adamw(w, g, m, v, *, name, anchor=None):
    c1 = 1.0 - ADAM_B1 ** ADAM_STEP
    c2 = 1.0 - ADAM_B2 ** ADAM_STEP

    def f(w, g, m, v):
        m = ADAM_B1 * m + (1.0 - ADAM_B1) * g
        v = ADAM_B2 * v + (1.0 - ADAM_B2) * jnp.square(g)
        delta = -ADAM_LR * ((m / c1) / (jnp.sqrt(v / c2) + ADAM_EPS) + ADAM_WD * w)
        return g, delta, m, v

    return _rw(f, [w, g, m, v], [], [F32] * 4, name=name, anchor=anchor, tile_bytes=STREAM_TILE_BYTES)


def _slab_rows(rows, cols, n_arrays):
    return _pick(rows, max(16, (8 * 1024 * 1024) // (cols * 4 * n_arrays)), 16)


def _scalars(*vals):
    return jnp.stack([jnp.asarray(v, jnp.int32) for v in vals])


def _into_slot(src, slot, nslots, dtype, *, name):
    R, C = src.shape
    tr = _slab_rows(R, C, 2)

    def body(s_ref, x_ref, o_ref):
        o_ref[...] = x_ref[...].astype(o_ref.dtype)

    return pl.pallas_call(
        body, name=name,
        grid_spec=pltpu.PrefetchScalarGridSpec(
            num_scalar_prefetch=1, grid=(R // tr,),
            in_specs=[pl.BlockSpec((tr, C), lambda i, s: (i, 0))],
            out_specs=pl.BlockSpec((None, tr, C), lambda i, s: (s[0], i, 0))),
        out_shape=jax.ShapeDtypeStruct((nslots, R, C), dtype),
        compiler_params=_cparams(("arbitrary",)),
    )(_scalars(slot), src)


def _pair_sum(g, got, c, *, name):
    _, R, C = g.shape
    hr = R // 2
    tr = _slab_rows(hr, C, 3)
    nblk = hr // tr

    def body(s_ref, g_ref, r_ref, o_ref):
        o_ref[...] = (g_ref[...].astype(F32) + r_ref[...].astype(F32)).astype(o_ref.dtype)

    return pl.pallas_call(
        body, name=name,
        grid_spec=pltpu.PrefetchScalarGridSpec(
            num_scalar_prefetch=1, grid=(4, nblk),
            in_specs=[pl.BlockSpec((None, tr, C), lambda j, i, s: (j, s[0] * nblk + i, 0)),
                      pl.BlockSpec((None, tr, C), lambda j, i, s: (j, i, 0))],
            out_specs=pl.BlockSpec((None, tr, C), lambda j, i, s: (j, i, 0))),
        out_shape=jax.ShapeDtypeStruct((4, hr, C), g.dtype),
        compiler_params=_cparams(("arbitrary", "arbitrary")),
    )(_scalars(c), g, got)


def _chip_sum(p, landed, me_chip, c, *, name):
    _, hr, C = p.shape
    tr = _slab_rows(hr, C, 5)

    def body(s_ref, p_ref, l0_ref, l1_ref, l2_ref, o_ref):
        o_ref[...] = ((p_ref[...].astype(F32) + l0_ref[...].astype(F32)) + l1_ref[...].astype(F32)) + l2_ref[...].astype(F32)

    return pl.pallas_call(
        body, name=name,
        grid_spec=pltpu.PrefetchScalarGridSpec(
            num_scalar_prefetch=1, grid=(hr // tr,),
            in_specs=[pl.BlockSpec((None, tr, C), lambda i, s: (s[0], i, 0))]
            + [pl.BlockSpec((None, tr, C), functools.partial(lambda i, s, k: (k, i, 0), k=k)) for k in range(3)],
            out_specs=pl.BlockSpec((None, tr, C), lambda i, s: (s[1], i, 0))),
        out_shape=jax.ShapeDtypeStruct((2, hr, C), F32),
        compiler_params=_cparams(("arbitrary",)),
    )(_scalars(me_chip, c), p, landed, landed, landed)


def _place():
    return lax.axis_index("x"), lax.axis_index("y"), lax.axis_index("c")


def _other_chips(x, y):
    chips = [(1 - x, y), (x, 1 - y), (1 - x, 1 - y)]
    return chips, [2 * cx + cy for cx, cy in chips]


HBM = pl.BlockSpec(memory_space=pl.ANY)


def _allgather8(v, *, name):
    rows, cols = v.shape

    def body(v_ref, out_ref, send_sems, recv_sems):
        x, y, c = _place()
        me = 4 * x + 2 * y + c
        out_ref[me] = v_ref[...]
        copies = []
        for k in range(1, 8):
            bx, by, bc = (k >> 2) & 1, (k >> 1) & 1, k & 1
            px, py, pc = x ^ bx, y ^ by, c ^ bc
            cp = pltpu.make_async_remote_copy(
                src_ref=v_ref, dst_ref=out_ref.at[me], send_sem=send_sems.at[k - 1], recv_sem=recv_sems.at[k - 1],
                device_id=(px, py, pc), device_id_type=MESH)
            cp.start()
            copies.append((cp, 4 * px + 2 * py + pc))
        for k, (cp, peer) in enumerate(copies):
            pltpu.make_async_remote_copy(
                src_ref=v_ref, dst_ref=out_ref.at[peer], send_sem=send_sems.at[k], recv_sem=recv_sems.at[k],
                device_id=(x, y, c), device_id_type=MESH).wait_recv()
        for cp, _ in copies:
            cp.wait_send()

    return pl.pallas_call(
        body, name=name, out_shape=jax.ShapeDtypeStruct((8, rows, cols), v.dtype),
        in_specs=[pl.BlockSpec(memory_space=pltpu.VMEM)], out_specs=pl.BlockSpec(memory_space=pltpu.VMEM),
        scratch_shapes=[pltpu.SemaphoreType.DMA((7,)), pltpu.SemaphoreType.DMA((7,))],
        compiler_params=pltpu.CompilerParams(vmem_limit_bytes=VMEM_LIMIT),
    )(v)


def _allgather_shards(bufs, *, name):
    n = len(bufs)

    def body(*refs):
        outs = refs[n:2 * n]
        send_sems, recv_sems = refs[2 * n:]
        x, y, c = _place()
        me_chip = 2 * x + y
        sibling = (x, y, 1 - c)
        chips, chip_ids = _other_chips(x, y)

        def remote(k, j, blk, hf, to):
            hr = bufs[k].shape[1] // 2
            piece = outs[k].at[blk, pl.ds(pl.multiple_of(hf * hr, 16), hr), :]
            return pltpu.make_async_remote_copy(
                src_ref=piece, dst_ref=piece, send_sem=send_sems.at[6 * k + j], recv_sem=recv_sems.at[6 * k + j],
                device_id=to, device_id_type=MESH)

        sends = []
        for k in range(n):
            for j, chip in enumerate(chips):
                cp = remote(k, j, me_chip, c, (*chip, c))
                cp.start()
                sends.append(cp)
        for k in range(n):
            for j, chip in enumerate(chips):
                remote(k, j, chip_ids[j], c, (x, y, c)).wait_recv()
                cp = remote(k, 3 + j, chip_ids[j], c, sibling)
                cp.start()
                sends.append(cp)
        for k in range(n):
            for j in range(3):
                remote(k, 3 + j, chip_ids[j], 1 - c, (x, y, c)).wait_recv()
        for cp in sends:
            cp.wait_send()

    return list(pl.pallas_call(
        body, name=name, out_shape=[jax.ShapeDtypeStruct(b.shape, b.dtype) for b in bufs],
        in_specs=[HBM] * n, out_specs=[HBM] * n, input_output_aliases={k: k for k in range(n)},
        scratch_shapes=[pltpu.SemaphoreType.DMA((6 * n,)), pltpu.SemaphoreType.DMA((6 * n,))],
    )(*bufs))


HBM_SPEC = pl.BlockSpec(memory_space=pltpu.HBM)
SEM_SPEC = pl.BlockSpec(memory_space=pltpu.SEMAPHORE)
EFFECT = pltpu.SideEffectType.DATAFLOW_SIDE_EFFECTING
TOKEN = jax.ShapeDtypeStruct((SUBLANES, LANES), F32)


def _in_hbm(a):
    return pltpu.with_memory_space_constraint(a, pltpu.HBM)


def _half_rows(buf, hf):
    hr = buf.shape[1] // 2
    return pl.ds(pl.multiple_of(hf * hr, 16), hr)


def _plan_ag_ici(refs):
    x, y, c = _place()
    chips, ids = _other_chips(x, y)
    out = []
    for r in refs:
        mine = r.at[2 * x + y, _half_rows(r, c), :]
        out += [(mine, mine, r.at[ids[j], _half_rows(r, c), :], (*chip, c)) for j, chip in enumerate(chips)]
    return out


def _plan_ag_pair(refs):
    x, y, c = _place()
    _, ids = _other_chips(x, y)
    out = []
    for r in refs:
        for j in range(3):
            piece = r.at[ids[j], _half_rows(r, c), :]
            out.append((piece, piece, r.at[ids[j], _half_rows(r, 1 - c), :], (x, y, 1 - c)))
    return out


def _plan_rs_ici(refs):
    x, y, c = _place()
    chips, ids = _other_chips(x, y)
    n = len(refs) // 2
    return [(refs[k].at[ids[j]], refs[n + k].at[j], refs[n + k].at[j], (*chip, c))
            for k in range(n) for j, chip in enumerate(chips)]


def _plan_pair_exchange(refs):
    x, y, c = _place()
    n = len(refs) // 2
    return [(refs[k].at[:, _half_rows(refs[k], 1 - c), :], refs[n + k], refs[n + k], (x, y, 1 - c)) for k in range(n)]


def _plan_pair_gather(refs):
    x, y, c = _place()
    return [(r.at[c], r.at[c], r.at[1 - c], (x, y, 1 - c)) for r in refs]


def _remote(src, dst, send_sem, recv_sem, target):
    return pltpu.make_async_remote_copy(src_ref=src, dst_ref=dst, send_sem=send_sem, recv_sem=recv_sem,
                                        device_id=target, device_id_type=MESH)


def _copy_start(groups, *, name, after=()):
    flat = [a for arrays, _, _ in groups for a in arrays]
    n, ng = len(flat), len(groups)
    after = list(after)
    n_in = n + len(after)

    def body(*refs):
        sems = refs[n_in:n_in + 2 * ng]
        thru = refs[n_in + 2 * ng:n_in + 2 * ng + n]
        token = refs[-1]
        pos = 0
        for g, (arrays, plan, n_copies) in enumerate(groups):
            copies = plan(thru[pos:pos + len(arrays)])
            pos += len(arrays)
            assert len(copies) == n_copies
            for i, (src, dst, _, target) in enumerate(copies):
                _remote(src, dst, sems[2 * g].at[i], sems[2 * g + 1].at[i], target).start()
        token[...] = jnp.zeros_like(token)

    out_shape = tuple(pltpu.SemaphoreType.DMA((n_copies,)) for _, _, n_copies in groups for _ in range(2))
    out_shape += tuple(pltpu.HBM(a.shape, a.dtype) for a in flat) + (TOKEN,)
    res = pl.pallas_call(
        body, name=name, out_shape=out_shape,
        in_specs=(HBM_SPEC,) * n + (pl.BlockSpec(memory_space=pl.ANY),) * len(after),
        out_specs=(SEM_SPEC,) * (2 * ng) + (HBM_SPEC,) * n + (pl.BlockSpec(memory_space=pltpu.VMEM),),
        input_output_aliases={k: 2 * ng + k for k in range(n)},
        compiler_params=pltpu.CompilerParams(has_side_effects=EFFECT),
    )(*[_in_hbm(a) for a in flat], *after)
    sems = [(res[2 * g], res[2 * g + 1]) for g in range(ng)]
    thru, pos = [], 2 * ng
    for arrays, _, _ in groups:
        thru.append(list(res[pos:pos + len(arrays)]))
        pos += len(arrays)
    return sems, thru, res[-1]


def _copy_wait(arrays, sems, plan, n_copies, after, *, name):
    n = len(arrays)
    after = list(after)

    def body(*refs):
        send, recv = refs[n], refs[n + 1]
        x, y, c = _place()
        copies = plan(refs[:n])
        assert len(copies) == n_copies
        for i, (src, dst, landing, target) in enumerate(copies):
            _remote(src, dst, send.at[i], recv.at[i], target).wait_send()
            _remote(landing, landing, send.at[i], recv.at[i], (x, y, c)).wait_recv()

    return list(pl.pallas_call(
        body, name=name, out_shape=tuple(pltpu.HBM(a.shape, a.dtype) for a in arrays),
        in_specs=(HBM_SPEC,) * n + (SEM_SPEC, SEM_SPEC) + (pl.BlockSpec(memory_space=pl.ANY),) * len(after),
        out_specs=(HBM_SPEC,) * n, input_output_aliases={k: k for k in range(n)},
        compiler_params=pltpu.CompilerParams(has_side_effects=EFFECT),
    )(*arrays, *sems, *after))


def _rs_stage1(gs, tag, after=()):
    n = len(gs)
    lands = [lax.empty((4, g.shape[1] // 2, g.shape[2]), g.dtype) for g in gs]
    sems, (arrays,), token = _copy_start([(list(gs) + lands, _plan_pair_exchange, n)], name=f"rs_pair_start_{tag}",
                                         after=after)
    return (sems[0], arrays), token


def _rs_stage2(handle, after, tag):
    sems, arrays = handle
    n = len(arrays) // 2
    arrays = _copy_wait(arrays, sems, _plan_pair_exchange, n, after, name=f"rs_pair_wait_{tag}")
    c = lax.axis_index("c")
    pair = [_pair_sum(g, r, c, name=f"rs_pair_sum_{tag}{k}") for k, (g, r) in enumerate(zip(arrays[:n], arrays[n:]))]
    lands = [lax.empty((3,) + p.shape[1:], p.dtype) for p in pair]
    sems, (arrays,), token = _copy_start([(pair + lands, _plan_rs_ici, 3 * n)], name=f"rs_start_{tag}")
    return (sems[0], arrays), token


def _rs_stage3(handle, after, tag):
    sems, arrays = handle
    n = len(arrays) // 2
    arrays = _copy_wait(arrays, sems, _plan_rs_ici, 3 * n, after, name=f"rs_wait_{tag}")
    x, y, c = _place()
    halves = [_chip_sum(p, l, 2 * x + y, c, name=f"rs_chip_sum_{tag}{k}") for k, (p, l) in enumerate(zip(arrays[:n], arrays[n:]))]
    sems, (halves,), token = _copy_start([(halves, _plan_pair_gather, n)], name=f"rs_gather_start_{tag}")
    return (sems[0], halves), token


def _rs_stage4(handle, after, tag):
    sems, halves = handle
    full = _copy_wait(halves, sems, _plan_pair_gather, len(halves), after, name=f"rs_gather_wait_{tag}")
    return [f.reshape(2 * f.shape[1], f.shape[2]) for f in full]


def _to_segments(a):
    rows = a.shape[0]
    return a.reshape(N_SEG, rows // N_SEG, -1).transpose(1, 0, 2).reshape(rows, -1)


def _from_segments(a):
    rows = a.shape[0]
    return a.reshape(rows // N_SEG, N_SEG, -1).transpose(1, 0, 2).reshape(rows, -1)


def _rope_tables(L):
    t = jnp.arange(L, dtype=jnp.int32)
    row = (t // GRID_W).astype(F32)
    col = (t % GRID_W).astype(F32)
    n_freq = QK_ROPE // 4
    inv = ROPE_BASE ** (-jnp.arange(n_freq, dtype=F32) / n_freq)
    a0, a1 = row[:, None] * inv, col[:, None] * inv
    z = jnp.zeros((L, LANES - QK_ROPE), F32)
    cos = jnp.concatenate([jnp.cos(a0), jnp.cos(a0), jnp.cos(a1), jnp.cos(a1), z], axis=1)
    sin = jnp.concatenate([-jnp.sin(a0), jnp.sin(a0), -jnp.sin(a1), jnp.sin(a1), z], axis=1)
    return _to_segments(cos), _to_segments(sin)


def _col_blocks(w, nblk):
    r, c = w.shape
    return w.reshape(r, nblk, c // nblk).transpose(1, 0, 2)


def _from_col_blocks(w4):
    nblk, r, c = w4.shape
    return w4.transpose(1, 0, 2).reshape(r, nblk * c)


def _s5_discretize(a_re, a_im, log_dt, b_re, b_im):
    dt = jnp.exp(log_dt)[:, None]
    mag = jnp.exp(a_re * dt)
    ab_re, ab_im = mag * jnp.cos(a_im * dt), mag * jnp.sin(a_im * dt)
    den = a_re * a_re + a_im * a_im
    nr, ni = ab_re - 1.0, ab_im
    co_re = (nr * a_re + ni * a_im) / den
    co_im = (ni * a_re - nr * a_im) / den
    bb_re = co_re[..., None] * b_re - co_im[..., None] * b_im
    bb_im = co_re[..., None] * b_im + co_im[..., None] * b_re
    return ab_re, ab_im, bb_re, bb_im


def _diag_blocks_in(bb, gpb):
    G, N, P = bb.shape
    t = jnp.tile(jnp.swapaxes(bb, 1, 2).reshape(G // gpb, gpb * P, N), (1, 1, gpb))
    row = lax.broadcasted_iota(jnp.int32, t.shape, 1) // P
    col = lax.broadcasted_iota(jnp.int32, t.shape, 2) // N
    return jnp.where(row == col, t, 0.0)


def _diag_blocks_out(cc, gpb):
    G, P, N = cc.shape
    t = jnp.tile(jnp.swapaxes(cc, 1, 2).reshape(G // gpb, gpb * N, P), (1, 1, gpb))
    row = lax.broadcasted_iota(jnp.int32, t.shape, 1) // N
    col = lax.broadcasted_iota(jnp.int32, t.shape, 2) // P
    return jnp.where(row == col, t, 0.0)


def _tr(ws):
    return [jnp.swapaxes(w, 1, 2) for w in ws]


WEIGHTS = ['c_ctx', 'w_mod', 'b_mod', 'norm1', 'norm2', 'w_in', 's5_a_re', 's5_a_im', 's5_log_dt', 's5_b_re', 's5_b_im',
           's5_c_re', 's5_c_im', 's5_d', 'w_glu', 'q_norm', 'kv_norm', 'w_uq', 'w_ukv', 'w_mla_o', 'w_out', 'w_ffn_in',
           'w_ffn_out', 'norm_f']
AG_GROUPS = [['w_in'], ['w_glu', 'w_uq', 'w_ukv', 'w_mla_o', 'w_out'], ['w_ffn_in', 'w_ffn_out']]
SMALL = ['norm1', 'norm2', 's5_a_re', 's5_a_im', 's5_log_dt', 's5_b_re', 's5_b_im', 's5_c_re', 's5_c_im', 's5_d',
         'q_norm', 'kv_norm', 'norm_f']


def _pad_rows(a, rows):
    return jnp.concatenate([a, jnp.zeros((rows - a.shape[0],) + a.shape[1:], a.dtype)], axis=0)


def _pack(vals, width, rows):
    flat = jnp.concatenate([v.reshape(-1).astype(F32) for v in vals])
    flat = jnp.concatenate([flat, jnp.zeros((rows * width - flat.shape[0],), F32)])
    return flat.reshape(rows, width)


def _unpack(buf, like):
    flat = buf.reshape(-1)
    out, pos = [], 0
    for v in like:
        out.append(flat[pos:pos + v.size].reshape(v.shape))
        pos += v.size
    return out


def _step(x, c, ctx, loss_target, w, m, v):
    px, py, pc = _place()
    me = 4 * px + 2 * py + pc
    me_chip = 2 * px + py
    L, D = x.shape[1], x.shape[2]
    Lc = ctx.shape[1]
    T = L + Lc
    SW = D // 2
    G = SW // S5_GROUP
    C = G * S5_STATE
    H = MLA_HEADS
    q_rank = w['q_norm'].shape[1]
    kv_rank = w['kv_norm'].shape[1]
    d_ff = w['w_ffn_out'].shape[1] * 4
    wa_used = SW + q_rank + kv_rank + QK_ROPE
    WA = -(-(SW + q_rank + kv_rank + LANES) // 512) * 512

    c_rows = _pad_rows(c.astype(F32), SUBLANES)
    c_all = _allgather8(c_rows, name="ag_cond")[:, 0, :]
    cond = jnp.concatenate([c_all, w['c_ctx'].reshape(1, D)], axis=0)
    cond = _pad_rows(cond, 16)
    (act,) = _rw(lambda t: (jax.nn.silu(t),), [cond], [], [F32], name="cond_silu")
    w_mod, cs_mod = w['w_mod'][0], w['w_mod'].shape[2]
    mod_part = _mm(act, w_mod, out_dtype=F32, name="mod_fwd")
    mod_all = _allgather8(mod_part, name="ag_mod")
    mod_full = jnp.concatenate([mod_all[0], mod_all[2], mod_all[4], mod_all[6]], axis=1) + w['b_mod']
    m_lat = lax.dynamic_slice_in_dim(mod_full, me, 1, axis=0).reshape(6, D)
    m_ctx = mod_full[8].reshape(6, D)
    sh1, sc1, g1, sh2, sc2, g2 = (m_lat[i:i + 1] for i in range(6))
    csh1, csc1 = m_ctx[0:1], m_ctx[1:2]

    ag_groups = [([_into_slot(w[nme][0], me_chip, 4, BF16, name=f"cast_{nme}") for nme in grp], _plan_ag_ici, 3 * len(grp))
                 for grp in AG_GROUPS]
    ag_sems, ag_bufs, ag_token = _copy_start(ag_groups, name="ag_start", after=[mod_full])
    gathered, ag_pair = {}, {}

    def landed(g, after):
        n_cp = 3 * len(AG_GROUPS[g])
        got = _copy_wait(ag_bufs[g], ag_sems[g], _plan_ag_ici, n_cp, after, name=f"ag_wait_{g}")
        sems, (got,), token = _copy_start([(got, _plan_ag_pair, n_cp)], name=f"ag_pair_start_{g}")
        ag_pair[g] = (sems[0], got)
        return token[0, 0]

    def arrive(g, after):
        sems, got = ag_pair[g]
        got = _copy_wait(got, sems, _plan_ag_pair, 3 * len(AG_GROUPS[g]), after, name=f"ag_pair_wait_{g}")
        gathered.update(zip(AG_GROUPS[g], got))

    xs = _to_segments(x[0])
    cs = _to_segments(ctx[0])
    tgt = _to_segments(loss_target[0])
    cos, sin = _rope_tables(L)
    n1, n2, nf = w['norm1'], w['norm2'], w['norm_f'].reshape(1, D)
    qg, kvg = w['q_norm'], w['kv_norm']

    (xn_lat,) = _rw(_f_norm_mod, [xs], [n1 + ag_token[0, 0], sc1, sh1], [BF16], name="norm1_lat")
    (xn_ctx,) = _rw(_f_norm_mod, [cs], [n1, csc1, csh1], [BF16], name="norm1_ctx")
    xn = jnp.concatenate([xn_lat, xn_ctx], axis=0)
    landed(0, [xn])

    gpb = min(S5_BLOCK_GROUPS, G)
    gpo = min(8, G)
    d_skip = w['s5_d'][0].reshape(1, SW)
    disc, vjp_disc, w_b, w_c = [], [], [], []
    for d in range(2):
        prm = (w['s5_a_re'][0, d], w['s5_a_im'][0, d], w['s5_log_dt'][0, d], w['s5_b_re'][0, d], w['s5_b_im'][0, d])

        def prep(a_re, a_im, log_dt, b_re, b_im):
            ab_re, ab_im, bb_re, bb_im = _s5_discretize(a_re, a_im, log_dt, b_re, b_im)
            return ab_re.reshape(1, C), ab_im.reshape(1, C), _diag_blocks_in(bb_re, gpb), _diag_blocks_in(bb_im, gpb)

        out, vj = jax.vjp(prep, *prm)
        disc.append(out)
        vjp_disc.append(vj)
        w_b += [out[2], out[3]]
        w_c += [_diag_blocks_out(w['s5_c_re'][0, d], gpo), -_diag_blocks_out(w['s5_c_im'][0, d], gpo)]
    nb_in = G // gpb
    nb_out = G // gpo

    arrive(0, [xn, tgt] + w_b + w_c)
    w_in = _from_col_blocks(gathered['w_in'])
    w_a = jnp.concatenate([w_in[:, :wa_used], jnp.zeros((D, WA - wa_used), BF16)], axis=1)
    w_g = w_in[:, wa_used:]
    ha = _mm(xn, w_a, out_dtype=F32, name="in_proj")
    ha_lat, ha_ctx = ha[:L], ha[L:]
    gt = _mm(xn_lat, w_g, out_dtype=F32, name="in_gates")
    f_post_lat = _make_f_post_in(SW, q_rank, kv_rank, True)
    f_post_ctx = _make_f_post_in(SW, q_rank, kv_rank, False)
    u_lat, cqn, ckvn_lat, kr_lat = _rw(f_post_lat, [ha_lat, cos, sin], [qg, kvg], [F32, BF16, BF16, BF16], name="post_in_lat")
    u_ctx, ckvn_ctx, kr_ctx = _rw(f_post_ctx, [ha_ctx], [kvg], [F32, BF16, BF16], name="post_in_ctx")
    zero = jnp.zeros((1, C), F32) + landed(1, [u_lat, u_ctx])

    h_lat, h_ctx, hT_ctx = [], [], []
    for d, rev in enumerate((False, True)):
        lr, li = disc[d][0], disc[d][1]
        hcr, hci, tr, ti = _s5_scan(u_ctx, w_b[2 * d], w_b[2 * d + 1], lr, li, zero, zero, zero, zero, reverse=rev,
                                    name=f"s5_scan_ctx_{d}")
        hlr, hli, _, _ = _s5_scan(u_lat, w_b[2 * d], w_b[2 * d + 1], lr, li, tr, ti, zero, zero, reverse=rev,
                                  name=f"s5_scan_lat_{d}")
        h_ctx += [hcr, hci]
        h_lat += [hlr, hli]
        hT_ctx += [tr, ti]
    r5 = _bd_fanin(h_lat, w_c, name="s5_readout")
    (z,) = _rw(_f_s5post, [u_lat, r5], [d_skip], [BF16], name="s5_post")

    arrive(1, [z])
    w_glu, w_ukv, w_mla_o = (gathered[nme] for nme in ('w_glu', 'w_ukv', 'w_mla_o'))
    w_out = gathered['w_out'].reshape(D, D)
    uq3 = _from_col_blocks(gathered['w_uq']).reshape(q_rank, H, QK_NOPE + QK_ROPE)
    w_q2 = jnp.concatenate([uq3, jnp.zeros((q_rank, H, LANES - QK_ROPE), BF16)], axis=2).reshape(q_rank, H * 2 * LANES)
    q2 = _mm(cqn, w_q2, out_dtype=F32, name="q_up")
    (qq,) = _rw(_f_qpost, [q2, cos, sin], [], [BF16], name="q_rope")
    kvn = jnp.concatenate([ckvn_lat, ckvn_ctx], axis=0)
    kr_all = jnp.concatenate([kr_lat, kr_ctx], axis=0)
    kv = _mm(kvn, w_ukv, b_shards=4, out_dtype=BF16, name="kv_up")
    kr_all = kr_all + landed(2, [kv, qq]).astype(BF16)
    o = _attn_fwd(qq, kv, kr_all, name="attn_fwd")

    ab = _mm(z, w_glu, b_shards=4, out_dtype=F32, name="glu_proj")
    bm = _mm(o, w_mla_o, b_shards=4, out_dtype=F32, name="mla_out")
    (mix,) = _rw(_f_merge, [ab, bm, gt], [], [BF16], name="merge")
    out1 = _mm(mix, w_out, out_dtype=F32, name="out_proj")
    x1, xn2 = _rw(_f_resid_norm, [xs, out1], [g1, n2, sc2, sh2], [F32, BF16], name="resid_norm2")
    arrive(2, [xn2])
    w_ffn_in = gathered['w_ffn_in']
    w_ffn_out = gathered['w_ffn_out'].reshape(d_ff, D)
    hmid, ab2 = _ffn_in_swiglu(xn2, w_ffn_in, name="ffn_in")
    f2 = _mm(hmid, w_ffn_out, out_dtype=F32, name="ffn_out")
    (row_loss,) = _rw(_f_final, [x1, f2, tgt], [g2, nf], [F32], name="final_loss")
    loss = lax.psum(jnp.sum(row_loss), ("x", "y", "c"))

    ones = jnp.ones((L, 1), F32)
    (dx1_a, df2), (dg2, dnf) = _rw_vjp(_f_final, [x1, f2, tgt], [g2, nf], [[ones]], [True, True, False], [True, True],
                                       [F32, BF16], name="final_loss_bwd")
    gw_ffn_out = _mm(hmid, df2, ta=True, out_dtype=BF16, name="ffn_out_dw")
    dab2 = _ffn_out_dx_swiglu(df2, w_ffn_out, ab2, name="ffn_out_dx")
    dxn2 = _mm(dab2, w_ffn_in, tb=True, a_shards=2, b_shards=4, out_dtype=F32, name="ffn_in_dx")
    gw_ffn_in = _mm(xn2, dab2, ta=True, b_shards=2, out_shards=4, out_dtype=BF16, name="ffn_in_dw")
    rs_ffn, tok = _rs_stage1([gw_ffn_out.reshape(4, -1, D), gw_ffn_in], "ffn")
    (dx_a, dout1), (dg1, dn2, dsc2, dsh2) = _rw_vjp(
        _f_resid_norm, [xs, out1], [g1, n2 + tok[0, 0], sc2, sh2], [[dx1_a], [dxn2]], [True, True], [True] * 4, [F32, BF16],
        name="resid_norm2_bwd")
    dmix = _mm(dout1, w_out, tb=True, out_dtype=F32, name="out_proj_dx")
    rs_ffn, tok = _rs_stage2(rs_ffn, [dmix], "ffn")
    gw_out = _mm(mix, dout1, ta=True, out_dtype=BF16, name="out_proj_dw")
    (dab, dbm, dgt), _ = _rw_vjp(_f_merge, [ab, bm, gt], [], [[dmix]], [True] * 3, [], [BF16] * 3, name="merge_bwd",
                                 anchor=tok)
    dz = _mm(dab, w_glu, tb=True, b_shards=4, out_dtype=F32, name="glu_proj_dx")
    gw_glu = _mm(z, dab, ta=True, out_shards=4, out_dtype=BF16, name="glu_proj_dw")
    do = _mm(dbm, w_mla_o, tb=True, b_shards=4, out_dtype=BF16, name="mla_out_dx")
    gw_mla_o = _mm(o, dbm, ta=True, out_shards=4, out_dtype=BF16, name="mla_out_dw")
    dxn_g = _mm(dgt, w_g, tb=True, out_dtype=F32, name="in_gates_dx")
    gw_g = _mm(xn_lat, dgt, ta=True, out_dtype=BF16, name="in_gates_dw")
    rs_mid, tok = _rs_stage1([gw_out.reshape(4, -1, D), gw_glu, gw_mla_o], "mid", after=[gw_g])

    (du_a, dr5), (dd_skip,) = _rw_vjp(_f_s5post, [u_lat, r5], [d_skip + tok[0, 0]], [[dz]], [True, True], [True], [F32, F32],
                                      name="s5_post_bwd")
    dw_c = _bd_dw(h_lat, [dr5] * 4, nb_out, name="s5_readout_dw")
    rs_mid, tok = _rs_stage2(rs_mid, dw_c[:1], "mid")
    zero = zero + tok[0, 0]
    w_ct = _tr(w_c)
    zeros_ctx = jnp.zeros((Lc, SW), BF16)
    mu_lat, mu_ctx, dlam = [], [], []
    for d, rev in enumerate((False, True)):
        lr, li = disc[d][0], disc[d][1]
        mlr, mli, fr, fi = _s5_scan(dr5, w_ct[2 * d], w_ct[2 * d + 1], lr, -li, zero, zero, zero, zero, reverse=not rev,
                                    name=f"s5_adj_lat_{d}")
        dh0r, dh0i = _cmul(lr, -li, fr, fi)
        mcr, mci, _, _ = _s5_scan(zeros_ctx, w_ct[2 * d], w_ct[2 * d + 1], lr, -li, zero, zero, dh0r, dh0i,
                                  reverse=not rev, name=f"s5_adj_ctx_{d}")
        dl_lat = _s5_dlam(mlr, mli, h_lat[2 * d], h_lat[2 * d + 1], hT_ctx[2 * d], hT_ctx[2 * d + 1], reverse=rev,
                          name=f"s5_dlam_lat_{d}")
        dl_ctx = _s5_dlam(mcr, mci, h_ctx[2 * d], h_ctx[2 * d + 1], zero, zero, reverse=rev, name=f"s5_dlam_ctx_{d}")
        mu_lat += [mlr, mli]
        mu_ctx += [mcr, mci]
        dlam.append((dl_lat[0] + dl_ctx[0], dl_lat[1] + dl_ctx[1]))
    du_b = _bd_fanin(mu_lat, _tr(w_b), name="s5_bu_lat_dx")
    du_ctx = _bd_fanin(mu_ctx, _tr(w_b), name="s5_bu_ctx_dx")
    dw_b_lat = _bd_dw([u_lat] * 4, mu_lat, nb_in, name="s5_bu_lat_dw")
    dw_b_ctx = _bd_dw([u_ctx] * 4, mu_ctx, nb_in, name="s5_bu_ctx_dw")
    g_s5 = {}
    for d in range(2):
        ct = (dlam[d][0], dlam[d][1], dw_b_lat[2 * d] + dw_b_ctx[2 * d], dw_b_lat[2 * d + 1] + dw_b_ctx[2 * d + 1])
        ga_re, ga_im, gdt, gb_re, gb_im = vjp_disc[d](ct)
        _, vj_c = jax.vjp(lambda cr, ci: (_diag_blocks_out(cr, gpo), -_diag_blocks_out(ci, gpo)),
                          w['s5_c_re'][0, d], w['s5_c_im'][0, d])
        gc_re, gc_im = vj_c((dw_c[2 * d], dw_c[2 * d + 1]))
        for nme, val in (('s5_a_re', ga_re), ('s5_a_im', ga_im), ('s5_log_dt', gdt), ('s5_b_re', gb_re),
                         ('s5_b_im', gb_im), ('s5_c_re', gc_re), ('s5_c_im', gc_im)):
            g_s5.setdefault(nme, []).append(val)
    g_small = {nme: jnp.stack(vals)[None] for nme, vals in g_s5.items()}
    g_small['s5_d'] = dd_skip.reshape(w['s5_d'].shape)

    dqq, dkv, dkr = _attn_bwd(qq, kv, kr_all, do, name="attn_bwd")
    (dq2,), _ = _rw_vjp(_f_qpost, [q2, cos, sin], [], [[dqq]], [True, False, False], [], [BF16], name="q_rope_bwd")
    dcqn = _mm(dq2, w_q2, tb=True, out_dtype=F32, name="q_up_dx")
    gw_q2 = _mm(cqn, dq2, ta=True, out_dtype=BF16, name="q_up_dw")
    dckvn = _mm(dkv, w_ukv, tb=True, b_shards=4, out_dtype=F32, name="kv_up_dx")
    gw_ukv = _mm(kvn, dkv, ta=True, out_shards=4, out_dtype=BF16, name="kv_up_dw")
    gw_uq = gw_q2.reshape(q_rank, H, 2 * LANES)[:, :, :QK_NOPE + QK_ROPE].reshape(q_rank, H * (QK_NOPE + QK_ROPE))
    rs_kv, tok = _rs_stage1([_col_blocks(gw_uq, 4), gw_ukv], "kv")

    (dha_lat,), (dqg, dkvg_lat) = _rw_vjp(
        f_post_lat, [ha_lat, cos, sin], [qg, kvg + tok[0, 0]], [[du_a, du_b], [dcqn], [dckvn[:L]], [dkr[:L]]],
        [True, False, False], [True, True], [BF16], name="post_in_lat_bwd")
    (dha_ctx,), (dkvg_ctx,) = _rw_vjp(f_post_ctx, [ha_ctx], [kvg], [[du_ctx], [dckvn[L:]], [dkr[L:]]], [True], [True],
                                      [BF16], name="post_in_ctx_bwd")
    dha = jnp.concatenate([dha_lat, dha_ctx], axis=0)
    dxn = _mm(dha, w_a, tb=True, out_dtype=F32, name="in_proj_dx")
    gw_a = _mm(xn, dha, ta=True, out_dtype=BF16, name="in_proj_dw")
    rs_kv, tok = _rs_stage2(rs_kv, [gw_a], "kv")
    (dx_seg,), (dn1_lat, dsc1, dsh1) = _rw_vjp(
        _f_norm_mod_keep, [xs], [n1 + tok[0, 0], sc1, sh1], [[dxn[:L], dxn_g], [dx_a]], [True], [True] * 3, [F32],
        name="norm1_lat_bwd")
    _, (dn1_ctx, dcsc1, dcsh1) = _rw_vjp(_f_norm_mod, [cs], [n1, csc1, csh1], [[dxn[L:]]], [False], [True] * 3, [],
                                         name="norm1_ctx_bwd")
    grad_x = _from_segments(dx_seg)[None]
    g_small.update(norm1=dn1_lat + dn1_ctx, norm2=dn2, q_norm=dqg, kv_norm=dkvg_lat + dkvg_ctx, norm_f=dnf.reshape(D))
    gw_in = jnp.concatenate([gw_a[:, :wa_used], gw_g], axis=1)
    small_vals = [g_small[nme] for nme in SMALL]
    n_small = sum(val.size for val in small_vals)
    small_rows = -(-n_small // (LANES * 4 * 32)) * 32

    zD = jnp.zeros((1, D), F32)
    dm = jnp.concatenate([
        jnp.concatenate([dsh1, dsc1, dg1, dsh2, dsc2, dg2], axis=1),
        jnp.concatenate([dcsh1, dcsc1, zD, zD, zD, zD], axis=1),
    ], axis=0)
    dm_all = _allgather8(_pad_rows(dm, SUBLANES), name="ag_dmod")
    rs_in, tok = _rs_stage1([_col_blocks(gw_in, 4), _pack(small_vals, LANES, 4 * small_rows).reshape(4, small_rows, LANES)],
                            "in", after=[dm_all])
    dm_ctx = dm_all[0, 1] + tok[0, 0]
    for k in range(1, 8):
        dm_ctx = dm_ctx + dm_all[k, 1]
    dmod = _pad_rows(jnp.concatenate([dm_all[:, 0, :], dm_ctx[None]], axis=0), 16)
    g_b_mod = jnp.sum(dmod, axis=0, keepdims=True)
    dmod_mine = lax.dynamic_slice_in_dim(dmod, me_chip * cs_mod, cs_mod, axis=1)
    g_w_mod = _mm(act, dmod_mine, ta=True, out_dtype=F32, name="mod_dw")
    dact_part = _mm(dmod_mine, w_mod, tb=True, out_dtype=F32, name="mod_dx")
    dact_all = _allgather8(dact_part, name="ag_dact")
    dact = dact_all[0] + dact_all[2] + dact_all[4] + dact_all[6]
    (dcond_rows,), _ = _rw_vjp(lambda t: (jax.nn.silu(t),), [cond], [], [[dact]], [True], [], [F32], name="cond_silu_bwd")
    g_c_ctx = dcond_rows[8]

    rs_in, tok = _rs_stage2(rs_in, [g_c_ctx], "in")

    grads, delta, new_m, new_v = {}, {}, {}, {}

    def update(members, reds, anchor):
        deltas = []
        for nme, red in zip(members, reds):
            res = _adamw(w[nme][0], red, m[nme][0], v[nme][0], name=f"adamw_{nme}", anchor=anchor)
            grads[nme], delta[nme], new_m[nme], new_v[nme] = (r.reshape(w[nme].shape) for r in res)
            deltas.append(res[1])
            anchor = None
        return deltas

    rs_ffn, tok = _rs_stage3(rs_ffn, [tok], "ffn")
    done = update(['w_mod'], [g_w_mod], tok)
    red_ffn = _rs_stage4(rs_ffn, done, "ffn")
    rs_mid, tok = _rs_stage3(rs_mid, red_ffn[:1], "mid")
    done = update(['w_ffn_out', 'w_ffn_in'], red_ffn, tok)
    red_mid = _rs_stage4(rs_mid, done, "mid")
    rs_kv, tok = _rs_stage3(rs_kv, red_mid[:1], "kv")
    done = update(['w_out', 'w_glu', 'w_mla_o'], red_mid, tok)
    red_kv = _rs_stage4(rs_kv, done, "kv")
    rs_in, tok = _rs_stage3(rs_in, red_kv[:1], "in")
    done = update(['w_uq', 'w_ukv'], red_kv, tok)
    red_in = _rs_stage4(rs_in, done, "in")
    update(['w_in'], red_in[:1], None)
    small_mine = red_in[-1]
    small_buf = _into_slot(small_mine, me_chip, 4, F32, name="small_grads_slot")
    small_all = _allgather_shards([small_buf], name="ag_small_grads")[0].reshape(4 * small_rows, LANES)
    g_small_red = dict(zip(SMALL, _unpack(small_all, [w[nme] for nme in SMALL])))
    rest = SMALL + ['c_ctx', 'b_mod']
    g_rest = dict(g_small_red, c_ctx=g_c_ctx, b_mod=g_b_mod)
    rows_rest = -(-sum(w[nme].size for nme in rest) // (LANES * 16)) * 16
    packed = [_pack([src[nme] for nme in rest], LANES, rows_rest) for src in (w, g_rest, m, v)]
    res = _adamw(*packed, name="adamw_small")
    for dst, buf in zip((grads, delta, new_m, new_v), res):
        dst.update(zip(rest, _unpack(buf, [w[nme] for nme in rest])))
    return (loss, grad_x, *[grads[nme] for nme in WEIGHTS], *[delta[nme] for nme in WEIGHTS],
            *[new_m[nme] for nme in WEIGHTS], *[new_v[nme] for nme in WEIGHTS])


def kernel(x, c, ctx, c_ctx, w_mod, b_mod, norm1, norm2, w_in, s5_a_re, s5_a_im, s5_log_dt, s5_b_re, s5_b_im, s5_c_re, s5_c_im, s5_d, w_glu, q_norm, kv_norm, w_uq, w_ukv, w_mla_o, w_out, w_ffn_in, w_ffn_out, norm_f, loss_target, m_c_ctx, m_w_mod, m_b_mod, m_norm1, m_norm2, m_w_in, m_s5_a_re, m_s5_a_im, m_s5_log_dt, m_s5_b_re, m_s5_b_im, m_s5_c_re, m_s5_c_im, m_s5_d, m_w_glu, m_q_norm, m_kv_norm, m_w_uq, m_w_ukv, m_w_mla_o, m_w_out, m_w_ffn_in, m_w_ffn_out, m_norm_f, v_c_ctx, v_w_mod, v_b_mod, v_norm1, v_norm2, v_w_in, v_s5_a_re, v_s5_a_im, v_s5_log_dt, v_s5_b_re, v_s5_b_im, v_s5_c_re, v_s5_c_im, v_s5_d, v_w_glu, v_q_norm, v_kv_norm, v_w_uq, v_w_ukv, v_w_mla_o, v_w_out, v_w_ffn_in, v_w_ffn_out, v_norm_f):
    w = dict(c_ctx=c_ctx, w_mod=w_mod, b_mod=b_mod, norm1=norm1, norm2=norm2, w_in=w_in, s5_a_re=s5_a_re, s5_a_im=s5_a_im,
             s5_log_dt=s5_log_dt, s5_b_re=s5_b_re, s5_b_im=s5_b_im, s5_c_re=s5_c_re, s5_c_im=s5_c_im, s5_d=s5_d, w_glu=w_glu,
             q_norm=q_norm, kv_norm=kv_norm, w_uq=w_uq, w_ukv=w_ukv, w_mla_o=w_mla_o, w_out=w_out, w_ffn_in=w_ffn_in,
             w_ffn_out=w_ffn_out, norm_f=norm_f)
    m = dict(c_ctx=m_c_ctx, w_mod=m_w_mod, b_mod=m_b_mod, norm1=m_norm1, norm2=m_norm2, w_in=m_w_in, s5_a_re=m_s5_a_re,
             s5_a_im=m_s5_a_im, s5_log_dt=m_s5_log_dt, s5_b_re=m_s5_b_re, s5_b_im=m_s5_b_im, s5_c_re=m_s5_c_re,
             s5_c_im=m_s5_c_im, s5_d=m_s5_d, w_glu=m_w_glu, q_norm=m_q_norm, kv_norm=m_kv_norm, w_uq=m_w_uq, w_ukv=m_w_ukv,
             w_mla_o=m_w_mla_o, w_out=m_w_out, w_ffn_in=m_w_ffn_in, w_ffn_out=m_w_ffn_out, norm_f=m_norm_f)
    v = dict(c_ctx=v_c_ctx, w_mod=v_w_mod, b_mod=v_b_mod, norm1=v_norm1, norm2=v_norm2, w_in=v_w_in, s5_a_re=v_s5_a_re,
             s5_a_im=v_s5_a_im, s5_log_dt=v_s5_log_dt, s5_b_re=v_s5_b_re, s5_b_im=v_s5_b_im, s5_c_re=v_s5_c_re,
             s5_c_im=v_s5_c_im, s5_d=v_s5_d, w_glu=v_w_glu, q_norm=v_q_norm, kv_norm=v_kv_norm, w_uq=v_w_uq, w_ukv=v_w_ukv,
             w_mla_o=v_w_mla_o, w_out=v_w_out, w_ffn_in=v_w_ffn_in, w_ffn_out=v_w_ffn_out, norm_f=v_norm_f)
    return _step(x, c, ctx, loss_target, w, m, v)
```

```python
import functools
import math

import jax
import jax.numpy as jnp
from jax import lax
from jax.experimental import pallas as pl
from jax.experimental.pallas import tpu as pltpu

F32 = jnp.float32
BF16 = jnp.bfloat16

EPS = 1e-6
GRID_W = 64
S5_GROUP = 16
S5_STATE = 64
MLA_HEADS = 8
QK_NOPE = 128
QK_ROPE = 64
V_DIM = 128
ROPE_BASE = 10000.0
ATTN_SCALE = (QK_NOPE + QK_ROPE) ** -0.5
ADAM_LR = 0.001
ADAM_B1 = 0.9
ADAM_B2 = 0.999
ADAM_EPS = 1e-08
ADAM_WD = 0.01
ADAM_STEP = 10

SUBLANES = 8
LANES = 128
V7X_VMEM_BYTES = 64 * 1024 * 1024
VMEM_LIMIT = (V7X_VMEM_BYTES * 7) // 8
N_SEG = 2 * SUBLANES
S5_BLOCK_GROUPS = 8
MESH = pl.DeviceIdType.MESH


def _pick(n, target, mult):
    best = None
    d = mult
    while d <= min(n, target):
        if n % d == 0:
            best = d
        d += mult
    return n if best is None else best


def _cparams(sem=None):
    return pltpu.CompilerParams(dimension_semantics=sem, vmem_limit_bytes=VMEM_LIMIT)


MM_VMEM_BUDGET = (V7X_VMEM_BYTES * 5) // 8


def _mm(a, b, *, ta=False, tb=False, out_dtype=F32, name, a_shards=1, b_shards=1, out_shards=1):
    if ta:
        K, M = a.shape
    else:
        M, K = a.shape[-2], a.shape[-1] * a_shards
    if tb:
        N, K2 = b.shape[-2], b.shape[-1] * b_shards
    else:
        K2, N = b.shape[-2], b.shape[-1] * b_shards
    assert K == K2, (a.shape, b.shape, ta, tb)
    n_unit = N // max(out_shards, 1 if tb else b_shards)
    k_unit = K // max(a_shards, b_shards if tb else 1)
    tn = _pick(n_unit, 1024, LANES)
    tm = _pick(M, 1024 if tn >= 512 else 2048, LANES if ta else 16)
    sa, sb, so = a.dtype.itemsize, b.dtype.itemsize, jnp.dtype(out_dtype).itemsize
    k_mult = LANES if (not ta or tb) else 16
    tk = k_mult if k_unit % k_mult == 0 else k_unit
    for cand in range(k_mult, k_unit + 1, k_mult):
        if k_unit % cand == 0 and 2 * cand * (tm * sa + tn * sb) + tm * tn * (4 + 2 * so) <= MM_VMEM_BUDGET:
            tk = cand
    nk = K // tk
    dims = (((0 if ta else 1,), (1 if tb else 0,)), ((), ()))

    def body(a_ref, b_ref, o_ref, *scratch):
        part = lax.dot_general(a_ref[...].astype(BF16), b_ref[...].astype(BF16), dims, preferred_element_type=F32)
        if nk == 1:
            o_ref[...] = part.astype(o_ref.dtype)
            return
        acc_ref, = scratch
        k = pl.program_id(2)

        @pl.when(k == 0)
        def _():
            acc_ref[...] = part

        @pl.when(k > 0)
        def _():
            acc_ref[...] += part

        @pl.when(k == nk - 1)
        def _():
            o_ref[...] = acc_ref[...].astype(o_ref.dtype)

    if ta:
        a_spec = pl.BlockSpec((tk, tm), lambda i, j, k: (k, i))
    elif a_shards == 1:
        a_spec = pl.BlockSpec((tm, tk), lambda i, j, k: (i, k))
    else:
        akb = (K // a_shards) // tk
        a_spec = pl.BlockSpec((None, tm, tk), lambda i, j, k: (k // akb, i, k % akb))
    if b_shards == 1:
        b_spec = pl.BlockSpec((tn, tk), lambda i, j, k: (j, k)) if tb else pl.BlockSpec((tk, tn), lambda i, j, k: (k, j))
    elif tb:
        kpb = (K // b_shards) // tk
        b_spec = pl.BlockSpec((None, tn, tk), lambda i, j, k: (k // kpb, j, k % kpb))
    else:
        npb = (N // b_shards) // tn
        b_spec = pl.BlockSpec((None, tk, tn), lambda i, j, k: (j // npb, k, j % npb))
    if out_shards == 1:
        out_spec = pl.BlockSpec((tm, tn), lambda i, j, k: (i, j))
        out_shape = jax.ShapeDtypeStruct((M, N), out_dtype)
    else:
        opb = (N // out_shards) // tn
        out_spec = pl.BlockSpec((None, tm, tn), lambda i, j, k: (j // opb, i, j % opb))
        out_shape = jax.ShapeDtypeStruct((out_shards, M, N // out_shards), out_dtype)
    return pl.pallas_call(
        body, name=name, grid=(M // tm, N // tn, nk),
        in_specs=[a_spec, b_spec], out_specs=out_spec, out_shape=out_shape,
        scratch_shapes=[pltpu.VMEM((tm, tn), F32)] if nk > 1 else [],
        compiler_params=_cparams(("parallel", "parallel", "arbitrary")),
    )(a, b)


FFN_TILE_ROWS = 1024


def _ffn_in_swiglu(x, w4, *, name):
    M, K = x.shape
    S, _, ns = w4.shape
    half = S * ns // 2
    tn = _pick(ns, 512, LANES)
    tm = _pick(M, FFN_TILE_ROWS, 16)
    npb = ns // tn

    def body(x_ref, wa_ref, wb_ref, h_ref, ab_ref):
        xb = x_ref[...].astype(BF16)
        a = jnp.dot(xb, wa_ref[...].astype(BF16), preferred_element_type=F32)
        b = jnp.dot(xb, wb_ref[...].astype(BF16), preferred_element_type=F32)
        h_ref[...] = (jax.nn.silu(a) * b).astype(h_ref.dtype)
        ab_ref[0] = a.astype(ab_ref.dtype)
        ab_ref[1] = b.astype(ab_ref.dtype)

    return pl.pallas_call(
        body, name=name, grid=(M // tm, half // tn),
        in_specs=[pl.BlockSpec((tm, K), lambda i, j: (i, 0)),
                  pl.BlockSpec((None, K, tn), lambda i, j: (j // npb, 0, j % npb)),
                  pl.BlockSpec((None, K, tn), lambda i, j: (S // 2 + j // npb, 0, j % npb))],
        out_specs=[pl.BlockSpec((tm, tn), lambda i, j: (i, j)), pl.BlockSpec((2, tm, tn), lambda i, j: (0, i, j))],
        out_shape=[jax.ShapeDtypeStruct((M, half), BF16), jax.ShapeDtypeStruct((2, M, half), BF16)],
        compiler_params=_cparams(("parallel", "parallel")),
    )(x, w4, w4)


def _ffn_out_dx_swiglu(dy, w, ab, *, name):
    M, D = dy.shape
    n2 = w.shape[0]
    tn = _pick(n2, 512, LANES)
    tm = _pick(M, FFN_TILE_ROWS, 16)

    def body(dy_ref, w_ref, ab_ref, o_ref):
        dh = lax.dot_general(dy_ref[...].astype(BF16), w_ref[...].astype(BF16), NT_DIMS, preferred_element_type=F32)
        a, b = ab_ref[0].astype(F32), ab_ref[1].astype(F32)
        s = jax.nn.sigmoid(a)
        o_ref[0] = (dh * b * (s * (1.0 + a * (1.0 - s)))).astype(o_ref.dtype)
        o_ref[1] = (dh * (a * s)).astype(o_ref.dtype)

    return pl.pallas_call(
        body, name=name, grid=(M // tm, n2 // tn),
        in_specs=[pl.BlockSpec((tm, D), lambda i, j: (i, 0)), pl.BlockSpec((tn, D), lambda i, j: (j, 0)),
                  pl.BlockSpec((2, tm, tn), lambda i, j: (0, i, j))],
        out_specs=pl.BlockSpec((2, tm, tn), lambda i, j: (0, i, j)),
        out_shape=jax.ShapeDtypeStruct((2, M, n2), BF16),
        compiler_params=_cparams(("parallel", "parallel")),
    )(dy, w, ab)


ROW_TILE_BYTES = 6 * 1024 * 1024
STREAM_TILE_BYTES = 14 * 1024 * 1024


def _row_tile(tiled, extra_bytes=0, budget=ROW_TILE_BYTES):
    rows = tiled[0].shape[0]
    per_row = sum(a.shape[1] * 4 for a in tiled) + extra_bytes
    target = max(SUBLANES, budget // max(per_row, 1))
    return _pick(rows, min(target, 512), 16)


def _rw(f, tiled, bcast, out_dtypes, *, name, anchor=None, tile_bytes=ROW_TILE_BYTES):
    nt, nb = len(tiled), len(bcast)
    rows = tiled[0].shape[0]
    outs_aval = jax.eval_shape(f, *[jax.ShapeDtypeStruct((16, a.shape[1]), F32) for a in tiled],
                               *[jax.ShapeDtypeStruct(b.shape, F32) for b in bcast])
    widths = [o.shape[1] for o in outs_aval]
    tm = _row_tile(tiled, sum(w * 4 for w in widths), tile_bytes)

    extra = [] if anchor is None else [anchor]
    n_in = nt + nb + len(extra)

    def body(*refs):
        tin = [r[...].astype(F32) for r in refs[:nt]]
        bin_ = [r[...].astype(F32) for r in refs[nt:nt + nb]]
        outs = f(*tin, *bin_)
        for o_ref, o in zip(refs[n_in:], outs):
            o_ref[...] = o.astype(o_ref.dtype)

    in_specs = [pl.BlockSpec((tm, a.shape[1]), lambda i: (i, 0)) for a in tiled]
    in_specs += [pl.BlockSpec(b.shape, lambda i: (0, 0)) for b in bcast + extra]
    res = pl.pallas_call(
        body, name=name, grid=(rows // tm,), in_specs=in_specs,
        out_specs=[pl.BlockSpec((tm, w), lambda i: (i, 0)) for w in widths],
        out_shape=[jax.ShapeDtypeStruct((rows, w), dt) for w, dt in zip(widths, out_dtypes)],
        compiler_params=_cparams(("parallel",)),
    )(*tiled, *bcast, *extra)
    return list(res)


def _rw_vjp(f, tiled, bcast, cts, need_t, need_b, t_dtypes, *, name, anchor=None):
    nt, nb = len(tiled), len(bcast)
    rows = tiled[0].shape[0]
    flat_cts = [c for group in cts for c in group]
    t_idx = [i for i in range(nt) if need_t[i]]
    b_idx = [i for i in range(nb) if need_b[i]]
    tm = _row_tile(list(tiled) + flat_cts, sum(tiled[i].shape[1] * 4 for i in t_idx))
    nc = len(flat_cts)
    extra = [] if anchor is None else [anchor]

    def body(*refs):
        i = pl.program_id(0)
        tin = [r[...].astype(F32) for r in refs[:nt]]
        bin_ = [r[...].astype(F32) for r in refs[nt:nt + nb]]
        ct_refs = refs[nt + nb:nt + nb + nc]
        out_refs = refs[nt + nb + nc + len(extra):]
        outs, vjp_fn = jax.vjp(f, *tin, *bin_)
        ct_vals, pos = [], 0
        for o, group in zip(outs, cts):
            acc = jnp.zeros_like(o)
            for _ in group:
                acc = acc + ct_refs[pos][...].astype(F32)
                pos += 1
            ct_vals.append(acc)
        grads = vjp_fn(tuple(ct_vals))
        for o_ref, k in zip(out_refs[:len(t_idx)], t_idx):
            o_ref[...] = grads[k].astype(o_ref.dtype)
        for o_ref, k in zip(out_refs[len(t_idx):], b_idx):
            @pl.when(i == 0)
            def _(o_ref=o_ref):
                o_ref[...] = jnp.zeros_like(o_ref)

            o_ref[...] += grads[nt + k]

    in_specs = [pl.BlockSpec((tm, a.shape[1]), lambda i: (i, 0)) for a in tiled]
    in_specs += [pl.BlockSpec(b.shape, lambda i: (0, 0)) for b in bcast]
    in_specs += [pl.BlockSpec((tm, c.shape[1]), lambda i: (i, 0)) for c in flat_cts]
    in_specs += [pl.BlockSpec(e.shape, lambda i: (0, 0)) for e in extra]
    out_specs = [pl.BlockSpec((tm, tiled[k].shape[1]), lambda i: (i, 0)) for k in t_idx]
    out_specs += [pl.BlockSpec(bcast[k].shape, lambda i: (0, 0)) for k in b_idx]
    out_shape = [jax.ShapeDtypeStruct(tiled[k].shape, dt) for k, dt in zip(t_idx, t_dtypes)]
    out_shape += [jax.ShapeDtypeStruct(bcast[k].shape, F32) for k in b_idx]
    res = pl.pallas_call(
        body, name=name, grid=(rows // tm,), in_specs=in_specs, out_specs=out_specs, out_shape=out_shape,
        compiler_params=_cparams(("arbitrary",)),
    )(*tiled, *bcast, *flat_cts, *extra)
    res = list(res)
    return res[:len(t_idx)], res[len(t_idx):]


def _rms(x, g):
    return x * lax.rsqrt(jnp.mean(x * x, axis=-1, keepdims=True) + EPS) * g


def _f_norm_mod(x, g, sc, sh):
    return (_rms(x, g) * (1.0 + sc) + sh,)


def _f_norm_mod_keep(x, g, sc, sh):
    return (_rms(x, g) * (1.0 + sc) + sh, x)


@jax.custom_vjp
def _swap16(x):
    w = x.shape[-1]
    lane = lax.broadcasted_iota(jnp.int32, x.shape, x.ndim - 1)
    return jnp.where((lane & 16) == 0, pltpu.roll(x, w - 16, x.ndim - 1), pltpu.roll(x, 16, x.ndim - 1))


_swap16.defvjp(lambda x: (_swap16(x), None), lambda _, g: (_swap16(g),))


def _rope(x, cos, sin):
    return x * cos + _swap16(x) * sin


def _make_f_post_in(sw, q_rank, kv_rank, with_q):
    o1, o2, o3 = sw, sw + q_rank, sw + q_rank + kv_rank

    if with_q:
        def f(ha, cos, sin, qg, kvg):
            u = ha[:, :o1]
            cqn = _rms(ha[:, o1:o2], qg)
            ckvn = _rms(ha[:, o2:o3], kvg)
            kr = _rope(ha[:, o3:o3 + LANES], cos, sin)
            return u, cqn, ckvn, kr
    else:
        def f(ha, kvg):
            return ha[:, :o1], _rms(ha[:, o2:o3], kvg), ha[:, o3:o3 + LANES]
    return f


def _f_qpost(q2, cos, sin):
    parts = []
    for h in range(q2.shape[1] // (2 * LANES)):
        o = 2 * LANES * h
        parts += [q2[:, o:o + LANES], _rope(q2[:, o + LANES:o + 2 * LANES], cos, sin)]
    return (jnp.concatenate(parts, axis=1),)


def _f_s5post(u, r0, r1, d):
    return (jax.nn.gelu(d * u + r0 + r1, approximate=True),)


def _f_merge(ab, bm, gt):
    d = bm.shape[1]
    br_s5 = ab[:, :d] * jax.nn.sigmoid(ab[:, d:])
    g = jax.nn.sigmoid(gt)
    return (g[:, :d] * br_s5 + g[:, d:] * bm,)


def _f_resid_norm(x, out, g1, n2, sc2, sh2):
    x1 = x + g1 * out
    return x1, _rms(x1, n2) * (1.0 + sc2) + sh2


def _f_final(x1, f, tgt, g2, nf):
    y = _rms(x1 + g2 * f, nf)
    return (0.5 * jnp.mean(jnp.square(y - tgt), axis=-1, keepdims=True),)


def _bd_fanin(xs, ws, *, name):
    nw = len(ws)
    nb, kb, nn = ws[0].shape
    T = xs[0].shape[0]
    tm = _pick(T, 512, 16)

    def body(*refs):
        acc = None
        for x_ref, w_ref in zip(refs[:nw], refs[nw:2 * nw]):
            t = jnp.dot(x_ref[...].astype(BF16), w_ref[0].astype(BF16), preferred_element_type=F32)
            acc = t if acc is None else acc + t
        refs[2 * nw][...] = acc

    return pl.pallas_call(
        body, name=name, grid=(nb, T // tm),
        in_specs=[pl.BlockSpec((tm, kb), lambda j, i: (i, j))] * nw + [pl.BlockSpec((1, kb, nn), lambda j, i: (j, 0, 0))] * nw,
        out_specs=pl.BlockSpec((tm, nn), lambda j, i: (i, j)),
        out_shape=jax.ShapeDtypeStruct((T, nb * nn), F32),
        compiler_params=_cparams(("parallel", "parallel")),
    )(*xs, *ws)


def _bd_dw(xs, dys, nb, *, name):
    npair = len(xs)
    T = xs[0].shape[0]
    kb = xs[0].shape[1] // nb
    nn = dys[0].shape[1] // nb
    tm = _pick(T, 512, 16)
    dims = (((0,), (0,)), ((), ()))

    def body(*refs):
        i = pl.program_id(1)
        for x_ref, d_ref, o_ref in zip(refs[:npair], refs[npair:2 * npair], refs[2 * npair:]):
            @pl.when(i == 0)
            def _(o_ref=o_ref):
                o_ref[...] = jnp.zeros_like(o_ref)

            o_ref[0] += lax.dot_general(x_ref[...].astype(BF16), d_ref[...].astype(BF16), dims,
                                        preferred_element_type=F32)

    return list(pl.pallas_call(
        body, name=name, grid=(nb, T // tm),
        in_specs=[pl.BlockSpec((tm, kb), lambda j, i: (i, j))] * npair + [pl.BlockSpec((tm, nn), lambda j, i: (i, j))] * npair,
        out_specs=[pl.BlockSpec((1, kb, nn), lambda j, i: (j, 0, 0))] * npair,
        out_shape=[jax.ShapeDtypeStruct((nb, kb, nn), F32)] * npair,
        compiler_params=_cparams(("parallel", "arbitrary")),
    )(*xs, *dys))


def _cmul(ar, ai, br, bi):
    return ar * br - ai * bi, ar * bi + ai * br


def _cpow(lr, li, n):
    rr, ri = None, None
    br, bi = lr, li
    while n:
        if n & 1:
            rr, ri = (br, bi) if rr is None else _cmul(rr, ri, br, bi)
        n >>= 1
        if n:
            br, bi = _cmul(br, bi, br, bi)
    return rr, ri


SCAN_MM_ROWS = 512


def _s5_scan(x, w_re, w_im, lam_re, lam_im, h0_re, h0_im, e0_re, e0_im, *, reverse, name, readout=None):
    rows = x.shape[0]
    nb, kb, cb = w_re.shape
    C = nb * cb
    n = rows // N_SEG
    mm_rows = _pick(rows, SCAN_MM_ROWS, 16)
    seg_order = list(range(N_SEG))[::-1] if reverse else list(range(N_SEG))
    s_first, s_last = seg_order[0], seg_order[-1]
    n_ro = 0 if readout is None else 2

    def body(x_ref, wr_ref, wi_ref, lr_ref, li_ref, h0r_ref, h0i_ref, e0r_ref, e0i_ref, *rest):
        ro_refs, (hr_ref, hi_ref, htr_ref, hti_ref), y_refs = rest[:n_ro], rest[n_ro:n_ro + 4], rest[n_ro + 4:-2]
        locr_ref, loci_ref = rest[-2:]
        shape = (N_SEG, cb)
        lr = jnp.broadcast_to(lr_ref[...], shape)
        li = jnp.broadcast_to(li_ref[...], shape)
        row = lax.broadcasted_iota(jnp.int32, shape, 0)

        def step_of(k):
            return (n - 1 - k) if reverse else k

        def rows_of(k):
            return pl.ds(pl.multiple_of(step_of(k) * N_SEG, N_SEG), N_SEG)

        wr, wi = wr_ref[...].astype(BF16), wi_ref[...].astype(BF16)
        for r0 in range(0, rows, mm_rows):
            xb = x_ref[r0:r0 + mm_rows, :].astype(BF16)
            locr_ref[r0:r0 + mm_rows, :] = jnp.dot(xb, wr, preferred_element_type=F32)
            loci_ref[r0:r0 + mm_rows, :] = jnp.dot(xb, wi, preferred_element_type=F32)

        first = row == s_first
        hr = locr_ref[rows_of(0), :] + jnp.where(first, e0r_ref[...], 0.0)
        hi = loci_ref[rows_of(0), :] + jnp.where(first, e0i_ref[...], 0.0)
        locr_ref[rows_of(0), :] = hr
        loci_ref[rows_of(0), :] = hi

        def pass1(k, carry):
            hr, hi = carry
            pr, pi = _cmul(lr, li, hr, hi)
            hr = pr + locr_ref[rows_of(k), :]
            hi = pi + loci_ref[rows_of(k), :]
            locr_ref[rows_of(k), :] = hr
            loci_ref[rows_of(k), :] = hi
            return hr, hi

        er, ei = lax.fori_loop(1, n, pass1, (hr, hi))

        lnr, lni = _cpow(lr_ref[...], li_ref[...], n)
        cr, ci = h0r_ref[...], h0i_ref[...]
        cin_r = jnp.zeros(shape, F32)
        cin_i = jnp.zeros(shape, F32)
        for s in seg_order:
            cin_r = jnp.where(row == s, cr, cin_r)
            cin_i = jnp.where(row == s, ci, cin_i)
            if s != s_last:
                pr, pi = _cmul(lnr, lni, cr, ci)
                cr = pr + jnp.sum(jnp.where(row == s, er, 0.0), axis=0, keepdims=True)
                ci = pi + jnp.sum(jnp.where(row == s, ei, 0.0), axis=0, keepdims=True)

        def pass2(k, carry):
            pr, pi, _, _ = carry
            ar, ai = _cmul(pr, pi, cin_r, cin_i)
            hr = locr_ref[rows_of(k), :] + ar
            hi = loci_ref[rows_of(k), :] + ai
            hr_ref[rows_of(k), :] = hr.astype(hr_ref.dtype)
            hi_ref[rows_of(k), :] = hi.astype(hi_ref.dtype)
            npr, npi = _cmul(pr, pi, lr, li)
            return npr, npi, hr, hi

        _, _, last_r, last_i = lax.fori_loop(0, n, pass2, (lr, li, er, ei))
        htr_ref[...] = jnp.sum(jnp.where(row == s_last, last_r, 0.0), axis=0, keepdims=True)
        hti_ref[...] = jnp.sum(jnp.where(row == s_last, last_i, 0.0), axis=0, keepdims=True)

        if readout is not None:
            cr, ci = ro_refs[0][...].astype(BF16), ro_refs[1][...].astype(BF16)
            for r0 in range(0, rows, mm_rows):
                y_refs[0][r0:r0 + mm_rows, :] = (
                    jnp.dot(hr_ref[r0:r0 + mm_rows, :].astype(BF16), cr, preferred_element_type=F32)
                    + jnp.dot(hi_ref[r0:r0 + mm_rows, :].astype(BF16), ci, preferred_element_type=F32))

    big = pl.BlockSpec((rows, cb), lambda j: (0, j))
    vec = pl.BlockSpec((1, cb), lambda j: (0, j))
    wspec = pl.BlockSpec((None, kb, cb), lambda j: (j, 0, 0))
    in_specs = [pl.BlockSpec((rows, kb), lambda j: (0, j)), wspec, wspec] + [vec] * 6
    out_specs = [big, big, vec, vec]
    out_shape = [jax.ShapeDtypeStruct((rows, C), BF16)] * 2 + [jax.ShapeDtypeStruct((1, C), F32)] * 2
    extra = []
    if readout is not None:
        pb = readout[0].shape[2]
        in_specs += [pl.BlockSpec((None, cb, pb), lambda j: (j, 0, 0))] * 2
        out_specs.append(pl.BlockSpec((rows, pb), lambda j: (0, j)))
        out_shape.append(jax.ShapeDtypeStruct((rows, nb * pb), F32))
        extra = list(readout)
    return pl.pallas_call(
        body, name=name, grid=(nb,), in_specs=in_specs, out_specs=out_specs, out_shape=out_shape,
        scratch_shapes=[pltpu.VMEM((rows, cb), F32)] * 2,
        compiler_params=_cparams(("parallel",)),
    )(x, w_re, w_im, lam_re, lam_im, h0_re, h0_im, e0_re, e0_im, *extra)


def _s5_dlam(mu_re, mu_im, h_re, h_im, h0_re, h0_im, *, reverse, name):
    rows, C = h_re.shape
    n = rows // N_SEG
    cb = _pick(C, 256, LANES)
    s_first = N_SEG - 1 if reverse else 0

    def body(mr_ref, mi_ref, hr_ref, hi_ref, h0r_ref, h0i_ref, dr_ref, di_ref):
        shape = (N_SEG, cb)
        row = lax.broadcasted_iota(jnp.int32, shape, 0)

        def rows_of(k):
            step = (n - 1 - k) if reverse else k
            return pl.ds(pl.multiple_of(step * N_SEG, N_SEG), N_SEG)

        def term(k, pr, pi):
            mr, mi = mr_ref[rows_of(k), :].astype(F32), mi_ref[rows_of(k), :].astype(F32)
            return mr * pr + mi * pi, mi * pr - mr * pi

        shift = N_SEG - 1 if reverse else 1
        pr = jnp.where(row == s_first, h0r_ref[...], pltpu.roll(hr_ref[rows_of(n - 1), :].astype(F32), shift, 0))
        pi = jnp.where(row == s_first, h0i_ref[...], pltpu.roll(hi_ref[rows_of(n - 1), :].astype(F32), shift, 0))
        acc = term(0, pr, pi)

        def loop(k, acc):
            tr, ti = term(k, hr_ref[rows_of(k - 1), :].astype(F32), hi_ref[rows_of(k - 1), :].astype(F32))
            return acc[0] + tr, acc[1] + ti

        ar, ai = lax.fori_loop(1, n, loop, acc)
        dr_ref[...] = jnp.sum(ar, axis=0, keepdims=True)
        di_ref[...] = jnp.sum(ai, axis=0, keepdims=True)

    big = pl.BlockSpec((rows, cb), lambda j: (0, j))
    vec = pl.BlockSpec((1, cb), lambda j: (0, j))
    return pl.pallas_call(
        body, name=name, grid=(C // cb,),
        in_specs=[big] * 4 + [vec] * 2, out_specs=[vec, vec],
        out_shape=[jax.ShapeDtypeStruct((1, C), F32)] * 2,
        compiler_params=_cparams(("parallel",)),
    )(mu_re, mu_im, h_re, h_im, h0_re, h0_im)


NT_DIMS = (((1,), (1,)), ((), ()))
TN_DIMS = (((0,), (0,)), ((), ()))


ATTN_Q_ROWS = 512


def _attn_exp(q, kvh, kr):
    s = (lax.dot_general(q[:, :LANES], kvh[:, :LANES], NT_DIMS, preferred_element_type=F32)
         + lax.dot_general(q[:, LANES:], kr, NT_DIMS, preferred_element_type=F32))
    e = jnp.exp2((s - jnp.max(s, axis=-1, keepdims=True)) * (ATTN_SCALE * math.log2(math.e)))
    return e, jnp.sum(e, axis=-1, keepdims=True)


def _attn_specs(L, T, tq):
    return [
        pl.BlockSpec((tq, 2 * LANES), lambda h, i: (i, h)),
        pl.BlockSpec((T, 2 * LANES), lambda h, i: (0, h)),
        pl.BlockSpec((T, LANES), lambda h, i: (0, 0)),
    ]


def _attn_fwd(qq, kv, kr, *, name):
    L, T = qq.shape[0], kv.shape[0]
    tq = _pick(L, ATTN_Q_ROWS // 2, 16)

    def body(q_ref, kv_ref, kr_ref, o_ref):
        kvh = kv_ref[...]
        e, l = _attn_exp(q_ref[...], kvh, kr_ref[...])
        o_ref[...] = (jnp.dot(e.astype(BF16), kvh[:, LANES:], preferred_element_type=F32) * (1.0 / l)).astype(o_ref.dtype)

    return pl.pallas_call(
        body, name=name, grid=(MLA_HEADS, L // tq), in_specs=_attn_specs(L, T, tq),
        out_specs=pl.BlockSpec((tq, LANES), lambda h, i: (i, h)),
        out_shape=jax.ShapeDtypeStruct((L, MLA_HEADS * V_DIM), BF16),
        compiler_params=_cparams(("parallel", "parallel")),
    )(qq, kv, kr)


def _attn_bwd(qq, kv, kr, do, *, name):
    L, T = qq.shape[0], kv.shape[0]
    H = MLA_HEADS
    tq = _pick(L, ATTN_Q_ROWS, 16)
    nq = L // tq

    def body(q_ref, kv_ref, kr_ref, do_ref, dq_ref, dkv_ref, dkr_ref, dkn_acc, dv_acc):
        h, i = pl.program_id(0), pl.program_id(1)
        q, kvh, krv, dov = q_ref[...], kv_ref[...], kr_ref[...], do_ref[...]
        e, l = _attn_exp(q, kvh, krv)
        inv = 1.0 / l
        ps = e * (inv * ATTN_SCALE)
        t = lax.dot_general(dov, kvh[:, LANES:], NT_DIMS, preferred_element_type=F32) * ps
        ds = (t - ps * (jnp.sum(t, axis=-1, keepdims=True) * (1.0 / ATTN_SCALE))).astype(BF16)
        dq_ref[:, :LANES] = jnp.dot(ds, kvh[:, :LANES], preferred_element_type=F32)
        dq_ref[:, LANES:] = jnp.dot(ds, krv, preferred_element_type=F32)

        @pl.when(i == 0)
        def _():
            dkn_acc[...] = jnp.zeros_like(dkn_acc)
            dv_acc[...] = jnp.zeros_like(dv_acc)

        @pl.when((i == 0) & (h == 0))
        def _():
            dkr_ref[...] = jnp.zeros_like(dkr_ref)

        dv_acc[...] += lax.dot_general(e.astype(BF16), (dov.astype(F32) * inv).astype(BF16), TN_DIMS,
                                       preferred_element_type=F32)
        dkn_acc[...] += lax.dot_general(ds, q[:, :LANES], TN_DIMS, preferred_element_type=F32)
        dkr_ref[...] += lax.dot_general(ds, q[:, LANES:], TN_DIMS, preferred_element_type=F32)

        @pl.when(i == nq - 1)
        def _():
            dkv_ref[:, :LANES] = dkn_acc[...].astype(dkv_ref.dtype)
            dkv_ref[:, LANES:] = dv_acc[...].astype(dkv_ref.dtype)

    in_specs = _attn_specs(L, T, tq) + [pl.BlockSpec((tq, LANES), lambda h, i: (i, h))]
    return pl.pallas_call(
        body, name=name, grid=(H, L // tq), in_specs=in_specs,
        out_specs=[pl.BlockSpec((tq, 2 * LANES), lambda h, i: (i, h)), pl.BlockSpec((T, 2 * LANES), lambda h, i: (0, h)),
                   pl.BlockSpec((T, LANES), lambda h, i: (0, 0))],
        out_shape=[jax.ShapeDtypeStruct((L, H * 2 * LANES), F32), jax.ShapeDtypeStruct((T, H * 2 * LANES), BF16),
                   jax.ShapeDtypeStruct((T, LANES), F32)],
        scratch_shapes=[pltpu.VMEM((T, LANES), F32), pltpu.VMEM((T, LANES), F32)],
        compiler_params=_cparams(("arbitrary", "arbitrary")),
    )(qq, kv, kr, do)


def _adamw(w, g, m, v, *, name, anchor=None):
    c1 = 1.0 - ADAM_B1 ** ADAM_STEP
    c2 = 1.0 - ADAM_B2 ** ADAM_STEP

    def f(w, g, m, v):
        m = ADAM_B1 * m + (1.0 - ADAM_B1) * g
        v = ADAM_B2 * v + (1.0 - ADAM_B2) * jnp.square(g)
        delta = -ADAM_LR * ((m / c1) / (jnp.sqrt(v / c2) + ADAM_EPS) + ADAM_WD * w)
        return g, delta, m, v

    return _rw(f, [w, g, m, v], [], [F32] * 4, name=name, anchor=anchor, tile_bytes=STREAM_TILE_BYTES)


def _slab_rows(rows, cols, n_arrays):
    return _pick(rows, max(16, (8 * 1024 * 1024) // (cols * 4 * n_arrays)), 16)


def _scalars(*vals):
    return jnp.stack([jnp.asarray(v, jnp.int32) for v in vals])


def _into_slot(src, slot, nslots, dtype, *, name):
    R, C = src.shape
    tr = _slab_rows(R, C, 2)

    def body(s_ref, x_ref, o_ref):
        o_ref[...] = x_ref[...].astype(o_ref.dtype)

    return pl.pallas_call(
        body, name=name,
        grid_spec=pltpu.PrefetchScalarGridSpec(
            num_scalar_prefetch=1, grid=(R // tr,),
            in_specs=[pl.BlockSpec((tr, C), lambda i, s: (i, 0))],
            out_specs=pl.BlockSpec((None, tr, C), lambda i, s: (s[0], i, 0))),
        out_shape=jax.ShapeDtypeStruct((nslots, R, C), dtype),
        compiler_params=_cparams(("arbitrary",)),
    )(_scalars(slot), src)


def _pair_sum(g, got, c, *, name):
    _, R, C = g.shape
    hr = R // 2
    tr = _slab_rows(hr, C, 3)
    nblk = hr // tr

    def body(s_ref, g_ref, r_ref, o_ref):
        o_ref[...] = (g_ref[...].astype(F32) + r_ref[...].astype(F32)).astype(o_ref.dtype)

    return pl.pallas_call(
        body, name=name,
        grid_spec=pltpu.PrefetchScalarGridSpec(
            num_scalar_prefetch=1, grid=(4, nblk),
            in_specs=[pl.BlockSpec((None, tr, C), lambda j, i, s: (j, s[0] * nblk + i, 0)),
                      pl.BlockSpec((None, tr, C), lambda j, i, s: (j, i, 0))],
            out_specs=pl.BlockSpec((None, tr, C), lambda j, i, s: (j, i, 0))),
        out_shape=jax.ShapeDtypeStruct((4, hr, C), g.dtype),
        compiler_params=_cparams(("arbitrary", "arbitrary")),
    )(_scalars(c), g, got)


def _chip_sum(p, landed, me_chip, c, *, name):
    _, hr, C = p.shape
    tr = _slab_rows(hr, C, 5)

    def body(s_ref, p_ref, l0_ref, l1_ref, l2_ref, o_ref):
        o_ref[...] = ((p_ref[...].astype(F32) + l0_ref[...].astype(F32)) + l1_ref[...].astype(F32)) + l2_ref[...].astype(F32)

    return pl.pallas_call(
        body, name=name,
        grid_spec=pltpu.PrefetchScalarGridSpec(
            num_scalar_prefetch=1, grid=(hr // tr,),
            in_specs=[pl.BlockSpec((None, tr, C), lambda i, s: (s[0], i, 0))]
            + [pl.BlockSpec((None, tr, C), functools.partial(lambda i, s, k: (k, i, 0), k=k)) for k in range(3)],
            out_specs=pl.BlockSpec((None, tr, C), lambda i, s: (s[1], i, 0))),
        out_shape=jax.ShapeDtypeStruct((2, hr, C), F32),
        compiler_params=_cparams(("arbitrary",)),
    )(_scalars(me_chip, c), p, landed, landed, landed)


def _place():
    return lax.axis_index("x"), lax.axis_index("y"), lax.axis_index("c")


def _other_chips(x, y):
    chips = [(1 - x, y), (x, 1 - y), (1 - x, 1 - y)]
    return chips, [2 * cx + cy for cx, cy in chips]


HBM = pl.BlockSpec(memory_space=pl.ANY)


def _allgather8(v, *, name):
    rows, cols = v.shape

    def body(v_ref, out_ref, send_sems, recv_sems):
        x, y, c = _place()
        me = 4 * x + 2 * y + c
        out_ref[me] = v_ref[...]
        copies = []
        for k in range(1, 8):
            bx, by, bc = (k >> 2) & 1, (k >> 1) & 1, k & 1
            px, py, pc = x ^ bx, y ^ by, c ^ bc
            cp = pltpu.make_async_remote_copy(
                src_ref=v_ref, dst_ref=out_ref.at[me], send_sem=send_sems.at[k - 1], recv_sem=recv_sems.at[k - 1],
                device_id=(px, py, pc), device_id_type=MESH)
            cp.start()
            copies.append((cp, 4 * px + 2 * py + pc))
        for k, (cp, peer) in enumerate(copies):
            pltpu.make_async_remote_copy(
                src_ref=v_ref, dst_ref=out_ref.at[peer], send_sem=send_sems.at[k], recv_sem=recv_sems.at[k],
                device_id=(x, y, c), device_id_type=MESH).wait_recv()
        for cp, _ in copies:
            cp.wait_send()

    return pl.pallas_call(
        body, name=name, out_shape=jax.ShapeDtypeStruct((8, rows, cols), v.dtype),
        in_specs=[pl.BlockSpec(memory_space=pltpu.VMEM)], out_specs=pl.BlockSpec(memory_space=pltpu.VMEM),
        scratch_shapes=[pltpu.SemaphoreType.DMA((7,)), pltpu.SemaphoreType.DMA((7,))],
        compiler_params=pltpu.CompilerParams(vmem_limit_bytes=VMEM_LIMIT),
    )(v)


def _allgather_shards(bufs, *, name):
    n = len(bufs)

    def body(*refs):
        outs = refs[n:2 * n]
        send_sems, recv_sems = refs[2 * n:]
        x, y, c = _place()
        me_chip = 2 * x + y
        sibling = (x, y, 1 - c)
        chips, chip_ids = _other_chips(x, y)

        def remote(k, j, blk, hf, to):
            hr = bufs[k].shape[1] // 2
            piece = outs[k].at[blk, pl.ds(pl.multiple_of(hf * hr, 16), hr), :]
            return pltpu.make_async_remote_copy(
                src_ref=piece, dst_ref=piece, send_sem=send_sems.at[6 * k + j], recv_sem=recv_sems.at[6 * k + j],
                device_id=to, device_id_type=MESH)

        sends = []
        for k in range(n):
            for j, chip in enumerate(chips):
                cp = remote(k, j, me_chip, c, (*chip, c))
                cp.start()
                sends.append(cp)
        for k in range(n):
            for j, chip in enumerate(chips):
                remote(k, j, chip_ids[j], c, (x, y, c)).wait_recv()
                cp = remote(k, 3 + j, chip_ids[j], c, sibling)
                cp.start()
                sends.append(cp)
        for k in range(n):
            for j in range(3):
                remote(k, 3 + j, chip_ids[j], 1 - c, (x, y, c)).wait_recv()
        for cp in sends:
            cp.wait_send()

    return list(pl.pallas_call(
        body, name=name, out_shape=[jax.ShapeDtypeStruct(b.shape, b.dtype) for b in bufs],
        in_specs=[HBM] * n, out_specs=[HBM] * n, input_output_aliases={k: k for k in range(n)},
        scratch_shapes=[pltpu.SemaphoreType.DMA((6 * n,)), pltpu.SemaphoreType.DMA((6 * n,))],
    )(*bufs))


HBM_SPEC = pl.BlockSpec(memory_space=pltpu.HBM)
SEM_SPEC = pl.BlockSpec(memory_space=pltpu.SEMAPHORE)
EFFECT = pltpu.SideEffectType.DATAFLOW_SIDE_EFFECTING
TOKEN = jax.ShapeDtypeStruct((SUBLANES, LANES), F32)


def _in_hbm(a):
    return pltpu.with_memory_space_constraint(a, pltpu.HBM)


def _half_rows(buf, hf):
    hr = buf.shape[1] // 2
    return pl.ds(pl.multiple_of(hf * hr, 16), hr)


def _plan_ag_ici(refs):
    x, y, c = _place()
    chips, ids = _other_chips(x, y)
    out = []
    for r in refs:
        mine = r.at[2 * x + y, _half_rows(r, c), :]
        out += [(mine, mine, r.at[ids[j], _half_rows(r, c), :], (*chip, c)) for j, chip in enumerate(chips)]
    return out


def _plan_ag_pair(refs):
    x, y, c = _place()
    _, ids = _other_chips(x, y)
    out = []
    for r in refs:
        for j in range(3):
            piece = r.at[ids[j], _half_rows(r, c), :]
            out.append((piece, piece, r.at[ids[j], _half_rows(r, 1 - c), :], (x, y, 1 - c)))
    return out


def _plan_rs_ici(refs):
    x, y, c = _place()
    chips, ids = _other_chips(x, y)
    n = len(refs) // 2
    return [(refs[k].at[ids[j]], refs[n + k].at[j], refs[n + k].at[j], (*chip, c))
            for k in range(n) for j, chip in enumerate(chips)]


def _plan_pair_exchange(refs):
    x, y, c = _place()
    n = len(refs) // 2
    return [(refs[k].at[:, _half_rows(refs[k], 1 - c), :], refs[n + k], refs[n + k], (x, y, 1 - c)) for k in range(n)]


def _plan_pair_gather(refs):
    x, y, c = _place()
    return [(r.at[c], r.at[c], r.at[1 - c], (x, y, 1 - c)) for r in refs]


def _remote(src, dst, send_sem, recv_sem, target):
    return pltpu.make_async_remote_copy(src_ref=src, dst_ref=dst, send_sem=send_sem, recv_sem=recv_sem,
                                        device_id=target, device_id_type=MESH)


def _copy_start(groups, *, name, after=()):
    flat = [a for arrays, _, _ in groups for a in arrays]
    n, ng = len(flat), len(groups)
    after = list(after)
    n_in = n + len(after)

    def body(*refs):
        sems = refs[n_in:n_in + 2 * ng]
        thru = refs[n_in + 2 * ng:n_in + 2 * ng + n]
        token = refs[-1]
        pos = 0
        for g, (arrays, plan, n_copies) in enumerate(groups):
            copies = plan(thru[pos:pos + len(arrays)])
            pos += len(arrays)
            assert len(copies) == n_copies
            for i, (src, dst, _, target) in enumerate(copies):
                _remote(src, dst, sems[2 * g].at[i], sems[2 * g + 1].at[i], target).start()
        token[...] = jnp.zeros_like(token)

    out_shape = tuple(pltpu.SemaphoreType.DMA((n_copies,)) for _, _, n_copies in groups for _ in range(2))
    out_shape += tuple(pltpu.HBM(a.shape, a.dtype) for a in flat) + (TOKEN,)
    res = pl.pallas_call(
        body, name=name, out_shape=out_shape,
        in_specs=(HBM_SPEC,) * n + (pl.BlockSpec(memory_space=pl.ANY),) * len(after),
        out_specs=(SEM_SPEC,) * (2 * ng) + (HBM_SPEC,) * n + (pl.BlockSpec(memory_space=pltpu.VMEM),),
        input_output_aliases={k: 2 * ng + k for k in range(n)},
        compiler_params=pltpu.CompilerParams(has_side_effects=EFFECT),
    )(*[_in_hbm(a) for a in flat], *after)
    sems = [(res[2 * g], res[2 * g + 1]) for g in range(ng)]
    thru, pos = [], 2 * ng
    for arrays, _, _ in groups:
        thru.append(list(res[pos:pos + len(arrays)]))
        pos += len(arrays)
    return sems, thru, res[-1]


def _copy_wait(arrays, sems, plan, n_copies, after, *, name):
    n = len(arrays)
    after = list(after)

    def body(*refs):
        send, recv = refs[n], refs[n + 1]
        x, y, c = _place()
        copies = plan(refs[:n])
        assert len(copies) == n_copies
        for i, (src, dst, landing, target) in enumerate(copies):
            _remote(src, dst, send.at[i], recv.at[i], target).wait_send()
            _remote(landing, landing, send.at[i], recv.at[i], (x, y, c)).wait_recv()

    return list(pl.pallas_call(
        body, name=name, out_shape=tuple(pltpu.HBM(a.shape, a.dtype) for a in arrays),
        in_specs=(HBM_SPEC,) * n + (SEM_SPEC, SEM_SPEC) + (pl.BlockSpec(memory_space=pl.ANY),) * len(after),
        out_specs=(HBM_SPEC,) * n, input_output_aliases={k: k for k in range(n)},
        compiler_params=pltpu.CompilerParams(has_side_effects=EFFECT),
    )(*arrays, *sems, *after))


def _rs_stage1(gs, tag, after=()):
    n = len(gs)
    lands = [lax.empty((4, g.shape[1] // 2, g.shape[2]), g.dtype) for g in gs]
    sems, (arrays,), token = _copy_start([(list(gs) + lands, _plan_pair_exchange, n)], name=f"rs_pair_start_{tag}",
                                         after=after)
    return (sems[0], arrays), token


def _rs_stage2(handle, after, tag):
    sems, arrays = handle
    n = len(arrays) // 2
    arrays = _copy_wait(arrays, sems, _plan_pair_exchange, n, after, name=f"rs_pair_wait_{tag}")
    c = lax.axis_index("c")
    pair = [_pair_sum(g, r, c, name=f"rs_pair_sum_{tag}{k}") for k, (g, r) in enumerate(zip(arrays[:n], arrays[n:]))]
    lands = [lax.empty((3,) + p.shape[1:], p.dtype) for p in pair]
    sems, (arrays,), token = _copy_start([(pair + lands, _plan_rs_ici, 3 * n)], name=f"rs_start_{tag}")
    return (sems[0], arrays), token


def _rs_stage3(handle, after, tag):
    sems, arrays = handle
    n = len(arrays) // 2
    arrays = _copy_wait(arrays, sems, _plan_rs_ici, 3 * n, after, name=f"rs_wait_{tag}")
    x, y, c = _place()
    halves = [_chip_sum(p, l, 2 * x + y, c, name=f"rs_chip_sum_{tag}{k}") for k, (p, l) in enumerate(zip(arrays[:n], arrays[n:]))]
    sems, (halves,), token = _copy_start([(halves, _plan_pair_gather, n)], name=f"rs_gather_start_{tag}")
    return (sems[0], halves), token


def _rs_stage4(handle, after, tag):
    sems, halves = handle
    full = _copy_wait(halves, sems, _plan_pair_gather, len(halves), after, name=f"rs_gather_wait_{tag}")
    return [f.reshape(2 * f.shape[1], f.shape[2]) for f in full]


def _to_segments(a):
    rows = a.shape[0]
    return a.reshape(N_SEG, rows // N_SEG, -1).transpose(1, 0, 2).reshape(rows, -1)


def _from_segments(a):
    rows = a.shape[0]
    return a.reshape(rows // N_SEG, N_SEG, -1).transpose(1, 0, 2).reshape(rows, -1)


def _rope_tables(L):
    t = jnp.arange(L, dtype=jnp.int32)
    row = (t // GRID_W).astype(F32)
    col = (t % GRID_W).astype(F32)
    n_freq = QK_ROPE // 4
    inv = ROPE_BASE ** (-jnp.arange(n_freq, dtype=F32) / n_freq)
    a0, a1 = row[:, None] * inv, col[:, None] * inv
    z = jnp.zeros((L, LANES - QK_ROPE), F32)
    cos = jnp.concatenate([jnp.cos(a0), jnp.cos(a0), jnp.cos(a1), jnp.cos(a1), z], axis=1)
    sin = jnp.concatenate([-jnp.sin(a0), jnp.sin(a0), -jnp.sin(a1), jnp.sin(a1), z], axis=1)
    return _to_segments(cos), _to_segments(sin)


def _col_blocks(w, nblk):
    r, c = w.shape
    return w.reshape(r, nblk, c // nblk).transpose(1, 0, 2)


def _from_col_blocks(w4):
    nblk, r, c = w4.shape
    return w4.transpose(1, 0, 2).reshape(r, nblk * c)


def _s5_discretize(a_re, a_im, log_dt, b_re, b_im):
    dt = jnp.exp(log_dt)[:, None]
    mag = jnp.exp(a_re * dt)
    ab_re, ab_im = mag * jnp.cos(a_im * dt), mag * jnp.sin(a_im * dt)
    den = a_re * a_re + a_im * a_im
    nr, ni = ab_re - 1.0, ab_im
    co_re = (nr * a_re + ni * a_im) / den
    co_im = (ni * a_re - nr * a_im) / den
    bb_re = co_re[..., None] * b_re - co_im[..., None] * b_im
    bb_im = co_re[..., None] * b_im + co_im[..., None] * b_re
    return ab_re, ab_im, bb_re, bb_im


def _diag_blocks_in(bb, gpb):
    G, N, P = bb.shape
    t = jnp.tile(jnp.swapaxes(bb, 1, 2).reshape(G // gpb, gpb * P, N), (1, 1, gpb))
    row = lax.broadcasted_iota(jnp.int32, t.shape, 1) // P
    col = lax.broadcasted_iota(jnp.int32, t.shape, 2) // N
    return jnp.where(row == col, t, 0.0)


def _diag_blocks_out(cc, gpb):
    G, P, N = cc.shape
    t = jnp.tile(jnp.swapaxes(cc, 1, 2).reshape(G // gpb, gpb * N, P), (1, 1, gpb))
    row = lax.broadcasted_iota(jnp.int32, t.shape, 1) // N
    col = lax.broadcasted_iota(jnp.int32, t.shape, 2) // P
    return jnp.where(row == col, t, 0.0)


def _tr(ws):
    return [jnp.swapaxes(w, 1, 2) for w in ws]


WEIGHTS = ['c_ctx', 'w_mod', 'b_mod', 'norm1', 'norm2', 'w_in', 's5_a_re', 's5_a_im', 's5_log_dt', 's5_b_re', 's5_b_im',
           's5_c_re', 's5_c_im', 's5_d', 'w_glu', 'q_norm', 'kv_norm', 'w_uq', 'w_ukv', 'w_mla_o', 'w_out', 'w_ffn_in',
           'w_ffn_out', 'norm_f']
AG_GROUPS = [['w_in'], ['w_glu', 'w_uq', 'w_ukv', 'w_mla_o', 'w_out'], ['w_ffn_in', 'w_ffn_out']]
SMALL = ['norm1', 'norm2', 's5_a_re', 's5_a_im', 's5_log_dt', 's5_b_re', 's5_b_im', 's5_c_re', 's5_c_im', 's5_d',
         'q_norm', 'kv_norm', 'norm_f']


def _pad_rows(a, rows):
    return jnp.concatenate([a, jnp.zeros((rows - a.shape[0],) + a.shape[1:], a.dtype)], axis=0)


def _pack(vals, width, rows):
    flat = jnp.concatenate([v.reshape(-1).astype(F32) for v in vals])
    flat = jnp.concatenate([flat, jnp.zeros((rows * width - flat.shape[0],), F32)])
    return flat.reshape(rows, width)


def _unpack(buf, like):
    flat = buf.reshape(-1)
    out, pos = [], 0
    for v in like:
        out.append(flat[pos:pos + v.size].reshape(v.shape))
        pos += v.size
    return out


def _step(x, c, ctx, loss_target, w, m, v):
    px, py, pc = _place()
    me = 4 * px + 2 * py + pc
    me_chip = 2 * px + py
    L, D = x.shape[1], x.shape[2]
    Lc = ctx.shape[1]
    T = L + Lc
    SW = D // 2
    G = SW // S5_GROUP
    C = G * S5_STATE
    H = MLA_HEADS
    q_rank = w['q_norm'].shape[1]
    kv_rank = w['kv_norm'].shape[1]
    d_ff = w['w_ffn_out'].shape[1] * 4
    wa_used = SW + q_rank + kv_rank + QK_ROPE
    WA = -(-(SW + q_rank + kv_rank + LANES) // 512) * 512

    c_rows = _pad_rows(c.astype(F32), SUBLANES)
    c_all = _allgather8(c_rows, name="ag_cond")[:, 0, :]
    cond = jnp.concatenate([c_all, w['c_ctx'].reshape(1, D)], axis=0)
    cond = _pad_rows(cond, 16)
    (act,) = _rw(lambda t: (jax.nn.silu(t),), [cond], [], [F32], name="cond_silu")
    w_mod, cs_mod = w['w_mod'][0], w['w_mod'].shape[2]
    mod_part = _mm(act, w_mod, out_dtype=F32, name="mod_fwd")
    mod_all = _allgather8(mod_part, name="ag_mod")
    mod_full = jnp.concatenate([mod_all[0], mod_all[2], mod_all[4], mod_all[6]], axis=1) + w['b_mod']
    m_lat = lax.dynamic_slice_in_dim(mod_full, me, 1, axis=0).reshape(6, D)
    m_ctx = mod_full[8].reshape(6, D)
    sh1, sc1, g1, sh2, sc2, g2 = (m_lat[i:i + 1] for i in range(6))
    csh1, csc1 = m_ctx[0:1], m_ctx[1:2]

    ag_groups = [([_into_slot(w[nme][0], me_chip, 4, BF16, name=f"cast_{nme}") for nme in grp], _plan_ag_ici, 3 * len(grp))
                 for grp in AG_GROUPS]
    ag_sems, ag_bufs, ag_token = _copy_start(ag_groups, name="ag_start", after=[mod_full])
    gathered, ag_pair = {}, {}

    def landed(g, after):
        n_cp = 3 * len(AG_GROUPS[g])
        got = _copy_wait(ag_bufs[g], ag_sems[g], _plan_ag_ici, n_cp, after, name=f"ag_wait_{g}")
        sems, (got,), token = _copy_start([(got, _plan_ag_pair, n_cp)], name=f"ag_pair_start_{g}")
        ag_pair[g] = (sems[0], got)
        return token[0, 0]

    def arrive(g, after):
        sems, got = ag_pair[g]
        got = _copy_wait(got, sems, _plan_ag_pair, 3 * len(AG_GROUPS[g]), after, name=f"ag_pair_wait_{g}")
        gathered.update(zip(AG_GROUPS[g], got))

    xs = _to_segments(x[0])
    cs = _to_segments(ctx[0])
    tgt = _to_segments(loss_target[0])
    cos, sin = _rope_tables(L)
    n1, n2, nf = w['norm1'], w['norm2'], w['norm_f'].reshape(1, D)
    qg, kvg = w['q_norm'], w['kv_norm']

    (xn_lat,) = _rw(_f_norm_mod, [xs], [n1 + ag_token[0, 0], sc1, sh1], [BF16], name="norm1_lat")
    (xn_ctx,) = _rw(_f_norm_mod, [cs], [n1, csc1, csh1], [BF16], name="norm1_ctx")
    xn = jnp.concatenate([xn_lat, xn_ctx], axis=0)
    landed(0, [xn])

    gpb = min(S5_BLOCK_GROUPS, G)
    gpo = min(8, G)
    d_skip = w['s5_d'][0].reshape(1, SW)
    disc, vjp_disc, w_b, w_c = [], [], [], []
    for d in range(2):
        prm = (w['s5_a_re'][0, d], w['s5_a_im'][0, d], w['s5_log_dt'][0, d], w['s5_b_re'][0, d], w['s5_b_im'][0, d])

        def prep(a_re, a_im, log_dt, b_re, b_im):
            ab_re, ab_im, bb_re, bb_im = _s5_discretize(a_re, a_im, log_dt, b_re, b_im)
            return ab_re.reshape(1, C), ab_im.reshape(1, C), _diag_blocks_in(bb_re, gpb), _diag_blocks_in(bb_im, gpb)

        out, vj = jax.vjp(prep, *prm)
        disc.append(out)
        vjp_disc.append(vj)
        w_b += [out[2], out[3]]
        w_c += [_diag_blocks_out(w['s5_c_re'][0, d], gpo), -_diag_blocks_out(w['s5_c_im'][0, d], gpo)]
    nb_in = G // gpb
    nb_out = G // gpo

    arrive(0, [xn, tgt] + w_b + w_c)
    w_in = _from_col_blocks(gathered['w_in'])
    w_a = jnp.concatenate([w_in[:, :wa_used], jnp.zeros((D, WA - wa_used), BF16)], axis=1)
    w_g = w_in[:, wa_used:]
    ha = _mm(xn, w_a, out_dtype=F32, name="in_proj")
    ha_lat, ha_ctx = ha[:L], ha[L:]
    gt = _mm(xn_lat, w_g, out_dtype=F32, name="in_gates")
    f_post_lat = _make_f_post_in(SW, q_rank, kv_rank, True)
    f_post_ctx = _make_f_post_in(SW, q_rank, kv_rank, False)
    u_lat, cqn, ckvn_lat, kr_lat = _rw(f_post_lat, [ha_lat, cos, sin], [qg, kvg], [F32, BF16, BF16, BF16], name="post_in_lat")
    u_ctx, ckvn_ctx, kr_ctx = _rw(f_post_ctx, [ha_ctx], [kvg], [F32, BF16, BF16], name="post_in_ctx")
    zero = jnp.zeros((1, C), F32) + landed(1, [u_lat, u_ctx])

    h_lat, h_ctx, hT_ctx, r5 = [], [], [], []
    for d, rev in enumerate((False, True)):
        lr, li = disc[d][0], disc[d][1]
        hcr, hci, tr, ti = _s5_scan(u_ctx, w_b[2 * d], w_b[2 * d + 1], lr, li, zero, zero, zero, zero, reverse=rev,
                                    name=f"s5_scan_ctx_{d}")
        hlr, hli, _, _, y = _s5_scan(u_lat, w_b[2 * d], w_b[2 * d + 1], lr, li, tr, ti, zero, zero, reverse=rev,
                                     name=f"s5_scan_lat_{d}", readout=(w_c[2 * d], w_c[2 * d + 1]))
        h_ctx += [hcr, hci]
        h_lat += [hlr, hli]
        hT_ctx += [tr, ti]
        r5.append(y)
    (z,) = _rw(_f_s5post, [u_lat] + r5, [d_skip], [BF16], name="s5_post")

    arrive(1, [z])
    w_glu, w_ukv, w_mla_o = (gathered[nme] for nme in ('w_glu', 'w_ukv', 'w_mla_o'))
    w_out = gathered['w_out'].reshape(D, D)
    uq3 = _from_col_blocks(gathered['w_uq']).reshape(q_rank, H, QK_NOPE + QK_ROPE)
    w_q2 = jnp.concatenate([uq3, jnp.zeros((q_rank, H, LANES - QK_ROPE), BF16)], axis=2).reshape(q_rank, H * 2 * LANES)
    q2 = _mm(cqn, w_q2, out_dtype=F32, name="q_up")
    (qq,) = _rw(_f_qpost, [q2, cos, sin], [], [BF16], name="q_rope")
    kvn = jnp.concatenate([ckvn_lat, ckvn_ctx], axis=0)
    kr_all = jnp.concatenate([kr_lat, kr_ctx], axis=0)
    kv = _mm(kvn, w_ukv, b_shards=4, out_dtype=BF16, name="kv_up")
    kr_all = kr_all + landed(2, [kv, qq]).astype(BF16)
    o = _attn_fwd(qq, kv, kr_all, name="attn_fwd")

    ab = _mm(z, w_glu, b_shards=4, out_dtype=F32, name="glu_proj")
    bm = _mm(o, w_mla_o, b_shards=4, out_dtype=F32, name="mla_out")
    (mix,) = _rw(_f_merge, [ab, bm, gt], [], [BF16], name="merge")
    out1 = _mm(mix, w_out, out_dtype=F32, name="out_proj")
    x1, xn2 = _rw(_f_resid_norm, [xs, out1], [g1, n2, sc2, sh2], [F32, BF16], name="resid_norm2")
    arrive(2, [xn2])
    w_ffn_in = gathered['w_ffn_in']
    w_ffn_out = gathered['w_ffn_out'].reshape(d_ff, D)
    hmid, ab2 = _ffn_in_swiglu(xn2, w_ffn_in, name="ffn_in")
    f2 = _mm(hmid, w_ffn_out, out_dtype=F32, name="ffn_out")
    (row_loss,) = _rw(_f_final, [x1, f2, tgt], [g2, nf], [F32], name="final_loss")
    loss = lax.psum(jnp.sum(row_loss), ("x", "y", "c"))

    ones = jnp.ones((L, 1), F32)
    (dx1_a, df2), (dg2, dnf) = _rw_vjp(_f_final, [x1, f2, tgt], [g2, nf], [[ones]], [True, True, False], [True, True],
                                       [F32, BF16], name="final_loss_bwd")
    gw_ffn_out = _mm(hmid, df2, ta=True, out_dtype=BF16, name="ffn_out_dw")
    dab2 = _ffn_out_dx_swiglu(df2, w_ffn_out, ab2, name="ffn_out_dx")
    dxn2 = _mm(dab2, w_ffn_in, tb=True, a_shards=2, b_shards=4, out_dtype=F32, name="ffn_in_dx")
    gw_ffn_in = _mm(xn2, dab2, ta=True, b_shards=2, out_shards=4, out_dtype=BF16, name="ffn_in_dw")
    rs_ffn, tok = _rs_stage1([gw_ffn_out.reshape(4, -1, D), gw_ffn_in], "ffn")
    (dx_a, dout1), (dg1, dn2, dsc2, dsh2) = _rw_vjp(
        _f_resid_norm, [xs, out1], [g1, n2 + tok[0, 0], sc2, sh2], [[dx1_a], [dxn2]], [True, True], [True] * 4, [F32, BF16],
        name="resid_norm2_bwd")
    dmix = _mm(dout1, w_out, tb=True, out_dtype=F32, name="out_proj_dx")
    rs_ffn, tok = _rs_stage2(rs_ffn, [dmix], "ffn")
    gw_out = _mm(mix, dout1, ta=True, out_dtype=BF16, name="out_proj_dw")
    (dab, dbm, dgt), _ = _rw_vjp(_f_merge, [ab, bm, gt], [], [[dmix]], [True] * 3, [], [BF16] * 3, name="merge_bwd",
                                 anchor=tok)
    dz = _mm(dab, w_glu, tb=True, b_shards=4, out_dtype=F32, name="glu_proj_dx")
    gw_glu = _mm(z, dab, ta=True, out_shards=4, out_dtype=BF16, name="glu_proj_dw")
    do = _mm(dbm, w_mla_o, tb=True, b_shards=4, out_dtype=BF16, name="mla_out_dx")
    gw_mla_o = _mm(o, dbm, ta=True, out_shards=4, out_dtype=BF16, name="mla_out_dw")
    dxn_g = _mm(dgt, w_g, tb=True, out_dtype=F32, name="in_gates_dx")
    gw_g = _mm(xn_lat, dgt, ta=True, out_dtype=BF16, name="in_gates_dw")
    rs_mid, tok = _rs_stage1([gw_out.reshape(4, -1, D), gw_glu, gw_mla_o], "mid", after=[gw_g])

    (du_a, dr5), (dd_skip,) = _rw_vjp(_f_s5post, [u_lat] + r5, [d_skip + tok[0, 0]], [[dz]], [True, True, False], [True],
                                      [F32, F32], name="s5_post_bwd")
    dw_c = _bd_dw(h_lat, [dr5] * 4, nb_out, name="s5_readout_dw")
    rs_mid, tok = _rs_stage2(rs_mid, dw_c[:1], "mid")
    zero = zero + tok[0, 0]
    w_ct = _tr(w_c)
    zeros_ctx = jnp.zeros((Lc, SW), BF16)
    mu_lat, mu_ctx, dlam = [], [], []
    for d, rev in enumerate((False, True)):
        lr, li = disc[d][0], disc[d][1]
        mlr, mli, fr, fi = _s5_scan(dr5, w_ct[2 * d], w_ct[2 * d + 1], lr, -li, zero, zero, zero, zero, reverse=not rev,
                                    name=f"s5_adj_lat_{d}")
        dh0r, dh0i = _cmul(lr, -li, fr, fi)
        mcr, mci, _, _ = _s5_scan(zeros_ctx, w_ct[2 * d], w_ct[2 * d + 1], lr, -li, zero, zero, dh0r, dh0i,
                                  reverse=not rev, name=f"s5_adj_ctx_{d}")
        dl_lat = _s5_dlam(mlr, mli, h_lat[2 * d], h_lat[2 * d + 1], hT_ctx[2 * d], hT_ctx[2 * d + 1], reverse=rev,
                          name=f"s5_dlam_lat_{d}")
        dl_ctx = _s5_dlam(mcr, mci, h_ctx[2 * d], h_ctx[2 * d + 1], zero, zero, reverse=rev, name=f"s5_dlam_ctx_{d}")
        mu_lat += [mlr, mli]
        mu_ctx += [mcr, mci]
        dlam.append((dl_lat[0] + dl_ctx[0], dl_lat[1] + dl_ctx[1]))
    du_b = _bd_fanin(mu_lat, _tr(w_b), name="s5_bu_lat_dx")
    du_ctx = _bd_fanin(mu_ctx, _tr(w_b), name="s5_bu_ctx_dx")
    dw_b_lat = _bd_dw([u_lat] * 4, mu_lat, nb_in, name="s5_bu_lat_dw")
    dw_b_ctx = _bd_dw([u_ctx] * 4, mu_ctx, nb_in, name="s5_bu_ctx_dw")
    g_s5 = {}
    for d in range(2):
        ct = (dlam[d][0], dlam[d][1], dw_b_lat[2 * d] + dw_b_ctx[2 * d], dw_b_lat[2 * d + 1] + dw_b_ctx[2 * d + 1])
        ga_re, ga_im, gdt, gb_re, gb_im = vjp_disc[d](ct)
        _, vj_c = jax.vjp(lambda cr, ci: (_diag_blocks_out(cr, gpo), -_diag_blocks_out(ci, gpo)),
                          w['s5_c_re'][0, d], w['s5_c_im'][0, d])
        gc_re, gc_im = vj_c((dw_c[2 * d], dw_c[2 * d + 1]))
        for nme, val in (('s5_a_re', ga_re), ('s5_a_im', ga_im), ('s5_log_dt', gdt), ('s5_b_re', gb_re),
                         ('s5_b_im', gb_im), ('s5_c_re', gc_re), ('s5_c_im', gc_im)):
            g_s5.setdefault(nme, []).append(val)
    g_small = {nme: jnp.stack(vals)[None] for nme, vals in g_s5.items()}
    g_small['s5_d'] = dd_skip.reshape(w['s5_d'].shape)

    dqq, dkv, dkr = _attn_bwd(qq, kv, kr_all, do, name="attn_bwd")
    (dq2,), _ = _rw_vjp(_f_qpost, [q2, cos, sin], [], [[dqq]], [True, False, False], [], [BF16], name="q_rope_bwd")
    dcqn = _mm(dq2, w_q2, tb=True, out_dtype=F32, name="q_up_dx")
    gw_q2 = _mm(cqn, dq2, ta=True, out_dtype=BF16, name="q_up_dw")
    dckvn = _mm(dkv, w_ukv, tb=True, b_shards=4, out_dtype=F32, name="kv_up_dx")
    gw_ukv = _mm(kvn, dkv, ta=True, out_shards=4, out_dtype=BF16, name="kv_up_dw")
    gw_uq = gw_q2.reshape(q_rank, H, 2 * LANES)[:, :, :QK_NOPE + QK_ROPE].reshape(q_rank, H * (QK_NOPE + QK_ROPE))
    rs_kv, tok = _rs_stage1([_col_blocks(gw_uq, 4), gw_ukv], "kv")

    (dha_lat,), (dqg, dkvg_lat) = _rw_vjp(
        f_post_lat, [ha_lat, cos, sin], [qg, kvg + tok[0, 0]], [[du_a, du_b], [dcqn], [dckvn[:L]], [dkr[:L]]],
        [True, False, False], [True, True], [BF16], name="post_in_lat_bwd")
    (dha_ctx,), (dkvg_ctx,) = _rw_vjp(f_post_ctx, [ha_ctx], [kvg], [[du_ctx], [dckvn[L:]], [dkr[L:]]], [True], [True],
                                      [BF16], name="post_in_ctx_bwd")
    dha = jnp.concatenate([dha_lat, dha_ctx], axis=0)
    dxn = _mm(dha, w_a, tb=True, out_dtype=F32, name="in_proj_dx")
    gw_a = _mm(xn, dha, ta=True, out_dtype=BF16, name="in_proj_dw")
    rs_kv, tok = _rs_stage2(rs_kv, [gw_a], "kv")
    (dx_seg,), (dn1_lat, dsc1, dsh1) = _rw_vjp(
        _f_norm_mod_keep, [xs], [n1 + tok[0, 0], sc1, sh1], [[dxn[:L], dxn_g], [dx_a]], [True], [True] * 3, [F32],
        name="norm1_lat_bwd")
    _, (dn1_ctx, dcsc1, dcsh1) = _rw_vjp(_f_norm_mod, [cs], [n1, csc1, csh1], [[dxn[L:]]], [False], [True] * 3, [],
                                         name="norm1_ctx_bwd")
    grad_x = _from_segments(dx_seg)[None]
    g_small.update(norm1=dn1_lat + dn1_ctx, norm2=dn2, q_norm=dqg, kv_norm=dkvg_lat + dkvg_ctx, norm_f=dnf.reshape(D))
    gw_in = jnp.concatenate([gw_a[:, :wa_used], gw_g], axis=1)
    small_vals = [g_small[nme] for nme in SMALL]
    n_small = sum(val.size for val in small_vals)
    small_rows = -(-n_small // (LANES * 4 * 32)) * 32

    zD = jnp.zeros((1, D), F32)
    dm = jnp.concatenate([
        jnp.concatenate([dsh1, dsc1, dg1, dsh2, dsc2, dg2], axis=1),
        jnp.concatenate([dcsh1, dcsc1, zD, zD, zD, zD], axis=1),
    ], axis=0)
    dm_all = _allgather8(_pad_rows(dm, SUBLANES), name="ag_dmod")
    rs_in, tok = _rs_stage1([_col_blocks(gw_in, 4), _pack(small_vals, LANES, 4 * small_rows).reshape(4, small_rows, LANES)],
                            "in", after=[dm_all])
    dm_ctx = dm_all[0, 1] + tok[0, 0]
    for k in range(1, 8):
        dm_ctx = dm_ctx + dm_all[k, 1]
    dmod = _pad_rows(jnp.concatenate([dm_all[:, 0, :], dm_ctx[None]], axis=0), 16)
    g_b_mod = jnp.sum(dmod, axis=0, keepdims=True)
    dmod_mine = lax.dynamic_slice_in_dim(dmod, me_chip * cs_mod, cs_mod, axis=1)
    g_w_mod = _mm(act, dmod_mine, ta=True, out_dtype=F32, name="mod_dw")
    dact_part = _mm(dmod_mine, w_mod, tb=True, out_dtype=F32, name="mod_dx")
    dact_all = _allgather8(dact_part, name="ag_dact")
    dact = dact_all[0] + dact_all[2] + dact_all[4] + dact_all[6]
    (dcond_rows,), _ = _rw_vjp(lambda t: (jax.nn.silu(t),), [cond], [], [[dact]], [True], [], [F32], name="cond_silu_bwd")
    g_c_ctx = dcond_rows[8]

    rs_in, tok = _rs_stage2(rs_in, [g_c_ctx], "in")

    grads, delta, new_m, new_v = {}, {}, {}, {}

    def update(members, reds, anchor):
        deltas = []
        for nme, red in zip(members, reds):
            res = _adamw(w[nme][0], red, m[nme][0], v[nme][0], name=f"adamw_{nme}", anchor=anchor)
            grads[nme], delta[nme], new_m[nme], new_v[nme] = (r.reshape(w[nme].shape) for r in res)
            deltas.append(res[1])
            anchor = None
        return deltas

    rs_ffn, tok = _rs_stage3(rs_ffn, [tok], "ffn")
    done = update(['w_mod'], [g_w_mod], tok)
    red_ffn = _rs_stage4(rs_ffn, done, "ffn")
    rs_mid, tok = _rs_stage3(rs_mid, red_ffn[:1], "mid")
    done = update(['w_ffn_out', 'w_ffn_in'], red_ffn, tok)
    red_mid = _rs_stage4(rs_mid, done, "mid")
    rs_kv, tok = _rs_stage3(rs_kv, red_mid[:1], "kv")
    done = update(['w_out', 'w_glu', 'w_mla_o'], red_mid, tok)
    red_kv = _rs_stage4(rs_kv, done, "kv")
    rs_in, tok = _rs_stage3(rs_in, red_kv[:1], "in")
    done = update(['w_uq', 'w_ukv'], red_kv, tok)
    red_in = _rs_stage4(rs_in, done, "in")
    update(['w_in'], red_in[:1], None)
    small_mine = red_in[-1]
    small_buf = _into_slot(small_mine, me_chip, 4, F32, name="small_grads_slot")
    small_all = _allgather_shards([small_buf], name="ag_small_grads")[0].reshape(4 * small_rows, LANES)
    g_small_red = dict(zip(SMALL, _unpack(small_all, [w[nme] for nme in SMALL])))
    rest = SMALL + ['c_ctx', 'b_mod']
    g_rest = dict(g_small_red, c_ctx=g_c_ctx, b_mod=g_b_mod)
    rows_rest = -(-sum(w[nme].size for nme in rest) // (LANES * 16)) * 16
    packed = [_pack([src[nme] for nme in rest], LANES, rows_rest) for src in (w, g_rest, m, v)]
    res = _adamw(*packed, name="adamw_small")
    for dst, buf in zip((grads, delta, new_m, new_v), res):
        dst.update(zip(rest, _unpack(buf, [w[nme] for nme in rest])))
    return (loss, grad_x, *[grads[nme] for nme in WEIGHTS], *[delta[nme] for nme in WEIGHTS],
            *[new_m[nme] for nme in WEIGHTS], *[new_v[nme] for nme in WEIGHTS])


def kernel(x, c, ctx, c_ctx, w_mod, b_mod, norm1, norm2, w_in, s5_a_re, s5_a_im, s5_log_dt, s5_b_re, s5_b_im, s5_c_re, s5_c_im, s5_d, w_glu, q_norm, kv_norm, w_uq, w_ukv, w_mla_o, w_out, w_ffn_in, w_ffn_out, norm_f, loss_target, m_c_ctx, m_w_mod, m_b_mod, m_norm1, m_norm2, m_w_in, m_s5_a_re, m_s5_a_im, m_s5_log_dt, m_s5_b_re, m_s5_b_im, m_s5_c_re, m_s5_c_im, m_s5_d, m_w_glu, m_q_norm, m_kv_norm, m_w_uq, m_w_ukv, m_w_mla_o, m_w_out, m_w_ffn_in, m_w_ffn_out, m_norm_f, v_c_ctx, v_w_mod, v_b_mod, v_norm1, v_norm2, v_w_in, v_s5_a_re, v_s5_a_im, v_s5_log_dt, v_s5_b_re, v_s5_b_im, v_s5_c_re, v_s5_c_im, v_s5_d, v_w_glu, v_q_norm, v_kv_norm, v_w_uq, v_w_ukv, v_w_mla_o, v_w_out, v_w_ffn_in, v_w_ffn_out, v_norm_f):
    w = dict(c_ctx=c_ctx, w_mod=w_mod, b_mod=b_mod, norm1=norm1, norm2=norm2, w_in=w_in, s5_a_re=s5_a_re, s5_a_im=s5_a_im,
             s5_log_dt=s5_log_dt, s5_b_re=s5_b_re, s5_b_im=s5_b_im, s5_c_re=s5_c_re, s5_c_im=s5_c_im, s5_d=s5_d, w_glu=w_glu,
             q_norm=q_norm, kv_norm=kv_norm, w_uq=w_uq, w_ukv=w_ukv, w_mla_o=w_mla_o, w_out=w_out, w_ffn_in=w_ffn_in,
             w_ffn_out=w_ffn_out, norm_f=norm_f)
    m = dict(c_ctx=m_c_ctx, w_mod=m_w_mod, b_mod=m_b_mod, norm1=m_norm1, norm2=m_norm2, w_in=m_w_in, s5_a_re=m_s5_a_re,
             s5_a_im=m_s5_a_im, s5_log_dt=m_s5_log_dt, s5_b_re=m_s5_b_re, s5_b_im=m_s5_b_im, s5_c_re=m_s5_c_re,
             s5_c_im=m_s5_c_im, s5_d=m_s5_d, w_glu=m_w_glu, q_norm=m_q_norm, kv_norm=m_kv_norm, w_uq=m_w_uq, w_ukv=m_w_ukv,
             w_mla_o=m_w_mla_o, w_out=m_w_out, w_ffn_in=m_w_ffn_in, w_ffn_out=m_w_ffn_out, norm_f=m_norm_f)
    v = dict(c_ctx=v_c_ctx, w_mod=v_w_mod, b_mod=v_b_mod, norm1=v_norm1, norm2=v_norm2, w_in=v_w_in, s5_a_re=v_s5_a_re,
             s5_a_im=v_s5_a_im, s5_log_dt=v_s5_log_dt, s5_b_re=v_s5_b_re, s5_b_im=v_s5_b_im, s5_c_re=v_s5_c_re,
             s5_c_im=v_s5_c_im, s5_d=v_s5_d, w_glu=v_w_glu, q_norm=v_q_norm, kv_norm=v_kv_norm, w_uq=v_w_uq, w_ukv=v_w_ukv,
             w_mla_o=v_w_mla_o, w_out=v_w_out, w_ffn_in=v_w_ffn_in, w_ffn_out=v_w_ffn_out, norm_f=v_norm_f)
    return _step(x, c, ctx, loss_target, w, m, v)
```

```python
import functools
import math

import jax
import jax.numpy as jnp
from jax import lax
from jax.experimental import pallas as pl
from jax.experimental.pallas import tpu as pltpu

F32 = jnp.float32
BF16 = jnp.bfloat16

EPS = 1e-6
GRID_W = 64
S5_GROUP = 16
S5_STATE = 64
MLA_HEADS = 8
QK_NOPE = 128
QK_ROPE = 64
V_DIM = 128
ROPE_BASE = 10000.0
ATTN_SCALE = (QK_NOPE + QK_ROPE) ** -0.5
ADAM_LR = 0.001
ADAM_B1 = 0.9
ADAM_B2 = 0.999
ADAM_EPS = 1e-08
ADAM_WD = 0.01
ADAM_STEP = 10

SUBLANES = 8
LANES = 128
V7X_VMEM_BYTES = 64 * 1024 * 1024
VMEM_LIMIT = (V7X_VMEM_BYTES * 7) // 8
N_SEG = 2 * SUBLANES
S5_BLOCK_GROUPS = 8
MESH = pl.DeviceIdType.MESH


def _pick(n, target, mult):
    best = None
    d = mult
    while d <= min(n, target):
        if n % d == 0:
            best = d
        d += mult
    return n if best is None else best


def _cparams(sem=None):
    return pltpu.CompilerParams(dimension_semantics=sem, vmem_limit_bytes=VMEM_LIMIT)


MM_VMEM_BUDGET = (V7X_VMEM_BYTES * 5) // 8


def _mm(a, b, *, ta=False, tb=False, out_dtype=F32, name, a_shards=1, b_shards=1, out_shards=1):
    if ta:
        K, M = a.shape
    else:
        M, K = a.shape[-2], a.shape[-1] * a_shards
    if tb:
        N, K2 = b.shape[-2], b.shape[-1] * b_shards
    else:
        K2, N = b.shape[-2], b.shape[-1] * b_shards
    assert K == K2, (a.shape, b.shape, ta, tb)
    n_unit = N // max(out_shards, 1 if tb else b_shards)
    k_unit = K // max(a_shards, b_shards if tb else 1)
    tn = _pick(n_unit, 1024, LANES)
    tm = _pick(M, 1024 if tn >= 512 else 2048, LANES if ta else 16)
    sa, sb, so = a.dtype.itemsize, b.dtype.itemsize, jnp.dtype(out_dtype).itemsize
    k_mult = LANES if (not ta or tb) else 16
    tk = k_mult if k_unit % k_mult == 0 else k_unit
    for cand in range(k_mult, k_unit + 1, k_mult):
        if k_unit % cand == 0 and 2 * cand * (tm * sa + tn * sb) + tm * tn * (4 + 2 * so) <= MM_VMEM_BUDGET:
            tk = cand
    nk = K // tk
    dims = (((0 if ta else 1,), (1 if tb else 0,)), ((), ()))

    def body(a_ref, b_ref, o_ref, *scratch):
        part = lax.dot_general(a_ref[...].astype(BF16), b_ref[...].astype(BF16), dims, preferred_element_type=F32)
        if nk == 1:
            o_ref[...] = part.astype(o_ref.dtype)
            return
        acc_ref, = scratch
        k = pl.program_id(2)

        @pl.when(k == 0)
        def _():
            acc_ref[...] = part

        @pl.when(k > 0)
        def _():
            acc_ref[...] += part

        @pl.when(k == nk - 1)
        def _():
            o_ref[...] = acc_ref[...].astype(o_ref.dtype)

    if ta:
        a_spec = pl.BlockSpec((tk, tm), lambda i, j, k: (k, i))
    elif a_shards == 1:
        a_spec = pl.BlockSpec((tm, tk), lambda i, j, k: (i, k))
    else:
        akb = (K // a_shards) // tk
        a_spec = pl.BlockSpec((None, tm, tk), lambda i, j, k: (k // akb, i, k % akb))
    if b_shards == 1:
        b_spec = pl.BlockSpec((tn, tk), lambda i, j, k: (j, k)) if tb else pl.BlockSpec((tk, tn), lambda i, j, k: (k, j))
    elif tb:
        kpb = (K // b_shards) // tk
        b_spec = pl.BlockSpec((None, tn, tk), lambda i, j, k: (k // kpb, j, k % kpb))
    else:
        npb = (N // b_shards) // tn
        b_spec = pl.BlockSpec((None, tk, tn), lambda i, j, k: (j // npb, k, j % npb))
    if out_shards == 1:
        out_spec = pl.BlockSpec((tm, tn), lambda i, j, k: (i, j))
        out_shape = jax.ShapeDtypeStruct((M, N), out_dtype)
    else:
        opb = (N // out_shards) // tn
        out_spec = pl.BlockSpec((None, tm, tn), lambda i, j, k: (j // opb, i, j % opb))
        out_shape = jax.ShapeDtypeStruct((out_shards, M, N // out_shards), out_dtype)
    return pl.pallas_call(
        body, name=name, grid=(M // tm, N // tn, nk),
        in_specs=[a_spec, b_spec], out_specs=out_spec, out_shape=out_shape,
        scratch_shapes=[pltpu.VMEM((tm, tn), F32)] if nk > 1 else [],
        compiler_params=_cparams(("parallel", "parallel", "arbitrary")),
    )(a, b)


FFN_TILE_ROWS = 1024


def _ffn_in_swiglu(x, w4, *, name):
    M, K = x.shape
    S, _, ns = w4.shape
    half = S * ns // 2
    tn = _pick(ns, 512, LANES)
    tm = _pick(M, FFN_TILE_ROWS, 16)
    npb = ns // tn

    def body(x_ref, wa_ref, wb_ref, h_ref, ab_ref):
        xb = x_ref[...].astype(BF16)
        a = jnp.dot(xb, wa_ref[...].astype(BF16), preferred_element_type=F32)
        b = jnp.dot(xb, wb_ref[...].astype(BF16), preferred_element_type=F32)
        h_ref[...] = (jax.nn.silu(a) * b).astype(h_ref.dtype)
        ab_ref[0] = a.astype(ab_ref.dtype)
        ab_ref[1] = b.astype(ab_ref.dtype)

    return pl.pallas_call(
        body, name=name, grid=(M // tm, half // tn),
        in_specs=[pl.BlockSpec((tm, K), lambda i, j: (i, 0)),
                  pl.BlockSpec((None, K, tn), lambda i, j: (j // npb, 0, j % npb)),
                  pl.BlockSpec((None, K, tn), lambda i, j: (S // 2 + j // npb, 0, j % npb))],
        out_specs=[pl.BlockSpec((tm, tn), lambda i, j: (i, j)), pl.BlockSpec((2, tm, tn), lambda i, j: (0, i, j))],
        out_shape=[jax.ShapeDtypeStruct((M, half), BF16), jax.ShapeDtypeStruct((2, M, half), BF16)],
        compiler_params=_cparams(("parallel", "parallel")),
    )(x, w4, w4)


def _ffn_out_dx_swiglu(dy, w, ab, *, name):
    M, D = dy.shape
    n2 = w.shape[0]
    tn = _pick(n2, 512, LANES)
    tm = _pick(M, FFN_TILE_ROWS, 16)

    def body(dy_ref, w_ref, ab_ref, o_ref):
        dh = lax.dot_general(dy_ref[...].astype(BF16), w_ref[...].astype(BF16), NT_DIMS, preferred_element_type=F32)
        a, b = ab_ref[0].astype(F32), ab_ref[1].astype(F32)
        s = jax.nn.sigmoid(a)
        o_ref[0] = (dh * b * (s * (1.0 + a * (1.0 - s)))).astype(o_ref.dtype)
        o_ref[1] = (dh * (a * s)).astype(o_ref.dtype)

    return pl.pallas_call(
        body, name=name, grid=(M // tm, n2 // tn),
        in_specs=[pl.BlockSpec((tm, D), lambda i, j: (i, 0)), pl.BlockSpec((tn, D), lambda i, j: (j, 0)),
                  pl.BlockSpec((2, tm, tn), lambda i, j: (0, i, j))],
        out_specs=pl.BlockSpec((2, tm, tn), lambda i, j: (0, i, j)),
        out_shape=jax.ShapeDtypeStruct((2, M, n2), BF16),
        compiler_params=_cparams(("parallel", "parallel")),
    )(dy, w, ab)


ROW_TILE_BYTES = 6 * 1024 * 1024
STREAM_TILE_BYTES = 14 * 1024 * 1024


def _row_tile(tiled, extra_bytes=0, budget=ROW_TILE_BYTES):
    rows = tiled[0].shape[0]
    per_row = sum(a.shape[1] * 4 for a in tiled) + extra_bytes
    target = max(SUBLANES, budget // max(per_row, 1))
    return _pick(rows, min(target, 512), 16)


def _rw(f, tiled, bcast, out_dtypes, *, name, anchor=None, tile_bytes=ROW_TILE_BYTES):
    nt, nb = len(tiled), len(bcast)
    rows = tiled[0].shape[0]
    outs_aval = jax.eval_shape(f, *[jax.ShapeDtypeStruct((16, a.shape[1]), F32) for a in tiled],
                               *[jax.ShapeDtypeStruct(b.shape, F32) for b in bcast])
    widths = [o.shape[1] for o in outs_aval]
    tm = _row_tile(tiled, sum(w * 4 for w in widths), tile_bytes)

    extra = [] if anchor is None else [anchor]
    n_in = nt + nb + len(extra)

    def body(*refs):
        tin = [r[...].astype(F32) for r in refs[:nt]]
        bin_ = [r[...].astype(F32) for r in refs[nt:nt + nb]]
        outs = f(*tin, *bin_)
        for o_ref, o in zip(refs[n_in:], outs):
            o_ref[...] = o.astype(o_ref.dtype)

    in_specs = [pl.BlockSpec((tm, a.shape[1]), lambda i: (i, 0)) for a in tiled]
    in_specs += [pl.BlockSpec(b.shape, lambda i: (0, 0)) for b in bcast + extra]
    res = pl.pallas_call(
        body, name=name, grid=(rows // tm,), in_specs=in_specs,
        out_specs=[pl.BlockSpec((tm, w), lambda i: (i, 0)) for w in widths],
        out_shape=[jax.ShapeDtypeStruct((rows, w), dt) for w, dt in zip(widths, out_dtypes)],
        compiler_params=_cparams(("parallel",)),
    )(*tiled, *bcast, *extra)
    return list(res)


def _rw_vjp(f, tiled, bcast, cts, need_t, need_b, t_dtypes, *, name, anchor=None):
    nt, nb = len(tiled), len(bcast)
    rows = tiled[0].shape[0]
    flat_cts = [c for group in cts for c in group]
    t_idx = [i for i in range(nt) if need_t[i]]
    b_idx = [i for i in range(nb) if need_b[i]]
    tm = _row_tile(list(tiled) + flat_cts, sum(tiled[i].shape[1] * 4 for i in t_idx))
    nc = len(flat_cts)
    extra = [] if anchor is None else [anchor]

    def body(*refs):
        i = pl.program_id(0)
        tin = [r[...].astype(F32) for r in refs[:nt]]
        bin_ = [r[...].astype(F32) for r in refs[nt:nt + nb]]
        ct_refs = refs[nt + nb:nt + nb + nc]
        out_refs = refs[nt + nb + nc + len(extra):]
        outs, vjp_fn = jax.vjp(f, *tin, *bin_)
        ct_vals, pos = [], 0
        for o, group in zip(outs, cts):
            acc = jnp.zeros_like(o)
            for _ in group:
                acc = acc + ct_refs[pos][...].astype(F32)
                pos += 1
            ct_vals.append(acc)
        grads = vjp_fn(tuple(ct_vals))
        for o_ref, k in zip(out_refs[:len(t_idx)], t_idx):
            o_ref[...] = grads[k].astype(o_ref.dtype)
        for o_ref, k in zip(out_refs[len(t_idx):], b_idx):
            @pl.when(i == 0)
            def _(o_ref=o_ref):
                o_ref[...] = jnp.zeros_like(o_ref)

            o_ref[...] += grads[nt + k]

    in_specs = [pl.BlockSpec((tm, a.shape[1]), lambda i: (i, 0)) for a in tiled]
    in_specs += [pl.BlockSpec(b.shape, lambda i: (0, 0)) for b in bcast]
    in_specs += [pl.BlockSpec((tm, c.shape[1]), lambda i: (i, 0)) for c in flat_cts]
    in_specs += [pl.BlockSpec(e.shape, lambda i: (0, 0)) for e in extra]
    out_specs = [pl.BlockSpec((tm, tiled[k].shape[1]), lambda i: (i, 0)) for k in t_idx]
    out_specs += [pl.BlockSpec(bcast[k].shape, lambda i: (0, 0)) for k in b_idx]
    out_shape = [jax.ShapeDtypeStruct(tiled[k].shape, dt) for k, dt in zip(t_idx, t_dtypes)]
    out_shape += [jax.ShapeDtypeStruct(bcast[k].shape, F32) for k in b_idx]
    res = pl.pallas_call(
        body, name=name, grid=(rows // tm,), in_specs=in_specs, out_specs=out_specs, out_shape=out_shape,
        compiler_params=_cparams(("arbitrary",)),
    )(*tiled, *bcast, *flat_cts, *extra)
    res = list(res)
    return res[:len(t_idx)], res[len(t_idx):]


def _rms(x, g):
    return x * lax.rsqrt(jnp.mean(x * x, axis=-1, keepdims=True) + EPS) * g


def _f_norm_mod(x, g, sc, sh):
    return (_rms(x, g) * (1.0 + sc) + sh,)


def _f_norm_mod_keep(x, g, sc, sh):
    return (_rms(x, g) * (1.0 + sc) + sh, x)


@jax.custom_vjp
def _swap16(x):
    w = x.shape[-1]
    lane = lax.broadcasted_iota(jnp.int32, x.shape, x.ndim - 1)
    return jnp.where((lane & 16) == 0, pltpu.roll(x, w - 16, x.ndim - 1), pltpu.roll(x, 16, x.ndim - 1))


_swap16.defvjp(lambda x: (_swap16(x), None), lambda _, g: (_swap16(g),))


def _rope(x, cos, sin):
    return x * cos + _swap16(x) * sin


def _make_f_post_in(sw, q_rank, kv_rank, with_q):
    o1, o2, o3 = sw, sw + q_rank, sw + q_rank + kv_rank

    if with_q:
        def f(ha, cos, sin, qg, kvg):
            u = ha[:, :o1]
            cqn = _rms(ha[:, o1:o2], qg)
            ckvn = _rms(ha[:, o2:o3], kvg)
            kr = _rope(ha[:, o3:o3 + LANES], cos, sin)
            return u, cqn, ckvn, kr
    else:
        def f(ha, kvg):
            return ha[:, :o1], _rms(ha[:, o2:o3], kvg), ha[:, o3:o3 + LANES]
    return f


def _f_qpost(q2, cos, sin):
    parts = []
    for h in range(q2.shape[1] // (2 * LANES)):
        o = 2 * LANES * h
        parts += [q2[:, o:o + LANES], _rope(q2[:, o + LANES:o + 2 * LANES], cos, sin)]
    return (jnp.concatenate(parts, axis=1),)


def _f_s5post(u, r0, r1, d):
    return (jax.nn.gelu(d * u + r0 + r1, approximate=True),)


def _f_merge(ab, bm, gt):
    d = bm.shape[1]
    br_s5 = ab[:, :d] * jax.nn.sigmoid(ab[:, d:])
    g = jax.nn.sigmoid(gt)
    return (g[:, :d] * br_s5 + g[:, d:] * bm,)


def _f_resid_norm(x, out, g1, n2, sc2, sh2):
    x1 = x + g1 * out
    return x1, _rms(x1, n2) * (1.0 + sc2) + sh2


def _f_final(x1, f, tgt, g2, nf):
    y = _rms(x1 + g2 * f, nf)
    return (0.5 * jnp.mean(jnp.square(y - tgt), axis=-1, keepdims=True),)


def _bd_fanin(xs, ws, *, name):
    nw = len(ws)
    nb, kb, nn = ws[0].shape
    T = xs[0].shape[0]
    tm = _pick(T, 512, 16)

    def body(*refs):
        acc = None
        for x_ref, w_ref in zip(refs[:nw], refs[nw:2 * nw]):
            t = jnp.dot(x_ref[...].astype(BF16), w_ref[0].astype(BF16), preferred_element_type=F32)
            acc = t if acc is None else acc + t
        refs[2 * nw][...] = acc

    return pl.pallas_call(
        body, name=name, grid=(nb, T // tm),
        in_specs=[pl.BlockSpec((tm, kb), lambda j, i: (i, j))] * nw + [pl.BlockSpec((1, kb, nn), lambda j, i: (j, 0, 0))] * nw,
        out_specs=pl.BlockSpec((tm, nn), lambda j, i: (i, j)),
        out_shape=jax.ShapeDtypeStruct((T, nb * nn), F32),
        compiler_params=_cparams(("parallel", "parallel")),
    )(*xs, *ws)


def _bd_dw(xs, dys, nb, *, name):
    npair = len(xs)
    T = xs[0].shape[0]
    kb = xs[0].shape[1] // nb
    nn = dys[0].shape[1] // nb
    tm = _pick(T, 512, 16)
    dims = (((0,), (0,)), ((), ()))

    def body(*refs):
        i = pl.program_id(1)
        for x_ref, d_ref, o_ref in zip(refs[:npair], refs[npair:2 * npair], refs[2 * npair:]):
            @pl.when(i == 0)
            def _(o_ref=o_ref):
                o_ref[...] = jnp.zeros_like(o_ref)

            o_ref[0] += lax.dot_general(x_ref[...].astype(BF16), d_ref[...].astype(BF16), dims,
                                        preferred_element_type=F32)

    return list(pl.pallas_call(
        body, name=name, grid=(nb, T // tm),
        in_specs=[pl.BlockSpec((tm, kb), lambda j, i: (i, j))] * npair + [pl.BlockSpec((tm, nn), lambda j, i: (i, j))] * npair,
        out_specs=[pl.BlockSpec((1, kb, nn), lambda j, i: (j, 0, 0))] * npair,
        out_shape=[jax.ShapeDtypeStruct((nb, kb, nn), F32)] * npair,
        compiler_params=_cparams(("parallel", "arbitrary")),
    )(*xs, *dys))


def _cmul(ar, ai, br, bi):
    return ar * br - ai * bi, ar * bi + ai * br


def _cpow(lr, li, n):
    rr, ri = None, None
    br, bi = lr, li
    while n:
        if n & 1:
            rr, ri = (br, bi) if rr is None else _cmul(rr, ri, br, bi)
        n >>= 1
        if n:
            br, bi = _cmul(br, bi, br, bi)
    return rr, ri


SCAN_MM_ROWS = 512


def _s5_scan(x, w_re, w_im, lam_re, lam_im, h0_re, h0_im, e0_re, e0_im, *, reverse, name, readout=None):
    rows = x.shape[0]
    nb, kb, cb = w_re.shape
    C = nb * cb
    n = rows // N_SEG
    mm_rows = _pick(rows, SCAN_MM_ROWS, 16)
    seg_order = list(range(N_SEG))[::-1] if reverse else list(range(N_SEG))
    s_first, s_last = seg_order[0], seg_order[-1]
    n_ro = 0 if readout is None else 2

    def body(x_ref, wr_ref, wi_ref, lr_ref, li_ref, h0r_ref, h0i_ref, e0r_ref, e0i_ref, *rest):
        ro_refs, (hr_ref, hi_ref, htr_ref, hti_ref), y_refs = rest[:n_ro], rest[n_ro:n_ro + 4], rest[n_ro + 4:-2]
        locr_ref, loci_ref = rest[-2:]
        shape = (N_SEG, cb)
        lr = jnp.broadcast_to(lr_ref[...], shape)
        li = jnp.broadcast_to(li_ref[...], shape)
        row = lax.broadcasted_iota(jnp.int32, shape, 0)

        def step_of(k):
            return (n - 1 - k) if reverse else k

        def rows_of(k):
            return pl.ds(pl.multiple_of(step_of(k) * N_SEG, N_SEG), N_SEG)

        wr, wi = wr_ref[...].astype(BF16), wi_ref[...].astype(BF16)
        for r0 in range(0, rows, mm_rows):
            xb = x_ref[r0:r0 + mm_rows, :].astype(BF16)
            locr_ref[r0:r0 + mm_rows, :] = jnp.dot(xb, wr, preferred_element_type=F32)
            loci_ref[r0:r0 + mm_rows, :] = jnp.dot(xb, wi, preferred_element_type=F32)

        first = row == s_first
        hr = locr_ref[rows_of(0), :] + jnp.where(first, e0r_ref[...], 0.0)
        hi = loci_ref[rows_of(0), :] + jnp.where(first, e0i_ref[...], 0.0)
        locr_ref[rows_of(0), :] = hr
        loci_ref[rows_of(0), :] = hi

        def pass1(k, carry):
            hr, hi = carry
            pr, pi = _cmul(lr, li, hr, hi)
            hr = pr + locr_ref[rows_of(k), :]
            hi = pi + loci_ref[rows_of(k), :]
            locr_ref[rows_of(k), :] = hr
            loci_ref[rows_of(k), :] = hi
            return hr, hi

        er, ei = lax.fori_loop(1, n, pass1, (hr, hi))

        lnr, lni = _cpow(lr_ref[...], li_ref[...], n)
        cr, ci = h0r_ref[...], h0i_ref[...]
        cin_r = jnp.zeros(shape, F32)
        cin_i = jnp.zeros(shape, F32)
        for s in seg_order:
            cin_r = jnp.where(row == s, cr, cin_r)
            cin_i = jnp.where(row == s, ci, cin_i)
            if s != s_last:
                pr, pi = _cmul(lnr, lni, cr, ci)
                cr = pr + jnp.sum(jnp.where(row == s, er, 0.0), axis=0, keepdims=True)
                ci = pi + jnp.sum(jnp.where(row == s, ei, 0.0), axis=0, keepdims=True)

        def pass2(k, carry):
            pr, pi, _, _ = carry
            ar, ai = _cmul(pr, pi, cin_r, cin_i)
            hr = locr_ref[rows_of(k), :] + ar
            hi = loci_ref[rows_of(k), :] + ai
            hr_ref[rows_of(k), :] = hr.astype(hr_ref.dtype)
            hi_ref[rows_of(k), :] = hi.astype(hi_ref.dtype)
            npr, npi = _cmul(pr, pi, lr, li)
            return npr, npi, hr, hi

        _, _, last_r, last_i = lax.fori_loop(0, n, pass2, (lr, li, er, ei))
        htr_ref[...] = jnp.sum(jnp.where(row == s_last, last_r, 0.0), axis=0, keepdims=True)
        hti_ref[...] = jnp.sum(jnp.where(row == s_last, last_i, 0.0), axis=0, keepdims=True)

        if readout is not None:
            cr, ci = ro_refs[0][...].astype(BF16), ro_refs[1][...].astype(BF16)
            for r0 in range(0, rows, mm_rows):
                y_refs[0][r0:r0 + mm_rows, :] = (
                    jnp.dot(hr_ref[r0:r0 + mm_rows, :].astype(BF16), cr, preferred_element_type=F32)
                    + jnp.dot(hi_ref[r0:r0 + mm_rows, :].astype(BF16), ci, preferred_element_type=F32))

    big = pl.BlockSpec((rows, cb), lambda j: (0, j))
    vec = pl.BlockSpec((1, cb), lambda j: (0, j))
    wspec = pl.BlockSpec((None, kb, cb), lambda j: (j, 0, 0))
    in_specs = [pl.BlockSpec((rows, kb), lambda j: (0, j)), wspec, wspec] + [vec] * 6
    out_specs = [big, big, vec, vec]
    out_shape = [jax.ShapeDtypeStruct((rows, C), BF16)] * 2 + [jax.ShapeDtypeStruct((1, C), F32)] * 2
    extra = []
    if readout is not None:
        pb = readout[0].shape[2]
        in_specs += [pl.BlockSpec((None, cb, pb), lambda j: (j, 0, 0))] * 2
        out_specs.append(pl.BlockSpec((rows, pb), lambda j: (0, j)))
        out_shape.append(jax.ShapeDtypeStruct((rows, nb * pb), F32))
        extra = list(readout)
    return pl.pallas_call(
        body, name=name, grid=(nb,), in_specs=in_specs, out_specs=out_specs, out_shape=out_shape,
        scratch_shapes=[pltpu.VMEM((rows, cb), F32)] * 2,
        compiler_params=_cparams(("parallel",)),
    )(x, w_re, w_im, lam_re, lam_im, h0_re, h0_im, e0_re, e0_im, *extra)


def _s5_dlam(mu_re, mu_im, h_re, h_im, h0_re, h0_im, *, reverse, name):
    rows, C = h_re.shape
    n = rows // N_SEG
    cb = _pick(C, 256, LANES)
    s_first = N_SEG - 1 if reverse else 0

    def body(mr_ref, mi_ref, hr_ref, hi_ref, h0r_ref, h0i_ref, dr_ref, di_ref):
        shape = (N_SEG, cb)
        row = lax.broadcasted_iota(jnp.int32, shape, 0)

        def rows_of(k):
            step = (n - 1 - k) if reverse else k
            return pl.ds(pl.multiple_of(step * N_SEG, N_SEG), N_SEG)

        def term(k, pr, pi):
            mr, mi = mr_ref[rows_of(k), :].astype(F32), mi_ref[rows_of(k), :].astype(F32)
            return mr * pr + mi * pi, mi * pr - mr * pi

        shift = N_SEG - 1 if reverse else 1
        pr = jnp.where(row == s_first, h0r_ref[...], pltpu.roll(hr_ref[rows_of(n - 1), :].astype(F32), shift, 0))
        pi = jnp.where(row == s_first, h0i_ref[...], pltpu.roll(hi_ref[rows_of(n - 1), :].astype(F32), shift, 0))
        acc = term(0, pr, pi)

        def loop(k, acc):
            tr, ti = term(k, hr_ref[rows_of(k - 1), :].astype(F32), hi_ref[rows_of(k - 1), :].astype(F32))
            return acc[0] + tr, acc[1] + ti

        ar, ai = lax.fori_loop(1, n, loop, acc)
        dr_ref[...] = jnp.sum(ar, axis=0, keepdims=True)
        di_ref[...] = jnp.sum(ai, axis=0, keepdims=True)

    big = pl.BlockSpec((rows, cb), lambda j: (0, j))
    vec = pl.BlockSpec((1, cb), lambda j: (0, j))
    return pl.pallas_call(
        body, name=name, grid=(C // cb,),
        in_specs=[big] * 4 + [vec] * 2, out_specs=[vec, vec],
        out_shape=[jax.ShapeDtypeStruct((1, C), F32)] * 2,
        compiler_params=_cparams(("parallel",)),
    )(mu_re, mu_im, h_re, h_im, h0_re, h0_im)


NT_DIMS = (((1,), (1,)), ((), ()))
TN_DIMS = (((0,), (0,)), ((), ()))


ATTN_Q_ROWS = 512


def _attn_exp(q, kvh, kr):
    s = (lax.dot_general(q[:, :LANES], kvh[:, :LANES], NT_DIMS, preferred_element_type=F32)
         + lax.dot_general(q[:, LANES:], kr, NT_DIMS, preferred_element_type=F32))
    e = jnp.exp2((s - jnp.max(s, axis=-1, keepdims=True)) * (ATTN_SCALE * math.log2(math.e)))
    return e, jnp.sum(e, axis=-1, keepdims=True)


def _attn_specs(L, T, tq):
    return [
        pl.BlockSpec((tq, 2 * LANES), lambda h, i: (i, h)),
        pl.BlockSpec((T, 2 * LANES), lambda h, i: (0, h)),
        pl.BlockSpec((T, LANES), lambda h, i: (0, 0)),
    ]


def _attn_fwd(qq, kv, kr, *, name):
    L, T = qq.shape[0], kv.shape[0]
    tq = _pick(L, ATTN_Q_ROWS // 2, 16)

    def body(q_ref, kv_ref, kr_ref, o_ref):
        kvh = kv_ref[...]
        e, l = _attn_exp(q_ref[...], kvh, kr_ref[...])
        o_ref[...] = (jnp.dot(e.astype(BF16), kvh[:, LANES:], preferred_element_type=F32) * (1.0 / l)).astype(o_ref.dtype)

    return pl.pallas_call(
        body, name=name, grid=(MLA_HEADS, L // tq), in_specs=_attn_specs(L, T, tq),
        out_specs=pl.BlockSpec((tq, LANES), lambda h, i: (i, h)),
        out_shape=jax.ShapeDtypeStruct((L, MLA_HEADS * V_DIM), BF16),
        compiler_params=_cparams(("parallel", "parallel")),
    )(qq, kv, kr)


def _attn_bwd(qq, kv, kr, do, *, name):
    L, T = qq.shape[0], kv.shape[0]
    H = MLA_HEADS
    tq = _pick(L, ATTN_Q_ROWS, 16)
    nq = L // tq

    def body(q_ref, kv_ref, kr_ref, do_ref, dq_ref, dkv_ref, dkr_ref, dkn_acc, dv_acc):
        h, i = pl.program_id(0), pl.program_id(1)
        q, kvh, krv, dov = q_ref[...], kv_ref[...], kr_ref[...], do_ref[...]
        e, l = _attn_exp(q, kvh, krv)
        inv = 1.0 / l
        ps = e * (inv * ATTN_SCALE)
        t = lax.dot_general(dov, kvh[:, LANES:], NT_DIMS, preferred_element_type=F32) * ps
        ds = (t - ps * (jnp.sum(t, axis=-1, keepdims=True) * (1.0 / ATTN_SCALE))).astype(BF16)
        dq_ref[:, :LANES] = jnp.dot(ds, kvh[:, :LANES], preferred_element_type=F32)
        dq_ref[:, LANES:] = jnp.dot(ds, krv, preferred_element_type=F32)

        @pl.when(i == 0)
        def _():
            dkn_acc[...] = jnp.zeros_like(dkn_acc)
            dv_acc[...] = jnp.zeros_like(dv_acc)

        @pl.when((i == 0) & (h == 0))
        def _():
            dkr_ref[...] = jnp.zeros_like(dkr_ref)

        dv_acc[...] += lax.dot_general(e.astype(BF16), (dov.astype(F32) * inv).astype(BF16), TN_DIMS,
                                       preferred_element_type=F32)
        dkn_acc[...] += lax.dot_general(ds, q[:, :LANES], TN_DIMS, preferred_element_type=F32)
        dkr_ref[...] += lax.dot_general(ds, q[:, LANES:], TN_DIMS, preferred_element_type=F32)

        @pl.when(i == nq - 1)
        def _():
            dkv_ref[:, :LANES] = dkn_acc[...].astype(dkv_ref.dtype)
            dkv_ref[:, LANES:] = dv_acc[...].astype(dkv_ref.dtype)

    in_specs = _attn_specs(L, T, tq) + [pl.BlockSpec((tq, LANES), lambda h, i: (i, h))]
    return pl.pallas_call(
        body, name=name, grid=(H, L // tq), in_specs=in_specs,
        out_specs=[pl.BlockSpec((tq, 2 * LANES), lambda h, i: (i, h)), pl.BlockSpec((T, 2 * LANES), lambda h, i: (0, h)),
                   pl.BlockSpec((T, LANES), lambda h, i: (0, 0))],
        out_shape=[jax.ShapeDtypeStruct((L, H * 2 * LANES), F32), jax.ShapeDtypeStruct((T, H * 2 * LANES), BF16),
                   jax.ShapeDtypeStruct((T, LANES), F32)],
        scratch_shapes=[pltpu.VMEM((T, LANES), F32), pltpu.VMEM((T, LANES), F32)],
        compiler_params=_cparams(("arbitrary", "arbitrary")),
    )(qq, kv, kr, do)


def _adamw(w, g, m, v, *, name, anchor=None):
    c1 = 1.0 - ADAM_B1 ** ADAM_STEP
    c2 = 1.0 - ADAM_B2 ** ADAM_STEP

    def f(w, g, m, v):
        m = ADAM_B1 * m + (1.0 - ADAM_B1) * g
        v = ADAM_B2 * v + (1.0 - ADAM_B2) * jnp.square(g)
        delta = -ADAM_LR * ((m / c1) / (jnp.sqrt(v / c2) + ADAM_EPS) + ADAM_WD * w)
        return g, delta, m, v

    return _rw(f, [w, g, m, v], [], [F32] * 4, name=name, anchor=anchor, tile_bytes=STREAM_TILE_BYTES)


def _slab_rows(rows, cols, n_arrays):
    return _pick(rows, max(16, (8 * 1024 * 1024) // (cols * 4 * n_arrays)), 16)


def _scalars(*vals):
    return jnp.stack([jnp.asarray(v, jnp.int32) for v in vals])


def _into_slot(src, slot, nslots, dtype, *, name):
    R, C = src.shape
    tr = _slab_rows(R, C, 2)

    def body(s_ref, x_ref, o_ref):
        o_ref[...] = x_ref[...].astype(o_ref.dtype)

    return pl.pallas_call(
        body, name=name,
        grid_spec=pltpu.PrefetchScalarGridSpec(
            num_scalar_prefetch=1, grid=(R // tr,),
            in_specs=[pl.BlockSpec((tr, C), lambda i, s: (i, 0))],
            out_specs=pl.BlockSpec((None, tr, C), lambda i, s: (s[0], i, 0))),
        out_shape=jax.ShapeDtypeStruct((nslots, R, C), dtype),
        compiler_params=_cparams(("arbitrary",)),
    )(_scalars(slot), src)


def _pair_sum(g, got, c, *, name):
    _, R, C = g.shape
    hr = R // 2
    tr = _slab_rows(hr, C, 3)
    nblk = hr // tr

    def body(s_ref, g_ref, r_ref, o_ref):
        o_ref[...] = (g_ref[...].astype(F32) + r_ref[...].astype(F32)).astype(o_ref.dtype)

    return pl.pallas_call(
        body, name=name,
        grid_spec=pltpu.PrefetchScalarGridSpec(
            num_scalar_prefetch=1, grid=(4, nblk),
            in_specs=[pl.BlockSpec((None, tr, C), lambda j, i, s: (j, s[0] * nblk + i, 0)),
                      pl.BlockSpec((None, tr, C), lambda j, i, s: (j, i, 0))],
            out_specs=pl.BlockSpec((None, tr, C), lambda j, i, s: (j, i, 0))),
        out_shape=jax.ShapeDtypeStruct((4, hr, C), g.dtype),
        compiler_params=_cparams(("arbitrary", "arbitrary")),
    )(_scalars(c), g, got)


def _chip_sum(p, landed, me_chip, c, *, name):
    _, hr, C = p.shape
    tr = _slab_rows(hr, C, 5)

    def body(s_ref, p_ref, l0_ref, l1_ref, l2_ref, o_ref):
        o_ref[...] = ((p_ref[...].astype(F32) + l0_ref[...].astype(F32)) + l1_ref[...].astype(F32)) + l2_ref[...].astype(F32)

    return pl.pallas_call(
        body, name=name,
        grid_spec=pltpu.PrefetchScalarGridSpec(
            num_scalar_prefetch=1, grid=(hr // tr,),
            in_specs=[pl.BlockSpec((None, tr, C), lambda i, s: (s[0], i, 0))]
            + [pl.BlockSpec((None, tr, C), functools.partial(lambda i, s, k: (k, i, 0), k=k)) for k in range(3)],
            out_specs=pl.BlockSpec((None, tr, C), lambda i, s: (s[1], i, 0))),
        out_shape=jax.ShapeDtypeStruct((2, hr, C), F32),
        compiler_params=_cparams(("arbitrary",)),
    )(_scalars(me_chip, c), p, landed, landed, landed)


def _place():
    return lax.axis_index("x"), lax.axis_index("y"), lax.axis_index("c")


def _other_chips(x, y):
    chips = [(1 - x, y), (x, 1 - y), (1 - x, 1 - y)]
    return chips, [2 * cx + cy for cx, cy in chips]


HBM = pl.BlockSpec(memory_space=pl.ANY)


def _allgather8(v, *, name):
    rows, cols = v.shape

    def body(v_ref, out_ref, send_sems, recv_sems):
        x, y, c = _place()
        me = 4 * x + 2 * y + c
        out_ref[me] = v_ref[...]
        copies = []
        for k in range(1, 8):
            bx, by, bc = (k >> 2) & 1, (k >> 1) & 1, k & 1
            px, py, pc = x ^ bx, y ^ by, c ^ bc
            cp = pltpu.make_async_remote_copy(
                src_ref=v_ref, dst_ref=out_ref.at[me], send_sem=send_sems.at[k - 1], recv_sem=recv_sems.at[k - 1],
                device_id=(px, py, pc), device_id_type=MESH)
            cp.start()
            copies.append((cp, 4 * px + 2 * py + pc))
        for k, (cp, peer) in enumerate(copies):
            pltpu.make_async_remote_copy(
                src_ref=v_ref, dst_ref=out_ref.at[peer], send_sem=send_sems.at[k], recv_sem=recv_sems.at[k],
                device_id=(x, y, c), device_id_type=MESH).wait_recv()
        for cp, _ in copies:
            cp.wait_send()

    return pl.pallas_call(
        body, name=name, out_shape=jax.ShapeDtypeStruct((8, rows, cols), v.dtype),
        in_specs=[pl.BlockSpec(memory_space=pltpu.VMEM)], out_specs=pl.BlockSpec(memory_space=pltpu.VMEM),
        scratch_shapes=[pltpu.SemaphoreType.DMA((7,)), pltpu.SemaphoreType.DMA((7,))],
        compiler_params=pltpu.CompilerParams(vmem_limit_bytes=VMEM_LIMIT),
    )(v)


def _allgather_shards(bufs, *, name):
    n = len(bufs)

    def body(*refs):
        outs = refs[n:2 * n]
        send_sems, recv_sems = refs[2 * n:]
        x, y, c = _place()
        me_chip = 2 * x + y
        sibling = (x, y, 1 - c)
        chips, chip_ids = _other_chips(x, y)

        def remote(k, j, blk, hf, to):
            hr = bufs[k].shape[1] // 2
            piece = outs[k].at[blk, pl.ds(pl.multiple_of(hf * hr, 16), hr), :]
            return pltpu.make_async_remote_copy(
                src_ref=piece, dst_ref=piece, send_sem=send_sems.at[6 * k + j], recv_sem=recv_sems.at[6 * k + j],
                device_id=to, device_id_type=MESH)

        sends = []
        for k in range(n):
            for j, chip in enumerate(chips):
                cp = remote(k, j, me_chip, c, (*chip, c))
                cp.start()
                sends.append(cp)
        for k in range(n):
            for j, chip in enumerate(chips):
                remote(k, j, chip_ids[j], c, (x, y, c)).wait_recv()
                cp = remote(k, 3 + j, chip_ids[j], c, sibling)
                cp.start()
                sends.append(cp)
        for k in range(n):
            for j in range(3):
                remote(k, 3 + j, chip_ids[j], 1 - c, (x, y, c)).wait_recv()
        for cp in sends:
            cp.wait_send()

    return list(pl.pallas_call(
        body, name=name, out_shape=[jax.ShapeDtypeStruct(b.shape, b.dtype) for b in bufs],
        in_specs=[HBM] * n, out_specs=[HBM] * n, input_output_aliases={k: k for k in range(n)},
        scratch_shapes=[pltpu.SemaphoreType.DMA((6 * n,)), pltpu.SemaphoreType.DMA((6 * n,))],
    )(*bufs))


HBM_SPEC = pl.BlockSpec(memory_space=pltpu.HBM)
SEM_SPEC = pl.BlockSpec(memory_space=pltpu.SEMAPHORE)
EFFECT = pltpu.SideEffectType.DATAFLOW_SIDE_EFFECTING
TOKEN = jax.ShapeDtypeStruct((SUBLANES, LANES), F32)


def _in_hbm(a):
    return pltpu.with_memory_space_constraint(a, pltpu.HBM)


def _half_rows(buf, hf):
    hr = buf.shape[1] // 2
    return pl.ds(pl.multiple_of(hf * hr, 16), hr)


def _plan_ag_ici(refs):
    x, y, c = _place()
    chips, ids = _other_chips(x, y)
    out = []
    for r in refs:
        mine = r.at[2 * x + y, _half_rows(r, c), :]
        out += [(mine, mine, r.at[ids[j], _half_rows(r, c), :], (*chip, c)) for j, chip in enumerate(chips)]
    return out


def _plan_ag_pair(refs):
    x, y, c = _place()
    _, ids = _other_chips(x, y)
    out = []
    for r in refs:
        for j in range(3):
            piece = r.at[ids[j], _half_rows(r, c), :]
            out.append((piece, piece, r.at[ids[j], _half_rows(r, 1 - c), :], (x, y, 1 - c)))
    return out


def _plan_rs_ici(refs):
    x, y, c = _place()
    chips, ids = _other_chips(x, y)
    n = len(refs) // 2
    return [(refs[k].at[ids[j]], refs[n + k].at[j], refs[n + k].at[j], (*chip, c))
            for k in range(n) for j, chip in enumerate(chips)]


def _plan_pair_exchange(refs):
    x, y, c = _place()
    n = len(refs) // 2
    return [(refs[k].at[:, _half_rows(refs[k], 1 - c), :], refs[n + k], refs[n + k], (x, y, 1 - c)) for k in range(n)]


def _plan_pair_gather(refs):
    x, y, c = _place()
    return [(r.at[c], r.at[c], r.at[1 - c], (x, y, 1 - c)) for r in refs]


def _remote(src, dst, send_sem, recv_sem, target):
    return pltpu.make_async_remote_copy(src_ref=src, dst_ref=dst, send_sem=send_sem, recv_sem=recv_sem,
                                        device_id=target, device_id_type=MESH)


def _copy_start(groups, *, name, after=()):
    flat = [a for arrays, _, _ in groups for a in arrays]
    n, ng = len(flat), len(groups)
    after = list(after)
    n_in = n + len(after)

    def body(*refs):
        sems = refs[n_in:n_in + 2 * ng]
        thru = refs[n_in + 2 * ng:n_in + 2 * ng + n]
        token = refs[-1]
        pos = 0
        for g, (arrays, plan, n_copies) in enumerate(groups):
            copies = plan(thru[pos:pos + len(arrays)])
            pos += len(arrays)
            assert len(copies) == n_copies
            for i, (src, dst, _, target) in enumerate(copies):
                _remote(src, dst, sems[2 * g].at[i], sems[2 * g + 1].at[i], target).start()
        token[...] = jnp.zeros_like(token)

    out_shape = tuple(pltpu.SemaphoreType.DMA((n_copies,)) for _, _, n_copies in groups for _ in range(2))
    out_shape += tuple(pltpu.HBM(a.shape, a.dtype) for a in flat) + (TOKEN,)
    res = pl.pallas_call(
        body, name=name, out_shape=out_shape,
        in_specs=(HBM_SPEC,) * n + (pl.BlockSpec(memory_space=pl.ANY),) * len(after),
        out_specs=(SEM_SPEC,) * (2 * ng) + (HBM_SPEC,) * n + (pl.BlockSpec(memory_space=pltpu.VMEM),),
        input_output_aliases={k: 2 * ng + k for k in range(n)},
        compiler_params=pltpu.CompilerParams(has_side_effects=EFFECT),
    )(*[_in_hbm(a) for a in flat], *after)
    sems = [(res[2 * g], res[2 * g + 1]) for g in range(ng)]
    thru, pos = [], 2 * ng
    for arrays, _, _ in groups:
        thru.append(list(res[pos:pos + len(arrays)]))
        pos += len(arrays)
    return sems, thru, res[-1]


def _copy_wait(arrays, sems, plan, n_copies, after, *, name):
    n = len(arrays)
    after = list(after)

    def body(*refs):
        send, recv = refs[n], refs[n + 1]
        x, y, c = _place()
        copies = plan(refs[:n])
        assert len(copies) == n_copies
        for i, (src, dst, landing, target) in enumerate(copies):
            _remote(src, dst, send.at[i], recv.at[i], target).wait_send()
            _remote(landing, landing, send.at[i], recv.at[i], (x, y, c)).wait_recv()

    return list(pl.pallas_call(
        body, name=name, out_shape=tuple(pltpu.HBM(a.shape, a.dtype) for a in arrays),
        in_specs=(HBM_SPEC,) * n + (SEM_SPEC, SEM_SPEC) + (pl.BlockSpec(memory_space=pl.ANY),) * len(after),
        out_specs=(HBM_SPEC,) * n, input_output_aliases={k: k for k in range(n)},
        compiler_params=pltpu.CompilerParams(has_side_effects=EFFECT),
    )(*arrays, *sems, *after))


def _rs_stage1(gs, tag, after=()):
    n = len(gs)
    lands = [lax.empty((4, g.shape[1] // 2, g.shape[2]), g.dtype) for g in gs]
    sems, (arrays,), token = _copy_start([(list(gs) + lands, _plan_pair_exchange, n)], name=f"rs_pair_start_{tag}",
                                         after=after)
    return (sems[0], arrays), token


def _rs_stage2(handle, after, tag):
    sems, arrays = handle
    n = len(arrays) // 2
    arrays = _copy_wait(arrays, sems, _plan_pair_exchange, n, after, name=f"rs_pair_wait_{tag}")
    c = lax.axis_index("c")
    pair = [_pair_sum(g, r, c, name=f"rs_pair_sum_{tag}{k}") for k, (g, r) in enumerate(zip(arrays[:n], arrays[n:]))]
    lands = [lax.empty((3,) + p.shape[1:], p.dtype) for p in pair]
    sems, (arrays,), token = _copy_start([(pair + lands, _plan_rs_ici, 3 * n)], name=f"rs_start_{tag}")
    return (sems[0], arrays), token


def _rs_stage3(handle, after, tag):
    sems, arrays = handle
    n = len(arrays) // 2
    arrays = _copy_wait(arrays, sems, _plan_rs_ici, 3 * n, after, name=f"rs_wait_{tag}")
    x, y, c = _place()
    halves = [_chip_sum(p, l, 2 * x + y, c, name=f"rs_chip_sum_{tag}{k}") for k, (p, l) in enumerate(zip(arrays[:n], arrays[n:]))]
    sems, (halves,), token = _copy_start([(halves, _plan_pair_gather, n)], name=f"rs_gather_start_{tag}")
    return (sems[0], halves), token


def _rs_stage4(handle, after, tag):
    sems, halves = handle
    full = _copy_wait(halves, sems, _plan_pair_gather, len(halves), after, name=f"rs_gather_wait_{tag}")
    return [f.reshape(2 * f.shape[1], f.shape[2]) for f in full]


def _to_segments(a):
    rows = a.shape[0]
    return a.reshape(N_SEG, rows // N_SEG, -1).transpose(1, 0, 2).reshape(rows, -1)


def _from_segments(a):
    rows = a.shape[0]
    return a.reshape(rows // N_SEG, N_SEG, -1).transpose(1, 0, 2).reshape(rows, -1)


def _rope_tables(L):
    t = jnp.arange(L, dtype=jnp.int32)
    row = (t // GRID_W).astype(F32)
    col = (t % GRID_W).astype(F32)
    n_freq = QK_ROPE // 4
    inv = ROPE_BASE ** (-jnp.arange(n_freq, dtype=F32) / n_freq)
    a0, a1 = row[:, None] * inv, col[:, None] * inv
    z = jnp.zeros((L, LANES - QK_ROPE), F32)
    cos = jnp.concatenate([jnp.cos(a0), jnp.cos(a0), jnp.cos(a1), jnp.cos(a1), z], axis=1)
    sin = jnp.concatenate([-jnp.sin(a0), jnp.sin(a0), -jnp.sin(a1), jnp.sin(a1), z], axis=1)
    return _to_segments(cos), _to_segments(sin)


def _col_blocks(w, nblk):
    r, c = w.shape
    return w.reshape(r, nblk, c // nblk).transpose(1, 0, 2)


def _from_col_blocks(w4):
    nblk, r, c = w4.shape
    return w4.transpose(1, 0, 2).reshape(r, nblk * c)


def _s5_discretize(a_re, a_im, log_dt, b_re, b_im):
    dt = jnp.exp(log_dt)[:, None]
    mag = jnp.exp(a_re * dt)
    ab_re, ab_im = mag * jnp.cos(a_im * dt), mag * jnp.sin(a_im * dt)
    den = a_re * a_re + a_im * a_im
    nr, ni = ab_re - 1.0, ab_im
    co_re = (nr * a_re + ni * a_im) / den
    co_im = (ni * a_re - nr * a_im) / den
    bb_re = co_re[..., None] * b_re - co_im[..., None] * b_im
    bb_im = co_re[..., None] * b_im + co_im[..., None] * b_re
    return ab_re, ab_im, bb_re, bb_im


def _diag_blocks_in(bb, gpb):
    G, N, P = bb.shape
    t = jnp.tile(jnp.swapaxes(bb, 1, 2).reshape(G // gpb, gpb * P, N), (1, 1, gpb))
    row = lax.broadcasted_iota(jnp.int32, t.shape, 1) // P
    col = lax.broadcasted_iota(jnp.int32, t.shape, 2) // N
    return jnp.where(row == col, t, 0.0)


def _diag_blocks_out(cc, gpb):
    G, P, N = cc.shape
    t = jnp.tile(jnp.swapaxes(cc, 1, 2).reshape(G // gpb, gpb * N, P), (1, 1, gpb))
    row = lax.broadcasted_iota(jnp.int32, t.shape, 1) // N
    col = lax.broadcasted_iota(jnp.int32, t.shape, 2) // P
    return jnp.where(row == col, t, 0.0)


def _tr(ws):
    return [jnp.swapaxes(w, 1, 2) for w in ws]


WEIGHTS = ['c_ctx', 'w_mod', 'b_mod', 'norm1', 'norm2', 'w_in', 's5_a_re', 's5_a_im', 's5_log_dt', 's5_b_re', 's5_b_im',
           's5_c_re', 's5_c_im', 's5_d', 'w_glu', 'q_norm', 'kv_norm', 'w_uq', 'w_ukv', 'w_mla_o', 'w_out', 'w_ffn_in',
           'w_ffn_out', 'norm_f']
AG_GROUPS = [['w_in'], ['w_glu', 'w_uq', 'w_ukv', 'w_mla_o', 'w_out'], ['w_ffn_in', 'w_ffn_out']]
SMALL = ['norm1', 'norm2', 's5_a_re', 's5_a_im', 's5_log_dt', 's5_b_re', 's5_b_im', 's5_c_re', 's5_c_im', 's5_d',
         'q_norm', 'kv_norm', 'norm_f']


def _pad_rows(a, rows):
    return jnp.concatenate([a, jnp.zeros((rows - a.shape[0],) + a.shape[1:], a.dtype)], axis=0)


def _pack(vals, width, rows):
    flat = jnp.concatenate([v.reshape(-1).astype(F32) for v in vals])
    flat = jnp.concatenate([flat, jnp.zeros((rows * width - flat.shape[0],), F32)])
    return flat.reshape(rows, width)


def _unpack(buf, like):
    flat = buf.reshape(-1)
    out, pos = [], 0
    for v in like:
        out.append(flat[pos:pos + v.size].reshape(v.shape))
        pos += v.size
    return out


def _step(x, c, ctx, loss_target, w, m, v):
    px, py, pc = _place()
    me = 4 * px + 2 * py + pc
    me_chip = 2 * px + py
    L, D = x.shape[1], x.shape[2]
    Lc = ctx.shape[1]
    T = L + Lc
    SW = D // 2
    G = SW // S5_GROUP
    C = G * S5_STATE
    H = MLA_HEADS
    q_rank = w['q_norm'].shape[1]
    kv_rank = w['kv_norm'].shape[1]
    d_ff = w['w_ffn_out'].shape[1] * 4
    wa_used = SW + q_rank + kv_rank + QK_ROPE
    WA = -(-(SW + q_rank + kv_rank + LANES) // 512) * 512

    c_all = _allgather8(c.astype(F32).reshape(SUBLANES, D // SUBLANES), name="ag_cond").reshape(8, D)
    cond = jnp.concatenate([c_all, w['c_ctx'].reshape(1, D)], axis=0)
    cond = _pad_rows(cond, 16)
    (act,) = _rw(lambda t: (jax.nn.silu(t),), [cond], [], [F32], name="cond_silu")
    w_mod, cs_mod = w['w_mod'][0], w['w_mod'].shape[2]
    mod_part = _mm(act, w_mod, out_dtype=F32, name="mod_fwd")
    mod_all = _allgather8(mod_part, name="ag_mod")
    mod_full = jnp.concatenate([mod_all[0], mod_all[2], mod_all[4], mod_all[6]], axis=1) + w['b_mod']
    m_lat = lax.dynamic_slice_in_dim(mod_full, me, 1, axis=0).reshape(6, D)
    m_ctx = mod_full[8].reshape(6, D)
    sh1, sc1, g1, sh2, sc2, g2 = (m_lat[i:i + 1] for i in range(6))
    csh1, csc1 = m_ctx[0:1], m_ctx[1:2]

    ag_groups = [([_into_slot(w[nme][0], me_chip, 4, BF16, name=f"cast_{nme}") for nme in grp], _plan_ag_ici, 3 * len(grp))
                 for grp in AG_GROUPS]
    ag_sems, ag_bufs, ag_token = _copy_start(ag_groups, name="ag_start", after=[mod_full])
    gathered, ag_pair = {}, {}

    def landed(g, after):
        n_cp = 3 * len(AG_GROUPS[g])
        got = _copy_wait(ag_bufs[g], ag_sems[g], _plan_ag_ici, n_cp, after, name=f"ag_wait_{g}")
        sems, (got,), token = _copy_start([(got, _plan_ag_pair, n_cp)], name=f"ag_pair_start_{g}")
        ag_pair[g] = (sems[0], got)
        return token[0, 0]

    def arrive(g, after):
        sems, got = ag_pair[g]
        got = _copy_wait(got, sems, _plan_ag_pair, 3 * len(AG_GROUPS[g]), after, name=f"ag_pair_wait_{g}")
        gathered.update(zip(AG_GROUPS[g], got))

    xs = _to_segments(x[0])
    cs = _to_segments(ctx[0])
    tgt = _to_segments(loss_target[0])
    cos, sin = _rope_tables(L)
    n1, n2, nf = w['norm1'], w['norm2'], w['norm_f'].reshape(1, D)
    qg, kvg = w['q_norm'], w['kv_norm']

    (xn_lat,) = _rw(_f_norm_mod, [xs], [n1 + ag_token[0, 0], sc1, sh1], [BF16], name="norm1_lat")
    (xn_ctx,) = _rw(_f_norm_mod, [cs], [n1, csc1, csh1], [BF16], name="norm1_ctx")
    xn = jnp.concatenate([xn_lat, xn_ctx], axis=0)
    landed(0, [xn])

    gpb = min(S5_BLOCK_GROUPS, G)
    gpo = min(8, G)
    d_skip = w['s5_d'][0].reshape(1, SW)
    disc, vjp_disc, w_b, w_c = [], [], [], []
    for d in range(2):
        prm = (w['s5_a_re'][0, d], w['s5_a_im'][0, d], w['s5_log_dt'][0, d], w['s5_b_re'][0, d], w['s5_b_im'][0, d])

        def prep(a_re, a_im, log_dt, b_re, b_im):
            ab_re, ab_im, bb_re, bb_im = _s5_discretize(a_re, a_im, log_dt, b_re, b_im)
            return ab_re.reshape(1, C), ab_im.reshape(1, C), _diag_blocks_in(bb_re, gpb), _diag_blocks_in(bb_im, gpb)

        out, vj = jax.vjp(prep, *prm)
        disc.append(out)
        vjp_disc.append(vj)
        w_b += [out[2], out[3]]
        w_c += [_diag_blocks_out(w['s5_c_re'][0, d], gpo), -_diag_blocks_out(w['s5_c_im'][0, d], gpo)]
    nb_in = G // gpb
    nb_out = G // gpo

    arrive(0, [xn, tgt] + w_b + w_c)
    w_in = _from_col_blocks(gathered['w_in'])
    w_a = jnp.concatenate([w_in[:, :wa_used], jnp.zeros((D, WA - wa_used), BF16)], axis=1)
    w_g = w_in[:, wa_used:]
    ha = _mm(xn, w_a, out_dtype=F32, name="in_proj")
    ha_lat, ha_ctx = ha[:L], ha[L:]
    gt = _mm(xn_lat, w_g, out_dtype=F32, name="in_gates")
    f_post_lat = _make_f_post_in(SW, q_rank, kv_rank, True)
    f_post_ctx = _make_f_post_in(SW, q_rank, kv_rank, False)
    u_lat, cqn, ckvn_lat, kr_lat = _rw(f_post_lat, [ha_lat, cos, sin], [qg, kvg], [F32, BF16, BF16, BF16], name="post_in_lat")
    u_ctx, ckvn_ctx, kr_ctx = _rw(f_post_ctx, [ha_ctx], [kvg], [F32, BF16, BF16], name="post_in_ctx")
    zero = jnp.zeros((1, C), F32) + landed(1, [u_lat, u_ctx])

    h_lat, h_ctx, hT_ctx, r5 = [], [], [], []
    for d, rev in enumerate((False, True)):
        lr, li = disc[d][0], disc[d][1]
        hcr, hci, tr, ti = _s5_scan(u_ctx, w_b[2 * d], w_b[2 * d + 1], lr, li, zero, zero, zero, zero, reverse=rev,
                                    name=f"s5_scan_ctx_{d}")
        hlr, hli, _, _, y = _s5_scan(u_lat, w_b[2 * d], w_b[2 * d + 1], lr, li, tr, ti, zero, zero, reverse=rev,
                                     name=f"s5_scan_lat_{d}", readout=(w_c[2 * d], w_c[2 * d + 1]))
        h_ctx += [hcr, hci]
        h_lat += [hlr, hli]
        hT_ctx += [tr, ti]
        r5.append(y)
    (z,) = _rw(_f_s5post, [u_lat] + r5, [d_skip], [BF16], name="s5_post")

    arrive(1, [z])
    w_glu, w_ukv, w_mla_o = (gathered[nme] for nme in ('w_glu', 'w_ukv', 'w_mla_o'))
    w_out = gathered['w_out'].reshape(D, D)
    uq3 = _from_col_blocks(gathered['w_uq']).reshape(q_rank, H, QK_NOPE + QK_ROPE)
    w_q2 = jnp.concatenate([uq3, jnp.zeros((q_rank, H, LANES - QK_ROPE), BF16)], axis=2).reshape(q_rank, H * 2 * LANES)
    q2 = _mm(cqn, w_q2, out_dtype=F32, name="q_up")
    (qq,) = _rw(_f_qpost, [q2, cos, sin], [], [BF16], name="q_rope")
    kvn = jnp.concatenate([ckvn_lat, ckvn_ctx], axis=0)
    kr_all = jnp.concatenate([kr_lat, kr_ctx], axis=0)
    kv = _mm(kvn, w_ukv, b_shards=4, out_dtype=BF16, name="kv_up")
    kr_all = kr_all + landed(2, [kv, qq]).astype(BF16)
    o = _attn_fwd(qq, kv, kr_all, name="attn_fwd")

    ab = _mm(z, w_glu, b_shards=4, out_dtype=F32, name="glu_proj")
    bm = _mm(o, w_mla_o, b_shards=4, out_dtype=F32, name="mla_out")
    (mix,) = _rw(_f_merge, [ab, bm, gt], [], [BF16], name="merge")
    out1 = _mm(mix, w_out, out_dtype=F32, name="out_proj")
    x1, xn2 = _rw(_f_resid_norm, [xs, out1], [g1, n2, sc2, sh2], [F32, BF16], name="resid_norm2")
    arrive(2, [xn2])
    w_ffn_in = gathered['w_ffn_in']
    w_ffn_out = gathered['w_ffn_out'].reshape(d_ff, D)
    hmid, ab2 = _ffn_in_swiglu(xn2, w_ffn_in, name="ffn_in")
    f2 = _mm(hmid, w_ffn_out, out_dtype=F32, name="ffn_out")
    (row_loss,) = _rw(_f_final, [x1, f2, tgt], [g2, nf], [F32], name="final_loss")
    loss = lax.psum(jnp.sum(row_loss), ("x", "y", "c"))

    ones = jnp.ones((L, 1), F32)
    (dx1_a, df2), (dg2, dnf) = _rw_vjp(_f_final, [x1, f2, tgt], [g2, nf], [[ones]], [True, True, False], [True, True],
                                       [F32, BF16], name="final_loss_bwd")
    gw_ffn_out = _mm(hmid, df2, ta=True, out_dtype=BF16, name="ffn_out_dw")
    dab2 = _ffn_out_dx_swiglu(df2, w_ffn_out, ab2, name="ffn_out_dx")
    dxn2 = _mm(dab2, w_ffn_in, tb=True, a_shards=2, b_shards=4, out_dtype=F32, name="ffn_in_dx")
    gw_ffn_in = _mm(xn2, dab2, ta=True, b_shards=2, out_shards=4, out_dtype=BF16, name="ffn_in_dw")
    rs_ffn, tok = _rs_stage1([gw_ffn_out.reshape(4, -1, D), gw_ffn_in], "ffn")
    (dx_a, dout1), (dg1, dn2, dsc2, dsh2) = _rw_vjp(
        _f_resid_norm, [xs, out1], [g1, n2 + tok[0, 0], sc2, sh2], [[dx1_a], [dxn2]], [True, True], [True] * 4, [F32, BF16],
        name="resid_norm2_bwd")
    dmix = _mm(dout1, w_out, tb=True, out_dtype=F32, name="out_proj_dx")
    rs_ffn, tok = _rs_stage2(rs_ffn, [dmix], "ffn")
    gw_out = _mm(mix, dout1, ta=True, out_dtype=BF16, name="out_proj_dw")
    (dab, dbm, dgt), _ = _rw_vjp(_f_merge, [ab, bm, gt], [], [[dmix]], [True] * 3, [], [BF16] * 3, name="merge_bwd",
                                 anchor=tok)
    dz = _mm(dab, w_glu, tb=True, b_shards=4, out_dtype=F32, name="glu_proj_dx")
    gw_glu = _mm(z, dab, ta=True, out_shards=4, out_dtype=BF16, name="glu_proj_dw")
    do = _mm(dbm, w_mla_o, tb=True, b_shards=4, out_dtype=BF16, name="mla_out_dx")
    gw_mla_o = _mm(o, dbm, ta=True, out_shards=4, out_dtype=BF16, name="mla_out_dw")
    dxn_g = _mm(dgt, w_g, tb=True, out_dtype=F32, name="in_gates_dx")
    gw_g = _mm(xn_lat, dgt, ta=True, out_dtype=BF16, name="in_gates_dw")
    rs_mid, tok = _rs_stage1([gw_out.reshape(4, -1, D), gw_glu, gw_mla_o], "mid", after=[gw_g])

    (du_a, dr5), (dd_skip,) = _rw_vjp(_f_s5post, [u_lat] + r5, [d_skip + tok[0, 0]], [[dz]], [True, True, False], [True],
                                      [F32, F32], name="s5_post_bwd")
    dw_c = _bd_dw(h_lat, [dr5] * 4, nb_out, name="s5_readout_dw")
    rs_mid, tok = _rs_stage2(rs_mid, dw_c[:1], "mid")
    zero = zero + tok[0, 0]
    w_ct = _tr(w_c)
    zeros_ctx = jnp.zeros((Lc, SW), BF16)
    mu_lat, mu_ctx, dlam = [], [], []
    for d, rev in enumerate((False, True)):
        lr, li = disc[d][0], disc[d][1]
        mlr, mli, fr, fi = _s5_scan(dr5, w_ct[2 * d], w_ct[2 * d + 1], lr, -li, zero, zero, zero, zero, reverse=not rev,
                                    name=f"s5_adj_lat_{d}")
        dh0r, dh0i = _cmul(lr, -li, fr, fi)
        mcr, mci, _, _ = _s5_scan(zeros_ctx, w_ct[2 * d], w_ct[2 * d + 1], lr, -li, zero, zero, dh0r, dh0i,
                                  reverse=not rev, name=f"s5_adj_ctx_{d}")
        dl_lat = _s5_dlam(mlr, mli, h_lat[2 * d], h_lat[2 * d + 1], hT_ctx[2 * d], hT_ctx[2 * d + 1], reverse=rev,
                          name=f"s5_dlam_lat_{d}")
        dl_ctx = _s5_dlam(mcr, mci, h_ctx[2 * d], h_ctx[2 * d + 1], zero, zero, reverse=rev, name=f"s5_dlam_ctx_{d}")
        mu_lat += [mlr, mli]
        mu_ctx += [mcr, mci]
        dlam.append((dl_lat[0] + dl_ctx[0], dl_lat[1] + dl_ctx[1]))
    du_b = _bd_fanin(mu_lat, _tr(w_b), name="s5_bu_lat_dx")
    du_ctx = _bd_fanin(mu_ctx, _tr(w_b), name="s5_bu_ctx_dx")
    dw_b_lat = _bd_dw([u_lat] * 4, mu_lat, nb_in, name="s5_bu_lat_dw")
    dw_b_ctx = _bd_dw([u_ctx] * 4, mu_ctx, nb_in, name="s5_bu_ctx_dw")
    g_s5 = {}
    for d in range(2):
        ct = (dlam[d][0], dlam[d][1], dw_b_lat[2 * d] + dw_b_ctx[2 * d], dw_b_lat[2 * d + 1] + dw_b_ctx[2 * d + 1])
        ga_re, ga_im, gdt, gb_re, gb_im = vjp_disc[d](ct)
        _, vj_c = jax.vjp(lambda cr, ci: (_diag_blocks_out(cr, gpo), -_diag_blocks_out(ci, gpo)),
                          w['s5_c_re'][0, d], w['s5_c_im'][0, d])
        gc_re, gc_im = vj_c((dw_c[2 * d], dw_c[2 * d + 1]))
        for nme, val in (('s5_a_re', ga_re), ('s5_a_im', ga_im), ('s5_log_dt', gdt), ('s5_b_re', gb_re),
                         ('s5_b_im', gb_im), ('s5_c_re', gc_re), ('s5_c_im', gc_im)):
            g_s5.setdefault(nme, []).append(val)
    g_small = {nme: jnp.stack(vals)[None] for nme, vals in g_s5.items()}
    g_small['s5_d'] = dd_skip.reshape(w['s5_d'].shape)

    dqq, dkv, dkr = _attn_bwd(qq, kv, kr_all, do, name="attn_bwd")
    (dq2,), _ = _rw_vjp(_f_qpost, [q2, cos, sin], [], [[dqq]], [True, False, False], [], [BF16], name="q_rope_bwd")
    dcqn = _mm(dq2, w_q2, tb=True, out_dtype=F32, name="q_up_dx")
    gw_q2 = _mm(cqn, dq2, ta=True, out_dtype=BF16, name="q_up_dw")
    dckvn = _mm(dkv, w_ukv, tb=True, b_shards=4, out_dtype=F32, name="kv_up_dx")
    gw_ukv = _mm(kvn, dkv, ta=True, out_shards=4, out_dtype=BF16, name="kv_up_dw")
    gw_uq = gw_q2.reshape(q_rank, H, 2 * LANES)[:, :, :QK_NOPE + QK_ROPE].reshape(q_rank, H * (QK_NOPE + QK_ROPE))
    rs_kv, tok = _rs_stage1([_col_blocks(gw_uq, 4), gw_ukv], "kv")

    (dha_lat,), (dqg, dkvg_lat) = _rw_vjp(
        f_post_lat, [ha_lat, cos, sin], [qg, kvg + tok[0, 0]], [[du_a, du_b], [dcqn], [dckvn[:L]], [dkr[:L]]],
        [True, False, False], [True, True], [BF16], name="post_in_lat_bwd")
    (dha_ctx,), (dkvg_ctx,) = _rw_vjp(f_post_ctx, [ha_ctx], [kvg], [[du_ctx], [dckvn[L:]], [dkr[L:]]], [True], [True],
                                      [BF16], name="post_in_ctx_bwd")
    dha = jnp.concatenate([dha_lat, dha_ctx], axis=0)
    dxn = _mm(dha, w_a, tb=True, out_dtype=F32, name="in_proj_dx")
    gw_a = _mm(xn, dha, ta=True, out_dtype=BF16, name="in_proj_dw")
    rs_kv, tok = _rs_stage2(rs_kv, [gw_a], "kv")
    (dx_seg,), (dn1_lat, dsc1, dsh1) = _rw_vjp(
        _f_norm_mod_keep, [xs], [n1 + tok[0, 0], sc1, sh1], [[dxn[:L], dxn_g], [dx_a]], [True], [True] * 3, [F32],
        name="norm1_lat_bwd")
    _, (dn1_ctx, dcsc1, dcsh1) = _rw_vjp(_f_norm_mod, [cs], [n1, csc1, csh1], [[dxn[L:]]], [False], [True] * 3, [],
                                         name="norm1_ctx_bwd")
    grad_x = _from_segments(dx_seg)[None]
    g_small.update(norm1=dn1_lat + dn1_ctx, norm2=dn2, q_norm=dqg, kv_norm=dkvg_lat + dkvg_ctx, norm_f=dnf.reshape(D))
    gw_in = jnp.concatenate([gw_a[:, :wa_used], gw_g], axis=1)
    small_vals = [g_small[nme] for nme in SMALL]
    n_small = sum(val.size for val in small_vals)
    small_rows = -(-n_small // (LANES * 4 * 32)) * 32

    zD = jnp.zeros((1, D), F32)
    dm = jnp.concatenate([
        jnp.concatenate([dsh1, dsc1, dg1, dsh2, dsc2, dg2], axis=1),
        jnp.concatenate([dcsh1, dcsc1, zD, zD, zD, zD], axis=1),
    ], axis=0)
    dm_all = _allgather8(dm.reshape(SUBLANES, -1), name="ag_dmod").reshape(8, 2, 6 * D)
    rs_in, tok = _rs_stage1([_col_blocks(gw_in, 4), _pack(small_vals, LANES, 4 * small_rows).reshape(4, small_rows, LANES)],
                            "in", after=[dm_all])
    dm_ctx = dm_all[0, 1] + tok[0, 0]
    for k in range(1, 8):
        dm_ctx = dm_ctx + dm_all[k, 1]
    dmod = _pad_rows(jnp.concatenate([dm_all[:, 0, :], dm_ctx[None]], axis=0), 16)
    g_b_mod = jnp.sum(dmod, axis=0, keepdims=True)
    dmod_mine = lax.dynamic_slice_in_dim(dmod, me_chip * cs_mod, cs_mod, axis=1)
    g_w_mod = _mm(act, dmod_mine, ta=True, out_dtype=F32, name="mod_dw")
    dact_part = _mm(dmod_mine, w_mod, tb=True, out_dtype=F32, name="mod_dx")
    dact_all = _allgather8(dact_part[8].reshape(SUBLANES, D // SUBLANES), name="ag_dact").reshape(8, D)
    dact = jnp.zeros((16, D), F32).at[8].set(dact_all[0] + dact_all[2] + dact_all[4] + dact_all[6])
    (dcond_rows,), _ = _rw_vjp(lambda t: (jax.nn.silu(t),), [cond], [], [[dact]], [True], [], [F32], name="cond_silu_bwd")
    g_c_ctx = dcond_rows[8]

    rs_in, tok = _rs_stage2(rs_in, [g_c_ctx], "in")

    grads, delta, new_m, new_v = {}, {}, {}, {}

    def update(members, reds, anchor):
        deltas = []
        for nme, red in zip(members, reds):
            res = _adamw(w[nme][0], red, m[nme][0], v[nme][0], name=f"adamw_{nme}", anchor=anchor)
            grads[nme], delta[nme], new_m[nme], new_v[nme] = (r.reshape(w[nme].shape) for r in res)
            deltas.append(res[1])
            anchor = None
        return deltas

    rs_ffn, tok = _rs_stage3(rs_ffn, [tok], "ffn")
    done = update(['w_mod'], [g_w_mod], tok)
    red_ffn = _rs_stage4(rs_ffn, done, "ffn")
    rs_mid, tok = _rs_stage3(rs_mid, red_ffn[:1], "mid")
    done = update(['w_ffn_out', 'w_ffn_in'], red_ffn, tok)
    red_mid = _rs_stage4(rs_mid, done, "mid")
    rs_kv, tok = _rs_stage3(rs_kv, red_mid[:1], "kv")
    done = update(['w_out', 'w_glu', 'w_mla_o'], red_mid, tok)
    red_kv = _rs_stage4(rs_kv, done, "kv")
    rs_in, tok = _rs_stage3(rs_in, red_kv[:1], "in")
    done = update(['w_uq', 'w_ukv'], red_kv, tok)
    red_in = _rs_stage4(rs_in, done, "in")
    update(['w_in'], red_in[:1], None)
    small_mine = red_in[-1]
    small_buf = _into_slot(small_mine, me_chip, 4, F32, name="small_grads_slot")
    small_all = _allgather_shards([small_buf], name="ag_small_grads")[0].reshape(4 * small_rows, LANES)
    g_small_red = dict(zip(SMALL, _unpack(small_all, [w[nme] for nme in SMALL])))
    rest = SMALL + ['c_ctx', 'b_mod']
    g_rest = dict(g_small_red, c_ctx=g_c_ctx, b_mod=g_b_mod)
    rows_rest = -(-sum(w[nme].size for nme in rest) // (LANES * 16)) * 16
    packed = [_pack([src[nme] for nme in rest], LANES, rows_rest) for src in (w, g_rest, m, v)]
    res = _adamw(*packed, name="adamw_small")
    for dst, buf in zip((grads, delta, new_m, new_v), res):
        dst.update(zip(rest, _unpack(buf, [w[nme] for nme in rest])))
    return (loss, grad_x, *[grads[nme] for nme in WEIGHTS], *[delta[nme] for nme in WEIGHTS],
            *[new_m[nme] for nme in WEIGHTS], *[new_v[nme] for nme in WEIGHTS])


def kernel(x, c, ctx, c_ctx, w_mod, b_mod, norm1, norm2, w_in, s5_a_re, s5_a_im, s5_log_dt, s5_b_re, s5_b_im, s5_c_re, s5_c_im, s5_d, w_glu, q_norm, kv_norm, w_uq, w_ukv, w_mla_o, w_out, w_ffn_in, w_ffn_out, norm_f, loss_target, m_c_ctx, m_w_mod, m_b_mod, m_norm1, m_norm2, m_w_in, m_s5_a_re, m_s5_a_im, m_s5_log_dt, m_s5_b_re, m_s5_b_im, m_s5_c_re, m_s5_c_im, m_s5_d, m_w_glu, m_q_norm, m_kv_norm, m_w_uq, m_w_ukv, m_w_mla_o, m_w_out, m_w_ffn_in, m_w_ffn_out, m_norm_f, v_c_ctx, v_w_mod, v_b_mod, v_norm1, v_norm2, v_w_in, v_s5_a_re, v_s5_a_im, v_s5_log_dt, v_s5_b_re, v_s5_b_im, v_s5_c_re, v_s5_c_im, v_s5_d, v_w_glu, v_q_norm, v_kv_norm, v_w_uq, v_w_ukv, v_w_mla_o, v_w_out, v_w_ffn_in, v_w_ffn_out, v_norm_f):
    w = dict(c_ctx=c_ctx, w_mod=w_mod, b_mod=b_mod, norm1=norm1, norm2=norm2, w_in=w_in, s5_a_re=s5_a_re, s5_a_im=s5_a_im,
             s5_log_dt=s5_log_dt, s5_b_re=s5_b_re, s5_b_im=s5_b_im, s5_c_re=s5_c_re, s5_c_im=s5_c_im, s5_d=s5_d, w_glu=w_glu,
             q_norm=q_norm, kv_norm=kv_norm, w_uq=w_uq, w_ukv=w_ukv, w_mla_o=w_mla_o, w_out=w_out, w_ffn_in=w_ffn_in,
             w_ffn_out=w_ffn_out, norm_f=norm_f)
    m = dict(c_ctx=m_c_ctx, w_mod=m_w_mod, b_mod=m_b_mod, norm1=m_norm1, norm2=m_norm2, w_in=m_w_in, s5_a_re=m_s5_a_re,
             s5_a_im=m_s5_a_im, s5_log_dt=m_s5_log_dt, s5_b_re=m_s5_b_re, s5_b_im=m_s5_b_im, s5_c_re=m_s5_c_re,
             s5_c_im=m_s5_c_im, s5_d=m_s5_d, w_glu=m_w_glu, q_norm=m_q_norm, kv_norm=m_kv_norm, w_uq=m_w_uq, w_ukv=m_w_ukv,
             w_mla_o=m_w_mla_o, w_out=m_w_out, w_ffn_in=m_w_ffn_in, w_ffn_out=m_w_ffn_out, norm_f=m_norm_f)
    v = dict(c_ctx=v_c_ctx, w_mod=v_w_mod, b_mod=v_b_mod, norm1=v_norm1, norm2=v_norm2, w_in=v_w_in, s5_a_re=v_s5_a_re,
             s5_a_im=v_s5_a_im, s5_log_dt=v_s5_log_dt, s5_b_re=v_s5_b_re, s5_b_im=v_s5_b_im, s5_c_re=v_s5_c_re,
             s5_c_im=v_s5_c_im, s5_d=v_s5_d, w_glu=v_w_glu, q_norm=v_q_norm, kv_norm=v_kv_norm, w_uq=v_w_uq, w_ukv=v_w_ukv,
             w_mla_o=v_w_mla_o, w_out=v_w_out, w_ffn_in=v_w_ffn_in, w_ffn_out=v_w_ffn_out, norm_f=v_norm_f)
    return _step(x, c, ctx, loss_target, w, m, v)
```

```python
import functools
import math

import jax
import jax.numpy as jnp
from jax import lax
from jax.experimental import pallas as pl
from jax.experimental.pallas import tpu as pltpu

F32 = jnp.float32
BF16 = jnp.bfloat16

EPS = 1e-6
GRID_W = 64
S5_GROUP = 16
S5_STATE = 64
MLA_HEADS = 8
QK_NOPE = 128
QK_ROPE = 64
V_DIM = 128
ROPE_BASE = 10000.0
ATTN_SCALE = (QK_NOPE + QK_ROPE) ** -0.5
ADAM_LR = 0.001
ADAM_B1 = 0.9
ADAM_B2 = 0.999
ADAM_EPS = 1e-08
ADAM_WD = 0.01
ADAM_STEP = 10

SUBLANES = 8
LANES = 128
V7X_VMEM_BYTES = 64 * 1024 * 1024
VMEM_LIMIT = (V7X_VMEM_BYTES * 7) // 8
N_SEG = 2 * SUBLANES
S5_BLOCK_GROUPS = 8
MESH = pl.DeviceIdType.MESH


def _pick(n, target, mult):
    best = None
    d = mult
    while d <= min(n, target):
        if n % d == 0:
            best = d
        d += mult
    return n if best is None else best


def _cparams(sem=None):
    return pltpu.CompilerParams(dimension_semantics=sem, vmem_limit_bytes=VMEM_LIMIT)


MM_VMEM_BUDGET = (V7X_VMEM_BYTES * 5) // 8


def _mm(a, b, *, ta=False, tb=False, out_dtype=F32, name, a_shards=1, b_shards=1, out_shards=1):
    if ta:
        K, M = a.shape
    else:
        M, K = a.shape[-2], a.shape[-1] * a_shards
    if tb:
        N, K2 = b.shape[-2], b.shape[-1] * b_shards
    else:
        K2, N = b.shape[-2], b.shape[-1] * b_shards
    assert K == K2, (a.shape, b.shape, ta, tb)
    n_unit = N // max(out_shards, 1 if tb else b_shards)
    k_unit = K // max(a_shards, b_shards if tb else 1)
    tn = _pick(n_unit, 1024, LANES)
    tm = _pick(M, 1024 if tn >= 512 else 2048, LANES if ta else 16)
    sa, sb, so = a.dtype.itemsize, b.dtype.itemsize, jnp.dtype(out_dtype).itemsize
    k_mult = LANES if (not ta or tb) else 16
    tk = k_mult if k_unit % k_mult == 0 else k_unit
    for cand in range(k_mult, k_unit + 1, k_mult):
        if k_unit % cand == 0 and 2 * cand * (tm * sa + tn * sb) + tm * tn * (4 + 2 * so) <= MM_VMEM_BUDGET:
            tk = cand
    nk = K // tk
    dims = (((0 if ta else 1,), (1 if tb else 0,)), ((), ()))

    def body(a_ref, b_ref, o_ref, *scratch):
        part = lax.dot_general(a_ref[...].astype(BF16), b_ref[...].astype(BF16), dims, preferred_element_type=F32)
        if nk == 1:
            o_ref[...] = part.astype(o_ref.dtype)
            return
        acc_ref, = scratch
        k = pl.program_id(2)

        @pl.when(k == 0)
        def _():
            acc_ref[...] = part

        @pl.when(k > 0)
        def _():
            acc_ref[...] += part

        @pl.when(k == nk - 1)
        def _():
            o_ref[...] = acc_ref[...].astype(o_ref.dtype)

    if ta:
        a_spec = pl.BlockSpec((tk, tm), lambda i, j, k: (k, i))
    elif a_shards == 1:
        a_spec = pl.BlockSpec((tm, tk), lambda i, j, k: (i, k))
    else:
        akb = (K // a_shards) // tk
        a_spec = pl.BlockSpec((None, tm, tk), lambda i, j, k: (k // akb, i, k % akb))
    if b_shards == 1:
        b_spec = pl.BlockSpec((tn, tk), lambda i, j, k: (j, k)) if tb else pl.BlockSpec((tk, tn), lambda i, j, k: (k, j))
    elif tb:
        kpb = (K // b_shards) // tk
        b_spec = pl.BlockSpec((None, tn, tk), lambda i, j, k: (k // kpb, j, k % kpb))
    else:
        npb = (N // b_shards) // tn
        b_spec = pl.BlockSpec((None, tk, tn), lambda i, j, k: (j // npb, k, j % npb))
    if out_shards == 1:
        out_spec = pl.BlockSpec((tm, tn), lambda i, j, k: (i, j))
        out_shape = jax.ShapeDtypeStruct((M, N), out_dtype)
    else:
        opb = (N // out_shards) // tn
        out_spec = pl.BlockSpec((None, tm, tn), lambda i, j, k: (j // opb, i, j % opb))
        out_shape = jax.ShapeDtypeStruct((out_shards, M, N // out_shards), out_dtype)
    return pl.pallas_call(
        body, name=name, grid=(M // tm, N // tn, nk),
        in_specs=[a_spec, b_spec], out_specs=out_spec, out_shape=out_shape,
        scratch_shapes=[pltpu.VMEM((tm, tn), F32)] if nk > 1 else [],
        compiler_params=_cparams(("parallel", "parallel", "arbitrary")),
    )(a, b)


FFN_TILE_ROWS = 2048


def _ffn_in_swiglu(x, w4, *, name):
    M, K = x.shape
    S, _, ns = w4.shape
    half = S * ns // 2
    tn = _pick(ns, 512, LANES)
    tm = _pick(M, FFN_TILE_ROWS, 16)
    npb = ns // tn

    def body(x_ref, wa_ref, wb_ref, h_ref, ab_ref):
        xb = x_ref[...].astype(BF16)
        a = jnp.dot(xb, wa_ref[...].astype(BF16), preferred_element_type=F32)
        b = jnp.dot(xb, wb_ref[...].astype(BF16), preferred_element_type=F32)
        h_ref[...] = (jax.nn.silu(a) * b).astype(h_ref.dtype)
        ab_ref[0] = a.astype(ab_ref.dtype)
        ab_ref[1] = b.astype(ab_ref.dtype)

    return pl.pallas_call(
        body, name=name, grid=(M // tm, half // tn),
        in_specs=[pl.BlockSpec((tm, K), lambda i, j: (i, 0)),
                  pl.BlockSpec((None, K, tn), lambda i, j: (j // npb, 0, j % npb)),
                  pl.BlockSpec((None, K, tn), lambda i, j: (S // 2 + j // npb, 0, j % npb))],
        out_specs=[pl.BlockSpec((tm, tn), lambda i, j: (i, j)), pl.BlockSpec((2, tm, tn), lambda i, j: (0, i, j))],
        out_shape=[jax.ShapeDtypeStruct((M, half), BF16), jax.ShapeDtypeStruct((2, M, half), BF16)],
        compiler_params=_cparams(("parallel", "parallel")),
    )(x, w4, w4)


def _ffn_out_dx_swiglu(dy, w, ab, *, name):
    M, D = dy.shape
    n2 = w.shape[0]
    tn = _pick(n2, 512, LANES)
    tm = _pick(M, FFN_TILE_ROWS // 2, 16)

    def body(dy_ref, w_ref, ab_ref, o_ref):
        dh = lax.dot_general(dy_ref[...].astype(BF16), w_ref[...].astype(BF16), NT_DIMS, preferred_element_type=F32)
        a, b = ab_ref[0].astype(F32), ab_ref[1].astype(F32)
        s = jax.nn.sigmoid(a)
        o_ref[0] = (dh * b * (s * (1.0 + a * (1.0 - s)))).astype(o_ref.dtype)
        o_ref[1] = (dh * (a * s)).astype(o_ref.dtype)

    return pl.pallas_call(
        body, name=name, grid=(M // tm, n2 // tn),
        in_specs=[pl.BlockSpec((tm, D), lambda i, j: (i, 0)), pl.BlockSpec((tn, D), lambda i, j: (j, 0)),
                  pl.BlockSpec((2, tm, tn), lambda i, j: (0, i, j))],
        out_specs=pl.BlockSpec((2, tm, tn), lambda i, j: (0, i, j)),
        out_shape=jax.ShapeDtypeStruct((2, M, n2), BF16),
        compiler_params=_cparams(("parallel", "parallel")),
    )(dy, w, ab)


ROW_TILE_BYTES = 6 * 1024 * 1024
STREAM_TILE_BYTES = 14 * 1024 * 1024


def _row_tile(tiled, extra_bytes=0, budget=ROW_TILE_BYTES):
    rows = tiled[0].shape[0]
    per_row = sum(a.shape[1] * 4 for a in tiled) + extra_bytes
    target = max(SUBLANES, budget // max(per_row, 1))
    return _pick(rows, min(target, 512), 16)


def _rw(f, tiled, bcast, out_dtypes, *, name, anchor=None, tile_bytes=ROW_TILE_BYTES):
    nt, nb = len(tiled), len(bcast)
    rows = tiled[0].shape[0]
    outs_aval = jax.eval_shape(f, *[jax.ShapeDtypeStruct((16, a.shape[1]), F32) for a in tiled],
                               *[jax.ShapeDtypeStruct(b.shape, F32) for b in bcast])
    widths = [o.shape[1] for o in outs_aval]
    tm = _row_tile(tiled, sum(w * 4 for w in widths), tile_bytes)

    extra = [] if anchor is None else [anchor]
    n_in = nt + nb + len(extra)

    def body(*refs):
        tin = [r[...].astype(F32) for r in refs[:nt]]
        bin_ = [r[...].astype(F32) for r in refs[nt:nt + nb]]
        outs = f(*tin, *bin_)
        for o_ref, o in zip(refs[n_in:], outs):
            o_ref[...] = o.astype(o_ref.dtype)

    in_specs = [pl.BlockSpec((tm, a.shape[1]), lambda i: (i, 0)) for a in tiled]
    in_specs += [pl.BlockSpec(b.shape, lambda i: (0, 0)) for b in bcast + extra]
    res = pl.pallas_call(
        body, name=name, grid=(rows // tm,), in_specs=in_specs,
        out_specs=[pl.BlockSpec((tm, w), lambda i: (i, 0)) for w in widths],
        out_shape=[jax.ShapeDtypeStruct((rows, w), dt) for w, dt in zip(widths, out_dtypes)],
        compiler_params=_cparams(("parallel",)),
    )(*tiled, *bcast, *extra)
    return list(res)


def _rw_vjp(f, tiled, bcast, cts, need_t, need_b, t_dtypes, *, name, anchor=None):
    nt, nb = len(tiled), len(bcast)
    rows = tiled[0].shape[0]
    flat_cts = [c for group in cts for c in group]
    t_idx = [i for i in range(nt) if need_t[i]]
    b_idx = [i for i in range(nb) if need_b[i]]
    tm = _row_tile(list(tiled) + flat_cts, sum(tiled[i].shape[1] * 4 for i in t_idx))
    nc = len(flat_cts)
    extra = [] if anchor is None else [anchor]

    def body(*refs):
        i = pl.program_id(0)
        tin = [r[...].astype(F32) for r in refs[:nt]]
        bin_ = [r[...].astype(F32) for r in refs[nt:nt + nb]]
        ct_refs = refs[nt + nb:nt + nb + nc]
        out_refs = refs[nt + nb + nc + len(extra):]
        outs, vjp_fn = jax.vjp(f, *tin, *bin_)
        ct_vals, pos = [], 0
        for o, group in zip(outs, cts):
            acc = jnp.zeros_like(o)
            for _ in group:
                acc = acc + ct_refs[pos][...].astype(F32)
                pos += 1
            ct_vals.append(acc)
        grads = vjp_fn(tuple(ct_vals))
        for o_ref, k in zip(out_refs[:len(t_idx)], t_idx):
            o_ref[...] = grads[k].astype(o_ref.dtype)
        for o_ref, k in zip(out_refs[len(t_idx):], b_idx):
            @pl.when(i == 0)
            def _(o_ref=o_ref):
                o_ref[...] = jnp.zeros_like(o_ref)

            o_ref[...] += grads[nt + k]

    in_specs = [pl.BlockSpec((tm, a.shape[1]), lambda i: (i, 0)) for a in tiled]
    in_specs += [pl.BlockSpec(b.shape, lambda i: (0, 0)) for b in bcast]
    in_specs += [pl.BlockSpec((tm, c.shape[1]), lambda i: (i, 0)) for c in flat_cts]
    in_specs += [pl.BlockSpec(e.shape, lambda i: (0, 0)) for e in extra]
    out_specs = [pl.BlockSpec((tm, tiled[k].shape[1]), lambda i: (i, 0)) for k in t_idx]
    out_specs += [pl.BlockSpec(bcast[k].shape, lambda i: (0, 0)) for k in b_idx]
    out_shape = [jax.ShapeDtypeStruct(tiled[k].shape, dt) for k, dt in zip(t_idx, t_dtypes)]
    out_shape += [jax.ShapeDtypeStruct(bcast[k].shape, F32) for k in b_idx]
    res = pl.pallas_call(
        body, name=name, grid=(rows // tm,), in_specs=in_specs, out_specs=out_specs, out_shape=out_shape,
        compiler_params=_cparams(("arbitrary",)),
    )(*tiled, *bcast, *flat_cts, *extra)
    res = list(res)
    return res[:len(t_idx)], res[len(t_idx):]


def _rms(x, g):
    return x * lax.rsqrt(jnp.mean(x * x, axis=-1, keepdims=True) + EPS) * g


def _f_norm_mod(x, g, sc, sh):
    return (_rms(x, g) * (1.0 + sc) + sh,)


def _f_norm_mod_keep(x, g, sc, sh):
    return (_rms(x, g) * (1.0 + sc) + sh, x)


@jax.custom_vjp
def _swap16(x):
    w = x.shape[-1]
    lane = lax.broadcasted_iota(jnp.int32, x.shape, x.ndim - 1)
    return jnp.where((lane & 16) == 0, pltpu.roll(x, w - 16, x.ndim - 1), pltpu.roll(x, 16, x.ndim - 1))


_swap16.defvjp(lambda x: (_swap16(x), None), lambda _, g: (_swap16(g),))


def _rope(x, cos, sin):
    return x * cos + _swap16(x) * sin


def _make_f_post_in(sw, q_rank, kv_rank, with_q):
    o1, o2, o3 = sw, sw + q_rank, sw + q_rank + kv_rank

    if with_q:
        def f(ha, cos, sin, qg, kvg):
            u = ha[:, :o1]
            cqn = _rms(ha[:, o1:o2], qg)
            ckvn = _rms(ha[:, o2:o3], kvg)
            kr = _rope(ha[:, o3:o3 + LANES], cos, sin)
            return u, cqn, ckvn, kr
    else:
        def f(ha, kvg):
            return ha[:, :o1], _rms(ha[:, o2:o3], kvg), ha[:, o3:o3 + LANES]
    return f


def _f_qpost(q2, cos, sin):
    parts = []
    for h in range(q2.shape[1] // (2 * LANES)):
        o = 2 * LANES * h
        parts += [q2[:, o:o + LANES], _rope(q2[:, o + LANES:o + 2 * LANES], cos, sin)]
    return (jnp.concatenate(parts, axis=1),)


def _f_s5post(u, r0, r1, d):
    return (jax.nn.gelu(d * u + r0 + r1, approximate=True),)


def _f_merge(ab, bm, gt):
    d = bm.shape[1]
    br_s5 = ab[:, :d] * jax.nn.sigmoid(ab[:, d:])
    g = jax.nn.sigmoid(gt)
    return (g[:, :d] * br_s5 + g[:, d:] * bm,)


def _f_resid_norm(x, out, g1, n2, sc2, sh2):
    x1 = x + g1 * out
    return x1, _rms(x1, n2) * (1.0 + sc2) + sh2


def _f_final(x1, f, tgt, g2, nf):
    y = _rms(x1 + g2 * f, nf)
    return (0.5 * jnp.mean(jnp.square(y - tgt), axis=-1, keepdims=True),)


def _bd_fanin(xs, ws, *, name):
    nw = len(ws)
    nb, kb, nn = ws[0].shape
    T = xs[0].shape[0]
    tm = _pick(T, 512, 16)

    def body(*refs):
        acc = None
        for x_ref, w_ref in zip(refs[:nw], refs[nw:2 * nw]):
            t = jnp.dot(x_ref[...].astype(BF16), w_ref[0].astype(BF16), preferred_element_type=F32)
            acc = t if acc is None else acc + t
        refs[2 * nw][...] = acc

    return pl.pallas_call(
        body, name=name, grid=(nb, T // tm),
        in_specs=[pl.BlockSpec((tm, kb), lambda j, i: (i, j))] * nw + [pl.BlockSpec((1, kb, nn), lambda j, i: (j, 0, 0))] * nw,
        out_specs=pl.BlockSpec((tm, nn), lambda j, i: (i, j)),
        out_shape=jax.ShapeDtypeStruct((T, nb * nn), F32),
        compiler_params=_cparams(("parallel", "parallel")),
    )(*xs, *ws)


def _bd_dw(xs, dys, nb, *, name):
    npair = len(xs)
    T = xs[0].shape[0]
    kb = xs[0].shape[1] // nb
    nn = dys[0].shape[1] // nb
    tm = _pick(T, 512, 16)
    dims = (((0,), (0,)), ((), ()))

    def body(*refs):
        i = pl.program_id(1)
        for x_ref, d_ref, o_ref in zip(refs[:npair], refs[npair:2 * npair], refs[2 * npair:]):
            @pl.when(i == 0)
            def _(o_ref=o_ref):
                o_ref[...] = jnp.zeros_like(o_ref)

            o_ref[0] += lax.dot_general(x_ref[...].astype(BF16), d_ref[...].astype(BF16), dims,
                                        preferred_element_type=F32)

    return list(pl.pallas_call(
        body, name=name, grid=(nb, T // tm),
        in_specs=[pl.BlockSpec((tm, kb), lambda j, i: (i, j))] * npair + [pl.BlockSpec((tm, nn), lambda j, i: (i, j))] * npair,
        out_specs=[pl.BlockSpec((1, kb, nn), lambda j, i: (j, 0, 0))] * npair,
        out_shape=[jax.ShapeDtypeStruct((nb, kb, nn), F32)] * npair,
        compiler_params=_cparams(("parallel", "arbitrary")),
    )(*xs, *dys))


def _cmul(ar, ai, br, bi):
    return ar * br - ai * bi, ar * bi + ai * br


def _cpow(lr, li, n):
    rr, ri = None, None
    br, bi = lr, li
    while n:
        if n & 1:
            rr, ri = (br, bi) if rr is None else _cmul(rr, ri, br, bi)
        n >>= 1
        if n:
            br, bi = _cmul(br, bi, br, bi)
    return rr, ri


SCAN_MM_ROWS = 512


def _s5_scan(x, w_re, w_im, lam_re, lam_im, h0_re, h0_im, e0_re, e0_im, *, reverse, name, readout=None):
    rows = x.shape[0]
    nb, kb, cb = w_re.shape
    C = nb * cb
    n = rows // N_SEG
    mm_rows = _pick(rows, SCAN_MM_ROWS, 16)
    seg_order = list(range(N_SEG))[::-1] if reverse else list(range(N_SEG))
    s_first, s_last = seg_order[0], seg_order[-1]
    n_ro = 0 if readout is None else 2

    def body(x_ref, wr_ref, wi_ref, lr_ref, li_ref, h0r_ref, h0i_ref, e0r_ref, e0i_ref, *rest):
        ro_refs, (hr_ref, hi_ref, htr_ref, hti_ref), y_refs = rest[:n_ro], rest[n_ro:n_ro + 4], rest[n_ro + 4:-2]
        locr_ref, loci_ref = rest[-2:]
        shape = (N_SEG, cb)
        lr = jnp.broadcast_to(lr_ref[...], shape)
        li = jnp.broadcast_to(li_ref[...], shape)
        row = lax.broadcasted_iota(jnp.int32, shape, 0)

        def step_of(k):
            return (n - 1 - k) if reverse else k

        def rows_of(k):
            return pl.ds(pl.multiple_of(step_of(k) * N_SEG, N_SEG), N_SEG)

        wr, wi = wr_ref[...].astype(BF16), wi_ref[...].astype(BF16)
        for r0 in range(0, rows, mm_rows):
            xb = x_ref[r0:r0 + mm_rows, :].astype(BF16)
            locr_ref[r0:r0 + mm_rows, :] = jnp.dot(xb, wr, preferred_element_type=F32)
            loci_ref[r0:r0 + mm_rows, :] = jnp.dot(xb, wi, preferred_element_type=F32)

        first = row == s_first
        hr = locr_ref[rows_of(0), :] + jnp.where(first, e0r_ref[...], 0.0)
        hi = loci_ref[rows_of(0), :] + jnp.where(first, e0i_ref[...], 0.0)
        locr_ref[rows_of(0), :] = hr
        loci_ref[rows_of(0), :] = hi

        def pass1(k, carry):
            hr, hi = carry
            pr, pi = _cmul(lr, li, hr, hi)
            hr = pr + locr_ref[rows_of(k), :]
            hi = pi + loci_ref[rows_of(k), :]
            locr_ref[rows_of(k), :] = hr
            loci_ref[rows_of(k), :] = hi
            return hr, hi

        er, ei = lax.fori_loop(1, n, pass1, (hr, hi))

        lnr, lni = _cpow(lr_ref[...], li_ref[...], n)
        cr, ci = h0r_ref[...], h0i_ref[...]
        cin_r = jnp.zeros(shape, F32)
        cin_i = jnp.zeros(shape, F32)
        for s in seg_order:
            cin_r = jnp.where(row == s, cr, cin_r)
            cin_i = jnp.where(row == s, ci, cin_i)
            if s != s_last:
                pr, pi = _cmul(lnr, lni, cr, ci)
                cr = pr + jnp.sum(jnp.where(row == s, er, 0.0), axis=0, keepdims=True)
                ci = pi + jnp.sum(jnp.where(row == s, ei, 0.0), axis=0, keepdims=True)

        def pass2(k, carry):
            pr, pi, _, _ = carry
            ar, ai = _cmul(pr, pi, cin_r, cin_i)
            hr = locr_ref[rows_of(k), :] + ar
            hi = loci_ref[rows_of(k), :] + ai
            hr_ref[rows_of(k), :] = hr.astype(hr_ref.dtype)
            hi_ref[rows_of(k), :] = hi.astype(hi_ref.dtype)
            npr, npi = _cmul(pr, pi, lr, li)
            return npr, npi, hr, hi

        _, _, last_r, last_i = lax.fori_loop(0, n, pass2, (lr, li, er, ei))
        htr_ref[...] = jnp.sum(jnp.where(row == s_last, last_r, 0.0), axis=0, keepdims=True)
        hti_ref[...] = jnp.sum(jnp.where(row == s_last, last_i, 0.0), axis=0, keepdims=True)

        if readout is not None:
            cr, ci = ro_refs[0][...].astype(BF16), ro_refs[1][...].astype(BF16)
            for r0 in range(0, rows, mm_rows):
                y_refs[0][r0:r0 + mm_rows, :] = (
                    jnp.dot(hr_ref[r0:r0 + mm_rows, :].astype(BF16), cr, preferred_element_type=F32)
                    + jnp.dot(hi_ref[r0:r0 + mm_rows, :].astype(BF16), ci, preferred_element_type=F32))

    big = pl.BlockSpec((rows, cb), lambda j: (0, j))
    vec = pl.BlockSpec((1, cb), lambda j: (0, j))
    wspec = pl.BlockSpec((None, kb, cb), lambda j: (j, 0, 0))
    in_specs = [pl.BlockSpec((rows, kb), lambda j: (0, j)), wspec, wspec] + [vec] * 6
    out_specs = [big, big, vec, vec]
    out_shape = [jax.ShapeDtypeStruct((rows, C), BF16)] * 2 + [jax.ShapeDtypeStruct((1, C), F32)] * 2
    extra = []
    if readout is not None:
        pb = readout[0].shape[2]
        in_specs += [pl.BlockSpec((None, cb, pb), lambda j: (j, 0, 0))] * 2
        out_specs.append(pl.BlockSpec((rows, pb), lambda j: (0, j)))
        out_shape.append(jax.ShapeDtypeStruct((rows, nb * pb), F32))
        extra = list(readout)
    return pl.pallas_call(
        body, name=name, grid=(nb,), in_specs=in_specs, out_specs=out_specs, out_shape=out_shape,
        scratch_shapes=[pltpu.VMEM((rows, cb), F32)] * 2,
        compiler_params=_cparams(("parallel",)),
    )(x, w_re, w_im, lam_re, lam_im, h0_re, h0_im, e0_re, e0_im, *extra)


def _s5_dlam(mu_re, mu_im, h_re, h_im, h0_re, h0_im, *, reverse, name):
    rows, C = h_re.shape
    n = rows // N_SEG
    cb = _pick(C, 256, LANES)
    s_first = N_SEG - 1 if reverse else 0

    def body(mr_ref, mi_ref, hr_ref, hi_ref, h0r_ref, h0i_ref, dr_ref, di_ref):
        shape = (N_SEG, cb)
        row = lax.broadcasted_iota(jnp.int32, shape, 0)

        def rows_of(k):
            step = (n - 1 - k) if reverse else k
            return pl.ds(pl.multiple_of(step * N_SEG, N_SEG), N_SEG)

        def term(k, pr, pi):
            mr, mi = mr_ref[rows_of(k), :].astype(F32), mi_ref[rows_of(k), :].astype(F32)
            return mr * pr + mi * pi, mi * pr - mr * pi

        shift = N_SEG - 1 if reverse else 1
        pr = jnp.where(row == s_first, h0r_ref[...], pltpu.roll(hr_ref[rows_of(n - 1), :].astype(F32), shift, 0))
        pi = jnp.where(row == s_first, h0i_ref[...], pltpu.roll(hi_ref[rows_of(n - 1), :].astype(F32), shift, 0))
        acc = term(0, pr, pi)

        def loop(k, acc):
            tr, ti = term(k, hr_ref[rows_of(k - 1), :].astype(F32), hi_ref[rows_of(k - 1), :].astype(F32))
            return acc[0] + tr, acc[1] + ti

        ar, ai = lax.fori_loop(1, n, loop, acc)
        dr_ref[...] = jnp.sum(ar, axis=0, keepdims=True)
        di_ref[...] = jnp.sum(ai, axis=0, keepdims=True)

    big = pl.BlockSpec((rows, cb), lambda j: (0, j))
    vec = pl.BlockSpec((1, cb), lambda j: (0, j))
    return pl.pallas_call(
        body, name=name, grid=(C // cb,),
        in_specs=[big] * 4 + [vec] * 2, out_specs=[vec, vec],
        out_shape=[jax.ShapeDtypeStruct((1, C), F32)] * 2,
        compiler_params=_cparams(("parallel",)),
    )(mu_re, mu_im, h_re, h_im, h0_re, h0_im)


NT_DIMS = (((1,), (1,)), ((), ()))
TN_DIMS = (((0,), (0,)), ((), ()))


ATTN_Q_ROWS = 512


def _attn_exp(q, kvh, kr):
    s = (lax.dot_general(q[:, :LANES], kvh[:, :LANES], NT_DIMS, preferred_element_type=F32)
         + lax.dot_general(q[:, LANES:], kr, NT_DIMS, preferred_element_type=F32))
    e = jnp.exp2((s - jnp.max(s, axis=-1, keepdims=True)) * (ATTN_SCALE * math.log2(math.e)))
    return e, jnp.sum(e, axis=-1, keepdims=True)


def _attn_specs(L, T, tq):
    return [
        pl.BlockSpec((tq, 2 * LANES), lambda h, i: (i, h)),
        pl.BlockSpec((T, 2 * LANES), lambda h, i: (0, h)),
        pl.BlockSpec((T, LANES), lambda h, i: (0, 0)),
    ]


def _attn_fwd(qq, kv, kr, *, name):
    L, T = qq.shape[0], kv.shape[0]
    tq = _pick(L, ATTN_Q_ROWS // 2, 16)

    def body(q_ref, kv_ref, kr_ref, o_ref):
        kvh = kv_ref[...]
        e, l = _attn_exp(q_ref[...], kvh, kr_ref[...])
        o_ref[...] = (jnp.dot(e.astype(BF16), kvh[:, LANES:], preferred_element_type=F32) * (1.0 / l)).astype(o_ref.dtype)

    return pl.pallas_call(
        body, name=name, grid=(MLA_HEADS, L // tq), in_specs=_attn_specs(L, T, tq),
        out_specs=pl.BlockSpec((tq, LANES), lambda h, i: (i, h)),
        out_shape=jax.ShapeDtypeStruct((L, MLA_HEADS * V_DIM), BF16),
        compiler_params=_cparams(("parallel", "parallel")),
    )(qq, kv, kr)


def _attn_bwd(qq, kv, kr, do, *, name):
    L, T = qq.shape[0], kv.shape[0]
    H = MLA_HEADS
    tq = _pick(L, ATTN_Q_ROWS, 16)
    nq = L // tq

    def body(q_ref, kv_ref, kr_ref, do_ref, dq_ref, dkv_ref, dkr_ref, dkn_acc, dv_acc):
        h, i = pl.program_id(0), pl.program_id(1)
        q, kvh, krv, dov = q_ref[...], kv_ref[...], kr_ref[...], do_ref[...]
        e, l = _attn_exp(q, kvh, krv)
        inv = 1.0 / l
        ps = e * (inv * ATTN_SCALE)
        t = lax.dot_general(dov, kvh[:, LANES:], NT_DIMS, preferred_element_type=F32) * ps
        ds = (t - ps * (jnp.sum(t, axis=-1, keepdims=True) * (1.0 / ATTN_SCALE))).astype(BF16)
        dq_ref[:, :LANES] = jnp.dot(ds, kvh[:, :LANES], preferred_element_type=F32)
        dq_ref[:, LANES:] = jnp.dot(ds, krv, preferred_element_type=F32)

        @pl.when(i == 0)
        def _():
            dkn_acc[...] = jnp.zeros_like(dkn_acc)
            dv_acc[...] = jnp.zeros_like(dv_acc)

        @pl.when((i == 0) & (h == 0))
        def _():
            dkr_ref[...] = jnp.zeros_like(dkr_ref)

        dv_acc[...] += lax.dot_general(e.astype(BF16), (dov.astype(F32) * inv).astype(BF16), TN_DIMS,
                                       preferred_element_type=F32)
        dkn_acc[...] += lax.dot_general(ds, q[:, :LANES], TN_DIMS, preferred_element_type=F32)
        dkr_ref[...] += lax.dot_general(ds, q[:, LANES:], TN_DIMS, preferred_element_type=F32)

        @pl.when(i == nq - 1)
        def _():
            dkv_ref[:, :LANES] = dkn_acc[...].astype(dkv_ref.dtype)
            dkv_ref[:, LANES:] = dv_acc[...].astype(dkv_ref.dtype)

    in_specs = _attn_specs(L, T, tq) + [pl.BlockSpec((tq, LANES), lambda h, i: (i, h))]
    return pl.pallas_call(
        body, name=name, grid=(H, L // tq), in_specs=in_specs,
        out_specs=[pl.BlockSpec((tq, 2 * LANES), lambda h, i: (i, h)), pl.BlockSpec((T, 2 * LANES), lambda h, i: (0, h)),
                   pl.BlockSpec((T, LANES), lambda h, i: (0, 0))],
        out_shape=[jax.ShapeDtypeStruct((L, H * 2 * LANES), F32), jax.ShapeDtypeStruct((T, H * 2 * LANES), BF16),
                   jax.ShapeDtypeStruct((T, LANES), F32)],
        scratch_shapes=[pltpu.VMEM((T, LANES), F32), pltpu.VMEM((T, LANES), F32)],
        compiler_params=_cparams(("arbitrary", "arbitrary")),
    )(qq, kv, kr, do)


def _adamw(w, g, m, v, *, name, anchor=None):
    c1 = 1.0 - ADAM_B1 ** ADAM_STEP
    c2 = 1.0 - ADAM_B2 ** ADAM_STEP

    def f(w, g, m, v):
        m = ADAM_B1 * m + (1.0 - ADAM_B1) * g
        v = ADAM_B2 * v + (1.0 - ADAM_B2) * jnp.square(g)
        delta = -ADAM_LR * ((m / c1) / (jnp.sqrt(v / c2) + ADAM_EPS) + ADAM_WD * w)
        return g, delta, m, v

    return _rw(f, [w, g, m, v], [], [F32] * 4, name=name, anchor=anchor, tile_bytes=STREAM_TILE_BYTES)


def _slab_rows(rows, cols, n_arrays):
    return _pick(rows, max(16, (8 * 1024 * 1024) // (cols * 4 * n_arrays)), 16)


def _scalars(*vals):
    return jnp.stack([jnp.asarray(v, jnp.int32) for v in vals])


def _into_slot(src, slot, nslots, dtype, *, name):
    R, C = src.shape
    tr = _slab_rows(R, C, 2)

    def body(s_ref, x_ref, o_ref):
        o_ref[...] = x_ref[...].astype(o_ref.dtype)

    return pl.pallas_call(
        body, name=name,
        grid_spec=pltpu.PrefetchScalarGridSpec(
            num_scalar_prefetch=1, grid=(R // tr,),
            in_specs=[pl.BlockSpec((tr, C), lambda i, s: (i, 0))],
            out_specs=pl.BlockSpec((None, tr, C), lambda i, s: (s[0], i, 0))),
        out_shape=jax.ShapeDtypeStruct((nslots, R, C), dtype),
        compiler_params=_cparams(("arbitrary",)),
    )(_scalars(slot), src)


def _pair_sum(g, got, c, *, name):
    _, R, C = g.shape
    hr = R // 2
    tr = _slab_rows(hr, C, 3)
    nblk = hr // tr

    def body(s_ref, g_ref, r_ref, o_ref):
        o_ref[...] = (g_ref[...].astype(F32) + r_ref[...].astype(F32)).astype(o_ref.dtype)

    return pl.pallas_call(
        body, name=name,
        grid_spec=pltpu.PrefetchScalarGridSpec(
            num_scalar_prefetch=1, grid=(4, nblk),
            in_specs=[pl.BlockSpec((None, tr, C), lambda j, i, s: (j, s[0] * nblk + i, 0)),
                      pl.BlockSpec((None, tr, C), lambda j, i, s: (j, i, 0))],
            out_specs=pl.BlockSpec((None, tr, C), lambda j, i, s: (j, i, 0))),
        out_shape=jax.ShapeDtypeStruct((4, hr, C), g.dtype),
        compiler_params=_cparams(("arbitrary", "arbitrary")),
    )(_scalars(c), g, got)


def _chip_sum(p, landed, me_chip, c, *, name):
    _, hr, C = p.shape
    tr = _slab_rows(hr, C, 5)

    def body(s_ref, p_ref, l0_ref, l1_ref, l2_ref, o_ref):
        o_ref[...] = ((p_ref[...].astype(F32) + l0_ref[...].astype(F32)) + l1_ref[...].astype(F32)) + l2_ref[...].astype(F32)

    return pl.pallas_call(
        body, name=name,
        grid_spec=pltpu.PrefetchScalarGridSpec(
            num_scalar_prefetch=1, grid=(hr // tr,),
            in_specs=[pl.BlockSpec((None, tr, C), lambda i, s: (s[0], i, 0))]
            + [pl.BlockSpec((None, tr, C), functools.partial(lambda i, s, k: (k, i, 0), k=k)) for k in range(3)],
            out_specs=pl.BlockSpec((None, tr, C), lambda i, s: (s[1], i, 0))),
        out_shape=jax.ShapeDtypeStruct((2, hr, C), F32),
        compiler_params=_cparams(("arbitrary",)),
    )(_scalars(me_chip, c), p, landed, landed, landed)


def _place():
    return lax.axis_index("x"), lax.axis_index("y"), lax.axis_index("c")


def _other_chips(x, y):
    chips = [(1 - x, y), (x, 1 - y), (1 - x, 1 - y)]
    return chips, [2 * cx + cy for cx, cy in chips]


HBM = pl.BlockSpec(memory_space=pl.ANY)


def _allgather8(v, *, name):
    rows, cols = v.shape

    def body(v_ref, out_ref, send_sems, recv_sems):
        x, y, c = _place()
        me = 4 * x + 2 * y + c
        out_ref[me] = v_ref[...]
        copies = []
        for k in range(1, 8):
            bx, by, bc = (k >> 2) & 1, (k >> 1) & 1, k & 1
            px, py, pc = x ^ bx, y ^ by, c ^ bc
            cp = pltpu.make_async_remote_copy(
                src_ref=v_ref, dst_ref=out_ref.at[me], send_sem=send_sems.at[k - 1], recv_sem=recv_sems.at[k - 1],
                device_id=(px, py, pc), device_id_type=MESH)
            cp.start()
            copies.append((cp, 4 * px + 2 * py + pc))
        for k, (cp, peer) in enumerate(copies):
            pltpu.make_async_remote_copy(
                src_ref=v_ref, dst_ref=out_ref.at[peer], send_sem=send_sems.at[k], recv_sem=recv_sems.at[k],
                device_id=(x, y, c), device_id_type=MESH).wait_recv()
        for cp, _ in copies:
            cp.wait_send()

    return pl.pallas_call(
        body, name=name, out_shape=jax.ShapeDtypeStruct((8, rows, cols), v.dtype),
        in_specs=[pl.BlockSpec(memory_space=pltpu.VMEM)], out_specs=pl.BlockSpec(memory_space=pltpu.VMEM),
        scratch_shapes=[pltpu.SemaphoreType.DMA((7,)), pltpu.SemaphoreType.DMA((7,))],
        compiler_params=pltpu.CompilerParams(vmem_limit_bytes=VMEM_LIMIT),
    )(v)


def _allgather_shards(bufs, *, name):
    n = len(bufs)

    def body(*refs):
        outs = refs[n:2 * n]
        send_sems, recv_sems = refs[2 * n:]
        x, y, c = _place()
        me_chip = 2 * x + y
        sibling = (x, y, 1 - c)
        chips, chip_ids = _other_chips(x, y)

        def remote(k, j, blk, hf, to):
            hr = bufs[k].shape[1] // 2
            piece = outs[k].at[blk, pl.ds(pl.multiple_of(hf * hr, 16), hr), :]
            return pltpu.make_async_remote_copy(
                src_ref=piece, dst_ref=piece, send_sem=send_sems.at[6 * k + j], recv_sem=recv_sems.at[6 * k + j],
                device_id=to, device_id_type=MESH)

        sends = []
        for k in range(n):
            for j, chip in enumerate(chips):
                cp = remote(k, j, me_chip, c, (*chip, c))
                cp.start()
                sends.append(cp)
        for k in range(n):
            for j, chip in enumerate(chips):
                remote(k, j, chip_ids[j], c, (x, y, c)).wait_recv()
                cp = remote(k, 3 + j, chip_ids[j], c, sibling)
                cp.start()
                sends.append(cp)
        for k in range(n):
            for j in range(3):
                remote(k, 3 + j, chip_ids[j], 1 - c, (x, y, c)).wait_recv()
        for cp in sends:
            cp.wait_send()

    return list(pl.pallas_call(
        body, name=name, out_shape=[jax.ShapeDtypeStruct(b.shape, b.dtype) for b in bufs],
        in_specs=[HBM] * n, out_specs=[HBM] * n, input_output_aliases={k: k for k in range(n)},
        scratch_shapes=[pltpu.SemaphoreType.DMA((6 * n,)), pltpu.SemaphoreType.DMA((6 * n,))],
    )(*bufs))


HBM_SPEC = pl.BlockSpec(memory_space=pltpu.HBM)
SEM_SPEC = pl.BlockSpec(memory_space=pltpu.SEMAPHORE)
EFFECT = pltpu.SideEffectType.DATAFLOW_SIDE_EFFECTING
TOKEN = jax.ShapeDtypeStruct((SUBLANES, LANES), F32)


def _in_hbm(a):
    return pltpu.with_memory_space_constraint(a, pltpu.HBM)


def _half_rows(buf, hf):
    hr = buf.shape[1] // 2
    return pl.ds(pl.multiple_of(hf * hr, 16), hr)


def _plan_ag_ici(refs):
    x, y, c = _place()
    chips, ids = _other_chips(x, y)
    out = []
    for r in refs:
        mine = r.at[2 * x + y, _half_rows(r, c), :]
        out += [(mine, mine, r.at[ids[j], _half_rows(r, c), :], (*chip, c)) for j, chip in enumerate(chips)]
    return out


def _plan_ag_pair(refs):
    x, y, c = _place()
    _, ids = _other_chips(x, y)
    out = []
    for r in refs:
        for j in range(3):
            piece = r.at[ids[j], _half_rows(r, c), :]
            out.append((piece, piece, r.at[ids[j], _half_rows(r, 1 - c), :], (x, y, 1 - c)))
    return out


def _plan_rs_ici(refs):
    x, y, c = _place()
    chips, ids = _other_chips(x, y)
    n = len(refs) // 2
    return [(refs[k].at[ids[j]], refs[n + k].at[j], refs[n + k].at[j], (*chip, c))
            for k in range(n) for j, chip in enumerate(chips)]


def _plan_pair_exchange(refs):
    x, y, c = _place()
    n = len(refs) // 2
    return [(refs[k].at[:, _half_rows(refs[k], 1 - c), :], refs[n + k], refs[n + k], (x, y, 1 - c)) for k in range(n)]


def _plan_pair_gather(refs):
    x, y, c = _place()
    return [(r.at[c], r.at[c], r.at[1 - c], (x, y, 1 - c)) for r in refs]


def _remote(src, dst, send_sem, recv_sem, target):
    return pltpu.make_async_remote_copy(src_ref=src, dst_ref=dst, send_sem=send_sem, recv_sem=recv_sem,
                                        device_id=target, device_id_type=MESH)


def _copy_start(groups, *, name, after=()):
    flat = [a for arrays, _, _ in groups for a in arrays]
    n, ng = len(flat), len(groups)
    after = list(after)
    n_in = n + len(after)

    def body(*refs):
        sems = refs[n_in:n_in + 2 * ng]
        thru = refs[n_in + 2 * ng:n_in + 2 * ng + n]
        token = refs[-1]
        pos = 0
        for g, (arrays, plan, n_copies) in enumerate(groups):
            copies = plan(thru[pos:pos + len(arrays)])
            pos += len(arrays)
            assert len(copies) == n_copies
            for i, (src, dst, _, target) in enumerate(copies):
                _remote(src, dst, sems[2 * g].at[i], sems[2 * g + 1].at[i], target).start()
        token[...] = jnp.zeros_like(token)

    out_shape = tuple(pltpu.SemaphoreType.DMA((n_copies,)) for _, _, n_copies in groups for _ in range(2))
    out_shape += tuple(pltpu.HBM(a.shape, a.dtype) for a in flat) + (TOKEN,)
    res = pl.pallas_call(
        body, name=name, out_shape=out_shape,
        in_specs=(HBM_SPEC,) * n + (pl.BlockSpec(memory_space=pl.ANY),) * len(after),
        out_specs=(SEM_SPEC,) * (2 * ng) + (HBM_SPEC,) * n + (pl.BlockSpec(memory_space=pltpu.VMEM),),
        input_output_aliases={k: 2 * ng + k for k in range(n)},
        compiler_params=pltpu.CompilerParams(has_side_effects=EFFECT),
    )(*[_in_hbm(a) for a in flat], *after)
    sems = [(res[2 * g], res[2 * g + 1]) for g in range(ng)]
    thru, pos = [], 2 * ng
    for arrays, _, _ in groups:
        thru.append(list(res[pos:pos + len(arrays)]))
        pos += len(arrays)
    return sems, thru, res[-1]


def _copy_wait(arrays, sems, plan, n_copies, after, *, name):
    n = len(arrays)
    after = list(after)

    def body(*refs):
        send, recv = refs[n], refs[n + 1]
        x, y, c = _place()
        copies = plan(refs[:n])
        assert len(copies) == n_copies
        for i, (src, dst, landing, target) in enumerate(copies):
            _remote(src, dst, send.at[i], recv.at[i], target).wait_send()
            _remote(landing, landing, send.at[i], recv.at[i], (x, y, c)).wait_recv()

    return list(pl.pallas_call(
        body, name=name, out_shape=tuple(pltpu.HBM(a.shape, a.dtype) for a in arrays),
        in_specs=(HBM_SPEC,) * n + (SEM_SPEC, SEM_SPEC) + (pl.BlockSpec(memory_space=pl.ANY),) * len(after),
        out_specs=(HBM_SPEC,) * n, input_output_aliases={k: k for k in range(n)},
        compiler_params=pltpu.CompilerParams(has_side_effects=EFFECT),
    )(*arrays, *sems, *after))


def _rs_stage1(gs, tag, after=()):
    n = len(gs)
    lands = [lax.empty((4, g.shape[1] // 2, g.shape[2]), g.dtype) for g in gs]
    sems, (arrays,), token = _copy_start([(list(gs) + lands, _plan_pair_exchange, n)], name=f"rs_pair_start_{tag}",
                                         after=after)
    return (sems[0], arrays), token


def _rs_stage2(handle, after, tag):
    sems, arrays = handle
    n = len(arrays) // 2
    arrays = _copy_wait(arrays, sems, _plan_pair_exchange, n, after, name=f"rs_pair_wait_{tag}")
    c = lax.axis_index("c")
    pair = [_pair_sum(g, r, c, name=f"rs_pair_sum_{tag}{k}") for k, (g, r) in enumerate(zip(arrays[:n], arrays[n:]))]
    lands = [lax.empty((3,) + p.shape[1:], p.dtype) for p in pair]
    sems, (arrays,), token = _copy_start([(pair + lands, _plan_rs_ici, 3 * n)], name=f"rs_start_{tag}")
    return (sems[0], arrays), token


def _rs_stage3(handle, after, tag):
    sems, arrays = handle
    n = len(arrays) // 2
    arrays = _copy_wait(arrays, sems, _plan_rs_ici, 3 * n, after, name=f"rs_wait_{tag}")
    x, y, c = _place()
    halves = [_chip_sum(p, l, 2 * x + y, c, name=f"rs_chip_sum_{tag}{k}") for k, (p, l) in enumerate(zip(arrays[:n], arrays[n:]))]
    sems, (halves,), token = _copy_start([(halves, _plan_pair_gather, n)], name=f"rs_gather_start_{tag}")
    return (sems[0], halves), token


def _rs_stage4(handle, after, tag):
    sems, halves = handle
    full = _copy_wait(halves, sems, _plan_pair_gather, len(halves), after, name=f"rs_gather_wait_{tag}")
    return [f.reshape(2 * f.shape[1], f.shape[2]) for f in full]


def _to_segments(a):
    rows = a.shape[0]
    return a.reshape(N_SEG, rows // N_SEG, -1).transpose(1, 0, 2).reshape(rows, -1)


def _from_segments(a):
    rows = a.shape[0]
    return a.reshape(rows // N_SEG, N_SEG, -1).transpose(1, 0, 2).reshape(rows, -1)


def _rope_tables(L):
    t = jnp.arange(L, dtype=jnp.int32)
    row = (t // GRID_W).astype(F32)
    col = (t % GRID_W).astype(F32)
    n_freq = QK_ROPE // 4
    inv = ROPE_BASE ** (-jnp.arange(n_freq, dtype=F32) / n_freq)
    a0, a1 = row[:, None] * inv, col[:, None] * inv
    z = jnp.zeros((L, LANES - QK_ROPE), F32)
    cos = jnp.concatenate([jnp.cos(a0), jnp.cos(a0), jnp.cos(a1), jnp.cos(a1), z], axis=1)
    sin = jnp.concatenate([-jnp.sin(a0), jnp.sin(a0), -jnp.sin(a1), jnp.sin(a1), z], axis=1)
    return _to_segments(cos), _to_segments(sin)


def _col_blocks(w, nblk):
    r, c = w.shape
    return w.reshape(r, nblk, c // nblk).transpose(1, 0, 2)


def _from_col_blocks(w4):
    nblk, r, c = w4.shape
    return w4.transpose(1, 0, 2).reshape(r, nblk * c)


def _s5_discretize(a_re, a_im, log_dt, b_re, b_im):
    dt = jnp.exp(log_dt)[:, None]
    mag = jnp.exp(a_re * dt)
    ab_re, ab_im = mag * jnp.cos(a_im * dt), mag * jnp.sin(a_im * dt)
    den = a_re * a_re + a_im * a_im
    nr, ni = ab_re - 1.0, ab_im
    co_re = (nr * a_re + ni * a_im) / den
    co_im = (ni * a_re - nr * a_im) / den
    bb_re = co_re[..., None] * b_re - co_im[..., None] * b_im
    bb_im = co_re[..., None] * b_im + co_im[..., None] * b_re
    return ab_re, ab_im, bb_re, bb_im


def _diag_blocks_in(bb, gpb):
    G, N, P = bb.shape
    t = jnp.tile(jnp.swapaxes(bb, 1, 2).reshape(G // gpb, gpb * P, N), (1, 1, gpb))
    row = lax.broadcasted_iota(jnp.int32, t.shape, 1) // P
    col = lax.broadcasted_iota(jnp.int32, t.shape, 2) // N
    return jnp.where(row == col, t, 0.0)


def _diag_blocks_out(cc, gpb):
    G, P, N = cc.shape
    t = jnp.tile(jnp.swapaxes(cc, 1, 2).reshape(G // gpb, gpb * N, P), (1, 1, gpb))
    row = lax.broadcasted_iota(jnp.int32, t.shape, 1) // N
    col = lax.broadcasted_iota(jnp.int32, t.shape, 2) // P
    return jnp.where(row == col, t, 0.0)


def _tr(ws):
    return [jnp.swapaxes(w, 1, 2) for w in ws]


WEIGHTS = ['c_ctx', 'w_mod', 'b_mod', 'norm1', 'norm2', 'w_in', 's5_a_re', 's5_a_im', 's5_log_dt', 's5_b_re', 's5_b_im',
           's5_c_re', 's5_c_im', 's5_d', 'w_glu', 'q_norm', 'kv_norm', 'w_uq', 'w_ukv', 'w_mla_o', 'w_out', 'w_ffn_in',
           'w_ffn_out', 'norm_f']
AG_GROUPS = [['w_in'], ['w_glu', 'w_uq', 'w_ukv', 'w_mla_o', 'w_out'], ['w_ffn_in', 'w_ffn_out']]
SMALL = ['norm1', 'norm2', 's5_a_re', 's5_a_im', 's5_log_dt', 's5_b_re', 's5_b_im', 's5_c_re', 's5_c_im', 's5_d',
         'q_norm', 'kv_norm', 'norm_f']


def _pad_rows(a, rows):
    return jnp.concatenate([a, jnp.zeros((rows - a.shape[0],) + a.shape[1:], a.dtype)], axis=0)


def _pack(vals, width, rows):
    flat = jnp.concatenate([v.reshape(-1).astype(F32) for v in vals])
    flat = jnp.concatenate([flat, jnp.zeros((rows * width - flat.shape[0],), F32)])
    return flat.reshape(rows, width)


def _unpack(buf, like):
    flat = buf.reshape(-1)
    out, pos = [], 0
    for v in like:
        out.append(flat[pos:pos + v.size].reshape(v.shape))
        pos += v.size
    return out


def _step(x, c, ctx, loss_target, w, m, v):
    px, py, pc = _place()
    me = 4 * px + 2 * py + pc
    me_chip = 2 * px + py
    L, D = x.shape[1], x.shape[2]
    Lc = ctx.shape[1]
    T = L + Lc
    SW = D // 2
    G = SW // S5_GROUP
    C = G * S5_STATE
    H = MLA_HEADS
    q_rank = w['q_norm'].shape[1]
    kv_rank = w['kv_norm'].shape[1]
    d_ff = w['w_ffn_out'].shape[1] * 4
    wa_used = SW + q_rank + kv_rank + QK_ROPE
    WA = -(-(SW + q_rank + kv_rank + LANES) // 512) * 512

    c_all = _allgather8(c.astype(F32).reshape(SUBLANES, D // SUBLANES), name="ag_cond").reshape(8, D)
    cond = jnp.concatenate([c_all, w['c_ctx'].reshape(1, D)], axis=0)
    cond = _pad_rows(cond, 16)
    (act,) = _rw(lambda t: (jax.nn.silu(t),), [cond], [], [F32], name="cond_silu")
    w_mod, cs_mod = w['w_mod'][0], w['w_mod'].shape[2]
    mod_part = _mm(act, w_mod, out_dtype=F32, name="mod_fwd")
    mod_all = _allgather8(mod_part, name="ag_mod")
    mod_full = jnp.concatenate([mod_all[0], mod_all[2], mod_all[4], mod_all[6]], axis=1) + w['b_mod']
    m_lat = lax.dynamic_slice_in_dim(mod_full, me, 1, axis=0).reshape(6, D)
    m_ctx = mod_full[8].reshape(6, D)
    sh1, sc1, g1, sh2, sc2, g2 = (m_lat[i:i + 1] for i in range(6))
    csh1, csc1 = m_ctx[0:1], m_ctx[1:2]

    ag_groups = [([_into_slot(w[nme][0], me_chip, 4, BF16, name=f"cast_{nme}") for nme in grp], _plan_ag_ici, 3 * len(grp))
                 for grp in AG_GROUPS]
    ag_sems, ag_bufs, ag_token = _copy_start(ag_groups, name="ag_start", after=[mod_full])
    gathered, ag_pair = {}, {}

    def landed(g, after):
        n_cp = 3 * len(AG_GROUPS[g])
        got = _copy_wait(ag_bufs[g], ag_sems[g], _plan_ag_ici, n_cp, after, name=f"ag_wait_{g}")
        sems, (got,), token = _copy_start([(got, _plan_ag_pair, n_cp)], name=f"ag_pair_start_{g}")
        ag_pair[g] = (sems[0], got)
        return token[0, 0]

    def arrive(g, after):
        sems, got = ag_pair[g]
        got = _copy_wait(got, sems, _plan_ag_pair, 3 * len(AG_GROUPS[g]), after, name=f"ag_pair_wait_{g}")
        gathered.update(zip(AG_GROUPS[g], got))

    xs = _to_segments(x[0])
    cs = _to_segments(ctx[0])
    tgt = _to_segments(loss_target[0])
    cos, sin = _rope_tables(L)
    n1, n2, nf = w['norm1'], w['norm2'], w['norm_f'].reshape(1, D)
    qg, kvg = w['q_norm'], w['kv_norm']

    (xn_lat,) = _rw(_f_norm_mod, [xs], [n1 + ag_token[0, 0], sc1, sh1], [BF16], name="norm1_lat")
    (xn_ctx,) = _rw(_f_norm_mod, [cs], [n1, csc1, csh1], [BF16], name="norm1_ctx")
    xn = jnp.concatenate([xn_lat, xn_ctx], axis=0)
    tok = landed(0, [xn])

    gpb = min(S5_BLOCK_GROUPS, G)
    gpo = min(8, G)
    d_skip = w['s5_d'][0].reshape(1, SW)
    disc, vjp_disc, w_b, w_c = [], [], [], []
    for d in range(2):
        prm = (w['s5_a_re'][0, d], w['s5_a_im'][0, d], w['s5_log_dt'][0, d] + tok, w['s5_b_re'][0, d], w['s5_b_im'][0, d])

        def prep(a_re, a_im, log_dt, b_re, b_im):
            ab_re, ab_im, bb_re, bb_im = _s5_discretize(a_re, a_im, log_dt, b_re, b_im)
            return ab_re.reshape(1, C), ab_im.reshape(1, C), _diag_blocks_in(bb_re, gpb), _diag_blocks_in(bb_im, gpb)

        out, vj = jax.vjp(prep, *prm)
        disc.append(out)
        vjp_disc.append(vj)
        w_b += [out[2], out[3]]
        w_c += [_diag_blocks_out(w['s5_c_re'][0, d], gpo), -_diag_blocks_out(w['s5_c_im'][0, d], gpo)]
    nb_in = G // gpb
    nb_out = G // gpo

    arrive(0, [xn, tgt] + w_b + w_c)
    w_in = _from_col_blocks(gathered['w_in'])
    w_a = jnp.concatenate([w_in[:, :wa_used], jnp.zeros((D, WA - wa_used), BF16)], axis=1)
    w_g = w_in[:, wa_used:]
    ha = _mm(xn, w_a, out_dtype=F32, name="in_proj")
    ha_lat, ha_ctx = ha[:L], ha[L:]
    gt = _mm(xn_lat, w_g, out_dtype=F32, name="in_gates")
    f_post_lat = _make_f_post_in(SW, q_rank, kv_rank, True)
    f_post_ctx = _make_f_post_in(SW, q_rank, kv_rank, False)
    u_lat, cqn, ckvn_lat, kr_lat = _rw(f_post_lat, [ha_lat, cos, sin], [qg, kvg], [F32, BF16, BF16, BF16], name="post_in_lat")
    u_ctx, ckvn_ctx, kr_ctx = _rw(f_post_ctx, [ha_ctx], [kvg], [F32, BF16, BF16], name="post_in_ctx")
    zero = jnp.zeros((1, C), F32) + landed(1, [u_lat, u_ctx])

    h_lat, h_ctx, hT_ctx, r5 = [], [], [], []
    for d, rev in enumerate((False, True)):
        lr, li = disc[d][0], disc[d][1]
        hcr, hci, tr, ti = _s5_scan(u_ctx, w_b[2 * d], w_b[2 * d + 1], lr, li, zero, zero, zero, zero, reverse=rev,
                                    name=f"s5_scan_ctx_{d}")
        hlr, hli, _, _, y = _s5_scan(u_lat, w_b[2 * d], w_b[2 * d + 1], lr, li, tr, ti, zero, zero, reverse=rev,
                                     name=f"s5_scan_lat_{d}", readout=(w_c[2 * d], w_c[2 * d + 1]))
        h_ctx += [hcr, hci]
        h_lat += [hlr, hli]
        hT_ctx += [tr, ti]
        r5.append(y)
    (z,) = _rw(_f_s5post, [u_lat] + r5, [d_skip], [BF16], name="s5_post")

    arrive(1, [z])
    w_glu, w_ukv, w_mla_o = (gathered[nme] for nme in ('w_glu', 'w_ukv', 'w_mla_o'))
    w_out = gathered['w_out'].reshape(D, D)
    uq3 = _from_col_blocks(gathered['w_uq']).reshape(q_rank, H, QK_NOPE + QK_ROPE)
    w_q2 = jnp.concatenate([uq3, jnp.zeros((q_rank, H, LANES - QK_ROPE), BF16)], axis=2).reshape(q_rank, H * 2 * LANES)
    q2 = _mm(cqn, w_q2, out_dtype=F32, name="q_up")
    (qq,) = _rw(_f_qpost, [q2, cos, sin], [], [BF16], name="q_rope")
    kvn = jnp.concatenate([ckvn_lat, ckvn_ctx], axis=0)
    kr_all = jnp.concatenate([kr_lat, kr_ctx], axis=0)
    kv = _mm(kvn, w_ukv, b_shards=4, out_dtype=BF16, name="kv_up")
    kr_all = kr_all + landed(2, [kv, qq]).astype(BF16)
    o = _attn_fwd(qq, kv, kr_all, name="attn_fwd")

    ab = _mm(z, w_glu, b_shards=4, out_dtype=F32, name="glu_proj")
    bm = _mm(o, w_mla_o, b_shards=4, out_dtype=F32, name="mla_out")
    (mix,) = _rw(_f_merge, [ab, bm, gt], [], [BF16], name="merge")
    out1 = _mm(mix, w_out, out_dtype=F32, name="out_proj")
    x1, xn2 = _rw(_f_resid_norm, [xs, out1], [g1, n2, sc2, sh2], [F32, BF16], name="resid_norm2")
    arrive(2, [xn2])
    w_ffn_in = gathered['w_ffn_in']
    w_ffn_out = gathered['w_ffn_out'].reshape(d_ff, D)
    hmid, ab2 = _ffn_in_swiglu(xn2, w_ffn_in, name="ffn_in")
    f2 = _mm(hmid, w_ffn_out, out_dtype=F32, name="ffn_out")
    (row_loss,) = _rw(_f_final, [x1, f2, tgt], [g2, nf], [F32], name="final_loss")
    loss = lax.psum(jnp.sum(row_loss), ("x", "y", "c"))

    ones = jnp.ones((L, 1), F32)
    (dx1_a, df2), (dg2, dnf) = _rw_vjp(_f_final, [x1, f2, tgt], [g2, nf], [[ones]], [True, True, False], [True, True],
                                       [F32, BF16], name="final_loss_bwd")
    gw_ffn_out = _mm(hmid, df2, ta=True, out_dtype=BF16, name="ffn_out_dw")
    dab2 = _ffn_out_dx_swiglu(df2, w_ffn_out, ab2, name="ffn_out_dx")
    dxn2 = _mm(dab2, w_ffn_in, tb=True, a_shards=2, b_shards=4, out_dtype=F32, name="ffn_in_dx")
    gw_ffn_in = _mm(xn2, dab2, ta=True, b_shards=2, out_shards=4, out_dtype=BF16, name="ffn_in_dw")
    rs_ffn, tok = _rs_stage1([gw_ffn_out.reshape(4, -1, D), gw_ffn_in], "ffn")
    (dx_a, dout1), (dg1, dn2, dsc2, dsh2) = _rw_vjp(
        _f_resid_norm, [xs, out1], [g1, n2 + tok[0, 0], sc2, sh2], [[dx1_a], [dxn2]], [True, True], [True] * 4, [F32, BF16],
        name="resid_norm2_bwd")
    dmix = _mm(dout1, w_out, tb=True, out_dtype=F32, name="out_proj_dx")
    rs_ffn, tok = _rs_stage2(rs_ffn, [dmix], "ffn")
    gw_out = _mm(mix, dout1, ta=True, out_dtype=BF16, name="out_proj_dw")
    (dab, dbm, dgt), _ = _rw_vjp(_f_merge, [ab, bm, gt], [], [[dmix]], [True] * 3, [], [BF16] * 3, name="merge_bwd",
                                 anchor=tok)
    dz = _mm(dab, w_glu, tb=True, b_shards=4, out_dtype=F32, name="glu_proj_dx")
    gw_glu = _mm(z, dab, ta=True, out_shards=4, out_dtype=BF16, name="glu_proj_dw")
    do = _mm(dbm, w_mla_o, tb=True, b_shards=4, out_dtype=BF16, name="mla_out_dx")
    gw_mla_o = _mm(o, dbm, ta=True, out_shards=4, out_dtype=BF16, name="mla_out_dw")
    dxn_g = _mm(dgt, w_g, tb=True, out_dtype=F32, name="in_gates_dx")
    gw_g = _mm(xn_lat, dgt, ta=True, out_dtype=BF16, name="in_gates_dw")
    rs_mid, tok = _rs_stage1([gw_out.reshape(4, -1, D), gw_glu, gw_mla_o], "mid", after=[gw_g])

    (du_a, dr5), (dd_skip,) = _rw_vjp(_f_s5post, [u_lat] + r5, [d_skip + tok[0, 0]], [[dz]], [True, True, False], [True],
                                      [F32, F32], name="s5_post_bwd")
    dw_c = _bd_dw(h_lat, [dr5] * 4, nb_out, name="s5_readout_dw")
    rs_mid, tok = _rs_stage2(rs_mid, dw_c[:1], "mid")
    zero = zero + tok[0, 0]
    w_ct = _tr(w_c)
    zeros_ctx = jnp.zeros((Lc, SW), BF16)
    mu_lat, mu_ctx, dlam = [], [], []
    for d, rev in enumerate((False, True)):
        lr, li = disc[d][0], disc[d][1]
        mlr, mli, fr, fi = _s5_scan(dr5, w_ct[2 * d], w_ct[2 * d + 1], lr, -li, zero, zero, zero, zero, reverse=not rev,
                                    name=f"s5_adj_lat_{d}")
        dh0r, dh0i = _cmul(lr, -li, fr, fi)
        mcr, mci, _, _ = _s5_scan(zeros_ctx, w_ct[2 * d], w_ct[2 * d + 1], lr, -li, zero, zero, dh0r, dh0i,
                                  reverse=not rev, name=f"s5_adj_ctx_{d}")
        dl_lat = _s5_dlam(mlr, mli, h_lat[2 * d], h_lat[2 * d + 1], hT_ctx[2 * d], hT_ctx[2 * d + 1], reverse=rev,
                          name=f"s5_dlam_lat_{d}")
        dl_ctx = _s5_dlam(mcr, mci, h_ctx[2 * d], h_ctx[2 * d + 1], zero, zero, reverse=rev, name=f"s5_dlam_ctx_{d}")
        mu_lat += [mlr, mli]
        mu_ctx += [mcr, mci]
        dlam.append((dl_lat[0] + dl_ctx[0], dl_lat[1] + dl_ctx[1]))
    du_b = _bd_fanin(mu_lat, _tr(w_b), name="s5_bu_lat_dx")
    du_ctx = _bd_fanin(mu_ctx, _tr(w_b), name="s5_bu_ctx_dx")
    dw_b_lat = _bd_dw([u_lat] * 4, mu_lat, nb_in, name="s5_bu_lat_dw")
    dw_b_ctx = _bd_dw([u_ctx] * 4, mu_ctx, nb_in, name="s5_bu_ctx_dw")
    g_s5 = {}
    for d in range(2):
        ct = (dlam[d][0], dlam[d][1], dw_b_lat[2 * d] + dw_b_ctx[2 * d], dw_b_lat[2 * d + 1] + dw_b_ctx[2 * d + 1])
        ga_re, ga_im, gdt, gb_re, gb_im = vjp_disc[d](ct)
        _, vj_c = jax.vjp(lambda cr, ci: (_diag_blocks_out(cr, gpo), -_diag_blocks_out(ci, gpo)),
                          w['s5_c_re'][0, d], w['s5_c_im'][0, d])
        gc_re, gc_im = vj_c((dw_c[2 * d], dw_c[2 * d + 1]))
        for nme, val in (('s5_a_re', ga_re), ('s5_a_im', ga_im), ('s5_log_dt', gdt), ('s5_b_re', gb_re),
                         ('s5_b_im', gb_im), ('s5_c_re', gc_re), ('s5_c_im', gc_im)):
            g_s5.setdefault(nme, []).append(val)
    g_small = {nme: jnp.stack(vals)[None] for nme, vals in g_s5.items()}
    g_small['s5_d'] = dd_skip.reshape(w['s5_d'].shape)

    dqq, dkv, dkr = _attn_bwd(qq, kv, kr_all, do, name="attn_bwd")
    (dq2,), _ = _rw_vjp(_f_qpost, [q2, cos, sin], [], [[dqq]], [True, False, False], [], [BF16], name="q_rope_bwd")
    dcqn = _mm(dq2, w_q2, tb=True, out_dtype=F32, name="q_up_dx")
    gw_q2 = _mm(cqn, dq2, ta=True, out_dtype=BF16, name="q_up_dw")
    dckvn = _mm(dkv, w_ukv, tb=True, b_shards=4, out_dtype=F32, name="kv_up_dx")
    gw_ukv = _mm(kvn, dkv, ta=True, out_shards=4, out_dtype=BF16, name="kv_up_dw")
    gw_uq = gw_q2.reshape(q_rank, H, 2 * LANES)[:, :, :QK_NOPE + QK_ROPE].reshape(q_rank, H * (QK_NOPE + QK_ROPE))
    rs_kv, tok = _rs_stage1([_col_blocks(gw_uq, 4), gw_ukv], "kv")

    (dha_lat,), (dqg, dkvg_lat) = _rw_vjp(
        f_post_lat, [ha_lat, cos, sin], [qg, kvg + tok[0, 0]], [[du_a, du_b], [dcqn], [dckvn[:L]], [dkr[:L]]],
        [True, False, False], [True, True], [BF16], name="post_in_lat_bwd")
    (dha_ctx,), (dkvg_ctx,) = _rw_vjp(f_post_ctx, [ha_ctx], [kvg], [[du_ctx], [dckvn[L:]], [dkr[L:]]], [True], [True],
                                      [BF16], name="post_in_ctx_bwd")
    dha = jnp.concatenate([dha_lat, dha_ctx], axis=0)
    dxn = _mm(dha, w_a, tb=True, out_dtype=F32, name="in_proj_dx")
    gw_a = _mm(xn, dha, ta=True, out_dtype=BF16, name="in_proj_dw")
    rs_kv, tok = _rs_stage2(rs_kv, [gw_a], "kv")
    (dx_seg,), (dn1_lat, dsc1, dsh1) = _rw_vjp(
        _f_norm_mod_keep, [xs], [n1 + tok[0, 0], sc1, sh1], [[dxn[:L], dxn_g], [dx_a]], [True], [True] * 3, [F32],
        name="norm1_lat_bwd")
    _, (dn1_ctx, dcsc1, dcsh1) = _rw_vjp(_f_norm_mod, [cs], [n1, csc1, csh1], [[dxn[L:]]], [False], [True] * 3, [],
                                         name="norm1_ctx_bwd")
    grad_x = _from_segments(dx_seg)[None]
    g_small.update(norm1=dn1_lat + dn1_ctx, norm2=dn2, q_norm=dqg, kv_norm=dkvg_lat + dkvg_ctx, norm_f=dnf.reshape(D))
    gw_in = jnp.concatenate([gw_a[:, :wa_used], gw_g], axis=1)
    small_vals = [g_small[nme] for nme in SMALL]
    n_small = sum(val.size for val in small_vals)
    small_rows = -(-n_small // (LANES * 4 * 32)) * 32

    zD = jnp.zeros((1, D), F32)
    dm = jnp.concatenate([
        jnp.concatenate([dsh1, dsc1, dg1, dsh2, dsc2, dg2], axis=1),
        jnp.concatenate([dcsh1, dcsc1, zD, zD, zD, zD], axis=1),
    ], axis=0)
    dm_all = _allgather8(dm.reshape(SUBLANES, -1), name="ag_dmod").reshape(8, 2, 6 * D)
    rs_in, tok = _rs_stage1([_col_blocks(gw_in, 4), _pack(small_vals, LANES, 4 * small_rows).reshape(4, small_rows, LANES)],
                            "in", after=[dm_all])
    dm_ctx = dm_all[0, 1] + tok[0, 0]
    for k in range(1, 8):
        dm_ctx = dm_ctx + dm_all[k, 1]
    dmod = _pad_rows(jnp.concatenate([dm_all[:, 0, :], dm_ctx[None]], axis=0), 16)
    g_b_mod = jnp.sum(dmod, axis=0, keepdims=True)
    dmod_mine = lax.dynamic_slice_in_dim(dmod, me_chip * cs_mod, cs_mod, axis=1)
    g_w_mod = _mm(act, dmod_mine, ta=True, out_dtype=F32, name="mod_dw")
    dact_part = _mm(dmod_mine, w_mod, tb=True, out_dtype=F32, name="mod_dx")
    dact_all = _allgather8(dact_part[8].reshape(SUBLANES, D // SUBLANES), name="ag_dact").reshape(8, D)
    dact = jnp.zeros((16, D), F32).at[8].set(dact_all[0] + dact_all[2] + dact_all[4] + dact_all[6])
    (dcond_rows,), _ = _rw_vjp(lambda t: (jax.nn.silu(t),), [cond], [], [[dact]], [True], [], [F32], name="cond_silu_bwd")
    g_c_ctx = dcond_rows[8]

    rs_in, tok = _rs_stage2(rs_in, [g_c_ctx], "in")

    grads, delta, new_m, new_v = {}, {}, {}, {}

    def update(members, reds, anchor):
        deltas = []
        for nme, red in zip(members, reds):
            res = _adamw(w[nme][0], red, m[nme][0], v[nme][0], name=f"adamw_{nme}", anchor=anchor)
            grads[nme], delta[nme], new_m[nme], new_v[nme] = (r.reshape(w[nme].shape) for r in res)
            deltas.append(res[1])
            anchor = None
        return deltas

    rs_ffn, tok = _rs_stage3(rs_ffn, [tok], "ffn")
    done = update(['w_mod'], [g_w_mod], tok)
    red_ffn = _rs_stage4(rs_ffn, done, "ffn")
    rs_mid, tok = _rs_stage3(rs_mid, red_ffn[:1], "mid")
    done = update(['w_ffn_out', 'w_ffn_in'], red_ffn, tok)
    red_mid = _rs_stage4(rs_mid, done, "mid")
    rs_kv, tok = _rs_stage3(rs_kv, red_mid[:1], "kv")
    done = update(['w_out', 'w_glu', 'w_mla_o'], red_mid, tok)
    red_kv = _rs_stage4(rs_kv, done, "kv")
    rs_in, tok = _rs_stage3(rs_in, red_kv[:1], "in")
    done = update(['w_uq', 'w_ukv'], red_kv, tok)
    red_in = _rs_stage4(rs_in, done, "in")
    update(['w_in'], red_in[:1], None)
    small_mine = red_in[-1]
    small_buf = _into_slot(small_mine, me_chip, 4, F32, name="small_grads_slot")
    small_all = _allgather_shards([small_buf], name="ag_small_grads")[0].reshape(4 * small_rows, LANES)
    g_small_red = dict(zip(SMALL, _unpack(small_all, [w[nme] for nme in SMALL])))
    rest = SMALL + ['c_ctx', 'b_mod']
    g_rest = dict(g_small_red, c_ctx=g_c_ctx, b_mod=g_b_mod)
    rows_rest = -(-sum(w[nme].size for nme in rest) // (LANES * 16)) * 16
    packed = [_pack([src[nme] for nme in rest], LANES, rows_rest) for src in (w, g_rest, m, v)]
    res = _adamw(*packed, name="adamw_small")
    for dst, buf in zip((grads, delta, new_m, new_v), res):
        dst.update(zip(rest, _unpack(buf, [w[nme] for nme in rest])))
    return (loss, grad_x, *[grads[nme] for nme in WEIGHTS], *[delta[nme] for nme in WEIGHTS],
            *[new_m[nme] for nme in WEIGHTS], *[new_v[nme] for nme in WEIGHTS])


def kernel(x, c, ctx, c_ctx, w_mod, b_mod, norm1, norm2, w_in, s5_a_re, s5_a_im, s5_log_dt, s5_b_re, s5_b_im, s5_c_re, s5_c_im, s5_d, w_glu, q_norm, kv_norm, w_uq, w_ukv, w_mla_o, w_out, w_ffn_in, w_ffn_out, norm_f, loss_target, m_c_ctx, m_w_mod, m_b_mod, m_norm1, m_norm2, m_w_in, m_s5_a_re, m_s5_a_im, m_s5_log_dt, m_s5_b_re, m_s5_b_im, m_s5_c_re, m_s5_c_im, m_s5_d, m_w_glu, m_q_norm, m_kv_norm, m_w_uq, m_w_ukv, m_w_mla_o, m_w_out, m_w_ffn_in, m_w_ffn_out, m_norm_f, v_c_ctx, v_w_mod, v_b_mod, v_norm1, v_norm2, v_w_in, v_s5_a_re, v_s5_a_im, v_s5_log_dt, v_s5_b_re, v_s5_b_im, v_s5_c_re, v_s5_c_im, v_s5_d, v_w_glu, v_q_norm, v_kv_norm, v_w_uq, v_w_ukv, v_w_mla_o, v_w_out, v_w_ffn_in, v_w_ffn_out, v_norm_f):
    w = dict(c_ctx=c_ctx, w_mod=w_mod, b_mod=b_mod, norm1=norm1, norm2=norm2, w_in=w_in, s5_a_re=s5_a_re, s5_a_im=s5_a_im,
             s5_log_dt=s5_log_dt, s5_b_re=s5_b_re, s5_b_im=s5_b_im, s5_c_re=s5_c_re, s5_c_im=s5_c_im, s5_d=s5_d, w_glu=w_glu,
             q_norm=q_norm, kv_norm=kv_norm, w_uq=w_uq, w_ukv=w_ukv, w_mla_o=w_mla_o, w_out=w_out, w_ffn_in=w_ffn_in,
             w_ffn_out=w_ffn_out, norm_f=norm_f)
    m = dict(c_ctx=m_c_ctx, w_mod=m_w_mod, b_mod=m_b_mod, norm1=m_norm1, norm2=m_norm2, w_in=m_w_in, s5_a_re=m_s5_a_re,
             s5_a_im=m_s5_a_im, s5_log_dt=m_s5_log_dt, s5_b_re=m_s5_b_re, s5_b_im=m_s5_b_im, s5_c_re=m_s5_c_re,
             s5_c_im=m_s5_c_im, s5_d=m_s5_d, w_glu=m_w_glu, q_norm=m_q_norm, kv_norm=m_kv_norm, w_uq=m_w_uq, w_ukv=m_w_ukv,
             w_mla_o=m_w_mla_o, w_out=m_w_out, w_ffn_in=m_w_ffn_in, w_ffn_out=m_w_ffn_out, norm_f=m_norm_f)
    v = dict(c_ctx=v_c_ctx, w_mod=v_w_mod, b_mod=v_b_mod, norm1=v_norm1, norm2=v_norm2, w_in=v_w_in, s5_a_re=v_s5_a_re,
             s5_a_im=v_s5_a_im, s5_log_dt=v_s5_log_dt, s5_b_re=v_s5_b_re, s5_b_im=v_s5_b_im, s5_c_re=v_s5_c_re,
             s5_c_im=v_s5_c_im, s5_d=v_s5_d, w_glu=v_w_glu, q_norm=v_q_norm, kv_norm=v_kv_norm, w_uq=v_w_uq, w_ukv=v_w_ukv,
             w_mla_o=v_w_mla_o, w_out=v_w_out, w_ffn_in=v_w_ffn_in, w_ffn_out=v_w_ffn_out, norm_f=v_norm_f)
    return _step(x, c, ctx, loss_target, w, m, v)
```

```python
import functools
import math

import jax
import jax.numpy as jnp
from jax import lax
from jax.experimental import pallas as pl
from jax.experimental.pallas import tpu as pltpu

F32 = jnp.float32
BF16 = jnp.bfloat16

EPS = 1e-6
GRID_W = 64
S5_GROUP = 16
S5_STATE = 64
MLA_HEADS = 8
QK_NOPE = 128
QK_ROPE = 64
V_DIM = 128
ROPE_BASE = 10000.0
ATTN_SCALE = (QK_NOPE + QK_ROPE) ** -0.5
ADAM_LR = 0.001
ADAM_B1 = 0.9
ADAM_B2 = 0.999
ADAM_EPS = 1e-08
ADAM_WD = 0.01
ADAM_STEP = 10

SUBLANES = 8
LANES = 128
V7X_VMEM_BYTES = 64 * 1024 * 1024
VMEM_LIMIT = (V7X_VMEM_BYTES * 7) // 8
N_SEG = 2 * SUBLANES
S5_BLOCK_GROUPS = 8
MESH = pl.DeviceIdType.MESH


def _pick(n, target, mult):
    best = None
    d = mult
    while d <= min(n, target):
        if n % d == 0:
            best = d
        d += mult
    return n if best is None else best


def _cparams(sem=None):
    return pltpu.CompilerParams(dimension_semantics=sem, vmem_limit_bytes=VMEM_LIMIT)


MM_VMEM_BUDGET = (V7X_VMEM_BYTES * 5) // 8


def _mm(a, b, *, ta=False, tb=False, out_dtype=F32, name, a_shards=1, b_shards=1, out_shards=1):
    if ta:
        K, M = a.shape
    else:
        M, K = a.shape[-2], a.shape[-1] * a_shards
    if tb:
        N, K2 = b.shape[-2], b.shape[-1] * b_shards
    else:
        K2, N = b.shape[-2], b.shape[-1] * b_shards
    assert K == K2, (a.shape, b.shape, ta, tb)
    n_unit = N // max(out_shards, 1 if tb else b_shards)
    k_unit = K // max(a_shards, b_shards if tb else 1)
    tn = _pick(n_unit, 1024, LANES)
    tm = _pick(M, 1024 if tn >= 512 else 2048, LANES if ta else 16)
    sa, sb, so = a.dtype.itemsize, b.dtype.itemsize, jnp.dtype(out_dtype).itemsize
    k_mult = LANES if (not ta or tb) else 16
    tk = k_mult if k_unit % k_mult == 0 else k_unit
    for cand in range(k_mult, k_unit + 1, k_mult):
        if k_unit % cand == 0 and 2 * cand * (tm * sa + tn * sb) + tm * tn * (4 + 2 * so) <= MM_VMEM_BUDGET:
            tk = cand
    nk = K // tk
    dims = (((0 if ta else 1,), (1 if tb else 0,)), ((), ()))

    def body(a_ref, b_ref, o_ref, *scratch):
        part = lax.dot_general(a_ref[...].astype(BF16), b_ref[...].astype(BF16), dims, preferred_element_type=F32)
        if nk == 1:
            o_ref[...] = part.astype(o_ref.dtype)
            return
        acc_ref, = scratch
        k = pl.program_id(2)

        @pl.when(k == 0)
        def _():
            acc_ref[...] = part

        @pl.when(k > 0)
        def _():
            acc_ref[...] += part

        @pl.when(k == nk - 1)
        def _():
            o_ref[...] = acc_ref[...].astype(o_ref.dtype)

    if ta:
        a_spec = pl.BlockSpec((tk, tm), lambda i, j, k: (k, i))
    elif a_shards == 1:
        a_spec = pl.BlockSpec((tm, tk), lambda i, j, k: (i, k))
    else:
        akb = (K // a_shards) // tk
        a_spec = pl.BlockSpec((None, tm, tk), lambda i, j, k: (k // akb, i, k % akb))
    if b_shards == 1:
        b_spec = pl.BlockSpec((tn, tk), lambda i, j, k: (j, k)) if tb else pl.BlockSpec((tk, tn), lambda i, j, k: (k, j))
    elif tb:
        kpb = (K // b_shards) // tk
        b_spec = pl.BlockSpec((None, tn, tk), lambda i, j, k: (k // kpb, j, k % kpb))
    else:
        npb = (N // b_shards) // tn
        b_spec = pl.BlockSpec((None, tk, tn), lambda i, j, k: (j // npb, k, j % npb))
    if out_shards == 1:
        out_spec = pl.BlockSpec((tm, tn), lambda i, j, k: (i, j))
        out_shape = jax.ShapeDtypeStruct((M, N), out_dtype)
    else:
        opb = (N // out_shards) // tn
        out_spec = pl.BlockSpec((None, tm, tn), lambda i, j, k: (j // opb, i, j % opb))
        out_shape = jax.ShapeDtypeStruct((out_shards, M, N // out_shards), out_dtype)
    return pl.pallas_call(
        body, name=name, grid=(M // tm, N // tn, nk),
        in_specs=[a_spec, b_spec], out_specs=out_spec, out_shape=out_shape,
        scratch_shapes=[pltpu.VMEM((tm, tn), F32)] if nk > 1 else [],
        compiler_params=_cparams(("parallel", "parallel", "arbitrary")),
    )(a, b)


FFN_TILE_ROWS = 2048


def _ffn_in_swiglu(x, w4, *, name):
    M, K = x.shape
    S, _, ns = w4.shape
    half = S * ns // 2
    tn = _pick(ns, 512, LANES)
    tm = _pick(M, FFN_TILE_ROWS, 16)
    npb = ns // tn

    def body(x_ref, wa_ref, wb_ref, h_ref, ab_ref):
        xb = x_ref[...].astype(BF16)
        a = jnp.dot(xb, wa_ref[...].astype(BF16), preferred_element_type=F32)
        b = jnp.dot(xb, wb_ref[...].astype(BF16), preferred_element_type=F32)
        h_ref[...] = (jax.nn.silu(a) * b).astype(h_ref.dtype)
        ab_ref[0] = a.astype(ab_ref.dtype)
        ab_ref[1] = b.astype(ab_ref.dtype)

    return pl.pallas_call(
        body, name=name, grid=(M // tm, half // tn),
        in_specs=[pl.BlockSpec((tm, K), lambda i, j: (i, 0)),
                  pl.BlockSpec((None, K, tn), lambda i, j: (j // npb, 0, j % npb)),
                  pl.BlockSpec((None, K, tn), lambda i, j: (S // 2 + j // npb, 0, j % npb))],
        out_specs=[pl.BlockSpec((tm, tn), lambda i, j: (i, j)), pl.BlockSpec((2, tm, tn), lambda i, j: (0, i, j))],
        out_shape=[jax.ShapeDtypeStruct((M, half), BF16), jax.ShapeDtypeStruct((2, M, half), BF16)],
        compiler_params=_cparams(("parallel", "parallel")),
    )(x, w4, w4)


def _ffn_out_dx_swiglu(dy, w, ab, *, name):
    M, D = dy.shape
    n2 = w.shape[0]
    tn = _pick(n2, 512, LANES)
    tm = _pick(M, FFN_TILE_ROWS // 2, 16)

    def body(dy_ref, w_ref, ab_ref, o_ref):
        dh = lax.dot_general(dy_ref[...].astype(BF16), w_ref[...].astype(BF16), NT_DIMS, preferred_element_type=F32)
        a, b = ab_ref[0].astype(F32), ab_ref[1].astype(F32)
        s = jax.nn.sigmoid(a)
        o_ref[0] = (dh * b * (s * (1.0 + a * (1.0 - s)))).astype(o_ref.dtype)
        o_ref[1] = (dh * (a * s)).astype(o_ref.dtype)

    return pl.pallas_call(
        body, name=name, grid=(M // tm, n2 // tn),
        in_specs=[pl.BlockSpec((tm, D), lambda i, j: (i, 0)), pl.BlockSpec((tn, D), lambda i, j: (j, 0)),
                  pl.BlockSpec((2, tm, tn), lambda i, j: (0, i, j))],
        out_specs=pl.BlockSpec((2, tm, tn), lambda i, j: (0, i, j)),
        out_shape=jax.ShapeDtypeStruct((2, M, n2), BF16),
        compiler_params=_cparams(("parallel", "parallel")),
    )(dy, w, ab)


ROW_TILE_BYTES = 6 * 1024 * 1024
STREAM_TILE_BYTES = 14 * 1024 * 1024


def _row_tile(tiled, extra_bytes=0, budget=ROW_TILE_BYTES):
    rows = tiled[0].shape[0]
    per_row = sum(a.shape[1] * 4 for a in tiled) + extra_bytes
    target = max(SUBLANES, budget // max(per_row, 1))
    return _pick(rows, min(target, 512), 16)


def _rw(f, tiled, bcast, out_dtypes, *, name, anchor=None, tile_bytes=ROW_TILE_BYTES):
    nt, nb = len(tiled), len(bcast)
    rows = tiled[0].shape[0]
    outs_aval = jax.eval_shape(f, *[jax.ShapeDtypeStruct((16, a.shape[1]), F32) for a in tiled],
                               *[jax.ShapeDtypeStruct(b.shape, F32) for b in bcast])
    widths = [o.shape[1] for o in outs_aval]
    tm = _row_tile(tiled, sum(w * 4 for w in widths), tile_bytes)

    extra = [] if anchor is None else [anchor]
    n_in = nt + nb + len(extra)

    def body(*refs):
        tin = [r[...].astype(F32) for r in refs[:nt]]
        bin_ = [r[...].astype(F32) for r in refs[nt:nt + nb]]
        outs = f(*tin, *bin_)
        for o_ref, o in zip(refs[n_in:], outs):
            o_ref[...] = o.astype(o_ref.dtype)

    in_specs = [pl.BlockSpec((tm, a.shape[1]), lambda i: (i, 0)) for a in tiled]
    in_specs += [pl.BlockSpec(b.shape, lambda i: (0, 0)) for b in bcast + extra]
    res = pl.pallas_call(
        body, name=name, grid=(rows // tm,), in_specs=in_specs,
        out_specs=[pl.BlockSpec((tm, w), lambda i: (i, 0)) for w in widths],
        out_shape=[jax.ShapeDtypeStruct((rows, w), dt) for w, dt in zip(widths, out_dtypes)],
        compiler_params=_cparams(("parallel",)),
    )(*tiled, *bcast, *extra)
    return list(res)


def _rw_vjp(f, tiled, bcast, cts, need_t, need_b, t_dtypes, *, name, anchor=None):
    nt, nb = len(tiled), len(bcast)
    rows = tiled[0].shape[0]
    flat_cts = [c for group in cts for c in group]
    t_idx = [i for i in range(nt) if need_t[i]]
    b_idx = [i for i in range(nb) if need_b[i]]
    tm = _row_tile(list(tiled) + flat_cts, sum(tiled[i].shape[1] * 4 for i in t_idx))
    nc = len(flat_cts)
    extra = [] if anchor is None else [anchor]

    def body(*refs):
        i = pl.program_id(0)
        tin = [r[...].astype(F32) for r in refs[:nt]]
        bin_ = [r[...].astype(F32) for r in refs[nt:nt + nb]]
        ct_refs = refs[nt + nb:nt + nb + nc]
        out_refs = refs[nt + nb + nc + len(extra):]
        outs, vjp_fn = jax.vjp(f, *tin, *bin_)
        ct_vals, pos = [], 0
        for o, group in zip(outs, cts):
            acc = jnp.zeros_like(o)
            for _ in group:
                acc = acc + ct_refs[pos][...].astype(F32)
                pos += 1
            ct_vals.append(acc)
        grads = vjp_fn(tuple(ct_vals))
        for o_ref, k in zip(out_refs[:len(t_idx)], t_idx):
            o_ref[...] = grads[k].astype(o_ref.dtype)
        for o_ref, k in zip(out_refs[len(t_idx):], b_idx):
            @pl.when(i == 0)
            def _(o_ref=o_ref):
                o_ref[...] = jnp.zeros_like(o_ref)

            o_ref[...] += grads[nt + k]

    in_specs = [pl.BlockSpec((tm, a.shape[1]), lambda i: (i, 0)) for a in tiled]
    in_specs += [pl.BlockSpec(b.shape, lambda i: (0, 0)) for b in bcast]
    in_specs += [pl.BlockSpec((tm, c.shape[1]), lambda i: (i, 0)) for c in flat_cts]
    in_specs += [pl.BlockSpec(e.shape, lambda i: (0, 0)) for e in extra]
    out_specs = [pl.BlockSpec((tm, tiled[k].shape[1]), lambda i: (i, 0)) for k in t_idx]
    out_specs += [pl.BlockSpec(bcast[k].shape, lambda i: (0, 0)) for k in b_idx]
    out_shape = [jax.ShapeDtypeStruct(tiled[k].shape, dt) for k, dt in zip(t_idx, t_dtypes)]
    out_shape += [jax.ShapeDtypeStruct(bcast[k].shape, F32) for k in b_idx]
    res = pl.pallas_call(
        body, name=name, grid=(rows // tm,), in_specs=in_specs, out_specs=out_specs, out_shape=out_shape,
        compiler_params=_cparams(("arbitrary",)),
    )(*tiled, *bcast, *flat_cts, *extra)
    res = list(res)
    return res[:len(t_idx)], res[len(t_idx):]


def _rms(x, g):
    return x * lax.rsqrt(jnp.mean(x * x, axis=-1, keepdims=True) + EPS) * g


def _f_norm_mod(x, g, sc, sh):
    return (_rms(x, g) * (1.0 + sc) + sh,)


def _f_norm_mod_keep(x, g, sc, sh):
    return (_rms(x, g) * (1.0 + sc) + sh, x)


@jax.custom_vjp
def _swap16(x):
    w = x.shape[-1]
    lane = lax.broadcasted_iota(jnp.int32, x.shape, x.ndim - 1)
    return jnp.where((lane & 16) == 0, pltpu.roll(x, w - 16, x.ndim - 1), pltpu.roll(x, 16, x.ndim - 1))


_swap16.defvjp(lambda x: (_swap16(x), None), lambda _, g: (_swap16(g),))


def _rope(x, cos, sin):
    return x * cos + _swap16(x) * sin


def _make_f_post_in(sw, q_rank, kv_rank, with_q):
    o1, o2, o3 = sw, sw + q_rank, sw + q_rank + kv_rank

    if with_q:
        def f(ha, cos, sin, qg, kvg):
            u = ha[:, :o1]
            cqn = _rms(ha[:, o1:o2], qg)
            ckvn = _rms(ha[:, o2:o3], kvg)
            kr = _rope(ha[:, o3:o3 + LANES], cos, sin)
            return u, cqn, ckvn, kr
    else:
        def f(ha, kvg):
            return ha[:, :o1], _rms(ha[:, o2:o3], kvg), ha[:, o3:o3 + LANES]
    return f


def _f_qpost(q2, cos, sin):
    parts = []
    for h in range(q2.shape[1] // (2 * LANES)):
        o = 2 * LANES * h
        parts += [q2[:, o:o + LANES], _rope(q2[:, o + LANES:o + 2 * LANES], cos, sin)]
    return (jnp.concatenate(parts, axis=1),)


def _f_s5post(u, r0, r1, d):
    return (jax.nn.gelu(d * u + r0 + r1, approximate=True),)


def _f_merge(ab, bm, gt):
    d = bm.shape[1]
    br_s5 = ab[:, :d] * jax.nn.sigmoid(ab[:, d:])
    g = jax.nn.sigmoid(gt)
    return (g[:, :d] * br_s5 + g[:, d:] * bm,)


def _f_resid_norm(x, out, g1, n2, sc2, sh2):
    x1 = x + g1 * out
    return x1, _rms(x1, n2) * (1.0 + sc2) + sh2


def _f_final(x1, f, tgt, g2, nf):
    y = _rms(x1 + g2 * f, nf)
    return (0.5 * jnp.mean(jnp.square(y - tgt), axis=-1, keepdims=True),)


def _bd_fanin(xs, ws, *, name):
    nw = len(ws)
    nb, kb, nn = ws[0].shape
    T = xs[0].shape[0]
    tm = _pick(T, 512, 16)

    def body(*refs):
        acc = None
        for x_ref, w_ref in zip(refs[:nw], refs[nw:2 * nw]):
            t = jnp.dot(x_ref[...].astype(BF16), w_ref[0].astype(BF16), preferred_element_type=F32)
            acc = t if acc is None else acc + t
        refs[2 * nw][...] = acc

    return pl.pallas_call(
        body, name=name, grid=(nb, T // tm),
        in_specs=[pl.BlockSpec((tm, kb), lambda j, i: (i, j))] * nw + [pl.BlockSpec((1, kb, nn), lambda j, i: (j, 0, 0))] * nw,
        out_specs=pl.BlockSpec((tm, nn), lambda j, i: (i, j)),
        out_shape=jax.ShapeDtypeStruct((T, nb * nn), F32),
        compiler_params=_cparams(("parallel", "parallel")),
    )(*xs, *ws)


def _bd_dw(xs, dys, nb, *, name):
    npair = len(xs)
    T = xs[0].shape[0]
    kb = xs[0].shape[1] // nb
    nn = dys[0].shape[1] // nb
    tm = _pick(T, 512, 16)
    dims = (((0,), (0,)), ((), ()))

    def body(*refs):
        i = pl.program_id(1)
        for x_ref, d_ref, o_ref in zip(refs[:npair], refs[npair:2 * npair], refs[2 * npair:]):
            @pl.when(i == 0)
            def _(o_ref=o_ref):
                o_ref[...] = jnp.zeros_like(o_ref)

            o_ref[0] += lax.dot_general(x_ref[...].astype(BF16), d_ref[...].astype(BF16), dims,
                                        preferred_element_type=F32)

    return list(pl.pallas_call(
        body, name=name, grid=(nb, T // tm),
        in_specs=[pl.BlockSpec((tm, kb), lambda j, i: (i, j))] * npair + [pl.BlockSpec((tm, nn), lambda j, i: (i, j))] * npair,
        out_specs=[pl.BlockSpec((1, kb, nn), lambda j, i: (j, 0, 0))] * npair,
        out_shape=[jax.ShapeDtypeStruct((nb, kb, nn), F32)] * npair,
        compiler_params=_cparams(("parallel", "arbitrary")),
    )(*xs, *dys))


def _cmul(ar, ai, br, bi):
    return ar * br - ai * bi, ar * bi + ai * br


def _cpow(lr, li, n):
    rr, ri = None, None
    br, bi = lr, li
    while n:
        if n & 1:
            rr, ri = (br, bi) if rr is None else _cmul(rr, ri, br, bi)
        n >>= 1
        if n:
            br, bi = _cmul(br, bi, br, bi)
    return rr, ri


SCAN_MM_ROWS = 512


def _s5_scan(x, w_re, w_im, lam_re, lam_im, h0_re, h0_im, e0_re, e0_im, *, reverse, name, readout=None):
    rows = x.shape[0]
    nb, kb, cb = w_re.shape
    C = nb * cb
    n = rows // N_SEG
    mm_rows = _pick(rows, SCAN_MM_ROWS, 16)
    seg_order = list(range(N_SEG))[::-1] if reverse else list(range(N_SEG))
    s_first, s_last = seg_order[0], seg_order[-1]
    n_ro = 0 if readout is None else 2

    def body(x_ref, wr_ref, wi_ref, lr_ref, li_ref, h0r_ref, h0i_ref, e0r_ref, e0i_ref, *rest):
        ro_refs, (hr_ref, hi_ref, htr_ref, hti_ref), y_refs = rest[:n_ro], rest[n_ro:n_ro + 4], rest[n_ro + 4:-2]
        locr_ref, loci_ref = rest[-2:]
        shape = (N_SEG, cb)
        lr = jnp.broadcast_to(lr_ref[...], shape)
        li = jnp.broadcast_to(li_ref[...], shape)
        row = lax.broadcasted_iota(jnp.int32, shape, 0)

        def step_of(k):
            return (n - 1 - k) if reverse else k

        def rows_of(k):
            return pl.ds(pl.multiple_of(step_of(k) * N_SEG, N_SEG), N_SEG)

        wr, wi = wr_ref[...].astype(BF16), wi_ref[...].astype(BF16)
        for r0 in range(0, rows, mm_rows):
            xb = x_ref[r0:r0 + mm_rows, :].astype(BF16)
            locr_ref[r0:r0 + mm_rows, :] = jnp.dot(xb, wr, preferred_element_type=F32)
            loci_ref[r0:r0 + mm_rows, :] = jnp.dot(xb, wi, preferred_element_type=F32)

        first = row == s_first
        hr = locr_ref[rows_of(0), :] + jnp.where(first, e0r_ref[...], 0.0)
        hi = loci_ref[rows_of(0), :] + jnp.where(first, e0i_ref[...], 0.0)
        locr_ref[rows_of(0), :] = hr
        loci_ref[rows_of(0), :] = hi

        def pass1(k, carry):
            hr, hi = carry
            pr, pi = _cmul(lr, li, hr, hi)
            hr = pr + locr_ref[rows_of(k), :]
            hi = pi + loci_ref[rows_of(k), :]
            locr_ref[rows_of(k), :] = hr
            loci_ref[rows_of(k), :] = hi
            return hr, hi

        er, ei = lax.fori_loop(1, n, pass1, (hr, hi))

        lnr, lni = _cpow(lr_ref[...], li_ref[...], n)
        cr, ci = h0r_ref[...], h0i_ref[...]
        cin_r = jnp.zeros(shape, F32)
        cin_i = jnp.zeros(shape, F32)
        for s in seg_order:
            cin_r = jnp.where(row == s, cr, cin_r)
            cin_i = jnp.where(row == s, ci, cin_i)
            if s != s_last:
                pr, pi = _cmul(lnr, lni, cr, ci)
                cr = pr + jnp.sum(jnp.where(row == s, er, 0.0), axis=0, keepdims=True)
                ci = pi + jnp.sum(jnp.where(row == s, ei, 0.0), axis=0, keepdims=True)

        def pass2(k, carry):
            pr, pi, _, _ = carry
            ar, ai = _cmul(pr, pi, cin_r, cin_i)
            hr = locr_ref[rows_of(k), :] + ar
            hi = loci_ref[rows_of(k), :] + ai
            hr_ref[rows_of(k), :] = hr.astype(hr_ref.dtype)
            hi_ref[rows_of(k), :] = hi.astype(hi_ref.dtype)
            npr, npi = _cmul(pr, pi, lr, li)
            return npr, npi, hr, hi

        _, _, last_r, last_i = lax.fori_loop(0, n, pass2, (lr, li, er, ei))
        htr_ref[...] = jnp.sum(jnp.where(row == s_last, last_r, 0.0), axis=0, keepdims=True)
        hti_ref[...] = jnp.sum(jnp.where(row == s_last, last_i, 0.0), axis=0, keepdims=True)

        if readout is not None:
            cr, ci = ro_refs[0][...].astype(BF16), ro_refs[1][...].astype(BF16)
            for r0 in range(0, rows, mm_rows):
                y_refs[0][r0:r0 + mm_rows, :] = (
                    jnp.dot(hr_ref[r0:r0 + mm_rows, :].astype(BF16), cr, preferred_element_type=F32)
                    + jnp.dot(hi_ref[r0:r0 + mm_rows, :].astype(BF16), ci, preferred_element_type=F32))

    big = pl.BlockSpec((rows, cb), lambda j: (0, j))
    vec = pl.BlockSpec((1, cb), lambda j: (0, j))
    wspec = pl.BlockSpec((None, kb, cb), lambda j: (j, 0, 0))
    in_specs = [pl.BlockSpec((rows, kb), lambda j: (0, j)), wspec, wspec] + [vec] * 6
    out_specs = [big, big, vec, vec]
    out_shape = [jax.ShapeDtypeStruct((rows, C), BF16)] * 2 + [jax.ShapeDtypeStruct((1, C), F32)] * 2
    extra = []
    if readout is not None:
        pb = readout[0].shape[2]
        in_specs += [pl.BlockSpec((None, cb, pb), lambda j: (j, 0, 0))] * 2
        out_specs.append(pl.BlockSpec((rows, pb), lambda j: (0, j)))
        out_shape.append(jax.ShapeDtypeStruct((rows, nb * pb), F32))
        extra = list(readout)
    return pl.pallas_call(
        body, name=name, grid=(nb,), in_specs=in_specs, out_specs=out_specs, out_shape=out_shape,
        scratch_shapes=[pltpu.VMEM((rows, cb), F32)] * 2,
        compiler_params=_cparams(("parallel",)),
    )(x, w_re, w_im, lam_re, lam_im, h0_re, h0_im, e0_re, e0_im, *extra)


def _s5_dlam(mu_re, mu_im, h_re, h_im, h0_re, h0_im, *, reverse, name):
    rows, C = h_re.shape
    n = rows // N_SEG
    cb = _pick(C, 256, LANES)
    s_first = N_SEG - 1 if reverse else 0

    def body(mr_ref, mi_ref, hr_ref, hi_ref, h0r_ref, h0i_ref, dr_ref, di_ref):
        shape = (N_SEG, cb)
        row = lax.broadcasted_iota(jnp.int32, shape, 0)

        def rows_of(k):
            step = (n - 1 - k) if reverse else k
            return pl.ds(pl.multiple_of(step * N_SEG, N_SEG), N_SEG)

        def term(k, pr, pi):
            mr, mi = mr_ref[rows_of(k), :].astype(F32), mi_ref[rows_of(k), :].astype(F32)
            return mr * pr + mi * pi, mi * pr - mr * pi

        shift = N_SEG - 1 if reverse else 1
        pr = jnp.where(row == s_first, h0r_ref[...], pltpu.roll(hr_ref[rows_of(n - 1), :].astype(F32), shift, 0))
        pi = jnp.where(row == s_first, h0i_ref[...], pltpu.roll(hi_ref[rows_of(n - 1), :].astype(F32), shift, 0))
        acc = term(0, pr, pi)

        def loop(k, acc):
            tr, ti = term(k, hr_ref[rows_of(k - 1), :].astype(F32), hi_ref[rows_of(k - 1), :].astype(F32))
            return acc[0] + tr, acc[1] + ti

        ar, ai = lax.fori_loop(1, n, loop, acc)
        dr_ref[...] = jnp.sum(ar, axis=0, keepdims=True)
        di_ref[...] = jnp.sum(ai, axis=0, keepdims=True)

    big = pl.BlockSpec((rows, cb), lambda j: (0, j))
    vec = pl.BlockSpec((1, cb), lambda j: (0, j))
    return pl.pallas_call(
        body, name=name, grid=(C // cb,),
        in_specs=[big] * 4 + [vec] * 2, out_specs=[vec, vec],
        out_shape=[jax.ShapeDtypeStruct((1, C), F32)] * 2,
        compiler_params=_cparams(("parallel",)),
    )(mu_re, mu_im, h_re, h_im, h0_re, h0_im)


NT_DIMS = (((1,), (1,)), ((), ()))
TN_DIMS = (((0,), (0,)), ((), ()))


ATTN_Q_ROWS = 512


def _attn_exp(q, kvh, kr):
    s = (lax.dot_general(q[:, :LANES], kvh[:, :LANES], NT_DIMS, preferred_element_type=F32)
         + lax.dot_general(q[:, LANES:], kr, NT_DIMS, preferred_element_type=F32))
    e = jnp.exp2((s - jnp.max(s, axis=-1, keepdims=True)) * (ATTN_SCALE * math.log2(math.e)))
    return e, jnp.sum(e, axis=-1, keepdims=True)


def _attn_specs(L, T, tq):
    return [
        pl.BlockSpec((tq, 2 * LANES), lambda h, i: (i, h)),
        pl.BlockSpec((T, 2 * LANES), lambda h, i: (0, h)),
        pl.BlockSpec((T, LANES), lambda h, i: (0, 0)),
    ]


def _attn_fwd(qq, kv, kr, *, name):
    L, T = qq.shape[0], kv.shape[0]
    tq = _pick(L, ATTN_Q_ROWS // 2, 16)

    def body(q_ref, kv_ref, kr_ref, o_ref):
        kvh = kv_ref[...]
        e, l = _attn_exp(q_ref[...], kvh, kr_ref[...])
        o_ref[...] = (jnp.dot(e.astype(BF16), kvh[:, LANES:], preferred_element_type=F32) * (1.0 / l)).astype(o_ref.dtype)

    return pl.pallas_call(
        body, name=name, grid=(MLA_HEADS, L // tq), in_specs=_attn_specs(L, T, tq),
        out_specs=pl.BlockSpec((tq, LANES), lambda h, i: (i, h)),
        out_shape=jax.ShapeDtypeStruct((L, MLA_HEADS * V_DIM), BF16),
        compiler_params=_cparams(("parallel", "parallel")),
    )(qq, kv, kr)


def _attn_bwd(qq, kv, kr, do, *, name):
    L, T = qq.shape[0], kv.shape[0]
    H = MLA_HEADS
    tq = _pick(L, ATTN_Q_ROWS, 16)
    nq = L // tq

    def body(q_ref, kv_ref, kr_ref, do_ref, dq_ref, dkv_ref, dkr_ref, dkn_acc, dv_acc):
        h, i = pl.program_id(0), pl.program_id(1)
        q, kvh, krv, dov = q_ref[...], kv_ref[...], kr_ref[...], do_ref[...]
        e, l = _attn_exp(q, kvh, krv)
        inv = 1.0 / l
        ps = e * (inv * ATTN_SCALE)
        t = lax.dot_general(dov, kvh[:, LANES:], NT_DIMS, preferred_element_type=F32) * ps
        ds = (t - ps * (jnp.sum(t, axis=-1, keepdims=True) * (1.0 / ATTN_SCALE))).astype(BF16)
        dq_ref[:, :LANES] = jnp.dot(ds, kvh[:, :LANES], preferred_element_type=F32)
        dq_ref[:, LANES:] = jnp.dot(ds, krv, preferred_element_type=F32)

        @pl.when(i == 0)
        def _():
            dkn_acc[...] = jnp.zeros_like(dkn_acc)
            dv_acc[...] = jnp.zeros_like(dv_acc)

        @pl.when((i == 0) & (h == 0))
        def _():
            dkr_ref[...] = jnp.zeros_like(dkr_ref)

        dv_acc[...] += lax.dot_general(e.astype(BF16), (dov.astype(F32) * inv).astype(BF16), TN_DIMS,
                                       preferred_element_type=F32)
        dkn_acc[...] += lax.dot_general(ds, q[:, :LANES], TN_DIMS, preferred_element_type=F32)
        dkr_ref[...] += lax.dot_general(ds, q[:, LANES:], TN_DIMS, preferred_element_type=F32)

        @pl.when(i == nq - 1)
        def _():
            dkv_ref[:, :LANES] = dkn_acc[...].astype(dkv_ref.dtype)
            dkv_ref[:, LANES:] = dv_acc[...].astype(dkv_ref.dtype)

    in_specs = _attn_specs(L, T, tq) + [pl.BlockSpec((tq, LANES), lambda h, i: (i, h))]
    return pl.pallas_call(
        body, name=name, grid=(H, L // tq), in_specs=in_specs,
        out_specs=[pl.BlockSpec((tq, 2 * LANES), lambda h, i: (i, h)), pl.BlockSpec((T, 2 * LANES), lambda h, i: (0, h)),
                   pl.BlockSpec((T, LANES), lambda h, i: (0, 0))],
        out_shape=[jax.ShapeDtypeStruct((L, H * 2 * LANES), F32), jax.ShapeDtypeStruct((T, H * 2 * LANES), BF16),
                   jax.ShapeDtypeStruct((T, LANES), F32)],
        scratch_shapes=[pltpu.VMEM((T, LANES), F32), pltpu.VMEM((T, LANES), F32)],
        compiler_params=_cparams(("arbitrary", "arbitrary")),
    )(qq, kv, kr, do)


def _adamw(w, g, m, v, *, name, anchor=None):
    c1 = 1.0 - ADAM_B1 ** ADAM_STEP
    c2 = 1.0 - ADAM_B2 ** ADAM_STEP

    def f(w, g, m, v):
        m = ADAM_B1 * m + (1.0 - ADAM_B1) * g
        v = ADAM_B2 * v + (1.0 - ADAM_B2) * jnp.square(g)
        delta = -ADAM_LR * ((m / c1) / (jnp.sqrt(v / c2) + ADAM_EPS) + ADAM_WD * w)
        return g, delta, m, v

    return _rw(f, [w, g, m, v], [], [F32] * 4, name=name, anchor=anchor, tile_bytes=STREAM_TILE_BYTES)


def _slab_rows(rows, cols, n_arrays):
    return _pick(rows, max(16, (8 * 1024 * 1024) // (cols * 4 * n_arrays)), 16)


def _scalars(*vals):
    return jnp.stack([jnp.asarray(v, jnp.int32) for v in vals])


def _into_slot(src, slot, nslots, dtype, *, name):
    R, C = src.shape
    tr = _slab_rows(R, C, 2)

    def body(s_ref, x_ref, o_ref):
        o_ref[...] = x_ref[...].astype(o_ref.dtype)

    return pl.pallas_call(
        body, name=name,
        grid_spec=pltpu.PrefetchScalarGridSpec(
            num_scalar_prefetch=1, grid=(R // tr,),
            in_specs=[pl.BlockSpec((tr, C), lambda i, s: (i, 0))],
            out_specs=pl.BlockSpec((None, tr, C), lambda i, s: (s[0], i, 0))),
        out_shape=jax.ShapeDtypeStruct((nslots, R, C), dtype),
        compiler_params=_cparams(("arbitrary",)),
    )(_scalars(slot), src)


def _pair_sum(g, got, c, *, name):
    _, R, C = g.shape
    hr = R // 2
    tr = _slab_rows(hr, C, 3)
    nblk = hr // tr

    def body(s_ref, g_ref, r_ref, o_ref):
        o_ref[...] = (g_ref[...].astype(F32) + r_ref[...].astype(F32)).astype(o_ref.dtype)

    return pl.pallas_call(
        body, name=name,
        grid_spec=pltpu.PrefetchScalarGridSpec(
            num_scalar_prefetch=1, grid=(4, nblk),
            in_specs=[pl.BlockSpec((None, tr, C), lambda j, i, s: (j, s[0] * nblk + i, 0)),
                      pl.BlockSpec((None, tr, C), lambda j, i, s: (j, i, 0))],
            out_specs=pl.BlockSpec((None, tr, C), lambda j, i, s: (j, i, 0))),
        out_shape=jax.ShapeDtypeStruct((4, hr, C), g.dtype),
        compiler_params=_cparams(("arbitrary", "arbitrary")),
    )(_scalars(c), g, got)


def _chip_sum(p, landed, me_chip, c, *, name):
    _, hr, C = p.shape
    tr = _slab_rows(hr, C, 5)

    def body(s_ref, p_ref, l0_ref, l1_ref, l2_ref, o_ref):
        o_ref[...] = ((p_ref[...].astype(F32) + l0_ref[...].astype(F32)) + l1_ref[...].astype(F32)) + l2_ref[...].astype(F32)

    return pl.pallas_call(
        body, name=name,
        grid_spec=pltpu.PrefetchScalarGridSpec(
            num_scalar_prefetch=1, grid=(hr // tr,),
            in_specs=[pl.BlockSpec((None, tr, C), lambda i, s: (s[0], i, 0))]
            + [pl.BlockSpec((None, tr, C), functools.partial(lambda i, s, k: (k, i, 0), k=k)) for k in range(3)],
            out_specs=pl.BlockSpec((None, tr, C), lambda i, s: (s[1], i, 0))),
        out_shape=jax.ShapeDtypeStruct((2, hr, C), F32),
        compiler_params=_cparams(("arbitrary",)),
    )(_scalars(me_chip, c), p, landed, landed, landed)


def _place():
    return lax.axis_index("x"), lax.axis_index("y"), lax.axis_index("c")


def _other_chips(x, y):
    chips = [(1 - x, y), (x, 1 - y), (1 - x, 1 - y)]
    return chips, [2 * cx + cy for cx, cy in chips]


HBM = pl.BlockSpec(memory_space=pl.ANY)


def _allgather8(v, *, name):
    rows, cols = v.shape

    def body(v_ref, out_ref, send_sems, recv_sems):
        x, y, c = _place()
        me = 4 * x + 2 * y + c
        out_ref[me] = v_ref[...]
        copies = []
        for k in range(1, 8):
            bx, by, bc = (k >> 2) & 1, (k >> 1) & 1, k & 1
            px, py, pc = x ^ bx, y ^ by, c ^ bc
            cp = pltpu.make_async_remote_copy(
                src_ref=v_ref, dst_ref=out_ref.at[me], send_sem=send_sems.at[k - 1], recv_sem=recv_sems.at[k - 1],
                device_id=(px, py, pc), device_id_type=MESH)
            cp.start()
            copies.append((cp, 4 * px + 2 * py + pc))
        for k, (cp, peer) in enumerate(copies):
            pltpu.make_async_remote_copy(
                src_ref=v_ref, dst_ref=out_ref.at[peer], send_sem=send_sems.at[k], recv_sem=recv_sems.at[k],
                device_id=(x, y, c), device_id_type=MESH).wait_recv()
        for cp, _ in copies:
            cp.wait_send()

    return pl.pallas_call(
        body, name=name, out_shape=jax.ShapeDtypeStruct((8, rows, cols), v.dtype),
        in_specs=[pl.BlockSpec(memory_space=pltpu.VMEM)], out_specs=pl.BlockSpec(memory_space=pltpu.VMEM),
        scratch_shapes=[pltpu.SemaphoreType.DMA((7,)), pltpu.SemaphoreType.DMA((7,))],
        compiler_params=pltpu.CompilerParams(vmem_limit_bytes=VMEM_LIMIT),
    )(v)


def _allgather_shards(bufs, *, name):
    n = len(bufs)

    def body(*refs):
        outs = refs[n:2 * n]
        send_sems, recv_sems = refs[2 * n:]
        x, y, c = _place()
        me_chip = 2 * x + y
        sibling = (x, y, 1 - c)
        chips, chip_ids = _other_chips(x, y)

        def remote(k, j, blk, hf, to):
            hr = bufs[k].shape[1] // 2
            piece = outs[k].at[blk, pl.ds(pl.multiple_of(hf * hr, 16), hr), :]
            return pltpu.make_async_remote_copy(
                src_ref=piece, dst_ref=piece, send_sem=send_sems.at[6 * k + j], recv_sem=recv_sems.at[6 * k + j],
                device_id=to, device_id_type=MESH)

        sends = []
        for k in range(n):
            for j, chip in enumerate(chips):
                cp = remote(k, j, me_chip, c, (*chip, c))
                cp.start()
                sends.append(cp)
        for k in range(n):
            for j, chip in enumerate(chips):
                remote(k, j, chip_ids[j], c, (x, y, c)).wait_recv()
                cp = remote(k, 3 + j, chip_ids[j], c, sibling)
                cp.start()
                sends.append(cp)
        for k in range(n):
            for j in range(3):
                remote(k, 3 + j, chip_ids[j], 1 - c, (x, y, c)).wait_recv()
        for cp in sends:
            cp.wait_send()

    return list(pl.pallas_call(
        body, name=name, out_shape=[jax.ShapeDtypeStruct(b.shape, b.dtype) for b in bufs],
        in_specs=[HBM] * n, out_specs=[HBM] * n, input_output_aliases={k: k for k in range(n)},
        scratch_shapes=[pltpu.SemaphoreType.DMA((6 * n,)), pltpu.SemaphoreType.DMA((6 * n,))],
    )(*bufs))


HBM_SPEC = pl.BlockSpec(memory_space=pltpu.HBM)
SEM_SPEC = pl.BlockSpec(memory_space=pltpu.SEMAPHORE)
EFFECT = pltpu.SideEffectType.DATAFLOW_SIDE_EFFECTING
TOKEN = jax.ShapeDtypeStruct((SUBLANES, LANES), F32)


def _in_hbm(a):
    return pltpu.with_memory_space_constraint(a, pltpu.HBM)


def _half_rows(buf, hf):
    hr = buf.shape[1] // 2
    return pl.ds(pl.multiple_of(hf * hr, 16), hr)


def _plan_ag_ici(refs):
    x, y, c = _place()
    chips, ids = _other_chips(x, y)
    out = []
    for r in refs:
        mine = r.at[2 * x + y, _half_rows(r, c), :]
        out += [(mine, mine, r.at[ids[j], _half_rows(r, c), :], (*chip, c)) for j, chip in enumerate(chips)]
    return out


def _plan_ag_pair(refs):
    x, y, c = _place()
    _, ids = _other_chips(x, y)
    out = []
    for r in refs:
        for j in range(3):
            piece = r.at[ids[j], _half_rows(r, c), :]
            out.append((piece, piece, r.at[ids[j], _half_rows(r, 1 - c), :], (x, y, 1 - c)))
    return out


def _plan_rs_ici(refs):
    x, y, c = _place()
    chips, ids = _other_chips(x, y)
    n = len(refs) // 2
    return [(refs[k].at[ids[j]], refs[n + k].at[j], refs[n + k].at[j], (*chip, c))
            for k in range(n) for j, chip in enumerate(chips)]


def _plan_pair_exchange(refs):
    x, y, c = _place()
    n = len(refs) // 2
    return [(refs[k].at[:, _half_rows(refs[k], 1 - c), :], refs[n + k], refs[n + k], (x, y, 1 - c)) for k in range(n)]


def _plan_pair_gather(refs):
    x, y, c = _place()
    return [(r.at[c], r.at[c], r.at[1 - c], (x, y, 1 - c)) for r in refs]


def _remote(src, dst, send_sem, recv_sem, target):
    return pltpu.make_async_remote_copy(src_ref=src, dst_ref=dst, send_sem=send_sem, recv_sem=recv_sem,
                                        device_id=target, device_id_type=MESH)


def _copy_start(groups, *, name, after=()):
    flat = [a for arrays, _, _ in groups for a in arrays]
    n, ng = len(flat), len(groups)
    after = list(after)
    n_in = n + len(after)

    def body(*refs):
        sems = refs[n_in:n_in + 2 * ng]
        thru = refs[n_in + 2 * ng:n_in + 2 * ng + n]
        token = refs[-1]
        pos = 0
        for g, (arrays, plan, n_copies) in enumerate(groups):
            copies = plan(thru[pos:pos + len(arrays)])
            pos += len(arrays)
            assert len(copies) == n_copies
            for i, (src, dst, _, target) in enumerate(copies):
                _remote(src, dst, sems[2 * g].at[i], sems[2 * g + 1].at[i], target).start()
        token[...] = jnp.zeros_like(token)

    out_shape = tuple(pltpu.SemaphoreType.DMA((n_copies,)) for _, _, n_copies in groups for _ in range(2))
    out_shape += tuple(pltpu.HBM(a.shape, a.dtype) for a in flat) + (TOKEN,)
    res = pl.pallas_call(
        body, name=name, out_shape=out_shape,
        in_specs=(HBM_SPEC,) * n + (pl.BlockSpec(memory_space=pl.ANY),) * len(after),
        out_specs=(SEM_SPEC,) * (2 * ng) + (HBM_SPEC,) * n + (pl.BlockSpec(memory_space=pltpu.VMEM),),
        input_output_aliases={k: 2 * ng + k for k in range(n)},
        compiler_params=pltpu.CompilerParams(has_side_effects=EFFECT),
    )(*[_in_hbm(a) for a in flat], *after)
    sems = [(res[2 * g], res[2 * g + 1]) for g in range(ng)]
    thru, pos = [], 2 * ng
    for arrays, _, _ in groups:
        thru.append(list(res[pos:pos + len(arrays)]))
        pos += len(arrays)
    return sems, thru, res[-1]


def _copy_wait(arrays, sems, plan, n_copies, after, *, name):
    n = len(arrays)
    after = list(after)

    def body(*refs):
        send, recv = refs[n], refs[n + 1]
        x, y, c = _place()
        copies = plan(refs[:n])
        assert len(copies) == n_copies
        for i, (src, dst, landing, target) in enumerate(copies):
            _remote(src, dst, send.at[i], recv.at[i], target).wait_send()
            _remote(landing, landing, send.at[i], recv.at[i], (x, y, c)).wait_recv()

    return list(pl.pallas_call(
        body, name=name, out_shape=tuple(pltpu.HBM(a.shape, a.dtype) for a in arrays),
        in_specs=(HBM_SPEC,) * n + (SEM_SPEC, SEM_SPEC) + (pl.BlockSpec(memory_space=pl.ANY),) * len(after),
        out_specs=(HBM_SPEC,) * n, input_output_aliases={k: k for k in range(n)},
        compiler_params=pltpu.CompilerParams(has_side_effects=EFFECT),
    )(*arrays, *sems, *after))


def _rs_stage1(gs, tag, after=()):
    n = len(gs)
    lands = [lax.empty((4, g.shape[1] // 2, g.shape[2]), g.dtype) for g in gs]
    sems, (arrays,), token = _copy_start([(list(gs) + lands, _plan_pair_exchange, n)], name=f"rs_pair_start_{tag}",
                                         after=after)
    return (sems[0], arrays), token


def _rs_stage2(handle, after, tag):
    sems, arrays = handle
    n = len(arrays) // 2
    arrays = _copy_wait(arrays, sems, _plan_pair_exchange, n, after, name=f"rs_pair_wait_{tag}")
    c = lax.axis_index("c")
    pair = [_pair_sum(g, r, c, name=f"rs_pair_sum_{tag}{k}") for k, (g, r) in enumerate(zip(arrays[:n], arrays[n:]))]
    lands = [lax.empty((3,) + p.shape[1:], p.dtype) for p in pair]
    sems, (arrays,), token = _copy_start([(pair + lands, _plan_rs_ici, 3 * n)], name=f"rs_start_{tag}")
    return (sems[0], arrays), token


def _rs_stage3(handle, after, tag):
    sems, arrays = handle
    n = len(arrays) // 2
    arrays = _copy_wait(arrays, sems, _plan_rs_ici, 3 * n, after, name=f"rs_wait_{tag}")
    x, y, c = _place()
    halves = [_chip_sum(p, l, 2 * x + y, c, name=f"rs_chip_sum_{tag}{k}") for k, (p, l) in enumerate(zip(arrays[:n], arrays[n:]))]
    sems, (halves,), token = _copy_start([(halves, _plan_pair_gather, n)], name=f"rs_gather_start_{tag}")
    return (sems[0], halves), token


def _rs_stage4(handle, after, tag):
    sems, halves = handle
    full = _copy_wait(halves, sems, _plan_pair_gather, len(halves), after, name=f"rs_gather_wait_{tag}")
    return [f.reshape(2 * f.shape[1], f.shape[2]) for f in full]


def _to_segments(a):
    rows = a.shape[0]
    return a.reshape(N_SEG, rows // N_SEG, -1).transpose(1, 0, 2).reshape(rows, -1)


def _from_segments(a):
    rows = a.shape[0]
    return a.reshape(rows // N_SEG, N_SEG, -1).transpose(1, 0, 2).reshape(rows, -1)


def _rope_tables(L):
    t = jnp.arange(L, dtype=jnp.int32)
    row = (t // GRID_W).astype(F32)
    col = (t % GRID_W).astype(F32)
    n_freq = QK_ROPE // 4
    inv = ROPE_BASE ** (-jnp.arange(n_freq, dtype=F32) / n_freq)
    a0, a1 = row[:, None] * inv, col[:, None] * inv
    z = jnp.zeros((L, LANES - QK_ROPE), F32)
    cos = jnp.concatenate([jnp.cos(a0), jnp.cos(a0), jnp.cos(a1), jnp.cos(a1), z], axis=1)
    sin = jnp.concatenate([-jnp.sin(a0), jnp.sin(a0), -jnp.sin(a1), jnp.sin(a1), z], axis=1)
    return _to_segments(cos), _to_segments(sin)


def _col_blocks(w, nblk):
    r, c = w.shape
    return w.reshape(r, nblk, c // nblk).transpose(1, 0, 2)


def _col_range(parts, lo, hi):
    out, start = [], 0
    for p in parts:
        a, b = max(lo, start), min(hi, start + p.shape[1])
        if a < b:
            out.append(p[:, a - start:b - start])
        start += p.shape[1]
    return out


def _from_col_blocks(w4):
    nblk, r, c = w4.shape
    return w4.transpose(1, 0, 2).reshape(r, nblk * c)


def _s5_discretize(a_re, a_im, log_dt, b_re, b_im):
    dt = jnp.exp(log_dt)[:, None]
    mag = jnp.exp(a_re * dt)
    ab_re, ab_im = mag * jnp.cos(a_im * dt), mag * jnp.sin(a_im * dt)
    den = a_re * a_re + a_im * a_im
    nr, ni = ab_re - 1.0, ab_im
    co_re = (nr * a_re + ni * a_im) / den
    co_im = (ni * a_re - nr * a_im) / den
    bb_re = co_re[..., None] * b_re - co_im[..., None] * b_im
    bb_im = co_re[..., None] * b_im + co_im[..., None] * b_re
    return ab_re, ab_im, bb_re, bb_im


def _diag_blocks_in(bb, gpb):
    G, N, P = bb.shape
    t = jnp.tile(jnp.swapaxes(bb, 1, 2).reshape(G // gpb, gpb * P, N), (1, 1, gpb))
    row = lax.broadcasted_iota(jnp.int32, t.shape, 1) // P
    col = lax.broadcasted_iota(jnp.int32, t.shape, 2) // N
    return jnp.where(row == col, t, 0.0)


def _diag_blocks_out(cc, gpb):
    G, P, N = cc.shape
    t = jnp.tile(jnp.swapaxes(cc, 1, 2).reshape(G // gpb, gpb * N, P), (1, 1, gpb))
    row = lax.broadcasted_iota(jnp.int32, t.shape, 1) // N
    col = lax.broadcasted_iota(jnp.int32, t.shape, 2) // P
    return jnp.where(row == col, t, 0.0)


def _tr(ws):
    return [jnp.swapaxes(w, 1, 2) for w in ws]


WEIGHTS = ['c_ctx', 'w_mod', 'b_mod', 'norm1', 'norm2', 'w_in', 's5_a_re', 's5_a_im', 's5_log_dt', 's5_b_re', 's5_b_im',
           's5_c_re', 's5_c_im', 's5_d', 'w_glu', 'q_norm', 'kv_norm', 'w_uq', 'w_ukv', 'w_mla_o', 'w_out', 'w_ffn_in',
           'w_ffn_out', 'norm_f']
AG_GROUPS = [['w_in'], ['w_glu', 'w_uq', 'w_ukv', 'w_mla_o', 'w_out'], ['w_ffn_in', 'w_ffn_out']]
SMALL = ['norm1', 'norm2', 's5_a_re', 's5_a_im', 's5_log_dt', 's5_b_re', 's5_b_im', 's5_c_re', 's5_c_im', 's5_d',
         'q_norm', 'kv_norm', 'norm_f']


def _pad_rows(a, rows):
    return jnp.concatenate([a, jnp.zeros((rows - a.shape[0],) + a.shape[1:], a.dtype)], axis=0)


def _pack(vals, width, rows):
    flat = jnp.concatenate([v.reshape(-1).astype(F32) for v in vals])
    flat = jnp.concatenate([flat, jnp.zeros((rows * width - flat.shape[0],), F32)])
    return flat.reshape(rows, width)


def _unpack(buf, like):
    flat = buf.reshape(-1)
    out, pos = [], 0
    for v in like:
        out.append(flat[pos:pos + v.size].reshape(v.shape))
        pos += v.size
    return out


def _step(x, c, ctx, loss_target, w, m, v):
    px, py, pc = _place()
    me = 4 * px + 2 * py + pc
    me_chip = 2 * px + py
    L, D = x.shape[1], x.shape[2]
    Lc = ctx.shape[1]
    T = L + Lc
    SW = D // 2
    G = SW // S5_GROUP
    C = G * S5_STATE
    H = MLA_HEADS
    q_rank = w['q_norm'].shape[1]
    kv_rank = w['kv_norm'].shape[1]
    d_ff = w['w_ffn_out'].shape[1] * 4
    wa_used = SW + q_rank + kv_rank + QK_ROPE
    WA = -(-(SW + q_rank + kv_rank + LANES) // 512) * 512

    c_all = _allgather8(c.astype(F32).reshape(SUBLANES, D // SUBLANES), name="ag_cond").reshape(8, D)
    cond = jnp.concatenate([c_all, w['c_ctx'].reshape(1, D)], axis=0)
    cond = _pad_rows(cond, 16)
    (act,) = _rw(lambda t: (jax.nn.silu(t),), [cond], [], [F32], name="cond_silu")
    w_mod, cs_mod = w['w_mod'][0], w['w_mod'].shape[2]
    mod_part = _mm(act, w_mod, out_dtype=F32, name="mod_fwd")
    mod_all = _allgather8(mod_part, name="ag_mod")
    mod_full = jnp.concatenate([mod_all[0], mod_all[2], mod_all[4], mod_all[6]], axis=1) + w['b_mod']
    m_lat = lax.dynamic_slice_in_dim(mod_full, me, 1, axis=0).reshape(6, D)
    m_ctx = mod_full[8].reshape(6, D)
    sh1, sc1, g1, sh2, sc2, g2 = (m_lat[i:i + 1] for i in range(6))
    csh1, csc1 = m_ctx[0:1], m_ctx[1:2]

    ag_groups = [([_into_slot(w[nme][0], me_chip, 4, BF16, name=f"cast_{nme}") for nme in grp], _plan_ag_ici, 3 * len(grp))
                 for grp in AG_GROUPS]
    ag_sems, ag_bufs, ag_token = _copy_start(ag_groups, name="ag_start", after=[mod_full])
    gathered, ag_pair = {}, {}

    def landed(g, after):
        n_cp = 3 * len(AG_GROUPS[g])
        got = _copy_wait(ag_bufs[g], ag_sems[g], _plan_ag_ici, n_cp, after, name=f"ag_wait_{g}")
        sems, (got,), token = _copy_start([(got, _plan_ag_pair, n_cp)], name=f"ag_pair_start_{g}")
        ag_pair[g] = (sems[0], got)
        return token[0, 0]

    def arrive(g, after):
        sems, got = ag_pair[g]
        got = _copy_wait(got, sems, _plan_ag_pair, 3 * len(AG_GROUPS[g]), after, name=f"ag_pair_wait_{g}")
        gathered.update(zip(AG_GROUPS[g], got))

    xs = _to_segments(x[0])
    cs = _to_segments(ctx[0])
    tgt = _to_segments(loss_target[0])
    cos, sin = _rope_tables(L)
    n1, n2, nf = w['norm1'], w['norm2'], w['norm_f'].reshape(1, D)
    qg, kvg = w['q_norm'], w['kv_norm']

    (xn_lat,) = _rw(_f_norm_mod, [xs], [n1 + ag_token[0, 0], sc1, sh1], [BF16], name="norm1_lat")
    (xn_ctx,) = _rw(_f_norm_mod, [cs], [n1, csc1, csh1], [BF16], name="norm1_ctx")
    xn = jnp.concatenate([xn_lat, xn_ctx], axis=0)
    tok = landed(0, [xn])

    gpb = min(S5_BLOCK_GROUPS, G)
    gpo = min(8, G)
    d_skip = w['s5_d'][0].reshape(1, SW)
    disc, vjp_disc, w_b, w_c = [], [], [], []
    for d in range(2):
        prm = (w['s5_a_re'][0, d], w['s5_a_im'][0, d], w['s5_log_dt'][0, d] + tok, w['s5_b_re'][0, d], w['s5_b_im'][0, d])

        def prep(a_re, a_im, log_dt, b_re, b_im):
            ab_re, ab_im, bb_re, bb_im = _s5_discretize(a_re, a_im, log_dt, b_re, b_im)
            return ab_re.reshape(1, C), ab_im.reshape(1, C), _diag_blocks_in(bb_re, gpb), _diag_blocks_in(bb_im, gpb)

        out, vj = jax.vjp(prep, *prm)
        disc.append(out)
        vjp_disc.append(vj)
        w_b += [out[2], out[3]]
        w_c += [_diag_blocks_out(w['s5_c_re'][0, d], gpo), -_diag_blocks_out(w['s5_c_im'][0, d], gpo)]
    nb_in = G // gpb
    nb_out = G // gpo

    arrive(0, [xn, tgt] + w_b + w_c)
    w_in = [gathered['w_in'][s] for s in range(4)]
    w_a = jnp.concatenate(_col_range(w_in, 0, wa_used) + [jnp.zeros((D, WA - wa_used), BF16)], axis=1)
    w_g = jnp.concatenate(_col_range(w_in, wa_used, wa_used + 2 * D), axis=1)
    ha = _mm(xn, w_a, out_dtype=F32, name="in_proj")
    ha_lat, ha_ctx = ha[:L], ha[L:]
    gt = _mm(xn_lat, w_g, out_dtype=F32, name="in_gates")
    f_post_lat = _make_f_post_in(SW, q_rank, kv_rank, True)
    f_post_ctx = _make_f_post_in(SW, q_rank, kv_rank, False)
    u_lat, cqn, ckvn_lat, kr_lat = _rw(f_post_lat, [ha_lat, cos, sin], [qg, kvg], [F32, BF16, BF16, BF16], name="post_in_lat")
    u_ctx, ckvn_ctx, kr_ctx = _rw(f_post_ctx, [ha_ctx], [kvg], [F32, BF16, BF16], name="post_in_ctx")
    zero = jnp.zeros((1, C), F32) + landed(1, [u_lat, u_ctx])

    h_lat, h_ctx, hT_ctx, r5 = [], [], [], []
    for d, rev in enumerate((False, True)):
        lr, li = disc[d][0], disc[d][1]
        hcr, hci, tr, ti = _s5_scan(u_ctx, w_b[2 * d], w_b[2 * d + 1], lr, li, zero, zero, zero, zero, reverse=rev,
                                    name=f"s5_scan_ctx_{d}")
        hlr, hli, _, _, y = _s5_scan(u_lat, w_b[2 * d], w_b[2 * d + 1], lr, li, tr, ti, zero, zero, reverse=rev,
                                     name=f"s5_scan_lat_{d}", readout=(w_c[2 * d], w_c[2 * d + 1]))
        h_ctx += [hcr, hci]
        h_lat += [hlr, hli]
        hT_ctx += [tr, ti]
        r5.append(y)
    (z,) = _rw(_f_s5post, [u_lat] + r5, [d_skip], [BF16], name="s5_post")

    arrive(1, [z])
    w_glu, w_ukv, w_mla_o = (gathered[nme] for nme in ('w_glu', 'w_ukv', 'w_mla_o'))
    w_out = gathered['w_out'].reshape(D, D)
    uq3 = _from_col_blocks(gathered['w_uq']).reshape(q_rank, H, QK_NOPE + QK_ROPE)
    w_q2 = jnp.concatenate([uq3, jnp.zeros((q_rank, H, LANES - QK_ROPE), BF16)], axis=2).reshape(q_rank, H * 2 * LANES)
    q2 = _mm(cqn, w_q2, out_dtype=F32, name="q_up")
    (qq,) = _rw(_f_qpost, [q2, cos, sin], [], [BF16], name="q_rope")
    kvn = jnp.concatenate([ckvn_lat, ckvn_ctx], axis=0)
    kr_all = jnp.concatenate([kr_lat, kr_ctx], axis=0)
    kv = _mm(kvn, w_ukv, b_shards=4, out_dtype=BF16, name="kv_up")
    kr_all = kr_all + landed(2, [kv, qq]).astype(BF16)
    o = _attn_fwd(qq, kv, kr_all, name="attn_fwd")

    ab = _mm(z, w_glu, b_shards=4, out_dtype=F32, name="glu_proj")
    bm = _mm(o, w_mla_o, b_shards=4, out_dtype=F32, name="mla_out")
    (mix,) = _rw(_f_merge, [ab, bm, gt], [], [BF16], name="merge")
    out1 = _mm(mix, w_out, out_dtype=F32, name="out_proj")
    x1, xn2 = _rw(_f_resid_norm, [xs, out1], [g1, n2, sc2, sh2], [F32, BF16], name="resid_norm2")
    arrive(2, [xn2])
    w_ffn_in = gathered['w_ffn_in']
    w_ffn_out = gathered['w_ffn_out'].reshape(d_ff, D)
    hmid, ab2 = _ffn_in_swiglu(xn2, w_ffn_in, name="ffn_in")
    f2 = _mm(hmid, w_ffn_out, out_dtype=F32, name="ffn_out")
    (row_loss,) = _rw(_f_final, [x1, f2, tgt], [g2, nf], [F32], name="final_loss")
    loss = lax.psum(jnp.sum(row_loss), ("x", "y", "c"))

    ones = jnp.ones((L, 1), F32)
    (dx1_a, df2), (dg2, dnf) = _rw_vjp(_f_final, [x1, f2, tgt], [g2, nf], [[ones]], [True, True, False], [True, True],
                                       [F32, BF16], name="final_loss_bwd")
    gw_ffn_out = _mm(hmid, df2, ta=True, out_dtype=BF16, name="ffn_out_dw")
    dab2 = _ffn_out_dx_swiglu(df2, w_ffn_out, ab2, name="ffn_out_dx")
    dxn2 = _mm(dab2, w_ffn_in, tb=True, a_shards=2, b_shards=4, out_dtype=F32, name="ffn_in_dx")
    gw_ffn_in = _mm(xn2, dab2, ta=True, b_shards=2, out_shards=4, out_dtype=BF16, name="ffn_in_dw")
    rs_ffn, tok = _rs_stage1([gw_ffn_out.reshape(4, -1, D), gw_ffn_in], "ffn")
    (dx_a, dout1), (dg1, dn2, dsc2, dsh2) = _rw_vjp(
        _f_resid_norm, [xs, out1], [g1, n2 + tok[0, 0], sc2, sh2], [[dx1_a], [dxn2]], [True, True], [True] * 4, [F32, BF16],
        name="resid_norm2_bwd")
    dmix = _mm(dout1, w_out, tb=True, out_dtype=F32, name="out_proj_dx")
    rs_ffn, tok = _rs_stage2(rs_ffn, [dmix], "ffn")
    gw_out = _mm(mix, dout1, ta=True, out_dtype=BF16, name="out_proj_dw")
    (dab, dbm, dgt), _ = _rw_vjp(_f_merge, [ab, bm, gt], [], [[dmix]], [True] * 3, [], [BF16] * 3, name="merge_bwd",
                                 anchor=tok)
    dz = _mm(dab, w_glu, tb=True, b_shards=4, out_dtype=F32, name="glu_proj_dx")
    gw_glu = _mm(z, dab, ta=True, out_shards=4, out_dtype=BF16, name="glu_proj_dw")
    do = _mm(dbm, w_mla_o, tb=True, b_shards=4, out_dtype=BF16, name="mla_out_dx")
    gw_mla_o = _mm(o, dbm, ta=True, out_shards=4, out_dtype=BF16, name="mla_out_dw")
    dxn_g = _mm(dgt, w_g, tb=True, out_dtype=F32, name="in_gates_dx")
    gw_g = _mm(xn_lat, dgt, ta=True, out_dtype=BF16, name="in_gates_dw")
    rs_mid, tok = _rs_stage1([gw_out.reshape(4, -1, D), gw_glu, gw_mla_o], "mid", after=[gw_g])

    (du_a, dr5), (dd_skip,) = _rw_vjp(_f_s5post, [u_lat] + r5, [d_skip + tok[0, 0]], [[dz]], [True, True, False], [True],
                                      [F32, F32], name="s5_post_bwd")
    dw_c = _bd_dw(h_lat, [dr5] * 4, nb_out, name="s5_readout_dw")
    rs_mid, tok = _rs_stage2(rs_mid, dw_c[:1], "mid")
    zero = zero + tok[0, 0]
    w_ct = _tr(w_c)
    zeros_ctx = jnp.zeros((Lc, SW), BF16)
    mu_lat, mu_ctx, dlam = [], [], []
    for d, rev in enumerate((False, True)):
        lr, li = disc[d][0], disc[d][1]
        mlr, mli, fr, fi = _s5_scan(dr5, w_ct[2 * d], w_ct[2 * d + 1], lr, -li, zero, zero, zero, zero, reverse=not rev,
                                    name=f"s5_adj_lat_{d}")
        dh0r, dh0i = _cmul(lr, -li, fr, fi)
        mcr, mci, _, _ = _s5_scan(zeros_ctx, w_ct[2 * d], w_ct[2 * d + 1], lr, -li, zero, zero, dh0r, dh0i,
                                  reverse=not rev, name=f"s5_adj_ctx_{d}")
        dl_lat = _s5_dlam(mlr, mli, h_lat[2 * d], h_lat[2 * d + 1], hT_ctx[2 * d], hT_ctx[2 * d + 1], reverse=rev,
                          name=f"s5_dlam_lat_{d}")
        dl_ctx = _s5_dlam(mcr, mci, h_ctx[2 * d], h_ctx[2 * d + 1], zero, zero, reverse=rev, name=f"s5_dlam_ctx_{d}")
        mu_lat += [mlr, mli]
        mu_ctx += [mcr, mci]
        dlam.append((dl_lat[0] + dl_ctx[0], dl_lat[1] + dl_ctx[1]))
    du_b = _bd_fanin(mu_lat, _tr(w_b), name="s5_bu_lat_dx")
    du_ctx = _bd_fanin(mu_ctx, _tr(w_b), name="s5_bu_ctx_dx")
    dw_b_lat = _bd_dw([u_lat] * 4, mu_lat, nb_in, name="s5_bu_lat_dw")
    dw_b_ctx = _bd_dw([u_ctx] * 4, mu_ctx, nb_in, name="s5_bu_ctx_dw")
    g_s5 = {}
    for d in range(2):
        ct = (dlam[d][0], dlam[d][1], dw_b_lat[2 * d] + dw_b_ctx[2 * d], dw_b_lat[2 * d + 1] + dw_b_ctx[2 * d + 1])
        ga_re, ga_im, gdt, gb_re, gb_im = vjp_disc[d](ct)
        _, vj_c = jax.vjp(lambda cr, ci: (_diag_blocks_out(cr, gpo), -_diag_blocks_out(ci, gpo)),
                          w['s5_c_re'][0, d], w['s5_c_im'][0, d])
        gc_re, gc_im = vj_c((dw_c[2 * d], dw_c[2 * d + 1]))
        for nme, val in (('s5_a_re', ga_re), ('s5_a_im', ga_im), ('s5_log_dt', gdt), ('s5_b_re', gb_re),
                         ('s5_b_im', gb_im), ('s5_c_re', gc_re), ('s5_c_im', gc_im)):
            g_s5.setdefault(nme, []).append(val)
    g_small = {nme: jnp.stack(vals)[None] for nme, vals in g_s5.items()}
    g_small['s5_d'] = dd_skip.reshape(w['s5_d'].shape)

    dqq, dkv, dkr = _attn_bwd(qq, kv, kr_all, do, name="attn_bwd")
    (dq2,), _ = _rw_vjp(_f_qpost, [q2, cos, sin], [], [[dqq]], [True, False, False], [], [BF16], name="q_rope_bwd")
    dcqn = _mm(dq2, w_q2, tb=True, out_dtype=F32, name="q_up_dx")
    gw_q2 = _mm(cqn, dq2, ta=True, out_dtype=BF16, name="q_up_dw")
    dckvn = _mm(dkv, w_ukv, tb=True, b_shards=4, out_dtype=F32, name="kv_up_dx")
    gw_ukv = _mm(kvn, dkv, ta=True, out_shards=4, out_dtype=BF16, name="kv_up_dw")
    gw_uq = gw_q2.reshape(q_rank, H, 2 * LANES)[:, :, :QK_NOPE + QK_ROPE].reshape(q_rank, H * (QK_NOPE + QK_ROPE))
    rs_kv, tok = _rs_stage1([_col_blocks(gw_uq, 4), gw_ukv], "kv")

    (dha_lat,), (dqg, dkvg_lat) = _rw_vjp(
        f_post_lat, [ha_lat, cos, sin], [qg, kvg + tok[0, 0]], [[du_a, du_b], [dcqn], [dckvn[:L]], [dkr[:L]]],
        [True, False, False], [True, True], [BF16], name="post_in_lat_bwd")
    (dha_ctx,), (dkvg_ctx,) = _rw_vjp(f_post_ctx, [ha_ctx], [kvg], [[du_ctx], [dckvn[L:]], [dkr[L:]]], [True], [True],
                                      [BF16], name="post_in_ctx_bwd")
    dha = jnp.concatenate([dha_lat, dha_ctx], axis=0)
    dxn = _mm(dha, w_a, tb=True, out_dtype=F32, name="in_proj_dx")
    gw_a = _mm(xn, dha, ta=True, out_dtype=BF16, name="in_proj_dw")
    rs_kv, tok = _rs_stage2(rs_kv, [gw_a], "kv")
    (dx_seg,), (dn1_lat, dsc1, dsh1) = _rw_vjp(
        _f_norm_mod_keep, [xs], [n1 + tok[0, 0], sc1, sh1], [[dxn[:L], dxn_g], [dx_a]], [True], [True] * 3, [F32],
        name="norm1_lat_bwd")
    _, (dn1_ctx, dcsc1, dcsh1) = _rw_vjp(_f_norm_mod, [cs], [n1, csc1, csh1], [[dxn[L:]]], [False], [True] * 3, [],
                                         name="norm1_ctx_bwd")
    grad_x = _from_segments(dx_seg)[None]
    g_small.update(norm1=dn1_lat + dn1_ctx, norm2=dn2, q_norm=dqg, kv_norm=dkvg_lat + dkvg_ctx, norm_f=dnf.reshape(D))
    cs_in = w['w_in'].shape[2]
    gw_in = jnp.stack([jnp.concatenate(_col_range([gw_a[:, :wa_used], gw_g], s * cs_in, (s + 1) * cs_in), axis=1)
                       for s in range(4)])
    small_vals = [g_small[nme] for nme in SMALL]
    n_small = sum(val.size for val in small_vals)
    small_rows = -(-n_small // (LANES * 4 * 32)) * 32

    zD = jnp.zeros((1, D), F32)
    dm = jnp.concatenate([
        jnp.concatenate([dsh1, dsc1, dg1, dsh2, dsc2, dg2], axis=1),
        jnp.concatenate([dcsh1, dcsc1, zD, zD, zD, zD], axis=1),
    ], axis=0)
    dm_all = _allgather8(dm.reshape(SUBLANES, -1), name="ag_dmod").reshape(8, 2, 6 * D)
    rs_in, tok = _rs_stage1([gw_in, _pack(small_vals, LANES, 4 * small_rows).reshape(4, small_rows, LANES)],
                            "in", after=[dm_all])
    dm_ctx = dm_all[0, 1] + tok[0, 0]
    for k in range(1, 8):
        dm_ctx = dm_ctx + dm_all[k, 1]
    dmod = _pad_rows(jnp.concatenate([dm_all[:, 0, :], dm_ctx[None]], axis=0), 16)
    g_b_mod = jnp.sum(dmod, axis=0, keepdims=True)
    dmod_mine = lax.dynamic_slice_in_dim(dmod, me_chip * cs_mod, cs_mod, axis=1)
    g_w_mod = _mm(act, dmod_mine, ta=True, out_dtype=F32, name="mod_dw")
    dact_part = _mm(dmod_mine, w_mod, tb=True, out_dtype=F32, name="mod_dx")
    dact_all = _allgather8(dact_part[8].reshape(SUBLANES, D // SUBLANES), name="ag_dact").reshape(8, D)
    dact = jnp.zeros((16, D), F32).at[8].set(dact_all[0] + dact_all[2] + dact_all[4] + dact_all[6])
    (dcond_rows,), _ = _rw_vjp(lambda t: (jax.nn.silu(t),), [cond], [], [[dact]], [True], [], [F32], name="cond_silu_bwd")
    g_c_ctx = dcond_rows[8]

    rs_in, tok = _rs_stage2(rs_in, [g_c_ctx], "in")

    grads, delta, new_m, new_v = {}, {}, {}, {}

    def update(members, reds, anchor):
        deltas = []
        for nme, red in zip(members, reds):
            res = _adamw(w[nme][0], red, m[nme][0], v[nme][0], name=f"adamw_{nme}", anchor=anchor)
            grads[nme], delta[nme], new_m[nme], new_v[nme] = (r.reshape(w[nme].shape) for r in res)
            deltas.append(res[1])
            anchor = None
        return deltas

    rs_ffn, tok = _rs_stage3(rs_ffn, [tok], "ffn")
    done = update(['w_mod'], [g_w_mod], tok)
    red_ffn = _rs_stage4(rs_ffn, done, "ffn")
    rs_mid, tok = _rs_stage3(rs_mid, red_ffn[:1], "mid")
    done = update(['w_ffn_out', 'w_ffn_in'], red_ffn, tok)
    red_mid = _rs_stage4(rs_mid, done, "mid")
    rs_kv, tok = _rs_stage3(rs_kv, red_mid[:1], "kv")
    done = update(['w_out', 'w_glu', 'w_mla_o'], red_mid, tok)
    red_kv = _rs_stage4(rs_kv, done, "kv")
    rs_in, tok = _rs_stage3(rs_in, red_kv[:1], "in")
    done = update(['w_uq', 'w_ukv'], red_kv, tok)
    red_in = _rs_stage4(rs_in, done, "in")
    update(['w_in'], red_in[:1], None)
    small_mine = red_in[-1]
    small_buf = _into_slot(small_mine, me_chip, 4, F32, name="small_grads_slot")
    small_all = _allgather_shards([small_buf], name="ag_small_grads")[0].reshape(4 * small_rows, LANES)
    g_small_red = dict(zip(SMALL, _unpack(small_all, [w[nme] for nme in SMALL])))
    rest = SMALL + ['c_ctx', 'b_mod']
    g_rest = dict(g_small_red, c_ctx=g_c_ctx, b_mod=g_b_mod)
    rows_rest = -(-sum(w[nme].size for nme in rest) // (LANES * 16)) * 16
    packed = [_pack([src[nme] for nme in rest], LANES, rows_rest) for src in (w, g_rest, m, v)]
    res = _adamw(*packed, name="adamw_small")
    for dst, buf in zip((grads, delta, new_m, new_v), res):
        dst.update(zip(rest, _unpack(buf, [w[nme] for nme in rest])))
    return (loss, grad_x, *[grads[nme] for nme in WEIGHTS], *[delta[nme] for nme in WEIGHTS],
            *[new_m[nme] for nme in WEIGHTS], *[new_v[nme] for nme in WEIGHTS])


def kernel(x, c, ctx, c_ctx, w_mod, b_mod, norm1, norm2, w_in, s5_a_re, s5_a_im, s5_log_dt, s5_b_re, s5_b_im, s5_c_re, s5_c_im, s5_d, w_glu, q_norm, kv_norm, w_uq, w_ukv, w_mla_o, w_out, w_ffn_in, w_ffn_out, norm_f, loss_target, m_c_ctx, m_w_mod, m_b_mod, m_norm1, m_norm2, m_w_in, m_s5_a_re, m_s5_a_im, m_s5_log_dt, m_s5_b_re, m_s5_b_im, m_s5_c_re, m_s5_c_im, m_s5_d, m_w_glu, m_q_norm, m_kv_norm, m_w_uq, m_w_ukv, m_w_mla_o, m_w_out, m_w_ffn_in, m_w_ffn_out, m_norm_f, v_c_ctx, v_w_mod, v_b_mod, v_norm1, v_norm2, v_w_in, v_s5_a_re, v_s5_a_im, v_s5_log_dt, v_s5_b_re, v_s5_b_im, v_s5_c_re, v_s5_c_im, v_s5_d, v_w_glu, v_q_norm, v_kv_norm, v_w_uq, v_w_ukv, v_w_mla_o, v_w_out, v_w_ffn_in, v_w_ffn_out, v_norm_f):
    w = dict(c_ctx=c_ctx, w_mod=w_mod, b_mod=b_mod, norm1=norm1, norm2=norm2, w_in=w_in, s5_a_re=s5_a_re, s5_a_im=s5_a_im,
             s5_log_dt=s5_log_dt, s5_b_re=s5_b_re, s5_b_im=s5_b_im, s5_c_re=s5_c_re, s5_c_im=s5_c_im, s5_d=s5_d, w_glu=w_glu,
             q_norm=q_norm, kv_norm=kv_norm, w_uq=w_uq, w_ukv=w_ukv, w_mla_o=w_mla_o, w_out=w_out, w_ffn_in=w_ffn_in,
             w_ffn_out=w_ffn_out, norm_f=norm_f)
    m = dict(c_ctx=m_c_ctx, w_mod=m_w_mod, b_mod=m_b_mod, norm1=m_norm1, norm2=m_norm2, w_in=m_w_in, s5_a_re=m_s5_a_re,
             s5_a_im=m_s5_a_im, s5_log_dt=m_s5_log_dt, s5_b_re=m_s5_b_re, s5_b_im=m_s5_b_im, s5_c_re=m_s5_c_re,
             s5_c_im=m_s5_c_im, s5_d=m_s5_d, w_glu=m_w_glu, q_norm=m_q_norm, kv_norm=m_kv_norm, w_uq=m_w_uq, w_ukv=m_w_ukv,
             w_mla_o=m_w_mla_o, w_out=m_w_out, w_ffn_in=m_w_ffn_in, w_ffn_out=m_w_ffn_out, norm_f=m_norm_f)
    v = dict(c_ctx=v_c_ctx, w_mod=v_w_mod, b_mod=v_b_mod, norm1=v_norm1, norm2=v_norm2, w_in=v_w_in, s5_a_re=v_s5_a_re,
             s5_a_im=v_s5_a_im, s5_log_dt=v_s5_log_dt, s5_b_re=v_s5_b_re, s5_b_im=v_s5_b_im, s5_c_re=v_s5_c_re,
             s5_c_im=v_s5_c_im, s5_d=v_s5_d, w_glu=v_w_glu, q_norm=v_q_norm, kv_norm=v_kv_norm, w_uq=v_w_uq, w_ukv=v_w_ukv,
             w_mla_o=v_w_mla_o, w_out=v_w_out, w_ffn_in=v_w_ffn_in, w_ffn_out=v_w_ffn_out, norm_f=v_norm_f)
    return _step(x, c, ctx, loss_target, w, m, v)
```

```python
import functools
import math

import jax
import jax.numpy as jnp
from jax import lax
from jax.experimental import pallas as pl
from jax.experimental.pallas import tpu as pltpu

F32 = jnp.float32
BF16 = jnp.bfloat16

EPS = 1e-6
GRID_W = 64
S5_GROUP = 16
S5_STATE = 64
MLA_HEADS = 8
QK_NOPE = 128
QK_ROPE = 64
V_DIM = 128
ROPE_BASE = 10000.0
ATTN_SCALE = (QK_NOPE + QK_ROPE) ** -0.5
ADAM_LR = 0.001
ADAM_B1 = 0.9
ADAM_B2 = 0.999
ADAM_EPS = 1e-08
ADAM_WD = 0.01
ADAM_STEP = 10

SUBLANES = 8
LANES = 128
V7X_VMEM_BYTES = 64 * 1024 * 1024
VMEM_LIMIT = (V7X_VMEM_BYTES * 7) // 8
N_SEG = 2 * SUBLANES
S5_BLOCK_GROUPS = 8
MESH = pl.DeviceIdType.MESH


def _pick(n, target, mult):
    best = None
    d = mult
    while d <= min(n, target):
        if n % d == 0:
            best = d
        d += mult
    return n if best is None else best


def _cparams(sem=None):
    return pltpu.CompilerParams(dimension_semantics=sem, vmem_limit_bytes=VMEM_LIMIT)


MM_VMEM_BUDGET = (V7X_VMEM_BYTES * 5) // 8


def _mm(a, b, *, ta=False, tb=False, out_dtype=F32, name, a_shards=1, b_shards=1, out_shards=1):
    if ta:
        K, M = a.shape
    else:
        M, K = a.shape[-2], a.shape[-1] * a_shards
    if tb:
        N, K2 = b.shape[-2], b.shape[-1] * b_shards
    else:
        K2, N = b.shape[-2], b.shape[-1] * b_shards
    assert K == K2, (a.shape, b.shape, ta, tb)
    n_unit = N // max(out_shards, 1 if tb else b_shards)
    k_unit = K // max(a_shards, b_shards if tb else 1)
    tn = _pick(n_unit, 1024, LANES)
    tm = _pick(M, 1024 if tn >= 512 else 2048, LANES if ta else 16)
    sa, sb, so = a.dtype.itemsize, b.dtype.itemsize, jnp.dtype(out_dtype).itemsize
    k_mult = LANES if (not ta or tb) else 16
    tk = k_mult if k_unit % k_mult == 0 else k_unit
    for cand in range(k_mult, k_unit + 1, k_mult):
        if k_unit % cand == 0 and 2 * cand * (tm * sa + tn * sb) + tm * tn * (4 + 2 * so) <= MM_VMEM_BUDGET:
            tk = cand
    nk = K // tk
    dims = (((0 if ta else 1,), (1 if tb else 0,)), ((), ()))

    def body(a_ref, b_ref, o_ref, *scratch):
        part = lax.dot_general(a_ref[...].astype(BF16), b_ref[...].astype(BF16), dims, preferred_element_type=F32)
        if nk == 1:
            o_ref[...] = part.astype(o_ref.dtype)
            return
        acc_ref, = scratch
        k = pl.program_id(2)

        @pl.when(k == 0)
        def _():
            acc_ref[...] = part

        @pl.when(k > 0)
        def _():
            acc_ref[...] += part

        @pl.when(k == nk - 1)
        def _():
            o_ref[...] = acc_ref[...].astype(o_ref.dtype)

    if ta:
        a_spec = pl.BlockSpec((tk, tm), lambda i, j, k: (k, i))
    elif a_shards == 1:
        a_spec = pl.BlockSpec((tm, tk), lambda i, j, k: (i, k))
    else:
        akb = (K // a_shards) // tk
        a_spec = pl.BlockSpec((None, tm, tk), lambda i, j, k: (k // akb, i, k % akb))
    if b_shards == 1:
        b_spec = pl.BlockSpec((tn, tk), lambda i, j, k: (j, k)) if tb else pl.BlockSpec((tk, tn), lambda i, j, k: (k, j))
    elif tb:
        kpb = (K // b_shards) // tk
        b_spec = pl.BlockSpec((None, tn, tk), lambda i, j, k: (k // kpb, j, k % kpb))
    else:
        npb = (N // b_shards) // tn
        b_spec = pl.BlockSpec((None, tk, tn), lambda i, j, k: (j // npb, k, j % npb))
    if out_shards == 1:
        out_spec = pl.BlockSpec((tm, tn), lambda i, j, k: (i, j))
        out_shape = jax.ShapeDtypeStruct((M, N), out_dtype)
    else:
        opb = (N // out_shards) // tn
        out_spec = pl.BlockSpec((None, tm, tn), lambda i, j, k: (j // opb, i, j % opb))
        out_shape = jax.ShapeDtypeStruct((out_shards, M, N // out_shards), out_dtype)
    return pl.pallas_call(
        body, name=name, grid=(M // tm, N // tn, nk),
        in_specs=[a_spec, b_spec], out_specs=out_spec, out_shape=out_shape,
        scratch_shapes=[pltpu.VMEM((tm, tn), F32)] if nk > 1 else [],
        compiler_params=_cparams(("parallel", "parallel", "arbitrary")),
    )(a, b)


FFN_TILE_ROWS = 2048


def _ffn_in_swiglu(x, w4, *, name):
    M, K = x.shape
    S, _, ns = w4.shape
    half = S * ns // 2
    tn = _pick(ns, 512, LANES)
    tm = _pick(M, FFN_TILE_ROWS, 16)
    npb = ns // tn

    def body(x_ref, wa_ref, wb_ref, h_ref, ab_ref):
        xb = x_ref[...].astype(BF16)
        a = jnp.dot(xb, wa_ref[...].astype(BF16), preferred_element_type=F32)
        b = jnp.dot(xb, wb_ref[...].astype(BF16), preferred_element_type=F32)
        h_ref[...] = (jax.nn.silu(a) * b).astype(h_ref.dtype)
        ab_ref[0] = a.astype(ab_ref.dtype)
        ab_ref[1] = b.astype(ab_ref.dtype)

    return pl.pallas_call(
        body, name=name, grid=(M // tm, half // tn),
        in_specs=[pl.BlockSpec((tm, K), lambda i, j: (i, 0)),
                  pl.BlockSpec((None, K, tn), lambda i, j: (j // npb, 0, j % npb)),
                  pl.BlockSpec((None, K, tn), lambda i, j: (S // 2 + j // npb, 0, j % npb))],
        out_specs=[pl.BlockSpec((tm, tn), lambda i, j: (i, j)), pl.BlockSpec((2, tm, tn), lambda i, j: (0, i, j))],
        out_shape=[jax.ShapeDtypeStruct((M, half), BF16), jax.ShapeDtypeStruct((2, M, half), BF16)],
        compiler_params=_cparams(("parallel", "parallel")),
    )(x, w4, w4)


def _ffn_out_dx_swiglu(dy, w, ab, *, name):
    M, D = dy.shape
    n2 = w.shape[0]
    tn = _pick(n2, 512, LANES)
    tm = _pick(M, FFN_TILE_ROWS // 2, 16)

    def body(dy_ref, w_ref, ab_ref, o_ref):
        dh = lax.dot_general(dy_ref[...].astype(BF16), w_ref[...].astype(BF16), NT_DIMS, preferred_element_type=F32)
        a, b = ab_ref[0].astype(F32), ab_ref[1].astype(F32)
        s = jax.nn.sigmoid(a)
        o_ref[0] = (dh * b * (s * (1.0 + a * (1.0 - s)))).astype(o_ref.dtype)
        o_ref[1] = (dh * (a * s)).astype(o_ref.dtype)

    return pl.pallas_call(
        body, name=name, grid=(M // tm, n2 // tn),
        in_specs=[pl.BlockSpec((tm, D), lambda i, j: (i, 0)), pl.BlockSpec((tn, D), lambda i, j: (j, 0)),
                  pl.BlockSpec((2, tm, tn), lambda i, j: (0, i, j))],
        out_specs=pl.BlockSpec((2, tm, tn), lambda i, j: (0, i, j)),
        out_shape=jax.ShapeDtypeStruct((2, M, n2), BF16),
        compiler_params=_cparams(("parallel", "parallel")),
    )(dy, w, ab)


ROW_TILE_BYTES = 6 * 1024 * 1024
STREAM_TILE_BYTES = 14 * 1024 * 1024


def _row_tile(tiled, extra_bytes=0, budget=ROW_TILE_BYTES):
    rows = tiled[0].shape[0]
    per_row = sum(a.shape[1] * 4 for a in tiled) + extra_bytes
    target = max(SUBLANES, budget // max(per_row, 1))
    return _pick(rows, min(target, 512), 16)


def _rw(f, tiled, bcast, out_dtypes, *, name, anchor=None, tile_bytes=ROW_TILE_BYTES):
    nt, nb = len(tiled), len(bcast)
    rows = tiled[0].shape[0]
    outs_aval = jax.eval_shape(f, *[jax.ShapeDtypeStruct((16, a.shape[1]), F32) for a in tiled],
                               *[jax.ShapeDtypeStruct(b.shape, F32) for b in bcast])
    widths = [o.shape[1] for o in outs_aval]
    tm = _row_tile(tiled, sum(w * 4 for w in widths), tile_bytes)

    extra = [] if anchor is None else [anchor]
    n_in = nt + nb + len(extra)

    def body(*refs):
        tin = [r[...].astype(F32) for r in refs[:nt]]
        bin_ = [r[...].astype(F32) for r in refs[nt:nt + nb]]
        outs = f(*tin, *bin_)
        for o_ref, o in zip(refs[n_in:], outs):
            o_ref[...] = o.astype(o_ref.dtype)

    in_specs = [pl.BlockSpec((tm, a.shape[1]), lambda i: (i, 0)) for a in tiled]
    in_specs += [pl.BlockSpec(b.shape, lambda i: (0, 0)) for b in bcast + extra]
    res = pl.pallas_call(
        body, name=name, grid=(rows // tm,), in_specs=in_specs,
        out_specs=[pl.BlockSpec((tm, w), lambda i: (i, 0)) for w in widths],
        out_shape=[jax.ShapeDtypeStruct((rows, w), dt) for w, dt in zip(widths, out_dtypes)],
        compiler_params=_cparams(("parallel",)),
    )(*tiled, *bcast, *extra)
    return list(res)


def _rw_vjp(f, tiled, bcast, cts, need_t, need_b, t_dtypes, *, name, anchor=None):
    nt, nb = len(tiled), len(bcast)
    rows = tiled[0].shape[0]
    flat_cts = [c for group in cts for c in group]
    t_idx = [i for i in range(nt) if need_t[i]]
    b_idx = [i for i in range(nb) if need_b[i]]
    tm = _row_tile(list(tiled) + flat_cts, sum(tiled[i].shape[1] * 4 for i in t_idx))
    nc = len(flat_cts)
    extra = [] if anchor is None else [anchor]

    def body(*refs):
        i = pl.program_id(0)
        tin = [r[...].astype(F32) for r in refs[:nt]]
        bin_ = [r[...].astype(F32) for r in refs[nt:nt + nb]]
        ct_refs = refs[nt + nb:nt + nb + nc]
        out_refs = refs[nt + nb + nc + len(extra):]
        outs, vjp_fn = jax.vjp(f, *tin, *bin_)
        ct_vals, pos = [], 0
        for o, group in zip(outs, cts):
            acc = jnp.zeros_like(o)
            for _ in group:
                acc = acc + ct_refs[pos][...].astype(F32)
                pos += 1
            ct_vals.append(acc)
        grads = vjp_fn(tuple(ct_vals))
        for o_ref, k in zip(out_refs[:len(t_idx)], t_idx):
            o_ref[...] = grads[k].astype(o_ref.dtype)
        for o_ref, k in zip(out_refs[len(t_idx):], b_idx):
            @pl.when(i == 0)
            def _(o_ref=o_ref):
                o_ref[...] = jnp.zeros_like(o_ref)

            o_ref[...] += grads[nt + k]

    in_specs = [pl.BlockSpec((tm, a.shape[1]), lambda i: (i, 0)) for a in tiled]
    in_specs += [pl.BlockSpec(b.shape, lambda i: (0, 0)) for b in bcast]
    in_specs += [pl.BlockSpec((tm, c.shape[1]), lambda i: (i, 0)) for c in flat_cts]
    in_specs += [pl.BlockSpec(e.shape, lambda i: (0, 0)) for e in extra]
    out_specs = [pl.BlockSpec((tm, tiled[k].shape[1]), lambda i: (i, 0)) for k in t_idx]
    out_specs += [pl.BlockSpec(bcast[k].shape, lambda i: (0, 0)) for k in b_idx]
    out_shape = [jax.ShapeDtypeStruct(tiled[k].shape, dt) for k, dt in zip(t_idx, t_dtypes)]
    out_shape += [jax.ShapeDtypeStruct(bcast[k].shape, F32) for k in b_idx]
    res = pl.pallas_call(
        body, name=name, grid=(rows // tm,), in_specs=in_specs, out_specs=out_specs, out_shape=out_shape,
        compiler_params=_cparams(("arbitrary",)),
    )(*tiled, *bcast, *flat_cts, *extra)
    res = list(res)
    return res[:len(t_idx)], res[len(t_idx):]


def _rms(x, g):
    return x * lax.rsqrt(jnp.mean(x * x, axis=-1, keepdims=True) + EPS) * g


def _f_norm_mod(x, g, sc, sh):
    return (_rms(x, g) * (1.0 + sc) + sh,)


def _f_norm_mod_keep(x, g, sc, sh):
    return (_rms(x, g) * (1.0 + sc) + sh, x)


@jax.custom_vjp
def _swap16(x):
    w = x.shape[-1]
    lane = lax.broadcasted_iota(jnp.int32, x.shape, x.ndim - 1)
    return jnp.where((lane & 16) == 0, pltpu.roll(x, w - 16, x.ndim - 1), pltpu.roll(x, 16, x.ndim - 1))


_swap16.defvjp(lambda x: (_swap16(x), None), lambda _, g: (_swap16(g),))


def _rope(x, cos, sin):
    return x * cos + _swap16(x) * sin


def _make_f_post_in(sw, q_rank, kv_rank, with_q):
    o1, o2, o3 = sw, sw + q_rank, sw + q_rank + kv_rank

    if with_q:
        def f(ha, cos, sin, qg, kvg):
            u = ha[:, :o1]
            cqn = _rms(ha[:, o1:o2], qg)
            ckvn = _rms(ha[:, o2:o3], kvg)
            kr = _rope(ha[:, o3:o3 + LANES], cos, sin)
            return u, cqn, ckvn, kr
    else:
        def f(ha, kvg):
            return ha[:, :o1], _rms(ha[:, o2:o3], kvg), ha[:, o3:o3 + LANES]
    return f


def _f_qpost(q2, cos, sin):
    parts = []
    for h in range(q2.shape[1] // (2 * LANES)):
        o = 2 * LANES * h
        parts += [q2[:, o:o + LANES], _rope(q2[:, o + LANES:o + 2 * LANES], cos, sin)]
    return (jnp.concatenate(parts, axis=1),)


def _f_s5post(u, r0, r1, d):
    return (jax.nn.gelu(d * u + r0 + r1, approximate=True),)


def _f_merge(ab, bm, gt):
    d = bm.shape[1]
    br_s5 = ab[:, :d] * jax.nn.sigmoid(ab[:, d:])
    g = jax.nn.sigmoid(gt)
    return (g[:, :d] * br_s5 + g[:, d:] * bm,)


def _f_resid_norm(x, out, g1, n2, sc2, sh2):
    x1 = x + g1 * out
    return x1, _rms(x1, n2) * (1.0 + sc2) + sh2


def _f_final(x1, f, tgt, g2, nf):
    y = _rms(x1 + g2 * f, nf)
    return (0.5 * jnp.mean(jnp.square(y - tgt), axis=-1, keepdims=True),)


def _bd_fanin(xs, ws, *, name):
    nw = len(ws)
    nb, kb, nn = ws[0].shape
    T = xs[0].shape[0]
    tm = _pick(T, 512, 16)

    def body(*refs):
        acc = None
        for x_ref, w_ref in zip(refs[:nw], refs[nw:2 * nw]):
            t = jnp.dot(x_ref[...].astype(BF16), w_ref[0].astype(BF16), preferred_element_type=F32)
            acc = t if acc is None else acc + t
        refs[2 * nw][...] = acc

    return pl.pallas_call(
        body, name=name, grid=(nb, T // tm),
        in_specs=[pl.BlockSpec((tm, kb), lambda j, i: (i, j))] * nw + [pl.BlockSpec((1, kb, nn), lambda j, i: (j, 0, 0))] * nw,
        out_specs=pl.BlockSpec((tm, nn), lambda j, i: (i, j)),
        out_shape=jax.ShapeDtypeStruct((T, nb * nn), F32),
        compiler_params=_cparams(("parallel", "parallel")),
    )(*xs, *ws)


def _bd_dw(xs, dys, nb, *, name):
    npair = len(xs)
    T = xs[0].shape[0]
    kb = xs[0].shape[1] // nb
    nn = dys[0].shape[1] // nb
    tm = _pick(T, 512, 16)
    dims = (((0,), (0,)), ((), ()))

    def body(*refs):
        i = pl.program_id(1)
        for x_ref, d_ref, o_ref in zip(refs[:npair], refs[npair:2 * npair], refs[2 * npair:]):
            @pl.when(i == 0)
            def _(o_ref=o_ref):
                o_ref[...] = jnp.zeros_like(o_ref)

            o_ref[0] += lax.dot_general(x_ref[...].astype(BF16), d_ref[...].astype(BF16), dims,
                                        preferred_element_type=F32)

    return list(pl.pallas_call(
        body, name=name, grid=(nb, T // tm),
        in_specs=[pl.BlockSpec((tm, kb), lambda j, i: (i, j))] * npair + [pl.BlockSpec((tm, nn), lambda j, i: (i, j))] * npair,
        out_specs=[pl.BlockSpec((1, kb, nn), lambda j, i: (j, 0, 0))] * npair,
        out_shape=[jax.ShapeDtypeStruct((nb, kb, nn), F32)] * npair,
        compiler_params=_cparams(("parallel", "arbitrary")),
    )(*xs, *dys))


def _cmul(ar, ai, br, bi):
    return ar * br - ai * bi, ar * bi + ai * br


def _cpow(lr, li, n):
    rr, ri = None, None
    br, bi = lr, li
    while n:
        if n & 1:
            rr, ri = (br, bi) if rr is None else _cmul(rr, ri, br, bi)
        n >>= 1
        if n:
            br, bi = _cmul(br, bi, br, bi)
    return rr, ri


SCAN_MM_ROWS = 512


def _s5_scan(x, w_re, w_im, lam_re, lam_im, h0_re, h0_im, e0_re, e0_im, *, reverse, name, readout=None):
    rows = x.shape[0]
    nb, kb, cb = w_re.shape
    C = nb * cb
    n = rows // N_SEG
    mm_rows = _pick(rows, SCAN_MM_ROWS, 16)
    seg_order = list(range(N_SEG))[::-1] if reverse else list(range(N_SEG))
    s_first, s_last = seg_order[0], seg_order[-1]
    n_ro = 0 if readout is None else 2

    def body(x_ref, wr_ref, wi_ref, lr_ref, li_ref, h0r_ref, h0i_ref, e0r_ref, e0i_ref, *rest):
        ro_refs, (hr_ref, hi_ref, htr_ref, hti_ref), y_refs = rest[:n_ro], rest[n_ro:n_ro + 4], rest[n_ro + 4:-2]
        locr_ref, loci_ref = rest[-2:]
        shape = (N_SEG, cb)
        lr = jnp.broadcast_to(lr_ref[...], shape)
        li = jnp.broadcast_to(li_ref[...], shape)
        row = lax.broadcasted_iota(jnp.int32, shape, 0)

        def step_of(k):
            return (n - 1 - k) if reverse else k

        def rows_of(k):
            return pl.ds(pl.multiple_of(step_of(k) * N_SEG, N_SEG), N_SEG)

        wr, wi = wr_ref[...].astype(BF16), wi_ref[...].astype(BF16)
        for r0 in range(0, rows, mm_rows):
            xb = x_ref[r0:r0 + mm_rows, :].astype(BF16)
            locr_ref[r0:r0 + mm_rows, :] = jnp.dot(xb, wr, preferred_element_type=F32)
            loci_ref[r0:r0 + mm_rows, :] = jnp.dot(xb, wi, preferred_element_type=F32)

        first = row == s_first
        hr = locr_ref[rows_of(0), :] + jnp.where(first, e0r_ref[...], 0.0)
        hi = loci_ref[rows_of(0), :] + jnp.where(first, e0i_ref[...], 0.0)
        locr_ref[rows_of(0), :] = hr
        loci_ref[rows_of(0), :] = hi

        def pass1(k, carry):
            hr, hi = carry
            pr, pi = _cmul(lr, li, hr, hi)
            hr = pr + locr_ref[rows_of(k), :]
            hi = pi + loci_ref[rows_of(k), :]
            locr_ref[rows_of(k), :] = hr
            loci_ref[rows_of(k), :] = hi
            return hr, hi

        er, ei = lax.fori_loop(1, n, pass1, (hr, hi))

        lnr, lni = _cpow(lr_ref[...], li_ref[...], n)
        cr, ci = h0r_ref[...], h0i_ref[...]
        cin_r = jnp.zeros(shape, F32)
        cin_i = jnp.zeros(shape, F32)
        for s in seg_order:
            cin_r = jnp.where(row == s, cr, cin_r)
            cin_i = jnp.where(row == s, ci, cin_i)
            if s != s_last:
                pr, pi = _cmul(lnr, lni, cr, ci)
                cr = pr + jnp.sum(jnp.where(row == s, er, 0.0), axis=0, keepdims=True)
                ci = pi + jnp.sum(jnp.where(row == s, ei, 0.0), axis=0, keepdims=True)

        def pass2(k, carry):
            pr, pi, _, _ = carry
            ar, ai = _cmul(pr, pi, cin_r, cin_i)
            hr = locr_ref[rows_of(k), :] + ar
            hi = loci_ref[rows_of(k), :] + ai
            hr_ref[rows_of(k), :] = hr.astype(hr_ref.dtype)
            hi_ref[rows_of(k), :] = hi.astype(hi_ref.dtype)
            npr, npi = _cmul(pr, pi, lr, li)
            return npr, npi, hr, hi

        _, _, last_r, last_i = lax.fori_loop(0, n, pass2, (lr, li, er, ei))
        htr_ref[...] = jnp.sum(jnp.where(row == s_last, last_r, 0.0), axis=0, keepdims=True)
        hti_ref[...] = jnp.sum(jnp.where(row == s_last, last_i, 0.0), axis=0, keepdims=True)

        if readout is not None:
            cr, ci = ro_refs[0][...].astype(BF16), ro_refs[1][...].astype(BF16)
            for r0 in range(0, rows, mm_rows):
                y_refs[0][r0:r0 + mm_rows, :] = (
                    jnp.dot(hr_ref[r0:r0 + mm_rows, :].astype(BF16), cr, preferred_element_type=F32)
                    + jnp.dot(hi_ref[r0:r0 + mm_rows, :].astype(BF16), ci, preferred_element_type=F32))

    big = pl.BlockSpec((rows, cb), lambda j: (0, j))
    vec = pl.BlockSpec((1, cb), lambda j: (0, j))
    wspec = pl.BlockSpec((None, kb, cb), lambda j: (j, 0, 0))
    in_specs = [pl.BlockSpec((rows, kb), lambda j: (0, j)), wspec, wspec] + [vec] * 6
    out_specs = [big, big, vec, vec]
    out_shape = [jax.ShapeDtypeStruct((rows, C), BF16)] * 2 + [jax.ShapeDtypeStruct((1, C), F32)] * 2
    extra = []
    if readout is not None:
        pb = readout[0].shape[2]
        in_specs += [pl.BlockSpec((None, cb, pb), lambda j: (j, 0, 0))] * 2
        out_specs.append(pl.BlockSpec((rows, pb), lambda j: (0, j)))
        out_shape.append(jax.ShapeDtypeStruct((rows, nb * pb), F32))
        extra = list(readout)
    return pl.pallas_call(
        body, name=name, grid=(nb,), in_specs=in_specs, out_specs=out_specs, out_shape=out_shape,
        scratch_shapes=[pltpu.VMEM((rows, cb), F32)] * 2,
        compiler_params=_cparams(("parallel",)),
    )(x, w_re, w_im, lam_re, lam_im, h0_re, h0_im, e0_re, e0_im, *extra)


def _s5_dlam(mu_re, mu_im, h_re, h_im, h0_re, h0_im, *, reverse, name):
    rows, C = h_re.shape
    n = rows // N_SEG
    cb = _pick(C, 256, LANES)
    s_first = N_SEG - 1 if reverse else 0

    def body(mr_ref, mi_ref, hr_ref, hi_ref, h0r_ref, h0i_ref, dr_ref, di_ref):
        shape = (N_SEG, cb)
        row = lax.broadcasted_iota(jnp.int32, shape, 0)

        def rows_of(k):
            step = (n - 1 - k) if reverse else k
            return pl.ds(pl.multiple_of(step * N_SEG, N_SEG), N_SEG)

        def term(k, pr, pi):
            mr, mi = mr_ref[rows_of(k), :].astype(F32), mi_ref[rows_of(k), :].astype(F32)
            return mr * pr + mi * pi, mi * pr - mr * pi

        shift = N_SEG - 1 if reverse else 1
        pr = jnp.where(row == s_first, h0r_ref[...], pltpu.roll(hr_ref[rows_of(n - 1), :].astype(F32), shift, 0))
        pi = jnp.where(row == s_first, h0i_ref[...], pltpu.roll(hi_ref[rows_of(n - 1), :].astype(F32), shift, 0))
        acc = term(0, pr, pi)

        def loop(k, acc):
            tr, ti = term(k, hr_ref[rows_of(k - 1), :].astype(F32), hi_ref[rows_of(k - 1), :].astype(F32))
            return acc[0] + tr, acc[1] + ti

        ar, ai = lax.fori_loop(1, n, loop, acc)
        dr_ref[...] = jnp.sum(ar, axis=0, keepdims=True)
        di_ref[...] = jnp.sum(ai, axis=0, keepdims=True)

    big = pl.BlockSpec((rows, cb), lambda j: (0, j))
    vec = pl.BlockSpec((1, cb), lambda j: (0, j))
    return pl.pallas_call(
        body, name=name, grid=(C // cb,),
        in_specs=[big] * 4 + [vec] * 2, out_specs=[vec, vec],
        out_shape=[jax.ShapeDtypeStruct((1, C), F32)] * 2,
        compiler_params=_cparams(("parallel",)),
    )(mu_re, mu_im, h_re, h_im, h0_re, h0_im)


NT_DIMS = (((1,), (1,)), ((), ()))
TN_DIMS = (((0,), (0,)), ((), ()))


ATTN_Q_ROWS = 512


def _attn_exp(q, kvh, kr):
    s = (lax.dot_general(q[:, :LANES], kvh[:, :LANES], NT_DIMS, preferred_element_type=F32)
         + lax.dot_general(q[:, LANES:], kr, NT_DIMS, preferred_element_type=F32))
    e = jnp.exp2((s - jnp.max(s, axis=-1, keepdims=True)) * (ATTN_SCALE * math.log2(math.e)))
    return e, jnp.sum(e, axis=-1, keepdims=True)


def _attn_specs(L, T, tq):
    return [
        pl.BlockSpec((tq, 2 * LANES), lambda h, i: (i, h)),
        pl.BlockSpec((T, 2 * LANES), lambda h, i: (0, h)),
        pl.BlockSpec((T, LANES), lambda h, i: (0, 0)),
    ]


def _attn_fwd(qq, kv, kr, *, name):
    L, T = qq.shape[0], kv.shape[0]
    tq = _pick(L, ATTN_Q_ROWS // 2, 16)

    def body(q_ref, kv_ref, kr_ref, o_ref):
        kvh = kv_ref[...]
        e, l = _attn_exp(q_ref[...], kvh, kr_ref[...])
        o_ref[...] = (jnp.dot(e.astype(BF16), kvh[:, LANES:], preferred_element_type=F32) * (1.0 / l)).astype(o_ref.dtype)

    return pl.pallas_call(
        body, name=name, grid=(MLA_HEADS, L // tq), in_specs=_attn_specs(L, T, tq),
        out_specs=pl.BlockSpec((tq, LANES), lambda h, i: (i, h)),
        out_shape=jax.ShapeDtypeStruct((L, MLA_HEADS * V_DIM), BF16),
        compiler_params=_cparams(("parallel", "parallel")),
    )(qq, kv, kr)


def _attn_bwd(qq, kv, kr, do, *, name):
    L, T = qq.shape[0], kv.shape[0]
    H = MLA_HEADS
    tq = _pick(L, ATTN_Q_ROWS, 16)
    nq = L // tq

    def body(q_ref, kv_ref, kr_ref, do_ref, dq_ref, dkv_ref, dkr_ref, dkn_acc, dv_acc):
        h, i = pl.program_id(0), pl.program_id(1)
        q, kvh, krv, dov = q_ref[...], kv_ref[...], kr_ref[...], do_ref[...]
        e, l = _attn_exp(q, kvh, krv)
        inv = 1.0 / l
        ps = e * (inv * ATTN_SCALE)
        t = lax.dot_general(dov, kvh[:, LANES:], NT_DIMS, preferred_element_type=F32) * ps
        ds = (t - ps * (jnp.sum(t, axis=-1, keepdims=True) * (1.0 / ATTN_SCALE))).astype(BF16)
        dq_ref[:, :LANES] = jnp.dot(ds, kvh[:, :LANES], preferred_element_type=F32)
        dq_ref[:, LANES:] = jnp.dot(ds, krv, preferred_element_type=F32)

        @pl.when(i == 0)
        def _():
            dkn_acc[...] = jnp.zeros_like(dkn_acc)
            dv_acc[...] = jnp.zeros_like(dv_acc)

        @pl.when((i == 0) & (h == 0))
        def _():
            dkr_ref[...] = jnp.zeros_like(dkr_ref)

        dv_acc[...] += lax.dot_general(e.astype(BF16), (dov.astype(F32) * inv).astype(BF16), TN_DIMS,
                                       preferred_element_type=F32)
        dkn_acc[...] += lax.dot_general(ds, q[:, :LANES], TN_DIMS, preferred_element_type=F32)
        dkr_ref[...] += lax.dot_general(ds, q[:, LANES:], TN_DIMS, preferred_element_type=F32)

        @pl.when(i == nq - 1)
        def _():
            dkv_ref[:, :LANES] = dkn_acc[...].astype(dkv_ref.dtype)
            dkv_ref[:, LANES:] = dv_acc[...].astype(dkv_ref.dtype)

    in_specs = _attn_specs(L, T, tq) + [pl.BlockSpec((tq, LANES), lambda h, i: (i, h))]
    return pl.pallas_call(
        body, name=name, grid=(H, L // tq), in_specs=in_specs,
        out_specs=[pl.BlockSpec((tq, 2 * LANES), lambda h, i: (i, h)), pl.BlockSpec((T, 2 * LANES), lambda h, i: (0, h)),
                   pl.BlockSpec((T, LANES), lambda h, i: (0, 0))],
        out_shape=[jax.ShapeDtypeStruct((L, H * 2 * LANES), F32), jax.ShapeDtypeStruct((T, H * 2 * LANES), BF16),
                   jax.ShapeDtypeStruct((T, LANES), F32)],
        scratch_shapes=[pltpu.VMEM((T, LANES), F32), pltpu.VMEM((T, LANES), F32)],
        compiler_params=_cparams(("arbitrary", "arbitrary")),
    )(qq, kv, kr, do)


def _adamw(w, g, m, v, *, name, anchor=None):
    c1 = 1.0 - ADAM_B1 ** ADAM_STEP
    c2 = 1.0 - ADAM_B2 ** ADAM_STEP

    def f(w, g, m, v):
        m = ADAM_B1 * m + (1.0 - ADAM_B1) * g
        v = ADAM_B2 * v + (1.0 - ADAM_B2) * jnp.square(g)
        delta = -ADAM_LR * ((m / c1) / (jnp.sqrt(v / c2) + ADAM_EPS) + ADAM_WD * w)
        return g, delta, m, v

    return _rw(f, [w, g, m, v], [], [F32] * 4, name=name, anchor=anchor, tile_bytes=STREAM_TILE_BYTES)


def _slab_rows(rows, cols, n_arrays):
    return _pick(rows, max(16, (8 * 1024 * 1024) // (cols * 4 * n_arrays)), 16)


def _scalars(*vals):
    return jnp.stack([jnp.asarray(v, jnp.int32) for v in vals])


def _into_slot(src, slot, nslots, dtype, *, name):
    R, C = src.shape
    tr = _slab_rows(R, C, 2)

    def body(s_ref, x_ref, o_ref):
        o_ref[...] = x_ref[...].astype(o_ref.dtype)

    return pl.pallas_call(
        body, name=name,
        grid_spec=pltpu.PrefetchScalarGridSpec(
            num_scalar_prefetch=1, grid=(R // tr,),
            in_specs=[pl.BlockSpec((tr, C), lambda i, s: (i, 0))],
            out_specs=pl.BlockSpec((None, tr, C), lambda i, s: (s[0], i, 0))),
        out_shape=jax.ShapeDtypeStruct((nslots, R, C), dtype),
        compiler_params=_cparams(("arbitrary",)),
    )(_scalars(slot), src)


def _pair_sum(g, got, c, *, name):
    _, R, C = g.shape
    hr = R // 2
    tr = _slab_rows(hr, C, 3)
    nblk = hr // tr

    def body(s_ref, g_ref, r_ref, o_ref):
        o_ref[...] = (g_ref[...].astype(F32) + r_ref[...].astype(F32)).astype(o_ref.dtype)

    return pl.pallas_call(
        body, name=name,
        grid_spec=pltpu.PrefetchScalarGridSpec(
            num_scalar_prefetch=1, grid=(4, nblk),
            in_specs=[pl.BlockSpec((None, tr, C), lambda j, i, s: (j, s[0] * nblk + i, 0)),
                      pl.BlockSpec((None, tr, C), lambda j, i, s: (j, i, 0))],
            out_specs=pl.BlockSpec((None, tr, C), lambda j, i, s: (j, i, 0))),
        out_shape=jax.ShapeDtypeStruct((4, hr, C), g.dtype),
        compiler_params=_cparams(("arbitrary", "arbitrary")),
    )(_scalars(c), g, got)


def _chip_sum(p, landed, me_chip, c, *, name):
    _, hr, C = p.shape
    tr = _slab_rows(hr, C, 5)

    def body(s_ref, p_ref, l0_ref, l1_ref, l2_ref, o_ref):
        o_ref[...] = ((p_ref[...].astype(F32) + l0_ref[...].astype(F32)) + l1_ref[...].astype(F32)) + l2_ref[...].astype(F32)

    return pl.pallas_call(
        body, name=name,
        grid_spec=pltpu.PrefetchScalarGridSpec(
            num_scalar_prefetch=1, grid=(hr // tr,),
            in_specs=[pl.BlockSpec((None, tr, C), lambda i, s: (s[0], i, 0))]
            + [pl.BlockSpec((None, tr, C), functools.partial(lambda i, s, k: (k, i, 0), k=k)) for k in range(3)],
            out_specs=pl.BlockSpec((None, tr, C), lambda i, s: (s[1], i, 0))),
        out_shape=jax.ShapeDtypeStruct((2, hr, C), F32),
        compiler_params=_cparams(("arbitrary",)),
    )(_scalars(me_chip, c), p, landed, landed, landed)


def _place():
    return lax.axis_index("x"), lax.axis_index("y"), lax.axis_index("c")


def _other_chips(x, y):
    chips = [(1 - x, y), (x, 1 - y), (1 - x, 1 - y)]
    return chips, [2 * cx + cy for cx, cy in chips]


def _allgather8(v, *, name):
    rows, cols = v.shape

    def body(v_ref, out_ref, send_sems, recv_sems):
        x, y, c = _place()
        me = 4 * x + 2 * y + c
        out_ref[me] = v_ref[...]
        copies = []
        for k in range(1, 8):
            bx, by, bc = (k >> 2) & 1, (k >> 1) & 1, k & 1
            px, py, pc = x ^ bx, y ^ by, c ^ bc
            cp = pltpu.make_async_remote_copy(
                src_ref=v_ref, dst_ref=out_ref.at[me], send_sem=send_sems.at[k - 1], recv_sem=recv_sems.at[k - 1],
                device_id=(px, py, pc), device_id_type=MESH)
            cp.start()
            copies.append((cp, 4 * px + 2 * py + pc))
        for k, (cp, peer) in enumerate(copies):
            pltpu.make_async_remote_copy(
                src_ref=v_ref, dst_ref=out_ref.at[peer], send_sem=send_sems.at[k], recv_sem=recv_sems.at[k],
                device_id=(x, y, c), device_id_type=MESH).wait_recv()
        for cp, _ in copies:
            cp.wait_send()

    return pl.pallas_call(
        body, name=name, out_shape=jax.ShapeDtypeStruct((8, rows, cols), v.dtype),
        in_specs=[pl.BlockSpec(memory_space=pltpu.VMEM)], out_specs=pl.BlockSpec(memory_space=pltpu.VMEM),
        scratch_shapes=[pltpu.SemaphoreType.DMA((7,)), pltpu.SemaphoreType.DMA((7,))],
        compiler_params=pltpu.CompilerParams(vmem_limit_bytes=VMEM_LIMIT),
    )(v)


HBM_SPEC = pl.BlockSpec(memory_space=pltpu.HBM)
SEM_SPEC = pl.BlockSpec(memory_space=pltpu.SEMAPHORE)
EFFECT = pltpu.SideEffectType.DATAFLOW_SIDE_EFFECTING
TOKEN = jax.ShapeDtypeStruct((SUBLANES, LANES), F32)


def _in_hbm(a):
    return pltpu.with_memory_space_constraint(a, pltpu.HBM)


def _half_rows(buf, hf):
    hr = buf.shape[1] // 2
    return pl.ds(pl.multiple_of(hf * hr, 16), hr)


def _plan_ag_ici(refs):
    x, y, c = _place()
    chips, ids = _other_chips(x, y)
    out = []
    for r in refs:
        mine = r.at[2 * x + y, _half_rows(r, c), :]
        out += [(mine, mine, r.at[ids[j], _half_rows(r, c), :], (*chip, c)) for j, chip in enumerate(chips)]
    return out


def _plan_ag_pair(refs):
    x, y, c = _place()
    _, ids = _other_chips(x, y)
    out = []
    for r in refs:
        for j in range(3):
            piece = r.at[ids[j], _half_rows(r, c), :]
            out.append((piece, piece, r.at[ids[j], _half_rows(r, 1 - c), :], (x, y, 1 - c)))
    return out


def _plan_rs_ici(refs):
    x, y, c = _place()
    chips, ids = _other_chips(x, y)
    n = len(refs) // 2
    return [(refs[k].at[ids[j]], refs[n + k].at[j], refs[n + k].at[j], (*chip, c))
            for k in range(n) for j, chip in enumerate(chips)]


def _plan_pair_exchange(refs):
    x, y, c = _place()
    n = len(refs) // 2
    return [(refs[k].at[:, _half_rows(refs[k], 1 - c), :], refs[n + k], refs[n + k], (x, y, 1 - c)) for k in range(n)]


def _plan_pair_gather(refs):
    x, y, c = _place()
    return [(r.at[c], r.at[c], r.at[1 - c], (x, y, 1 - c)) for r in refs]


def _remote(src, dst, send_sem, recv_sem, target):
    return pltpu.make_async_remote_copy(src_ref=src, dst_ref=dst, send_sem=send_sem, recv_sem=recv_sem,
                                        device_id=target, device_id_type=MESH)


def _copy_start(groups, *, name, after=()):
    flat = [a for arrays, _, _ in groups for a in arrays]
    n, ng = len(flat), len(groups)
    after = list(after)
    n_in = n + len(after)

    def body(*refs):
        sems = refs[n_in:n_in + 2 * ng]
        thru = refs[n_in + 2 * ng:n_in + 2 * ng + n]
        token = refs[-1]
        pos = 0
        for g, (arrays, plan, n_copies) in enumerate(groups):
            copies = plan(thru[pos:pos + len(arrays)])
            pos += len(arrays)
            assert len(copies) == n_copies
            for i, (src, dst, _, target) in enumerate(copies):
                _remote(src, dst, sems[2 * g].at[i], sems[2 * g + 1].at[i], target).start()
        token[...] = jnp.zeros_like(token)

    out_shape = tuple(pltpu.SemaphoreType.DMA((n_copies,)) for _, _, n_copies in groups for _ in range(2))
    out_shape += tuple(pltpu.HBM(a.shape, a.dtype) for a in flat) + (TOKEN,)
    res = pl.pallas_call(
        body, name=name, out_shape=out_shape,
        in_specs=(HBM_SPEC,) * n + (pl.BlockSpec(memory_space=pl.ANY),) * len(after),
        out_specs=(SEM_SPEC,) * (2 * ng) + (HBM_SPEC,) * n + (pl.BlockSpec(memory_space=pltpu.VMEM),),
        input_output_aliases={k: 2 * ng + k for k in range(n)},
        compiler_params=pltpu.CompilerParams(has_side_effects=EFFECT),
    )(*[_in_hbm(a) for a in flat], *after)
    sems = [(res[2 * g], res[2 * g + 1]) for g in range(ng)]
    thru, pos = [], 2 * ng
    for arrays, _, _ in groups:
        thru.append(list(res[pos:pos + len(arrays)]))
        pos += len(arrays)
    return sems, thru, res[-1]


def _copy_wait(arrays, sems, plan, n_copies, after, *, name):
    n = len(arrays)
    after = list(after)

    def body(*refs):
        send, recv = refs[n], refs[n + 1]
        x, y, c = _place()
        copies = plan(refs[:n])
        assert len(copies) == n_copies
        for i, (src, dst, landing, target) in enumerate(copies):
            _remote(src, dst, send.at[i], recv.at[i], target).wait_send()
            _remote(landing, landing, send.at[i], recv.at[i], (x, y, c)).wait_recv()

    return list(pl.pallas_call(
        body, name=name, out_shape=tuple(pltpu.HBM(a.shape, a.dtype) for a in arrays),
        in_specs=(HBM_SPEC,) * n + (SEM_SPEC, SEM_SPEC) + (pl.BlockSpec(memory_space=pl.ANY),) * len(after),
        out_specs=(HBM_SPEC,) * n, input_output_aliases={k: k for k in range(n)},
        compiler_params=pltpu.CompilerParams(has_side_effects=EFFECT),
    )(*arrays, *sems, *after))


def _rs_stage1(gs, tag, after=()):
    n = len(gs)
    lands = [lax.empty((4, g.shape[1] // 2, g.shape[2]), g.dtype) for g in gs]
    sems, (arrays,), token = _copy_start([(list(gs) + lands, _plan_pair_exchange, n)], name=f"rs_pair_start_{tag}",
                                         after=after)
    return (sems[0], arrays), token


def _rs_stage2(handle, after, tag):
    sems, arrays = handle
    n = len(arrays) // 2
    arrays = _copy_wait(arrays, sems, _plan_pair_exchange, n, after, name=f"rs_pair_wait_{tag}")
    c = lax.axis_index("c")
    pair = [_pair_sum(g, r, c, name=f"rs_pair_sum_{tag}{k}") for k, (g, r) in enumerate(zip(arrays[:n], arrays[n:]))]
    lands = [lax.empty((3,) + p.shape[1:], p.dtype) for p in pair]
    sems, (arrays,), token = _copy_start([(pair + lands, _plan_rs_ici, 3 * n)], name=f"rs_start_{tag}")
    return (sems[0], arrays), token


def _rs_stage3(handle, after, tag):
    sems, arrays = handle
    n = len(arrays) // 2
    arrays = _copy_wait(arrays, sems, _plan_rs_ici, 3 * n, after, name=f"rs_wait_{tag}")
    x, y, c = _place()
    halves = [_chip_sum(p, l, 2 * x + y, c, name=f"rs_chip_sum_{tag}{k}") for k, (p, l) in enumerate(zip(arrays[:n], arrays[n:]))]
    sems, (halves,), token = _copy_start([(halves, _plan_pair_gather, n)], name=f"rs_gather_start_{tag}")
    return (sems[0], halves), token


def _rs_stage4(handle, after, tag):
    sems, halves = handle
    full = _copy_wait(halves, sems, _plan_pair_gather, len(halves), after, name=f"rs_gather_wait_{tag}")
    return [f.reshape(2 * f.shape[1], f.shape[2]) for f in full]


def _to_segments(a):
    rows = a.shape[0]
    return a.reshape(N_SEG, rows // N_SEG, -1).transpose(1, 0, 2).reshape(rows, -1)


def _from_segments(a):
    rows = a.shape[0]
    return a.reshape(rows // N_SEG, N_SEG, -1).transpose(1, 0, 2).reshape(rows, -1)


def _rope_tables(L):
    t = jnp.arange(L, dtype=jnp.int32)
    row = (t // GRID_W).astype(F32)
    col = (t % GRID_W).astype(F32)
    n_freq = QK_ROPE // 4
    inv = ROPE_BASE ** (-jnp.arange(n_freq, dtype=F32) / n_freq)
    a0, a1 = row[:, None] * inv, col[:, None] * inv
    z = jnp.zeros((L, LANES - QK_ROPE), F32)
    cos = jnp.concatenate([jnp.cos(a0), jnp.cos(a0), jnp.cos(a1), jnp.cos(a1), z], axis=1)
    sin = jnp.concatenate([-jnp.sin(a0), jnp.sin(a0), -jnp.sin(a1), jnp.sin(a1), z], axis=1)
    return _to_segments(cos), _to_segments(sin)


def _col_blocks(w, nblk):
    r, c = w.shape
    return w.reshape(r, nblk, c // nblk).transpose(1, 0, 2)


def _from_col_blocks(w4):
    nblk, r, c = w4.shape
    return w4.transpose(1, 0, 2).reshape(r, nblk * c)


def _s5_discretize(a_re, a_im, log_dt, b_re, b_im):
    dt = jnp.exp(log_dt)[:, None]
    mag = jnp.exp(a_re * dt)
    ab_re, ab_im = mag * jnp.cos(a_im * dt), mag * jnp.sin(a_im * dt)
    den = a_re * a_re + a_im * a_im
    nr, ni = ab_re - 1.0, ab_im
    co_re = (nr * a_re + ni * a_im) / den
    co_im = (ni * a_re - nr * a_im) / den
    bb_re = co_re[..., None] * b_re - co_im[..., None] * b_im
    bb_im = co_re[..., None] * b_im + co_im[..., None] * b_re
    return ab_re, ab_im, bb_re, bb_im


def _diag_blocks_in(bb, gpb):
    G, N, P = bb.shape
    t = jnp.tile(jnp.swapaxes(bb, 1, 2).reshape(G // gpb, gpb * P, N), (1, 1, gpb))
    row = lax.broadcasted_iota(jnp.int32, t.shape, 1) // P
    col = lax.broadcasted_iota(jnp.int32, t.shape, 2) // N
    return jnp.where(row == col, t, 0.0)


def _diag_blocks_out(cc, gpb):
    G, P, N = cc.shape
    t = jnp.tile(jnp.swapaxes(cc, 1, 2).reshape(G // gpb, gpb * N, P), (1, 1, gpb))
    row = lax.broadcasted_iota(jnp.int32, t.shape, 1) // N
    col = lax.broadcasted_iota(jnp.int32, t.shape, 2) // P
    return jnp.where(row == col, t, 0.0)


def _tr(ws):
    return [jnp.swapaxes(w, 1, 2) for w in ws]


WEIGHTS = ['c_ctx', 'w_mod', 'b_mod', 'norm1', 'norm2', 'w_in', 's5_a_re', 's5_a_im', 's5_log_dt', 's5_b_re', 's5_b_im',
           's5_c_re', 's5_c_im', 's5_d', 'w_glu', 'q_norm', 'kv_norm', 'w_uq', 'w_ukv', 'w_mla_o', 'w_out', 'w_ffn_in',
           'w_ffn_out', 'norm_f']
AG_GROUPS = [['w_in'], ['w_glu', 'w_uq', 'w_ukv', 'w_mla_o', 'w_out'], ['w_ffn_in', 'w_ffn_out']]
SMALL = ['norm1', 'norm2', 's5_a_re', 's5_a_im', 's5_log_dt', 's5_b_re', 's5_b_im', 's5_c_re', 's5_c_im', 's5_d',
         'q_norm', 'kv_norm', 'norm_f']


def _pad_rows(a, rows):
    return jnp.concatenate([a, jnp.zeros((rows - a.shape[0],) + a.shape[1:], a.dtype)], axis=0)


def _pack(vals, width, rows):
    flat = jnp.concatenate([v.reshape(-1).astype(F32) for v in vals])
    flat = jnp.concatenate([flat, jnp.zeros((rows * width - flat.shape[0],), F32)])
    return flat.reshape(rows, width)


def _unpack(buf, like):
    flat = buf.reshape(-1)
    out, pos = [], 0
    for v in like:
        out.append(flat[pos:pos + v.size].reshape(v.shape))
        pos += v.size
    return out


def _step(x, c, ctx, loss_target, w, m, v):
    px, py, pc = _place()
    me = 4 * px + 2 * py + pc
    me_chip = 2 * px + py
    L, D = x.shape[1], x.shape[2]
    Lc = ctx.shape[1]
    T = L + Lc
    SW = D // 2
    G = SW // S5_GROUP
    C = G * S5_STATE
    H = MLA_HEADS
    q_rank = w['q_norm'].shape[1]
    kv_rank = w['kv_norm'].shape[1]
    d_ff = w['w_ffn_out'].shape[1] * 4
    wa_used = SW + q_rank + kv_rank + QK_ROPE
    WA = -(-(SW + q_rank + kv_rank + LANES) // 512) * 512

    c_all = _allgather8(c.astype(F32).reshape(SUBLANES, D // SUBLANES), name="ag_cond").reshape(8, D)
    cond = jnp.concatenate([c_all, w['c_ctx'].reshape(1, D)], axis=0)
    cond = _pad_rows(cond, 16)
    (act,) = _rw(lambda t: (jax.nn.silu(t),), [cond], [], [F32], name="cond_silu")
    w_mod, cs_mod = w['w_mod'][0], w['w_mod'].shape[2]
    mod_part = _mm(act, w_mod, out_dtype=F32, name="mod_fwd")
    mod_all = _allgather8(mod_part, name="ag_mod")
    mod_full = jnp.concatenate([mod_all[0], mod_all[2], mod_all[4], mod_all[6]], axis=1) + w['b_mod']
    m_lat = lax.dynamic_slice_in_dim(mod_full, me, 1, axis=0).reshape(6, D)
    m_ctx = mod_full[8].reshape(6, D)
    sh1, sc1, g1, sh2, sc2, g2 = (m_lat[i:i + 1] for i in range(6))
    csh1, csc1 = m_ctx[0:1], m_ctx[1:2]

    ag_groups = [([_into_slot(w[nme][0], me_chip, 4, BF16, name=f"cast_{nme}") for nme in grp], _plan_ag_ici, 3 * len(grp))
                 for grp in AG_GROUPS]
    ag_sems, ag_bufs, ag_token = _copy_start(ag_groups, name="ag_start", after=[mod_full])
    gathered, ag_pair = {}, {}

    def landed(g, after):
        n_cp = 3 * len(AG_GROUPS[g])
        got = _copy_wait(ag_bufs[g], ag_sems[g], _plan_ag_ici, n_cp, after, name=f"ag_wait_{g}")
        sems, (got,), token = _copy_start([(got, _plan_ag_pair, n_cp)], name=f"ag_pair_start_{g}")
        ag_pair[g] = (sems[0], got)
        return token[0, 0]

    def arrive(g, after):
        sems, got = ag_pair[g]
        got = _copy_wait(got, sems, _plan_ag_pair, 3 * len(AG_GROUPS[g]), after, name=f"ag_pair_wait_{g}")
        gathered.update(zip(AG_GROUPS[g], got))

    xs = _to_segments(x[0])
    cs = _to_segments(ctx[0])
    tgt = _to_segments(loss_target[0])
    cos, sin = _rope_tables(L)
    n1, n2, nf = w['norm1'], w['norm2'], w['norm_f'].reshape(1, D)
    qg, kvg = w['q_norm'], w['kv_norm']

    (xn_lat,) = _rw(_f_norm_mod, [xs], [n1 + ag_token[0, 0], sc1, sh1], [BF16], name="norm1_lat")
    (xn_ctx,) = _rw(_f_norm_mod, [cs], [n1, csc1, csh1], [BF16], name="norm1_ctx")
    xn = jnp.concatenate([xn_lat, xn_ctx], axis=0)
    tok = landed(0, [xn])

    gpb = min(S5_BLOCK_GROUPS, G)
    gpo = min(8, G)
    d_skip = w['s5_d'][0].reshape(1, SW)
    disc, vjp_disc, w_b, w_c = [], [], [], []
    for d in range(2):
        prm = (w['s5_a_re'][0, d], w['s5_a_im'][0, d], w['s5_log_dt'][0, d] + tok, w['s5_b_re'][0, d], w['s5_b_im'][0, d])

        def prep(a_re, a_im, log_dt, b_re, b_im):
            ab_re, ab_im, bb_re, bb_im = _s5_discretize(a_re, a_im, log_dt, b_re, b_im)
            return ab_re.reshape(1, C), ab_im.reshape(1, C), _diag_blocks_in(bb_re, gpb), _diag_blocks_in(bb_im, gpb)

        out, vj = jax.vjp(prep, *prm)
        disc.append(out)
        vjp_disc.append(vj)
        w_b += [out[2], out[3]]
        w_c += [_diag_blocks_out(w['s5_c_re'][0, d], gpo), -_diag_blocks_out(w['s5_c_im'][0, d], gpo)]
    nb_in = G // gpb
    nb_out = G // gpo

    arrive(0, [xn, tgt] + w_b + w_c)
    w_in = _from_col_blocks(gathered['w_in'])
    w_a = jnp.concatenate([w_in[:, :wa_used], jnp.zeros((D, WA - wa_used), BF16)], axis=1)
    w_g = w_in[:, wa_used:]
    ha = _mm(xn, w_a, out_dtype=F32, name="in_proj")
    ha_lat, ha_ctx = ha[:L], ha[L:]
    gt = _mm(xn_lat, w_g, out_dtype=F32, name="in_gates")
    f_post_lat = _make_f_post_in(SW, q_rank, kv_rank, True)
    f_post_ctx = _make_f_post_in(SW, q_rank, kv_rank, False)
    u_lat, cqn, ckvn_lat, kr_lat = _rw(f_post_lat, [ha_lat, cos, sin], [qg, kvg], [F32, BF16, BF16, BF16], name="post_in_lat")
    u_ctx, ckvn_ctx, kr_ctx = _rw(f_post_ctx, [ha_ctx], [kvg], [F32, BF16, BF16], name="post_in_ctx")
    zero = jnp.zeros((1, C), F32) + landed(1, [u_lat, u_ctx])

    h_lat, h_ctx, hT_ctx, r5 = [], [], [], []
    for d, rev in enumerate((False, True)):
        lr, li = disc[d][0], disc[d][1]
        hcr, hci, tr, ti = _s5_scan(u_ctx, w_b[2 * d], w_b[2 * d + 1], lr, li, zero, zero, zero, zero, reverse=rev,
                                    name=f"s5_scan_ctx_{d}")
        hlr, hli, _, _, y = _s5_scan(u_lat, w_b[2 * d], w_b[2 * d + 1], lr, li, tr, ti, zero, zero, reverse=rev,
                                     name=f"s5_scan_lat_{d}", readout=(w_c[2 * d], w_c[2 * d + 1]))
        h_ctx += [hcr, hci]
        h_lat += [hlr, hli]
        hT_ctx += [tr, ti]
        r5.append(y)
    (z,) = _rw(_f_s5post, [u_lat] + r5, [d_skip], [BF16], name="s5_post")

    arrive(1, [z])
    w_glu, w_ukv, w_mla_o = (gathered[nme] for nme in ('w_glu', 'w_ukv', 'w_mla_o'))
    w_out = gathered['w_out'].reshape(D, D)
    uq3 = _from_col_blocks(gathered['w_uq']).reshape(q_rank, H, QK_NOPE + QK_ROPE)
    w_q2 = jnp.concatenate([uq3, jnp.zeros((q_rank, H, LANES - QK_ROPE), BF16)], axis=2).reshape(q_rank, H * 2 * LANES)
    q2 = _mm(cqn, w_q2, out_dtype=F32, name="q_up")
    (qq,) = _rw(_f_qpost, [q2, cos, sin], [], [BF16], name="q_rope")
    kvn = jnp.concatenate([ckvn_lat, ckvn_ctx], axis=0)
    kr_all = jnp.concatenate([kr_lat, kr_ctx], axis=0)
    kv = _mm(kvn, w_ukv, b_shards=4, out_dtype=BF16, name="kv_up")
    kr_all = kr_all + landed(2, [kv, qq]).astype(BF16)
    o = _attn_fwd(qq, kv, kr_all, name="attn_fwd")

    ab = _mm(z, w_glu, b_shards=4, out_dtype=F32, name="glu_proj")
    bm = _mm(o, w_mla_o, b_shards=4, out_dtype=F32, name="mla_out")
    (mix,) = _rw(_f_merge, [ab, bm, gt], [], [BF16], name="merge")
    out1 = _mm(mix, w_out, out_dtype=F32, name="out_proj")
    x1, xn2 = _rw(_f_resid_norm, [xs, out1], [g1, n2, sc2, sh2], [F32, BF16], name="resid_norm2")
    arrive(2, [xn2])
    w_ffn_in = gathered['w_ffn_in']
    w_ffn_out = gathered['w_ffn_out'].reshape(d_ff, D)
    hmid, ab2 = _ffn_in_swiglu(xn2, w_ffn_in, name="ffn_in")
    f2 = _mm(hmid, w_ffn_out, out_dtype=F32, name="ffn_out")
    (row_loss,) = _rw(_f_final, [x1, f2, tgt], [g2, nf], [F32], name="final_loss")
    loss = lax.psum(jnp.sum(row_loss), ("x", "y", "c"))

    ones = jnp.ones((L, 1), F32)
    (dx1_a, df2), (dg2, dnf) = _rw_vjp(_f_final, [x1, f2, tgt], [g2, nf], [[ones]], [True, True, False], [True, True],
                                       [F32, BF16], name="final_loss_bwd")
    gw_ffn_out = _mm(hmid, df2, ta=True, out_dtype=BF16, name="ffn_out_dw")
    dab2 = _ffn_out_dx_swiglu(df2, w_ffn_out, ab2, name="ffn_out_dx")
    dxn2 = _mm(dab2, w_ffn_in, tb=True, a_shards=2, b_shards=4, out_dtype=F32, name="ffn_in_dx")
    gw_ffn_in = _mm(xn2, dab2, ta=True, b_shards=2, out_shards=4, out_dtype=BF16, name="ffn_in_dw")
    rs_ffn, tok = _rs_stage1([gw_ffn_out.reshape(4, -1, D), gw_ffn_in], "ffn")
    (dx_a, dout1), (dg1, dn2, dsc2, dsh2) = _rw_vjp(
        _f_resid_norm, [xs, out1], [g1, n2 + tok[0, 0], sc2, sh2], [[dx1_a], [dxn2]], [True, True], [True] * 4, [F32, BF16],
        name="resid_norm2_bwd")
    dmix = _mm(dout1, w_out, tb=True, out_dtype=F32, name="out_proj_dx")
    rs_ffn, tok = _rs_stage2(rs_ffn, [dmix], "ffn")
    gw_out = _mm(mix, dout1, ta=True, out_dtype=BF16, name="out_proj_dw")
    (dab, dbm, dgt), _ = _rw_vjp(_f_merge, [ab, bm, gt], [], [[dmix]], [True] * 3, [], [BF16] * 3, name="merge_bwd",
                                 anchor=tok)
    dz = _mm(dab, w_glu, tb=True, b_shards=4, out_dtype=F32, name="glu_proj_dx")
    gw_glu = _mm(z, dab, ta=True, out_shards=4, out_dtype=BF16, name="glu_proj_dw")
    do = _mm(dbm, w_mla_o, tb=True, b_shards=4, out_dtype=BF16, name="mla_out_dx")
    gw_mla_o = _mm(o, dbm, ta=True, out_shards=4, out_dtype=BF16, name="mla_out_dw")
    dxn_g = _mm(dgt, w_g, tb=True, out_dtype=F32, name="in_gates_dx")
    gw_g = _mm(xn_lat, dgt, ta=True, out_dtype=BF16, name="in_gates_dw")
    rs_mid, tok = _rs_stage1([gw_out.reshape(4, -1, D), gw_glu, gw_mla_o], "mid", after=[gw_g])

    (du_a, dr5), (dd_skip,) = _rw_vjp(_f_s5post, [u_lat] + r5, [d_skip + tok[0, 0]], [[dz]], [True, True, False], [True],
                                      [F32, F32], name="s5_post_bwd")
    dw_c = _bd_dw(h_lat, [dr5] * 4, nb_out, name="s5_readout_dw")
    rs_mid, tok = _rs_stage2(rs_mid, dw_c[:1], "mid")
    zero = zero + tok[0, 0]
    w_ct = _tr(w_c)
    zeros_ctx = jnp.zeros((Lc, SW), BF16)
    mu_lat, mu_ctx, dlam = [], [], []
    for d, rev in enumerate((False, True)):
        lr, li = disc[d][0], disc[d][1]
        mlr, mli, fr, fi = _s5_scan(dr5, w_ct[2 * d], w_ct[2 * d + 1], lr, -li, zero, zero, zero, zero, reverse=not rev,
                                    name=f"s5_adj_lat_{d}")
        dh0r, dh0i = _cmul(lr, -li, fr, fi)
        mcr, mci, _, _ = _s5_scan(zeros_ctx, w_ct[2 * d], w_ct[2 * d + 1], lr, -li, zero, zero, dh0r, dh0i,
                                  reverse=not rev, name=f"s5_adj_ctx_{d}")
        dl_lat = _s5_dlam(mlr, mli, h_lat[2 * d], h_lat[2 * d + 1], hT_ctx[2 * d], hT_ctx[2 * d + 1], reverse=rev,
                          name=f"s5_dlam_lat_{d}")
        dl_ctx = _s5_dlam(mcr, mci, h_ctx[2 * d], h_ctx[2 * d + 1], zero, zero, reverse=rev, name=f"s5_dlam_ctx_{d}")
        mu_lat += [mlr, mli]
        mu_ctx += [mcr, mci]
        dlam.append((dl_lat[0] + dl_ctx[0], dl_lat[1] + dl_ctx[1]))
    du_b = _bd_fanin(mu_lat, _tr(w_b), name="s5_bu_lat_dx")
    du_ctx = _bd_fanin(mu_ctx, _tr(w_b), name="s5_bu_ctx_dx")
    dw_b_lat = _bd_dw([u_lat] * 4, mu_lat, nb_in, name="s5_bu_lat_dw")
    dw_b_ctx = _bd_dw([u_ctx] * 4, mu_ctx, nb_in, name="s5_bu_ctx_dw")
    g_s5 = {}
    for d in range(2):
        ct = (dlam[d][0], dlam[d][1], dw_b_lat[2 * d] + dw_b_ctx[2 * d], dw_b_lat[2 * d + 1] + dw_b_ctx[2 * d + 1])
        ga_re, ga_im, gdt, gb_re, gb_im = vjp_disc[d](ct)
        _, vj_c = jax.vjp(lambda cr, ci: (_diag_blocks_out(cr, gpo), -_diag_blocks_out(ci, gpo)),
                          w['s5_c_re'][0, d], w['s5_c_im'][0, d])
        gc_re, gc_im = vj_c((dw_c[2 * d], dw_c[2 * d + 1]))
        for nme, val in (('s5_a_re', ga_re), ('s5_a_im', ga_im), ('s5_log_dt', gdt), ('s5_b_re', gb_re),
                         ('s5_b_im', gb_im), ('s5_c_re', gc_re), ('s5_c_im', gc_im)):
            g_s5.setdefault(nme, []).append(val)
    g_small = {nme: jnp.stack(vals)[None] for nme, vals in g_s5.items()}
    g_small['s5_d'] = dd_skip.reshape(w['s5_d'].shape)

    dqq, dkv, dkr = _attn_bwd(qq, kv, kr_all, do, name="attn_bwd")
    (dq2,), _ = _rw_vjp(_f_qpost, [q2, cos, sin], [], [[dqq]], [True, False, False], [], [BF16], name="q_rope_bwd")
    dcqn = _mm(dq2, w_q2, tb=True, out_dtype=F32, name="q_up_dx")
    gw_q2 = _mm(cqn, dq2, ta=True, out_dtype=BF16, name="q_up_dw")
    dckvn = _mm(dkv, w_ukv, tb=True, b_shards=4, out_dtype=F32, name="kv_up_dx")
    gw_ukv = _mm(kvn, dkv, ta=True, out_shards=4, out_dtype=BF16, name="kv_up_dw")
    gw_uq = gw_q2.reshape(q_rank, H, 2 * LANES)[:, :, :QK_NOPE + QK_ROPE].reshape(q_rank, H * (QK_NOPE + QK_ROPE))
    rs_kv, tok = _rs_stage1([_col_blocks(gw_uq, 4), gw_ukv], "kv")

    (dha_lat,), (dqg, dkvg_lat) = _rw_vjp(
        f_post_lat, [ha_lat, cos, sin], [qg, kvg + tok[0, 0]], [[du_a, du_b], [dcqn], [dckvn[:L]], [dkr[:L]]],
        [True, False, False], [True, True], [BF16], name="post_in_lat_bwd")
    (dha_ctx,), (dkvg_ctx,) = _rw_vjp(f_post_ctx, [ha_ctx], [kvg], [[du_ctx], [dckvn[L:]], [dkr[L:]]], [True], [True],
                                      [BF16], name="post_in_ctx_bwd")
    dha = jnp.concatenate([dha_lat, dha_ctx], axis=0)
    dxn = _mm(dha, w_a, tb=True, out_dtype=F32, name="in_proj_dx")
    gw_a = _mm(xn, dha, ta=True, out_dtype=BF16, name="in_proj_dw")
    rs_kv, tok = _rs_stage2(rs_kv, [gw_a], "kv")
    (dx_seg,), (dn1_lat, dsc1, dsh1) = _rw_vjp(
        _f_norm_mod_keep, [xs], [n1 + tok[0, 0], sc1, sh1], [[dxn[:L], dxn_g], [dx_a]], [True], [True] * 3, [F32],
        name="norm1_lat_bwd")
    _, (dn1_ctx, dcsc1, dcsh1) = _rw_vjp(_f_norm_mod, [cs], [n1, csc1, csh1], [[dxn[L:]]], [False], [True] * 3, [],
                                         name="norm1_ctx_bwd")
    grad_x = _from_segments(dx_seg)[None]
    g_small.update(norm1=dn1_lat + dn1_ctx, norm2=dn2, q_norm=dqg, kv_norm=dkvg_lat + dkvg_ctx, norm_f=dnf.reshape(D))
    gw_in = jnp.concatenate([gw_a[:, :wa_used], gw_g], axis=1)
    small_vals = [g_small[nme] for nme in SMALL]
    n_small = sum(val.size for val in small_vals)
    small_rows = -(-n_small // (LANES * 4 * 32)) * 32

    zD = jnp.zeros((1, D), F32)
    dm = jnp.concatenate([
        jnp.concatenate([dsh1, dsc1, dg1, dsh2, dsc2, dg2], axis=1),
        jnp.concatenate([dcsh1, dcsc1, zD, zD, zD, zD], axis=1),
    ], axis=0)
    dm_all = _allgather8(dm.reshape(SUBLANES, -1), name="ag_dmod").reshape(8, 2, 6 * D)
    rs_in, tok = _rs_stage1([_col_blocks(gw_in, 4), _pack(small_vals, LANES, 4 * small_rows).reshape(4, small_rows, LANES)],
                            "in", after=[dm_all])
    dm_ctx = dm_all[0, 1] + tok[0, 0]
    for k in range(1, 8):
        dm_ctx = dm_ctx + dm_all[k, 1]
    dmod = _pad_rows(jnp.concatenate([dm_all[:, 0, :], dm_ctx[None]], axis=0), 16)
    g_b_mod = jnp.sum(dmod, axis=0, keepdims=True)
    dmod_mine = lax.dynamic_slice_in_dim(dmod, me_chip * cs_mod, cs_mod, axis=1)
    g_w_mod = _mm(act, dmod_mine, ta=True, out_dtype=F32, name="mod_dw")
    dact_part = _mm(dmod_mine, w_mod, tb=True, out_dtype=F32, name="mod_dx")
    dact_all = _allgather8(dact_part[8].reshape(SUBLANES, D // SUBLANES), name="ag_dact").reshape(8, D)
    dact = jnp.zeros((16, D), F32).at[8].set(dact_all[0] + dact_all[2] + dact_all[4] + dact_all[6])
    (dcond_rows,), _ = _rw_vjp(lambda t: (jax.nn.silu(t),), [cond], [], [[dact]], [True], [], [F32], name="cond_silu_bwd")
    g_c_ctx = dcond_rows[8]

    rs_in, tok = _rs_stage2(rs_in, [g_c_ctx], "in")

    grads, delta, new_m, new_v = {}, {}, {}, {}

    def update(members, reds, anchor):
        deltas = []
        for nme, red in zip(members, reds):
            res = _adamw(w[nme][0], red, m[nme][0], v[nme][0], name=f"adamw_{nme}", anchor=anchor)
            grads[nme], delta[nme], new_m[nme], new_v[nme] = (r.reshape(w[nme].shape) for r in res)
            deltas.append(res[1])
            anchor = None
        return deltas

    rs_ffn, tok = _rs_stage3(rs_ffn, [tok], "ffn")
    done = update(['w_mod'], [g_w_mod], tok)
    red_ffn = _rs_stage4(rs_ffn, done, "ffn")
    rs_mid, tok = _rs_stage3(rs_mid, red_ffn[:1], "mid")
    done = update(['w_ffn_out', 'w_ffn_in'], red_ffn, tok)
    red_mid = _rs_stage4(rs_mid, done, "mid")
    rs_kv, tok = _rs_stage3(rs_kv, red_mid[:1], "kv")
    done = update(['w_out', 'w_glu', 'w_mla_o'], red_mid, tok)
    red_kv = _rs_stage4(rs_kv, done, "kv")
    rs_in, tok = _rs_stage3(rs_in, red_kv[:1], "in")
    done = update(['w_uq', 'w_ukv'], red_kv, tok)
    red_in = _rs_stage4(rs_in, done, "in")
    small_buf = _into_slot(red_in[-1], me_chip, 4, F32, name="small_grads_slot")
    sems, (bufs,), tok = _copy_start([([small_buf], _plan_ag_ici, 3)], name="ag_small_start")
    done = update(['w_in'], red_in[:1], tok)
    bufs = _copy_wait(bufs, sems[0], _plan_ag_ici, 3, done, name="ag_small_wait")
    sems, (bufs,), _ = _copy_start([(bufs, _plan_ag_pair, 3)], name="ag_small_pair_start")
    bufs = _copy_wait(bufs, sems[0], _plan_ag_pair, 3, [], name="ag_small_pair_wait")
    small_all = bufs[0].reshape(4 * small_rows, LANES)
    g_small_red = dict(zip(SMALL, _unpack(small_all, [w[nme] for nme in SMALL])))
    rest = SMALL + ['c_ctx', 'b_mod']
    g_rest = dict(g_small_red, c_ctx=g_c_ctx, b_mod=g_b_mod)
    rows_rest = -(-sum(w[nme].size for nme in rest) // (LANES * 16)) * 16
    packed = [_pack([src[nme] for nme in rest], LANES, rows_rest) for src in (w, g_rest, m, v)]
    res = _adamw(*packed, name="adamw_small")
    for dst, buf in zip((grads, delta, new_m, new_v), res):
        dst.update(zip(rest, _unpack(buf, [w[nme] for nme in rest])))
    return (loss, grad_x, *[grads[nme] for nme in WEIGHTS], *[delta[nme] for nme in WEIGHTS],
            *[new_m[nme] for nme in WEIGHTS], *[new_v[nme] for nme in WEIGHTS])


def kernel(x, c, ctx, c_ctx, w_mod, b_mod, norm1, norm2, w_in, s5_a_re, s5_a_im, s5_log_dt, s5_b_re, s5_b_im, s5_c_re, s5_c_im, s5_d, w_glu, q_norm, kv_norm, w_uq, w_ukv, w_mla_o, w_out, w_ffn_in, w_ffn_out, norm_f, loss_target, m_c_ctx, m_w_mod, m_b_mod, m_norm1, m_norm2, m_w_in, m_s5_a_re, m_s5_a_im, m_s5_log_dt, m_s5_b_re, m_s5_b_im, m_s5_c_re, m_s5_c_im, m_s5_d, m_w_glu, m_q_norm, m_kv_norm, m_w_uq, m_w_ukv, m_w_mla_o, m_w_out, m_w_ffn_in, m_w_ffn_out, m_norm_f, v_c_ctx, v_w_mod, v_b_mod, v_norm1, v_norm2, v_w_in, v_s5_a_re, v_s5_a_im, v_s5_log_dt, v_s5_b_re, v_s5_b_im, v_s5_c_re, v_s5_c_im, v_s5_d, v_w_glu, v_q_norm, v_kv_norm, v_w_uq, v_w_ukv, v_w_mla_o, v_w_out, v_w_ffn_in, v_w_ffn_out, v_norm_f):
    w = dict(c_ctx=c_ctx, w_mod=w_mod, b_mod=b_mod, norm1=norm1, norm2=norm2, w_in=w_in, s5_a_re=s5_a_re, s5_a_im=s5_a_im,
             s5_log_dt=s5_log_dt, s5_b_re=s5_b_re, s5_b_im=s5_b_im, s5_c_re=s5_c_re, s5_c_im=s5_c_im, s5_d=s5_d, w_glu=w_glu,
             q_norm=q_norm, kv_norm=kv_norm, w_uq=w_uq, w_ukv=w_ukv, w_mla_o=w_mla_o, w_out=w_out, w_ffn_in=w_ffn_in,
             w_ffn_out=w_ffn_out, norm_f=norm_f)
    m = dict(c_ctx=m_c_ctx, w_mod=m_w_mod, b_mod=m_b_mod, norm1=m_norm1, norm2=m_norm2, w_in=m_w_in, s5_a_re=m_s5_a_re,
             s5_a_im=m_s5_a_im, s5_log_dt=m_s5_log_dt, s5_b_re=m_s5_b_re, s5_b_im=m_s5_b_im, s5_c_re=m_s5_c_re,
             s5_c_im=m_s5_c_im, s5_d=m_s5_d, w_glu=m_w_glu, q_norm=m_q_norm, kv_norm=m_kv_norm, w_uq=m_w_uq, w_ukv=m_w_ukv,
             w_mla_o=m_w_mla_o, w_out=m_w_out, w_ffn_in=m_w_ffn_in, w_ffn_out=m_w_ffn_out, norm_f=m_norm_f)
    v = dict(c_ctx=v_c_ctx, w_mod=v_w_mod, b_mod=v_b_mod, norm1=v_norm1, norm2=v_norm2, w_in=v_w_in, s5_a_re=v_s5_a_re,
             s5_a_im=v_s5_a_im, s5_log_dt=v_s5_log_dt, s5_b_re=v_s5_b_re, s5_b_im=v_s5_b_im, s5_c_re=v_s5_c_re,
             s5_c_im=v_s5_c_im, s5_d=v_s5_d, w_glu=v_w_glu, q_norm=v_q_norm, kv_norm=v_kv_norm, w_uq=v_w_uq, w_ukv=v_w_ukv,
             w_mla_o=v_w_mla_o, w_out=v_w_out, w_ffn_in=v_w_ffn_in, w_ffn_out=v_w_ffn_out, norm_f=v_norm_f)
    return _step(x, c, ctx, loss_target, w, m, v)
```
